```python
import math
import jax, jax.numpy as jnp
from jax import lax
import numpy as np

D_MODEL = 1024
BATCH = 2
SEQ = 8192
DEPTH = 2

CTX_LEN = 256
GRID_W = 64
HEAD_DIM = 64
ROPE_BASE = 10000.0
EPS = 1e-6
CHUNK = 128
D_RET = D_MODEL // 2
D_MLSTM = D_MODEL // 2
H_RET = D_RET // HEAD_DIM
H_MLSTM = D_MLSTM // HEAD_DIM
CONV_W = 3
AB_IN = 4 * D_RET + 4 * D_MLSTM + 4 * H_MLSTM
AB_SPLITS = tuple(int(s) for s in np.cumsum([D_RET] * 4 + [D_MLSTM] * 4))
H_ATTN = D_MODEL // HEAD_DIM
H_KV = 4
GQA_G = H_ATTN // H_KV
WINDOW = 128
BLK = 128
ATTN_IN = D_MODEL + 2 * H_KV * HEAD_DIM
D_FF = ((8 * D_MODEL // 3 + 255) // 256) * 256

kernel_name = 'hybrid_retention_mlstm_window_gqa_dit'


def _rmsnorm(x, w):
    xf = x.astype(jnp.float32)
    y = xf * lax.rsqrt(jnp.mean(xf * xf, axis=-1, keepdims=True) + EPS)
    return (y * w.astype(jnp.float32)).astype(x.dtype)


def _heads(x, h):
    return x.reshape(x.shape[0], x.shape[1], h, -1)


def _head_major(a):
    return jnp.transpose(a, (0, 2, 1, 3)).astype(jnp.float32)


def _axial_rope(L, dtype):
    rows = L // GRID_W
    row = jnp.repeat(jnp.arange(rows, dtype=jnp.float32), GRID_W)
    col = jnp.tile(jnp.arange(GRID_W, dtype=jnp.float32), rows)
    n = HEAD_DIM // 4
    inv = ROPE_BASE ** (-jnp.arange(n, dtype=jnp.float32) / n)
    ang = jnp.concatenate([row[:, None] * inv, col[:, None] * inv], axis=-1)
    return jnp.cos(ang).astype(dtype), jnp.sin(ang).astype(dtype)


def _apply_rope(x, cos, sin):
    half = x.shape[-1] // 2
    x1, x2 = x[..., :half], x[..., half:]
    c = cos[None, :, None, :]
    s = sin[None, :, None, :]
    return jnp.concatenate([x1 * c - x2 * s, x2 * c + x1 * s], axis=-1)


def _short_conv(x, w, b):
    L = x.shape[1]
    p = CONV_W // 2
    xp = jnp.pad(x, ((0, 0), (p, p), (0, 0)))
    y = b
    for j in range(CONV_W):
        y = y + w[j] * xp[:, j:j + L]
    return y


def _swiglu(h, w_in, w_out):
    g, u = jnp.split(h @ w_in, 2, axis=-1)
    return (jax.nn.silu(g) * u) @ w_out


def _retention_chunked(q, k, v, log_g, s0):
    B, H, L, dk = q.shape
    n = L // CHUNK
    k = k * dk ** -0.5
    qc = q.reshape(B, H, n, CHUNK, dk)
    kc = k.reshape(B, H, n, CHUNK, dk)
    vc = v.reshape(B, H, n, CHUNK, -1)
    pos = jnp.arange(CHUNK, dtype=jnp.float32)
    diff = pos[:, None] - pos[None, :]
    decay = jnp.where(diff >= 0, jnp.exp(log_g[:, None, None] * jnp.maximum(diff, 0.0)), 0.0)
    scores = jnp.einsum('bhnid,bhnjd->bhnij', qc, kc) * decay[None, :, None]
    intra = jnp.einsum('bhnij,bhnje->bhnie', scores, vc)
    k_dec = jnp.exp(log_g[:, None] * (CHUNK - 1.0 - pos))
    kv_local = jnp.einsum('bhnjd,hj,bhnje->nbhde', kc, k_dec, vc)
    chunk_decay = jnp.exp(log_g * CHUNK)[None, :, None, None]

    def step(s, kv):
        return chunk_decay * s + kv, s

    s_fin, s_prev = lax.scan(step, s0, kv_local)
    q_dec = jnp.exp(log_g[:, None] * (pos + 1.0))
    inter = jnp.einsum('bhnid,hi,nbhde->bhnie', qc, q_dec, s_prev)
    return (intra + inter).reshape(B, H, L, -1), s_fin


def _mlstm_chunked(q, k, v, i_pre, f_pre, state):
    B, H, L, d = q.shape
    n = L // CHUNK
    k = k * d ** -0.5
    qc = q.reshape(B, H, n, CHUNK, d)
    kc = k.reshape(B, H, n, CHUNK, d)
    vc = v.reshape(B, H, n, CHUNK, d)
    ic = i_pre.reshape(B, H, n, CHUNK)
    b = jnp.cumsum(jax.nn.log_sigmoid(f_pre).reshape(B, H, n, CHUNK), axis=-1)
    b_end = b[..., -1]
    causal = jnp.tril(jnp.ones((CHUNK, CHUNK), dtype=bool))
    d_log = jnp.where(causal, b[..., :, None] - b[..., None, :] + ic[..., None, :], -jnp.inf)
    g_end = b_end[..., None] - b + ic
    g_max = jnp.max(g_end, axis=-1)
    w_end = jnp.exp(g_end - g_max[..., None])
    kv_local = jnp.einsum('bhns,bhnsd,bhnse->nbhde', w_end, kc, vc)
    n_local = jnp.einsum('bhns,bhnsd->nbhd', w_end, kc)

    def step(carry, xs):
        c_mat, n_vec, m = carry
        kv, nl, gm, be = xs
        m_new = jnp.maximum(be + m, gm)
        a = jnp.exp(be + m - m_new)
        bb = jnp.exp(gm - m_new)
        c_new = a[..., None, None] * c_mat + bb[..., None, None] * kv
        n_new = a[..., None] * n_vec + bb[..., None] * nl
        return (c_new, n_new, m_new), (c_mat, n_vec, m)

    xs = (kv_local, n_local, jnp.moveaxis(g_max, 2, 0), jnp.moveaxis(b_end, 2, 0))
    final, (c_prev, n_prev, m_prev) = lax.scan(step, state, xs)
    m_prev = jnp.moveaxis(m_prev, 0, 2)
    a_log = b + m_prev[..., None]
    m_t = jnp.maximum(a_log, jnp.max(d_log, axis=-1))
    w = jnp.exp(d_log - m_t[..., None])
    a = jnp.exp(a_log - m_t)
    s = jnp.einsum('bhntd,bhnsd->bhnts', qc, kc) * w
    num = jnp.einsum('bhnts,bhnse->bhnte', s, vc) + a[..., None] * jnp.einsum('bhntd,nbhde->bhnte', qc, c_prev)
    den = jnp.sum(s, axis=-1) + a * jnp.einsum('bhntd,nbhd->bhnt', qc, n_prev)
    h = num / jnp.maximum(jnp.abs(den), jnp.exp(-m_t))[..., None]
    return h.reshape(B, H, L, d), final


def _sink_softmax(scores, sink):
    snk = sink[:, :, None]
    m = snk
    for s in scores:
        m = jnp.maximum(m, jnp.max(s, axis=-1))
    ex = [jnp.exp(s - m[..., None]) for s in scores]
    denom = jnp.exp(snk - m)
    for e in ex:
        denom = denom + jnp.sum(e, axis=-1)
    return [e / denom[..., None] for e in ex]


def _ret_mlstm_mixer(h, hc, rope, w_in, w_out, ret_log_gamma, ret_norm_w,
                     conv_w, conv_b, gate_b, mlstm_norm_w):
    f32 = jnp.float32
    B = h.shape[0]

    def prep(t, use_rope):
        rq, rk, rv, rg, mq, mk, mv, mo, mg = jnp.split(t @ w_in, AB_SPLITS, axis=-1)
        rq, rk, rv = _heads(rq, H_RET), _heads(rk, H_RET), _heads(rv, H_RET)
        if use_rope:
            rq, rk = rope(rq), rope(rk)
        mqk = jax.nn.silu(_short_conv(jnp.concatenate([mq, mk], axis=-1), conv_w, conv_b))
        mq, mk = jnp.split(mqk, 2, axis=-1)
        gates = (mg.astype(f32).reshape(t.shape[0], t.shape[1], 4, H_MLSTM)
                 + gate_b.astype(f32)).transpose(2, 0, 3, 1)
        ret = tuple(_head_major(a) for a in (rq, rk, rv))
        mls = tuple(_head_major(_heads(a, H_MLSTM)) for a in (mq, mk, mv))
        return ret, mls, gates, rg, mo

    ret_l, mls_l, gates_l, rg_l, mo_l = prep(h, True)
    ret_c, mls_c, gates_c, rg_c, mo_c = prep(hc, False)
    flip = lambda t: jnp.flip(t, axis=2)
    lg = ret_log_gamma.astype(f32)
    s0_r = jnp.zeros((B, H_RET, HEAD_DIM, HEAD_DIM), f32)
    s0_m = (jnp.zeros((B, H_MLSTM, HEAD_DIM, HEAD_DIM), f32),
            jnp.zeros((B, H_MLSTM, HEAD_DIM), f32),
            jnp.zeros((B, H_MLSTM), f32))

    def ret_dir(direction, ctx_args, lat_args):
        yc, s = _retention_chunked(*ctx_args, lg[direction], s0_r)
        yl, _ = _retention_chunked(*lat_args, lg[direction], s)
        return yc, yl

    def mls_dir(ctx_args, lat_args):
        yc, s = _mlstm_chunked(*ctx_args, s0_m)
        yl, _ = _mlstm_chunked(*lat_args, s)
        return yc, yl

    rc_f, rl_f = ret_dir(0, ret_c, ret_l)
    rc_b, rl_b = ret_dir(1, tuple(map(flip, ret_c)), tuple(map(flip, ret_l)))
    mc_f, ml_f = mls_dir((*mls_c, gates_c[0], gates_c[1]), (*mls_l, gates_l[0], gates_l[1]))
    mc_b, ml_b = mls_dir(tuple(map(flip, (*mls_c, gates_c[2], gates_c[3]))),
                         tuple(map(flip, (*mls_l, gates_l[2], gates_l[3]))))

    def merge(r, m, rg, mo, dtype):
        r = _rmsnorm(jnp.transpose(r, (0, 2, 1, 3)), ret_norm_w.reshape(H_RET, HEAD_DIM)).astype(dtype)
        r = r.reshape(r.shape[0], r.shape[1], D_RET) * jax.nn.silu(rg)
        m = _rmsnorm(jnp.transpose(m, (0, 2, 1, 3)), mlstm_norm_w.reshape(H_MLSTM, HEAD_DIM)).astype(dtype)
        m = m.reshape(m.shape[0], m.shape[1], D_MLSTM) * jax.nn.sigmoid(mo)
        return jnp.concatenate([r, m], axis=-1) @ w_out

    y_lat = merge(rl_f + flip(rl_b), ml_f + flip(ml_b), rg_l, mo_l, h.dtype)
    y_ctx = merge(rc_f + flip(rc_b), mc_f + flip(mc_b), rg_c, mo_c, hc.dtype)
    return y_lat, y_ctx


def _window_attn_mixer(h, hc, rope, w_in, w_out, q_norm_w, k_norm_w, sink, need_ctx_out):
    f32 = jnp.float32
    B, L, _ = h.shape
    nb = L // BLK
    scale = HEAD_DIM ** -0.5

    def prep(t):
        q, k, v = jnp.split(t @ w_in, [D_MODEL, D_MODEL + H_KV * HEAD_DIM], axis=-1)
        return (_rmsnorm(_heads(q, H_ATTN), q_norm_w), _rmsnorm(_heads(k, H_KV), k_norm_w), _heads(v, H_KV))

    q, k, v = prep(h)
    q, k = rope(q), rope(k)
    qc, kc, vc = prep(hc)
    snk = sink.astype(f32).reshape(H_KV, GQA_G)

    qb = q.reshape(B, nb, BLK, H_KV, GQA_G, HEAD_DIM)

    def band(t):
        tp = jnp.pad(t, ((0, 0), (BLK, BLK), (0, 0), (0, 0))).reshape(B, nb + 2, BLK, H_KV, HEAD_DIM)
        return jnp.concatenate([tp[:, :-2], tp[:, 1:-1], tp[:, 2:]], axis=2)

    kw, vw = band(k), band(v)
    s_win = jnp.einsum('bnqhgd,bnkhd->bnhgqk', qb, kw).astype(f32) * scale
    blk = jnp.arange(nb)[:, None, None]
    qpos = blk * BLK + jnp.arange(BLK)[None, :, None]
    kpos = (blk - 1) * BLK + jnp.arange(3 * BLK)[None, None, :]
    valid = (jnp.abs(qpos - kpos) <= WINDOW) & (kpos >= 0) & (kpos < L)
    s_win = jnp.where(valid[None, :, None, None], s_win, -jnp.inf)
    s_ctx = jnp.einsum('bnqhgd,bkhd->bnhgqk', qb, kc).astype(f32) * scale
    p_win, p_ctx = _sink_softmax([s_win, s_ctx], snk)
    o = (jnp.einsum('bnhgqk,bnkhd->bnqhgd', p_win.astype(v.dtype), vw)
         + jnp.einsum('bnhgqk,bkhd->bnqhgd', p_ctx.astype(vc.dtype), vc))
    y = o.reshape(B, L, D_MODEL) @ w_out
    if not need_ctx_out:
        return y, None
    Lc = hc.shape[1]
    qcg = qc.reshape(B, Lc, H_KV, GQA_G, HEAD_DIM)
    s_cc = jnp.einsum('bqhgd,bkhd->bhgqk', qcg, kc).astype(f32) * scale
    (p_cc,) = _sink_softmax([s_cc], snk)
    oc = jnp.einsum('bhgqk,bkhd->bqhgd', p_cc.astype(vc.dtype), vc).reshape(B, Lc, D_MODEL) @ w_out
    return y, oc


def setup_inputs(seed: int = 0) -> dict:
    key = jax.random.key(seed)
    ks = jax.random.split(key, 24)
    f32 = jnp.float32
    n_even = (DEPTH + 1) // 2
    n_odd = DEPTH // 2
    D = D_MODEL

    def nrm(k, shape, scale):
        return jax.random.normal(k, shape, f32) * scale

    base_lg = jnp.log1p(-jnp.exp2(-5.0 - jnp.arange(H_RET, dtype=f32)))
    fb = jnp.linspace(3.0, 6.0, H_MLSTM, dtype=f32)
    zb = jnp.zeros((H_MLSTM,), f32)
    gate_base = jnp.stack([zb, fb, zb, fb])
    return {
        'x': nrm(ks[0], (BATCH, SEQ, D), 1.0),
        'c': nrm(ks[1], (BATCH, D), 1.0),
        'ctx': nrm(ks[2], (BATCH, CTX_LEN, D), 1.0),
        'c_ctx': nrm(ks[3], (D,), 1.0),
        'ada_w': nrm(ks[4], (DEPTH, D, 6 * D), 0.5 * D ** -0.5),
        'ada_b': nrm(ks[5], (DEPTH, 6 * D), 0.02),
        'norm_w': 1.0 + nrm(ks[6], (DEPTH, 2, D), 0.02),
        'ffn_w_in': nrm(ks[7], (DEPTH, D, 2 * D_FF), D ** -0.5),
        'ffn_w_out': nrm(ks[8], (DEPTH, D_FF, D), D_FF ** -0.5),
        'ab_w_in': nrm(ks[9], (n_even, D, AB_IN), D ** -0.5),
        'ab_w_out': nrm(ks[10], (n_even, D_RET + D_MLSTM, D), (D_RET + D_MLSTM) ** -0.5),
        'ret_log_gamma': base_lg * (1.0 + nrm(ks[11], (n_even, 2, H_RET), 0.05)),
        'ret_norm_w': 1.0 + nrm(ks[12], (n_even, D_RET), 0.02),
        'mlstm_conv_w': nrm(ks[13], (n_even, CONV_W, 2 * D_MLSTM), CONV_W ** -0.5),
        'mlstm_conv_b': nrm(ks[14], (n_even, 2 * D_MLSTM), 0.02),
        'mlstm_gate_b': gate_base + nrm(ks[15], (n_even, 4, H_MLSTM), 0.1),
        'mlstm_norm_w': 1.0 + nrm(ks[16], (n_even, D_MLSTM), 0.02),
        'attn_w_in': nrm(ks[17], (n_odd, D, ATTN_IN), D ** -0.5),
        'attn_w_out': nrm(ks[18], (n_odd, D, D), D ** -0.5),
        'attn_q_norm_w': 1.0 + nrm(ks[19], (n_odd, HEAD_DIM), 0.02),
        'attn_k_norm_w': 1.0 + nrm(ks[20], (n_odd, HEAD_DIM), 0.02),
        'attn_sink': nrm(ks[21], (n_odd, H_ATTN), 0.5),
    }


def reference(x, c, ctx, c_ctx, ada_w, ada_b, norm_w, ffn_w_in, ffn_w_out, ab_w_in, ab_w_out,
              ret_log_gamma, ret_norm_w, mlstm_conv_w, mlstm_conv_b, mlstm_gate_b, mlstm_norm_w,
              attn_w_in, attn_w_out, attn_q_norm_w, attn_k_norm_w, attn_sink):
    L = x.shape[1]
    cos, sin = _axial_rope(L, x.dtype)
    rope = lambda t: _apply_rope(t, cos, sin)
    silu_c = jax.nn.silu(c)
    silu_cc = jax.nn.silu(c_ctx)
    for layer in range(DEPTH):
        last = layer == DEPTH - 1
        mod = (silu_c @ ada_w[layer] + ada_b[layer])[:, None, :]
        mod_c = (silu_cc @ ada_w[layer] + ada_b[layer])[None, None, :]
        sh1, sc1, g1, sh2, sc2, g2 = jnp.split(mod, 6, axis=-1)
        csh1, csc1, cg1, csh2, csc2, cg2 = jnp.split(mod_c, 6, axis=-1)
        h = _rmsnorm(x, norm_w[layer, 0]) * (1.0 + sc1) + sh1
        hc = _rmsnorm(ctx, norm_w[layer, 0]) * (1.0 + csc1) + csh1
        if layer % 2 == 0:
            e = layer // 2
            y, yc = _ret_mlstm_mixer(h, hc, rope, ab_w_in[e], ab_w_out[e], ret_log_gamma[e], ret_norm_w[e],
                                     mlstm_conv_w[e], mlstm_conv_b[e], mlstm_gate_b[e], mlstm_norm_w[e])
        else:
            o = layer // 2
            y, yc = _window_attn_mixer(h, hc, rope, attn_w_in[o], attn_w_out[o], attn_q_norm_w[o],
                                       attn_k_norm_w[o], attn_sink[o], not last)
        x = x + g1 * y
        x = x + g2 * _swiglu(_rmsnorm(x, norm_w[layer, 1]) * (1.0 + sc2) + sh2, ffn_w_in[layer], ffn_w_out[layer])
        if not last:
            ctx = ctx + cg1 * yc
            ctx = ctx + cg2 * _swiglu(_rmsnorm(ctx, norm_w[layer, 1]) * (1.0 + csc2) + csh2,
                                      ffn_w_in[layer], ffn_w_out[layer])
    return x
```

```python
import functools

import numpy as np
import jax
import jax.numpy as jnp
from jax import lax
from jax.experimental import pallas as pl
from jax.experimental.pallas import tpu as pltpu

F32 = jnp.float32
BF16 = jnp.bfloat16

HEAD_DIM = 64
CHUNK = 128
GRID_W = 64
ROPE_BASE = 10000.0
EPS = 1e-6
H_KV = 4
LANES = 128
ROW_TILE = 256
HALO = 16
NEG = -1e30
VMEM_LIMIT = 56 * 1024 * 1024


def _cparams(*sem):
    return pltpu.CompilerParams(dimension_semantics=sem, vmem_limit_bytes=VMEM_LIMIT)


def _const_spec(shape):
    nd = len(shape)
    return pl.BlockSpec(shape, lambda *_: (0,) * nd, pipeline_mode=pl.Buffered(1))


def _smem_spec():
    return pl.BlockSpec(memory_space=pltpu.SMEM)


def _lane(shape=(CHUNK, LANES)):
    return lax.broadcasted_iota(jnp.int32, shape, len(shape) - 1)


def _dot(a, b):
    return jnp.dot(a, b, preferred_element_type=F32)


def _dot_nt(a, b):
    return lax.dot_general(a, b, (((1,), (1,)), ((), ())), preferred_element_type=F32)


def _dot_tn(a, b):
    return lax.dot_general(a, b, (((0,), (0,)), ((), ())), preferred_element_type=F32)


def _split3(x):
    hi = x.astype(BF16)
    r = x - hi.astype(F32)
    mid = r.astype(BF16)
    lo = (r - mid.astype(F32)).astype(BF16)
    return hi, mid, lo


def _log_sigmoid(x):
    return jnp.minimum(x, 0.0) - jnp.log1p(jnp.exp(-jnp.abs(x)))


def _rope(x, cos, sin_signed):
    return x * cos + pltpu.roll(x, LANES // 2, 1) * sin_signed


def _mod_kernel(rows_ref, w_ref, b_ref, o_ref):
    a = rows_ref[...]
    a = a * jax.nn.sigmoid(a)
    a_hi = a.astype(BF16)
    a_lo = (a - a_hi.astype(F32)).astype(BF16)
    w = w_ref[...]
    w_hi = w.astype(BF16)
    w_lo = (w - w_hi.astype(F32)).astype(BF16)
    o_ref[...] = _dot(a_hi, w_hi) + _dot(a_hi, w_lo) + _dot(a_lo, w_hi) + b_ref[...]


def _modulation(rows, ada_w, ada_b):
    depth, d, n = ada_w.shape
    tn = n // 4
    return pl.pallas_call(
        _mod_kernel,
        grid=(depth, n // tn),
        in_specs=[pl.BlockSpec((8, d), lambda l, j: (0, 0)),
                  pl.BlockSpec((None, d, tn), lambda l, j: (l, 0, j)),
                  pl.BlockSpec((None, 1, tn), lambda l, j: (l, 0, j))],
        out_specs=pl.BlockSpec((None, 8, tn), lambda l, j: (l, 0, j)),
        out_shape=jax.ShapeDtypeStruct((depth, 8, n), F32),
        compiler_params=_cparams("arbitrary", "arbitrary"),
        name="adaln_modulation",
    )(rows, ada_w, ada_b.reshape(depth, 1, n))


def _norm_mod(x, ab_ref):
    ms = jnp.mean(x * x, axis=-1, keepdims=True)
    h = (x * lax.rsqrt(ms + EPS)) * ab_ref[0:1, :] + ab_ref[1:2, :]
    return h.astype(BF16)


def _inproj_ab_kernel(x_ref, ab_ref, w_ref, wg_ref, wgt_ref, gb_ref, gbt_ref, cos_ref, sin_ref,
                      y_ref, g_ref, gt_ref):
    hb = _norm_mod(x_ref[...], ab_ref)
    cos = cos_ref[...]
    sin = sin_ref[...]
    gw = 4 * LANES
    for j in range(8):
        acc = _dot(hb, w_ref[:, j * gw:(j + 1) * gw])
        if j in (0, 1):
            if j == 0:
                acc = acc * (HEAD_DIM ** -0.5)
            acc = jnp.concatenate(
                [_rope(acc[:, p * LANES:(p + 1) * LANES], cos, sin) for p in range(4)], axis=1)
        elif j == 3:
            acc = acc * jax.nn.sigmoid(acc)
        elif j == 7:
            acc = jax.nn.sigmoid(acc)
        y_ref[:, j * gw:(j + 1) * gw] = acc.astype(BF16)
    g_ref[...] = _dot(hb, wg_ref[...]) + gb_ref[...]
    gt_ref[...] = _dot_nt(wgt_ref[...], hb) + gbt_ref[...]


def _inproj_ab(x2, ab, w, wg, wgt, gb, gbt, cos, sin, tiles_per_batch):
    r, d = x2.shape
    tm = ROW_TILE
    n = w.shape[1]
    tpb = tiles_per_batch
    sel = lambda i: ((i // tpb) * 2 + jnp.minimum(i % tpb, 1), 0, 0)
    return pl.pallas_call(
        _inproj_ab_kernel,
        grid=(r // tm,),
        in_specs=[pl.BlockSpec((tm, d), lambda i: (i, 0)),
                  pl.BlockSpec((None, 2, d), sel),
                  _const_spec(w.shape), _const_spec(wg.shape), _const_spec(wgt.shape),
                  _const_spec(gb.shape), _const_spec(gbt.shape),
                  pl.BlockSpec((tm, LANES), lambda i: (i % tpb, 0)),
                  pl.BlockSpec((tm, LANES), lambda i: (i % tpb, 0))],
        out_specs=[pl.BlockSpec((tm, n), lambda i: (i, 0)),
                   pl.BlockSpec((tm, LANES), lambda i: (i, 0)),
                   pl.BlockSpec((32, tm), lambda i: (0, i))],
        out_shape=[jax.ShapeDtypeStruct((r, n), BF16),
                   jax.ShapeDtypeStruct((r, LANES), F32),
                   jax.ShapeDtypeStruct((32, r), F32)],
        compiler_params=_cparams("arbitrary"),
        name="inproj_ret_mlstm",
    )(x2, ab, w, wg, wgt, gb, gbt, cos, sin)


def _pair_rmsnorm(a, mask_a, w_lanes):
    sq = a * a
    s_a = jnp.sum(jnp.where(mask_a, sq, 0.0), axis=-1, keepdims=True)
    s_b = jnp.sum(jnp.where(mask_a, 0.0, sq), axis=-1, keepdims=True)
    ms = jnp.where(mask_a, s_a, s_b) * (1.0 / HEAD_DIM)
    return a * lax.rsqrt(ms + EPS) * w_lanes


def _inproj_attn_kernel(x_ref, ab_ref, w_ref, nw_ref, cos_ref, sin_ref, y_ref):
    hb = _norm_mod(x_ref[...], ab_ref)
    cos = cos_ref[...]
    sin = sin_ref[...]
    tm = hb.shape[0]
    mask_a = (_lane((tm, LANES)) & 32) == 0
    for j in range(10):
        acc = _dot(hb, w_ref[:, j * LANES:(j + 1) * LANES])
        nw = nw_ref[0:1, :] if j < 8 else nw_ref[1:2, :]
        acc = _rope(_pair_rmsnorm(acc, mask_a, nw), cos, sin)
        y_ref[:, j * LANES:(j + 1) * LANES] = acc.astype(BF16)
    acc = _dot(hb, w_ref[:, 10 * LANES:12 * LANES])
    y_ref[:, 10 * LANES:12 * LANES] = acc.astype(BF16)


def _inproj_attn(x2, ab, w, nw, cos, sin, tiles_per_batch):
    r, d = x2.shape
    tm = ROW_TILE
    n = w.shape[1]
    tpb = tiles_per_batch
    sel = lambda i: ((i // tpb) * 2 + jnp.minimum(i % tpb, 1), 0, 0)
    return pl.pallas_call(
        _inproj_attn_kernel,
        grid=(r // tm,),
        in_specs=[pl.BlockSpec((tm, d), lambda i: (i, 0)),
                  pl.BlockSpec((None, 2, d), sel),
                  _const_spec(w.shape), _const_spec(nw.shape),
                  pl.BlockSpec((tm, LANES), lambda i: (i % tpb, 0)),
                  pl.BlockSpec((tm, LANES), lambda i: (i % tpb, 0))],
        out_specs=pl.BlockSpec((tm, n), lambda i: (i, 0)),
        out_shape=jax.ShapeDtypeStruct((r, n), BF16),
        compiler_params=_cparams("arbitrary"),
        name="inproj_attn",
    )(x2, ab, w, nw, cos, sin)


def _conv_silu(cur_ref, prev_ref, next_ref, cw_ref, prev_on, next_on):
    cur = cur_ref[...].astype(F32)
    row = lax.broadcasted_iota(jnp.int32, cur.shape, 0)
    prev_row = prev_ref[HALO - 1:HALO, :].astype(F32) * prev_on
    next_row = next_ref[0:1, :].astype(F32) * next_on
    xm = jnp.where(row == 0, prev_row, pltpu.roll(cur, 1, 0))
    xp = jnp.where(row == CHUNK - 1, next_row, pltpu.roll(cur, CHUNK - 1, 0))
    y = cw_ref[3:4, :] + cw_ref[0:1, :] * xm + cw_ref[1:2, :] * cur + cw_ref[2:3, :] * xp
    return y * jax.nn.sigmoid(y)


def _segment_flags(c, nc, ctx_chunks):
    prev_on = jnp.where((c == 0) | (c == ctx_chunks), 0.0, 1.0).astype(F32)
    next_on = jnp.where((c == ctx_chunks - 1) | (c == nc - 1), 0.0, 1.0).astype(F32)
    return prev_on, next_on


def _cumsum_cols(tri_bf, lf):
    hi, mid, lo = _split3(lf)
    return _dot(tri_bf, hi) + _dot(tri_bf, mid) + _dot(tri_bf, lo)


def _cumsum_rows(lf, tri_bf):
    hi, mid, lo = _split3(lf)
    return _dot(hi, tri_bf) + _dot(mid, tri_bf) + _dot(lo, tri_bf)


def _ret_state_update(s_ref, p, k2, v2, kdec, cd_lanes, bd):
    kf = (k2.astype(F32) * kdec).astype(BF16)
    kv = _dot_tn(kf, v2)
    s_ref[p] = s_ref[p] * cd_lanes + jnp.where(bd, kv, 0.0)


def _mlstm_state_update(c_ref, n_ref, m_ref, p, k2c, v2, b_col, i_col, bend_row, gcol0, lo, bd):
    wes, a_s, bb_s = [], [], []
    for a in range(2):
        h = 2 * p + a
        bc = b_col[:, gcol0 + 8 + h:gcol0 + 9 + h]
        ic = i_col[:, gcol0 + h:gcol0 + 1 + h]
        bend = b_col[bend_row:bend_row + 1, gcol0 + 8 + h:gcol0 + 9 + h]
        ge = bend - bc + ic
        gmax = jnp.max(ge, axis=0, keepdims=True)
        wes.append(jnp.exp(ge - gmax))
        m_old = m_ref[h:h + 1, 0:1]
        m_new = jnp.maximum(bend + m_old, gmax)
        a_s.append(jnp.exp(bend + m_old - m_new))
        bb_s.append(jnp.exp(gmax - m_new))
        m_ref[h:h + 1, :] = jnp.broadcast_to(m_new, (1, LANES))
    kw = k2c * jnp.where(lo, wes[0], wes[1])
    kv = _dot_tn(kw.astype(BF16), v2)
    nloc = jnp.sum(kw, axis=0, keepdims=True)
    lo_row = lo[0:1, :]
    a_l = jnp.where(lo_row, a_s[0], a_s[1])
    bb_l = jnp.where(lo_row, bb_s[0], bb_s[1])
    c_ref[p] = c_ref[p] * a_l + jnp.where(bd, kv, 0.0) * bb_l
    n_ref[p:p + 1, :] = n_ref[p:p + 1, :] * a_l + nloc * bb_l


def _bwd_state_kernel(rk_ref, rv_ref, mk_ref, mkp_ref, mkn_ref, mv_ref, g_ref,
                      lgk_ref, lgv_ref, cw_ref,
                      sret_ref, cm_ref, nm_ref, mm_ref,
                      s_scr, c_scr, n_scr, m_scr, kdec_scr, *, nc, ctx_chunks):
    i = pl.program_id(1)
    c = jnp.where(i < ctx_chunks, ctx_chunks - 1 - i, nc + ctx_chunks - 1 - i)
    lane = _lane()
    sub = lax.broadcasted_iota(jnp.int32, (CHUNK, LANES), 0)
    lo = lane < HEAD_DIM
    bd_ret = ((sub & 32) != 0) == (lane >= HEAD_DIM)
    bd_m = (sub >= HEAD_DIM) == (lane >= HEAD_DIM)

    @pl.when(i == 0)
    def _():
        s_scr[...] = jnp.zeros_like(s_scr)
        c_scr[...] = jnp.zeros_like(c_scr)
        n_scr[...] = jnp.zeros_like(n_scr)
        m_scr[...] = jnp.zeros_like(m_scr)
        pos = sub.astype(F32)
        for p in range(4):
            kdec_scr[p] = jnp.exp(lgk_ref[4 + p:5 + p, :] * pos)

    sret_ref[...] = s_scr[...].astype(BF16)
    cm_ref[...] = c_scr[...].astype(BF16)
    nm_ref[...] = n_scr[...]
    mm_ref[...] = m_scr[...]

    for p in range(4):
        cd = jnp.exp(lgv_ref[4 + p:5 + p, :] * float(CHUNK))
        _ret_state_update(s_scr, p, rk_ref[:, p * LANES:(p + 1) * LANES],
                          rv_ref[:, p * LANES:(p + 1) * LANES], kdec_scr[p], cd, bd_ret)

    prev_on, next_on = _segment_flags(c, nc, ctx_chunks)
    kc = _conv_silu(mk_ref, mkp_ref, mkn_ref, cw_ref, prev_on, next_on)
    g = g_ref[...]
    upper = (sub <= lane).astype(BF16)
    b_col = _cumsum_cols(upper, _log_sigmoid(g))
    for p in range(4):
        _mlstm_state_update(c_scr, n_scr, m_scr, p, kc[:, p * LANES:(p + 1) * LANES],
                            mv_ref[:, p * LANES:(p + 1) * LANES], b_col, g, 0, 16, lo, bd_m)


def _bwd_states(y, g, lgk, lgv, cw_k, nb, nc, ctx_chunks):
    gw = 4 * LANES
    def cidx(b, i):
        c = jnp.where(i < ctx_chunks, ctx_chunks - 1 - i, nc + ctx_chunks - 1 - i)
        return b * nc + c
    hb = CHUNK // HALO
    nhalo = y.shape[0] // HALO
    return pl.pallas_call(
        functools.partial(_bwd_state_kernel, nc=nc, ctx_chunks=ctx_chunks),
        grid=(nb, nc),
        in_specs=[pl.BlockSpec((CHUNK, gw), lambda b, i: (cidx(b, i), 1)),
                  pl.BlockSpec((CHUNK, gw), lambda b, i: (cidx(b, i), 2)),
                  pl.BlockSpec((CHUNK, gw), lambda b, i: (cidx(b, i), 5)),
                  pl.BlockSpec((HALO, gw), lambda b, i: (jnp.maximum(cidx(b, i) * hb - 1, 0), 5)),
                  pl.BlockSpec((HALO, gw), lambda b, i: (jnp.minimum((cidx(b, i) + 1) * hb, nhalo - 1), 5)),
                  pl.BlockSpec((CHUNK, gw), lambda b, i: (cidx(b, i), 6)),
                  pl.BlockSpec((CHUNK, LANES), lambda b, i: (cidx(b, i), 0)),
                  _const_spec(lgk.shape), _const_spec(lgv.shape), _const_spec(cw_k.shape)],
        out_specs=[pl.BlockSpec((None, 4, CHUNK, LANES), lambda b, i: (cidx(b, i), 0, 0, 0)),
                   pl.BlockSpec((None, 4, CHUNK, LANES), lambda b, i: (cidx(b, i), 0, 0, 0)),
                   pl.BlockSpec((None, 8, LANES), lambda b, i: (cidx(b, i), 0, 0)),
                   pl.BlockSpec((None, 8, LANES), lambda b, i: (cidx(b, i), 0, 0))],
        out_shape=[jax.ShapeDtypeStruct((nb * nc, 4, CHUNK, LANES), BF16),
                   jax.ShapeDtypeStruct((nb * nc, 4, CHUNK, LANES), BF16),
                   jax.ShapeDtypeStruct((nb * nc, 8, LANES), F32),
                   jax.ShapeDtypeStruct((nb * nc, 8, LANES), F32)],
        scratch_shapes=[pltpu.VMEM((4, CHUNK, LANES), F32), pltpu.VMEM((4, CHUNK, LANES), F32),
                        pltpu.VMEM((8, LANES), F32), pltpu.VMEM((8, LANES), F32),
                        pltpu.VMEM((4, CHUNK, LANES), F32)],
        compiler_params=_cparams("arbitrary", "arbitrary"),
        name="bwd_state_sweep",
    )(y, y, y, y, y, y, g, lgk, lgv, cw_k)


def _mlstm_dir_weights(s_h, qn, b_col_h, c_row_h, m_prev, tri):
    dl = jnp.where(tri, b_col_h + c_row_h, NEG)
    mx = jnp.max(dl, axis=1, keepdims=True)
    al = b_col_h + m_prev
    m_t = jnp.maximum(al, mx)
    w = jnp.exp(dl - m_t)
    a_t = jnp.exp(al - m_t)
    sw = s_h * w
    den = jnp.sum(sw, axis=1, keepdims=True) + a_t * qn
    r = 1.0 / jnp.maximum(jnp.abs(den), jnp.exp(-m_t))
    return sw * r, a_t * r


def _mixer_kernel(lg_ref, rq_ref, rk_ref, rv_ref, rg_ref, mqk_ref, mqkp_ref, mqkn_ref, mv_ref, mo_ref,
                  g_ref, gt_ref, sretb_ref, cmb_ref, nmb_ref, mmb_ref,
                  lgk_ref, lgv_ref, cw_ref, nw_ref,
                  out_ref,
                  s_scr, c_scr, n_scr, m_scr, dm_scr, dec_scr, *, nc, ctx_chunks):
    c = pl.program_id(1)
    lane = _lane()
    sub = lax.broadcasted_iota(jnp.int32, (CHUNK, LANES), 0)
    lo = lane < HEAD_DIM
    mask_ret = [(lane & 32) == 0, (lane & 32) != 0]
    mask_nat = [lo, lane >= HEAD_DIM]
    bd_ret = ((sub & 32) != 0) == (lane >= HEAD_DIM)
    bd_m = (sub >= HEAD_DIM) == (lane >= HEAD_DIM)
    lower = sub >= lane
    upper = sub <= lane

    @pl.when(c == 0)
    def _():
        s_scr[...] = jnp.zeros_like(s_scr)
        c_scr[...] = jnp.zeros_like(c_scr)
        n_scr[...] = jnp.zeros_like(n_scr)
        m_scr[...] = jnp.zeros_like(m_scr)
        pos = sub.astype(F32)
        diff = (sub - lane).astype(F32)
        for h in range(8):
            dm_scr[h] = (jnp.where(lower, jnp.exp(lg_ref[h] * diff), 0.0)
                         + jnp.where(upper, jnp.exp(lg_ref[8 + h] * (-diff)), 0.0))
        for p in range(4):
            dec_scr[0, p] = jnp.exp(lgk_ref[p:p + 1, :] * (pos + 1.0))
            dec_scr[1, p] = jnp.exp(lgk_ref[4 + p:5 + p, :] * (float(CHUNK) - pos))
            dec_scr[2, p] = jnp.exp(lgk_ref[p:p + 1, :] * (float(CHUNK) - 1.0 - pos))

    for p in range(4):
        sl = slice(p * LANES, (p + 1) * LANES)
        q2, k2, v2 = rq_ref[:, sl], rk_ref[:, sl], rv_ref[:, sl]
        o_heads = []
        for a in range(2):
            qa = jnp.where(mask_ret[a], q2, jnp.zeros_like(q2))
            s_a = _dot_nt(qa, k2)
            o_heads.append(_dot((s_a * dm_scr[2 * p + a]).astype(BF16), v2))
        q2f = q2.astype(F32)
        o = jnp.where(lo, o_heads[0], o_heads[1])
        o = o + _dot((q2f * dec_scr[0, p]).astype(BF16), s_scr[p].astype(BF16))
        o = o + _dot((q2f * dec_scr[1, p]).astype(BF16), sretb_ref[p])
        cd = jnp.exp(lgv_ref[p:p + 1, :] * float(CHUNK))
        _ret_state_update(s_scr, p, k2, v2, dec_scr[2, p], cd, bd_ret)
        sq = o * o
        s_a = jnp.sum(jnp.where(lo, sq, 0.0), axis=-1, keepdims=True)
        s_b = jnp.sum(jnp.where(lo, 0.0, sq), axis=-1, keepdims=True)
        ms = jnp.where(lo, s_a, s_b) * (1.0 / HEAD_DIM)
        y = o * lax.rsqrt(ms + EPS) * nw_ref[0:1, sl]
        out_ref[:, sl] = (y * rg_ref[:, sl].astype(F32)).astype(BF16)

    prev_on, next_on = _segment_flags(c, nc, ctx_chunks)
    qk = _conv_silu(mqk_ref, mqkp_ref, mqkn_ref, cw_ref, prev_on, next_on)
    g = g_ref[...]
    gt = gt_ref[...]
    lf_col = _log_sigmoid(g)
    lf_row = _log_sigmoid(gt)
    lower_bf = lower.astype(BF16)
    upper_bf = upper.astype(BF16)
    bf_col = _cumsum_cols(lower_bf, lf_col)
    bb_col = _cumsum_cols(upper_bf, lf_col)
    bf_row = _cumsum_rows(lf_row, upper_bf)
    bb_row = _cumsum_rows(lf_row, lower_bf)
    for p in range(4):
        sl = slice(p * LANES, (p + 1) * LANES)
        q2c = qk[:, sl] * (HEAD_DIM ** -0.5)
        k2c = qk[:, 4 * LANES + p * LANES:4 * LANES + (p + 1) * LANES]
        v2 = mv_ref[:, sl]
        q2b = q2c.astype(BF16)
        k2b = k2c.astype(BF16)
        nf_row = n_scr[p:p + 1, :]
        nb_row = nmb_ref[p:p + 1, :]
        o_heads, cf, cb = [], [], []
        for a in range(2):
            h = 2 * p + a
            qa = jnp.where(mask_nat[a], q2b, jnp.zeros_like(q2b))
            qaf = jnp.where(mask_nat[a], q2c, 0.0)
            s_h = _dot_nt(qa, k2b)
            pf, coef_f = _mlstm_dir_weights(
                s_h, jnp.sum(qaf * nf_row, axis=1, keepdims=True),
                bf_col[:, 8 + h:9 + h], gt[h:h + 1, :] - bf_row[8 + h:9 + h, :],
                m_scr[h:h + 1, 0:1], lower)
            pb, coef_b = _mlstm_dir_weights(
                s_h, jnp.sum(qaf * nb_row, axis=1, keepdims=True),
                bb_col[:, 24 + h:25 + h], gt[16 + h:17 + h, :] - bb_row[24 + h:25 + h, :],
                mmb_ref[h:h + 1, 0:1], upper)
            o_heads.append(_dot((pf + pb).astype(BF16), v2))
            cf.append(coef_f)
            cb.append(coef_b)
        hout = jnp.where(lo, o_heads[0], o_heads[1])
        hout = hout + jnp.where(lo, cf[0], cf[1]) * _dot(q2b, c_scr[p].astype(BF16))
        hout = hout + jnp.where(lo, cb[0], cb[1]) * _dot(q2b, cmb_ref[p])
        _mlstm_state_update(c_scr, n_scr, m_scr, p, k2c, v2, bf_col, g, CHUNK - 1, 0, lo, bd_m)
        sq = hout * hout
        s_a = jnp.sum(jnp.where(lo, sq, 0.0), axis=-1, keepdims=True)
        s_b = jnp.sum(jnp.where(lo, 0.0, sq), axis=-1, keepdims=True)
        ms = jnp.where(lo, s_a, s_b) * (1.0 / HEAD_DIM)
        y = hout * lax.rsqrt(ms + EPS) * nw_ref[1:2, sl]
        out_ref[:, 4 * LANES + p * LANES:4 * LANES + (p + 1) * LANES] = (
            y * mo_ref[:, sl].astype(F32)).astype(BF16)


def _mixer(y, g, gt, states, lg_smem, lgk, lgv, cw_qk, nw, nb, nc, ctx_chunks):
    gw = 4 * LANES
    sretb, cmb, nmb, mmb = states
    rc = lambda b, c: b * nc + c
    hb = CHUNK // HALO
    nhalo = y.shape[0] // HALO
    blk = lambda j: pl.BlockSpec((CHUNK, gw), lambda b, c: (rc(b, c), j))
    return pl.pallas_call(
        functools.partial(_mixer_kernel, nc=nc, ctx_chunks=ctx_chunks),
        grid=(nb, nc),
        in_specs=[_smem_spec(),
                  blk(0), blk(1), blk(2), blk(3),
                  pl.BlockSpec((CHUNK, 2 * gw), lambda b, c: (rc(b, c), 2)),
                  pl.BlockSpec((HALO, 2 * gw), lambda b, c: (jnp.maximum(rc(b, c) * hb - 1, 0), 2)),
                  pl.BlockSpec((HALO, 2 * gw), lambda b, c: (jnp.minimum((rc(b, c) + 1) * hb, nhalo - 1), 2)),
                  blk(6), blk(7),
                  pl.BlockSpec((CHUNK, LANES), lambda b, c: (rc(b, c), 0)),
                  pl.BlockSpec((32, CHUNK), lambda b, c: (0, rc(b, c))),
                  pl.BlockSpec((None, 4, CHUNK, LANES), lambda b, c: (rc(b, c), 0, 0, 0)),
                  pl.BlockSpec((None, 4, CHUNK, LANES), lambda b, c: (rc(b, c), 0, 0, 0)),
                  pl.BlockSpec((None, 8, LANES), lambda b, c: (rc(b, c), 0, 0)),
                  pl.BlockSpec((None, 8, LANES), lambda b, c: (rc(b, c), 0, 0)),
                  _const_spec(lgk.shape), _const_spec(lgv.shape), _const_spec(cw_qk.shape),
                  _const_spec(nw.shape)],
        out_specs=pl.BlockSpec((CHUNK, 2 * gw), lambda b, c: (rc(b, c), 0)),
        out_shape=jax.ShapeDtypeStruct((nb * nc * CHUNK, 2 * gw), BF16),
        scratch_shapes=[pltpu.VMEM((4, CHUNK, LANES), F32), pltpu.VMEM((4, CHUNK, LANES), F32),
                        pltpu.VMEM((8, LANES), F32), pltpu.VMEM((8, LANES), F32),
                        pltpu.VMEM((8, CHUNK, LANES), F32), pltpu.VMEM((3, 4, CHUNK, LANES), F32)],
        compiler_params=_cparams("arbitrary", "arbitrary"),
        name="ret_mlstm_mixer",
    )(lg_smem, y, y, y, y, y, y, y, y, y, g, gt, sretb, cmb, nmb, mmb, lgk, lgv, cw_qk, nw)


def _attn_kernel(sink_ref, q_ref, kp_ref, kc_ref, kn_ref, kx_ref, vp_ref, vc_ref, vn_ref, vx_ref,
                 o_ref, *, nq, ctx_len):
    i = pl.program_id(1)
    nk = 3 * CHUNK + ctx_len
    rows = GQA_ROWS = 4 * CHUNK
    t = lax.broadcasted_iota(jnp.int32, (rows, nk), 0) & (CHUNK - 1)
    kk = lax.broadcasted_iota(jnp.int32, (rows, nk), 1)
    rel = kk - CHUNK - t
    valid = (jnp.abs(rel) <= CHUNK) & ((kk >= CHUNK) | (i > 0)) & ((kk < 2 * CHUNK) | (i < nq - 1))
    valid = valid | (kk >= 3 * CHUNK)
    grp = lax.broadcasted_iota(jnp.int32, (rows, 1), 0) // CHUNK
    lane = _lane()
    lo = lane < HEAD_DIM
    mask_q = [(lane & 32) == 0, (lane & 32) != 0]
    for kvp in range(2):
        sl = slice(kvp * LANES, (kvp + 1) * LANES)
        kcat = jnp.concatenate([kp_ref[:, sl], kc_ref[:, sl], kn_ref[:, sl], kx_ref[:, sl]], axis=0)
        vcat = jnp.concatenate([vp_ref[:, sl], vc_ref[:, sl], vn_ref[:, sl], vx_ref[:, sl]], axis=0)
        outs = []
        for a in range(2):
            kv = 2 * kvp + a
            qs = jnp.concatenate(
                [jnp.where(mask_q[a], q_ref[:, (kvp * 4 + g) * LANES:(kvp * 4 + g + 1) * LANES],
                           jnp.zeros((CHUNK, LANES), BF16)) for g in range(4)], axis=0)
            s = jnp.where(valid, _dot_nt(qs, kcat), NEG)
            snk = jnp.where(grp == 0, sink_ref[kv * 4],
                            jnp.where(grp == 1, sink_ref[kv * 4 + 1],
                                      jnp.where(grp == 2, sink_ref[kv * 4 + 2], sink_ref[kv * 4 + 3])))
            m = jnp.maximum(jnp.max(s, axis=1, keepdims=True), snk)
            e = jnp.exp(s - m)
            denom = jnp.exp(snk - m) + jnp.sum(e, axis=1, keepdims=True)
            outs.append(_dot(e.astype(BF16), vcat) * (1.0 / denom))
        for g in range(4):
            rs = slice(g * CHUNK, (g + 1) * CHUNK)
            o_ref[:, (kvp * 4 + g) * LANES:(kvp * 4 + g + 1) * LANES] = jnp.where(
                lo, outs[0][rs], outs[1][rs]).astype(BF16)


def _window_attn(y, sink, nb, nc, ctx_chunks):
    nq = nc - ctx_chunks
    ctx_len = ctx_chunks * CHUNK
    d = 8 * LANES
    kcol, vcol = d // (2 * LANES), d // (2 * LANES) + 1
    qrow = lambda b, i: b * nc + ctx_chunks + i
    prow = lambda b, i: b * nc + ctx_chunks + jnp.maximum(i - 1, 0)
    nrow = lambda b, i: b * nc + ctx_chunks + jnp.minimum(i + 1, nq - 1)
    xrow = lambda b, i: (b * nc * CHUNK) // ctx_len
    kv_spec = lambda rowf, col: pl.BlockSpec((CHUNK, 2 * LANES), lambda b, i: (rowf(b, i), col))
    x_spec = lambda col: pl.BlockSpec((ctx_len, 2 * LANES), lambda b, i: (xrow(b, i), col))
    return pl.pallas_call(
        functools.partial(_attn_kernel, nq=nq, ctx_len=ctx_len),
        grid=(nb, nq),
        in_specs=[_smem_spec(),
                  pl.BlockSpec((CHUNK, d), lambda b, i: (qrow(b, i), 0)),
                  kv_spec(prow, kcol), kv_spec(qrow, kcol), kv_spec(nrow, kcol), x_spec(kcol),
                  kv_spec(prow, vcol), kv_spec(qrow, vcol), kv_spec(nrow, vcol), x_spec(vcol)],
        out_specs=pl.BlockSpec((CHUNK, d), lambda b, i: (b * nq + i, 0)),
        out_shape=jax.ShapeDtypeStruct((nb * nq * CHUNK, d), BF16),
        compiler_params=_cparams("arbitrary", "arbitrary"),
        name="window_gqa",
    )(sink, y, y, y, y, y, y, y, y, y)


def _post_kernel(x_ref, m_ref, p_ref, wo_ref, wi_ref, w2_ref, o_ref, act_scr, *, d_ff, ff_chunk):
    x1 = x_ref[...] + p_ref[0:1, :] * _dot(m_ref[...], wo_ref[...])
    ms = jnp.mean(x1 * x1, axis=-1, keepdims=True)
    h = ((x1 * lax.rsqrt(ms + EPS)) * p_ref[1:2, :] + p_ref[2:3, :]).astype(BF16)
    for j in range(d_ff // ff_chunk):
        gate = _dot(h, wi_ref[:, j * ff_chunk:(j + 1) * ff_chunk])
        up = _dot(h, wi_ref[:, d_ff + j * ff_chunk:d_ff + (j + 1) * ff_chunk])
        act_scr[:, j * ff_chunk:(j + 1) * ff_chunk] = (gate * jax.nn.sigmoid(gate) * up).astype(BF16)
    o_ref[...] = x1 + p_ref[3:4, :] * _dot(act_scr[...], w2_ref[...])


def _post(x2, m2, prm, wo, wi, w2, tiles_per_batch, skip_tiles, m_has_ctx):
    r, d = x2.shape
    tm = ROW_TILE
    d_ff = w2.shape[0]
    tpb = tiles_per_batch
    kept = tpb - skip_tiles
    nb = r // (tm * tpb)
    xrow = lambda i: (i // kept) * tpb + skip_tiles + i % kept
    mrow = xrow if m_has_ctx else (lambda i: i)
    sel = lambda i: ((i // kept) * 2 + jnp.minimum(skip_tiles + i % kept, 1), 0, 0)
    ff_chunk = d_ff // 4 if (d_ff // 4) % LANES == 0 else d_ff
    return pl.pallas_call(
        functools.partial(_post_kernel, d_ff=d_ff, ff_chunk=ff_chunk),
        grid=(nb * kept,),
        in_specs=[pl.BlockSpec((tm, d), lambda i: (xrow(i), 0)),
                  pl.BlockSpec((tm, d), lambda i: (mrow(i), 0)),
                  pl.BlockSpec((None, 8, d), sel),
                  _const_spec(wo.shape), _const_spec(wi.shape), _const_spec(w2.shape)],
        out_specs=pl.BlockSpec((tm, d), lambda i: (i, 0)),
        out_shape=jax.ShapeDtypeStruct((nb * kept * tm, d), F32),
        scratch_shapes=[pltpu.VMEM((tm, d_ff), BF16)],
        compiler_params=_cparams("arbitrary"),
        name="outproj_swiglu",
    )(x2, m2, prm, wo, wi, w2)


def _pair_perm(n_heads):
    idx = []
    for p in range(n_heads // 2):
        for l in range(LANES):
            grp, r = divmod(l, 32)
            idx.append((2 * p + grp % 2) * HEAD_DIM + (grp // 2) * 32 + r)
    return np.asarray(idx, np.int32)


def _attn_q_perm():
    g_per = 4
    idx = []
    for kvp in range(H_KV // 2):
        for g in range(g_per):
            heads = ((2 * kvp) * g_per + g, (2 * kvp + 1) * g_per + g)
            for l in range(LANES):
                grp, r = divmod(l, 32)
                idx.append(heads[grp % 2] * HEAD_DIM + (grp // 2) * 32 + r)
    return np.asarray(idx, np.int32)


def _attn_o_perm():
    g_per = 4
    idx = []
    for kvp in range(H_KV // 2):
        for g in range(g_per):
            for a in range(2):
                hq = (2 * kvp + a) * g_per + g
                idx.extend(range(hq * HEAD_DIM, (hq + 1) * HEAD_DIM))
    return np.asarray(idx, np.int32)


def _rope_tables(seq, ctx_len):
    rows = seq // GRID_W
    row = jnp.repeat(jnp.arange(rows, dtype=F32), GRID_W)
    col = jnp.tile(jnp.arange(GRID_W, dtype=F32), rows)
    n = HEAD_DIM // 4
    inv = ROPE_BASE ** (-jnp.arange(n, dtype=F32) / n)
    ang = jnp.concatenate([row[:, None] * inv, col[:, None] * inv], axis=-1)
    cos, sin = jnp.cos(ang), jnp.sin(ang)
    cos_t = jnp.concatenate([jnp.ones((ctx_len, LANES), F32), jnp.tile(cos, (1, 4))], axis=0)
    sin_t = jnp.concatenate([jnp.zeros((ctx_len, LANES), F32),
                             jnp.concatenate([-sin, -sin, sin, sin], axis=-1)], axis=0)
    return cos_t, sin_t


def _mod_tables(mod, nb, norm_w):
    d = norm_w.shape[-1]
    lat = mod[:nb].reshape(nb, 6, d)
    ctx = jnp.broadcast_to(mod[nb].reshape(1, 6, d), (nb, 6, d))
    both = jnp.stack([ctx, lat], axis=1).reshape(nb * 2, 6, d)
    sh1, sc1, g1, sh2, sc2, g2 = [both[:, k] for k in range(6)]
    ab1 = jnp.stack([norm_w[0] * (1.0 + sc1), sh1], axis=1)
    zeros = jnp.zeros_like(g1)
    prm = jnp.stack([g1, norm_w[1] * (1.0 + sc2), sh2, g2, zeros, zeros, zeros, zeros], axis=1)
    return ab1, prm


def kernel(x, c, ctx, c_ctx, ada_w, ada_b, norm_w, ffn_w_in, ffn_w_out, ab_w_in, ab_w_out,
           ret_log_gamma, ret_norm_w, mlstm_conv_w, mlstm_conv_b, mlstm_gate_b, mlstm_norm_w,
           attn_w_in, attn_w_out, attn_q_norm_w, attn_k_norm_w, attn_sink):
    nb, seq, d = x.shape
    ctx_len = ctx.shape[1]
    depth = ada_w.shape[0]
    assert ctx_len == ROW_TILE and seq % ROW_TILE == 0 and d == 8 * LANES and nb < 8
    t_all = ctx_len + seq
    nc = t_all // CHUNK
    ctx_chunks = ctx_len // CHUNK
    tpb = t_all // ROW_TILE
    dr = d // 2

    rows = jnp.zeros((8, d), F32).at[:nb].set(c).at[nb].set(c_ctx)
    mod_all = _modulation(rows, ada_w, ada_b)
    cos_t, sin_t = _rope_tables(seq, ctx_len)
    xc = jnp.concatenate([ctx, x], axis=1).reshape(nb * t_all, d)

    out = None
    for layer in range(depth):
        last = layer == depth - 1
        ab1, prm = _mod_tables(mod_all[layer], nb, norm_w[layer])
        wi = ffn_w_in[layer].astype(BF16)
        w2 = ffn_w_out[layer].astype(BF16)
        if layer % 2 == 0:
            e = layer // 2
            w = ab_w_in[e]
            perm = _pair_perm(dr // HEAD_DIM)
            w_main = jnp.concatenate([w[:, :dr][:, perm], w[:, dr:2 * dr][:, perm], w[:, 2 * dr:8 * dr]],
                                     axis=1).astype(BF16)
            wg = jnp.zeros((d, LANES), F32).at[:, :32].set(w[:, 8 * dr:]).astype(BF16)
            wgt = w[:, 8 * dr:].T.astype(BF16)
            gb = jnp.zeros((1, LANES), F32).at[0, :32].set(mlstm_gate_b[e].reshape(-1))
            gbt = mlstm_gate_b[e].reshape(32, 1)
            y, g, gt = _inproj_ab(xc, ab1, w_main, wg, wgt, gb, gbt, cos_t, sin_t, tpb)

            lg = ret_log_gamma[e].astype(F32)
            lgk = jnp.stack([lg[dd, (_pair_perm(8)[p * LANES:(p + 1) * LANES] // HEAD_DIM)]
                             for dd in range(2) for p in range(4)])
            lgv = jnp.repeat(lg, HEAD_DIM, axis=1).reshape(8, LANES)
            cw = jnp.concatenate([mlstm_conv_w[e], mlstm_conv_b[e][None],
                                  jnp.zeros((4, 2 * dr), F32)], axis=0)
            nw = jnp.stack([ret_norm_w[e], mlstm_norm_w[e]] + [jnp.zeros((dr,), F32)] * 6)
            lg_s = lg.reshape(-1)
            states = _bwd_states(y, g, lgk, lgv, cw[:, dr:], nb, nc, ctx_chunks)
            merged = _mixer(y, g, gt, states, lg_s, lgk, lgv, cw, nw, nb, nc, ctx_chunks)
            wo = ab_w_out[e].astype(BF16)
            m_has_ctx = True
        else:
            o = layer // 2
            w = attn_w_in[o]
            qp, kp = _attn_q_perm(), _pair_perm(H_KV)
            w_main = jnp.concatenate([w[:, :d][:, qp], w[:, d:d + 2 * LANES][:, kp],
                                      w[:, d + 2 * LANES:]], axis=1).astype(BF16)
            lane_w = lambda v: jnp.concatenate([v[:32], v[:32], v[32:], v[32:]])
            nwq = jnp.stack([lane_w(attn_q_norm_w[o]) * (HEAD_DIM ** -0.5), lane_w(attn_k_norm_w[o])]
                            + [jnp.zeros((LANES,), F32)] * 6)
            y = _inproj_attn(xc, ab1, w_main, nwq, cos_t, sin_t, tpb)
            merged = _window_attn(y, attn_sink[o].astype(F32), nb, nc, ctx_chunks)
            wo = attn_w_out[o][_attn_o_perm()].astype(BF16)
            m_has_ctx = False
            if not last:
                raise NotImplementedError("attention layers are only supported as the final layer")
        skip = tpb - seq // ROW_TILE if last else 0
        xc = _post(xc, merged, prm, wo, wi, w2, tpb, skip, m_has_ctx)
        out = xc
    return out.reshape(nb, seq, d)
```

```python
import functools

import numpy as np
import jax
import jax.numpy as jnp
from jax import lax
from jax.experimental import pallas as pl
from jax.experimental.pallas import tpu as pltpu

F32 = jnp.float32
BF16 = jnp.bfloat16

HEAD_DIM = 64
CHUNK = 128
GRID_W = 64
ROPE_BASE = 10000.0
EPS = 1e-6
H_KV = 4
LANES = 128
ROW_TILE = 256
HALO = 16
NEG = -1e30
VMEM_LIMIT = 56 * 1024 * 1024


def _cparams(*sem):
    return pltpu.CompilerParams(dimension_semantics=sem, vmem_limit_bytes=VMEM_LIMIT)


def _const_spec(shape):
    nd = len(shape)
    return pl.BlockSpec(shape, lambda *_: (0,) * nd, pipeline_mode=pl.Buffered(1))


def _smem_spec():
    return pl.BlockSpec(memory_space=pltpu.SMEM)


def _lane(shape=(CHUNK, LANES)):
    return lax.broadcasted_iota(jnp.int32, shape, len(shape) - 1)


def _dot(a, b):
    return jnp.dot(a, b, preferred_element_type=F32)


def _dot_nt(a, b):
    return lax.dot_general(a, b, (((1,), (1,)), ((), ())), preferred_element_type=F32)


def _dot_tn(a, b):
    return lax.dot_general(a, b, (((0,), (0,)), ((), ())), preferred_element_type=F32)


def _split3(x):
    hi = x.astype(BF16)
    r = x - hi.astype(F32)
    mid = r.astype(BF16)
    lo = (r - mid.astype(F32)).astype(BF16)
    return hi, mid, lo


def _log_sigmoid(x):
    return jnp.minimum(x, 0.0) - jnp.log1p(jnp.exp(-jnp.abs(x)))


def _rope(x, cos, sin_signed):
    return x * cos + pltpu.roll(x, LANES // 2, 1) * sin_signed


def _mod_kernel(rows_ref, w_ref, b_ref, o_ref):
    a = rows_ref[...]
    a = a * jax.nn.sigmoid(a)
    a_hi = a.astype(BF16)
    a_lo = (a - a_hi.astype(F32)).astype(BF16)
    w = w_ref[...]
    w_hi = w.astype(BF16)
    w_lo = (w - w_hi.astype(F32)).astype(BF16)
    o_ref[...] = _dot(a_hi, w_hi) + _dot(a_hi, w_lo) + _dot(a_lo, w_hi) + b_ref[...]


def _modulation(rows, ada_w, ada_b):
    depth, d, n = ada_w.shape
    tn = n // 4
    return pl.pallas_call(
        _mod_kernel,
        grid=(depth, n // tn),
        in_specs=[pl.BlockSpec((8, d), lambda l, j: (0, 0)),
                  pl.BlockSpec((None, d, tn), lambda l, j: (l, 0, j)),
                  pl.BlockSpec((None, 1, tn), lambda l, j: (l, 0, j))],
        out_specs=pl.BlockSpec((None, 8, tn), lambda l, j: (l, 0, j)),
        out_shape=jax.ShapeDtypeStruct((depth, 8, n), F32),
        compiler_params=_cparams("arbitrary", "arbitrary"),
        name="adaln_modulation",
    )(rows, ada_w, ada_b.reshape(depth, 1, n))


def _norm_mod(x, ab_ref):
    ms = jnp.mean(x * x, axis=-1, keepdims=True)
    h = (x * lax.rsqrt(ms + EPS)) * ab_ref[0:1, :] + ab_ref[1:2, :]
    return h.astype(BF16)


def _inproj_ab_kernel(x_ref, ab_ref, w_ref, wg_ref, wgt_ref, gb_ref, gbt_ref, cos_ref, sin_ref,
                      y_ref, g_ref, gt_ref):
    hb = _norm_mod(x_ref[...], ab_ref)
    cos = cos_ref[...]
    sin = sin_ref[...]
    gw = 4 * LANES
    for j in range(8):
        acc = _dot(hb, w_ref[:, j * gw:(j + 1) * gw])
        if j in (0, 1):
            if j == 0:
                acc = acc * (HEAD_DIM ** -0.5)
            acc = jnp.concatenate(
                [_rope(acc[:, p * LANES:(p + 1) * LANES], cos, sin) for p in range(4)], axis=1)
        elif j == 3:
            acc = acc * jax.nn.sigmoid(acc)
        elif j == 7:
            acc = jax.nn.sigmoid(acc)
        y_ref[:, j * gw:(j + 1) * gw] = acc.astype(BF16)
    g_ref[...] = _dot(hb, wg_ref[...]) + gb_ref[...]
    gt_ref[...] = _dot_nt(wgt_ref[...], hb) + gbt_ref[...]


def _inproj_ab(x2, ab, w, wg, wgt, gb, gbt, cos, sin, tiles_per_batch):
    r, d = x2.shape
    tm = ROW_TILE
    n = w.shape[1]
    tpb = tiles_per_batch
    sel = lambda i: ((i // tpb) * 2 + jnp.minimum(i % tpb, 1), 0, 0)
    return pl.pallas_call(
        _inproj_ab_kernel,
        grid=(r // tm,),
        in_specs=[pl.BlockSpec((tm, d), lambda i: (i, 0)),
                  pl.BlockSpec((None, 2, d), sel),
                  _const_spec(w.shape), _const_spec(wg.shape), _const_spec(wgt.shape),
                  _const_spec(gb.shape), _const_spec(gbt.shape),
                  pl.BlockSpec((tm, LANES), lambda i: (i % tpb, 0)),
                  pl.BlockSpec((tm, LANES), lambda i: (i % tpb, 0))],
        out_specs=[pl.BlockSpec((tm, n), lambda i: (i, 0)),
                   pl.BlockSpec((tm, LANES), lambda i: (i, 0)),
                   pl.BlockSpec((32, tm), lambda i: (0, i))],
        out_shape=[jax.ShapeDtypeStruct((r, n), BF16),
                   jax.ShapeDtypeStruct((r, LANES), F32),
                   jax.ShapeDtypeStruct((32, r), F32)],
        compiler_params=_cparams("arbitrary"),
        name="inproj_ret_mlstm",
    )(x2, ab, w, wg, wgt, gb, gbt, cos, sin)


def _pair_rmsnorm(a, mask_a, w_lanes):
    sq = a * a
    s_a = jnp.sum(jnp.where(mask_a, sq, 0.0), axis=-1, keepdims=True)
    s_b = jnp.sum(jnp.where(mask_a, 0.0, sq), axis=-1, keepdims=True)
    ms = jnp.where(mask_a, s_a, s_b) * (1.0 / HEAD_DIM)
    return a * lax.rsqrt(ms + EPS) * w_lanes


def _inproj_attn_kernel(x_ref, ab_ref, w_ref, nw_ref, cos_ref, sin_ref, y_ref):
    hb = _norm_mod(x_ref[...], ab_ref)
    cos = cos_ref[...]
    sin = sin_ref[...]
    tm = hb.shape[0]
    mask_a = (_lane((tm, LANES)) & 32) == 0
    acc_q = _dot(hb, w_ref[:, 0:8 * LANES])
    acc_k = _dot(hb, w_ref[:, 8 * LANES:10 * LANES])
    for j in range(10):
        acc = acc_q[:, j * LANES:(j + 1) * LANES] if j < 8 else acc_k[:, (j - 8) * LANES:(j - 7) * LANES]
        nw = nw_ref[0:1, :] if j < 8 else nw_ref[1:2, :]
        y_ref[:, j * LANES:(j + 1) * LANES] = _rope(_pair_rmsnorm(acc, mask_a, nw), cos, sin).astype(BF16)
    y_ref[:, 10 * LANES:12 * LANES] = _dot(hb, w_ref[:, 10 * LANES:12 * LANES]).astype(BF16)


def _inproj_attn(x2, ab, w, nw, cos, sin, tiles_per_batch):
    r, d = x2.shape
    tm = ROW_TILE
    n = w.shape[1]
    tpb = tiles_per_batch
    sel = lambda i: ((i // tpb) * 2 + jnp.minimum(i % tpb, 1), 0, 0)
    return pl.pallas_call(
        _inproj_attn_kernel,
        grid=(r // tm,),
        in_specs=[pl.BlockSpec((tm, d), lambda i: (i, 0)),
                  pl.BlockSpec((None, 2, d), sel),
                  _const_spec(w.shape), _const_spec(nw.shape),
                  pl.BlockSpec((tm, LANES), lambda i: (i % tpb, 0)),
                  pl.BlockSpec((tm, LANES), lambda i: (i % tpb, 0))],
        out_specs=pl.BlockSpec((tm, n), lambda i: (i, 0)),
        out_shape=jax.ShapeDtypeStruct((r, n), BF16),
        compiler_params=_cparams("arbitrary"),
        name="inproj_attn",
    )(x2, ab, w, nw, cos, sin)


def _conv_silu(cur_ref, prev_ref, next_ref, cw_ref, prev_on, next_on):
    cur = cur_ref[...].astype(F32)
    row = lax.broadcasted_iota(jnp.int32, cur.shape, 0)
    prev_row = prev_ref[HALO - 1:HALO, :].astype(F32) * prev_on
    next_row = next_ref[0:1, :].astype(F32) * next_on
    xm = jnp.where(row == 0, prev_row, pltpu.roll(cur, 1, 0))
    xp = jnp.where(row == CHUNK - 1, next_row, pltpu.roll(cur, CHUNK - 1, 0))
    y = cw_ref[3:4, :] + cw_ref[0:1, :] * xm + cw_ref[1:2, :] * cur + cw_ref[2:3, :] * xp
    return y * jax.nn.sigmoid(y)


def _segment_flags(c, nc, ctx_chunks):
    prev_on = jnp.where((c == 0) | (c == ctx_chunks), 0.0, 1.0).astype(F32)
    next_on = jnp.where((c == ctx_chunks - 1) | (c == nc - 1), 0.0, 1.0).astype(F32)
    return prev_on, next_on


def _cumsum_cols(tri_bf, lf):
    hi, mid, lo = _split3(lf)
    return _dot(tri_bf, hi) + _dot(tri_bf, mid) + _dot(tri_bf, lo)


def _cumsum_rows(lf, tri_bf):
    hi, mid, lo = _split3(lf)
    return _dot(hi, tri_bf) + _dot(mid, tri_bf) + _dot(lo, tri_bf)


def _ret_state_update(s_ref, p, k2, v2, kdec, cd_lanes, bd):
    kf = (k2.astype(F32) * kdec).astype(BF16)
    s_ref[p] = s_ref[p] * cd_lanes + jnp.where(bd, _dot_tn(v2, kf), 0.0)


def _mlstm_state_update(c_ref, n_ref, m_ref, k_pairs, v_pairs, c_all, bend, col0, lo, bd):
    cmax = jnp.max(c_all, axis=0, keepdims=True)
    w_all = jnp.exp(c_all - cmax)
    m_old = m_ref[0:1, :]
    mrel = jnp.maximum(m_old, cmax)
    a_row = jnp.exp(m_old - mrel)
    bb_row = jnp.exp(cmax - mrel)
    m_ref[0:1, :] = bend + mrel
    lo_row = lo[0:1, :]
    for p in range(4):
        h0 = col0 + 2 * p
        kw = k_pairs[p] * jnp.where(lo, w_all[:, h0:h0 + 1], w_all[:, h0 + 1:h0 + 2])
        kvt = _dot_tn(v_pairs[p], kw.astype(BF16))
        nloc = jnp.sum(kw, axis=0, keepdims=True)
        a_l = jnp.where(lo_row, a_row[:, h0:h0 + 1], a_row[:, h0 + 1:h0 + 2])
        bb_l = jnp.where(lo_row, bb_row[:, h0:h0 + 1], bb_row[:, h0 + 1:h0 + 2])
        c_ref[p] = c_ref[p] * a_l + jnp.where(bd, kvt, 0.0) * bb_l
        n_new = (n_ref[p, 0:1, :] + n_ref[p, 1:2, :]) * a_l + nloc * bb_l
        n_ref[p, 0:1, :] = jnp.where(lo_row, n_new, 0.0)
        n_ref[p, 1:2, :] = jnp.where(lo_row, 0.0, n_new)


def _bwd_state_kernel(rk_ref, rv_ref, mk_ref, mkp_ref, mkn_ref, mv_ref, g_ref,
                      lgk_ref, cw_ref,
                      sret_ref, cm_ref, nm_ref, mm_ref,
                      s_scr, c_scr, n_scr, m_scr, kdec_scr, *, nc, ctx_chunks):
    i = pl.program_id(1)
    c = jnp.where(i < ctx_chunks, ctx_chunks - 1 - i, nc + ctx_chunks - 1 - i)
    lane = _lane()
    sub = lax.broadcasted_iota(jnp.int32, (CHUNK, LANES), 0)
    lo = lane < HEAD_DIM
    bd_ret = (sub >= HEAD_DIM) == ((lane & 32) != 0)
    bd_m = (sub >= HEAD_DIM) == (lane >= HEAD_DIM)

    @pl.when(i == 0)
    def _():
        s_scr[...] = jnp.zeros_like(s_scr)
        c_scr[...] = jnp.zeros_like(c_scr)
        n_scr[...] = jnp.zeros_like(n_scr)
        m_scr[...] = jnp.zeros_like(m_scr)
        pos = sub.astype(F32)
        for p in range(4):
            kdec_scr[p] = jnp.exp(lgk_ref[4 + p:5 + p, :] * pos)

    sret_ref[...] = s_scr[...].astype(BF16)
    cm_ref[...] = c_scr[...].astype(BF16)
    nm_ref[...] = n_scr[...].astype(BF16)
    mm_ref[...] = m_scr[...]

    for p in range(4):
        cd = jnp.exp(lgk_ref[4 + p:5 + p, :] * float(CHUNK))
        _ret_state_update(s_scr, p, rk_ref[:, p * LANES:(p + 1) * LANES],
                          rv_ref[:, p * LANES:(p + 1) * LANES], kdec_scr[p], cd, bd_ret)

    prev_on, next_on = _segment_flags(c, nc, ctx_chunks)
    kc = _conv_silu(mk_ref, mkp_ref, mkn_ref, cw_ref, prev_on, next_on)
    g = g_ref[...]
    le_bf = (sub <= lane).astype(BF16)
    bal = pltpu.roll(_cumsum_cols(le_bf, _log_sigmoid(g)), LANES - 8, 1)
    _mlstm_state_update(c_scr, n_scr, m_scr,
                        [kc[:, p * LANES:(p + 1) * LANES] for p in range(4)],
                        [mv_ref[:, p * LANES:(p + 1) * LANES] for p in range(4)],
                        g - bal, bal[0:1, :], 16, lo, bd_m)


def _bwd_states(y, g, lgk, cw_k, nb, nc, ctx_chunks):
    gw = 4 * LANES
    def cidx(b, i):
        c = jnp.where(i < ctx_chunks, ctx_chunks - 1 - i, nc + ctx_chunks - 1 - i)
        return b * nc + c
    hb = CHUNK // HALO
    nhalo = y.shape[0] // HALO
    return pl.pallas_call(
        functools.partial(_bwd_state_kernel, nc=nc, ctx_chunks=ctx_chunks),
        grid=(nb, nc),
        in_specs=[pl.BlockSpec((CHUNK, gw), lambda b, i: (cidx(b, i), 1)),
                  pl.BlockSpec((CHUNK, gw), lambda b, i: (cidx(b, i), 2)),
                  pl.BlockSpec((CHUNK, gw), lambda b, i: (cidx(b, i), 5)),
                  pl.BlockSpec((HALO, gw), lambda b, i: (jnp.maximum(cidx(b, i) * hb - 1, 0), 5)),
                  pl.BlockSpec((HALO, gw), lambda b, i: (jnp.minimum((cidx(b, i) + 1) * hb, nhalo - 1), 5)),
                  pl.BlockSpec((CHUNK, gw), lambda b, i: (cidx(b, i), 6)),
                  pl.BlockSpec((CHUNK, LANES), lambda b, i: (cidx(b, i), 0)),
                  _const_spec(lgk.shape), _const_spec(cw_k.shape)],
        out_specs=[pl.BlockSpec((None, 4, CHUNK, LANES), lambda b, i: (cidx(b, i), 0, 0, 0)),
                   pl.BlockSpec((None, 4, CHUNK, LANES), lambda b, i: (cidx(b, i), 0, 0, 0)),
                   pl.BlockSpec((None, 4, HALO, LANES), lambda b, i: (cidx(b, i), 0, 0, 0)),
                   pl.BlockSpec((None, 8, LANES), lambda b, i: (cidx(b, i), 0, 0))],
        out_shape=[jax.ShapeDtypeStruct((nb * nc, 4, CHUNK, LANES), BF16),
                   jax.ShapeDtypeStruct((nb * nc, 4, CHUNK, LANES), BF16),
                   jax.ShapeDtypeStruct((nb * nc, 4, HALO, LANES), BF16),
                   jax.ShapeDtypeStruct((nb * nc, 8, LANES), F32)],
        scratch_shapes=[pltpu.VMEM((4, CHUNK, LANES), F32), pltpu.VMEM((4, CHUNK, LANES), F32),
                        pltpu.VMEM((4, HALO, LANES), F32), pltpu.VMEM((8, LANES), F32),
                        pltpu.VMEM((4, CHUNK, LANES), F32)],
        compiler_params=_cparams("arbitrary", "arbitrary"),
        name="bwd_state_sweep",
    )(y, y, y, y, y, y, g, lgk, cw_k)


def _mlstm_dir_weights(st, qn_row, c_col, bt_row, m_prev, tri):
    dl = jnp.where(tri, c_col + bt_row, NEG)
    mx = jnp.max(dl, axis=0, keepdims=True)
    al = bt_row + m_prev
    m_t = jnp.maximum(al, mx)
    w = jnp.exp(dl - m_t)
    a_t = jnp.exp(al - m_t)
    sw = st * w
    den = jnp.sum(sw, axis=0, keepdims=True) + a_t * qn_row
    r = 1.0 / jnp.maximum(jnp.abs(den), jnp.exp(-m_t))
    return sw * r, a_t * r


def _heads_out(ot, inter, nw_tab):
    ht = ot + inter
    rows = []
    for a in range(2):
        ha = ht[a * HEAD_DIM:(a + 1) * HEAD_DIM, :]
        ms = jnp.mean(ha * ha, axis=0, keepdims=True)
        rows.append(ha * lax.rsqrt(ms + EPS))
    return (jnp.concatenate(rows, axis=0) * nw_tab).T


def _mixer_kernel(lg_ref, rq_ref, rk_ref, rv_ref, rg_ref, mqk_ref, mqkp_ref, mqkn_ref, mv_ref, mo_ref,
                  g_ref, gt_ref, sretb_ref, cmb_ref, nmb_ref, mmb_ref,
                  lgk_ref, lgkt_ref, cw_ref, nw_ref,
                  out_ref,
                  s_scr, c_scr, n_scr, m_scr, dm_scr, dec_scr, *, nc, ctx_chunks):
    c = pl.program_id(1)
    lane = _lane()
    sub = lax.broadcasted_iota(jnp.int32, (CHUNK, LANES), 0)
    lo = lane < HEAD_DIM
    sub_lo = sub < HEAD_DIM
    mask_ret = [(lane & 32) == 0, (lane & 32) != 0]
    mask_nat = [lo, lane >= HEAD_DIM]
    bd_ret = (sub >= HEAD_DIM) == ((lane & 32) != 0)
    bd_m = (sub >= HEAD_DIM) == (lane >= HEAD_DIM)
    le = sub <= lane
    ge = sub >= lane

    @pl.when(c == 0)
    def _():
        s_scr[...] = jnp.zeros_like(s_scr)
        c_scr[...] = jnp.zeros_like(c_scr)
        n_scr[...] = jnp.zeros_like(n_scr)
        m_scr[...] = jnp.zeros_like(m_scr)
        spos = sub.astype(F32)
        tpos = lane.astype(F32)
        diff = (lane - sub).astype(F32)
        for h in range(8):
            dm_scr[h] = (jnp.where(le, jnp.exp(lg_ref[h] * diff), 0.0)
                         + jnp.where(ge, jnp.exp(lg_ref[8 + h] * (-diff)), 0.0))
        for p in range(4):
            dec_scr[0, p] = jnp.exp(lgkt_ref[:, p:p + 1] * (tpos + 1.0))
            dec_scr[1, p] = jnp.exp(lgkt_ref[:, 4 + p:5 + p] * (float(CHUNK) - tpos))
            dec_scr[2, p] = jnp.exp(lgk_ref[p:p + 1, :] * (float(CHUNK) - 1.0 - spos))

    for p in range(4):
        sl = slice(p * LANES, (p + 1) * LANES)
        q2, k2, v2 = rq_ref[:, sl], rk_ref[:, sl], rv_ref[:, sl]
        qt = q2.astype(F32).T
        vt = v2.astype(F32).T.astype(BF16)
        rows = []
        for a in range(2):
            qa = jnp.where(mask_ret[a], q2, jnp.zeros_like(q2))
            pt = (_dot_nt(k2, qa) * dm_scr[2 * p + a]).astype(BF16)
            rows.append(_dot(vt[a * HEAD_DIM:(a + 1) * HEAD_DIM, :], pt))
        rhs = jnp.concatenate([(qt * dec_scr[0, p]).astype(BF16), (qt * dec_scr[1, p]).astype(BF16)], axis=0)
        lhs = jnp.concatenate([s_scr[p].astype(BF16), sretb_ref[p]], axis=1)
        y = _heads_out(jnp.concatenate(rows, axis=0), _dot(lhs, rhs), nw_ref[p])
        out_ref[:, sl] = (y * rg_ref[:, sl].astype(F32)).astype(BF16)
        cd = jnp.exp(lgk_ref[p:p + 1, :] * float(CHUNK))
        _ret_state_update(s_scr, p, k2, v2, dec_scr[2, p], cd, bd_ret)

    prev_on, next_on = _segment_flags(c, nc, ctx_chunks)
    qk = _conv_silu(mqk_ref, mqkp_ref, mqkn_ref, cw_ref, prev_on, next_on)
    g = g_ref[...]
    gt = gt_ref[...]
    lf_col = _log_sigmoid(g)
    lf_row = _log_sigmoid(gt)
    le_bf = le.astype(BF16)
    ge_bf = ge.astype(BF16)
    bal_f = pltpu.roll(_cumsum_cols(ge_bf, lf_col), LANES - 8, 1)
    bal_b = pltpu.roll(_cumsum_cols(le_bf, lf_col), LANES - 8, 1)
    cf_all = g - bal_f
    cb_all = g - bal_b
    bf_row = _cumsum_rows(lf_row, le_bf)
    bb_row = _cumsum_rows(lf_row, ge_bf)
    k_pairs, v_pairs = [], []
    for p in range(4):
        sl = slice(p * LANES, (p + 1) * LANES)
        q2c = qk[:, sl] * (HEAD_DIM ** -0.5)
        k2c = qk[:, 4 * LANES + p * LANES:4 * LANES + (p + 1) * LANES]
        v2 = mv_ref[:, sl]
        q2b = q2c.astype(BF16)
        k2b = k2c.astype(BF16)
        qt = q2c.T
        vt = v2.astype(F32).T.astype(BF16)
        qn_f = _dot_nt(n_scr[p].astype(BF16), q2b)
        qn_b = _dot_nt(nmb_ref[p], q2b)
        rows, cf, cb = [], [], []
        for a in range(2):
            h = 2 * p + a
            qa = jnp.where(mask_nat[a], q2b, jnp.zeros_like(q2b))
            st = _dot_nt(k2b, qa)
            pf, coef_f = _mlstm_dir_weights(st, qn_f[a:a + 1, :], cf_all[:, h:h + 1],
                                            bf_row[8 + h:9 + h, :], m_scr[0:1, h:h + 1], le)
            pb, coef_b = _mlstm_dir_weights(st, qn_b[a:a + 1, :], cb_all[:, 16 + h:17 + h],
                                            bb_row[24 + h:25 + h, :], mmb_ref[0:1, 16 + h:17 + h], ge)
            rows.append(_dot(vt[a * HEAD_DIM:(a + 1) * HEAD_DIM, :], (pf + pb).astype(BF16)))
            cf.append(coef_f)
            cb.append(coef_b)
        rhs = jnp.concatenate([(qt * jnp.where(sub_lo, cf[0], cf[1])).astype(BF16),
                               (qt * jnp.where(sub_lo, cb[0], cb[1])).astype(BF16)], axis=0)
        lhs = jnp.concatenate([c_scr[p].astype(BF16), cmb_ref[p]], axis=1)
        y = _heads_out(jnp.concatenate(rows, axis=0), _dot(lhs, rhs), nw_ref[4 + p])
        out_ref[:, 4 * LANES + p * LANES:4 * LANES + (p + 1) * LANES] = (
            y * mo_ref[:, sl].astype(F32)).astype(BF16)
        k_pairs.append(k2c)
        v_pairs.append(v2)
    _mlstm_state_update(c_scr, n_scr, m_scr, k_pairs, v_pairs, cf_all, bal_f[CHUNK - 1:CHUNK, :], 0, lo, bd_m)


def _mixer(y, g, gt, states, lg_smem, lgk, lgkt, cw_qk, nw, nb, nc, ctx_chunks):
    gw = 4 * LANES
    sretb, cmb, nmb, mmb = states
    rc = lambda b, c: b * nc + c
    hb = CHUNK // HALO
    nhalo = y.shape[0] // HALO
    blk = lambda j: pl.BlockSpec((CHUNK, gw), lambda b, c: (rc(b, c), j))
    return pl.pallas_call(
        functools.partial(_mixer_kernel, nc=nc, ctx_chunks=ctx_chunks),
        grid=(nb, nc),
        in_specs=[_smem_spec(),
                  blk(0), blk(1), blk(2), blk(3),
                  pl.BlockSpec((CHUNK, 2 * gw), lambda b, c: (rc(b, c), 2)),
                  pl.BlockSpec((HALO, 2 * gw), lambda b, c: (jnp.maximum(rc(b, c) * hb - 1, 0), 2)),
                  pl.BlockSpec((HALO, 2 * gw), lambda b, c: (jnp.minimum((rc(b, c) + 1) * hb, nhalo - 1), 2)),
                  blk(6), blk(7),
                  pl.BlockSpec((CHUNK, LANES), lambda b, c: (rc(b, c), 0)),
                  pl.BlockSpec((32, CHUNK), lambda b, c: (0, rc(b, c))),
                  pl.BlockSpec((None, 4, CHUNK, LANES), lambda b, c: (rc(b, c), 0, 0, 0)),
                  pl.BlockSpec((None, 4, CHUNK, LANES), lambda b, c: (rc(b, c), 0, 0, 0)),
                  pl.BlockSpec((None, 4, HALO, LANES), lambda b, c: (rc(b, c), 0, 0, 0)),
                  pl.BlockSpec((None, 8, LANES), lambda b, c: (rc(b, c), 0, 0)),
                  _const_spec(lgk.shape), _const_spec(lgkt.shape), _const_spec(cw_qk.shape),
                  _const_spec(nw.shape)],
        out_specs=pl.BlockSpec((CHUNK, 2 * gw), lambda b, c: (rc(b, c), 0)),
        out_shape=jax.ShapeDtypeStruct((nb * nc * CHUNK, 2 * gw), BF16),
        scratch_shapes=[pltpu.VMEM((4, CHUNK, LANES), F32), pltpu.VMEM((4, CHUNK, LANES), F32),
                        pltpu.VMEM((4, HALO, LANES), F32), pltpu.VMEM((8, LANES), F32),
                        pltpu.VMEM((8, CHUNK, LANES), F32), pltpu.VMEM((3, 4, CHUNK, LANES), F32)],
        compiler_params=_cparams("arbitrary", "arbitrary"),
        name="ret_mlstm_mixer",
    )(lg_smem, y, y, y, y, y, y, y, y, y, g, gt, sretb, cmb, nmb, mmb, lgk, lgkt, cw_qk, nw)


def _attn_kernel(sink_ref, q_ref, kp_ref, kc_ref, kn_ref, kx_ref, vp_ref, vc_ref, vn_ref, vx_ref,
                 o_ref, *, nq, ctx_len):
    i = pl.program_id(1)
    nk = 3 * CHUNK + ctx_len
    rows = GQA_ROWS = 4 * CHUNK
    t = lax.broadcasted_iota(jnp.int32, (rows, nk), 0) & (CHUNK - 1)
    kk = lax.broadcasted_iota(jnp.int32, (rows, nk), 1)
    rel = kk - CHUNK - t
    valid = (jnp.abs(rel) <= CHUNK) & ((kk >= CHUNK) | (i > 0)) & ((kk < 2 * CHUNK) | (i < nq - 1))
    valid = valid | (kk >= 3 * CHUNK)
    grp = lax.broadcasted_iota(jnp.int32, (rows, 1), 0) // CHUNK
    lane = _lane()
    lo = lane < HEAD_DIM
    mask_q = [(lane & 32) == 0, (lane & 32) != 0]
    for kvp in range(2):
        sl = slice(kvp * LANES, (kvp + 1) * LANES)
        kcat = jnp.concatenate([kp_ref[:, sl], kc_ref[:, sl], kn_ref[:, sl], kx_ref[:, sl]], axis=0)
        vcat = jnp.concatenate([vp_ref[:, sl], vc_ref[:, sl], vn_ref[:, sl], vx_ref[:, sl]], axis=0)
        outs = []
        for a in range(2):
            kv = 2 * kvp + a
            qs = jnp.concatenate(
                [jnp.where(mask_q[a], q_ref[:, (kvp * 4 + g) * LANES:(kvp * 4 + g + 1) * LANES],
                           jnp.zeros((CHUNK, LANES), BF16)) for g in range(4)], axis=0)
            s = jnp.where(valid, _dot_nt(qs, kcat), NEG)
            snk = jnp.where(grp == 0, sink_ref[kv * 4],
                            jnp.where(grp == 1, sink_ref[kv * 4 + 1],
                                      jnp.where(grp == 2, sink_ref[kv * 4 + 2], sink_ref[kv * 4 + 3])))
            m = jnp.maximum(jnp.max(s, axis=1, keepdims=True), snk)
            e = jnp.exp(s - m)
            denom = jnp.exp(snk - m) + jnp.sum(e, axis=1, keepdims=True)
            outs.append(_dot(e.astype(BF16), vcat) * (1.0 / denom))
        for g in range(4):
            rs = slice(g * CHUNK, (g + 1) * CHUNK)
            o_ref[:, (kvp * 4 + g) * LANES:(kvp * 4 + g + 1) * LANES] = jnp.where(
                lo, outs[0][rs], outs[1][rs]).astype(BF16)


def _window_attn(y, sink, nb, nc, ctx_chunks):
    nq = nc - ctx_chunks
    ctx_len = ctx_chunks * CHUNK
    d = 8 * LANES
    kcol, vcol = d // (2 * LANES), d // (2 * LANES) + 1
    qrow = lambda b, i: b * nc + ctx_chunks + i
    prow = lambda b, i: b * nc + ctx_chunks + jnp.maximum(i - 1, 0)
    nrow = lambda b, i: b * nc + ctx_chunks + jnp.minimum(i + 1, nq - 1)
    xrow = lambda b, i: (b * nc * CHUNK) // ctx_len
    kv_spec = lambda rowf, col: pl.BlockSpec((CHUNK, 2 * LANES), lambda b, i: (rowf(b, i), col))
    x_spec = lambda col: pl.BlockSpec((ctx_len, 2 * LANES), lambda b, i: (xrow(b, i), col))
    return pl.pallas_call(
        functools.partial(_attn_kernel, nq=nq, ctx_len=ctx_len),
        grid=(nb, nq),
        in_specs=[_smem_spec(),
                  pl.BlockSpec((CHUNK, d), lambda b, i: (qrow(b, i), 0)),
                  kv_spec(prow, kcol), kv_spec(qrow, kcol), kv_spec(nrow, kcol), x_spec(kcol),
                  kv_spec(prow, vcol), kv_spec(qrow, vcol), kv_spec(nrow, vcol), x_spec(vcol)],
        out_specs=pl.BlockSpec((CHUNK, d), lambda b, i: (b * nq + i, 0)),
        out_shape=jax.ShapeDtypeStruct((nb * nq * CHUNK, d), BF16),
        compiler_params=_cparams("arbitrary", "arbitrary"),
        name="window_gqa",
    )(sink, y, y, y, y, y, y, y, y, y)


def _post_kernel(x_ref, m_ref, p_ref, wo_ref, wi_ref, w2_ref, o_ref, act_scr, *, d_ff, ff_chunk):
    x1 = x_ref[...] + p_ref[0:1, :] * _dot(m_ref[...], wo_ref[...])
    ms = jnp.mean(x1 * x1, axis=-1, keepdims=True)
    h = ((x1 * lax.rsqrt(ms + EPS)) * p_ref[1:2, :] + p_ref[2:3, :]).astype(BF16)
    for j in range(d_ff // ff_chunk):
        gate = _dot(h, wi_ref[:, j * ff_chunk:(j + 1) * ff_chunk])
        up = _dot(h, wi_ref[:, d_ff + j * ff_chunk:d_ff + (j + 1) * ff_chunk])
        act_scr[:, j * ff_chunk:(j + 1) * ff_chunk] = (gate * jax.nn.sigmoid(gate) * up).astype(BF16)
    o_ref[...] = x1 + p_ref[3:4, :] * _dot(act_scr[...], w2_ref[...])


def _post(x2, m2, prm, wo, wi, w2, tiles_per_batch, skip_tiles, m_has_ctx):
    r, d = x2.shape
    tm = ROW_TILE
    d_ff = w2.shape[0]
    tpb = tiles_per_batch
    kept = tpb - skip_tiles
    nb = r // (tm * tpb)
    xrow = lambda i: (i // kept) * tpb + skip_tiles + i % kept
    mrow = xrow if m_has_ctx else (lambda i: i)
    sel = lambda i: ((i // kept) * 2 + jnp.minimum(skip_tiles + i % kept, 1), 0, 0)
    ff_chunk = d_ff // 4 if (d_ff // 4) % LANES == 0 else d_ff
    return pl.pallas_call(
        functools.partial(_post_kernel, d_ff=d_ff, ff_chunk=ff_chunk),
        grid=(nb * kept,),
        in_specs=[pl.BlockSpec((tm, d), lambda i: (xrow(i), 0)),
                  pl.BlockSpec((tm, d), lambda i: (mrow(i), 0)),
                  pl.BlockSpec((None, 8, d), sel),
                  _const_spec(wo.shape), _const_spec(wi.shape), _const_spec(w2.shape)],
        out_specs=pl.BlockSpec((tm, d), lambda i: (i, 0)),
        out_shape=jax.ShapeDtypeStruct((nb * kept * tm, d), F32),
        scratch_shapes=[pltpu.VMEM((tm, d_ff), BF16)],
        compiler_params=_cparams("arbitrary"),
        name="outproj_swiglu",
    )(x2, m2, prm, wo, wi, w2)


def _pair_perm(n_heads):
    idx = []
    for p in range(n_heads // 2):
        for l in range(LANES):
            grp, r = divmod(l, 32)
            idx.append((2 * p + grp % 2) * HEAD_DIM + (grp // 2) * 32 + r)
    return np.asarray(idx, np.int32)


def _attn_q_perm():
    g_per = 4
    idx = []
    for kvp in range(H_KV // 2):
        for g in range(g_per):
            heads = ((2 * kvp) * g_per + g, (2 * kvp + 1) * g_per + g)
            for l in range(LANES):
                grp, r = divmod(l, 32)
                idx.append(heads[grp % 2] * HEAD_DIM + (grp // 2) * 32 + r)
    return np.asarray(idx, np.int32)


def _attn_o_perm():
    g_per = 4
    idx = []
    for kvp in range(H_KV // 2):
        for g in range(g_per):
            for a in range(2):
                hq = (2 * kvp + a) * g_per + g
                idx.extend(range(hq * HEAD_DIM, (hq + 1) * HEAD_DIM))
    return np.asarray(idx, np.int32)


def _rope_tables(seq, ctx_len):
    rows = seq // GRID_W
    row = jnp.repeat(jnp.arange(rows, dtype=F32), GRID_W)
    col = jnp.tile(jnp.arange(GRID_W, dtype=F32), rows)
    n = HEAD_DIM // 4
    inv = ROPE_BASE ** (-jnp.arange(n, dtype=F32) / n)
    ang = jnp.concatenate([row[:, None] * inv, col[:, None] * inv], axis=-1)
    cos, sin = jnp.cos(ang), jnp.sin(ang)
    cos_t = jnp.concatenate([jnp.ones((ctx_len, LANES), F32), jnp.tile(cos, (1, 4))], axis=0)
    sin_t = jnp.concatenate([jnp.zeros((ctx_len, LANES), F32),
                             jnp.concatenate([-sin, -sin, sin, sin], axis=-1)], axis=0)
    return cos_t, sin_t


def _mod_tables(mod, nb, norm_w):
    d = norm_w.shape[-1]
    lat = mod[:nb].reshape(nb, 6, d)
    ctx = jnp.broadcast_to(mod[nb].reshape(1, 6, d), (nb, 6, d))
    both = jnp.stack([ctx, lat], axis=1).reshape(nb * 2, 6, d)
    sh1, sc1, g1, sh2, sc2, g2 = [both[:, k] for k in range(6)]
    ab1 = jnp.stack([norm_w[0] * (1.0 + sc1), sh1], axis=1)
    zeros = jnp.zeros_like(g1)
    prm = jnp.stack([g1, norm_w[1] * (1.0 + sc2), sh2, g2, zeros, zeros, zeros, zeros], axis=1)
    return ab1, prm


def kernel(x, c, ctx, c_ctx, ada_w, ada_b, norm_w, ffn_w_in, ffn_w_out, ab_w_in, ab_w_out,
           ret_log_gamma, ret_norm_w, mlstm_conv_w, mlstm_conv_b, mlstm_gate_b, mlstm_norm_w,
           attn_w_in, attn_w_out, attn_q_norm_w, attn_k_norm_w, attn_sink):
    nb, seq, d = x.shape
    ctx_len = ctx.shape[1]
    depth = ada_w.shape[0]
    assert ctx_len == ROW_TILE and seq % ROW_TILE == 0 and d == 8 * LANES and nb < 8
    t_all = ctx_len + seq
    nc = t_all // CHUNK
    ctx_chunks = ctx_len // CHUNK
    tpb = t_all // ROW_TILE
    dr = d // 2

    rows = jnp.zeros((8, d), F32).at[:nb].set(c).at[nb].set(c_ctx)
    mod_all = _modulation(rows, ada_w, ada_b)
    cos_t, sin_t = _rope_tables(seq, ctx_len)
    xc = jnp.concatenate([ctx, x], axis=1).reshape(nb * t_all, d)

    out = None
    for layer in range(depth):
        last = layer == depth - 1
        ab1, prm = _mod_tables(mod_all[layer], nb, norm_w[layer])
        wi = ffn_w_in[layer].astype(BF16)
        w2 = ffn_w_out[layer].astype(BF16)
        if layer % 2 == 0:
            e = layer // 2
            w = ab_w_in[e]
            perm = _pair_perm(dr // HEAD_DIM)
            w_main = jnp.concatenate([w[:, :dr][:, perm], w[:, dr:2 * dr][:, perm], w[:, 2 * dr:8 * dr]],
                                     axis=1).astype(BF16)
            wg = jnp.zeros((d, LANES), F32).at[:, :32].set(w[:, 8 * dr:]).astype(BF16)
            wgt = w[:, 8 * dr:].T.astype(BF16)
            gb = jnp.zeros((1, LANES), F32).at[0, :32].set(mlstm_gate_b[e].reshape(-1))
            gbt = mlstm_gate_b[e].reshape(32, 1)
            y, g, gt = _inproj_ab(xc, ab1, w_main, wg, wgt, gb, gbt, cos_t, sin_t, tpb)

            lg = ret_log_gamma[e].astype(F32)
            lgk = jnp.stack([lg[dd, (_pair_perm(8)[p * LANES:(p + 1) * LANES] // HEAD_DIM)]
                             for dd in range(2) for p in range(4)])
            cw = jnp.concatenate([mlstm_conv_w[e], mlstm_conv_b[e][None],
                                  jnp.zeros((4, 2 * dr), F32)], axis=0)
            nw = jnp.broadcast_to(jnp.concatenate([ret_norm_w[e], mlstm_norm_w[e]]).reshape(8, LANES, 1),
                                  (8, LANES, LANES))
            lg_s = lg.reshape(-1)
            states = _bwd_states(y, g, lgk, cw[:, dr:], nb, nc, ctx_chunks)
            merged = _mixer(y, g, gt, states, lg_s, lgk, lgk.T, cw, nw, nb, nc, ctx_chunks)
            wo = ab_w_out[e].astype(BF16)
            m_has_ctx = True
        else:
            o = layer // 2
            w = attn_w_in[o]
            qp, kp = _attn_q_perm(), _pair_perm(H_KV)
            w_main = jnp.concatenate([w[:, :d][:, qp], w[:, d:d + 2 * LANES][:, kp],
                                      w[:, d + 2 * LANES:]], axis=1).astype(BF16)
            lane_w = lambda v: jnp.concatenate([v[:32], v[:32], v[32:], v[32:]])
            nwq = jnp.stack([lane_w(attn_q_norm_w[o]) * (HEAD_DIM ** -0.5), lane_w(attn_k_norm_w[o])]
                            + [jnp.zeros((LANES,), F32)] * 6)
            y = _inproj_attn(xc, ab1, w_main, nwq, cos_t, sin_t, tpb)
            merged = _window_attn(y, attn_sink[o].astype(F32), nb, nc, ctx_chunks)
            wo = attn_w_out[o][_attn_o_perm()].astype(BF16)
            m_has_ctx = False
            if not last:
                raise NotImplementedError("attention layers are only supported as the final layer")
        skip = tpb - seq // ROW_TILE if last else 0
        xc = _post(xc, merged, prm, wo, wi, w2, tpb, skip, m_has_ctx)
        out = xc
    return out.reshape(nb, seq, d)
```

```python
import functools

import numpy as np
import jax
import jax.numpy as jnp
from jax import lax
from jax.experimental import pallas as pl
from jax.experimental.pallas import tpu as pltpu

F32 = jnp.float32
BF16 = jnp.bfloat16

HEAD_DIM = 64
CHUNK = 128
GRID_W = 64
ROPE_BASE = 10000.0
EPS = 1e-6
H_KV = 4
LANES = 128
ROW_TILE = 256
HALO = 16
NEG = -1e30
VMEM_LIMIT = 56 * 1024 * 1024


def _cparams(*sem):
    return pltpu.CompilerParams(dimension_semantics=sem, vmem_limit_bytes=VMEM_LIMIT)


def _const_spec(shape):
    nd = len(shape)
    return pl.BlockSpec(shape, lambda *_: (0,) * nd, pipeline_mode=pl.Buffered(1))


def _smem_spec():
    return pl.BlockSpec(memory_space=pltpu.SMEM)


def _lane(shape=(CHUNK, LANES)):
    return lax.broadcasted_iota(jnp.int32, shape, len(shape) - 1)


def _dot(a, b):
    return jnp.dot(a, b, preferred_element_type=F32)


def _dot_nt(a, b):
    return lax.dot_general(a, b, (((1,), (1,)), ((), ())), preferred_element_type=F32)


def _dot_tn(a, b):
    return lax.dot_general(a, b, (((0,), (0,)), ((), ())), preferred_element_type=F32)


def _split3(x):
    hi = x.astype(BF16)
    r = x - hi.astype(F32)
    mid = r.astype(BF16)
    lo = (r - mid.astype(F32)).astype(BF16)
    return hi, mid, lo


def _log_sigmoid(x):
    return jnp.minimum(x, 0.0) - jnp.log1p(jnp.exp(-jnp.abs(x)))


def _rope(x, cos, sin_signed):
    return x * cos + pltpu.roll(x, LANES // 2, 1) * sin_signed


def _mod_kernel(rows_ref, w_ref, b_ref, o_ref):
    a = rows_ref[...]
    a = a * jax.nn.sigmoid(a)
    a_hi = a.astype(BF16)
    a_lo = (a - a_hi.astype(F32)).astype(BF16)
    w = w_ref[...]
    w_hi = w.astype(BF16)
    w_lo = (w - w_hi.astype(F32)).astype(BF16)
    o_ref[...] = _dot(a_hi, w_hi) + _dot(a_hi, w_lo) + _dot(a_lo, w_hi) + b_ref[...]


def _modulation(rows, ada_w, ada_b):
    depth, d, n = ada_w.shape
    tn = n // 4
    return pl.pallas_call(
        _mod_kernel,
        grid=(depth, n // tn),
        in_specs=[pl.BlockSpec((8, d), lambda l, j: (0, 0)),
                  pl.BlockSpec((None, d, tn), lambda l, j: (l, 0, j)),
                  pl.BlockSpec((None, 1, tn), lambda l, j: (l, 0, j))],
        out_specs=pl.BlockSpec((None, 8, tn), lambda l, j: (l, 0, j)),
        out_shape=jax.ShapeDtypeStruct((depth, 8, n), F32),
        compiler_params=_cparams("arbitrary", "arbitrary"),
        name="adaln_modulation",
    )(rows, ada_w, ada_b.reshape(depth, 1, n))


def _norm_mod(x, ab_ref):
    ms = jnp.mean(x * x, axis=-1, keepdims=True)
    h = (x * lax.rsqrt(ms + EPS)) * ab_ref[0:1, :] + ab_ref[1:2, :]
    return h.astype(BF16)


def _inproj_ab_kernel(x_ref, ab_ref, w_ref, wg_ref, wgt_ref, gb_ref, gbt_ref, cos_ref, sin_ref,
                      y_ref, g_ref, gt_ref):
    hb = _norm_mod(x_ref[...], ab_ref)
    cos = cos_ref[...]
    sin = sin_ref[...]
    gw = 4 * LANES
    for j in range(8):
        acc = _dot(hb, w_ref[:, j * gw:(j + 1) * gw])
        if j in (0, 1):
            if j == 0:
                acc = acc * (HEAD_DIM ** -0.5)
            acc = jnp.concatenate(
                [_rope(acc[:, p * LANES:(p + 1) * LANES], cos, sin) for p in range(4)], axis=1)
        elif j == 3:
            acc = acc * jax.nn.sigmoid(acc)
        elif j == 7:
            acc = jax.nn.sigmoid(acc)
        y_ref[:, j * gw:(j + 1) * gw] = acc.astype(BF16)
    g_ref[...] = _dot(hb, wg_ref[...]) + gb_ref[...]
    gt_ref[...] = _dot_nt(wgt_ref[...], hb) + gbt_ref[...]


def _inproj_ab(x2, ab, w, wg, wgt, gb, gbt, cos, sin, tiles_per_batch):
    r, d = x2.shape
    tm = ROW_TILE
    n = w.shape[1]
    tpb = tiles_per_batch
    sel = lambda i: ((i // tpb) * 2 + jnp.minimum(i % tpb, 1), 0, 0)
    return pl.pallas_call(
        _inproj_ab_kernel,
        grid=(r // tm,),
        in_specs=[pl.BlockSpec((tm, d), lambda i: (i, 0)),
                  pl.BlockSpec((None, 2, d), sel),
                  _const_spec(w.shape), _const_spec(wg.shape), _const_spec(wgt.shape),
                  _const_spec(gb.shape), _const_spec(gbt.shape),
                  pl.BlockSpec((tm, LANES), lambda i: (i % tpb, 0)),
                  pl.BlockSpec((tm, LANES), lambda i: (i % tpb, 0))],
        out_specs=[pl.BlockSpec((tm, n), lambda i: (i, 0)),
                   pl.BlockSpec((tm, LANES), lambda i: (i, 0)),
                   pl.BlockSpec((32, tm), lambda i: (0, i))],
        out_shape=[jax.ShapeDtypeStruct((r, n), BF16),
                   jax.ShapeDtypeStruct((r, LANES), F32),
                   jax.ShapeDtypeStruct((32, r), F32)],
        compiler_params=_cparams("arbitrary"),
        name="inproj_ret_mlstm",
    )(x2, ab, w, wg, wgt, gb, gbt, cos, sin)


def _pair_rmsnorm(a, mask_a, w_lanes):
    sq = a * a
    s_a = jnp.sum(jnp.where(mask_a, sq, 0.0), axis=-1, keepdims=True)
    s_b = jnp.sum(jnp.where(mask_a, 0.0, sq), axis=-1, keepdims=True)
    ms = jnp.where(mask_a, s_a, s_b) * (1.0 / HEAD_DIM)
    return a * lax.rsqrt(ms + EPS) * w_lanes


def _inproj_attn_kernel(x_ref, ab_ref, w_ref, nw_ref, cos_ref, sin_ref, y_ref):
    hb = _norm_mod(x_ref[...], ab_ref)
    cos = cos_ref[...]
    sin = sin_ref[...]
    tm = hb.shape[0]
    mask_a = (_lane((tm, LANES)) & 32) == 0
    acc_q = _dot(hb, w_ref[:, 0:8 * LANES])
    acc_k = _dot(hb, w_ref[:, 8 * LANES:10 * LANES])
    for j in range(10):
        acc = acc_q[:, j * LANES:(j + 1) * LANES] if j < 8 else acc_k[:, (j - 8) * LANES:(j - 7) * LANES]
        nw = nw_ref[0:1, :] if j < 8 else nw_ref[1:2, :]
        y_ref[:, j * LANES:(j + 1) * LANES] = _rope(_pair_rmsnorm(acc, mask_a, nw), cos, sin).astype(BF16)
    y_ref[:, 10 * LANES:12 * LANES] = _dot(hb, w_ref[:, 10 * LANES:12 * LANES]).astype(BF16)


def _inproj_attn(x2, ab, w, nw, cos, sin, tiles_per_batch):
    r, d = x2.shape
    tm = ROW_TILE
    n = w.shape[1]
    tpb = tiles_per_batch
    sel = lambda i: ((i // tpb) * 2 + jnp.minimum(i % tpb, 1), 0, 0)
    return pl.pallas_call(
        _inproj_attn_kernel,
        grid=(r // tm,),
        in_specs=[pl.BlockSpec((tm, d), lambda i: (i, 0)),
                  pl.BlockSpec((None, 2, d), sel),
                  _const_spec(w.shape), _const_spec(nw.shape),
                  pl.BlockSpec((tm, LANES), lambda i: (i % tpb, 0)),
                  pl.BlockSpec((tm, LANES), lambda i: (i % tpb, 0))],
        out_specs=pl.BlockSpec((tm, n), lambda i: (i, 0)),
        out_shape=jax.ShapeDtypeStruct((r, n), BF16),
        compiler_params=_cparams("arbitrary"),
        name="inproj_attn",
    )(x2, ab, w, nw, cos, sin)


def _conv_silu(cur_ref, prev_ref, next_ref, cw_ref, prev_on, next_on):
    cur = cur_ref[...].astype(F32)
    row = lax.broadcasted_iota(jnp.int32, cur.shape, 0)
    prev_row = prev_ref[HALO - 1:HALO, :].astype(F32) * prev_on
    next_row = next_ref[0:1, :].astype(F32) * next_on
    xm = jnp.where(row == 0, prev_row, pltpu.roll(cur, 1, 0))
    xp = jnp.where(row == CHUNK - 1, next_row, pltpu.roll(cur, CHUNK - 1, 0))
    y = cw_ref[3:4, :] + cw_ref[0:1, :] * xm + cw_ref[1:2, :] * cur + cw_ref[2:3, :] * xp
    return y * jax.nn.sigmoid(y)


def _segment_flags(c, nc, ctx_chunks):
    prev_on = jnp.where((c == 0) | (c == ctx_chunks), 0.0, 1.0).astype(F32)
    next_on = jnp.where((c == ctx_chunks - 1) | (c == nc - 1), 0.0, 1.0).astype(F32)
    return prev_on, next_on


def _cumsum_cols(tri_bf, lf):
    hi, mid, lo = _split3(lf)
    return _dot(tri_bf, hi) + _dot(tri_bf, mid) + _dot(tri_bf, lo)


def _cumsum_rows(lf, tri_bf):
    hi, mid, lo = _split3(lf)
    return _dot(hi, tri_bf) + _dot(mid, tri_bf) + _dot(lo, tri_bf)


def _ret_state_update(s_ref, p, k2, v2, kdec, cd_lanes, bd):
    kf = (k2.astype(F32) * kdec).astype(BF16)
    s_ref[p] = s_ref[p] * cd_lanes + jnp.where(bd, _dot_tn(v2, kf), 0.0)


def _mlstm_state_update(c_ref, n_ref, m_ref, k_pairs, v_pairs, c_all, bend, col0, lo, bd):
    cmax = jnp.max(c_all, axis=0, keepdims=True)
    w_all = jnp.exp(c_all - cmax)
    m_old = m_ref[0:1, :]
    mrel = jnp.maximum(m_old, cmax)
    a_row = jnp.exp(m_old - mrel)
    bb_row = jnp.exp(cmax - mrel)
    m_ref[0:1, :] = bend + mrel
    lo_row = lo[0:1, :]
    for p in range(4):
        h0 = col0 + 2 * p
        kw = k_pairs[p] * jnp.where(lo, w_all[:, h0:h0 + 1], w_all[:, h0 + 1:h0 + 2])
        kvt = _dot_tn(v_pairs[p], kw.astype(BF16))
        nloc = jnp.sum(kw, axis=0, keepdims=True)
        a_l = jnp.where(lo_row, a_row[:, h0:h0 + 1], a_row[:, h0 + 1:h0 + 2])
        bb_l = jnp.where(lo_row, bb_row[:, h0:h0 + 1], bb_row[:, h0 + 1:h0 + 2])
        c_ref[p] = c_ref[p] * a_l + jnp.where(bd, kvt, 0.0) * bb_l
        n_new = (n_ref[p, 0:1, :] + n_ref[p, 1:2, :]) * a_l + nloc * bb_l
        n_ref[p, 0:1, :] = jnp.where(lo_row, n_new, 0.0)
        n_ref[p, 1:2, :] = jnp.where(lo_row, 0.0, n_new)


def _bwd_state_kernel(rk_ref, rv_ref, mk_ref, mkp_ref, mkn_ref, mv_ref, g_ref,
                      lgk_ref, cw_ref,
                      sret_ref, cm_ref, nm_ref, mm_ref,
                      s_scr, c_scr, n_scr, m_scr, kdec_scr, *, nc, ctx_chunks):
    i = pl.program_id(1)
    c = jnp.where(i < ctx_chunks, ctx_chunks - 1 - i, nc + ctx_chunks - 1 - i)
    lane = _lane()
    sub = lax.broadcasted_iota(jnp.int32, (CHUNK, LANES), 0)
    lo = lane < HEAD_DIM
    bd_ret = (sub >= HEAD_DIM) == ((lane & 32) != 0)
    bd_m = (sub >= HEAD_DIM) == (lane >= HEAD_DIM)

    @pl.when(i == 0)
    def _():
        s_scr[...] = jnp.zeros_like(s_scr)
        c_scr[...] = jnp.zeros_like(c_scr)
        n_scr[...] = jnp.zeros_like(n_scr)
        m_scr[...] = jnp.zeros_like(m_scr)
        pos = sub.astype(F32)
        for p in range(4):
            kdec_scr[p] = jnp.exp(lgk_ref[4 + p:5 + p, :] * pos)

    sret_ref[...] = s_scr[...].astype(BF16)
    cm_ref[...] = c_scr[...].astype(BF16)
    nm_ref[...] = n_scr[...].astype(BF16)
    mm_ref[...] = m_scr[...]

    for p in range(4):
        cd = jnp.exp(lgk_ref[4 + p:5 + p, :] * float(CHUNK))
        _ret_state_update(s_scr, p, rk_ref[:, p * LANES:(p + 1) * LANES],
                          rv_ref[:, p * LANES:(p + 1) * LANES], kdec_scr[p], cd, bd_ret)

    prev_on, next_on = _segment_flags(c, nc, ctx_chunks)
    kc = _conv_silu(mk_ref, mkp_ref, mkn_ref, cw_ref, prev_on, next_on)
    g = g_ref[...]
    le_bf = (sub <= lane).astype(BF16)
    bal = pltpu.roll(_cumsum_cols(le_bf, _log_sigmoid(g)), LANES - 8, 1)
    _mlstm_state_update(c_scr, n_scr, m_scr,
                        [kc[:, p * LANES:(p + 1) * LANES] for p in range(4)],
                        [mv_ref[:, p * LANES:(p + 1) * LANES] for p in range(4)],
                        g - bal, bal[0:1, :], 16, lo, bd_m)


def _bwd_states(y, g, lgk, cw_k, nb, nc, ctx_chunks):
    gw = 4 * LANES
    def cidx(b, i):
        c = jnp.where(i < ctx_chunks, ctx_chunks - 1 - i, nc + ctx_chunks - 1 - i)
        return b * nc + c
    hb = CHUNK // HALO
    nhalo = y.shape[0] // HALO
    return pl.pallas_call(
        functools.partial(_bwd_state_kernel, nc=nc, ctx_chunks=ctx_chunks),
        grid=(nb, nc),
        in_specs=[pl.BlockSpec((CHUNK, gw), lambda b, i: (cidx(b, i), 1)),
                  pl.BlockSpec((CHUNK, gw), lambda b, i: (cidx(b, i), 2)),
                  pl.BlockSpec((CHUNK, gw), lambda b, i: (cidx(b, i), 5)),
                  pl.BlockSpec((HALO, gw), lambda b, i: (jnp.maximum(cidx(b, i) * hb - 1, 0), 5)),
                  pl.BlockSpec((HALO, gw), lambda b, i: (jnp.minimum((cidx(b, i) + 1) * hb, nhalo - 1), 5)),
                  pl.BlockSpec((CHUNK, gw), lambda b, i: (cidx(b, i), 6)),
                  pl.BlockSpec((CHUNK, LANES), lambda b, i: (cidx(b, i), 0)),
                  _const_spec(lgk.shape), _const_spec(cw_k.shape)],
        out_specs=[pl.BlockSpec((None, 4, CHUNK, LANES), lambda b, i: (cidx(b, i), 0, 0, 0)),
                   pl.BlockSpec((None, 4, CHUNK, LANES), lambda b, i: (cidx(b, i), 0, 0, 0)),
                   pl.BlockSpec((None, 4, HALO, LANES), lambda b, i: (cidx(b, i), 0, 0, 0)),
                   pl.BlockSpec((None, 8, LANES), lambda b, i: (cidx(b, i), 0, 0))],
        out_shape=[jax.ShapeDtypeStruct((nb * nc, 4, CHUNK, LANES), BF16),
                   jax.ShapeDtypeStruct((nb * nc, 4, CHUNK, LANES), BF16),
                   jax.ShapeDtypeStruct((nb * nc, 4, HALO, LANES), BF16),
                   jax.ShapeDtypeStruct((nb * nc, 8, LANES), F32)],
        scratch_shapes=[pltpu.VMEM((4, CHUNK, LANES), F32), pltpu.VMEM((4, CHUNK, LANES), F32),
                        pltpu.VMEM((4, HALO, LANES), F32), pltpu.VMEM((8, LANES), F32),
                        pltpu.VMEM((4, CHUNK, LANES), F32)],
        compiler_params=_cparams("arbitrary", "arbitrary"),
        name="bwd_state_sweep",
    )(y, y, y, y, y, y, g, lgk, cw_k)


def _mlstm_dir_weights(st, qn_row, c_col, bt_row, m_prev, tri):
    dl = jnp.where(tri, c_col + bt_row, NEG)
    mx = jnp.max(dl, axis=0, keepdims=True)
    al = bt_row + m_prev
    m_t = jnp.maximum(al, mx)
    w = jnp.exp(dl - m_t)
    a_t = jnp.exp(al - m_t)
    sw = st * w
    den = jnp.sum(sw, axis=0, keepdims=True) + a_t * qn_row
    r = 1.0 / jnp.maximum(jnp.abs(den), jnp.exp(-m_t))
    return sw * r, a_t * r


def _heads_out(ht, nw_tab):
    rows = []
    for a in range(2):
        ha = ht[a * HEAD_DIM:(a + 1) * HEAD_DIM, :]
        ms = jnp.mean(ha * ha, axis=0, keepdims=True)
        rows.append(ha * lax.rsqrt(ms + EPS))
    return (jnp.concatenate(rows, axis=0) * nw_tab).T


def _mixer_kernel(lg_ref, rq_ref, rk_ref, rv_ref, rg_ref, mqk_ref, mqkp_ref, mqkn_ref, mv_ref, mo_ref,
                  g_ref, gt_ref, sretb_ref, cmb_ref, nmb_ref, mmb_ref,
                  lgk_ref, lgkt_ref, cw_ref, nw_ref,
                  out_ref,
                  s_scr, c_scr, n_scr, m_scr, dm_scr, dec_scr, *, nc, ctx_chunks):
    c = pl.program_id(1)
    lane = _lane()
    sub = lax.broadcasted_iota(jnp.int32, (CHUNK, LANES), 0)
    lo = lane < HEAD_DIM
    sub_lo = sub < HEAD_DIM
    mask_ret = [(lane & 32) == 0, (lane & 32) != 0]
    mask_nat = [lo, lane >= HEAD_DIM]
    bd_ret = (sub >= HEAD_DIM) == ((lane & 32) != 0)
    bd_m = (sub >= HEAD_DIM) == (lane >= HEAD_DIM)
    le = sub <= lane
    ge = sub >= lane

    @pl.when(c == 0)
    def _():
        s_scr[...] = jnp.zeros_like(s_scr)
        c_scr[...] = jnp.zeros_like(c_scr)
        n_scr[...] = jnp.zeros_like(n_scr)
        m_scr[...] = jnp.zeros_like(m_scr)
        spos = sub.astype(F32)
        tpos = lane.astype(F32)
        diff = (lane - sub).astype(F32)
        for h in range(8):
            dm_scr[h] = (jnp.where(le, jnp.exp(lg_ref[h] * diff), 0.0)
                         + jnp.where(ge, jnp.exp(lg_ref[8 + h] * (-diff)), 0.0))
        for p in range(4):
            dec_scr[0, p] = jnp.exp(lgkt_ref[:, p:p + 1] * (tpos + 1.0))
            dec_scr[1, p] = jnp.exp(lgkt_ref[:, 4 + p:5 + p] * (float(CHUNK) - tpos))
            dec_scr[2, p] = jnp.exp(lgk_ref[p:p + 1, :] * (float(CHUNK) - 1.0 - spos))


    prev_on, next_on = _segment_flags(c, nc, ctx_chunks)
    qk = _conv_silu(mqk_ref, mqkp_ref, mqkn_ref, cw_ref, prev_on, next_on)
    g = g_ref[...]
    gt = gt_ref[...]
    lf_col = _log_sigmoid(g)
    lf_row = _log_sigmoid(gt)
    le_bf = le.astype(BF16)
    ge_bf = ge.astype(BF16)
    bal_f = pltpu.roll(_cumsum_cols(ge_bf, lf_col), LANES - 8, 1)
    bal_b = pltpu.roll(_cumsum_cols(le_bf, lf_col), LANES - 8, 1)
    cf_all = g - bal_f
    cb_all = g - bal_b
    bf_row = _cumsum_rows(lf_row, le_bf)
    bb_row = _cumsum_rows(lf_row, ge_bf)

    qb, kb, vb, kf32, qt, vbd = [], [], [], [], [], []
    for p in range(8):
        sl = slice((p % 4) * LANES, (p % 4 + 1) * LANES)
        if p < 4:
            q2, k2, v2 = rq_ref[:, sl], rk_ref[:, sl], rv_ref[:, sl]
            qf = q2.astype(F32)
            kf = None
            old =jnp.concatenate([s_scr[p].astype(BF16), sretb_ref[p]], axis=1)
        else:
            qf = qk[:, sl] * (HEAD_DIM ** -0.5)
            kf = qk[:, 4 * LANES + (p - 4) * LANES:4 * LANES + (p - 3) * LANES]
            q2, k2, v2 = qf.astype(BF16), kf.astype(BF16), mv_ref[:, sl]
            old = jnp.concatenate([c_scr[p - 4].astype(BF16), cmb_ref[p - 4]], axis=1)
        vt = v2.astype(F32).T.astype(BF16)
        qb.append(q2)
        kb.append(k2)
        vb.append(v2)
        kf32.append(kf)
        qt.append(qf.T)
        vbd.append(jnp.concatenate([jnp.where(sub_lo, vt, jnp.zeros_like(vt)),
                                    jnp.where(sub_lo, jnp.zeros_like(vt), vt), old], axis=1))

    st2, qn_f, qn_b = [], [], []
    for p in range(8):
        masks = mask_ret if p < 4 else mask_nat
        zero = jnp.zeros_like(qb[p])
        qstack = jnp.concatenate([jnp.where(masks[0], qb[p], zero), jnp.where(masks[1], qb[p], zero)], axis=0)
        st2.append(_dot_nt(kb[p], qstack))
        if p >= 4:
            qn_f.append(_dot_nt(n_scr[p - 4].astype(BF16), qb[p]))
            qn_b.append(_dot_nt(nmb_ref[p - 4], qb[p]))

    rhs = []
    for p in range(8):
        if p < 4:
            pts = [(st2[p][:, a * LANES:(a + 1) * LANES] * dm_scr[2 * p + a]).astype(BF16) for a in range(2)]
            x_f, x_b = dec_scr[0, p], dec_scr[1, p]
        else:
            pts, cf, cb = [], [], []
            for a in range(2):
                h = 2 * (p - 4) + a
                st = st2[p][:, a * LANES:(a + 1) * LANES]
                pf, coef_f = _mlstm_dir_weights(st, qn_f[p - 4][a:a + 1, :], cf_all[:, h:h + 1],
                                                bf_row[8 + h:9 + h, :], m_scr[0:1, h:h + 1], le)
                pb, coef_b = _mlstm_dir_weights(st, qn_b[p - 4][a:a + 1, :], cb_all[:, 16 + h:17 + h],
                                                bb_row[24 + h:25 + h, :], mmb_ref[0:1, 16 + h:17 + h], ge)
                pts.append((pf + pb).astype(BF16))
                cf.append(coef_f)
                cb.append(coef_b)
            x_f = jnp.where(sub_lo, cf[0], cf[1])
            x_b = jnp.where(sub_lo, cb[0], cb[1])
        rhs.append(jnp.concatenate(pts + [(qt[p] * x_f).astype(BF16), (qt[p] * x_b).astype(BF16)], axis=0))

    ht = [_dot(vbd[p], rhs[p]) for p in range(8)]

    for p in range(8):
        sl = slice((p % 4) * LANES, (p % 4 + 1) * LANES)
        gate_ref = rg_ref if p < 4 else mo_ref
        y = _heads_out(ht[p], nw_ref[p])
        out_ref[:, p * LANES:(p + 1) * LANES] = (y * gate_ref[:, sl].astype(F32)).astype(BF16)

    for p in range(4):
        cd = jnp.exp(lgk_ref[p:p + 1, :] * float(CHUNK))
        _ret_state_update(s_scr, p, kb[p], vb[p], dec_scr[2, p], cd, bd_ret)
    _mlstm_state_update(c_scr, n_scr, m_scr, kf32[4:], vb[4:], cf_all, bal_f[CHUNK - 1:CHUNK, :], 0, lo, bd_m)


def _mixer(y, g, gt, states, lg_smem, lgk, lgkt, cw_qk, nw, nb, nc, ctx_chunks):
    gw = 4 * LANES
    sretb, cmb, nmb, mmb = states
    rc = lambda b, c: b * nc + c
    hb = CHUNK // HALO
    nhalo = y.shape[0] // HALO
    blk = lambda j: pl.BlockSpec((CHUNK, gw), lambda b, c: (rc(b, c), j))
    return pl.pallas_call(
        functools.partial(_mixer_kernel, nc=nc, ctx_chunks=ctx_chunks),
        grid=(nb, nc),
        in_specs=[_smem_spec(),
                  blk(0), blk(1), blk(2), blk(3),
                  pl.BlockSpec((CHUNK, 2 * gw), lambda b, c: (rc(b, c), 2)),
                  pl.BlockSpec((HALO, 2 * gw), lambda b, c: (jnp.maximum(rc(b, c) * hb - 1, 0), 2)),
                  pl.BlockSpec((HALO, 2 * gw), lambda b, c: (jnp.minimum((rc(b, c) + 1) * hb, nhalo - 1), 2)),
                  blk(6), blk(7),
                  pl.BlockSpec((CHUNK, LANES), lambda b, c: (rc(b, c), 0)),
                  pl.BlockSpec((32, CHUNK), lambda b, c: (0, rc(b, c))),
                  pl.BlockSpec((None, 4, CHUNK, LANES), lambda b, c: (rc(b, c), 0, 0, 0)),
                  pl.BlockSpec((None, 4, CHUNK, LANES), lambda b, c: (rc(b, c), 0, 0, 0)),
                  pl.BlockSpec((None, 4, HALO, LANES), lambda b, c: (rc(b, c), 0, 0, 0)),
                  pl.BlockSpec((None, 8, LANES), lambda b, c: (rc(b, c), 0, 0)),
                  _const_spec(lgk.shape), _const_spec(lgkt.shape), _const_spec(cw_qk.shape),
                  _const_spec(nw.shape)],
        out_specs=pl.BlockSpec((CHUNK, 2 * gw), lambda b, c: (rc(b, c), 0)),
        out_shape=jax.ShapeDtypeStruct((nb * nc * CHUNK, 2 * gw), BF16),
        scratch_shapes=[pltpu.VMEM((4, CHUNK, LANES), F32), pltpu.VMEM((4, CHUNK, LANES), F32),
                        pltpu.VMEM((4, HALO, LANES), F32), pltpu.VMEM((8, LANES), F32),
                        pltpu.VMEM((8, CHUNK, LANES), F32), pltpu.VMEM((3, 4, CHUNK, LANES), F32)],
        compiler_params=_cparams("arbitrary", "arbitrary"),
        name="ret_mlstm_mixer",
    )(lg_smem, y, y, y, y, y, y, y, y, y, g, gt, sretb, cmb, nmb, mmb, lgk, lgkt, cw_qk, nw)


def _attn_kernel(sink_ref, q_ref, kp_ref, kc_ref, kn_ref, kx_ref, vp_ref, vc_ref, vn_ref, vx_ref,
                 bias_ref, o_ref):
    grp = lax.broadcasted_iota(jnp.int32, (1, 4 * CHUNK), 1) // CHUNK
    lane = _lane()
    mask_q = [(lane & 32) == 0, (lane & 32) != 0]
    bias = jnp.concatenate([bias_ref[...]] * 4, axis=1)
    vts, sts = [], []
    for kvp in range(2):
        sl = slice(kvp * LANES, (kvp + 1) * LANES)
        v_blocks = [vp_ref[:, sl], vc_ref[:, sl], vn_ref[:, sl], vx_ref[:, sl]]
        vts.append(jnp.concatenate(
            [vb[r * CHUNK:(r + 1) * CHUNK, :].astype(F32).T.astype(BF16)
             for vb in v_blocks for r in range(vb.shape[0] // CHUNK)], axis=1))
    for kvp in range(2):
        sl = slice(kvp * LANES, (kvp + 1) * LANES)
        kcat = jnp.concatenate([kp_ref[:, sl], kc_ref[:, sl], kn_ref[:, sl], kx_ref[:, sl]], axis=0)
        for a in range(2):
            qs = jnp.concatenate(
                [jnp.where(mask_q[a], q_ref[:, (kvp * 4 + g) * LANES:(kvp * 4 + g + 1) * LANES],
                           jnp.zeros((CHUNK, LANES), BF16)) for g in range(4)], axis=0)
            sts.append(_dot_nt(kcat, qs))
    outs = []
    for kv in range(4):
        st = sts[kv]
        st = jnp.concatenate([st[0:CHUNK] + bias[0:CHUNK], st[CHUNK:2 * CHUNK],
                              st[2 * CHUNK:3 * CHUNK] + bias[CHUNK:2 * CHUNK], st[3 * CHUNK:]], axis=0)
        snk = jnp.where(grp == 0, sink_ref[kv * 4],
                        jnp.where(grp == 1, sink_ref[kv * 4 + 1],
                                  jnp.where(grp == 2, sink_ref[kv * 4 + 2], sink_ref[kv * 4 + 3])))
        m = jnp.maximum(jnp.max(st, axis=0, keepdims=True), snk)
        e = jnp.exp(st - m)
        denom = jnp.exp(snk - m) + jnp.sum(e, axis=0, keepdims=True)
        a = kv % 2
        outs.append(_dot(vts[kv // 2][a * HEAD_DIM:(a + 1) * HEAD_DIM, :], e.astype(BF16)) * (1.0 / denom))
    for kvp in range(2):
        full = jnp.concatenate(outs[2 * kvp:2 * kvp + 2], axis=0)
        for g in range(4):
            o_ref[:, (kvp * 4 + g) * LANES:(kvp * 4 + g + 1) * LANES] = (
                full[:, g * CHUNK:(g + 1) * CHUNK].T.astype(BF16))


def _window_bias():
    kk = np.arange(CHUNK)[:, None]
    t = np.arange(CHUNK)[None, :]
    tabs = []
    for has_prev, has_next in ((False, True), (True, True), (True, False)):
        prev_ok = (kk >= t) & has_prev
        next_ok = (kk <= t) & has_next
        tabs.append(np.where(np.concatenate([prev_ok, next_ok], axis=0), 0.0, NEG))
    return jnp.asarray(np.stack(tabs), F32)


def _window_attn(y, sink, nb, nc, ctx_chunks):
    nq = nc - ctx_chunks
    assert nq >= 2
    ctx_len = ctx_chunks * CHUNK
    bias = _window_bias()
    d = 8 * LANES
    kcol, vcol = d // (2 * LANES), d // (2 * LANES) + 1
    qrow = lambda b, i: b * nc + ctx_chunks + i
    prow = lambda b, i: b * nc + ctx_chunks + jnp.maximum(i - 1, 0)
    nrow = lambda b, i: b * nc + ctx_chunks + jnp.minimum(i + 1, nq - 1)
    xrow = lambda b, i: (b * nc * CHUNK) // ctx_len
    kv_spec = lambda rowf, col: pl.BlockSpec((CHUNK, 2 * LANES), lambda b, i: (rowf(b, i), col))
    x_spec = lambda col: pl.BlockSpec((ctx_len, 2 * LANES), lambda b, i: (xrow(b, i), col))
    bias_spec = pl.BlockSpec((None,) + bias.shape[1:],
                             lambda b, i: (jnp.where(i == 0, 0, jnp.where(i == nq - 1, 2, 1)), 0, 0))
    return pl.pallas_call(
        _attn_kernel,
        grid=(nb, nq),
        in_specs=[_smem_spec(),
                  pl.BlockSpec((CHUNK, d), lambda b, i: (qrow(b, i), 0)),
                  kv_spec(prow, kcol), kv_spec(qrow, kcol), kv_spec(nrow, kcol), x_spec(kcol),
                  kv_spec(prow, vcol), kv_spec(qrow, vcol), kv_spec(nrow, vcol), x_spec(vcol),
                  bias_spec],
        out_specs=pl.BlockSpec((CHUNK, d), lambda b, i: (b * nq + i, 0)),
        out_shape=jax.ShapeDtypeStruct((nb * nq * CHUNK, d), BF16),
        compiler_params=_cparams("arbitrary", "arbitrary"),
        name="window_gqa",
    )(sink, y, y, y, y, y, y, y, y, y, bias)


def _post_kernel(x_ref, m_ref, p_ref, wo_ref, wi_ref, w2_ref, o_ref, act_scr, *, d_ff, ff_chunk):
    x1 = x_ref[...] + p_ref[0:1, :] * _dot(m_ref[...], wo_ref[...])
    ms = jnp.mean(x1 * x1, axis=-1, keepdims=True)
    h = ((x1 * lax.rsqrt(ms + EPS)) * p_ref[1:2, :] + p_ref[2:3, :]).astype(BF16)
    for j in range(d_ff // ff_chunk):
        gate = _dot(h, wi_ref[:, j * ff_chunk:(j + 1) * ff_chunk])
        up = _dot(h, wi_ref[:, d_ff + j * ff_chunk:d_ff + (j + 1) * ff_chunk])
        act_scr[:, j * ff_chunk:(j + 1) * ff_chunk] = (gate * jax.nn.sigmoid(gate) * up).astype(BF16)
    o_ref[...] = x1 + p_ref[3:4, :] * _dot(act_scr[...], w2_ref[...])


def _post(x2, m2, prm, wo, wi, w2, tiles_per_batch, skip_tiles, m_has_ctx):
    r, d = x2.shape
    tm = ROW_TILE
    d_ff = w2.shape[0]
    tpb = tiles_per_batch
    kept = tpb - skip_tiles
    nb = r // (tm * tpb)
    xrow = lambda i: (i // kept) * tpb + skip_tiles + i % kept
    mrow = xrow if m_has_ctx else (lambda i: i)
    sel = lambda i: ((i // kept) * 2 + jnp.minimum(skip_tiles + i % kept, 1), 0, 0)
    ff_chunk = d_ff // 4 if (d_ff // 4) % LANES == 0 else d_ff
    return pl.pallas_call(
        functools.partial(_post_kernel, d_ff=d_ff, ff_chunk=ff_chunk),
        grid=(nb * kept,),
        in_specs=[pl.BlockSpec((tm, d), lambda i: (xrow(i), 0)),
                  pl.BlockSpec((tm, d), lambda i: (mrow(i), 0)),
                  pl.BlockSpec((None, 8, d), sel),
                  _const_spec(wo.shape), _const_spec(wi.shape), _const_spec(w2.shape)],
        out_specs=pl.BlockSpec((tm, d), lambda i: (i, 0)),
        out_shape=jax.ShapeDtypeStruct((nb * kept * tm, d), F32),
        scratch_shapes=[pltpu.VMEM((tm, d_ff), BF16)],
        compiler_params=_cparams("arbitrary"),
        name="outproj_swiglu",
    )(x2, m2, prm, wo, wi, w2)


def _pair_perm(n_heads):
    idx = []
    for p in range(n_heads // 2):
        for l in range(LANES):
            grp, r = divmod(l, 32)
            idx.append((2 * p + grp % 2) * HEAD_DIM + (grp // 2) * 32 + r)
    return np.asarray(idx, np.int32)


def _attn_q_perm():
    g_per = 4
    idx = []
    for kvp in range(H_KV // 2):
        for g in range(g_per):
            heads = ((2 * kvp) * g_per + g, (2 * kvp + 1) * g_per + g)
            for l in range(LANES):
                grp, r = divmod(l, 32)
                idx.append(heads[grp % 2] * HEAD_DIM + (grp // 2) * 32 + r)
    return np.asarray(idx, np.int32)


def _attn_o_perm():
    g_per = 4
    idx = []
    for kvp in range(H_KV // 2):
        for g in range(g_per):
            for a in range(2):
                hq = (2 * kvp + a) * g_per + g
                idx.extend(range(hq * HEAD_DIM, (hq + 1) * HEAD_DIM))
    return np.asarray(idx, np.int32)


def _rope_tables(seq, ctx_len):
    rows = seq // GRID_W
    row = jnp.repeat(jnp.arange(rows, dtype=F32), GRID_W)
    col = jnp.tile(jnp.arange(GRID_W, dtype=F32), rows)
    n = HEAD_DIM // 4
    inv = ROPE_BASE ** (-jnp.arange(n, dtype=F32) / n)
    ang = jnp.concatenate([row[:, None] * inv, col[:, None] * inv], axis=-1)
    cos, sin = jnp.cos(ang), jnp.sin(ang)
    cos_t = jnp.concatenate([jnp.ones((ctx_len, LANES), F32), jnp.tile(cos, (1, 4))], axis=0)
    sin_t = jnp.concatenate([jnp.zeros((ctx_len, LANES), F32),
                             jnp.concatenate([-sin, -sin, sin, sin], axis=-1)], axis=0)
    return cos_t, sin_t


def _mod_tables(mod, nb, norm_w):
    d = norm_w.shape[-1]
    lat = mod[:nb].reshape(nb, 6, d)
    ctx = jnp.broadcast_to(mod[nb].reshape(1, 6, d), (nb, 6, d))
    both = jnp.stack([ctx, lat], axis=1).reshape(nb * 2, 6, d)
    sh1, sc1, g1, sh2, sc2, g2 = [both[:, k] for k in range(6)]
    ab1 = jnp.stack([norm_w[0] * (1.0 + sc1), sh1], axis=1)
    zeros = jnp.zeros_like(g1)
    prm = jnp.stack([g1, norm_w[1] * (1.0 + sc2), sh2, g2, zeros, zeros, zeros, zeros], axis=1)
    return ab1, prm


def kernel(x, c, ctx, c_ctx, ada_w, ada_b, norm_w, ffn_w_in, ffn_w_out, ab_w_in, ab_w_out,
           ret_log_gamma, ret_norm_w, mlstm_conv_w, mlstm_conv_b, mlstm_gate_b, mlstm_norm_w,
           attn_w_in, attn_w_out, attn_q_norm_w, attn_k_norm_w, attn_sink):
    nb, seq, d = x.shape
    ctx_len = ctx.shape[1]
    depth = ada_w.shape[0]
    assert ctx_len == ROW_TILE and seq % ROW_TILE == 0 and d == 8 * LANES and nb < 8
    t_all = ctx_len + seq
    nc = t_all // CHUNK
    ctx_chunks = ctx_len // CHUNK
    tpb = t_all // ROW_TILE
    dr = d // 2

    rows = jnp.zeros((8, d), F32).at[:nb].set(c).at[nb].set(c_ctx)
    mod_all = _modulation(rows, ada_w, ada_b)
    cos_t, sin_t = _rope_tables(seq, ctx_len)
    xc = jnp.concatenate([ctx, x], axis=1).reshape(nb * t_all, d)

    out = None
    for layer in range(depth):
        last = layer == depth - 1
        ab1, prm = _mod_tables(mod_all[layer], nb, norm_w[layer])
        wi = ffn_w_in[layer].astype(BF16)
        w2 = ffn_w_out[layer].astype(BF16)
        if layer % 2 == 0:
            e = layer // 2
            w = ab_w_in[e]
            perm = _pair_perm(dr // HEAD_DIM)
            w_main = jnp.concatenate([w[:, :dr][:, perm], w[:, dr:2 * dr][:, perm], w[:, 2 * dr:8 * dr]],
                                     axis=1).astype(BF16)
            wg = jnp.zeros((d, LANES), F32).at[:, :32].set(w[:, 8 * dr:]).astype(BF16)
            wgt = w[:, 8 * dr:].T.astype(BF16)
            gb = jnp.zeros((1, LANES), F32).at[0, :32].set(mlstm_gate_b[e].reshape(-1))
            gbt = mlstm_gate_b[e].reshape(32, 1)
            y, g, gt = _inproj_ab(xc, ab1, w_main, wg, wgt, gb, gbt, cos_t, sin_t, tpb)

            lg = ret_log_gamma[e].astype(F32)
            lgk = jnp.stack([lg[dd, (_pair_perm(8)[p * LANES:(p + 1) * LANES] // HEAD_DIM)]
                             for dd in range(2) for p in range(4)])
            cw = jnp.concatenate([mlstm_conv_w[e], mlstm_conv_b[e][None],
                                  jnp.zeros((4, 2 * dr), F32)], axis=0)
            nw = jnp.broadcast_to(jnp.concatenate([ret_norm_w[e], mlstm_norm_w[e]]).reshape(8, LANES, 1),
                                  (8, LANES, LANES))
            lg_s = lg.reshape(-1)
            states = _bwd_states(y, g, lgk, cw[:, dr:], nb, nc, ctx_chunks)
            merged = _mixer(y, g, gt, states, lg_s, lgk, lgk.T, cw, nw, nb, nc, ctx_chunks)
            wo = ab_w_out[e].astype(BF16)
            m_has_ctx = True
        else:
            o = layer // 2
            w = attn_w_in[o]
            qp, kp = _attn_q_perm(), _pair_perm(H_KV)
            w_main = jnp.concatenate([w[:, :d][:, qp], w[:, d:d + 2 * LANES][:, kp],
                                      w[:, d + 2 * LANES:]], axis=1).astype(BF16)
            lane_w = lambda v: jnp.concatenate([v[:32], v[:32], v[32:], v[32:]])
            nwq = jnp.stack([lane_w(attn_q_norm_w[o]) * (HEAD_DIM ** -0.5), lane_w(attn_k_norm_w[o])]
                            + [jnp.zeros((LANES,), F32)] * 6)
            y = _inproj_attn(xc, ab1, w_main, nwq, cos_t, sin_t, tpb)
            merged = _window_attn(y, attn_sink[o].astype(F32), nb, nc, ctx_chunks)
            wo = attn_w_out[o][_attn_o_perm()].astype(BF16)
            m_has_ctx = False
            if not last:
                raise NotImplementedError("attention layers are only supported as the final layer")
        skip = tpb - seq // ROW_TILE if last else 0
        xc = _post(xc, merged, prm, wo, wi, w2, tpb, skip, m_has_ctx)
        out = xc
    return out.reshape(nb, seq, d)
```

```python
import functools

import numpy as np
import jax
import jax.numpy as jnp
from jax import lax
from jax.experimental import pallas as pl
from jax.experimental.pallas import tpu as pltpu

F32 = jnp.float32
BF16 = jnp.bfloat16

HEAD_DIM = 64
CHUNK = 128
GRID_W = 64
ROPE_BASE = 10000.0
EPS = 1e-6
H_KV = 4
LANES = 128
ROW_TILE = 256
HALO = 16
NEG = -1e30
VMEM_LIMIT = 56 * 1024 * 1024


def _cparams(*sem):
    return pltpu.CompilerParams(dimension_semantics=sem, vmem_limit_bytes=VMEM_LIMIT)


def _const_spec(shape):
    nd = len(shape)
    return pl.BlockSpec(shape, lambda *_: (0,) * nd, pipeline_mode=pl.Buffered(1))


def _smem_spec():
    return pl.BlockSpec(memory_space=pltpu.SMEM)


def _lane(shape=(CHUNK, LANES)):
    return lax.broadcasted_iota(jnp.int32, shape, len(shape) - 1)


def _dot(a, b):
    return jnp.dot(a, b, preferred_element_type=F32)


def _dot_nt(a, b):
    return lax.dot_general(a, b, (((1,), (1,)), ((), ())), preferred_element_type=F32)


def _dot_tn(a, b):
    return lax.dot_general(a, b, (((0,), (0,)), ((), ())), preferred_element_type=F32)


def _split3(x):
    hi = x.astype(BF16)
    r = x - hi.astype(F32)
    mid = r.astype(BF16)
    lo = (r - mid.astype(F32)).astype(BF16)
    return hi, mid, lo


def _log_sigmoid(x):
    return jnp.minimum(x, 0.0) - jnp.log1p(jnp.exp(-jnp.abs(x)))


def _rope(x, cos, sin_signed):
    return x * cos + pltpu.roll(x, LANES // 2, 1) * sin_signed


def _mod_kernel(rows_ref, w_ref, b_ref, o_ref):
    a = rows_ref[...]
    a = a * jax.nn.sigmoid(a)
    a_hi = a.astype(BF16)
    a_lo = (a - a_hi.astype(F32)).astype(BF16)
    w = w_ref[...]
    w_hi = w.astype(BF16)
    w_lo = (w - w_hi.astype(F32)).astype(BF16)
    o_ref[...] = _dot(a_hi, w_hi) + _dot(a_hi, w_lo) + _dot(a_lo, w_hi) + b_ref[...]


def _modulation(rows, ada_w, ada_b):
    depth, d, n = ada_w.shape
    tn = n // 4
    return pl.pallas_call(
        _mod_kernel,
        grid=(depth, n // tn),
        in_specs=[pl.BlockSpec((8, d), lambda l, j: (0, 0)),
                  pl.BlockSpec((None, d, tn), lambda l, j: (l, 0, j)),
                  pl.BlockSpec((None, 1, tn), lambda l, j: (l, 0, j))],
        out_specs=pl.BlockSpec((None, 8, tn), lambda l, j: (l, 0, j)),
        out_shape=jax.ShapeDtypeStruct((depth, 8, n), F32),
        compiler_params=_cparams("arbitrary", "arbitrary"),
        name="adaln_modulation",
    )(rows, ada_w, ada_b.reshape(depth, 1, n))


def _norm_mod(x, ab_ref):
    ms = jnp.mean(x * x, axis=-1, keepdims=True)
    h = (x * lax.rsqrt(ms + EPS)) * ab_ref[0:1, :] + ab_ref[1:2, :]
    return h.astype(BF16)


def _ctx_or_latent_rows(ctx_ref, x_ref, tiles_per_batch):
    is_ctx = pl.program_id(0) % tiles_per_batch == 0
    return jnp.where(is_ctx, ctx_ref[...], x_ref[...])


def _split_row_specs(tm, d, tpb):
    lat = tpb - 1
    return [pl.BlockSpec((tm, d), lambda i: (i // tpb, 0)),
            pl.BlockSpec((tm, d), lambda i: ((i // tpb) * lat + jnp.maximum(i % tpb - 1, 0), 0))]


def _inproj_ab_kernel(ctx_ref, x_ref, ab_ref, w_ref, wg_ref, wgt_ref, gb_ref, gbt_ref, cos_ref, sin_ref,
                      y_ref, g_ref, gt_ref, *, tpb):
    hb = _norm_mod(_ctx_or_latent_rows(ctx_ref, x_ref, tpb), ab_ref)
    cos = cos_ref[...]
    sin = sin_ref[...]
    gw = 4 * LANES
    for j in range(8):
        acc = _dot(hb, w_ref[:, j * gw:(j + 1) * gw])
        if j in (0, 1):
            if j == 0:
                acc = acc * (HEAD_DIM ** -0.5)
            acc = jnp.concatenate(
                [_rope(acc[:, p * LANES:(p + 1) * LANES], cos, sin) for p in range(4)], axis=1)
        elif j == 3:
            acc = acc * jax.nn.sigmoid(acc)
        elif j == 7:
            acc = jax.nn.sigmoid(acc)
        y_ref[:, j * gw:(j + 1) * gw] = acc.astype(BF16)
    g_ref[...] = _dot(hb, wg_ref[...]) + gb_ref[...]
    gt_ref[...] = _dot_nt(wgt_ref[...], hb) + gbt_ref[...]


def _inproj_ab(ctx2, x2, ab, w, wg, wgt, gb, gbt, cos, sin, tiles_per_batch):
    d = x2.shape[1]
    r = ctx2.shape[0] + x2.shape[0]
    tm = ROW_TILE
    n = w.shape[1]
    tpb = tiles_per_batch
    sel = lambda i: ((i // tpb) * 2 + jnp.minimum(i % tpb, 1), 0, 0)
    return pl.pallas_call(
        functools.partial(_inproj_ab_kernel, tpb=tpb),
        grid=(r // tm,),
        in_specs=_split_row_specs(tm, d, tpb) + [
                  pl.BlockSpec((None, 2, d), sel),
                  _const_spec(w.shape), _const_spec(wg.shape), _const_spec(wgt.shape),
                  _const_spec(gb.shape), _const_spec(gbt.shape),
                  pl.BlockSpec((tm, LANES), lambda i: (i % tpb, 0)),
                  pl.BlockSpec((tm, LANES), lambda i: (i % tpb, 0))],
        out_specs=[pl.BlockSpec((tm, n), lambda i: (i, 0)),
                   pl.BlockSpec((tm, LANES), lambda i: (i, 0)),
                   pl.BlockSpec((32, tm), lambda i: (0, i))],
        out_shape=[jax.ShapeDtypeStruct((r, n), BF16),
                   jax.ShapeDtypeStruct((r, LANES), F32),
                   jax.ShapeDtypeStruct((32, r), F32)],
        compiler_params=_cparams("arbitrary"),
        name="inproj_ret_mlstm",
    )(ctx2, x2, ab, w, wg, wgt, gb, gbt, cos, sin)


def _inproj_attn_kernel(x_ref, ab_ref, w_ref, nw_ref, cos_ref, sin_ref, y_ref):
    hb = _norm_mod(x_ref[...], ab_ref)
    cos = cos_ref[...]
    sin = sin_ref[...]
    r2 = lax.broadcasted_iota(jnp.int32, (2 * LANES, 2 * LANES), 0)
    c2 = lax.broadcasted_iota(jnp.int32, (2 * LANES, 2 * LANES), 1)
    same_head = (((r2 ^ c2) & (LANES | 32)) == 0).astype(BF16)
    acc_q = _dot(hb, w_ref[:, 0:8 * LANES])
    acc_k = _dot(hb, w_ref[:, 8 * LANES:10 * LANES])
    for j in range(5):
        acc = acc_q[:, j * 2 * LANES:(j + 1) * 2 * LANES] if j < 4 else acc_k
        sq = acc * acc
        hi = sq.astype(BF16)
        lo = (sq - hi.astype(F32)).astype(BF16)
        ms = (_dot(hi, same_head) + _dot(lo, same_head)) * (1.0 / HEAD_DIM)
        nrm = acc * lax.rsqrt(ms + EPS)
        nw = nw_ref[0:1, :] if j < 4 else nw_ref[1:2, :]
        for u in range(2):
            ls = slice(u * LANES, (u + 1) * LANES)
            y_ref[:, (2 * j + u) * LANES:(2 * j + u + 1) * LANES] = _rope(nrm[:, ls] * nw, cos, sin).astype(BF16)
    y_ref[:, 10 * LANES:12 * LANES] = _dot(hb, w_ref[:, 10 * LANES:12 * LANES]).astype(BF16)


def _inproj_attn(x2, ab, w, nw, cos, sin, tiles_per_batch):
    r, d = x2.shape
    tm = ROW_TILE
    n = w.shape[1]
    tpb = tiles_per_batch
    sel = lambda i: ((i // tpb) * 2 + jnp.minimum(i % tpb, 1), 0, 0)
    return pl.pallas_call(
        _inproj_attn_kernel,
        grid=(r // tm,),
        in_specs=[pl.BlockSpec((tm, d), lambda i: (i, 0)),
                  pl.BlockSpec((None, 2, d), sel),
                  _const_spec(w.shape), _const_spec(nw.shape),
                  pl.BlockSpec((tm, LANES), lambda i: (i % tpb, 0)),
                  pl.BlockSpec((tm, LANES), lambda i: (i % tpb, 0))],
        out_specs=pl.BlockSpec((tm, n), lambda i: (i, 0)),
        out_shape=jax.ShapeDtypeStruct((r, n), BF16),
        compiler_params=_cparams("arbitrary"),
        name="inproj_attn",
    )(x2, ab, w, nw, cos, sin)


def _conv_silu(cur_ref, prev_ref, next_ref, cw_ref, prev_on, next_on):
    cur = cur_ref[...].astype(F32)
    row = lax.broadcasted_iota(jnp.int32, cur.shape, 0)
    prev_row = prev_ref[HALO - 1:HALO, :].astype(F32) * prev_on
    next_row = next_ref[0:1, :].astype(F32) * next_on
    xm = jnp.where(row == 0, prev_row, pltpu.roll(cur, 1, 0))
    xp = jnp.where(row == CHUNK - 1, next_row, pltpu.roll(cur, CHUNK - 1, 0))
    y = cw_ref[3:4, :] + cw_ref[0:1, :] * xm + cw_ref[1:2, :] * cur + cw_ref[2:3, :] * xp
    return y * jax.nn.sigmoid(y)


def _segment_flags(c, nc, ctx_chunks):
    prev_on = jnp.where((c == 0) | (c == ctx_chunks), 0.0, 1.0).astype(F32)
    next_on = jnp.where((c == ctx_chunks - 1) | (c == nc - 1), 0.0, 1.0).astype(F32)
    return prev_on, next_on


def _cumsum_cols(tri_bf, lf):
    hi, mid, lo = _split3(lf)
    return _dot(tri_bf, hi) + _dot(tri_bf, mid) + _dot(tri_bf, lo)


def _cumsum_rows(lf, tri_bf):
    hi, mid, lo = _split3(lf)
    return _dot(hi, tri_bf) + _dot(mid, tri_bf) + _dot(lo, tri_bf)


def _ret_state_update(s_ref, p, k2, v2, kdec, cd_lanes, bd):
    kf = (k2.astype(F32) * kdec).astype(BF16)
    s_ref[p] = s_ref[p] * cd_lanes + jnp.where(bd, _dot_tn(v2, kf), 0.0)


def _mlstm_state_update(c_ref, n_ref, m_ref, k_pairs, v_pairs, c_all, bend, col0, lo, bd):
    cmax = jnp.max(c_all, axis=0, keepdims=True)
    w_all = jnp.exp(c_all - cmax)
    m_old = m_ref[0:1, :]
    mrel = jnp.maximum(m_old, cmax)
    a_row = jnp.exp(m_old - mrel)
    bb_row = jnp.exp(cmax - mrel)
    m_ref[0:1, :] = bend + mrel
    lo_row = lo[0:1, :]
    for p in range(4):
        h0 = col0 + 2 * p
        kw = k_pairs[p] * jnp.where(lo, w_all[:, h0:h0 + 1], w_all[:, h0 + 1:h0 + 2])
        kvt = _dot_tn(v_pairs[p], kw.astype(BF16))
        nloc = jnp.sum(kw, axis=0, keepdims=True)
        a_l = jnp.where(lo_row, a_row[:, h0:h0 + 1], a_row[:, h0 + 1:h0 + 2])
        bb_l = jnp.where(lo_row, bb_row[:, h0:h0 + 1], bb_row[:, h0 + 1:h0 + 2])
        c_ref[p] = c_ref[p] * a_l + jnp.where(bd, kvt, 0.0) * bb_l
        n_new = (n_ref[p, 0:1, :] + n_ref[p, 1:2, :]) * a_l + nloc * bb_l
        n_ref[p, 0:1, :] = jnp.where(lo_row, n_new, 0.0)
        n_ref[p, 1:2, :] = jnp.where(lo_row, 0.0, n_new)


def _bwd_state_kernel(rk_ref, rv_ref, mk_ref, mkp_ref, mkn_ref, mv_ref, g_ref,
                      lgk_ref, cw_ref,
                      sret_ref, cm_ref, nm_ref, mm_ref,
                      s_scr, c_scr, n_scr, m_scr, kdec_scr, *, nc, ctx_chunks):
    i = pl.program_id(1)
    c = jnp.where(i < ctx_chunks, ctx_chunks - 1 - i, nc + ctx_chunks - 1 - i)
    lane = _lane()
    sub = lax.broadcasted_iota(jnp.int32, (CHUNK, LANES), 0)
    lo = lane < HEAD_DIM
    bd_ret = (sub >= HEAD_DIM) == ((lane & 32) != 0)
    bd_m = (sub >= HEAD_DIM) == (lane >= HEAD_DIM)

    @pl.when(i == 0)
    def _():
        s_scr[...] = jnp.zeros_like(s_scr)
        c_scr[...] = jnp.zeros_like(c_scr)
        n_scr[...] = jnp.zeros_like(n_scr)
        m_scr[...] = jnp.zeros_like(m_scr)
        pos = sub.astype(F32)
        for p in range(4):
            kdec_scr[p] = jnp.exp(lgk_ref[4 + p:5 + p, :] * pos)

    sret_ref[...] = s_scr[...].astype(BF16)
    cm_ref[...] = c_scr[...].astype(BF16)
    nm_ref[...] = n_scr[...].astype(BF16)
    mm_ref[...] = m_scr[...]

    for p in range(4):
        cd = jnp.exp(lgk_ref[4 + p:5 + p, :] * float(CHUNK))
        _ret_state_update(s_scr, p, rk_ref[:, p * LANES:(p + 1) * LANES],
                          rv_ref[:, p * LANES:(p + 1) * LANES], kdec_scr[p], cd, bd_ret)

    prev_on, next_on = _segment_flags(c, nc, ctx_chunks)
    kc = _conv_silu(mk_ref, mkp_ref, mkn_ref, cw_ref, prev_on, next_on)
    g = g_ref[...]
    le_bf = (sub <= lane).astype(BF16)
    bal = pltpu.roll(_cumsum_cols(le_bf, _log_sigmoid(g)), LANES - 8, 1)
    _mlstm_state_update(c_scr, n_scr, m_scr,
                        [kc[:, p * LANES:(p + 1) * LANES] for p in range(4)],
                        [mv_ref[:, p * LANES:(p + 1) * LANES] for p in range(4)],
                        g - bal, bal[0:1, :], 16, lo, bd_m)


def _bwd_states(y, g, lgk, cw_k, nb, nc, ctx_chunks):
    gw = 4 * LANES
    def cidx(b, i):
        c = jnp.where(i < ctx_chunks, ctx_chunks - 1 - i, nc + ctx_chunks - 1 - i)
        return b * nc + c
    hb = CHUNK // HALO
    nhalo = y.shape[0] // HALO
    return pl.pallas_call(
        functools.partial(_bwd_state_kernel, nc=nc, ctx_chunks=ctx_chunks),
        grid=(nb, nc),
        in_specs=[pl.BlockSpec((CHUNK, gw), lambda b, i: (cidx(b, i), 1)),
                  pl.BlockSpec((CHUNK, gw), lambda b, i: (cidx(b, i), 2)),
                  pl.BlockSpec((CHUNK, gw), lambda b, i: (cidx(b, i), 5)),
                  pl.BlockSpec((HALO, gw), lambda b, i: (jnp.maximum(cidx(b, i) * hb - 1, 0), 5)),
                  pl.BlockSpec((HALO, gw), lambda b, i: (jnp.minimum((cidx(b, i) + 1) * hb, nhalo - 1), 5)),
                  pl.BlockSpec((CHUNK, gw), lambda b, i: (cidx(b, i), 6)),
                  pl.BlockSpec((CHUNK, LANES), lambda b, i: (cidx(b, i), 0)),
                  _const_spec(lgk.shape), _const_spec(cw_k.shape)],
        out_specs=[pl.BlockSpec((None, 4, CHUNK, LANES), lambda b, i: (cidx(b, i), 0, 0, 0)),
                   pl.BlockSpec((None, 4, CHUNK, LANES), lambda b, i: (cidx(b, i), 0, 0, 0)),
                   pl.BlockSpec((None, 4, HALO, LANES), lambda b, i: (cidx(b, i), 0, 0, 0)),
                   pl.BlockSpec((None, 8, LANES), lambda b, i: (cidx(b, i), 0, 0))],
        out_shape=[jax.ShapeDtypeStruct((nb * nc, 4, CHUNK, LANES), BF16),
                   jax.ShapeDtypeStruct((nb * nc, 4, CHUNK, LANES), BF16),
                   jax.ShapeDtypeStruct((nb * nc, 4, HALO, LANES), BF16),
                   jax.ShapeDtypeStruct((nb * nc, 8, LANES), F32)],
        scratch_shapes=[pltpu.VMEM((4, CHUNK, LANES), F32), pltpu.VMEM((4, CHUNK, LANES), F32),
                        pltpu.VMEM((4, HALO, LANES), F32), pltpu.VMEM((8, LANES), F32),
                        pltpu.VMEM((4, CHUNK, LANES), F32)],
        compiler_params=_cparams("arbitrary", "arbitrary"),
        name="bwd_state_sweep",
    )(y, y, y, y, y, y, g, lgk, cw_k)


def _mlstm_dir_weights(st, qn_row, c_col, bt_row, m_prev, tri):
    dl = jnp.where(tri, c_col + bt_row, NEG)
    mx = jnp.max(dl, axis=0, keepdims=True)
    al = bt_row + m_prev
    m_t = jnp.maximum(al, mx)
    w = jnp.exp(dl - m_t)
    a_t = jnp.exp(al - m_t)
    sw = st * w
    den = jnp.sum(sw, axis=0, keepdims=True) + a_t * qn_row
    r = 1.0 / jnp.maximum(jnp.abs(den), jnp.exp(-m_t))
    return sw * r, a_t * r


def _heads_out(ht, nw_tab):
    rows = []
    for a in range(2):
        ha = ht[a * HEAD_DIM:(a + 1) * HEAD_DIM, :]
        ms = jnp.mean(ha * ha, axis=0, keepdims=True)
        rows.append(ha * lax.rsqrt(ms + EPS))
    return (jnp.concatenate(rows, axis=0) * nw_tab).T


def _mixer_kernel(lg_ref, rq_ref, rk_ref, rv_ref, rg_ref, mqk_ref, mqkp_ref, mqkn_ref, mv_ref, mo_ref,
                  g_ref, gt_ref, sretb_ref, cmb_ref, nmb_ref, mmb_ref,
                  lgk_ref, lgkt_ref, cw_ref, nw_ref,
                  out_ref,
                  s_scr, c_scr, n_scr, m_scr, dm_scr, dec_scr, *, nc, ctx_chunks):
    c = pl.program_id(1)
    lane = _lane()
    sub = lax.broadcasted_iota(jnp.int32, (CHUNK, LANES), 0)
    lo = lane < HEAD_DIM
    sub_lo = sub < HEAD_DIM
    mask_ret = [(lane & 32) == 0, (lane & 32) != 0]
    mask_nat = [lo, lane >= HEAD_DIM]
    bd_ret = (sub >= HEAD_DIM) == ((lane & 32) != 0)
    bd_m = (sub >= HEAD_DIM) == (lane >= HEAD_DIM)
    le = sub <= lane
    ge = sub >= lane

    @pl.when(c == 0)
    def _():
        s_scr[...] = jnp.zeros_like(s_scr)
        c_scr[...] = jnp.zeros_like(c_scr)
        n_scr[...] = jnp.zeros_like(n_scr)
        m_scr[...] = jnp.zeros_like(m_scr)
        spos = sub.astype(F32)
        tpos = lane.astype(F32)
        diff = (lane - sub).astype(F32)
        for h in range(8):
            dm_scr[h] = (jnp.where(le, jnp.exp(lg_ref[h] * diff), 0.0)
                         + jnp.where(ge, jnp.exp(lg_ref[8 + h] * (-diff)), 0.0))
        for p in range(4):
            dec_scr[0, p] = jnp.exp(lgkt_ref[:, p:p + 1] * (tpos + 1.0))
            dec_scr[1, p] = jnp.exp(lgkt_ref[:, 4 + p:5 + p] * (float(CHUNK) - tpos))
            dec_scr[2, p] = jnp.exp(lgk_ref[p:p + 1, :] * (float(CHUNK) - 1.0 - spos))


    prev_on, next_on = _segment_flags(c, nc, ctx_chunks)
    qk = _conv_silu(mqk_ref, mqkp_ref, mqkn_ref, cw_ref, prev_on, next_on)
    g = g_ref[...]
    gt = gt_ref[...]
    lf_col = _log_sigmoid(g)
    lf_row = _log_sigmoid(gt)
    le_bf = le.astype(BF16)
    ge_bf = ge.astype(BF16)
    bal_f = pltpu.roll(_cumsum_cols(ge_bf, lf_col), LANES - 8, 1)
    bal_b = pltpu.roll(_cumsum_cols(le_bf, lf_col), LANES - 8, 1)
    cf_all = g - bal_f
    cb_all = g - bal_b
    bf_row = _cumsum_rows(lf_row, le_bf)
    bb_row = _cumsum_rows(lf_row, ge_bf)

    qb, kb, vb, kf32, qt, vbd = [], [], [], [], [], []
    for p in range(8):
        sl = slice((p % 4) * LANES, (p % 4 + 1) * LANES)
        if p < 4:
            q2, k2, v2 = rq_ref[:, sl], rk_ref[:, sl], rv_ref[:, sl]
            qf = q2.astype(F32)
            kf = None
            old =jnp.concatenate([s_scr[p].astype(BF16), sretb_ref[p]], axis=1)
        else:
            qf = qk[:, sl] * (HEAD_DIM ** -0.5)
            kf = qk[:, 4 * LANES + (p - 4) * LANES:4 * LANES + (p - 3) * LANES]
            q2, k2, v2 = qf.astype(BF16), kf.astype(BF16), mv_ref[:, sl]
            old = jnp.concatenate([c_scr[p - 4].astype(BF16), cmb_ref[p - 4]], axis=1)
        vt = v2.astype(F32).T.astype(BF16)
        qb.append(q2)
        kb.append(k2)
        vb.append(v2)
        kf32.append(kf)
        qt.append(qf.T)
        vbd.append(jnp.concatenate([jnp.where(sub_lo, vt, jnp.zeros_like(vt)),
                                    jnp.where(sub_lo, jnp.zeros_like(vt), vt), old], axis=1))

    st2, qn_f, qn_b = [], [], []
    for p in range(8):
        masks = mask_ret if p < 4 else mask_nat
        zero = jnp.zeros_like(qb[p])
        qstack = jnp.concatenate([jnp.where(masks[0], qb[p], zero), jnp.where(masks[1], qb[p], zero)], axis=0)
        st2.append(_dot_nt(kb[p], qstack))
        if p >= 4:
            qn_f.append(_dot_nt(n_scr[p - 4].astype(BF16), qb[p]))
            qn_b.append(_dot_nt(nmb_ref[p - 4], qb[p]))

    rhs = []
    for p in range(8):
        if p < 4:
            pts = [(st2[p][:, a * LANES:(a + 1) * LANES] * dm_scr[2 * p + a]).astype(BF16) for a in range(2)]
            x_f, x_b = dec_scr[0, p], dec_scr[1, p]
        else:
            pts, cf, cb = [], [], []
            for a in range(2):
                h = 2 * (p - 4) + a
                st = st2[p][:, a * LANES:(a + 1) * LANES]
                pf, coef_f = _mlstm_dir_weights(st, qn_f[p - 4][a:a + 1, :], cf_all[:, h:h + 1],
                                                bf_row[8 + h:9 + h, :], m_scr[0:1, h:h + 1], le)
                pb, coef_b = _mlstm_dir_weights(st, qn_b[p - 4][a:a + 1, :], cb_all[:, 16 + h:17 + h],
                                                bb_row[24 + h:25 + h, :], mmb_ref[0:1, 16 + h:17 + h], ge)
                pts.append((pf + pb).astype(BF16))
                cf.append(coef_f)
                cb.append(coef_b)
            x_f = jnp.where(sub_lo, cf[0], cf[1])
            x_b = jnp.where(sub_lo, cb[0], cb[1])
        rhs.append(jnp.concatenate(pts + [(qt[p] * x_f).astype(BF16), (qt[p] * x_b).astype(BF16)], axis=0))

    ht = [_dot(vbd[p], rhs[p]) for p in range(8)]

    for p in range(8):
        sl = slice((p % 4) * LANES, (p % 4 + 1) * LANES)
        gate_ref = rg_ref if p < 4 else mo_ref
        y = _heads_out(ht[p], nw_ref[p])
        out_ref[:, p * LANES:(p + 1) * LANES] = (y * gate_ref[:, sl].astype(F32)).astype(BF16)

    for p in range(4):
        cd = jnp.exp(lgk_ref[p:p + 1, :] * float(CHUNK))
        _ret_state_update(s_scr, p, kb[p], vb[p], dec_scr[2, p], cd, bd_ret)
    _mlstm_state_update(c_scr, n_scr, m_scr, kf32[4:], vb[4:], cf_all, bal_f[CHUNK - 1:CHUNK, :], 0, lo, bd_m)


def _mixer(y, g, gt, states, lg_smem, lgk, lgkt, cw_qk, nw, nb, nc, ctx_chunks):
    gw = 4 * LANES
    sretb, cmb, nmb, mmb = states
    rc = lambda b, c: b * nc + c
    hb = CHUNK // HALO
    nhalo = y.shape[0] // HALO
    blk = lambda j: pl.BlockSpec((CHUNK, gw), lambda b, c: (rc(b, c), j))
    return pl.pallas_call(
        functools.partial(_mixer_kernel, nc=nc, ctx_chunks=ctx_chunks),
        grid=(nb, nc),
        in_specs=[_smem_spec(),
                  blk(0), blk(1), blk(2), blk(3),
                  pl.BlockSpec((CHUNK, 2 * gw), lambda b, c: (rc(b, c), 2)),
                  pl.BlockSpec((HALO, 2 * gw), lambda b, c: (jnp.maximum(rc(b, c) * hb - 1, 0), 2)),
                  pl.BlockSpec((HALO, 2 * gw), lambda b, c: (jnp.minimum((rc(b, c) + 1) * hb, nhalo - 1), 2)),
                  blk(6), blk(7),
                  pl.BlockSpec((CHUNK, LANES), lambda b, c: (rc(b, c), 0)),
                  pl.BlockSpec((32, CHUNK), lambda b, c: (0, rc(b, c))),
                  pl.BlockSpec((None, 4, CHUNK, LANES), lambda b, c: (rc(b, c), 0, 0, 0)),
                  pl.BlockSpec((None, 4, CHUNK, LANES), lambda b, c: (rc(b, c), 0, 0, 0)),
                  pl.BlockSpec((None, 4, HALO, LANES), lambda b, c: (rc(b, c), 0, 0, 0)),
                  pl.BlockSpec((None, 8, LANES), lambda b, c: (rc(b, c), 0, 0)),
                  _const_spec(lgk.shape), _const_spec(lgkt.shape), _const_spec(cw_qk.shape),
                  _const_spec(nw.shape)],
        out_specs=pl.BlockSpec((CHUNK, 2 * gw), lambda b, c: (rc(b, c), 0)),
        out_shape=jax.ShapeDtypeStruct((nb * nc * CHUNK, 2 * gw), BF16),
        scratch_shapes=[pltpu.VMEM((4, CHUNK, LANES), F32), pltpu.VMEM((4, CHUNK, LANES), F32),
                        pltpu.VMEM((4, HALO, LANES), F32), pltpu.VMEM((8, LANES), F32),
                        pltpu.VMEM((8, CHUNK, LANES), F32), pltpu.VMEM((3, 4, CHUNK, LANES), F32)],
        compiler_params=_cparams("arbitrary", "arbitrary"),
        name="ret_mlstm_mixer",
    )(lg_smem, y, y, y, y, y, y, y, y, y, g, gt, sretb, cmb, nmb, mmb, lgk, lgkt, cw_qk, nw)


def _attn_kernel(sink_ref, q_ref, kp_ref, kc_ref, kn_ref, kx_ref, vp_ref, vc_ref, vn_ref, vx_ref,
                 bias_ref, o_ref):
    grp = lax.broadcasted_iota(jnp.int32, (1, 4 * CHUNK), 1) // CHUNK
    lane = _lane()
    mask_q = [(lane & 32) == 0, (lane & 32) != 0]
    bias = jnp.concatenate([bias_ref[...]] * 4, axis=1)
    vts, sts = [], []
    for kvp in range(2):
        sl = slice(kvp * LANES, (kvp + 1) * LANES)
        v_blocks = [vp_ref[:, sl], vc_ref[:, sl], vn_ref[:, sl], vx_ref[:, sl]]
        vts.append(jnp.concatenate(
            [vb[r * CHUNK:(r + 1) * CHUNK, :].astype(F32).T.astype(BF16)
             for vb in v_blocks for r in range(vb.shape[0] // CHUNK)], axis=1))
    for kvp in range(2):
        sl = slice(kvp * LANES, (kvp + 1) * LANES)
        kcat = jnp.concatenate([kp_ref[:, sl], kc_ref[:, sl], kn_ref[:, sl], kx_ref[:, sl]], axis=0)
        for a in range(2):
            qs = jnp.concatenate(
                [jnp.where(mask_q[a], q_ref[:, (kvp * 4 + g) * LANES:(kvp * 4 + g + 1) * LANES],
                           jnp.zeros((CHUNK, LANES), BF16)) for g in range(4)], axis=0)
            sts.append(_dot_nt(kcat, qs))
    outs = []
    for kv in range(4):
        st = sts[kv]
        st = jnp.concatenate([st[0:CHUNK] + bias[0:CHUNK], st[CHUNK:2 * CHUNK],
                              st[2 * CHUNK:3 * CHUNK] + bias[CHUNK:2 * CHUNK], st[3 * CHUNK:]], axis=0)
        snk = jnp.where(grp == 0, sink_ref[kv * 4],
                        jnp.where(grp == 1, sink_ref[kv * 4 + 1],
                                  jnp.where(grp == 2, sink_ref[kv * 4 + 2], sink_ref[kv * 4 + 3])))
        m = jnp.maximum(jnp.max(st, axis=0, keepdims=True), snk)
        e = jnp.exp(st - m)
        denom = jnp.exp(snk - m) + jnp.sum(e, axis=0, keepdims=True)
        a = kv % 2
        outs.append(_dot(vts[kv // 2][a * HEAD_DIM:(a + 1) * HEAD_DIM, :], e.astype(BF16)) * (1.0 / denom))
    for kvp in range(2):
        full = jnp.concatenate(outs[2 * kvp:2 * kvp + 2], axis=0)
        for g in range(4):
            o_ref[:, (kvp * 4 + g) * LANES:(kvp * 4 + g + 1) * LANES] = (
                full[:, g * CHUNK:(g + 1) * CHUNK].T.astype(BF16))


def _window_bias():
    kk = np.arange(CHUNK)[:, None]
    t = np.arange(CHUNK)[None, :]
    tabs = []
    for has_prev, has_next in ((False, True), (True, True), (True, False)):
        prev_ok = (kk >= t) & has_prev
        next_ok = (kk <= t) & has_next
        tabs.append(np.where(np.concatenate([prev_ok, next_ok], axis=0), 0.0, NEG))
    return jnp.asarray(np.stack(tabs), F32)


def _window_attn(y, sink, nb, nc, ctx_chunks):
    nq = nc - ctx_chunks
    assert nq >= 2
    ctx_len = ctx_chunks * CHUNK
    bias = _window_bias()
    d = 8 * LANES
    kcol, vcol = d // (2 * LANES), d // (2 * LANES) + 1
    qrow = lambda b, i: b * nc + ctx_chunks + i
    prow = lambda b, i: b * nc + ctx_chunks + jnp.maximum(i - 1, 0)
    nrow = lambda b, i: b * nc + ctx_chunks + jnp.minimum(i + 1, nq - 1)
    xrow = lambda b, i: (b * nc * CHUNK) // ctx_len
    kv_spec = lambda rowf, col: pl.BlockSpec((CHUNK, 2 * LANES), lambda b, i: (rowf(b, i), col))
    x_spec = lambda col: pl.BlockSpec((ctx_len, 2 * LANES), lambda b, i: (xrow(b, i), col))
    bias_spec = pl.BlockSpec((None,) + bias.shape[1:],
                             lambda b, i: (jnp.where(i == 0, 0, jnp.where(i == nq - 1, 2, 1)), 0, 0))
    return pl.pallas_call(
        _attn_kernel,
        grid=(nb, nq),
        in_specs=[_smem_spec(),
                  pl.BlockSpec((CHUNK, d), lambda b, i: (qrow(b, i), 0)),
                  kv_spec(prow, kcol), kv_spec(qrow, kcol), kv_spec(nrow, kcol), x_spec(kcol),
                  kv_spec(prow, vcol), kv_spec(qrow, vcol), kv_spec(nrow, vcol), x_spec(vcol),
                  bias_spec],
        out_specs=pl.BlockSpec((CHUNK, d), lambda b, i: (b * nq + i, 0)),
        out_shape=jax.ShapeDtypeStruct((nb * nq * CHUNK, d), BF16),
        compiler_params=_cparams("arbitrary", "arbitrary"),
        name="window_gqa",
    )(sink, y, y, y, y, y, y, y, y, y, bias)


def _post_kernel(*refs, d_ff, ff_chunk, split_tpb):
    if split_tpb:
        ctx_ref, x_ref, m_ref, p_ref, wo_ref, wi_ref, w2_ref, o_ref, act_scr = refs
        x = _ctx_or_latent_rows(ctx_ref, x_ref, split_tpb)
    else:
        x_ref, m_ref, p_ref, wo_ref, wi_ref, w2_ref, o_ref, act_scr = refs
        x = x_ref[...]
    x1 = x + p_ref[0:1, :] * _dot(m_ref[...], wo_ref[...])
    ms = jnp.mean(x1 * x1, axis=-1, keepdims=True)
    h = ((x1 * lax.rsqrt(ms + EPS)) * p_ref[1:2, :] + p_ref[2:3, :]).astype(BF16)
    for j in range(d_ff // ff_chunk):
        gate = _dot(h, wi_ref[:, j * ff_chunk:(j + 1) * ff_chunk])
        up = _dot(h, wi_ref[:, d_ff + j * ff_chunk:d_ff + (j + 1) * ff_chunk])
        act_scr[:, j * ff_chunk:(j + 1) * ff_chunk] = (gate * jax.nn.sigmoid(gate) * up).astype(BF16)
    o_ref[...] = x1 + p_ref[3:4, :] * _dot(act_scr[...], w2_ref[...])


def _post(xs, m2, prm, wo, wi_all, w2_all, layer, tiles_per_batch, skip_tiles, m_has_ctx):
    split = len(xs) == 2
    d = xs[-1].shape[1]
    r = sum(a.shape[0] for a in xs)
    tm = ROW_TILE
    d_ff = w2_all.shape[1]
    tpb = tiles_per_batch
    kept = tpb - skip_tiles
    nb = r // (tm * tpb)
    assert not (split and skip_tiles)
    xrow = lambda i: (i // kept) * tpb + skip_tiles + i % kept
    mrow = xrow if m_has_ctx else (lambda i: i)
    sel = lambda i: ((i // kept) * 2 + jnp.minimum(skip_tiles + i % kept, 1), 0, 0)
    x_specs = _split_row_specs(tm, d, tpb) if split else [pl.BlockSpec((tm, d), lambda i: (xrow(i), 0))]
    layer_spec = lambda a: pl.BlockSpec((None,) + a.shape[1:], lambda i: (layer, 0, 0),
                                        pipeline_mode=pl.Buffered(1))
    ff_chunk = d_ff // 4 if (d_ff // 4) % LANES == 0 else d_ff
    return pl.pallas_call(
        functools.partial(_post_kernel, d_ff=d_ff, ff_chunk=ff_chunk, split_tpb=tpb if split else 0),
        grid=(nb * kept,),
        in_specs=x_specs + [
                  pl.BlockSpec((tm, d), lambda i: (mrow(i), 0)),
                  pl.BlockSpec((None, 8, d), sel),
                  _const_spec(wo.shape), layer_spec(wi_all), layer_spec(w2_all)],
        out_specs=pl.BlockSpec((tm, d), lambda i: (i, 0)),
        out_shape=jax.ShapeDtypeStruct((nb * kept * tm, d), F32),
        scratch_shapes=[pltpu.VMEM((tm, d_ff), BF16)],
        compiler_params=_cparams("arbitrary"),
        name="outproj_swiglu",
    )(*xs, m2, prm, wo, wi_all, w2_all)


def _pair_cols(w):
    rows, cols = w.shape
    return w.reshape(rows, cols // LANES, 2, 2, 32).transpose(0, 1, 3, 2, 4).reshape(rows, cols)


def _attn_q_cols(w):
    rows = w.shape[0]
    g_per = w.shape[1] // (H_KV * HEAD_DIM)
    return (w.reshape(rows, H_KV // 2, 2, g_per, 2, 32).transpose(0, 1, 3, 4, 2, 5)
            .reshape(rows, w.shape[1]))


def _attn_o_rows(w):
    cols = w.shape[1]
    g_per = w.shape[0] // (H_KV * HEAD_DIM)
    return (w.reshape(H_KV // 2, 2, g_per, HEAD_DIM, cols).transpose(0, 2, 1, 3, 4)
            .reshape(w.shape[0], cols))


def _rope_tables(seq, ctx_len):
    rows = seq // GRID_W
    row = np.repeat(np.arange(rows, dtype=np.float32), GRID_W)
    col = np.tile(np.arange(GRID_W, dtype=np.float32), rows)
    n = HEAD_DIM // 4
    inv = (np.float32(ROPE_BASE) ** (-np.arange(n, dtype=np.float32) / np.float32(n))).astype(np.float32)
    ang = np.concatenate([row[:, None] * inv, col[:, None] * inv], axis=-1).astype(np.float32)
    cos, sin = np.cos(ang), np.sin(ang)
    cos_t = np.concatenate([np.ones((ctx_len, LANES), np.float32), np.tile(cos, (1, 4))], axis=0)
    sin_t = np.concatenate([np.zeros((ctx_len, LANES), np.float32),
                            np.concatenate([-sin, -sin, sin, sin], axis=-1)], axis=0)
    return jnp.asarray(cos_t, F32), jnp.asarray(sin_t, F32)


def _mod_tables(mod, nb, norm_w):
    d = norm_w.shape[-1]
    lat = mod[:nb].reshape(nb, 6, d)
    ctx = jnp.broadcast_to(mod[nb].reshape(1, 6, d), (nb, 6, d))
    both = jnp.stack([ctx, lat], axis=1).reshape(nb * 2, 6, d)
    sh1, sc1, g1, sh2, sc2, g2 = [both[:, k] for k in range(6)]
    ab1 = jnp.stack([norm_w[0] * (1.0 + sc1), sh1], axis=1)
    zeros = jnp.zeros_like(g1)
    prm = jnp.stack([g1, norm_w[1] * (1.0 + sc2), sh2, g2, zeros, zeros, zeros, zeros], axis=1)
    return ab1, prm


def kernel(x, c, ctx, c_ctx, ada_w, ada_b, norm_w, ffn_w_in, ffn_w_out, ab_w_in, ab_w_out,
           ret_log_gamma, ret_norm_w, mlstm_conv_w, mlstm_conv_b, mlstm_gate_b, mlstm_norm_w,
           attn_w_in, attn_w_out, attn_q_norm_w, attn_k_norm_w, attn_sink):
    nb, seq, d = x.shape
    ctx_len = ctx.shape[1]
    depth = ada_w.shape[0]
    assert ctx_len == ROW_TILE and seq % ROW_TILE == 0 and d == 8 * LANES and nb < 8
    t_all = ctx_len + seq
    nc = t_all // CHUNK
    ctx_chunks = ctx_len // CHUNK
    tpb = t_all // ROW_TILE
    dr = d // 2

    rows = jnp.zeros((8, d), F32).at[:nb].set(c).at[nb].set(c_ctx)
    mod_all = _modulation(rows, ada_w, ada_b)
    cos_t, sin_t = _rope_tables(seq, ctx_len)
    wi_all = ffn_w_in.astype(BF16)
    w2_all = ffn_w_out.astype(BF16)
    xs = (ctx.reshape(nb * ctx_len, d), x.reshape(nb * seq, d))

    out = None
    for layer in range(depth):
        last = layer == depth - 1
        ab1, prm = _mod_tables(mod_all[layer], nb, norm_w[layer])
        if layer % 2 == 0:
            assert layer == 0
            e = layer // 2
            w = ab_w_in[e]
            w_main = jnp.concatenate([_pair_cols(w[:, :dr]), _pair_cols(w[:, dr:2 * dr]), w[:, 2 * dr:8 * dr]],
                                     axis=1).astype(BF16)
            wg = jnp.zeros((d, LANES), F32).at[:, :32].set(w[:, 8 * dr:]).astype(BF16)
            wgt = w[:, 8 * dr:].T.astype(BF16)
            gb = jnp.zeros((1, LANES), F32).at[0, :32].set(mlstm_gate_b[e].reshape(-1))
            gbt = mlstm_gate_b[e].reshape(32, 1)
            y, g, gt = _inproj_ab(*xs, ab1, w_main, wg, wgt, gb, gbt, cos_t, sin_t, tpb)

            lg = ret_log_gamma[e].astype(F32)
            lgk = jnp.tile(jnp.repeat(lg.reshape(2, 4, 2), 32, axis=-1), (1, 1, 2)).reshape(8, LANES)
            cw =jnp.concatenate([mlstm_conv_w[e], mlstm_conv_b[e][None],
                                  jnp.zeros((4, 2 * dr), F32)], axis=0)
            nw = jnp.broadcast_to(jnp.concatenate([ret_norm_w[e], mlstm_norm_w[e]]).reshape(8, LANES, 1),
                                  (8, LANES, LANES))
            lg_s = lg.reshape(-1)
            states = _bwd_states(y, g, lgk, cw[:, dr:], nb, nc, ctx_chunks)
            merged = _mixer(y, g, gt, states, lg_s, lgk, lgk.T, cw, nw, nb, nc, ctx_chunks)
            wo = ab_w_out[e].astype(BF16)
            m_has_ctx = True
        else:
            o = layer // 2
            w = attn_w_in[o]
            w_main = jnp.concatenate([_attn_q_cols(w[:, :d]), _pair_cols(w[:, d:d + 2 * LANES]),
                                      w[:, d + 2 * LANES:]], axis=1).astype(BF16)
            lane_w = lambda v: jnp.concatenate([v[:32], v[:32], v[32:], v[32:]])
            nwq = jnp.stack([lane_w(attn_q_norm_w[o]) * (HEAD_DIM ** -0.5), lane_w(attn_k_norm_w[o])]
                            + [jnp.zeros((LANES,), F32)] * 6)
            y = _inproj_attn(xs[0], ab1, w_main, nwq, cos_t, sin_t, tpb)
            merged = _window_attn(y, attn_sink[o].astype(F32), nb, nc, ctx_chunks)
            wo = _attn_o_rows(attn_w_out[o]).astype(BF16)
            m_has_ctx = False
            if not last:
                raise NotImplementedError("attention layers are only supported as the final layer")
        skip = tpb - seq // ROW_TILE if last else 0
        out = _post(xs, merged, prm, wo, wi_all, w2_all, layer, tpb, skip, m_has_ctx)
        xs = (out,)
    return out.reshape(nb, seq, d)
```

```python
import functools

import numpy as np
import jax
import jax.numpy as jnp
from jax import lax
from jax.experimental import pallas as pl
from jax.experimental.pallas import tpu as pltpu

F32 = jnp.float32
BF16 = jnp.bfloat16

HEAD_DIM = 64
CHUNK = 128
GRID_W = 64
ROPE_BASE = 10000.0
EPS = 1e-6
H_KV = 4
LANES = 128
ROW_TILE = 256
TILES_PER_STEP = 2
HALO = 16
NEG = -1e30
VMEM_LIMIT = 56 * 1024 * 1024


def _cparams(*sem):
    return pltpu.CompilerParams(dimension_semantics=sem, vmem_limit_bytes=VMEM_LIMIT)


def _const_spec(shape):
    nd = len(shape)
    return pl.BlockSpec(shape, lambda *_: (0,) * nd, pipeline_mode=pl.Buffered(1))


def _smem_spec():
    return pl.BlockSpec(memory_space=pltpu.SMEM)


def _lane(shape=(CHUNK, LANES)):
    return lax.broadcasted_iota(jnp.int32, shape, len(shape) - 1)


def _dot(a, b):
    return jnp.dot(a, b, preferred_element_type=F32)


def _dot_nt(a, b):
    return lax.dot_general(a, b, (((1,), (1,)), ((), ())), preferred_element_type=F32)


def _dot_tn(a, b):
    return lax.dot_general(a, b, (((0,), (0,)), ((), ())), preferred_element_type=F32)


def _split3(x):
    hi = x.astype(BF16)
    r = x - hi.astype(F32)
    mid = r.astype(BF16)
    lo = (r - mid.astype(F32)).astype(BF16)
    return hi, mid, lo


def _log_sigmoid(x):
    return jnp.minimum(x, 0.0) - jnp.log1p(jnp.exp(-jnp.abs(x)))


def _rope(x, cos, sin_signed):
    return x * cos + pltpu.roll(x, LANES // 2, 1) * sin_signed


def _mod_kernel(rows_ref, w_ref, b_ref, o_ref):
    a = rows_ref[...]
    a = a * jax.nn.sigmoid(a)
    a_hi = a.astype(BF16)
    a_lo = (a - a_hi.astype(F32)).astype(BF16)
    w = w_ref[...]
    w_hi = w.astype(BF16)
    w_lo = (w - w_hi.astype(F32)).astype(BF16)
    o_ref[...] = _dot(a_hi, w_hi) + _dot(a_hi, w_lo) + _dot(a_lo, w_hi) + b_ref[...]


def _modulation(rows, ada_w, ada_b):
    depth, d, n = ada_w.shape
    tn = n // 4
    return pl.pallas_call(
        _mod_kernel,
        grid=(depth, n // tn),
        in_specs=[pl.BlockSpec((8, d), lambda l, j: (0, 0)),
                  pl.BlockSpec((None, d, tn), lambda l, j: (l, 0, j)),
                  pl.BlockSpec((None, 1, tn), lambda l, j: (l, 0, j))],
        out_specs=pl.BlockSpec((None, 8, tn), lambda l, j: (l, 0, j)),
        out_shape=jax.ShapeDtypeStruct((depth, 8, n), F32),
        compiler_params=_cparams("arbitrary", "arbitrary"),
        name="adaln_modulation",
    )(rows, ada_w, ada_b.reshape(depth, 1, n))


def _norm_mod(x, ab_ref):
    ms = jnp.mean(x * x, axis=-1, keepdims=True)
    h = (x * lax.rsqrt(ms + EPS)) * ab_ref[0:1, :] + ab_ref[1:2, :]
    return h.astype(BF16)


def _ctx_or_latent_rows(ctx_ref, x_ref, tile, tiles_per_batch):
    return jnp.where(tile % tiles_per_batch == 0, ctx_ref[...], x_ref[...])


def _split_row_specs(tm, d, tpb, tile_of):
    lat = tpb - 1
    return [pl.BlockSpec((tm, d), lambda i: (tile_of(i) // tpb, 0)),
            pl.BlockSpec((tm, d), lambda i: ((tile_of(i) // tpb) * lat + jnp.maximum(tile_of(i) % tpb - 1, 0), 0))]


def _inproj_ab_kernel(*refs, tpb):
    tps = TILES_PER_STEP
    x_refs, ab_refs, refs = refs[:2 * tps], refs[2 * tps:3 * tps], refs[3 * tps:]
    w_ref, wg_ref, wgt_ref, gb_ref, gbt_ref = refs[:5]
    rope_refs, (y_ref, g_ref, gt_ref) = refs[5:5 + 2 * tps], refs[5 + 2 * tps:]
    tm = y_ref.shape[0] // tps
    hbs = [_norm_mod(_ctx_or_latent_rows(x_refs[2 * u], x_refs[2 * u + 1], pl.program_id(0) * tps + u, tpb),
                     ab_refs[u]) for u in range(tps)]
    gw = 4 * LANES
    for j in range(8):
        for u in range(tps):
            acc = _dot(hbs[u], w_ref[:, j * gw:(j + 1) * gw])
            if j in (0, 1):
                if j == 0:
                    acc = acc * (HEAD_DIM ** -0.5)
                cos, sin = rope_refs[2 * u][...], rope_refs[2 * u + 1][...]
                acc = jnp.concatenate(
                    [_rope(acc[:, p * LANES:(p + 1) * LANES], cos, sin) for p in range(4)], axis=1)
            elif j == 3:
                acc = acc * jax.nn.sigmoid(acc)
            elif j == 7:
                acc = jax.nn.sigmoid(acc)
            y_ref[u * tm:(u + 1) * tm, j * gw:(j + 1) * gw] = acc.astype(BF16)
    for u in range(tps):
        g_ref[u * tm:(u + 1) * tm, :] = _dot(hbs[u], wg_ref[...]) + gb_ref[...]
        gt_ref[:, u * tm:(u + 1) * tm] = _dot_nt(wgt_ref[...], hbs[u]) + gbt_ref[...]


def _tile_specs(tm, tpb, tps):
    at = lambda f, u: (lambda i: f(i * tps + u))
    sel = lambda t: ((t // tpb) * 2 + jnp.minimum(t % tpb, 1), 0, 0)
    mod_spec = lambda d, u: pl.BlockSpec((None, 2, d), at(sel, u))
    rope_spec = lambda u: pl.BlockSpec((tm, LANES), at(lambda t: (t % tpb, 0), u))
    return at, mod_spec, rope_spec


def _inproj_ab(ctx2, x2, ab, w, wg, wgt, gb, gbt, cos, sin, tiles_per_batch):
    d = x2.shape[1]
    r = ctx2.shape[0] + x2.shape[0]
    tm, tps = ROW_TILE, TILES_PER_STEP
    n = w.shape[1]
    tpb = tiles_per_batch
    assert (r // tm) % tps == 0
    at, mod_spec, rope_spec = _tile_specs(tm, tpb, tps)
    x_specs, rope_specs = [], []
    for u in range(tps):
        x_specs += _split_row_specs(tm, d, tpb, at(lambda t: t, u))
        rope_specs += [rope_spec(u), rope_spec(u)]
    return pl.pallas_call(
        functools.partial(_inproj_ab_kernel, tpb=tpb),
        grid=(r // (tm * tps),),
        in_specs=x_specs + [mod_spec(d, u) for u in range(tps)] + [
                  _const_spec(w.shape), _const_spec(wg.shape), _const_spec(wgt.shape),
                  _const_spec(gb.shape), _const_spec(gbt.shape)] + rope_specs,
        out_specs=[pl.BlockSpec((tps * tm, n), lambda i: (i, 0)),
                   pl.BlockSpec((tps * tm, LANES), lambda i: (i, 0)),
                   pl.BlockSpec((32, tps * tm), lambda i: (0, i))],
        out_shape=[jax.ShapeDtypeStruct((r, n), BF16),
                   jax.ShapeDtypeStruct((r, LANES), F32),
                   jax.ShapeDtypeStruct((32, r), F32)],
        compiler_params=_cparams("arbitrary"),
        name="inproj_ret_mlstm",
    )(*([ctx2, x2] * tps), *([ab] * tps), w, wg, wgt, gb, gbt, *([cos, sin] * tps))


def _inproj_attn_kernel(*refs):
    tps = TILES_PER_STEP
    x_ref, ab_refs, refs = refs[0], refs[1:1 + tps], refs[1 + tps:]
    w_ref, nw_ref = refs[:2]
    rope_refs, y_ref = refs[2:2 + 2 * tps], refs[2 + 2 * tps]
    tm = y_ref.shape[0] // tps
    hbs = [_norm_mod(x_ref[u * tm:(u + 1) * tm, :], ab_refs[u]) for u in range(tps)]
    r2 = lax.broadcasted_iota(jnp.int32, (2 * LANES, 2 * LANES), 0)
    c2 = lax.broadcasted_iota(jnp.int32, (2 * LANES, 2 * LANES), 1)
    same_head = (((r2 ^ c2) & (LANES | 32)) == 0).astype(BF16)
    acc_q = [_dot(hbs[u], w_ref[:, 0:8 * LANES]) for u in range(tps)]
    acc_k = [_dot(hbs[u], w_ref[:, 8 * LANES:10 * LANES]) for u in range(tps)]
    acc_v = [_dot(hbs[u], w_ref[:, 10 * LANES:12 * LANES]) for u in range(tps)]
    for j in range(5):
        for u in range(tps):
            acc = acc_q[u][:, j * 2 * LANES:(j + 1) * 2 * LANES] if j < 4 else acc_k[u]
            sq = acc * acc
            hi = sq.astype(BF16)
            lo = (sq - hi.astype(F32)).astype(BF16)
            ms = (_dot(hi, same_head) + _dot(lo, same_head)) * (1.0 / HEAD_DIM)
            nrm = acc * lax.rsqrt(ms + EPS)
            nw = nw_ref[0:1, :] if j < 4 else nw_ref[1:2, :]
            cos, sin = rope_refs[2 * u][...], rope_refs[2 * u + 1][...]
            for v in range(2):
                ls = slice(v * LANES, (v + 1) * LANES)
                y_ref[u * tm:(u + 1) * tm, (2 * j + v) * LANES:(2 * j + v + 1) * LANES] = (
                    _rope(nrm[:, ls] * nw, cos, sin).astype(BF16))
    for u in range(tps):
        y_ref[u * tm:(u + 1) * tm, 10 * LANES:12 * LANES] = acc_v[u].astype(BF16)


def _inproj_attn(x2, ab, w, nw, cos, sin, tiles_per_batch):
    r, d = x2.shape
    tm, tps = ROW_TILE, TILES_PER_STEP
    n = w.shape[1]
    tpb = tiles_per_batch
    assert (r // tm) % tps == 0
    _, mod_spec, rope_spec = _tile_specs(tm, tpb, tps)
    rope_specs = []
    for u in range(tps):
        rope_specs += [rope_spec(u), rope_spec(u)]
    return pl.pallas_call(
        _inproj_attn_kernel,
        grid=(r // (tm * tps),),
        in_specs=[pl.BlockSpec((tps * tm, d), lambda i: (i, 0))] + [mod_spec(d, u) for u in range(tps)]
                 + [_const_spec(w.shape), _const_spec(nw.shape)] + rope_specs,
        out_specs=pl.BlockSpec((tps * tm, n), lambda i: (i, 0)),
        out_shape=jax.ShapeDtypeStruct((r, n), BF16),
        compiler_params=_cparams("arbitrary"),
        name="inproj_attn",
    )(x2, *([ab] * tps), w, nw, *([cos, sin] * tps))


def _conv_silu(cur_ref, prev_ref, next_ref, cw_ref, prev_on, next_on):
    cur = cur_ref[...].astype(F32)
    row = lax.broadcasted_iota(jnp.int32, cur.shape, 0)
    prev_row = prev_ref[HALO - 1:HALO, :].astype(F32) * prev_on
    next_row = next_ref[0:1, :].astype(F32) * next_on
    xm = jnp.where(row == 0, prev_row, pltpu.roll(cur, 1, 0))
    xp = jnp.where(row == CHUNK - 1, next_row, pltpu.roll(cur, CHUNK - 1, 0))
    y = cw_ref[3:4, :] + cw_ref[0:1, :] * xm + cw_ref[1:2, :] * cur + cw_ref[2:3, :] * xp
    return y * jax.nn.sigmoid(y)


def _segment_flags(c, nc, ctx_chunks):
    prev_on = jnp.where((c == 0) | (c == ctx_chunks), 0.0, 1.0).astype(F32)
    next_on = jnp.where((c == ctx_chunks - 1) | (c == nc - 1), 0.0, 1.0).astype(F32)
    return prev_on, next_on


def _cumsum_cols(tri_bf, lf):
    hi, mid, lo = _split3(lf)
    return _dot(tri_bf, hi) + _dot(tri_bf, mid) + _dot(tri_bf, lo)


def _cumsum_rows(lf, tri_bf):
    hi, mid, lo = _split3(lf)
    return _dot(hi, tri_bf) + _dot(mid, tri_bf) + _dot(lo, tri_bf)


def _ret_state_update(s_ref, p, k2, v2, kdec, cd_lanes, bd):
    kf = (k2.astype(F32) * kdec).astype(BF16)
    s_ref[p] = s_ref[p] * cd_lanes + jnp.where(bd, _dot_tn(v2, kf), 0.0)


def _mlstm_state_update(c_ref, n_ref, m_ref, k_pairs, v_pairs, c_all, bend, col0, lo, bd):
    cmax = jnp.max(c_all, axis=0, keepdims=True)
    w_all = jnp.exp(c_all - cmax)
    m_old = m_ref[0:1, :]
    mrel = jnp.maximum(m_old, cmax)
    a_row = jnp.exp(m_old - mrel)
    bb_row = jnp.exp(cmax - mrel)
    m_ref[0:1, :] = bend + mrel
    lo_row = lo[0:1, :]
    for p in range(4):
        h0 = col0 + 2 * p
        kw = k_pairs[p] * jnp.where(lo, w_all[:, h0:h0 + 1], w_all[:, h0 + 1:h0 + 2])
        kvt = _dot_tn(v_pairs[p], kw.astype(BF16))
        nloc = jnp.sum(kw, axis=0, keepdims=True)
        a_l = jnp.where(lo_row, a_row[:, h0:h0 + 1], a_row[:, h0 + 1:h0 + 2])
        bb_l = jnp.where(lo_row, bb_row[:, h0:h0 + 1], bb_row[:, h0 + 1:h0 + 2])
        c_ref[p] = c_ref[p] * a_l + jnp.where(bd, kvt, 0.0) * bb_l
        n_new = (n_ref[p, 0:1, :] + n_ref[p, 1:2, :]) * a_l + nloc * bb_l
        n_ref[p, 0:1, :] = jnp.where(lo_row, n_new, 0.0)
        n_ref[p, 1:2, :] = jnp.where(lo_row, 0.0, n_new)


def _bwd_state_kernel(rk_ref, rv_ref, mk_ref, mkp_ref, mkn_ref, mv_ref, g_ref,
                      lgk_ref, cw_ref,
                      sret_ref, cm_ref, nm_ref, mm_ref,
                      s_scr, c_scr, n_scr, m_scr, kdec_scr, *, nc, ctx_chunks):
    i = pl.program_id(1)
    c = jnp.where(i < ctx_chunks, ctx_chunks - 1 - i, nc + ctx_chunks - 1 - i)
    lane = _lane()
    sub = lax.broadcasted_iota(jnp.int32, (CHUNK, LANES), 0)
    lo = lane < HEAD_DIM
    bd_ret = (sub >= HEAD_DIM) == ((lane & 32) != 0)
    bd_m = (sub >= HEAD_DIM) == (lane >= HEAD_DIM)

    @pl.when(i == 0)
    def _():
        s_scr[...] = jnp.zeros_like(s_scr)
        c_scr[...] = jnp.zeros_like(c_scr)
        n_scr[...] = jnp.zeros_like(n_scr)
        m_scr[...] = jnp.zeros_like(m_scr)
        pos = sub.astype(F32)
        for p in range(4):
            kdec_scr[p] = jnp.exp(lgk_ref[4 + p:5 + p, :] * pos)

    sret_ref[...] = s_scr[...].astype(BF16)
    cm_ref[...] = c_scr[...].astype(BF16)
    nm_ref[...] = n_scr[...].astype(BF16)
    mm_ref[...] = m_scr[...]

    for p in range(4):
        cd = jnp.exp(lgk_ref[4 + p:5 + p, :] * float(CHUNK))
        _ret_state_update(s_scr, p, rk_ref[:, p * LANES:(p + 1) * LANES],
                          rv_ref[:, p * LANES:(p + 1) * LANES], kdec_scr[p], cd, bd_ret)

    prev_on, next_on = _segment_flags(c, nc, ctx_chunks)
    kc = _conv_silu(mk_ref, mkp_ref, mkn_ref, cw_ref, prev_on, next_on)
    g = g_ref[...]
    le_bf = (sub <= lane).astype(BF16)
    bal = pltpu.roll(_cumsum_cols(le_bf, _log_sigmoid(g)), LANES - 8, 1)
    _mlstm_state_update(c_scr, n_scr, m_scr,
                        [kc[:, p * LANES:(p + 1) * LANES] for p in range(4)],
                        [mv_ref[:, p * LANES:(p + 1) * LANES] for p in range(4)],
                        g - bal, bal[0:1, :], 16, lo, bd_m)


def _bwd_states(y, g, lgk, cw_k, nb, nc, ctx_chunks):
    gw = 4 * LANES
    def cidx(b, i):
        c = jnp.where(i < ctx_chunks, ctx_chunks - 1 - i, nc + ctx_chunks - 1 - i)
        return b * nc + c
    hb = CHUNK // HALO
    nhalo = y.shape[0] // HALO
    return pl.pallas_call(
        functools.partial(_bwd_state_kernel, nc=nc, ctx_chunks=ctx_chunks),
        grid=(nb, nc),
        in_specs=[pl.BlockSpec((CHUNK, gw), lambda b, i: (cidx(b, i), 1)),
                  pl.BlockSpec((CHUNK, gw), lambda b, i: (cidx(b, i), 2)),
                  pl.BlockSpec((CHUNK, gw), lambda b, i: (cidx(b, i), 5)),
                  pl.BlockSpec((HALO, gw), lambda b, i: (jnp.maximum(cidx(b, i) * hb - 1, 0), 5)),
                  pl.BlockSpec((HALO, gw), lambda b, i: (jnp.minimum((cidx(b, i) + 1) * hb, nhalo - 1), 5)),
                  pl.BlockSpec((CHUNK, gw), lambda b, i: (cidx(b, i), 6)),
                  pl.BlockSpec((CHUNK, LANES), lambda b, i: (cidx(b, i), 0)),
                  _const_spec(lgk.shape), _const_spec(cw_k.shape)],
        out_specs=[pl.BlockSpec((None, 4, CHUNK, LANES), lambda b, i: (cidx(b, i), 0, 0, 0)),
                   pl.BlockSpec((None, 4, CHUNK, LANES), lambda b, i: (cidx(b, i), 0, 0, 0)),
                   pl.BlockSpec((None, 4, HALO, LANES), lambda b, i: (cidx(b, i), 0, 0, 0)),
                   pl.BlockSpec((None, 8, LANES), lambda b, i: (cidx(b, i), 0, 0))],
        out_shape=[jax.ShapeDtypeStruct((nb * nc, 4, CHUNK, LANES), BF16),
                   jax.ShapeDtypeStruct((nb * nc, 4, CHUNK, LANES), BF16),
                   jax.ShapeDtypeStruct((nb * nc, 4, HALO, LANES), BF16),
                   jax.ShapeDtypeStruct((nb * nc, 8, LANES), F32)],
        scratch_shapes=[pltpu.VMEM((4, CHUNK, LANES), F32), pltpu.VMEM((4, CHUNK, LANES), F32),
                        pltpu.VMEM((4, HALO, LANES), F32), pltpu.VMEM((8, LANES), F32),
                        pltpu.VMEM((4, CHUNK, LANES), F32)],
        compiler_params=_cparams("arbitrary", "arbitrary"),
        name="bwd_state_sweep",
    )(y, y, y, y, y, y, g, lgk, cw_k)


def _mlstm_dir_weights(st, qn_row, c_col, bt_row, m_prev, tri):
    dl = jnp.where(tri, c_col + bt_row, NEG)
    mx = jnp.max(dl, axis=0, keepdims=True)
    al = bt_row + m_prev
    m_t = jnp.maximum(al, mx)
    w = jnp.exp(dl - m_t)
    a_t = jnp.exp(al - m_t)
    sw = st * w
    den = jnp.sum(sw, axis=0, keepdims=True) + a_t * qn_row
    r = 1.0 / jnp.maximum(jnp.abs(den), jnp.exp(-m_t))
    return sw * r, a_t * r


def _heads_out(ht, nw_tab):
    rows = []
    for a in range(2):
        ha = ht[a * HEAD_DIM:(a + 1) * HEAD_DIM, :]
        ms = jnp.mean(ha * ha, axis=0, keepdims=True)
        rows.append(ha * lax.rsqrt(ms + EPS))
    return (jnp.concatenate(rows, axis=0) * nw_tab).T


def _mixer_kernel(lg_ref, rq_ref, rk_ref, rv_ref, rg_ref, mqk_ref, mqkp_ref, mqkn_ref, mv_ref, mo_ref,
                  g_ref, gt_ref, sretb_ref, cmb_ref, nmb_ref, mmb_ref,
                  lgk_ref, lgkt_ref, cw_ref, nw_ref,
                  out_ref,
                  s_scr, c_scr, n_scr, m_scr, dm_scr, dec_scr, *, nc, ctx_chunks):
    c = pl.program_id(1)
    lane = _lane()
    sub = lax.broadcasted_iota(jnp.int32, (CHUNK, LANES), 0)
    lo = lane < HEAD_DIM
    sub_lo = sub < HEAD_DIM
    mask_ret = [(lane & 32) == 0, (lane & 32) != 0]
    mask_nat = [lo, lane >= HEAD_DIM]
    bd_ret = (sub >= HEAD_DIM) == ((lane & 32) != 0)
    bd_m = (sub >= HEAD_DIM) == (lane >= HEAD_DIM)
    le = sub <= lane
    ge = sub >= lane

    @pl.when(c == 0)
    def _():
        s_scr[...] = jnp.zeros_like(s_scr)
        c_scr[...] = jnp.zeros_like(c_scr)
        n_scr[...] = jnp.zeros_like(n_scr)
        m_scr[...] = jnp.zeros_like(m_scr)
        spos = sub.astype(F32)
        tpos = lane.astype(F32)
        diff = (lane - sub).astype(F32)
        for h in range(8):
            dm_scr[h] = (jnp.where(le, jnp.exp(lg_ref[h] * diff), 0.0)
                         + jnp.where(ge, jnp.exp(lg_ref[8 + h] * (-diff)), 0.0))
        for p in range(4):
            dec_scr[0, p] = jnp.exp(lgkt_ref[:, p:p + 1] * (tpos + 1.0))
            dec_scr[1, p] = jnp.exp(lgkt_ref[:, 4 + p:5 + p] * (float(CHUNK) - tpos))
            dec_scr[2, p] = jnp.exp(lgk_ref[p:p + 1, :] * (float(CHUNK) - 1.0 - spos))


    prev_on, next_on = _segment_flags(c, nc, ctx_chunks)
    qk = _conv_silu(mqk_ref, mqkp_ref, mqkn_ref, cw_ref, prev_on, next_on)
    g = g_ref[...]
    gt = gt_ref[...]
    lf_col = _log_sigmoid(g)
    lf_row = _log_sigmoid(gt)
    le_bf = le.astype(BF16)
    ge_bf = ge.astype(BF16)
    bal_f = pltpu.roll(_cumsum_cols(ge_bf, lf_col), LANES - 8, 1)
    bal_b = pltpu.roll(_cumsum_cols(le_bf, lf_col), LANES - 8, 1)
    cf_all = g - bal_f
    cb_all = g - bal_b
    bf_row = _cumsum_rows(lf_row, le_bf)
    bb_row = _cumsum_rows(lf_row, ge_bf)

    qb, kb, vb, kf32, qt, vbd = [], [], [], [], [], []
    for p in range(8):
        sl = slice((p % 4) * LANES, (p % 4 + 1) * LANES)
        if p < 4:
            q2, k2, v2 = rq_ref[:, sl], rk_ref[:, sl], rv_ref[:, sl]
            qf = q2.astype(F32)
            kf = None
            old =jnp.concatenate([s_scr[p].astype(BF16), sretb_ref[p]], axis=1)
        else:
            qf = qk[:, sl] * (HEAD_DIM ** -0.5)
            kf = qk[:, 4 * LANES + (p - 4) * LANES:4 * LANES + (p - 3) * LANES]
            q2, k2, v2 = qf.astype(BF16), kf.astype(BF16), mv_ref[:, sl]
            old = jnp.concatenate([c_scr[p - 4].astype(BF16), cmb_ref[p - 4]], axis=1)
        vt = v2.astype(F32).T.astype(BF16)
        qb.append(q2)
        kb.append(k2)
        vb.append(v2)
        kf32.append(kf)
        qt.append(qf.T)
        vbd.append(jnp.concatenate([jnp.where(sub_lo, vt, jnp.zeros_like(vt)),
                                    jnp.where(sub_lo, jnp.zeros_like(vt), vt), old], axis=1))

    st2, qn_f, qn_b = [], [], []
    for p in range(8):
        masks = mask_ret if p < 4 else mask_nat
        zero = jnp.zeros_like(qb[p])
        qstack = jnp.concatenate([jnp.where(masks[0], qb[p], zero), jnp.where(masks[1], qb[p], zero)], axis=0)
        st2.append(_dot_nt(kb[p], qstack))
        if p >= 4:
            qn_f.append(_dot_nt(n_scr[p - 4].astype(BF16), qb[p]))
            qn_b.append(_dot_nt(nmb_ref[p - 4], qb[p]))

    rhs = []
    for p in range(8):
        if p < 4:
            pts = [(st2[p][:, a * LANES:(a + 1) * LANES] * dm_scr[2 * p + a]).astype(BF16) for a in range(2)]
            x_f, x_b = dec_scr[0, p], dec_scr[1, p]
        else:
            pts, cf, cb = [], [], []
            for a in range(2):
                h = 2 * (p - 4) + a
                st = st2[p][:, a * LANES:(a + 1) * LANES]
                pf, coef_f = _mlstm_dir_weights(st, qn_f[p - 4][a:a + 1, :], cf_all[:, h:h + 1],
                                                bf_row[8 + h:9 + h, :], m_scr[0:1, h:h + 1], le)
                pb, coef_b = _mlstm_dir_weights(st, qn_b[p - 4][a:a + 1, :], cb_all[:, 16 + h:17 + h],
                                                bb_row[24 + h:25 + h, :], mmb_ref[0:1, 16 + h:17 + h], ge)
                pts.append((pf + pb).astype(BF16))
                cf.append(coef_f)
                cb.append(coef_b)
            x_f = jnp.where(sub_lo, cf[0], cf[1])
            x_b = jnp.where(sub_lo, cb[0], cb[1])
        rhs.append(jnp.concatenate(pts + [(qt[p] * x_f).astype(BF16), (qt[p] * x_b).astype(BF16)], axis=0))

    ht = [_dot(vbd[p], rhs[p]) for p in range(8)]

    for p in range(8):
        sl = slice((p % 4) * LANES, (p % 4 + 1) * LANES)
        gate_ref = rg_ref if p < 4 else mo_ref
        y = _heads_out(ht[p], nw_ref[p])
        out_ref[:, p * LANES:(p + 1) * LANES] = (y * gate_ref[:, sl].astype(F32)).astype(BF16)

    for p in range(4):
        cd = jnp.exp(lgk_ref[p:p + 1, :] * float(CHUNK))
        _ret_state_update(s_scr, p, kb[p], vb[p], dec_scr[2, p], cd, bd_ret)
    _mlstm_state_update(c_scr, n_scr, m_scr, kf32[4:], vb[4:], cf_all, bal_f[CHUNK - 1:CHUNK, :], 0, lo, bd_m)


def _mixer(y, g, gt, states, lg_smem, lgk, lgkt, cw_qk, nw, nb, nc, ctx_chunks):
    gw = 4 * LANES
    sretb, cmb, nmb, mmb = states
    rc = lambda b, c: b * nc + c
    hb = CHUNK // HALO
    nhalo = y.shape[0] // HALO
    blk = lambda j: pl.BlockSpec((CHUNK, gw), lambda b, c: (rc(b, c), j))
    return pl.pallas_call(
        functools.partial(_mixer_kernel, nc=nc, ctx_chunks=ctx_chunks),
        grid=(nb, nc),
        in_specs=[_smem_spec(),
                  blk(0), blk(1), blk(2), blk(3),
                  pl.BlockSpec((CHUNK, 2 * gw), lambda b, c: (rc(b, c), 2)),
                  pl.BlockSpec((HALO, 2 * gw), lambda b, c: (jnp.maximum(rc(b, c) * hb - 1, 0), 2)),
                  pl.BlockSpec((HALO, 2 * gw), lambda b, c: (jnp.minimum((rc(b, c) + 1) * hb, nhalo - 1), 2)),
                  blk(6), blk(7),
                  pl.BlockSpec((CHUNK, LANES), lambda b, c: (rc(b, c), 0)),
                  pl.BlockSpec((32, CHUNK), lambda b, c: (0, rc(b, c))),
                  pl.BlockSpec((None, 4, CHUNK, LANES), lambda b, c: (rc(b, c), 0, 0, 0)),
                  pl.BlockSpec((None, 4, CHUNK, LANES), lambda b, c: (rc(b, c), 0, 0, 0)),
                  pl.BlockSpec((None, 4, HALO, LANES), lambda b, c: (rc(b, c), 0, 0, 0)),
                  pl.BlockSpec((None, 8, LANES), lambda b, c: (rc(b, c), 0, 0)),
                  _const_spec(lgk.shape), _const_spec(lgkt.shape), _const_spec(cw_qk.shape),
                  _const_spec(nw.shape)],
        out_specs=pl.BlockSpec((CHUNK, 2 * gw), lambda b, c: (rc(b, c), 0)),
        out_shape=jax.ShapeDtypeStruct((nb * nc * CHUNK, 2 * gw), BF16),
        scratch_shapes=[pltpu.VMEM((4, CHUNK, LANES), F32), pltpu.VMEM((4, CHUNK, LANES), F32),
                        pltpu.VMEM((4, HALO, LANES), F32), pltpu.VMEM((8, LANES), F32),
                        pltpu.VMEM((8, CHUNK, LANES), F32), pltpu.VMEM((3, 4, CHUNK, LANES), F32)],
        compiler_params=_cparams("arbitrary", "arbitrary"),
        name="ret_mlstm_mixer",
    )(lg_smem, y, y, y, y, y, y, y, y, y, g, gt, sretb, cmb, nmb, mmb, lgk, lgkt, cw_qk, nw)


def _attn_kernel(sink_ref, q_ref, kp_ref, kc_ref, kn_ref, kx_ref, vp_ref, vc_ref, vn_ref, vx_ref,
                 bias_ref, o_ref):
    grp = lax.broadcasted_iota(jnp.int32, (1, 4 * CHUNK), 1) // CHUNK
    lane = _lane()
    mask_q = [(lane & 32) == 0, (lane & 32) != 0]
    bias = jnp.concatenate([bias_ref[...]] * 4, axis=1)
    vts, sts = [], []
    for kvp in range(2):
        sl = slice(kvp * LANES, (kvp + 1) * LANES)
        v_blocks = [vp_ref[:, sl], vc_ref[:, sl], vn_ref[:, sl], vx_ref[:, sl]]
        vts.append(jnp.concatenate(
            [vb[r * CHUNK:(r + 1) * CHUNK, :].astype(F32).T.astype(BF16)
             for vb in v_blocks for r in range(vb.shape[0] // CHUNK)], axis=1))
    for kvp in range(2):
        sl = slice(kvp * LANES, (kvp + 1) * LANES)
        kcat = jnp.concatenate([kp_ref[:, sl], kc_ref[:, sl], kn_ref[:, sl], kx_ref[:, sl]], axis=0)
        for a in range(2):
            qs = jnp.concatenate(
                [jnp.where(mask_q[a], q_ref[:, (kvp * 4 + g) * LANES:(kvp * 4 + g + 1) * LANES],
                           jnp.zeros((CHUNK, LANES), BF16)) for g in range(4)], axis=0)
            sts.append(_dot_nt(kcat, qs))
    outs = []
    for kv in range(4):
        st = sts[kv]
        st = jnp.concatenate([st[0:CHUNK] + bias[0:CHUNK], st[CHUNK:2 * CHUNK],
                              st[2 * CHUNK:3 * CHUNK] + bias[CHUNK:2 * CHUNK], st[3 * CHUNK:]], axis=0)
        snk = jnp.where(grp == 0, sink_ref[kv * 4],
                        jnp.where(grp == 1, sink_ref[kv * 4 + 1],
                                  jnp.where(grp == 2, sink_ref[kv * 4 + 2], sink_ref[kv * 4 + 3])))
        m = jnp.maximum(jnp.max(st, axis=0, keepdims=True), snk)
        e = jnp.exp(st - m)
        denom = jnp.exp(snk - m) + jnp.sum(e, axis=0, keepdims=True)
        a = kv % 2
        outs.append(_dot(vts[kv // 2][a * HEAD_DIM:(a + 1) * HEAD_DIM, :], e.astype(BF16)) * (1.0 / denom))
    for kvp in range(2):
        full = jnp.concatenate(outs[2 * kvp:2 * kvp + 2], axis=0)
        for g in range(4):
            o_ref[:, (kvp * 4 + g) * LANES:(kvp * 4 + g + 1) * LANES] = (
                full[:, g * CHUNK:(g + 1) * CHUNK].T.astype(BF16))


def _window_bias():
    kk = np.arange(CHUNK)[:, None]
    t = np.arange(CHUNK)[None, :]
    tabs = []
    for has_prev, has_next in ((False, True), (True, True), (True, False)):
        prev_ok = (kk >= t) & has_prev
        next_ok = (kk <= t) & has_next
        tabs.append(np.where(np.concatenate([prev_ok, next_ok], axis=0), 0.0, NEG))
    return jnp.asarray(np.stack(tabs), F32)


def _window_attn(y, sink, nb, nc, ctx_chunks):
    nq = nc - ctx_chunks
    assert nq >= 2
    ctx_len = ctx_chunks * CHUNK
    bias = _window_bias()
    d = 8 * LANES
    kcol, vcol = d // (2 * LANES), d // (2 * LANES) + 1
    qrow = lambda b, i: b * nc + ctx_chunks + i
    prow = lambda b, i: b * nc + ctx_chunks + jnp.maximum(i - 1, 0)
    nrow = lambda b, i: b * nc + ctx_chunks + jnp.minimum(i + 1, nq - 1)
    xrow = lambda b, i: (b * nc * CHUNK) // ctx_len
    kv_spec = lambda rowf, col: pl.BlockSpec((CHUNK, 2 * LANES), lambda b, i: (rowf(b, i), col))
    x_spec = lambda col: pl.BlockSpec((ctx_len, 2 * LANES), lambda b, i: (xrow(b, i), col))
    bias_spec = pl.BlockSpec((None,) + bias.shape[1:],
                             lambda b, i: (jnp.where(i == 0, 0, jnp.where(i == nq - 1, 2, 1)), 0, 0))
    return pl.pallas_call(
        _attn_kernel,
        grid=(nb, nq),
        in_specs=[_smem_spec(),
                  pl.BlockSpec((CHUNK, d), lambda b, i: (qrow(b, i), 0)),
                  kv_spec(prow, kcol), kv_spec(qrow, kcol), kv_spec(nrow, kcol), x_spec(kcol),
                  kv_spec(prow, vcol), kv_spec(qrow, vcol), kv_spec(nrow, vcol), x_spec(vcol),
                  bias_spec],
        out_specs=pl.BlockSpec((CHUNK, d), lambda b, i: (b * nq + i, 0)),
        out_shape=jax.ShapeDtypeStruct((nb * nq * CHUNK, d), BF16),
        compiler_params=_cparams("arbitrary", "arbitrary"),
        name="window_gqa",
    )(sink, y, y, y, y, y, y, y, y, y, bias)


def _post_kernel(*refs, d_ff, split_tpb):
    tps = TILES_PER_STEP
    n_x = 2 if split_tpb else 1
    x_refs, refs = refs[:n_x * tps], refs[n_x * tps:]
    m_refs, p_refs = refs[:tps], refs[tps:2 * tps]
    wo_ref, wi_ref, w2_ref, o_ref, act_scr = refs[2 * tps:]
    tm = m_refs[0].shape[0]
    ys = [_dot(m_refs[u][...], wo_ref[...]) for u in range(tps)]
    x1s, hs = [], []
    for u in range(tps):
        if split_tpb:
            x = _ctx_or_latent_rows(x_refs[2 * u], x_refs[2 * u + 1], pl.program_id(0) * tps + u, split_tpb)
        else:
            x = x_refs[u][...]
        p_ref = p_refs[u]
        x1 = x + p_ref[0:1, :] * ys[u]
        ms = jnp.mean(x1 * x1, axis=-1, keepdims=True)
        hs.append(((x1 * lax.rsqrt(ms + EPS)) * p_ref[1:2, :] + p_ref[2:3, :]).astype(BF16))
        x1s.append(x1)
    for u in range(tps):
        gate = _dot(hs[u], wi_ref[:, :d_ff])
        up = _dot(hs[u], wi_ref[:, d_ff:])
        act_scr[u] = (gate * jax.nn.sigmoid(gate) * up).astype(BF16)
    for u in range(tps):
        o_ref[u * tm:(u + 1) * tm, :] = x1s[u] + p_refs[u][3:4, :] * _dot(act_scr[u], w2_ref[...])


def _post(xs, m2, prm, wo, wi_all, w2_all, layer, tiles_per_batch, skip_tiles, m_has_ctx):
    split = len(xs) == 2
    d = xs[-1].shape[1]
    r = sum(a.shape[0] for a in xs)
    tm = ROW_TILE
    tps = TILES_PER_STEP
    d_ff = w2_all.shape[1]
    tpb = tiles_per_batch
    kept = tpb - skip_tiles
    nb = r // (tm * tpb)
    assert not (split and skip_tiles) and (nb * kept) % tps == 0
    xrow = lambda t: (t // kept) * tpb + skip_tiles + t % kept
    mrow = xrow if m_has_ctx else (lambda t: t)
    sel = lambda t: ((t // kept) * 2 + jnp.minimum(skip_tiles + t % kept, 1), 0, 0)
    at = lambda f, u: (lambda i: f(i * tps + u))
    row_spec = lambda f, u: pl.BlockSpec((tm, d), lambda i: (f(i * tps + u), 0))
    x_specs = []
    for u in range(tps):
        x_specs += _split_row_specs(tm, d, tpb, at(lambda t: t, u)) if split else [row_spec(xrow, u)]
    layer_spec = lambda a: pl.BlockSpec((None,) + a.shape[1:], lambda i: (layer, 0, 0),
                                        pipeline_mode=pl.Buffered(1))
    return pl.pallas_call(
        functools.partial(_post_kernel, d_ff=d_ff, split_tpb=tpb if split else 0),
        grid=(nb * kept // tps,),
        in_specs=x_specs + [row_spec(mrow, u) for u in range(tps)]
                 + [pl.BlockSpec((None, 8, d), at(sel, u)) for u in range(tps)]
                 + [_const_spec(wo.shape), layer_spec(wi_all), layer_spec(w2_all)],
        out_specs=pl.BlockSpec((tps * tm, d), lambda i: (i, 0)),
        out_shape=jax.ShapeDtypeStruct((nb * kept * tm, d), F32),
        scratch_shapes=[pltpu.VMEM((tps, tm, d_ff), BF16)],
        compiler_params=_cparams("arbitrary"),
        name="outproj_swiglu",
    )(*(list(xs) * tps), *([m2] * tps), *([prm] * tps), wo, wi_all, w2_all)


def _pair_cols(w):
    rows, cols = w.shape
    return w.reshape(rows, cols // LANES, 2, 2, 32).transpose(0, 1, 3, 2, 4).reshape(rows, cols)


def _attn_q_cols(w):
    rows = w.shape[0]
    g_per = w.shape[1] // (H_KV * HEAD_DIM)
    return (w.reshape(rows, H_KV // 2, 2, g_per, 2, 32).transpose(0, 1, 3, 4, 2, 5)
            .reshape(rows, w.shape[1]))


def _attn_o_rows(w):
    cols = w.shape[1]
    g_per = w.shape[0] // (H_KV * HEAD_DIM)
    return (w.reshape(H_KV // 2, 2, g_per, HEAD_DIM, cols).transpose(0, 2, 1, 3, 4)
            .reshape(w.shape[0], cols))


def _rope_tables(seq, ctx_len):
    rows = seq // GRID_W
    row = np.repeat(np.arange(rows, dtype=np.float32), GRID_W)
    col = np.tile(np.arange(GRID_W, dtype=np.float32), rows)
    n = HEAD_DIM // 4
    inv = (np.float32(ROPE_BASE) ** (-np.arange(n, dtype=np.float32) / np.float32(n))).astype(np.float32)
    ang = np.concatenate([row[:, None] * inv, col[:, None] * inv], axis=-1).astype(np.float32)
    cos, sin = np.cos(ang), np.sin(ang)
    cos_t = np.concatenate([np.ones((ctx_len, LANES), np.float32), np.tile(cos, (1, 4))], axis=0)
    sin_t = np.concatenate([np.zeros((ctx_len, LANES), np.float32),
                            np.concatenate([-sin, -sin, sin, sin], axis=-1)], axis=0)
    return jnp.asarray(cos_t, F32), jnp.asarray(sin_t, F32)


def _mod_tables(mod, nb, norm_w):
    d = norm_w.shape[-1]
    lat = mod[:nb].reshape(nb, 6, d)
    ctx = jnp.broadcast_to(mod[nb].reshape(1, 6, d), (nb, 6, d))
    both = jnp.stack([ctx, lat], axis=1).reshape(nb * 2, 6, d)
    sh1, sc1, g1, sh2, sc2, g2 = [both[:, k] for k in range(6)]
    ab1 = jnp.stack([norm_w[0] * (1.0 + sc1), sh1], axis=1)
    zeros = jnp.zeros_like(g1)
    prm = jnp.stack([g1, norm_w[1] * (1.0 + sc2), sh2, g2, zeros, zeros, zeros, zeros], axis=1)
    return ab1, prm


def kernel(x, c, ctx, c_ctx, ada_w, ada_b, norm_w, ffn_w_in, ffn_w_out, ab_w_in, ab_w_out,
           ret_log_gamma, ret_norm_w, mlstm_conv_w, mlstm_conv_b, mlstm_gate_b, mlstm_norm_w,
           attn_w_in, attn_w_out, attn_q_norm_w, attn_k_norm_w, attn_sink):
    nb, seq, d = x.shape
    ctx_len = ctx.shape[1]
    depth = ada_w.shape[0]
    assert ctx_len == ROW_TILE and seq % ROW_TILE == 0 and d == 8 * LANES and nb < 8
    t_all = ctx_len + seq
    nc = t_all // CHUNK
    ctx_chunks = ctx_len // CHUNK
    tpb = t_all // ROW_TILE
    dr = d // 2

    rows = jnp.zeros((8, d), F32).at[:nb].set(c).at[nb].set(c_ctx)
    mod_all = _modulation(rows, ada_w, ada_b)
    cos_t, sin_t = _rope_tables(seq, ctx_len)
    wi_all = ffn_w_in.astype(BF16)
    w2_all = ffn_w_out.astype(BF16)
    xs = (ctx.reshape(nb * ctx_len, d), x.reshape(nb * seq, d))

    out = None
    for layer in range(depth):
        last = layer == depth - 1
        ab1, prm = _mod_tables(mod_all[layer], nb, norm_w[layer])
        if layer % 2 == 0:
            assert layer == 0
            e = layer // 2
            w = ab_w_in[e]
            w_main = jnp.concatenate([_pair_cols(w[:, :dr]), _pair_cols(w[:, dr:2 * dr]), w[:, 2 * dr:8 * dr]],
                                     axis=1).astype(BF16)
            wg = jnp.zeros((d, LANES), F32).at[:, :32].set(w[:, 8 * dr:]).astype(BF16)
            wgt = w[:, 8 * dr:].T.astype(BF16)
            gb = jnp.zeros((1, LANES), F32).at[0, :32].set(mlstm_gate_b[e].reshape(-1))
            gbt = mlstm_gate_b[e].reshape(32, 1)
            y, g, gt = _inproj_ab(*xs, ab1, w_main, wg, wgt, gb, gbt, cos_t, sin_t, tpb)

            lg = ret_log_gamma[e].astype(F32)
            lgk = jnp.tile(jnp.repeat(lg.reshape(2, 4, 2), 32, axis=-1), (1, 1, 2)).reshape(8, LANES)
            cw =jnp.concatenate([mlstm_conv_w[e], mlstm_conv_b[e][None],
                                  jnp.zeros((4, 2 * dr), F32)], axis=0)
            nw = jnp.broadcast_to(jnp.concatenate([ret_norm_w[e], mlstm_norm_w[e]]).reshape(8, LANES, 1),
                                  (8, LANES, LANES))
            lg_s = lg.reshape(-1)
            states = _bwd_states(y, g, lgk, cw[:, dr:], nb, nc, ctx_chunks)
            merged = _mixer(y, g, gt, states, lg_s, lgk, lgk.T, cw, nw, nb, nc, ctx_chunks)
            wo = ab_w_out[e].astype(BF16)
            m_has_ctx = True
        else:
            o = layer // 2
            w = attn_w_in[o]
            w_main = jnp.concatenate([_attn_q_cols(w[:, :d]), _pair_cols(w[:, d:d + 2 * LANES]),
                                      w[:, d + 2 * LANES:]], axis=1).astype(BF16)
            lane_w = lambda v: jnp.concatenate([v[:32], v[:32], v[32:], v[32:]])
            nwq = jnp.stack([lane_w(attn_q_norm_w[o]) * (HEAD_DIM ** -0.5), lane_w(attn_k_norm_w[o])]
                            + [jnp.zeros((LANES,), F32)] * 6)
            y = _inproj_attn(xs[0], ab1, w_main, nwq, cos_t, sin_t, tpb)
            merged = _window_attn(y, attn_sink[o].astype(F32), nb, nc, ctx_chunks)
            wo = _attn_o_rows(attn_w_out[o]).astype(BF16)
            m_has_ctx = False
            if not last:
                raise NotImplementedError("attention layers are only supported as the final layer")
        skip = tpb - seq // ROW_TILE if last else 0
        out = _post(xs, merged, prm, wo, wi_all, w2_all, layer, tpb, skip, m_has_ctx)
        xs = (out,)
    return out.reshape(nb, seq, d)
```

```python
import functools

import numpy as np
import jax
import jax.numpy as jnp
from jax import lax
from jax.experimental import pallas as pl
from jax.experimental.pallas import tpu as pltpu

F32 = jnp.float32
BF16 = jnp.bfloat16

HEAD_DIM = 64
CHUNK = 128
GRID_W = 64
ROPE_BASE = 10000.0
EPS = 1e-6
H_KV = 4
LANES = 128
ROW_TILE = 256
TILES_PER_STEP = 2
HALO = 16
NEG = -1e30
VMEM_LIMIT = 56 * 1024 * 1024


def _cparams(*sem):
    return pltpu.CompilerParams(dimension_semantics=sem, vmem_limit_bytes=VMEM_LIMIT)


def _const_spec(shape):
    nd = len(shape)
    return pl.BlockSpec(shape, lambda *_: (0,) * nd, pipeline_mode=pl.Buffered(1))


def _smem_spec():
    return pl.BlockSpec(memory_space=pltpu.SMEM)


def _lane(shape=(CHUNK, LANES)):
    return lax.broadcasted_iota(jnp.int32, shape, len(shape) - 1)


def _dot(a, b):
    return jnp.dot(a, b, preferred_element_type=F32)


def _dot_nt(a, b):
    return lax.dot_general(a, b, (((1,), (1,)), ((), ())), preferred_element_type=F32)


def _dot_tn(a, b):
    return lax.dot_general(a, b, (((0,), (0,)), ((), ())), preferred_element_type=F32)


def _split3(x):
    hi = x.astype(BF16)
    r = x - hi.astype(F32)
    mid = r.astype(BF16)
    lo = (r - mid.astype(F32)).astype(BF16)
    return hi, mid, lo


def _log_sigmoid(x):
    return jnp.minimum(x, 0.0) - jnp.log1p(jnp.exp(-jnp.abs(x)))


def _rope(x, cos, sin_signed):
    return x * cos + pltpu.roll(x, LANES // 2, 1) * sin_signed


def _mod_kernel(rows_ref, w_ref, b_ref, o_ref):
    a = rows_ref[...]
    a = a * jax.nn.sigmoid(a)
    a_hi = a.astype(BF16)
    a_lo = (a - a_hi.astype(F32)).astype(BF16)
    w = w_ref[...]
    w_hi = w.astype(BF16)
    w_lo = (w - w_hi.astype(F32)).astype(BF16)
    o_ref[...] = _dot(a_hi, w_hi) + _dot(a_hi, w_lo) + _dot(a_lo, w_hi) + b_ref[...]


def _modulation(rows, ada_w, ada_b):
    depth, d, n = ada_w.shape
    tn = n // 4
    return pl.pallas_call(
        _mod_kernel,
        grid=(depth, n // tn),
        in_specs=[pl.BlockSpec((8, d), lambda l, j: (0, 0)),
                  pl.BlockSpec((None, d, tn), lambda l, j: (l, 0, j)),
                  pl.BlockSpec((None, 1, tn), lambda l, j: (l, 0, j))],
        out_specs=pl.BlockSpec((None, 8, tn), lambda l, j: (l, 0, j)),
        out_shape=jax.ShapeDtypeStruct((depth, 8, n), F32),
        compiler_params=_cparams("arbitrary", "arbitrary"),
        name="adaln_modulation",
    )(rows, ada_w, ada_b.reshape(depth, 1, n))


def _norm_mod(x, ab_ref):
    ms = jnp.mean(x * x, axis=-1, keepdims=True)
    h = (x * lax.rsqrt(ms + EPS)) * ab_ref[0:1, :] + ab_ref[1:2, :]
    return h.astype(BF16)


def _ctx_or_latent_rows(ctx_ref, x_ref, tile, tiles_per_batch):
    return jnp.where(tile % tiles_per_batch == 0, ctx_ref[...], x_ref[...])


def _split_row_specs(tm, d, tpb, tile_of):
    lat = tpb - 1
    return [pl.BlockSpec((tm, d), lambda i: (tile_of(i) // tpb, 0)),
            pl.BlockSpec((tm, d), lambda i: ((tile_of(i) // tpb) * lat + jnp.maximum(tile_of(i) % tpb - 1, 0), 0))]


def _inproj_ab_kernel(*refs, tpb):
    tps = TILES_PER_STEP
    x_refs, ab_refs, refs = refs[:2 * tps], refs[2 * tps:3 * tps], refs[3 * tps:]
    w_ref, wg_ref, wgt_ref, gb_ref, gbt_ref = refs[:5]
    rope_refs, (y_ref, g_ref, gt_ref) = refs[5:5 + 2 * tps], refs[5 + 2 * tps:]
    tm = y_ref.shape[0] // tps
    hbs = [_norm_mod(_ctx_or_latent_rows(x_refs[2 * u], x_refs[2 * u + 1], pl.program_id(0) * tps + u, tpb),
                     ab_refs[u]) for u in range(tps)]
    gw = 4 * LANES
    for j in range(8):
        for u in range(tps):
            acc = _dot(hbs[u], w_ref[:, j * gw:(j + 1) * gw])
            if j in (0, 1):
                if j == 0:
                    acc = acc * (HEAD_DIM ** -0.5)
                cos, sin = rope_refs[2 * u][...], rope_refs[2 * u + 1][...]
                acc = jnp.concatenate(
                    [_rope(acc[:, p * LANES:(p + 1) * LANES], cos, sin) for p in range(4)], axis=1)
            elif j == 3:
                acc = acc * jax.nn.sigmoid(acc)
            elif j == 7:
                acc = jax.nn.sigmoid(acc)
            y_ref[u * tm:(u + 1) * tm, j * gw:(j + 1) * gw] = acc.astype(BF16)
    for u in range(tps):
        g_ref[u * tm:(u + 1) * tm, :] = _dot(hbs[u], wg_ref[...]) + gb_ref[...]
        gt_ref[:, u * tm:(u + 1) * tm] = _dot_nt(wgt_ref[...], hbs[u]) + gbt_ref[...]


def _tile_specs(tm, tpb, tps):
    at = lambda f, u: (lambda i: f(i * tps + u))
    sel = lambda t: ((t // tpb) * 2 + jnp.minimum(t % tpb, 1), 0, 0)
    mod_spec = lambda d, u: pl.BlockSpec((None, 2, d), at(sel, u))
    rope_spec = lambda u: pl.BlockSpec((tm, LANES), at(lambda t: (t % tpb, 0), u))
    return at, mod_spec, rope_spec


def _inproj_ab(ctx2, x2, ab, w, wg, wgt, gb, gbt, cos, sin, tiles_per_batch):
    d = x2.shape[1]
    r = ctx2.shape[0] + x2.shape[0]
    tm, tps = ROW_TILE, TILES_PER_STEP
    n = w.shape[1]
    tpb = tiles_per_batch
    assert (r // tm) % tps == 0
    at, mod_spec, rope_spec = _tile_specs(tm, tpb, tps)
    x_specs, rope_specs = [], []
    for u in range(tps):
        x_specs += _split_row_specs(tm, d, tpb, at(lambda t: t, u))
        rope_specs += [rope_spec(u), rope_spec(u)]
    return pl.pallas_call(
        functools.partial(_inproj_ab_kernel, tpb=tpb),
        grid=(r // (tm * tps),),
        in_specs=x_specs + [mod_spec(d, u) for u in range(tps)] + [
                  _const_spec(w.shape), _const_spec(wg.shape), _const_spec(wgt.shape),
                  _const_spec(gb.shape), _const_spec(gbt.shape)] + rope_specs,
        out_specs=[pl.BlockSpec((tps * tm, n), lambda i: (i, 0)),
                   pl.BlockSpec((tps * tm, LANES), lambda i: (i, 0)),
                   pl.BlockSpec((32, tps * tm), lambda i: (0, i))],
        out_shape=[jax.ShapeDtypeStruct((r, n), BF16),
                   jax.ShapeDtypeStruct((r, LANES), F32),
                   jax.ShapeDtypeStruct((32, r), F32)],
        compiler_params=_cparams("arbitrary"),
        name="inproj_ret_mlstm",
    )(*([ctx2, x2] * tps), *([ab] * tps), w, wg, wgt, gb, gbt, *([cos, sin] * tps))


def _inproj_attn_kernel(*refs):
    tps = TILES_PER_STEP
    x_ref, ab_refs, refs = refs[0], refs[1:1 + tps], refs[1 + tps:]
    w_ref, nw_ref = refs[:2]
    rope_refs, y_ref = refs[2:2 + 2 * tps], refs[2 + 2 * tps]
    tm = y_ref.shape[0] // tps
    hbs = [_norm_mod(x_ref[u * tm:(u + 1) * tm, :], ab_refs[u]) for u in range(tps)]
    r2 = lax.broadcasted_iota(jnp.int32, (2 * LANES, 2 * LANES), 0)
    c2 = lax.broadcasted_iota(jnp.int32, (2 * LANES, 2 * LANES), 1)
    same_head = (((r2 ^ c2) & (LANES | 32)) == 0).astype(BF16)
    acc_q = [_dot(hbs[u], w_ref[:, 0:8 * LANES]) for u in range(tps)]
    acc_k = [_dot(hbs[u], w_ref[:, 8 * LANES:10 * LANES]) for u in range(tps)]
    acc_v = [_dot(hbs[u], w_ref[:, 10 * LANES:12 * LANES]) for u in range(tps)]
    for j in range(5):
        for u in range(tps):
            acc = acc_q[u][:, j * 2 * LANES:(j + 1) * 2 * LANES] if j < 4 else acc_k[u]
            sq = acc * acc
            hi = sq.astype(BF16)
            lo = (sq - hi.astype(F32)).astype(BF16)
            ms = (_dot(hi, same_head) + _dot(lo, same_head)) * (1.0 / HEAD_DIM)
            nrm = acc * lax.rsqrt(ms + EPS)
            nw = nw_ref[0:1, :] if j < 4 else nw_ref[1:2, :]
            cos, sin = rope_refs[2 * u][...], rope_refs[2 * u + 1][...]
            for v in range(2):
                ls = slice(v * LANES, (v + 1) * LANES)
                y_ref[u * tm:(u + 1) * tm, (2 * j + v) * LANES:(2 * j + v + 1) * LANES] = (
                    _rope(nrm[:, ls] * nw, cos, sin).astype(BF16))
    for u in range(tps):
        y_ref[u * tm:(u + 1) * tm, 10 * LANES:12 * LANES] = acc_v[u].astype(BF16)


def _inproj_attn(x2, ab, w, nw, cos, sin, tiles_per_batch):
    r, d = x2.shape
    tm, tps = ROW_TILE, TILES_PER_STEP
    n = w.shape[1]
    tpb = tiles_per_batch
    assert (r // tm) % tps == 0
    _, mod_spec, rope_spec = _tile_specs(tm, tpb, tps)
    rope_specs = []
    for u in range(tps):
        rope_specs += [rope_spec(u), rope_spec(u)]
    return pl.pallas_call(
        _inproj_attn_kernel,
        grid=(r // (tm * tps),),
        in_specs=[pl.BlockSpec((tps * tm, d), lambda i: (i, 0))] + [mod_spec(d, u) for u in range(tps)]
                 + [_const_spec(w.shape), _const_spec(nw.shape)] + rope_specs,
        out_specs=pl.BlockSpec((tps * tm, n), lambda i: (i, 0)),
        out_shape=jax.ShapeDtypeStruct((r, n), BF16),
        compiler_params=_cparams("arbitrary"),
        name="inproj_attn",
    )(x2, *([ab] * tps), w, nw, *([cos, sin] * tps))


def _conv_silu(cur_ref, prev_ref, next_ref, cw_ref, prev_on, next_on):
    cur = cur_ref[...].astype(F32)
    row = lax.broadcasted_iota(jnp.int32, cur.shape, 0)
    prev_row = prev_ref[HALO - 1:HALO, :].astype(F32) * prev_on
    next_row = next_ref[0:1, :].astype(F32) * next_on
    xm = jnp.where(row == 0, prev_row, pltpu.roll(cur, 1, 0))
    xp = jnp.where(row == CHUNK - 1, next_row, pltpu.roll(cur, CHUNK - 1, 0))
    y = cw_ref[3:4, :] + cw_ref[0:1, :] * xm + cw_ref[1:2, :] * cur + cw_ref[2:3, :] * xp
    return y * jax.nn.sigmoid(y)


def _segment_flags(c, nc, ctx_chunks):
    prev_on = jnp.where((c == 0) | (c == ctx_chunks), 0.0, 1.0).astype(F32)
    next_on = jnp.where((c == ctx_chunks - 1) | (c == nc - 1), 0.0, 1.0).astype(F32)
    return prev_on, next_on


def _cumsum_cols(tri_bf, lf):
    hi, mid, lo = _split3(lf)
    return _dot(tri_bf, hi) + _dot(tri_bf, mid) + _dot(tri_bf, lo)


def _cumsum_rows(lf, tri_bf):
    hi, mid, lo = _split3(lf)
    return _dot(hi, tri_bf) + _dot(mid, tri_bf) + _dot(lo, tri_bf)


def _ret_state_update(s_ref, p, k2, v2, kdec, cd_lanes, bd):
    kf = (k2.astype(F32) * kdec).astype(BF16)
    s_ref[p] = s_ref[p] * cd_lanes + jnp.where(bd, _dot_tn(v2, kf), 0.0)


def _mlstm_state_update(c_ref, n_ref, m_ref, k_pairs, v_pairs, c_all, bend, col0, lo, bd):
    cmax = jnp.max(c_all, axis=0, keepdims=True)
    w_all = jnp.exp(c_all - cmax)
    m_old = m_ref[0:1, :]
    mrel = jnp.maximum(m_old, cmax)
    a_row = jnp.exp(m_old - mrel)
    bb_row = jnp.exp(cmax - mrel)
    m_ref[0:1, :] = bend + mrel
    lo_row = lo[0:1, :]
    for p in range(4):
        h0 = col0 + 2 * p
        kw = k_pairs[p] * jnp.where(lo, w_all[:, h0:h0 + 1], w_all[:, h0 + 1:h0 + 2])
        kvt = _dot_tn(v_pairs[p], kw.astype(BF16))
        nloc = jnp.sum(kw, axis=0, keepdims=True)
        a_l = jnp.where(lo_row, a_row[:, h0:h0 + 1], a_row[:, h0 + 1:h0 + 2])
        bb_l = jnp.where(lo_row, bb_row[:, h0:h0 + 1], bb_row[:, h0 + 1:h0 + 2])
        c_ref[p] = c_ref[p] * a_l + jnp.where(bd, kvt, 0.0) * bb_l
        n_new = (n_ref[p, 0:1, :] + n_ref[p, 1:2, :]) * a_l + nloc * bb_l
        n_ref[p, 0:1, :] = jnp.where(lo_row, n_new, 0.0)
        n_ref[p, 1:2, :] = jnp.where(lo_row, 0.0, n_new)


def _bwd_state_kernel(rk_ref, rv_ref, mk_ref, mkp_ref, mkn_ref, mv_ref, g_ref,
                      lgk_ref, cw_ref,
                      sret_ref, cm_ref, nm_ref, mm_ref,
                      s_scr, c_scr, n_scr, m_scr, kdec_scr, *, nc, ctx_chunks):
    i = pl.program_id(1)
    c = jnp.where(i < ctx_chunks, ctx_chunks - 1 - i, nc + ctx_chunks - 1 - i)
    lane = _lane()
    sub = lax.broadcasted_iota(jnp.int32, (CHUNK, LANES), 0)
    lo = lane < HEAD_DIM
    bd_ret = (sub >= HEAD_DIM) == ((lane & 32) != 0)
    bd_m = (sub >= HEAD_DIM) == (lane >= HEAD_DIM)

    @pl.when(i == 0)
    def _():
        s_scr[...] = jnp.zeros_like(s_scr)
        c_scr[...] = jnp.zeros_like(c_scr)
        n_scr[...] = jnp.zeros_like(n_scr)
        m_scr[...] = jnp.zeros_like(m_scr)
        pos = sub.astype(F32)
        for p in range(4):
            kdec_scr[p] = jnp.exp(lgk_ref[4 + p:5 + p, :] * pos)

    sret_ref[...] = s_scr[...].astype(BF16)
    cm_ref[...] = c_scr[...].astype(BF16)
    nm_ref[...] = n_scr[...].astype(BF16)
    mm_ref[...] = m_scr[...]

    for p in range(4):
        cd = jnp.exp(lgk_ref[4 + p:5 + p, :] * float(CHUNK))
        _ret_state_update(s_scr, p, rk_ref[:, p * LANES:(p + 1) * LANES],
                          rv_ref[:, p * LANES:(p + 1) * LANES], kdec_scr[p], cd, bd_ret)

    prev_on, next_on = _segment_flags(c, nc, ctx_chunks)
    kc = _conv_silu(mk_ref, mkp_ref, mkn_ref, cw_ref, prev_on, next_on)
    g = g_ref[...]
    le_bf = (sub <= lane).astype(BF16)
    bal = pltpu.roll(_cumsum_cols(le_bf, _log_sigmoid(g)), LANES - 8, 1)
    _mlstm_state_update(c_scr, n_scr, m_scr,
                        [kc[:, p * LANES:(p + 1) * LANES] for p in range(4)],
                        [mv_ref[:, p * LANES:(p + 1) * LANES] for p in range(4)],
                        g - bal, bal[0:1, :], 16, lo, bd_m)


def _bwd_states(y, g, lgk, cw_k, nb, nc, ctx_chunks):
    gw = 4 * LANES
    def cidx(b, i):
        c = jnp.where(i < ctx_chunks, ctx_chunks - 1 - i, nc + ctx_chunks - 1 - i)
        return b * nc + c
    hb = CHUNK // HALO
    nhalo = y.shape[0] // HALO
    return pl.pallas_call(
        functools.partial(_bwd_state_kernel, nc=nc, ctx_chunks=ctx_chunks),
        grid=(nb, nc),
        in_specs=[pl.BlockSpec((CHUNK, gw), lambda b, i: (cidx(b, i), 1)),
                  pl.BlockSpec((CHUNK, gw), lambda b, i: (cidx(b, i), 2)),
                  pl.BlockSpec((CHUNK, gw), lambda b, i: (cidx(b, i), 5)),
                  pl.BlockSpec((HALO, gw), lambda b, i: (jnp.maximum(cidx(b, i) * hb - 1, 0), 5)),
                  pl.BlockSpec((HALO, gw), lambda b, i: (jnp.minimum((cidx(b, i) + 1) * hb, nhalo - 1), 5)),
                  pl.BlockSpec((CHUNK, gw), lambda b, i: (cidx(b, i), 6)),
                  pl.BlockSpec((CHUNK, LANES), lambda b, i: (cidx(b, i), 0)),
                  _const_spec(lgk.shape), _const_spec(cw_k.shape)],
        out_specs=[pl.BlockSpec((None, 4, CHUNK, LANES), lambda b, i: (cidx(b, i), 0, 0, 0)),
                   pl.BlockSpec((None, 4, CHUNK, LANES), lambda b, i: (cidx(b, i), 0, 0, 0)),
                   pl.BlockSpec((None, 4, HALO, LANES), lambda b, i: (cidx(b, i), 0, 0, 0)),
                   pl.BlockSpec((None, 8, LANES), lambda b, i: (cidx(b, i), 0, 0))],
        out_shape=[jax.ShapeDtypeStruct((nb * nc, 4, CHUNK, LANES), BF16),
                   jax.ShapeDtypeStruct((nb * nc, 4, CHUNK, LANES), BF16),
                   jax.ShapeDtypeStruct((nb * nc, 4, HALO, LANES), BF16),
                   jax.ShapeDtypeStruct((nb * nc, 8, LANES), F32)],
        scratch_shapes=[pltpu.VMEM((4, CHUNK, LANES), F32), pltpu.VMEM((4, CHUNK, LANES), F32),
                        pltpu.VMEM((4, HALO, LANES), F32), pltpu.VMEM((8, LANES), F32),
                        pltpu.VMEM((4, CHUNK, LANES), F32)],
        compiler_params=_cparams("arbitrary", "arbitrary"),
        name="bwd_state_sweep",
    )(y, y, y, y, y, y, g, lgk, cw_k)


def _mlstm_dir_weights(st, qn_row, c_col, bt_row, m_prev, tri):
    dl = jnp.where(tri, c_col + bt_row, NEG)
    mx = jnp.max(dl, axis=0, keepdims=True)
    al = bt_row + m_prev
    m_t = jnp.maximum(al, mx)
    w = jnp.exp(dl - m_t)
    a_t = jnp.exp(al - m_t)
    sw = st * w
    den = jnp.sum(sw, axis=0, keepdims=True) + a_t * qn_row
    r = 1.0 / jnp.maximum(jnp.abs(den), jnp.exp(-m_t))
    return sw * r, a_t * r


def _heads_out(ht, nw_tab):
    rows = []
    for a in range(2):
        ha = ht[a * HEAD_DIM:(a + 1) * HEAD_DIM, :]
        ms = jnp.mean(ha * ha, axis=0, keepdims=True)
        rows.append(ha * lax.rsqrt(ms + EPS))
    return (jnp.concatenate(rows, axis=0) * nw_tab).T


def _mixer_kernel(lg_ref, rq_ref, rk_ref, rv_ref, rg_ref, mqk_ref, mqkp_ref, mqkn_ref, mv_ref, mo_ref,
                  g_ref, gt_ref, sretb_ref, cmb_ref, nmb_ref, mmb_ref,
                  lgk_ref, lgkt_ref, cw_ref, nw_ref,
                  out_ref,
                  s_scr, c_scr, n_scr, m_scr, dm_scr, dec_scr, *, nc, ctx_chunks):
    c = pl.program_id(1)
    lane = _lane()
    sub = lax.broadcasted_iota(jnp.int32, (CHUNK, LANES), 0)
    lo = lane < HEAD_DIM
    sub_lo = sub < HEAD_DIM
    mask_ret = [(lane & 32) == 0, (lane & 32) != 0]
    mask_nat = [lo, lane >= HEAD_DIM]
    bd_ret = (sub >= HEAD_DIM) == ((lane & 32) != 0)
    bd_m = (sub >= HEAD_DIM) == (lane >= HEAD_DIM)
    le = sub <= lane
    ge = sub >= lane

    @pl.when(c == 0)
    def _():
        s_scr[...] = jnp.zeros_like(s_scr)
        c_scr[...] = jnp.zeros_like(c_scr)
        n_scr[...] = jnp.zeros_like(n_scr)
        m_scr[...] = jnp.zeros_like(m_scr)
        spos = sub.astype(F32)
        tpos = lane.astype(F32)
        diff = (lane - sub).astype(F32)
        for h in range(8):
            dm_scr[h] = (jnp.where(le, jnp.exp(lg_ref[h] * diff), 0.0)
                         + jnp.where(ge, jnp.exp(lg_ref[8 + h] * (-diff)), 0.0))
        for p in range(4):
            dec_scr[0, p] = jnp.exp(lgkt_ref[:, p:p + 1] * (tpos + 1.0))
            dec_scr[1, p] = jnp.exp(lgkt_ref[:, 4 + p:5 + p] * (float(CHUNK) - tpos))
            dec_scr[2, p] = jnp.exp(lgk_ref[p:p + 1, :] * (float(CHUNK) - 1.0 - spos))


    prev_on, next_on = _segment_flags(c, nc, ctx_chunks)
    qk = _conv_silu(mqk_ref, mqkp_ref, mqkn_ref, cw_ref, prev_on, next_on)
    g = g_ref[...]
    gt = gt_ref[...]
    lf_col = _log_sigmoid(g)
    lf_row = _log_sigmoid(gt)
    le_bf = le.astype(BF16)
    ge_bf = ge.astype(BF16)
    bal_f = pltpu.roll(_cumsum_cols(ge_bf, lf_col), LANES - 8, 1)
    bal_b = pltpu.roll(_cumsum_cols(le_bf, lf_col), LANES - 8, 1)
    cf_all = g - bal_f
    cb_all = g - bal_b
    bf_row = _cumsum_rows(lf_row, le_bf)
    bb_row = _cumsum_rows(lf_row, ge_bf)

    qb, kb, vb, kf32, qt, vbd = [], [], [], [], [], []
    for p in range(8):
        sl = slice((p % 4) * LANES, (p % 4 + 1) * LANES)
        if p < 4:
            q2, k2, v2 = rq_ref[:, sl], rk_ref[:, sl], rv_ref[:, sl]
            qf = q2.astype(F32)
            kf = None
            old =jnp.concatenate([s_scr[p].astype(BF16), sretb_ref[p]], axis=1)
        else:
            qf = qk[:, sl] * (HEAD_DIM ** -0.5)
            kf = qk[:, 4 * LANES + (p - 4) * LANES:4 * LANES + (p - 3) * LANES]
            q2, k2, v2 = qf.astype(BF16), kf.astype(BF16), mv_ref[:, sl]
            old = jnp.concatenate([c_scr[p - 4].astype(BF16), cmb_ref[p - 4]], axis=1)
        vt = v2.astype(F32).T.astype(BF16)
        qb.append(q2)
        kb.append(k2)
        vb.append(v2)
        kf32.append(kf)
        qt.append(qf.T)
        vbd.append(jnp.concatenate([jnp.where(sub_lo, vt, jnp.zeros_like(vt)),
                                    jnp.where(sub_lo, jnp.zeros_like(vt), vt), old], axis=1))

    st2, qn_f, qn_b = [], [], []
    for p in range(8):
        masks = mask_ret if p < 4 else mask_nat
        zero = jnp.zeros_like(qb[p])
        qstack = jnp.concatenate([jnp.where(masks[0], qb[p], zero), jnp.where(masks[1], qb[p], zero)], axis=0)
        st2.append(_dot_nt(kb[p], qstack))
        if p >= 4:
            qn_f.append(_dot_nt(n_scr[p - 4].astype(BF16), qb[p]))
            qn_b.append(_dot_nt(nmb_ref[p - 4], qb[p]))

    rhs = []
    for p in range(8):
        if p < 4:
            pts = [(st2[p][:, a * LANES:(a + 1) * LANES] * dm_scr[2 * p + a]).astype(BF16) for a in range(2)]
            x_f, x_b = dec_scr[0, p], dec_scr[1, p]
        else:
            pts, cf, cb = [], [], []
            for a in range(2):
                h = 2 * (p - 4) + a
                st = st2[p][:, a * LANES:(a + 1) * LANES]
                pf, coef_f = _mlstm_dir_weights(st, qn_f[p - 4][a:a + 1, :], cf_all[:, h:h + 1],
                                                bf_row[8 + h:9 + h, :], m_scr[0:1, h:h + 1], le)
                pb, coef_b = _mlstm_dir_weights(st, qn_b[p - 4][a:a + 1, :], cb_all[:, 16 + h:17 + h],
                                                bb_row[24 + h:25 + h, :], mmb_ref[0:1, 16 + h:17 + h], ge)
                pts.append((pf + pb).astype(BF16))
                cf.append(coef_f)
                cb.append(coef_b)
            x_f = jnp.where(sub_lo, cf[0], cf[1])
            x_b = jnp.where(sub_lo, cb[0], cb[1])
        rhs.append(jnp.concatenate(pts + [(qt[p] * x_f).astype(BF16), (qt[p] * x_b).astype(BF16)], axis=0))

    ht = [_dot(vbd[p], rhs[p]) for p in range(8)]

    for p in range(8):
        sl = slice((p % 4) * LANES, (p % 4 + 1) * LANES)
        gate_ref = rg_ref if p < 4 else mo_ref
        y = _heads_out(ht[p], nw_ref[p])
        out_ref[:, p * LANES:(p + 1) * LANES] = (y * gate_ref[:, sl].astype(F32)).astype(BF16)

    for p in range(4):
        cd = jnp.exp(lgk_ref[p:p + 1, :] * float(CHUNK))
        _ret_state_update(s_scr, p, kb[p], vb[p], dec_scr[2, p], cd, bd_ret)
    _mlstm_state_update(c_scr, n_scr, m_scr, kf32[4:], vb[4:], cf_all, bal_f[CHUNK - 1:CHUNK, :], 0, lo, bd_m)


def _mixer(y, g, gt, states, lg_smem, lgk, lgkt, cw_qk, nw, nb, nc, ctx_chunks):
    gw = 4 * LANES
    sretb, cmb, nmb, mmb = states
    rc = lambda b, c: b * nc + c
    hb = CHUNK // HALO
    nhalo = y.shape[0] // HALO
    blk = lambda j: pl.BlockSpec((CHUNK, gw), lambda b, c: (rc(b, c), j))
    return pl.pallas_call(
        functools.partial(_mixer_kernel, nc=nc, ctx_chunks=ctx_chunks),
        grid=(nb, nc),
        in_specs=[_smem_spec(),
                  blk(0), blk(1), blk(2), blk(3),
                  pl.BlockSpec((CHUNK, 2 * gw), lambda b, c: (rc(b, c), 2)),
                  pl.BlockSpec((HALO, 2 * gw), lambda b, c: (jnp.maximum(rc(b, c) * hb - 1, 0), 2)),
                  pl.BlockSpec((HALO, 2 * gw), lambda b, c: (jnp.minimum((rc(b, c) + 1) * hb, nhalo - 1), 2)),
                  blk(6), blk(7),
                  pl.BlockSpec((CHUNK, LANES), lambda b, c: (rc(b, c), 0)),
                  pl.BlockSpec((32, CHUNK), lambda b, c: (0, rc(b, c))),
                  pl.BlockSpec((None, 4, CHUNK, LANES), lambda b, c: (rc(b, c), 0, 0, 0)),
                  pl.BlockSpec((None, 4, CHUNK, LANES), lambda b, c: (rc(b, c), 0, 0, 0)),
                  pl.BlockSpec((None, 4, HALO, LANES), lambda b, c: (rc(b, c), 0, 0, 0)),
                  pl.BlockSpec((None, 8, LANES), lambda b, c: (rc(b, c), 0, 0)),
                  _const_spec(lgk.shape), _const_spec(lgkt.shape), _const_spec(cw_qk.shape),
                  _const_spec(nw.shape)],
        out_specs=pl.BlockSpec((CHUNK, 2 * gw), lambda b, c: (rc(b, c), 0)),
        out_shape=jax.ShapeDtypeStruct((nb * nc * CHUNK, 2 * gw), BF16),
        scratch_shapes=[pltpu.VMEM((4, CHUNK, LANES), F32), pltpu.VMEM((4, CHUNK, LANES), F32),
                        pltpu.VMEM((4, HALO, LANES), F32), pltpu.VMEM((8, LANES), F32),
                        pltpu.VMEM((8, CHUNK, LANES), F32), pltpu.VMEM((3, 4, CHUNK, LANES), F32)],
        compiler_params=_cparams("arbitrary", "arbitrary"),
        name="ret_mlstm_mixer",
    )(lg_smem, y, y, y, y, y, y, y, y, y, g, gt, sretb, cmb, nmb, mmb, lgk, lgkt, cw_qk, nw)


def _attn_kernel(sink_ref, q_ref, kp_ref, kc_ref, kn_ref, kx_ref, vp_ref, vc_ref, vn_ref, vx_ref,
                 bias_ref, o_ref):
    grp = lax.broadcasted_iota(jnp.int32, (1, 4 * CHUNK), 1) // CHUNK
    lane = _lane()
    mask_q = [(lane & 32) == 0, (lane & 32) != 0]
    bias = jnp.concatenate([bias_ref[...]] * 4, axis=1)
    vts, sts = [], []
    for kvp in range(2):
        sl = slice(kvp * LANES, (kvp + 1) * LANES)
        v_blocks = [vp_ref[:, sl], vc_ref[:, sl], vn_ref[:, sl], vx_ref[:, sl]]
        vts.append(jnp.concatenate(
            [vb[r * CHUNK:(r + 1) * CHUNK, :].astype(F32).T.astype(BF16)
             for vb in v_blocks for r in range(vb.shape[0] // CHUNK)], axis=1))
    for kvp in range(2):
        sl = slice(kvp * LANES, (kvp + 1) * LANES)
        kcat = jnp.concatenate([kp_ref[:, sl], kc_ref[:, sl], kn_ref[:, sl], kx_ref[:, sl]], axis=0)
        for a in range(2):
            qs = jnp.concatenate(
                [jnp.where(mask_q[a], q_ref[:, (kvp * 4 + g) * LANES:(kvp * 4 + g + 1) * LANES],
                           jnp.zeros((CHUNK, LANES), BF16)) for g in range(4)], axis=0)
            sts.append(_dot_nt(kcat, qs))
    outs = []
    for kv in range(4):
        st = sts[kv]
        st = jnp.concatenate([st[0:CHUNK] + bias[0:CHUNK], st[CHUNK:2 * CHUNK],
                              st[2 * CHUNK:3 * CHUNK] + bias[CHUNK:2 * CHUNK], st[3 * CHUNK:]], axis=0)
        snk = jnp.where(grp == 0, sink_ref[kv * 4],
                        jnp.where(grp == 1, sink_ref[kv * 4 + 1],
                                  jnp.where(grp == 2, sink_ref[kv * 4 + 2], sink_ref[kv * 4 + 3])))
        m = jnp.maximum(jnp.max(st, axis=0, keepdims=True), snk)
        e = jnp.exp(st - m)
        denom = jnp.exp(snk - m) + jnp.sum(e, axis=0, keepdims=True)
        a = kv % 2
        outs.append(_dot(vts[kv // 2][a * HEAD_DIM:(a + 1) * HEAD_DIM, :], e.astype(BF16)) * (1.0 / denom))
    for kvp in range(2):
        full = jnp.concatenate(outs[2 * kvp:2 * kvp + 2], axis=0)
        for g in range(4):
            o_ref[:, (kvp * 4 + g) * LANES:(kvp * 4 + g + 1) * LANES] = (
                full[:, g * CHUNK:(g + 1) * CHUNK].T.astype(BF16))


def _window_bias():
    kk = np.arange(CHUNK)[:, None]
    t = np.arange(CHUNK)[None, :]
    tabs = []
    for has_prev, has_next in ((False, True), (True, True), (True, False)):
        prev_ok = (kk >= t) & has_prev
        next_ok = (kk <= t) & has_next
        tabs.append(np.where(np.concatenate([prev_ok, next_ok], axis=0), 0.0, NEG))
    return jnp.asarray(np.stack(tabs), F32)


def _window_attn(y, sink, nb, nc, ctx_chunks):
    nq = nc - ctx_chunks
    assert nq >= 2
    ctx_len = ctx_chunks * CHUNK
    bias = _window_bias()
    d = 8 * LANES
    kcol, vcol = d // (2 * LANES), d // (2 * LANES) + 1
    qrow = lambda b, i: b * nc + ctx_chunks + i
    prow = lambda b, i: b * nc + ctx_chunks + jnp.maximum(i - 1, 0)
    nrow = lambda b, i: b * nc + ctx_chunks + jnp.minimum(i + 1, nq - 1)
    xrow = lambda b, i: (b * nc * CHUNK) // ctx_len
    kv_spec = lambda rowf, col: pl.BlockSpec((CHUNK, 2 * LANES), lambda b, i: (rowf(b, i), col))
    x_spec = lambda col: pl.BlockSpec((ctx_len, 2 * LANES), lambda b, i: (xrow(b, i), col))
    bias_spec = pl.BlockSpec((None,) + bias.shape[1:],
                             lambda b, i: (jnp.where(i == 0, 0, jnp.where(i == nq - 1, 2, 1)), 0, 0))
    return pl.pallas_call(
        _attn_kernel,
        grid=(nb, nq),
        in_specs=[_smem_spec(),
                  pl.BlockSpec((CHUNK, d), lambda b, i: (qrow(b, i), 0)),
                  kv_spec(prow, kcol), kv_spec(qrow, kcol), kv_spec(nrow, kcol), x_spec(kcol),
                  kv_spec(prow, vcol), kv_spec(qrow, vcol), kv_spec(nrow, vcol), x_spec(vcol),
                  bias_spec],
        out_specs=pl.BlockSpec((CHUNK, d), lambda b, i: (b * nq + i, 0)),
        out_shape=jax.ShapeDtypeStruct((nb * nq * CHUNK, d), BF16),
        compiler_params=_cparams("arbitrary", "arbitrary"),
        name="window_gqa",
    )(sink, y, y, y, y, y, y, y, y, y, bias)


def _ffn_in(x, m, p_ref, wo_ref):
    x1 = x + p_ref[0:1, :] * _dot(m, wo_ref[...])
    ms = jnp.mean(x1 * x1, axis=-1, keepdims=True)
    return x1, ((x1 * lax.rsqrt(ms + EPS)) * p_ref[1:2, :] + p_ref[2:3, :]).astype(BF16)


def _ffn_cols(h, wi_ref, act_scr, lo, hi, d_ff):
    gate = _dot(h, wi_ref[:, lo:hi])
    up = _dot(h, wi_ref[:, d_ff + lo:d_ff + hi])
    act_scr[:, lo:hi] = (gate * jax.nn.sigmoid(gate) * up).astype(BF16)


def _attn_ffn_kernel(sink_ref, q_ref, kp_ref, kc_ref, kn_ref, kx_ref, vp_ref, vc_ref, vn_ref, vx_ref, bias_ref,
                     x_ref, p_ref, wo_ref, wi_ref, w2_ref, o_ref, m_scr, act_scr, *, tiles, lat_tiles, d_ff):
    s = pl.program_id(0)

    @pl.when(s == 0)
    def _():
        m_scr[...] = jnp.zeros_like(m_scr)

    j = jnp.minimum(s, tiles - 1) % lat_tiles
    grp = lax.broadcasted_iota(jnp.int32, (1, 4 * CHUNK), 1) // CHUNK
    lane = _lane()
    mask_q = [(lane & 32) == 0, (lane & 32) != 0]
    tile4 = lambda b: jnp.concatenate([b] * 4, axis=1)
    biases = [tile4(jnp.where(j == 0, bias_ref[0], bias_ref[1])),
              tile4(jnp.where(j == lat_tiles - 1, bias_ref[2], bias_ref[1]))]
    third = (d_ff // 3 // (2 * LANES) + 1) * 2 * LANES
    splits = [0, third, 2 * third, d_ff]

    x1, h = _ffn_in(x_ref[...], m_scr[...], p_ref, wo_ref)

    kcats, vts = [], []
    for kvp in range(2):
        sl = slice(kvp * LANES, (kvp + 1) * LANES)
        k_chunks = [kp_ref[:, sl], kc_ref[0:CHUNK, sl], kc_ref[CHUNK:2 * CHUNK, sl], kn_ref[:, sl]]
        v_chunks = [vp_ref[:, sl], vc_ref[0:CHUNK, sl], vc_ref[CHUNK:2 * CHUNK, sl], vn_ref[:, sl],
                    vx_ref[0:CHUNK, sl], vx_ref[CHUNK:2 * CHUNK, sl]]
        v_t = [v.astype(F32).T.astype(BF16) for v in v_chunks]
        kcats.append([jnp.concatenate(k_chunks[b:b + 3] + [kx_ref[:, sl]], axis=0) for b in range(2)])
        vts.append([jnp.concatenate(v_t[b:b + 3] + v_t[4:], axis=1) for b in range(2)])

    outs = []
    for blk in range(2):
        rows = slice(blk * CHUNK, (blk + 1) * CHUNK)
        sts = []
        for kvp in range(2):
            for a in range(2):
                qs = jnp.concatenate(
                    [jnp.where(mask_q[a], q_ref[rows, (kvp * 4 + g) * LANES:(kvp * 4 + g + 1) * LANES],
                               jnp.zeros((CHUNK, LANES), BF16)) for g in range(4)], axis=0)
                sts.append(_dot_nt(kcats[kvp][blk], qs))
        _ffn_cols(h, wi_ref, act_scr, splits[blk], splits[blk + 1], d_ff)
        bias = biases[blk]
        for kv in range(4):
            st = sts[kv]
            st = jnp.concatenate([st[0:CHUNK] + bias[0:CHUNK], st[CHUNK:2 * CHUNK],
                                  st[2 * CHUNK:3 * CHUNK] + bias[CHUNK:2 * CHUNK], st[3 * CHUNK:]], axis=0)
            snk = jnp.where(grp == 0, sink_ref[kv * 4],
                            jnp.where(grp == 1, sink_ref[kv * 4 + 1],
                                      jnp.where(grp == 2, sink_ref[kv * 4 + 2], sink_ref[kv * 4 + 3])))
            m = jnp.maximum(jnp.max(st, axis=0, keepdims=True), snk)
            e = jnp.exp(st - m)
            denom = jnp.exp(snk - m) + jnp.sum(e, axis=0, keepdims=True)
            a = kv % 2
            outs.append(_dot(vts[kv // 2][blk][a * HEAD_DIM:(a + 1) * HEAD_DIM, :], e.astype(BF16))
                        * (1.0 / denom))
    _ffn_cols(h, wi_ref, act_scr, splits[2], splits[3], d_ff)

    for blk in range(2):
        for kvp in range(2):
            full = jnp.concatenate(outs[4 * blk + 2 * kvp:4 * blk + 2 * kvp + 2], axis=0)
            for g in range(4):
                m_scr[blk * CHUNK:(blk + 1) * CHUNK, (kvp * 4 + g) * LANES:(kvp * 4 + g + 1) * LANES] = (
                    full[:, g * CHUNK:(g + 1) * CHUNK].T.astype(BF16))

    o_ref[...] = x1 + p_ref[3:4, :] * _dot(act_scr[...], w2_ref[...])


def _attn_ffn(y, sink, xc, prm, wo, wi_all, w2_all, layer, nb, nc, ctx_chunks):
    tm = ROW_TILE
    cpt = tm // CHUNK
    tpb = nc // cpt
    lat_tiles = (nc - ctx_chunks) // cpt
    tiles = nb * lat_tiles
    ctx_tiles = ctx_chunks // cpt
    assert ctx_tiles == 1 and lat_tiles >= 2
    d = xc.shape[1]
    d_ff = w2_all.shape[1]
    bias = _window_bias()
    kcol, vcol = d // (2 * LANES), d // (2 * LANES) + 1
    att = lambda s: jnp.minimum(s, tiles - 1)
    ffn = lambda s: jnp.maximum(s - 1, 0)
    row_tile = lambda t: (t // lat_tiles) * tpb + ctx_tiles + t % lat_tiles
    chunk0 = lambda t: (t // lat_tiles) * nc + ctx_chunks
    prev_c = lambda s: chunk0(att(s)) + jnp.maximum((att(s) % lat_tiles) * cpt - 1, 0)
    next_c = lambda s: chunk0(att(s)) + jnp.minimum((att(s) % lat_tiles) * cpt + cpt, lat_tiles * cpt - 1)
    cur_spec = lambda col: pl.BlockSpec((tm, 2 * LANES), lambda s: (row_tile(att(s)), col))
    edge_spec = lambda f, col: pl.BlockSpec((CHUNK, 2 * LANES), lambda s: (f(s), col))
    ctx_spec = lambda col: pl.BlockSpec((tm, 2 * LANES), lambda s: ((att(s) // lat_tiles) * tpb, col))
    layer_spec = lambda a: pl.BlockSpec((None,) + a.shape[1:], lambda s: (layer, 0, 0),
                                        pipeline_mode=pl.Buffered(1))
    return pl.pallas_call(
        functools.partial(_attn_ffn_kernel, tiles=tiles, lat_tiles=lat_tiles, d_ff=d_ff),
        grid=(tiles + 1,),
        in_specs=[_smem_spec(),
                  pl.BlockSpec((tm, d), lambda s: (row_tile(att(s)), 0)),
                  edge_spec(prev_c, kcol), cur_spec(kcol), edge_spec(next_c, kcol), ctx_spec(kcol),
                  edge_spec(prev_c, vcol), cur_spec(vcol), edge_spec(next_c, vcol), ctx_spec(vcol),
                  _const_spec(bias.shape),
                  pl.BlockSpec((tm, d), lambda s: (row_tile(ffn(s)), 0)),
                  pl.BlockSpec((None, 8, d), lambda s: ((ffn(s) // lat_tiles) * 2 + 1, 0, 0)),
                  _const_spec(wo.shape), layer_spec(wi_all), layer_spec(w2_all)],
        out_specs=pl.BlockSpec((tm, d), lambda s: (ffn(s), 0)),
        out_shape=jax.ShapeDtypeStruct((tiles * tm, d), F32),
        scratch_shapes=[pltpu.VMEM((tm, d), BF16), pltpu.VMEM((tm, d_ff), BF16)],
        compiler_params=_cparams("arbitrary"),
        name="window_gqa_ffn",
    )(sink, y, y, y, y, y, y, y, y, y, bias, xc, prm, wo, wi_all, w2_all)


def _post_kernel(*refs, d_ff, split_tpb):
    tps = TILES_PER_STEP
    n_x = 2 if split_tpb else 1
    x_refs, refs = refs[:n_x * tps], refs[n_x * tps:]
    m_refs, p_refs = refs[:tps], refs[tps:2 * tps]
    wo_ref, wi_ref, w2_ref, o_ref, act_scr = refs[2 * tps:]
    tm = m_refs[0].shape[0]
    ys = [_dot(m_refs[u][...], wo_ref[...]) for u in range(tps)]
    x1s, hs = [], []
    for u in range(tps):
        if split_tpb:
            x = _ctx_or_latent_rows(x_refs[2 * u], x_refs[2 * u + 1], pl.program_id(0) * tps + u, split_tpb)
        else:
            x = x_refs[u][...]
        p_ref = p_refs[u]
        x1 = x + p_ref[0:1, :] * ys[u]
        ms = jnp.mean(x1 * x1, axis=-1, keepdims=True)
        hs.append(((x1 * lax.rsqrt(ms + EPS)) * p_ref[1:2, :] + p_ref[2:3, :]).astype(BF16))
        x1s.append(x1)
    for u in range(tps):
        gate = _dot(hs[u], wi_ref[:, :d_ff])
        up = _dot(hs[u], wi_ref[:, d_ff:])
        act_scr[u] = (gate * jax.nn.sigmoid(gate) * up).astype(BF16)
    for u in range(tps):
        o_ref[u * tm:(u + 1) * tm, :] = x1s[u] + p_refs[u][3:4, :] * _dot(act_scr[u], w2_ref[...])


def _post(xs, m2, prm, wo, wi_all, w2_all, layer, tiles_per_batch, skip_tiles, m_has_ctx):
    split = len(xs) == 2
    d = xs[-1].shape[1]
    r = sum(a.shape[0] for a in xs)
    tm = ROW_TILE
    tps = TILES_PER_STEP
    d_ff = w2_all.shape[1]
    tpb = tiles_per_batch
    kept = tpb - skip_tiles
    nb = r // (tm * tpb)
    assert not (split and skip_tiles) and (nb * kept) % tps == 0
    xrow = lambda t: (t // kept) * tpb + skip_tiles + t % kept
    mrow = xrow if m_has_ctx else (lambda t: t)
    sel = lambda t: ((t // kept) * 2 + jnp.minimum(skip_tiles + t % kept, 1), 0, 0)
    at = lambda f, u: (lambda i: f(i * tps + u))
    row_spec = lambda f, u: pl.BlockSpec((tm, d), lambda i: (f(i * tps + u), 0))
    x_specs = []
    for u in range(tps):
        x_specs += _split_row_specs(tm, d, tpb, at(lambda t: t, u)) if split else [row_spec(xrow, u)]
    layer_spec = lambda a: pl.BlockSpec((None,) + a.shape[1:], lambda i: (layer, 0, 0),
                                        pipeline_mode=pl.Buffered(1))
    return pl.pallas_call(
        functools.partial(_post_kernel, d_ff=d_ff, split_tpb=tpb if split else 0),
        grid=(nb * kept // tps,),
        in_specs=x_specs + [row_spec(mrow, u) for u in range(tps)]
                 + [pl.BlockSpec((None, 8, d), at(sel, u)) for u in range(tps)]
                 + [_const_spec(wo.shape), layer_spec(wi_all), layer_spec(w2_all)],
        out_specs=pl.BlockSpec((tps * tm, d), lambda i: (i, 0)),
        out_shape=jax.ShapeDtypeStruct((nb * kept * tm, d), F32),
        scratch_shapes=[pltpu.VMEM((tps, tm, d_ff), BF16)],
        compiler_params=_cparams("arbitrary"),
        name="outproj_swiglu",
    )(*(list(xs) * tps), *([m2] * tps), *([prm] * tps), wo, wi_all, w2_all)


def _pair_cols(w):
    rows, cols = w.shape
    return w.reshape(rows, cols // LANES, 2, 2, 32).transpose(0, 1, 3, 2, 4).reshape(rows, cols)


def _attn_q_cols(w):
    rows = w.shape[0]
    g_per = w.shape[1] // (H_KV * HEAD_DIM)
    return (w.reshape(rows, H_KV // 2, 2, g_per, 2, 32).transpose(0, 1, 3, 4, 2, 5)
            .reshape(rows, w.shape[1]))


def _attn_o_rows(w):
    cols = w.shape[1]
    g_per = w.shape[0] // (H_KV * HEAD_DIM)
    return (w.reshape(H_KV // 2, 2, g_per, HEAD_DIM, cols).transpose(0, 2, 1, 3, 4)
            .reshape(w.shape[0], cols))


def _rope_tables(seq, ctx_len):
    rows = seq // GRID_W
    row = np.repeat(np.arange(rows, dtype=np.float32), GRID_W)
    col = np.tile(np.arange(GRID_W, dtype=np.float32), rows)
    n = HEAD_DIM // 4
    inv = (np.float32(ROPE_BASE) ** (-np.arange(n, dtype=np.float32) / np.float32(n))).astype(np.float32)
    ang = np.concatenate([row[:, None] * inv, col[:, None] * inv], axis=-1).astype(np.float32)
    cos, sin = np.cos(ang), np.sin(ang)
    cos_t = np.concatenate([np.ones((ctx_len, LANES), np.float32), np.tile(cos, (1, 4))], axis=0)
    sin_t = np.concatenate([np.zeros((ctx_len, LANES), np.float32),
                            np.concatenate([-sin, -sin, sin, sin], axis=-1)], axis=0)
    return jnp.asarray(cos_t, F32), jnp.asarray(sin_t, F32)


def _mod_tables(mod, nb, norm_w):
    d = norm_w.shape[-1]
    lat = mod[:nb].reshape(nb, 6, d)
    ctx = jnp.broadcast_to(mod[nb].reshape(1, 6, d), (nb, 6, d))
    both = jnp.stack([ctx, lat], axis=1).reshape(nb * 2, 6, d)
    sh1, sc1, g1, sh2, sc2, g2 = [both[:, k] for k in range(6)]
    ab1 = jnp.stack([norm_w[0] * (1.0 + sc1), sh1], axis=1)
    zeros = jnp.zeros_like(g1)
    prm = jnp.stack([g1, norm_w[1] * (1.0 + sc2), sh2, g2, zeros, zeros, zeros, zeros], axis=1)
    return ab1, prm


def kernel(x, c, ctx, c_ctx, ada_w, ada_b, norm_w, ffn_w_in, ffn_w_out, ab_w_in, ab_w_out,
           ret_log_gamma, ret_norm_w, mlstm_conv_w, mlstm_conv_b, mlstm_gate_b, mlstm_norm_w,
           attn_w_in, attn_w_out, attn_q_norm_w, attn_k_norm_w, attn_sink):
    nb, seq, d = x.shape
    ctx_len = ctx.shape[1]
    depth = ada_w.shape[0]
    assert ctx_len == ROW_TILE and seq % ROW_TILE == 0 and d == 8 * LANES and nb < 8
    t_all = ctx_len + seq
    nc = t_all // CHUNK
    ctx_chunks = ctx_len // CHUNK
    tpb = t_all // ROW_TILE
    dr = d // 2

    rows = jnp.zeros((8, d), F32).at[:nb].set(c).at[nb].set(c_ctx)
    mod_all = _modulation(rows, ada_w, ada_b)
    cos_t, sin_t = _rope_tables(seq, ctx_len)
    wi_all = ffn_w_in.astype(BF16)
    w2_all = ffn_w_out.astype(BF16)
    xs = (ctx.reshape(nb * ctx_len, d), x.reshape(nb * seq, d))

    out = None
    for layer in range(depth):
        last = layer == depth - 1
        ab1, prm = _mod_tables(mod_all[layer], nb, norm_w[layer])
        if layer % 2 == 0:
            assert layer == 0
            e = layer // 2
            w = ab_w_in[e]
            w_main = jnp.concatenate([_pair_cols(w[:, :dr]), _pair_cols(w[:, dr:2 * dr]), w[:, 2 * dr:8 * dr]],
                                     axis=1).astype(BF16)
            wg = jnp.zeros((d, LANES), F32).at[:, :32].set(w[:, 8 * dr:]).astype(BF16)
            wgt = w[:, 8 * dr:].T.astype(BF16)
            gb = jnp.zeros((1, LANES), F32).at[0, :32].set(mlstm_gate_b[e].reshape(-1))
            gbt = mlstm_gate_b[e].reshape(32, 1)
            y, g, gt = _inproj_ab(*xs, ab1, w_main, wg, wgt, gb, gbt, cos_t, sin_t, tpb)

            lg = ret_log_gamma[e].astype(F32)
            lgk = jnp.tile(jnp.repeat(lg.reshape(2, 4, 2), 32, axis=-1), (1, 1, 2)).reshape(8, LANES)
            cw =jnp.concatenate([mlstm_conv_w[e], mlstm_conv_b[e][None],
                                  jnp.zeros((4, 2 * dr), F32)], axis=0)
            nw = jnp.broadcast_to(jnp.concatenate([ret_norm_w[e], mlstm_norm_w[e]]).reshape(8, LANES, 1),
                                  (8, LANES, LANES))
            lg_s = lg.reshape(-1)
            states = _bwd_states(y, g, lgk, cw[:, dr:], nb, nc, ctx_chunks)
            merged = _mixer(y, g, gt, states, lg_s, lgk, lgk.T, cw, nw, nb, nc, ctx_chunks)
            wo = ab_w_out[e].astype(BF16)
        else:
            o = layer // 2
            w = attn_w_in[o]
            w_main = jnp.concatenate([_attn_q_cols(w[:, :d]), _pair_cols(w[:, d:d + 2 * LANES]),
                                      w[:, d + 2 * LANES:]], axis=1).astype(BF16)
            lane_w = lambda v: jnp.concatenate([v[:32], v[:32], v[32:], v[32:]])
            nwq = jnp.stack([lane_w(attn_q_norm_w[o]) * (HEAD_DIM ** -0.5), lane_w(attn_k_norm_w[o])]
                            + [jnp.zeros((LANES,), F32)] * 6)
            assert last
            y = _inproj_attn(xs[0], ab1, w_main, nwq, cos_t, sin_t, tpb)
            wo = _attn_o_rows(attn_w_out[o]).astype(BF16)
            out = _attn_ffn(y, attn_sink[o].astype(F32), xs[0], prm, wo, wi_all, w2_all, layer,
                            nb, nc, ctx_chunks)
            break
        out = _post(xs, merged, prm, wo, wi_all, w2_all, layer, tpb, 0, True)
        xs = (out,)
    return out.reshape(nb, seq, d)
```

```python
import functools

import numpy as np
import jax
import jax.numpy as jnp
from jax import lax
from jax.experimental import pallas as pl
from jax.experimental.pallas import tpu as pltpu

F32 = jnp.float32
BF16 = jnp.bfloat16

HEAD_DIM = 64
CHUNK = 128
GRID_W = 64
ROPE_BASE = 10000.0
EPS = 1e-6
H_KV = 4
LANES = 128
ROW_TILE = 256
TILES_PER_STEP = 2
HALO = 16
NEG = -1e30
VMEM_LIMIT = 56 * 1024 * 1024


def _cparams(*sem):
    return pltpu.CompilerParams(dimension_semantics=sem, vmem_limit_bytes=VMEM_LIMIT)


def _const_spec(shape):
    nd = len(shape)
    return pl.BlockSpec(shape, lambda *_: (0,) * nd, pipeline_mode=pl.Buffered(1))


def _smem_spec():
    return pl.BlockSpec(memory_space=pltpu.SMEM)


def _lane(shape=(CHUNK, LANES)):
    return lax.broadcasted_iota(jnp.int32, shape, len(shape) - 1)


def _dot(a, b):
    return jnp.dot(a, b, preferred_element_type=F32)


def _dot_nt(a, b):
    return lax.dot_general(a, b, (((1,), (1,)), ((), ())), preferred_element_type=F32)


def _dot_tn(a, b):
    return lax.dot_general(a, b, (((0,), (0,)), ((), ())), preferred_element_type=F32)


def _split3(x):
    hi = x.astype(BF16)
    r = x - hi.astype(F32)
    mid = r.astype(BF16)
    lo = (r - mid.astype(F32)).astype(BF16)
    return hi, mid, lo


def _log_sigmoid(x):
    return jnp.minimum(x, 0.0) - jnp.log1p(jnp.exp(-jnp.abs(x)))


def _rope(x, cos, sin_signed):
    return x * cos + pltpu.roll(x, LANES // 2, 1) * sin_signed


def _mod_kernel(rows_ref, w_ref, b_ref, o_ref):
    a = rows_ref[...]
    a = a * jax.nn.sigmoid(a)
    a_hi = a.astype(BF16)
    a_lo = (a - a_hi.astype(F32)).astype(BF16)
    w = w_ref[...]
    w_hi = w.astype(BF16)
    w_lo = (w - w_hi.astype(F32)).astype(BF16)
    o_ref[...] = _dot(a_hi, w_hi) + _dot(a_hi, w_lo) + _dot(a_lo, w_hi) + b_ref[...]


def _modulation(rows, ada_w, ada_b):
    depth, d, n = ada_w.shape
    tn = n // 4
    return pl.pallas_call(
        _mod_kernel,
        grid=(depth, n // tn),
        in_specs=[pl.BlockSpec((8, d), lambda l, j: (0, 0)),
                  pl.BlockSpec((None, d, tn), lambda l, j: (l, 0, j)),
                  pl.BlockSpec((None, 1, tn), lambda l, j: (l, 0, j))],
        out_specs=pl.BlockSpec((None, 8, tn), lambda l, j: (l, 0, j)),
        out_shape=jax.ShapeDtypeStruct((depth, 8, n), F32),
        compiler_params=_cparams("arbitrary", "arbitrary"),
        name="adaln_modulation",
    )(rows, ada_w, ada_b.reshape(depth, 1, n))


def _norm_mod(x, ab_ref):
    ms = jnp.mean(x * x, axis=-1, keepdims=True)
    h = (x * lax.rsqrt(ms + EPS)) * ab_ref[0:1, :] + ab_ref[1:2, :]
    return h.astype(BF16)


def _ctx_or_latent_rows(ctx_ref, x_ref, tile, tiles_per_batch):
    return jnp.where(tile % tiles_per_batch == 0, ctx_ref[...], x_ref[...])


def _split_row_specs(tm, d, tpb, tile_of):
    lat = tpb - 1
    return [pl.BlockSpec((tm, d), lambda i: (tile_of(i) // tpb, 0)),
            pl.BlockSpec((tm, d), lambda i: ((tile_of(i) // tpb) * lat + jnp.maximum(tile_of(i) % tpb - 1, 0), 0))]


def _inproj_ab_kernel(*refs, tpb):
    tps = TILES_PER_STEP
    x_refs, ab_refs, refs = refs[:2 * tps], refs[2 * tps:3 * tps], refs[3 * tps:]
    w_ref, wg_ref, wgt_ref, gb_ref, gbt_ref = refs[:5]
    rope_refs, (y_ref, g_ref, gt_ref) = refs[5:5 + 2 * tps], refs[5 + 2 * tps:]
    tm = y_ref.shape[0] // tps
    hbs = [_norm_mod(_ctx_or_latent_rows(x_refs[2 * u], x_refs[2 * u + 1], pl.program_id(0) * tps + u, tpb),
                     ab_refs[u]) for u in range(tps)]
    gw = 4 * LANES
    for j in range(8):
        for u in range(tps):
            acc = _dot(hbs[u], w_ref[:, j * gw:(j + 1) * gw])
            if j in (0, 1):
                if j == 0:
                    acc = acc * (HEAD_DIM ** -0.5)
                cos, sin = rope_refs[2 * u][...], rope_refs[2 * u + 1][...]
                acc = jnp.concatenate(
                    [_rope(acc[:, p * LANES:(p + 1) * LANES], cos, sin) for p in range(4)], axis=1)
            elif j == 3:
                acc = acc * jax.nn.sigmoid(acc)
            elif j == 7:
                acc = jax.nn.sigmoid(acc)
            y_ref[u * tm:(u + 1) * tm, j * gw:(j + 1) * gw] = acc.astype(BF16)
    for u in range(tps):
        g_ref[u * tm:(u + 1) * tm, :] = _dot(hbs[u], wg_ref[...]) + gb_ref[...]
        gt_ref[:, u * tm:(u + 1) * tm] = _dot_nt(wgt_ref[...], hbs[u]) + gbt_ref[...]


def _tile_specs(tm, tpb, tps):
    at = lambda f, u: (lambda i: f(i * tps + u))
    sel = lambda t: ((t // tpb) * 2 + jnp.minimum(t % tpb, 1), 0, 0)
    mod_spec = lambda d, u: pl.BlockSpec((None, 2, d), at(sel, u))
    rope_spec = lambda u: pl.BlockSpec((tm, LANES), at(lambda t: (t % tpb, 0), u))
    return at, mod_spec, rope_spec


def _inproj_ab(ctx2, x2, ab, w, wg, wgt, gb, gbt, cos, sin, tiles_per_batch):
    d = x2.shape[1]
    r = ctx2.shape[0] + x2.shape[0]
    tm, tps = ROW_TILE, TILES_PER_STEP
    n = w.shape[1]
    tpb = tiles_per_batch
    assert (r // tm) % tps == 0
    at, mod_spec, rope_spec = _tile_specs(tm, tpb, tps)
    x_specs, rope_specs = [], []
    for u in range(tps):
        x_specs += _split_row_specs(tm, d, tpb, at(lambda t: t, u))
        rope_specs += [rope_spec(u), rope_spec(u)]
    return pl.pallas_call(
        functools.partial(_inproj_ab_kernel, tpb=tpb),
        grid=(r // (tm * tps),),
        in_specs=x_specs + [mod_spec(d, u) for u in range(tps)] + [
                  _const_spec(w.shape), _const_spec(wg.shape), _const_spec(wgt.shape),
                  _const_spec(gb.shape), _const_spec(gbt.shape)] + rope_specs,
        out_specs=[pl.BlockSpec((tps * tm, n), lambda i: (i, 0)),
                   pl.BlockSpec((tps * tm, LANES), lambda i: (i, 0)),
                   pl.BlockSpec((32, tps * tm), lambda i: (0, i))],
        out_shape=[jax.ShapeDtypeStruct((r, n), BF16),
                   jax.ShapeDtypeStruct((r, LANES), F32),
                   jax.ShapeDtypeStruct((32, r), F32)],
        compiler_params=_cparams("arbitrary"),
        name="inproj_ret_mlstm",
    )(*([ctx2, x2] * tps), *([ab] * tps), w, wg, wgt, gb, gbt, *([cos, sin] * tps))


def _inproj_attn_kernel(*refs):
    tps = TILES_PER_STEP
    x_ref, ab_refs, refs = refs[0], refs[1:1 + tps], refs[1 + tps:]
    w_ref, nw_ref = refs[:2]
    rope_refs, y_ref = refs[2:2 + 2 * tps], refs[2 + 2 * tps]
    tm = y_ref.shape[0] // tps
    hbs = [_norm_mod(x_ref[u * tm:(u + 1) * tm, :], ab_refs[u]) for u in range(tps)]
    r2 = lax.broadcasted_iota(jnp.int32, (2 * LANES, 2 * LANES), 0)
    c2 = lax.broadcasted_iota(jnp.int32, (2 * LANES, 2 * LANES), 1)
    same_head = (((r2 ^ c2) & (LANES | 32)) == 0).astype(BF16)
    acc_q = [_dot(hbs[u], w_ref[:, 0:8 * LANES]) for u in range(tps)]
    acc_k = [_dot(hbs[u], w_ref[:, 8 * LANES:10 * LANES]) for u in range(tps)]
    acc_v = [_dot(hbs[u], w_ref[:, 10 * LANES:12 * LANES]) for u in range(tps)]
    for j in range(5):
        for u in range(tps):
            acc = acc_q[u][:, j * 2 * LANES:(j + 1) * 2 * LANES] if j < 4 else acc_k[u]
            sq = acc * acc
            hi = sq.astype(BF16)
            lo = (sq - hi.astype(F32)).astype(BF16)
            ms = (_dot(hi, same_head) + _dot(lo, same_head)) * (1.0 / HEAD_DIM)
            nrm = acc * lax.rsqrt(ms + EPS)
            nw = nw_ref[0:1, :] if j < 4 else nw_ref[1:2, :]
            cos, sin = rope_refs[2 * u][...], rope_refs[2 * u + 1][...]
            for v in range(2):
                ls = slice(v * LANES, (v + 1) * LANES)
                y_ref[u * tm:(u + 1) * tm, (2 * j + v) * LANES:(2 * j + v + 1) * LANES] = (
                    _rope(nrm[:, ls] * nw, cos, sin).astype(BF16))
    for u in range(tps):
        y_ref[u * tm:(u + 1) * tm, 10 * LANES:12 * LANES] = acc_v[u].astype(BF16)


def _inproj_attn(x2, ab, w, nw, cos, sin, tiles_per_batch):
    r, d = x2.shape
    tm, tps = ROW_TILE, TILES_PER_STEP
    n = w.shape[1]
    tpb = tiles_per_batch
    assert (r // tm) % tps == 0
    _, mod_spec, rope_spec = _tile_specs(tm, tpb, tps)
    rope_specs = []
    for u in range(tps):
        rope_specs += [rope_spec(u), rope_spec(u)]
    return pl.pallas_call(
        _inproj_attn_kernel,
        grid=(r // (tm * tps),),
        in_specs=[pl.BlockSpec((tps * tm, d), lambda i: (i, 0))] + [mod_spec(d, u) for u in range(tps)]
                 + [_const_spec(w.shape), _const_spec(nw.shape)] + rope_specs,
        out_specs=pl.BlockSpec((tps * tm, n), lambda i: (i, 0)),
        out_shape=jax.ShapeDtypeStruct((r, n), BF16),
        compiler_params=_cparams("arbitrary"),
        name="inproj_attn",
    )(x2, *([ab] * tps), w, nw, *([cos, sin] * tps))


def _conv_silu(cur_ref, prev_ref, next_ref, cw_ref, prev_on, next_on):
    cur = cur_ref[...].astype(F32)
    n = cur.shape[0]
    row = lax.broadcasted_iota(jnp.int32, cur.shape, 0)
    prev_row = prev_ref[HALO - 1:HALO, :].astype(F32) * prev_on
    next_row = next_ref[0:1, :].astype(F32) * next_on
    xm = jnp.where(row == 0, prev_row, pltpu.roll(cur, 1, 0))
    xp = jnp.where(row == n - 1, next_row, pltpu.roll(cur, n - 1, 0))
    y = cw_ref[3:4, :] + cw_ref[0:1, :] * xm + cw_ref[1:2, :] * cur + cw_ref[2:3, :] * xp
    return y * jax.nn.sigmoid(y)


def _segment_flags(c, nc, ctx_chunks):
    prev_on = jnp.where((c == 0) | (c == ctx_chunks), 0.0, 1.0).astype(F32)
    next_on = jnp.where((c == ctx_chunks - 1) | (c == nc - 1), 0.0, 1.0).astype(F32)
    return prev_on, next_on


def _cumsum_cols(tri_bf, lf):
    hi, mid, lo = _split3(lf)
    return _dot(tri_bf, hi) + _dot(tri_bf, mid) + _dot(tri_bf, lo)


def _cumsum_rows(lf, tri_bf):
    hi, mid, lo = _split3(lf)
    return _dot(hi, tri_bf) + _dot(mid, tri_bf) + _dot(lo, tri_bf)


def _ret_state_update(s_ref, p, k2, v2, kdec, cd_lanes, bd):
    kf = (k2.astype(F32) * kdec).astype(BF16)
    s_ref[p] = s_ref[p] * cd_lanes + jnp.where(bd, _dot_tn(v2, kf), 0.0)


def _mlstm_state_update(c_ref, n_ref, m_ref, k_pairs, v_pairs, c_all, bend, col0, lo, bd):
    cmax = jnp.max(c_all, axis=0, keepdims=True)
    w_all = jnp.exp(c_all - cmax)
    m_old = m_ref[0:1, :]
    mrel = jnp.maximum(m_old, cmax)
    a_row = jnp.exp(m_old - mrel)
    bb_row = jnp.exp(cmax - mrel)
    m_ref[0:1, :] = bend + mrel
    lo_row = lo[0:1, :]
    for p in range(4):
        h0 = col0 + 2 * p
        kw = k_pairs[p] * jnp.where(lo, w_all[:, h0:h0 + 1], w_all[:, h0 + 1:h0 + 2])
        kvt = _dot_tn(v_pairs[p], kw.astype(BF16))
        nloc = jnp.sum(kw, axis=0, keepdims=True)
        a_l = jnp.where(lo_row, a_row[:, h0:h0 + 1], a_row[:, h0 + 1:h0 + 2])
        bb_l = jnp.where(lo_row, bb_row[:, h0:h0 + 1], bb_row[:, h0 + 1:h0 + 2])
        c_ref[p] = c_ref[p] * a_l + jnp.where(bd, kvt, 0.0) * bb_l
        n_new = (n_ref[p, 0:1, :] + n_ref[p, 1:2, :]) * a_l + nloc * bb_l
        n_ref[p, 0:1, :] = jnp.where(lo_row, n_new, 0.0)
        n_ref[p, 1:2, :] = jnp.where(lo_row, 0.0, n_new)


def _bwd_state_kernel(rk_ref, rv_ref, mk_ref, mkp_ref, mkn_ref, mv_ref, g_ref,
                      lgk_ref, cw_ref,
                      sret_ref, cm_ref, nm_ref, mm_ref,
                      s_scr, c_scr, n_scr, m_scr, kdec_scr, *, nc, ctx_chunks):
    i = pl.program_id(1)
    c = jnp.where(i < ctx_chunks, ctx_chunks - 1 - i, nc + ctx_chunks - 1 - i)
    lane = _lane()
    sub = lax.broadcasted_iota(jnp.int32, (CHUNK, LANES), 0)
    lo = lane < HEAD_DIM
    bd_ret = (sub >= HEAD_DIM) == ((lane & 32) != 0)
    bd_m = (sub >= HEAD_DIM) == (lane >= HEAD_DIM)

    @pl.when(i == 0)
    def _():
        s_scr[...] = jnp.zeros_like(s_scr)
        c_scr[...] = jnp.zeros_like(c_scr)
        n_scr[...] = jnp.zeros_like(n_scr)
        m_scr[...] = jnp.zeros_like(m_scr)
        pos = sub.astype(F32)
        for p in range(4):
            kdec_scr[p] = jnp.exp(lgk_ref[4 + p:5 + p, :] * pos)

    sret_ref[...] = s_scr[...].astype(BF16)
    cm_ref[...] = c_scr[...].astype(BF16)
    nm_ref[...] = n_scr[...].astype(BF16)
    mm_ref[...] = m_scr[...]

    for p in range(4):
        cd = jnp.exp(lgk_ref[4 + p:5 + p, :] * float(CHUNK))
        _ret_state_update(s_scr, p, rk_ref[:, p * LANES:(p + 1) * LANES],
                          rv_ref[:, p * LANES:(p + 1) * LANES], kdec_scr[p], cd, bd_ret)

    prev_on, next_on = _segment_flags(c, nc, ctx_chunks)
    kc = _conv_silu(mk_ref, mkp_ref, mkn_ref, cw_ref, prev_on, next_on)
    g = g_ref[...]
    le_bf = (sub <= lane).astype(BF16)
    bal = pltpu.roll(_cumsum_cols(le_bf, _log_sigmoid(g)), LANES - 8, 1)
    _mlstm_state_update(c_scr, n_scr, m_scr,
                        [kc[:, p * LANES:(p + 1) * LANES] for p in range(4)],
                        [mv_ref[:, p * LANES:(p + 1) * LANES] for p in range(4)],
                        g - bal, bal[0:1, :], 16, lo, bd_m)


def _bwd_states(y, g, lgk, cw_k, nb, nc, ctx_chunks):
    gw = 4 * LANES
    def cidx(b, i):
        c = jnp.where(i < ctx_chunks, ctx_chunks - 1 - i, nc + ctx_chunks - 1 - i)
        return b * nc + c
    hb = CHUNK // HALO
    nhalo = y.shape[0] // HALO
    return pl.pallas_call(
        functools.partial(_bwd_state_kernel, nc=nc, ctx_chunks=ctx_chunks),
        grid=(nb, nc),
        in_specs=[pl.BlockSpec((CHUNK, gw), lambda b, i: (cidx(b, i), 1)),
                  pl.BlockSpec((CHUNK, gw), lambda b, i: (cidx(b, i), 2)),
                  pl.BlockSpec((CHUNK, gw), lambda b, i: (cidx(b, i), 5)),
                  pl.BlockSpec((HALO, gw), lambda b, i: (jnp.maximum(cidx(b, i) * hb - 1, 0), 5)),
                  pl.BlockSpec((HALO, gw), lambda b, i: (jnp.minimum((cidx(b, i) + 1) * hb, nhalo - 1), 5)),
                  pl.BlockSpec((CHUNK, gw), lambda b, i: (cidx(b, i), 6)),
                  pl.BlockSpec((CHUNK, LANES), lambda b, i: (cidx(b, i), 0)),
                  _const_spec(lgk.shape), _const_spec(cw_k.shape)],
        out_specs=[pl.BlockSpec((None, 4, CHUNK, LANES), lambda b, i: (cidx(b, i), 0, 0, 0)),
                   pl.BlockSpec((None, 4, CHUNK, LANES), lambda b, i: (cidx(b, i), 0, 0, 0)),
                   pl.BlockSpec((None, 4, HALO, LANES), lambda b, i: (cidx(b, i), 0, 0, 0)),
                   pl.BlockSpec((None, 8, LANES), lambda b, i: (cidx(b, i), 0, 0))],
        out_shape=[jax.ShapeDtypeStruct((nb * nc, 4, CHUNK, LANES), BF16),
                   jax.ShapeDtypeStruct((nb * nc, 4, CHUNK, LANES), BF16),
                   jax.ShapeDtypeStruct((nb * nc, 4, HALO, LANES), BF16),
                   jax.ShapeDtypeStruct((nb * nc, 8, LANES), F32)],
        scratch_shapes=[pltpu.VMEM((4, CHUNK, LANES), F32), pltpu.VMEM((4, CHUNK, LANES), F32),
                        pltpu.VMEM((4, HALO, LANES), F32), pltpu.VMEM((8, LANES), F32),
                        pltpu.VMEM((4, CHUNK, LANES), F32)],
        compiler_params=_cparams("arbitrary", "arbitrary"),
        name="bwd_state_sweep",
    )(y, y, y, y, y, y, g, lgk, cw_k)


def _mlstm_dir_weights(st, qn_row, c_col, bt_row, m_prev, tri):
    dl = jnp.where(tri, c_col + bt_row, NEG)
    mx = jnp.max(dl, axis=0, keepdims=True)
    al = bt_row + m_prev
    m_t = jnp.maximum(al, mx)
    w = jnp.exp(dl - m_t)
    a_t = jnp.exp(al - m_t)
    sw = st * w
    den = jnp.sum(sw, axis=0, keepdims=True) + a_t * qn_row
    r = 1.0 / jnp.maximum(jnp.abs(den), jnp.exp(-m_t))
    return sw * r, a_t * r


def _heads_out(ht, nw_tab):
    rows = []
    for a in range(2):
        ha = ht[a * HEAD_DIM:(a + 1) * HEAD_DIM, :]
        ms = jnp.mean(ha * ha, axis=0, keepdims=True)
        rows.append(ha * lax.rsqrt(ms + EPS))
    return (jnp.concatenate(rows, axis=0) * nw_tab).T


def _mixer_kernel(lg_ref, rq_ref, rk_ref, rv_ref, rg_ref, mqk_ref, mqkp_ref, mqkn_ref, mv_ref, mo_ref,
                  g_ref, gt_ref, sretb_ref, cmb_ref, nmb_ref, mmb_ref,
                  lgk_ref, lgkt_ref, cw_ref, nw_ref,
                  out_ref,
                  s_scr, c_scr, n_scr, m_scr, dm_scr, dec_scr, *, nc, ctx_chunks):
    c = pl.program_id(1)
    lane = _lane()
    sub = lax.broadcasted_iota(jnp.int32, (CHUNK, LANES), 0)
    lo = lane < HEAD_DIM
    sub_lo = sub < HEAD_DIM
    mask_ret = [(lane & 32) == 0, (lane & 32) != 0]
    mask_nat = [lo, lane >= HEAD_DIM]
    bd_ret = (sub >= HEAD_DIM) == ((lane & 32) != 0)
    bd_m = (sub >= HEAD_DIM) == (lane >= HEAD_DIM)
    le = sub <= lane
    ge = sub >= lane

    @pl.when(c == 0)
    def _():
        s_scr[...] = jnp.zeros_like(s_scr)
        c_scr[...] = jnp.zeros_like(c_scr)
        n_scr[...] = jnp.zeros_like(n_scr)
        m_scr[...] = jnp.zeros_like(m_scr)
        spos = sub.astype(F32)
        tpos = lane.astype(F32)
        diff = (lane - sub).astype(F32)
        for h in range(8):
            dm_scr[h] = (jnp.where(le, jnp.exp(lg_ref[h] * diff), 0.0)
                         + jnp.where(ge, jnp.exp(lg_ref[8 + h] * (-diff)), 0.0))
        for p in range(4):
            dec_scr[0, p] = jnp.exp(lgkt_ref[:, p:p + 1] * (tpos + 1.0))
            dec_scr[1, p] = jnp.exp(lgkt_ref[:, 4 + p:5 + p] * (float(CHUNK) - tpos))
            dec_scr[2, p] = jnp.exp(lgk_ref[p:p + 1, :] * (float(CHUNK) - 1.0 - spos))


    prev_on, next_on = _segment_flags(c, nc, ctx_chunks)
    qk = _conv_silu(mqk_ref, mqkp_ref, mqkn_ref, cw_ref, prev_on, next_on)
    g = g_ref[...]
    gt = gt_ref[...]
    lf_col = _log_sigmoid(g)
    lf_row = _log_sigmoid(gt)
    le_bf = le.astype(BF16)
    ge_bf = ge.astype(BF16)
    bal_f = pltpu.roll(_cumsum_cols(ge_bf, lf_col), LANES - 8, 1)
    bal_b = pltpu.roll(_cumsum_cols(le_bf, lf_col), LANES - 8, 1)
    cf_all = g - bal_f
    cb_all = g - bal_b
    bf_row = _cumsum_rows(lf_row, le_bf)
    bb_row = _cumsum_rows(lf_row, ge_bf)

    qb, kb, vb, kf32, qt, vbd = [], [], [], [], [], []
    for p in range(8):
        sl = slice((p % 4) * LANES, (p % 4 + 1) * LANES)
        if p < 4:
            q2, k2, v2 = rq_ref[:, sl], rk_ref[:, sl], rv_ref[:, sl]
            qf = q2.astype(F32)
            kf = None
            old =jnp.concatenate([s_scr[p].astype(BF16), sretb_ref[p]], axis=1)
        else:
            qf = qk[:, sl] * (HEAD_DIM ** -0.5)
            kf = qk[:, 4 * LANES + (p - 4) * LANES:4 * LANES + (p - 3) * LANES]
            q2, k2, v2 = qf.astype(BF16), kf.astype(BF16), mv_ref[:, sl]
            old = jnp.concatenate([c_scr[p - 4].astype(BF16), cmb_ref[p - 4]], axis=1)
        vt = v2.astype(F32).T.astype(BF16)
        qb.append(q2)
        kb.append(k2)
        vb.append(v2)
        kf32.append(kf)
        qt.append(qf.T)
        vbd.append(jnp.concatenate([jnp.where(sub_lo, vt, jnp.zeros_like(vt)),
                                    jnp.where(sub_lo, jnp.zeros_like(vt), vt), old], axis=1))

    st2, qn_f, qn_b = [], [], []
    for p in range(8):
        masks = mask_ret if p < 4 else mask_nat
        zero = jnp.zeros_like(qb[p])
        qstack = jnp.concatenate([jnp.where(masks[0], qb[p], zero), jnp.where(masks[1], qb[p], zero)], axis=0)
        st2.append(_dot_nt(kb[p], qstack))
        if p >= 4:
            qn_f.append(_dot_nt(n_scr[p - 4].astype(BF16), qb[p]))
            qn_b.append(_dot_nt(nmb_ref[p - 4], qb[p]))

    rhs = []
    for p in range(8):
        if p < 4:
            pts = [(st2[p][:, a * LANES:(a + 1) * LANES] * dm_scr[2 * p + a]).astype(BF16) for a in range(2)]
            x_f, x_b = dec_scr[0, p], dec_scr[1, p]
        else:
            pts, cf, cb = [], [], []
            for a in range(2):
                h = 2 * (p - 4) + a
                st = st2[p][:, a * LANES:(a + 1) * LANES]
                pf, coef_f = _mlstm_dir_weights(st, qn_f[p - 4][a:a + 1, :], cf_all[:, h:h + 1],
                                                bf_row[8 + h:9 + h, :], m_scr[0:1, h:h + 1], le)
                pb, coef_b = _mlstm_dir_weights(st, qn_b[p - 4][a:a + 1, :], cb_all[:, 16 + h:17 + h],
                                                bb_row[24 + h:25 + h, :], mmb_ref[0:1, 16 + h:17 + h], ge)
                pts.append((pf + pb).astype(BF16))
                cf.append(coef_f)
                cb.append(coef_b)
            x_f = jnp.where(sub_lo, cf[0], cf[1])
            x_b = jnp.where(sub_lo, cb[0], cb[1])
        rhs.append(jnp.concatenate(pts + [(qt[p] * x_f).astype(BF16), (qt[p] * x_b).astype(BF16)], axis=0))

    ht = [_dot(vbd[p], rhs[p]) for p in range(8)]

    for p in range(8):
        sl = slice((p % 4) * LANES, (p % 4 + 1) * LANES)
        gate_ref = rg_ref if p < 4 else mo_ref
        y = _heads_out(ht[p], nw_ref[p])
        out_ref[:, p * LANES:(p + 1) * LANES] = (y * gate_ref[:, sl].astype(F32)).astype(BF16)

    for p in range(4):
        cd = jnp.exp(lgk_ref[p:p + 1, :] * float(CHUNK))
        _ret_state_update(s_scr, p, kb[p], vb[p], dec_scr[2, p], cd, bd_ret)
    _mlstm_state_update(c_scr, n_scr, m_scr, kf32[4:], vb[4:], cf_all, bal_f[CHUNK - 1:CHUNK, :], 0, lo, bd_m)


def _mixer(y, g, gt, states, lg_smem, lgk, lgkt, cw_qk, nw, nb, nc, ctx_chunks):
    gw = 4 * LANES
    sretb, cmb, nmb, mmb = states
    rc = lambda b, c: b * nc + c
    hb = CHUNK // HALO
    nhalo = y.shape[0] // HALO
    blk = lambda j: pl.BlockSpec((CHUNK, gw), lambda b, c: (rc(b, c), j))
    return pl.pallas_call(
        functools.partial(_mixer_kernel, nc=nc, ctx_chunks=ctx_chunks),
        grid=(nb, nc),
        in_specs=[_smem_spec(),
                  blk(0), blk(1), blk(2), blk(3),
                  pl.BlockSpec((CHUNK, 2 * gw), lambda b, c: (rc(b, c), 2)),
                  pl.BlockSpec((HALO, 2 * gw), lambda b, c: (jnp.maximum(rc(b, c) * hb - 1, 0), 2)),
                  pl.BlockSpec((HALO, 2 * gw), lambda b, c: (jnp.minimum((rc(b, c) + 1) * hb, nhalo - 1), 2)),
                  blk(6), blk(7),
                  pl.BlockSpec((CHUNK, LANES), lambda b, c: (rc(b, c), 0)),
                  pl.BlockSpec((32, CHUNK), lambda b, c: (0, rc(b, c))),
                  pl.BlockSpec((None, 4, CHUNK, LANES), lambda b, c: (rc(b, c), 0, 0, 0)),
                  pl.BlockSpec((None, 4, CHUNK, LANES), lambda b, c: (rc(b, c), 0, 0, 0)),
                  pl.BlockSpec((None, 4, HALO, LANES), lambda b, c: (rc(b, c), 0, 0, 0)),
                  pl.BlockSpec((None, 8, LANES), lambda b, c: (rc(b, c), 0, 0)),
                  _const_spec(lgk.shape), _const_spec(lgkt.shape), _const_spec(cw_qk.shape),
                  _const_spec(nw.shape)],
        out_specs=pl.BlockSpec((CHUNK, 2 * gw), lambda b, c: (rc(b, c), 0)),
        out_shape=jax.ShapeDtypeStruct((nb * nc * CHUNK, 2 * gw), BF16),
        scratch_shapes=[pltpu.VMEM((4, CHUNK, LANES), F32), pltpu.VMEM((4, CHUNK, LANES), F32),
                        pltpu.VMEM((4, HALO, LANES), F32), pltpu.VMEM((8, LANES), F32),
                        pltpu.VMEM((8, CHUNK, LANES), F32), pltpu.VMEM((3, 4, CHUNK, LANES), F32)],
        compiler_params=_cparams("arbitrary", "arbitrary"),
        name="ret_mlstm_mixer",
    )(lg_smem, y, y, y, y, y, y, y, y, y, g, gt, sretb, cmb, nmb, mmb, lgk, lgkt, cw_qk, nw)


class _Bag:
    def __init__(self, **kw):
        self.__dict__.update(kw)


def _mixer_chunk_stages(blk, r, qk):
    rows = slice(blk * CHUNK, (blk + 1) * CHUNK)
    lane = _lane()
    sub = lax.broadcasted_iota(jnp.int32, (CHUNK, LANES), 0)
    lo = lane < HEAD_DIM
    sub_lo = sub < HEAD_DIM
    mask_ret = [(lane & 32) == 0, (lane & 32) != 0]
    mask_nat = [lo, lane >= HEAD_DIM]
    bd_ret = (sub >= HEAD_DIM) == ((lane & 32) != 0)
    bd_m = (sub >= HEAD_DIM) == (lane >= HEAD_DIM)
    le = sub <= lane
    ge = sub >= lane

    g = r.g_ref[rows, :]
    gt = r.gt_ref[:, rows]
    lf_col = _log_sigmoid(g)
    lf_row = _log_sigmoid(gt)
    le_bf = le.astype(BF16)
    ge_bf = ge.astype(BF16)
    bal_f = pltpu.roll(_cumsum_cols(ge_bf, lf_col), LANES - 8, 1)
    bal_b = pltpu.roll(_cumsum_cols(le_bf, lf_col), LANES - 8, 1)
    cf_all = g - bal_f
    cb_all = g - bal_b
    bf_row = _cumsum_rows(lf_row, le_bf)
    bb_row = _cumsum_rows(lf_row, ge_bf)
    qb, kb, vb, kf32, qt, vbd = [], [], [], [], [], []
    for p in range(8):
        sl = slice((p % 4) * LANES, (p % 4 + 1) * LANES)
        if p < 4:
            q2, k2, v2 = r.rq_ref[rows, sl], r.rk_ref[rows, sl], r.rv_ref[rows, sl]
            qf = q2.astype(F32)
            kf = None
        else:
            qf = qk[rows, sl] * (HEAD_DIM ** -0.5)
            kf = qk[rows, 4 * LANES + (p - 4) * LANES:4 * LANES + (p - 3) * LANES]
            q2, k2, v2 = qf.astype(BF16), kf.astype(BF16), r.mv_ref[rows, sl]
        vt = v2.astype(F32).T.astype(BF16)
        qb.append(q2)
        kb.append(k2)
        vb.append(v2)
        kf32.append(kf)
        qt.append(qf.T)
        vbd.append([jnp.where(sub_lo, vt, jnp.zeros_like(vt)), jnp.where(sub_lo, jnp.zeros_like(vt), vt)])
    yield

    st2 = []
    for p in range(8):
        masks = mask_ret if p < 4 else mask_nat
        zero = jnp.zeros_like(qb[p])
        qstack = jnp.concatenate([jnp.where(masks[0], qb[p], zero), jnp.where(masks[1], qb[p], zero)], axis=0)
        st2.append(_dot_nt(kb[p], qstack))
    yield

    qn_f = [_dot_nt(r.n_scr[p].astype(BF16), qb[4 + p]) for p in range(4)]
    qn_b = [_dot_nt(r.nmb_ref[blk, p], qb[4 + p]) for p in range(4)]
    lhs, rhs = [], []
    for p in range(8):
        if p < 4:
            pts = [(st2[p][:, a * LANES:(a + 1) * LANES] * r.dm_scr[2 * p + a]).astype(BF16) for a in range(2)]
            x_f, x_b = r.dec_scr[0, p], r.dec_scr[1, p]
            old = [r.s_scr[p].astype(BF16), r.sretb_ref[blk, p]]
        else:
            pts, cf, cb = [], [], []
            for a in range(2):
                h = 2 * (p - 4) + a
                st = st2[p][:, a * LANES:(a + 1) * LANES]
                pf, coef_f = _mlstm_dir_weights(st, qn_f[p - 4][a:a + 1, :], cf_all[:, h:h + 1],
                                                bf_row[8 + h:9 + h, :], r.m_state[0:1, h:h + 1], le)
                pb, coef_b = _mlstm_dir_weights(st, qn_b[p - 4][a:a + 1, :], cb_all[:, 16 + h:17 + h],
                                                bb_row[24 + h:25 + h, :], r.mmb_ref[blk, 0:1, 16 + h:17 + h], ge)
                pts.append((pf + pb).astype(BF16))
                cf.append(coef_f)
                cb.append(coef_b)
            x_f = jnp.where(sub_lo, cf[0], cf[1])
            x_b = jnp.where(sub_lo, cb[0], cb[1])
            old = [r.c_scr[p - 4].astype(BF16), r.cmb_ref[blk, p - 4]]
        lhs.append(jnp.concatenate(vbd[p] + old, axis=1))
        rhs.append(jnp.concatenate(pts + [(qt[p] * x_f).astype(BF16), (qt[p] * x_b).astype(BF16)], axis=0))
    yield

    ht = [_dot(lhs[p], rhs[p]) for p in range(8)]
    yield

    for p in range(8):
        sl = slice((p % 4) * LANES, (p % 4 + 1) * LANES)
        gate_ref = r.rg_ref if p < 4 else r.mo_ref
        y = _heads_out(ht[p], r.nw_ref[p])
        r.mix_scr[rows, p * LANES:(p + 1) * LANES] = (y * gate_ref[rows, sl].astype(F32)).astype(BF16)
    yield

    for p in range(4):
        cd = jnp.exp(r.lgk_ref[p:p + 1, :] * float(CHUNK))
        _ret_state_update(r.s_scr, p, kb[p], vb[p], r.dec_scr[2, p], cd, bd_ret)
    _mlstm_state_update(r.c_scr, r.n_scr, r.m_state, kf32[4:], vb[4:], cf_all, bal_f[CHUNK - 1:CHUNK, :],
                        0, lo, bd_m)
    yield


def _ffn_splits(d_ff, pieces):
    blocks = d_ff // (2 * LANES)
    assert blocks * 2 * LANES == d_ff and blocks >= pieces
    cuts = [((i * blocks) // pieces) * 2 * LANES for i in range(pieces)]
    return cuts + [d_ff]


def _mixer_ffn_kernel(lg_ref, rq_ref, rk_ref, rv_ref, rg_ref, mqk_ref, mqkp_ref, mqkn_ref, mv_ref, mo_ref,
                      g_ref, gt_ref, sretb_ref, cmb_ref, nmb_ref, mmb_ref,
                      lgk_ref, lgkt_ref, cw_ref, nw_ref,
                      ctx_ref, x_ref, p_ref, wo_ref, wi_ref, w2_ref,
                      o_ref,
                      s_scr, c_scr, n_scr, m_state, dm_scr, dec_scr, mix_scr, act_scr, *, tiles, tpb, d_ff):
    s = pl.program_id(0)
    jt = jnp.minimum(s, tiles - 1) % tpb
    lane = _lane()
    sub = lax.broadcasted_iota(jnp.int32, (CHUNK, LANES), 0)

    @pl.when(s == 0)
    def _():
        mix_scr[...] = jnp.zeros_like(mix_scr)
        le = sub <= lane
        ge = sub >= lane
        spos = sub.astype(F32)
        tpos = lane.astype(F32)
        diff = (lane - sub).astype(F32)
        for h in range(8):
            dm_scr[h] = (jnp.where(le, jnp.exp(lg_ref[h] * diff), 0.0)
                         + jnp.where(ge, jnp.exp(lg_ref[8 + h] * (-diff)), 0.0))
        for p in range(4):
            dec_scr[0, p] = jnp.exp(lgkt_ref[:, p:p + 1] * (tpos + 1.0))
            dec_scr[1, p] = jnp.exp(lgkt_ref[:, 4 + p:5 + p] * (float(CHUNK) - tpos))
            dec_scr[2, p] = jnp.exp(lgk_ref[p:p + 1, :] * (float(CHUNK) - 1.0 - spos))

    @pl.when(jt == 0)
    def _():
        s_scr[...] = jnp.zeros_like(s_scr)
        c_scr[...] = jnp.zeros_like(c_scr)
        n_scr[...] = jnp.zeros_like(n_scr)
        m_state[...] = jnp.zeros_like(m_state)

    r = _Bag(rq_ref=rq_ref, rk_ref=rk_ref, rv_ref=rv_ref, rg_ref=rg_ref, mv_ref=mv_ref, mo_ref=mo_ref,
             g_ref=g_ref, gt_ref=gt_ref, sretb_ref=sretb_ref, cmb_ref=cmb_ref, nmb_ref=nmb_ref, mmb_ref=mmb_ref,
             lgk_ref=lgk_ref, nw_ref=nw_ref, s_scr=s_scr, c_scr=c_scr, n_scr=n_scr, m_state=m_state,
             dm_scr=dm_scr, dec_scr=dec_scr, mix_scr=mix_scr)
    cuts = _ffn_splits(d_ff, 4)
    ffn_piece = lambda i: _ffn_cols(h, wi_ref, act_scr, cuts[i], cuts[i + 1], d_ff)

    x = _ctx_or_latent_rows(ctx_ref, x_ref, jnp.maximum(s - 1, 0), tpb)
    x1, h = _ffn_in(x, mix_scr[...], p_ref, wo_ref)

    prev_on = jnp.where(jt <= 1, 0.0, 1.0).astype(F32)
    next_on = jnp.where((jt == 0) | (jt == tpb - 1), 0.0, 1.0).astype(F32)
    qk = _conv_silu(mqk_ref, mqkp_ref, mqkn_ref, cw_ref, prev_on, next_on)
    chunk_a, chunk_b = _mixer_chunk_stages(0, r, qk), _mixer_chunk_stages(1, r, qk)
    next(chunk_a), next(chunk_b)
    next(chunk_a), next(chunk_b)
    ffn_piece(0)
    next(chunk_a), next(chunk_a)
    ffn_piece(1)
    next(chunk_a), next(chunk_a)
    ffn_piece(2)
    next(chunk_b), next(chunk_b)
    ffn_piece(3)
    next(chunk_b), next(chunk_b)
    o_ref[...] = x1 + p_ref[3:4, :] * _dot(act_scr[...], w2_ref[...])


def _mixer_ffn(y, g, gt, states, lg_smem, lgk, lgkt, cw_qk, nw, xs, prm, wo, wi_all, w2_all, layer, nb, tpb):
    gw = 4 * LANES
    tm = ROW_TILE
    cpt = tm // CHUNK
    sretb, cmb, nmb, mmb = states
    tiles = nb * tpb
    d = xs[-1].shape[1]
    d_ff = w2_all.shape[1]
    mix = lambda s: jnp.minimum(s, tiles - 1)
    ffn = lambda s: jnp.maximum(s - 1, 0)
    hb = tm // HALO
    nhalo = y.shape[0] // HALO
    blk = lambda j: pl.BlockSpec((tm, gw), lambda s: (mix(s), j))
    state_spec = lambda a: pl.BlockSpec((cpt,) + a.shape[1:], lambda s: (mix(s),) + (0,) * (a.ndim - 1))
    sel = lambda t: ((t // tpb) * 2 + jnp.minimum(t % tpb, 1), 0, 0)
    layer_spec = lambda a: pl.BlockSpec((None,) + a.shape[1:], lambda s: (layer, 0, 0),
                                        pipeline_mode=pl.Buffered(1))
    return pl.pallas_call(
        functools.partial(_mixer_ffn_kernel, tiles=tiles, tpb=tpb, d_ff=d_ff),
        grid=(tiles + 1,),
        in_specs=[_smem_spec(),
                  blk(0), blk(1), blk(2), blk(3),
                  pl.BlockSpec((tm, 2 * gw), lambda s: (mix(s), 2)),
                  pl.BlockSpec((HALO, 2 * gw), lambda s: (jnp.maximum(mix(s) * hb - 1, 0), 2)),
                  pl.BlockSpec((HALO, 2 * gw), lambda s: (jnp.minimum((mix(s) + 1) * hb, nhalo - 1), 2)),
                  blk(6), blk(7),
                  pl.BlockSpec((tm, LANES), lambda s: (mix(s), 0)),
                  pl.BlockSpec((32, tm), lambda s: (0, mix(s))),
                  state_spec(sretb), state_spec(cmb), state_spec(nmb), state_spec(mmb),
                  _const_spec(lgk.shape), _const_spec(lgkt.shape), _const_spec(cw_qk.shape),
                  _const_spec(nw.shape)]
                 + _split_row_specs(tm, d, tpb, ffn)
                 + [pl.BlockSpec((None, 8, d), lambda s: sel(ffn(s))),
                    _const_spec(wo.shape), layer_spec(wi_all), layer_spec(w2_all)],
        out_specs=pl.BlockSpec((tm, d), lambda s: (ffn(s), 0)),
        out_shape=jax.ShapeDtypeStruct((tiles * tm, d), F32),
        scratch_shapes=[pltpu.VMEM((4, CHUNK, LANES), F32), pltpu.VMEM((4, CHUNK, LANES), F32),
                        pltpu.VMEM((4, HALO, LANES), F32), pltpu.VMEM((8, LANES), F32),
                        pltpu.VMEM((8, CHUNK, LANES), F32), pltpu.VMEM((3, 4, CHUNK, LANES), F32),
                        pltpu.VMEM((tm, 2 * gw), BF16), pltpu.VMEM((tm, d_ff), BF16)],
        compiler_params=_cparams("arbitrary"),
        name="ret_mlstm_mixer_ffn",
    )(lg_smem, y, y, y, y, y, y, y, y, y, g, gt, sretb, cmb, nmb, mmb, lgk, lgkt, cw_qk, nw,
      *xs, prm, wo, wi_all, w2_all)


def _attn_kernel(sink_ref, q_ref, kp_ref, kc_ref, kn_ref, kx_ref, vp_ref, vc_ref, vn_ref, vx_ref,
                 bias_ref, o_ref):
    grp = lax.broadcasted_iota(jnp.int32, (1, 4 * CHUNK), 1) // CHUNK
    lane = _lane()
    mask_q = [(lane & 32) == 0, (lane & 32) != 0]
    bias = jnp.concatenate([bias_ref[...]] * 4, axis=1)
    vts, sts = [], []
    for kvp in range(2):
        sl = slice(kvp * LANES, (kvp + 1) * LANES)
        v_blocks = [vp_ref[:, sl], vc_ref[:, sl], vn_ref[:, sl], vx_ref[:, sl]]
        vts.append(jnp.concatenate(
            [vb[r * CHUNK:(r + 1) * CHUNK, :].astype(F32).T.astype(BF16)
             for vb in v_blocks for r in range(vb.shape[0] // CHUNK)], axis=1))
    for kvp in range(2):
        sl = slice(kvp * LANES, (kvp + 1) * LANES)
        kcat = jnp.concatenate([kp_ref[:, sl], kc_ref[:, sl], kn_ref[:, sl], kx_ref[:, sl]], axis=0)
        for a in range(2):
            qs = jnp.concatenate(
                [jnp.where(mask_q[a], q_ref[:, (kvp * 4 + g) * LANES:(kvp * 4 + g + 1) * LANES],
                           jnp.zeros((CHUNK, LANES), BF16)) for g in range(4)], axis=0)
            sts.append(_dot_nt(kcat, qs))
    outs = []
    for kv in range(4):
        st = sts[kv]
        st = jnp.concatenate([st[0:CHUNK] + bias[0:CHUNK], st[CHUNK:2 * CHUNK],
                              st[2 * CHUNK:3 * CHUNK] + bias[CHUNK:2 * CHUNK], st[3 * CHUNK:]], axis=0)
        snk = jnp.where(grp == 0, sink_ref[kv * 4],
                        jnp.where(grp == 1, sink_ref[kv * 4 + 1],
                                  jnp.where(grp == 2, sink_ref[kv * 4 + 2], sink_ref[kv * 4 + 3])))
        m = jnp.maximum(jnp.max(st, axis=0, keepdims=True), snk)
        e = jnp.exp(st - m)
        denom = jnp.exp(snk - m) + jnp.sum(e, axis=0, keepdims=True)
        a = kv % 2
        outs.append(_dot(vts[kv // 2][a * HEAD_DIM:(a + 1) * HEAD_DIM, :], e.astype(BF16)) * (1.0 / denom))
    for kvp in range(2):
        full = jnp.concatenate(outs[2 * kvp:2 * kvp + 2], axis=0)
        for g in range(4):
            o_ref[:, (kvp * 4 + g) * LANES:(kvp * 4 + g + 1) * LANES] = (
                full[:, g * CHUNK:(g + 1) * CHUNK].T.astype(BF16))


def _window_bias():
    kk = np.arange(CHUNK)[:, None]
    t = np.arange(CHUNK)[None, :]
    tabs = []
    for has_prev, has_next in ((False, True), (True, True), (True, False)):
        prev_ok = (kk >= t) & has_prev
        next_ok = (kk <= t) & has_next
        tabs.append(np.where(np.concatenate([prev_ok, next_ok], axis=0), 0.0, NEG))
    return jnp.asarray(np.stack(tabs), F32)


def _window_attn(y, sink, nb, nc, ctx_chunks):
    nq = nc - ctx_chunks
    assert nq >= 2
    ctx_len = ctx_chunks * CHUNK
    bias = _window_bias()
    d = 8 * LANES
    kcol, vcol = d // (2 * LANES), d // (2 * LANES) + 1
    qrow = lambda b, i: b * nc + ctx_chunks + i
    prow = lambda b, i: b * nc + ctx_chunks + jnp.maximum(i - 1, 0)
    nrow = lambda b, i: b * nc + ctx_chunks + jnp.minimum(i + 1, nq - 1)
    xrow = lambda b, i: (b * nc * CHUNK) // ctx_len
    kv_spec = lambda rowf, col: pl.BlockSpec((CHUNK, 2 * LANES), lambda b, i: (rowf(b, i), col))
    x_spec = lambda col: pl.BlockSpec((ctx_len, 2 * LANES), lambda b, i: (xrow(b, i), col))
    bias_spec = pl.BlockSpec((None,) + bias.shape[1:],
                             lambda b, i: (jnp.where(i == 0, 0, jnp.where(i == nq - 1, 2, 1)), 0, 0))
    return pl.pallas_call(
        _attn_kernel,
        grid=(nb, nq),
        in_specs=[_smem_spec(),
                  pl.BlockSpec((CHUNK, d), lambda b, i: (qrow(b, i), 0)),
                  kv_spec(prow, kcol), kv_spec(qrow, kcol), kv_spec(nrow, kcol), x_spec(kcol),
                  kv_spec(prow, vcol), kv_spec(qrow, vcol), kv_spec(nrow, vcol), x_spec(vcol),
                  bias_spec],
        out_specs=pl.BlockSpec((CHUNK, d), lambda b, i: (b * nq + i, 0)),
        out_shape=jax.ShapeDtypeStruct((nb * nq * CHUNK, d), BF16),
        compiler_params=_cparams("arbitrary", "arbitrary"),
        name="window_gqa",
    )(sink, y, y, y, y, y, y, y, y, y, bias)


def _ffn_in(x, m, p_ref, wo_ref):
    x1 = x + p_ref[0:1, :] * _dot(m, wo_ref[...])
    ms = jnp.mean(x1 * x1, axis=-1, keepdims=True)
    return x1, ((x1 * lax.rsqrt(ms + EPS)) * p_ref[1:2, :] + p_ref[2:3, :]).astype(BF16)


def _ffn_cols(h, wi_ref, act_scr, lo, hi, d_ff):
    gate = _dot(h, wi_ref[:, lo:hi])
    up = _dot(h, wi_ref[:, d_ff + lo:d_ff + hi])
    act_scr[:, lo:hi] = (gate * jax.nn.sigmoid(gate) * up).astype(BF16)


def _attn_ffn_kernel(sink_ref, q_ref, kp_ref, kc_ref, kn_ref, kx_ref, vp_ref, vc_ref, vn_ref, vx_ref, bias_ref,
                     x_ref, p_ref, wo_ref, wi_ref, w2_ref, o_ref, m_scr, act_scr, *, tiles, lat_tiles, d_ff):
    s = pl.program_id(0)

    @pl.when(s == 0)
    def _():
        m_scr[...] = jnp.zeros_like(m_scr)

    j = jnp.minimum(s, tiles - 1) % lat_tiles
    grp = lax.broadcasted_iota(jnp.int32, (1, 4 * CHUNK), 1) // CHUNK
    lane = _lane()
    mask_q = [(lane & 32) == 0, (lane & 32) != 0]
    tile4 = lambda b: jnp.concatenate([b] * 4, axis=1)
    biases = [tile4(jnp.where(j == 0, bias_ref[0], bias_ref[1])),
              tile4(jnp.where(j == lat_tiles - 1, bias_ref[2], bias_ref[1]))]
    third = (d_ff // 3 // (2 * LANES) + 1) * 2 * LANES
    splits = [0, third, 2 * third, d_ff]

    x1, h = _ffn_in(x_ref[...], m_scr[...], p_ref, wo_ref)

    kcats, vts = [], []
    for kvp in range(2):
        sl = slice(kvp * LANES, (kvp + 1) * LANES)
        k_chunks = [kp_ref[:, sl], kc_ref[0:CHUNK, sl], kc_ref[CHUNK:2 * CHUNK, sl], kn_ref[:, sl]]
        v_chunks = [vp_ref[:, sl], vc_ref[0:CHUNK, sl], vc_ref[CHUNK:2 * CHUNK, sl], vn_ref[:, sl],
                    vx_ref[0:CHUNK, sl], vx_ref[CHUNK:2 * CHUNK, sl]]
        v_t = [v.astype(F32).T.astype(BF16) for v in v_chunks]
        kcats.append([jnp.concatenate(k_chunks[b:b + 3] + [kx_ref[:, sl]], axis=0) for b in range(2)])
        vts.append([jnp.concatenate(v_t[b:b + 3] + v_t[4:], axis=1) for b in range(2)])

    outs = []
    for blk in range(2):
        rows = slice(blk * CHUNK, (blk + 1) * CHUNK)
        sts = []
        for kvp in range(2):
            for a in range(2):
                qs = jnp.concatenate(
                    [jnp.where(mask_q[a], q_ref[rows, (kvp * 4 + g) * LANES:(kvp * 4 + g + 1) * LANES],
                               jnp.zeros((CHUNK, LANES), BF16)) for g in range(4)], axis=0)
                sts.append(_dot_nt(kcats[kvp][blk], qs))
        _ffn_cols(h, wi_ref, act_scr, splits[blk], splits[blk + 1], d_ff)
        bias = biases[blk]
        for kv in range(4):
            st = sts[kv]
            st = jnp.concatenate([st[0:CHUNK] + bias[0:CHUNK], st[CHUNK:2 * CHUNK],
                                  st[2 * CHUNK:3 * CHUNK] + bias[CHUNK:2 * CHUNK], st[3 * CHUNK:]], axis=0)
            snk = jnp.where(grp == 0, sink_ref[kv * 4],
                            jnp.where(grp == 1, sink_ref[kv * 4 + 1],
                                      jnp.where(grp == 2, sink_ref[kv * 4 + 2], sink_ref[kv * 4 + 3])))
            m = jnp.maximum(jnp.max(st, axis=0, keepdims=True), snk)
            e = jnp.exp(st - m)
            denom = jnp.exp(snk - m) + jnp.sum(e, axis=0, keepdims=True)
            a = kv % 2
            outs.append(_dot(vts[kv // 2][blk][a * HEAD_DIM:(a + 1) * HEAD_DIM, :], e.astype(BF16))
                        * (1.0 / denom))
    _ffn_cols(h, wi_ref, act_scr, splits[2], splits[3], d_ff)

    for blk in range(2):
        for kvp in range(2):
            full = jnp.concatenate(outs[4 * blk + 2 * kvp:4 * blk + 2 * kvp + 2], axis=0)
            for g in range(4):
                m_scr[blk * CHUNK:(blk + 1) * CHUNK, (kvp * 4 + g) * LANES:(kvp * 4 + g + 1) * LANES] = (
                    full[:, g * CHUNK:(g + 1) * CHUNK].T.astype(BF16))

    o_ref[...] = x1 + p_ref[3:4, :] * _dot(act_scr[...], w2_ref[...])


def _attn_ffn(y, sink, xc, prm, wo, wi_all, w2_all, layer, nb, nc, ctx_chunks):
    tm = ROW_TILE
    cpt = tm // CHUNK
    tpb = nc // cpt
    lat_tiles = (nc - ctx_chunks) // cpt
    tiles = nb * lat_tiles
    ctx_tiles = ctx_chunks // cpt
    assert ctx_tiles == 1 and lat_tiles >= 2
    d = xc.shape[1]
    d_ff = w2_all.shape[1]
    bias = _window_bias()
    kcol, vcol = d // (2 * LANES), d // (2 * LANES) + 1
    att = lambda s: jnp.minimum(s, tiles - 1)
    ffn = lambda s: jnp.maximum(s - 1, 0)
    row_tile = lambda t: (t // lat_tiles) * tpb + ctx_tiles + t % lat_tiles
    chunk0 = lambda t: (t // lat_tiles) * nc + ctx_chunks
    prev_c = lambda s: chunk0(att(s)) + jnp.maximum((att(s) % lat_tiles) * cpt - 1, 0)
    next_c = lambda s: chunk0(att(s)) + jnp.minimum((att(s) % lat_tiles) * cpt + cpt, lat_tiles * cpt - 1)
    cur_spec = lambda col: pl.BlockSpec((tm, 2 * LANES), lambda s: (row_tile(att(s)), col))
    edge_spec = lambda f, col: pl.BlockSpec((CHUNK, 2 * LANES), lambda s: (f(s), col))
    ctx_spec = lambda col: pl.BlockSpec((tm, 2 * LANES), lambda s: ((att(s) // lat_tiles) * tpb, col))
    layer_spec = lambda a: pl.BlockSpec((None,) + a.shape[1:], lambda s: (layer, 0, 0),
                                        pipeline_mode=pl.Buffered(1))
    return pl.pallas_call(
        functools.partial(_attn_ffn_kernel, tiles=tiles, lat_tiles=lat_tiles, d_ff=d_ff),
        grid=(tiles + 1,),
        in_specs=[_smem_spec(),
                  pl.BlockSpec((tm, d), lambda s: (row_tile(att(s)), 0)),
                  edge_spec(prev_c, kcol), cur_spec(kcol), edge_spec(next_c, kcol), ctx_spec(kcol),
                  edge_spec(prev_c, vcol), cur_spec(vcol), edge_spec(next_c, vcol), ctx_spec(vcol),
                  _const_spec(bias.shape),
                  pl.BlockSpec((tm, d), lambda s: (row_tile(ffn(s)), 0)),
                  pl.BlockSpec((None, 8, d), lambda s: ((ffn(s) // lat_tiles) * 2 + 1, 0, 0)),
                  _const_spec(wo.shape), layer_spec(wi_all), layer_spec(w2_all)],
        out_specs=pl.BlockSpec((tm, d), lambda s: (ffn(s), 0)),
        out_shape=jax.ShapeDtypeStruct((tiles * tm, d), F32),
        scratch_shapes=[pltpu.VMEM((tm, d), BF16), pltpu.VMEM((tm, d_ff), BF16)],
        compiler_params=_cparams("arbitrary"),
        name="window_gqa_ffn",
    )(sink, y, y, y, y, y, y, y, y, y, bias, xc, prm, wo, wi_all, w2_all)


def _post_kernel(*refs, d_ff, split_tpb):
    tps = TILES_PER_STEP
    n_x = 2 if split_tpb else 1
    x_refs, refs = refs[:n_x * tps], refs[n_x * tps:]
    m_refs, p_refs = refs[:tps], refs[tps:2 * tps]
    wo_ref, wi_ref, w2_ref, o_ref, act_scr = refs[2 * tps:]
    tm = m_refs[0].shape[0]
    ys = [_dot(m_refs[u][...], wo_ref[...]) for u in range(tps)]
    x1s, hs = [], []
    for u in range(tps):
        if split_tpb:
            x = _ctx_or_latent_rows(x_refs[2 * u], x_refs[2 * u + 1], pl.program_id(0) * tps + u, split_tpb)
        else:
            x = x_refs[u][...]
        p_ref = p_refs[u]
        x1 = x + p_ref[0:1, :] * ys[u]
        ms = jnp.mean(x1 * x1, axis=-1, keepdims=True)
        hs.append(((x1 * lax.rsqrt(ms + EPS)) * p_ref[1:2, :] + p_ref[2:3, :]).astype(BF16))
        x1s.append(x1)
    for u in range(tps):
        gate = _dot(hs[u], wi_ref[:, :d_ff])
        up = _dot(hs[u], wi_ref[:, d_ff:])
        act_scr[u] = (gate * jax.nn.sigmoid(gate) * up).astype(BF16)
    for u in range(tps):
        o_ref[u * tm:(u + 1) * tm, :] = x1s[u] + p_refs[u][3:4, :] * _dot(act_scr[u], w2_ref[...])


def _post(xs, m2, prm, wo, wi_all, w2_all, layer, tiles_per_batch, skip_tiles, m_has_ctx):
    split = len(xs) == 2
    d = xs[-1].shape[1]
    r = sum(a.shape[0] for a in xs)
    tm = ROW_TILE
    tps = TILES_PER_STEP
    d_ff = w2_all.shape[1]
    tpb = tiles_per_batch
    kept = tpb - skip_tiles
    nb = r // (tm * tpb)
    assert not (split and skip_tiles) and (nb * kept) % tps == 0
    xrow = lambda t: (t // kept) * tpb + skip_tiles + t % kept
    mrow = xrow if m_has_ctx else (lambda t: t)
    sel = lambda t: ((t // kept) * 2 + jnp.minimum(skip_tiles + t % kept, 1), 0, 0)
    at = lambda f, u: (lambda i: f(i * tps + u))
    row_spec = lambda f, u: pl.BlockSpec((tm, d), lambda i: (f(i * tps + u), 0))
    x_specs = []
    for u in range(tps):
        x_specs += _split_row_specs(tm, d, tpb, at(lambda t: t, u)) if split else [row_spec(xrow, u)]
    layer_spec = lambda a: pl.BlockSpec((None,) + a.shape[1:], lambda i: (layer, 0, 0),
                                        pipeline_mode=pl.Buffered(1))
    return pl.pallas_call(
        functools.partial(_post_kernel, d_ff=d_ff, split_tpb=tpb if split else 0),
        grid=(nb * kept // tps,),
        in_specs=x_specs + [row_spec(mrow, u) for u in range(tps)]
                 + [pl.BlockSpec((None, 8, d), at(sel, u)) for u in range(tps)]
                 + [_const_spec(wo.shape), layer_spec(wi_all), layer_spec(w2_all)],
        out_specs=pl.BlockSpec((tps * tm, d), lambda i: (i, 0)),
        out_shape=jax.ShapeDtypeStruct((nb * kept * tm, d), F32),
        scratch_shapes=[pltpu.VMEM((tps, tm, d_ff), BF16)],
        compiler_params=_cparams("arbitrary"),
        name="outproj_swiglu",
    )(*(list(xs) * tps), *([m2] * tps), *([prm] * tps), wo, wi_all, w2_all)


def _pair_cols(w):
    rows, cols = w.shape
    return w.reshape(rows, cols // LANES, 2, 2, 32).transpose(0, 1, 3, 2, 4).reshape(rows, cols)


def _attn_q_cols(w):
    rows = w.shape[0]
    g_per = w.shape[1] // (H_KV * HEAD_DIM)
    return (w.reshape(rows, H_KV // 2, 2, g_per, 2, 32).transpose(0, 1, 3, 4, 2, 5)
            .reshape(rows, w.shape[1]))


def _attn_o_rows(w):
    cols = w.shape[1]
    g_per = w.shape[0] // (H_KV * HEAD_DIM)
    return (w.reshape(H_KV // 2, 2, g_per, HEAD_DIM, cols).transpose(0, 2, 1, 3, 4)
            .reshape(w.shape[0], cols))


def _rope_tables(seq, ctx_len):
    rows = seq // GRID_W
    row = np.repeat(np.arange(rows, dtype=np.float32), GRID_W)
    col = np.tile(np.arange(GRID_W, dtype=np.float32), rows)
    n = HEAD_DIM // 4
    inv = (np.float32(ROPE_BASE) ** (-np.arange(n, dtype=np.float32) / np.float32(n))).astype(np.float32)
    ang = np.concatenate([row[:, None] * inv, col[:, None] * inv], axis=-1).astype(np.float32)
    cos, sin = np.cos(ang), np.sin(ang)
    cos_t = np.concatenate([np.ones((ctx_len, LANES), np.float32), np.tile(cos, (1, 4))], axis=0)
    sin_t = np.concatenate([np.zeros((ctx_len, LANES), np.float32),
                            np.concatenate([-sin, -sin, sin, sin], axis=-1)], axis=0)
    return jnp.asarray(cos_t, F32), jnp.asarray(sin_t, F32)


def _mod_tables(mod, nb, norm_w):
    d = norm_w.shape[-1]
    lat = mod[:nb].reshape(nb, 6, d)
    ctx = jnp.broadcast_to(mod[nb].reshape(1, 6, d), (nb, 6, d))
    both = jnp.stack([ctx, lat], axis=1).reshape(nb * 2, 6, d)
    sh1, sc1, g1, sh2, sc2, g2 = [both[:, k] for k in range(6)]
    ab1 = jnp.stack([norm_w[0] * (1.0 + sc1), sh1], axis=1)
    zeros = jnp.zeros_like(g1)
    prm = jnp.stack([g1, norm_w[1] * (1.0 + sc2), sh2, g2, zeros, zeros, zeros, zeros], axis=1)
    return ab1, prm


def kernel(x, c, ctx, c_ctx, ada_w, ada_b, norm_w, ffn_w_in, ffn_w_out, ab_w_in, ab_w_out,
           ret_log_gamma, ret_norm_w, mlstm_conv_w, mlstm_conv_b, mlstm_gate_b, mlstm_norm_w,
           attn_w_in, attn_w_out, attn_q_norm_w, attn_k_norm_w, attn_sink):
    nb, seq, d = x.shape
    ctx_len = ctx.shape[1]
    depth = ada_w.shape[0]
    assert ctx_len == ROW_TILE and seq % ROW_TILE == 0 and d == 8 * LANES and nb < 8
    t_all = ctx_len + seq
    nc = t_all // CHUNK
    ctx_chunks = ctx_len // CHUNK
    tpb = t_all // ROW_TILE
    dr = d // 2

    rows = jnp.zeros((8, d), F32).at[:nb].set(c).at[nb].set(c_ctx)
    mod_all = _modulation(rows, ada_w, ada_b)
    cos_t, sin_t = _rope_tables(seq, ctx_len)
    wi_all = ffn_w_in.astype(BF16)
    w2_all = ffn_w_out.astype(BF16)
    xs = (ctx.reshape(nb * ctx_len, d), x.reshape(nb * seq, d))

    out = None
    for layer in range(depth):
        last = layer == depth - 1
        ab1, prm = _mod_tables(mod_all[layer], nb, norm_w[layer])
        if layer % 2 == 0:
            assert layer == 0
            e = layer // 2
            w = ab_w_in[e]
            w_main = jnp.concatenate([_pair_cols(w[:, :dr]), _pair_cols(w[:, dr:2 * dr]), w[:, 2 * dr:8 * dr]],
                                     axis=1).astype(BF16)
            wg = jnp.zeros((d, LANES), F32).at[:, :32].set(w[:, 8 * dr:]).astype(BF16)
            wgt = w[:, 8 * dr:].T.astype(BF16)
            gb = jnp.zeros((1, LANES), F32).at[0, :32].set(mlstm_gate_b[e].reshape(-1))
            gbt = mlstm_gate_b[e].reshape(32, 1)
            y, g, gt = _inproj_ab(*xs, ab1, w_main, wg, wgt, gb, gbt, cos_t, sin_t, tpb)

            lg = ret_log_gamma[e].astype(F32)
            lgk = jnp.tile(jnp.repeat(lg.reshape(2, 4, 2), 32, axis=-1), (1, 1, 2)).reshape(8, LANES)
            cw =jnp.concatenate([mlstm_conv_w[e], mlstm_conv_b[e][None],
                                  jnp.zeros((4, 2 * dr), F32)], axis=0)
            nw = jnp.broadcast_to(jnp.concatenate([ret_norm_w[e], mlstm_norm_w[e]]).reshape(8, LANES, 1),
                                  (8, LANES, LANES))
            lg_s = lg.reshape(-1)
            states = _bwd_states(y, g, lgk, cw[:, dr:], nb, nc, ctx_chunks)
            wo = ab_w_out[e].astype(BF16)
            out = _mixer_ffn(y, g, gt, states, lg_s, lgk, lgk.T, cw, nw, xs, prm, wo, wi_all, w2_all, layer,
                             nb, tpb)
        else:
            o = layer // 2
            w = attn_w_in[o]
            w_main = jnp.concatenate([_attn_q_cols(w[:, :d]), _pair_cols(w[:, d:d + 2 * LANES]),
                                      w[:, d + 2 * LANES:]], axis=1).astype(BF16)
            lane_w = lambda v: jnp.concatenate([v[:32], v[:32], v[32:], v[32:]])
            nwq = jnp.stack([lane_w(attn_q_norm_w[o]) * (HEAD_DIM ** -0.5), lane_w(attn_k_norm_w[o])]
                            + [jnp.zeros((LANES,), F32)] * 6)
            assert last
            y = _inproj_attn(xs[0], ab1, w_main, nwq, cos_t, sin_t, tpb)
            wo = _attn_o_rows(attn_w_out[o]).astype(BF16)
            out = _attn_ffn(y, attn_sink[o].astype(F32), xs[0], prm, wo, wi_all, w2_all, layer,
                            nb, nc, ctx_chunks)
        xs = (out,)
    return out.reshape(nb, seq, d)
```

```python
import functools

import numpy as np
import jax
import jax.numpy as jnp
from jax import lax
from jax.experimental import pallas as pl
from jax.experimental.pallas import tpu as pltpu

F32 = jnp.float32
BF16 = jnp.bfloat16

HEAD_DIM = 64
CHUNK = 128
GRID_W = 64
ROPE_BASE = 10000.0
EPS = 1e-6
H_KV = 4
LANES = 128
ROW_TILE = 256
TILES_PER_STEP = 2
HALO = 16
NEG = -1e30
VMEM_LIMIT = 56 * 1024 * 1024


def _cparams(*sem):
    return pltpu.CompilerParams(dimension_semantics=sem, vmem_limit_bytes=VMEM_LIMIT)


def _const_spec(shape):
    nd = len(shape)
    return pl.BlockSpec(shape, lambda *_: (0,) * nd, pipeline_mode=pl.Buffered(1))


def _smem_spec():
    return pl.BlockSpec(memory_space=pltpu.SMEM)


def _lane(shape=(CHUNK, LANES)):
    return lax.broadcasted_iota(jnp.int32, shape, len(shape) - 1)


def _dot(a, b):
    return jnp.dot(a, b, preferred_element_type=F32)


def _dot_nt(a, b):
    return lax.dot_general(a, b, (((1,), (1,)), ((), ())), preferred_element_type=F32)


def _dot_tn(a, b):
    return lax.dot_general(a, b, (((0,), (0,)), ((), ())), preferred_element_type=F32)


def _split3(x):
    hi = x.astype(BF16)
    r = x - hi.astype(F32)
    mid = r.astype(BF16)
    lo = (r - mid.astype(F32)).astype(BF16)
    return hi, mid, lo


def _log_sigmoid(x):
    return jnp.minimum(x, 0.0) - jnp.log1p(jnp.exp(-jnp.abs(x)))


def _rope(x, cos, sin_signed):
    return x * cos + pltpu.roll(x, LANES // 2, 1) * sin_signed


def _mod_kernel(rows_ref, w_ref, b_ref, o_ref):
    a = rows_ref[...]
    a = a * jax.nn.sigmoid(a)
    a_hi = a.astype(BF16)
    a_lo = (a - a_hi.astype(F32)).astype(BF16)
    w = w_ref[...]
    w_hi = w.astype(BF16)
    w_lo = (w - w_hi.astype(F32)).astype(BF16)
    o_ref[...] = _dot(a_hi, w_hi) + _dot(a_hi, w_lo) + _dot(a_lo, w_hi) + b_ref[...]


def _modulation(rows, ada_w, ada_b):
    depth, d, n = ada_w.shape
    tn = n // 4
    return pl.pallas_call(
        _mod_kernel,
        grid=(depth, n // tn),
        in_specs=[pl.BlockSpec((8, d), lambda l, j: (0, 0)),
                  pl.BlockSpec((None, d, tn), lambda l, j: (l, 0, j)),
                  pl.BlockSpec((None, 1, tn), lambda l, j: (l, 0, j))],
        out_specs=pl.BlockSpec((None, 8, tn), lambda l, j: (l, 0, j)),
        out_shape=jax.ShapeDtypeStruct((depth, 8, n), F32),
        compiler_params=_cparams("arbitrary", "arbitrary"),
        name="adaln_modulation",
    )(rows, ada_w, ada_b.reshape(depth, 1, n))


def _norm_mod(x, ab_ref):
    ms = jnp.mean(x * x, axis=-1, keepdims=True)
    h = (x * lax.rsqrt(ms + EPS)) * ab_ref[0:1, :] + ab_ref[1:2, :]
    return h.astype(BF16)


def _ctx_or_latent_rows(ctx_ref, x_ref, tile, tiles_per_batch):
    return jnp.where(tile % tiles_per_batch == 0, ctx_ref[...], x_ref[...])


def _split_row_specs(tm, d, tpb, tile_of):
    lat = tpb - 1
    return [pl.BlockSpec((tm, d), lambda i: (tile_of(i) // tpb, 0)),
            pl.BlockSpec((tm, d), lambda i: ((tile_of(i) // tpb) * lat + jnp.maximum(tile_of(i) % tpb - 1, 0), 0))]


def _inproj_ab_kernel(*refs, tpb):
    tps = TILES_PER_STEP
    x_refs, ab_refs, refs = refs[:2 * tps], refs[2 * tps:3 * tps], refs[3 * tps:]
    w_ref, wg_ref, wgt_ref, gb_ref, gbt_ref = refs[:5]
    rope_refs, (y_ref, g_ref, gt_ref) = refs[5:5 + 2 * tps], refs[5 + 2 * tps:]
    tm = y_ref.shape[0] // tps
    hbs = [_norm_mod(_ctx_or_latent_rows(x_refs[2 * u], x_refs[2 * u + 1], pl.program_id(0) * tps + u, tpb),
                     ab_refs[u]) for u in range(tps)]
    gw = 4 * LANES
    for j in range(8):
        for u in range(tps):
            acc = _dot(hbs[u], w_ref[:, j * gw:(j + 1) * gw])
            if j in (0, 1):
                if j == 0:
                    acc = acc * (HEAD_DIM ** -0.5)
                cos, sin = rope_refs[2 * u][...], rope_refs[2 * u + 1][...]
                acc = jnp.concatenate(
                    [_rope(acc[:, p * LANES:(p + 1) * LANES], cos, sin) for p in range(4)], axis=1)
            elif j == 3:
                acc = acc * jax.nn.sigmoid(acc)
            elif j == 7:
                acc = jax.nn.sigmoid(acc)
            y_ref[u * tm:(u + 1) * tm, j * gw:(j + 1) * gw] = acc.astype(BF16)
    for u in range(tps):
        g_ref[u * tm:(u + 1) * tm, :] = _dot(hbs[u], wg_ref[...]) + gb_ref[...]
        gt_ref[:, u * tm:(u + 1) * tm] = _dot_nt(wgt_ref[...], hbs[u]) + gbt_ref[...]


def _tile_specs(tm, tpb, tps):
    at = lambda f, u: (lambda i: f(i * tps + u))
    sel = lambda t: ((t // tpb) * 2 + jnp.minimum(t % tpb, 1), 0, 0)
    mod_spec = lambda d, u: pl.BlockSpec((None, 2, d), at(sel, u))
    rope_spec = lambda u: pl.BlockSpec((tm, LANES), at(lambda t: (t % tpb, 0), u))
    return at, mod_spec, rope_spec


def _inproj_ab(ctx2, x2, ab, w, wg, wgt, gb, gbt, cos, sin, tiles_per_batch):
    d = x2.shape[1]
    r = ctx2.shape[0] + x2.shape[0]
    tm, tps = ROW_TILE, TILES_PER_STEP
    n = w.shape[1]
    tpb = tiles_per_batch
    assert (r // tm) % tps == 0
    at, mod_spec, rope_spec = _tile_specs(tm, tpb, tps)
    x_specs, rope_specs = [], []
    for u in range(tps):
        x_specs += _split_row_specs(tm, d, tpb, at(lambda t: t, u))
        rope_specs += [rope_spec(u), rope_spec(u)]
    return pl.pallas_call(
        functools.partial(_inproj_ab_kernel, tpb=tpb),
        grid=(r // (tm * tps),),
        in_specs=x_specs + [mod_spec(d, u) for u in range(tps)] + [
                  _const_spec(w.shape), _const_spec(wg.shape), _const_spec(wgt.shape),
                  _const_spec(gb.shape), _const_spec(gbt.shape)] + rope_specs,
        out_specs=[pl.BlockSpec((tps * tm, n), lambda i: (i, 0)),
                   pl.BlockSpec((tps * tm, LANES), lambda i: (i, 0)),
                   pl.BlockSpec((32, tps * tm), lambda i: (0, i))],
        out_shape=[jax.ShapeDtypeStruct((r, n), BF16),
                   jax.ShapeDtypeStruct((r, LANES), F32),
                   jax.ShapeDtypeStruct((32, r), F32)],
        compiler_params=_cparams("arbitrary"),
        name="inproj_ret_mlstm",
    )(*([ctx2, x2] * tps), *([ab] * tps), w, wg, wgt, gb, gbt, *([cos, sin] * tps))


def _inproj_attn_kernel(*refs):
    tps = TILES_PER_STEP
    x_ref, ab_refs, refs = refs[0], refs[1:1 + tps], refs[1 + tps:]
    w_ref, nw_ref = refs[:2]
    rope_refs, y_ref = refs[2:2 + 2 * tps], refs[2 + 2 * tps]
    tm = y_ref.shape[0] // tps
    hbs = [_norm_mod(x_ref[u * tm:(u + 1) * tm, :], ab_refs[u]) for u in range(tps)]
    r2 = lax.broadcasted_iota(jnp.int32, (2 * LANES, 2 * LANES), 0)
    c2 = lax.broadcasted_iota(jnp.int32, (2 * LANES, 2 * LANES), 1)
    same_head = (((r2 ^ c2) & (LANES | 32)) == 0).astype(BF16)
    acc_q = [_dot(hbs[u], w_ref[:, 0:8 * LANES]) for u in range(tps)]
    acc_k = [_dot(hbs[u], w_ref[:, 8 * LANES:10 * LANES]) for u in range(tps)]
    acc_v = [_dot(hbs[u], w_ref[:, 10 * LANES:12 * LANES]) for u in range(tps)]
    for j in range(5):
        for u in range(tps):
            acc = acc_q[u][:, j * 2 * LANES:(j + 1) * 2 * LANES] if j < 4 else acc_k[u]
            sq = acc * acc
            hi = sq.astype(BF16)
            lo = (sq - hi.astype(F32)).astype(BF16)
            ms = (_dot(hi, same_head) + _dot(lo, same_head)) * (1.0 / HEAD_DIM)
            nrm = acc * lax.rsqrt(ms + EPS)
            nw = nw_ref[0:1, :] if j < 4 else nw_ref[1:2, :]
            cos, sin = rope_refs[2 * u][...], rope_refs[2 * u + 1][...]
            for v in range(2):
                ls = slice(v * LANES, (v + 1) * LANES)
                y_ref[u * tm:(u + 1) * tm, (2 * j + v) * LANES:(2 * j + v + 1) * LANES] = (
                    _rope(nrm[:, ls] * nw, cos, sin).astype(BF16))
    for u in range(tps):
        y_ref[u * tm:(u + 1) * tm, 10 * LANES:12 * LANES] = acc_v[u].astype(BF16)


def _inproj_attn(x2, ab, w, nw, cos, sin, tiles_per_batch):
    r, d = x2.shape
    tm, tps = ROW_TILE, TILES_PER_STEP
    n = w.shape[1]
    tpb = tiles_per_batch
    assert (r // tm) % tps == 0
    _, mod_spec, rope_spec = _tile_specs(tm, tpb, tps)
    rope_specs = []
    for u in range(tps):
        rope_specs += [rope_spec(u), rope_spec(u)]
    return pl.pallas_call(
        _inproj_attn_kernel,
        grid=(r // (tm * tps),),
        in_specs=[pl.BlockSpec((tps * tm, d), lambda i: (i, 0))] + [mod_spec(d, u) for u in range(tps)]
                 + [_const_spec(w.shape), _const_spec(nw.shape)] + rope_specs,
        out_specs=pl.BlockSpec((tps * tm, n), lambda i: (i, 0)),
        out_shape=jax.ShapeDtypeStruct((r, n), BF16),
        compiler_params=_cparams("arbitrary"),
        name="inproj_attn",
    )(x2, *([ab] * tps), w, nw, *([cos, sin] * tps))


def _conv_silu(cur_ref, prev_ref, next_ref, cw_ref, prev_on, next_on):
    cur = cur_ref[...].astype(F32)
    n = cur.shape[0]
    row = lax.broadcasted_iota(jnp.int32, cur.shape, 0)
    prev_row = prev_ref[HALO - 1:HALO, :].astype(F32) * prev_on
    next_row = next_ref[0:1, :].astype(F32) * next_on
    xm = jnp.where(row == 0, prev_row, pltpu.roll(cur, 1, 0))
    xp = jnp.where(row == n - 1, next_row, pltpu.roll(cur, n - 1, 0))
    y = cw_ref[3:4, :] + cw_ref[0:1, :] * xm + cw_ref[1:2, :] * cur + cw_ref[2:3, :] * xp
    return y * jax.nn.sigmoid(y)


def _segment_flags(c, nc, ctx_chunks):
    prev_on = jnp.where((c == 0) | (c == ctx_chunks), 0.0, 1.0).astype(F32)
    next_on = jnp.where((c == ctx_chunks - 1) | (c == nc - 1), 0.0, 1.0).astype(F32)
    return prev_on, next_on


def _cumsum_cols(tri_bf, lf):
    hi, mid, lo = _split3(lf)
    return _dot(tri_bf, hi) + _dot(tri_bf, mid) + _dot(tri_bf, lo)


def _cumsum_rows(lf, tri_bf):
    hi, mid, lo = _split3(lf)
    return _dot(hi, tri_bf) + _dot(mid, tri_bf) + _dot(lo, tri_bf)


def _ret_state_update(s_ref, p, k2, v2, kdec, cd_lanes, bd):
    kf = (k2.astype(F32) * kdec).astype(BF16)
    s_ref[p] = s_ref[p] * cd_lanes + jnp.where(bd, _dot_tn(v2, kf), 0.0)


def _mlstm_state_update(c_ref, n_ref, m_ref, k_pairs, v_pairs, c_all, bend, col0, lo, bd):
    cmax = jnp.max(c_all, axis=0, keepdims=True)
    w_all = jnp.exp(c_all - cmax)
    m_old = m_ref[0:1, :]
    mrel = jnp.maximum(m_old, cmax)
    a_row = jnp.exp(m_old - mrel)
    bb_row = jnp.exp(cmax - mrel)
    m_ref[0:1, :] = bend + mrel
    lo_row = lo[0:1, :]
    for p in range(4):
        h0 = col0 + 2 * p
        kw = k_pairs[p] * jnp.where(lo, w_all[:, h0:h0 + 1], w_all[:, h0 + 1:h0 + 2])
        kvt = _dot_tn(v_pairs[p], kw.astype(BF16))
        nloc = jnp.sum(kw, axis=0, keepdims=True)
        a_l = jnp.where(lo_row, a_row[:, h0:h0 + 1], a_row[:, h0 + 1:h0 + 2])
        bb_l = jnp.where(lo_row, bb_row[:, h0:h0 + 1], bb_row[:, h0 + 1:h0 + 2])
        c_ref[p] = c_ref[p] * a_l + jnp.where(bd, kvt, 0.0) * bb_l
        n_new = (n_ref[p, 0:1, :] + n_ref[p, 1:2, :]) * a_l + nloc * bb_l
        n_ref[p, 0:1, :] = jnp.where(lo_row, n_new, 0.0)
        n_ref[p, 1:2, :] = jnp.where(lo_row, 0.0, n_new)


def _bwd_state_kernel(rk_ref, rv_ref, mqk_ref, mqkp_ref, mqkn_ref, mv_ref, g_ref, lgk_ref, cw_ref,
                      sret_ref, cm_ref, nm_ref, mm_ref, qk_ref,
                      s_scr, c_scr, n_scr, m_scr, kdec_scr, *, tpb):
    i = pl.program_id(1)
    jt = jnp.where(i == 0, 0, tpb - i)
    lane = _lane()
    sub = lax.broadcasted_iota(jnp.int32, (CHUNK, LANES), 0)
    lo = lane < HEAD_DIM
    lo_row = lo[0:1, :]
    bd_ret = (sub >= HEAD_DIM) == ((lane & 32) != 0)
    bd_m = (sub >= HEAD_DIM) == (lane >= HEAD_DIM)

    @pl.when(i == 0)
    def _():
        s_scr[...] = jnp.zeros_like(s_scr)
        c_scr[...] = jnp.zeros_like(c_scr)
        n_scr[...] = jnp.zeros_like(n_scr)
        m_scr[...] = jnp.zeros_like(m_scr)
        pos = sub.astype(F32)
        for p in range(4):
            kdec_scr[p] = jnp.exp(lgk_ref[4 + p:5 + p, :] * pos)

    prev_on = jnp.where(jt <= 1, 0.0, 1.0).astype(F32)
    next_on = jnp.where((jt == 0) | (jt == tpb - 1), 0.0, 1.0).astype(F32)
    qk = _conv_silu(mqk_ref, mqkp_ref, mqkn_ref, cw_ref, prev_on, next_on)
    qk_ref[:, 0:4 * LANES] = (qk[:, 0:4 * LANES] * (HEAD_DIM ** -0.5)).astype(BF16)
    qk_ref[:, 4 * LANES:] = qk[:, 4 * LANES:].astype(BF16)
    le_bf = (sub <= lane).astype(BF16)

    order = (1, 0)
    pre = {}
    for blk in order:
        rows = slice(blk * CHUNK, (blk + 1) * CHUNK)
        g = g_ref[rows, :]
        bal = pltpu.roll(_cumsum_cols(le_bf, _log_sigmoid(g)), LANES - 8, 1)
        c_all = g - bal
        cmax = jnp.max(c_all, axis=0, keepdims=True)
        w_all = jnp.exp(c_all - cmax)
        vts, ks, nlocs = [], [], []
        for p in range(8):
            sl = slice((p % 4) * LANES, (p % 4 + 1) * LANES)
            if p < 4:
                v2 = rv_ref[rows, sl]
                ks.append((rk_ref[rows, sl].astype(F32) * kdec_scr[p]).astype(BF16))
            else:
                v2 = mv_ref[rows, sl]
                h0 = 16 + 2 * (p - 4)
                kw = qk[rows, 4 * LANES + (p - 4) * LANES:4 * LANES + (p - 3) * LANES] * jnp.where(
                    lo, w_all[:, h0:h0 + 1], w_all[:, h0 + 1:h0 + 2])
                ks.append(kw.astype(BF16))
                nlocs.append(jnp.sum(kw, axis=0, keepdims=True))
            vts.append(v2.astype(F32).T.astype(BF16))
        pre[blk] = (vts, ks, nlocs, cmax, bal[0:1, :])
    kvs = {blk: [_dot(pre[blk][0][p], pre[blk][1][p]) for p in range(8)] for blk in order}

    for blk in order:
        _, _, nlocs, cmax, bend = pre[blk]
        sret_ref[blk] = s_scr[...].astype(BF16)
        cm_ref[blk] = c_scr[...].astype(BF16)
        nm_ref[blk] = n_scr[...].astype(BF16)
        mm_ref[blk] = m_scr[...]
        m_old = m_scr[0:1, :]
        mrel = jnp.maximum(m_old, cmax)
        a_row = jnp.exp(m_old - mrel)
        bb_row = jnp.exp(cmax - mrel)
        m_scr[0:1, :] = bend + mrel
        for p in range(4):
            cd = jnp.exp(lgk_ref[4 + p:5 + p, :] * float(CHUNK))
            s_scr[p] = s_scr[p] * cd + jnp.where(bd_ret, kvs[blk][p], 0.0)
            h0 = 16 + 2 * p
            a_l = jnp.where(lo_row, a_row[:, h0:h0 + 1], a_row[:, h0 + 1:h0 + 2])
            bb_l = jnp.where(lo_row, bb_row[:, h0:h0 + 1], bb_row[:, h0 + 1:h0 + 2])
            c_scr[p] = c_scr[p] * a_l + jnp.where(bd_m, kvs[blk][4 + p], 0.0) * bb_l
            n_new = (n_scr[p, 0:1, :] + n_scr[p, 1:2, :]) * a_l + nlocs[p] * bb_l
            n_scr[p, 0:1, :] = jnp.where(lo_row, n_new, 0.0)
            n_scr[p, 1:2, :] = jnp.where(lo_row, 0.0, n_new)


def _bwd_states(y, g, lgk, cw, nb, tpb):
    gw = 4 * LANES
    tm = ROW_TILE
    cpt = tm // CHUNK
    tile = lambda b, i: b * tpb + jnp.where(i == 0, 0, tpb - i)
    hb = tm // HALO
    nhalo = y.shape[0] // HALO
    blk = lambda j: pl.BlockSpec((tm, gw), lambda b, i: (tile(b, i), j))
    state = lambda *dims: pl.BlockSpec((cpt,) + dims, lambda b, i: (tile(b, i),) + (0,) * len(dims))
    nchunks = nb * tpb * cpt
    return pl.pallas_call(
        functools.partial(_bwd_state_kernel, tpb=tpb),
        grid=(nb, tpb),
        in_specs=[blk(1), blk(2),
                  pl.BlockSpec((tm, 2 * gw), lambda b, i: (tile(b, i), 2)),
                  pl.BlockSpec((HALO, 2 * gw), lambda b, i: (jnp.maximum(tile(b, i) * hb - 1, 0), 2)),
                  pl.BlockSpec((HALO, 2 * gw), lambda b, i: (jnp.minimum((tile(b, i) + 1) * hb, nhalo - 1), 2)),
                  blk(6),
                  pl.BlockSpec((tm, LANES), lambda b, i: (tile(b, i), 0)),
                  _const_spec(lgk.shape), _const_spec(cw.shape)],
        out_specs=[state(4, CHUNK, LANES), state(4, CHUNK, LANES), state(4, HALO, LANES), state(8, LANES),
                   pl.BlockSpec((tm, 2 * gw), lambda b, i: (tile(b, i), 0))],
        out_shape=[jax.ShapeDtypeStruct((nchunks, 4, CHUNK, LANES), BF16),
                   jax.ShapeDtypeStruct((nchunks, 4, CHUNK, LANES), BF16),
                   jax.ShapeDtypeStruct((nchunks, 4, HALO, LANES), BF16),
                   jax.ShapeDtypeStruct((nchunks, 8, LANES), F32),
                   jax.ShapeDtypeStruct((y.shape[0], 2 * gw), BF16)],
        scratch_shapes=[pltpu.VMEM((4, CHUNK, LANES), F32), pltpu.VMEM((4, CHUNK, LANES), F32),
                        pltpu.VMEM((4, HALO, LANES), F32), pltpu.VMEM((8, LANES), F32),
                        pltpu.VMEM((4, CHUNK, LANES), F32)],
        compiler_params=_cparams("arbitrary", "arbitrary"),
        name="bwd_state_sweep",
    )(y, y, y, y, y, y, g, lgk, cw)


def _mlstm_dir_weights(st, qn_row, c_col, bt_row, m_prev, tri):
    dl = jnp.where(tri, c_col + bt_row, NEG)
    mx = jnp.max(dl, axis=0, keepdims=True)
    al = bt_row + m_prev
    m_t = jnp.maximum(al, mx)
    w = jnp.exp(dl - m_t)
    a_t = jnp.exp(al - m_t)
    sw = st * w
    den = jnp.sum(sw, axis=0, keepdims=True) + a_t * qn_row
    r = 1.0 / jnp.maximum(jnp.abs(den), jnp.exp(-m_t))
    return sw * r, a_t * r


def _heads_out(ht, nw_tab):
    rows = []
    for a in range(2):
        ha = ht[a * HEAD_DIM:(a + 1) * HEAD_DIM, :]
        ms = jnp.mean(ha * ha, axis=0, keepdims=True)
        rows.append(ha * lax.rsqrt(ms + EPS))
    return (jnp.concatenate(rows, axis=0) * nw_tab).T


def _mixer_kernel(lg_ref, rq_ref, rk_ref, rv_ref, rg_ref, mqk_ref, mqkp_ref, mqkn_ref, mv_ref, mo_ref,
                  g_ref, gt_ref, sretb_ref, cmb_ref, nmb_ref, mmb_ref,
                  lgk_ref, lgkt_ref, cw_ref, nw_ref,
                  out_ref,
                  s_scr, c_scr, n_scr, m_scr, dm_scr, dec_scr, *, nc, ctx_chunks):
    c = pl.program_id(1)
    lane = _lane()
    sub = lax.broadcasted_iota(jnp.int32, (CHUNK, LANES), 0)
    lo = lane < HEAD_DIM
    sub_lo = sub < HEAD_DIM
    mask_ret = [(lane & 32) == 0, (lane & 32) != 0]
    mask_nat = [lo, lane >= HEAD_DIM]
    bd_ret = (sub >= HEAD_DIM) == ((lane & 32) != 0)
    bd_m = (sub >= HEAD_DIM) == (lane >= HEAD_DIM)
    le = sub <= lane
    ge = sub >= lane

    @pl.when(c == 0)
    def _():
        s_scr[...] = jnp.zeros_like(s_scr)
        c_scr[...] = jnp.zeros_like(c_scr)
        n_scr[...] = jnp.zeros_like(n_scr)
        m_scr[...] = jnp.zeros_like(m_scr)
        spos = sub.astype(F32)
        tpos = lane.astype(F32)
        diff = (lane - sub).astype(F32)
        for h in range(8):
            dm_scr[h] = (jnp.where(le, jnp.exp(lg_ref[h] * diff), 0.0)
                         + jnp.where(ge, jnp.exp(lg_ref[8 + h] * (-diff)), 0.0))
        for p in range(4):
            dec_scr[0, p] = jnp.exp(lgkt_ref[:, p:p + 1] * (tpos + 1.0))
            dec_scr[1, p] = jnp.exp(lgkt_ref[:, 4 + p:5 + p] * (float(CHUNK) - tpos))
            dec_scr[2, p] = jnp.exp(lgk_ref[p:p + 1, :] * (float(CHUNK) - 1.0 - spos))


    prev_on, next_on = _segment_flags(c, nc, ctx_chunks)
    qk = _conv_silu(mqk_ref, mqkp_ref, mqkn_ref, cw_ref, prev_on, next_on)
    g = g_ref[...]
    gt = gt_ref[...]
    lf_col = _log_sigmoid(g)
    lf_row = _log_sigmoid(gt)
    le_bf = le.astype(BF16)
    ge_bf = ge.astype(BF16)
    bal_f = pltpu.roll(_cumsum_cols(ge_bf, lf_col), LANES - 8, 1)
    bal_b = pltpu.roll(_cumsum_cols(le_bf, lf_col), LANES - 8, 1)
    cf_all = g - bal_f
    cb_all = g - bal_b
    bf_row = _cumsum_rows(lf_row, le_bf)
    bb_row = _cumsum_rows(lf_row, ge_bf)

    qb, kb, vb, kf32, qt, vbd = [], [], [], [], [], []
    for p in range(8):
        sl = slice((p % 4) * LANES, (p % 4 + 1) * LANES)
        if p < 4:
            q2, k2, v2 = rq_ref[:, sl], rk_ref[:, sl], rv_ref[:, sl]
            qf = q2.astype(F32)
            kf = None
            old =jnp.concatenate([s_scr[p].astype(BF16), sretb_ref[p]], axis=1)
        else:
            qf = qk[:, sl] * (HEAD_DIM ** -0.5)
            kf = qk[:, 4 * LANES + (p - 4) * LANES:4 * LANES + (p - 3) * LANES]
            q2, k2, v2 = qf.astype(BF16), kf.astype(BF16), mv_ref[:, sl]
            old = jnp.concatenate([c_scr[p - 4].astype(BF16), cmb_ref[p - 4]], axis=1)
        vt = v2.astype(F32).T.astype(BF16)
        qb.append(q2)
        kb.append(k2)
        vb.append(v2)
        kf32.append(kf)
        qt.append(qf.T)
        vbd.append(jnp.concatenate([jnp.where(sub_lo, vt, jnp.zeros_like(vt)),
                                    jnp.where(sub_lo, jnp.zeros_like(vt), vt), old], axis=1))

    st2, qn_f, qn_b = [], [], []
    for p in range(8):
        masks = mask_ret if p < 4 else mask_nat
        zero = jnp.zeros_like(qb[p])
        qstack = jnp.concatenate([jnp.where(masks[0], qb[p], zero), jnp.where(masks[1], qb[p], zero)], axis=0)
        st2.append(_dot_nt(kb[p], qstack))
        if p >= 4:
            qn_f.append(_dot_nt(n_scr[p - 4].astype(BF16), qb[p]))
            qn_b.append(_dot_nt(nmb_ref[p - 4], qb[p]))

    rhs = []
    for p in range(8):
        if p < 4:
            pts = [(st2[p][:, a * LANES:(a + 1) * LANES] * dm_scr[2 * p + a]).astype(BF16) for a in range(2)]
            x_f, x_b = dec_scr[0, p], dec_scr[1, p]
        else:
            pts, cf, cb = [], [], []
            for a in range(2):
                h = 2 * (p - 4) + a
                st = st2[p][:, a * LANES:(a + 1) * LANES]
                pf, coef_f = _mlstm_dir_weights(st, qn_f[p - 4][a:a + 1, :], cf_all[:, h:h + 1],
                                                bf_row[8 + h:9 + h, :], m_scr[0:1, h:h + 1], le)
                pb, coef_b = _mlstm_dir_weights(st, qn_b[p - 4][a:a + 1, :], cb_all[:, 16 + h:17 + h],
                                                bb_row[24 + h:25 + h, :], mmb_ref[0:1, 16 + h:17 + h], ge)
                pts.append((pf + pb).astype(BF16))
                cf.append(coef_f)
                cb.append(coef_b)
            x_f = jnp.where(sub_lo, cf[0], cf[1])
            x_b = jnp.where(sub_lo, cb[0], cb[1])
        rhs.append(jnp.concatenate(pts + [(qt[p] * x_f).astype(BF16), (qt[p] * x_b).astype(BF16)], axis=0))

    ht = [_dot(vbd[p], rhs[p]) for p in range(8)]

    for p in range(8):
        sl = slice((p % 4) * LANES, (p % 4 + 1) * LANES)
        gate_ref = rg_ref if p < 4 else mo_ref
        y = _heads_out(ht[p], nw_ref[p])
        out_ref[:, p * LANES:(p + 1) * LANES] = (y * gate_ref[:, sl].astype(F32)).astype(BF16)

    for p in range(4):
        cd = jnp.exp(lgk_ref[p:p + 1, :] * float(CHUNK))
        _ret_state_update(s_scr, p, kb[p], vb[p], dec_scr[2, p], cd, bd_ret)
    _mlstm_state_update(c_scr, n_scr, m_scr, kf32[4:], vb[4:], cf_all, bal_f[CHUNK - 1:CHUNK, :], 0, lo, bd_m)


def _mixer(y, g, gt, states, lg_smem, lgk, lgkt, cw_qk, nw, nb, nc, ctx_chunks):
    gw = 4 * LANES
    sretb, cmb, nmb, mmb = states
    rc = lambda b, c: b * nc + c
    hb = CHUNK // HALO
    nhalo = y.shape[0] // HALO
    blk = lambda j: pl.BlockSpec((CHUNK, gw), lambda b, c: (rc(b, c), j))
    return pl.pallas_call(
        functools.partial(_mixer_kernel, nc=nc, ctx_chunks=ctx_chunks),
        grid=(nb, nc),
        in_specs=[_smem_spec(),
                  blk(0), blk(1), blk(2), blk(3),
                  pl.BlockSpec((CHUNK, 2 * gw), lambda b, c: (rc(b, c), 2)),
                  pl.BlockSpec((HALO, 2 * gw), lambda b, c: (jnp.maximum(rc(b, c) * hb - 1, 0), 2)),
                  pl.BlockSpec((HALO, 2 * gw), lambda b, c: (jnp.minimum((rc(b, c) + 1) * hb, nhalo - 1), 2)),
                  blk(6), blk(7),
                  pl.BlockSpec((CHUNK, LANES), lambda b, c: (rc(b, c), 0)),
                  pl.BlockSpec((32, CHUNK), lambda b, c: (0, rc(b, c))),
                  pl.BlockSpec((None, 4, CHUNK, LANES), lambda b, c: (rc(b, c), 0, 0, 0)),
                  pl.BlockSpec((None, 4, CHUNK, LANES), lambda b, c: (rc(b, c), 0, 0, 0)),
                  pl.BlockSpec((None, 4, HALO, LANES), lambda b, c: (rc(b, c), 0, 0, 0)),
                  pl.BlockSpec((None, 8, LANES), lambda b, c: (rc(b, c), 0, 0)),
                  _const_spec(lgk.shape), _const_spec(lgkt.shape), _const_spec(cw_qk.shape),
                  _const_spec(nw.shape)],
        out_specs=pl.BlockSpec((CHUNK, 2 * gw), lambda b, c: (rc(b, c), 0)),
        out_shape=jax.ShapeDtypeStruct((nb * nc * CHUNK, 2 * gw), BF16),
        scratch_shapes=[pltpu.VMEM((4, CHUNK, LANES), F32), pltpu.VMEM((4, CHUNK, LANES), F32),
                        pltpu.VMEM((4, HALO, LANES), F32), pltpu.VMEM((8, LANES), F32),
                        pltpu.VMEM((8, CHUNK, LANES), F32), pltpu.VMEM((3, 4, CHUNK, LANES), F32)],
        compiler_params=_cparams("arbitrary", "arbitrary"),
        name="ret_mlstm_mixer",
    )(lg_smem, y, y, y, y, y, y, y, y, y, g, gt, sretb, cmb, nmb, mmb, lgk, lgkt, cw_qk, nw)


class _Bag:
    def __init__(self, **kw):
        self.__dict__.update(kw)


def _mixer_chunk_stages(blk, r):
    rows = slice(blk * CHUNK, (blk + 1) * CHUNK)
    lane = _lane()
    sub = lax.broadcasted_iota(jnp.int32, (CHUNK, LANES), 0)
    lo = lane < HEAD_DIM
    sub_lo = sub < HEAD_DIM
    mask_ret = [(lane & 32) == 0, (lane & 32) != 0]
    mask_nat = [lo, lane >= HEAD_DIM]
    bd_ret = (sub >= HEAD_DIM) == ((lane & 32) != 0)
    bd_m = (sub >= HEAD_DIM) == (lane >= HEAD_DIM)
    le = sub <= lane
    ge = sub >= lane

    g = r.g_ref[rows, :]
    gt = r.gt_ref[:, rows]
    lf_col = _log_sigmoid(g)
    lf_row = _log_sigmoid(gt)
    le_bf = le.astype(BF16)
    ge_bf = ge.astype(BF16)
    bal_f = pltpu.roll(_cumsum_cols(ge_bf, lf_col), LANES - 8, 1)
    bal_b = pltpu.roll(_cumsum_cols(le_bf, lf_col), LANES - 8, 1)
    cf_all = g - bal_f
    cb_all = g - bal_b
    bf_row = _cumsum_rows(lf_row, le_bf)
    bb_row = _cumsum_rows(lf_row, ge_bf)
    qb, kb, vb, kf32, qt, vbd = [], [], [], [], [], []
    for p in range(8):
        sl = slice((p % 4) * LANES, (p % 4 + 1) * LANES)
        if p < 4:
            q2, k2, v2 = r.rq_ref[rows, sl], r.rk_ref[rows, sl], r.rv_ref[rows, sl]
            qf = q2.astype(F32)
            kf = None
        else:
            q2 = r.qk_ref[rows, sl]
            k2 = r.qk_ref[rows, 4 * LANES + (p - 4) * LANES:4 * LANES + (p - 3) * LANES]
            v2 = r.mv_ref[rows, sl]
            qf, kf = q2.astype(F32), k2.astype(F32)
        vt = v2.astype(F32).T.astype(BF16)
        qb.append(q2)
        kb.append(k2)
        vb.append(v2)
        kf32.append(kf)
        qt.append(qf.T)
        vbd.append([jnp.where(sub_lo, vt, jnp.zeros_like(vt)), jnp.where(sub_lo, jnp.zeros_like(vt), vt)])
    yield

    st2 = []
    for p in range(8):
        masks = mask_ret if p < 4 else mask_nat
        zero = jnp.zeros_like(qb[p])
        qstack = jnp.concatenate([jnp.where(masks[0], qb[p], zero), jnp.where(masks[1], qb[p], zero)], axis=0)
        st2.append(_dot_nt(kb[p], qstack))
    yield

    qn_f = [_dot_nt(r.n_scr[p].astype(BF16), qb[4 + p]) for p in range(4)]
    qn_b = [_dot_nt(r.nmb_ref[blk, p], qb[4 + p]) for p in range(4)]
    lhs, rhs = [], []
    for p in range(8):
        if p < 4:
            pts = [(st2[p][:, a * LANES:(a + 1) * LANES] * r.dm_scr[2 * p + a]).astype(BF16) for a in range(2)]
            x_f, x_b = r.dec_scr[0, p], r.dec_scr[1, p]
            old = [r.s_scr[p].astype(BF16), r.sretb_ref[blk, p]]
        else:
            pts, cf, cb = [], [], []
            for a in range(2):
                h = 2 * (p - 4) + a
                st = st2[p][:, a * LANES:(a + 1) * LANES]
                pf, coef_f = _mlstm_dir_weights(st, qn_f[p - 4][a:a + 1, :], cf_all[:, h:h + 1],
                                                bf_row[8 + h:9 + h, :], r.m_state[0:1, h:h + 1], le)
                pb, coef_b = _mlstm_dir_weights(st, qn_b[p - 4][a:a + 1, :], cb_all[:, 16 + h:17 + h],
                                                bb_row[24 + h:25 + h, :], r.mmb_ref[blk, 0:1, 16 + h:17 + h], ge)
                pts.append((pf + pb).astype(BF16))
                cf.append(coef_f)
                cb.append(coef_b)
            x_f = jnp.where(sub_lo, cf[0], cf[1])
            x_b = jnp.where(sub_lo, cb[0], cb[1])
            old = [r.c_scr[p - 4].astype(BF16), r.cmb_ref[blk, p - 4]]
        lhs.append(jnp.concatenate(vbd[p] + old, axis=1))
        rhs.append(jnp.concatenate(pts + [(qt[p] * x_f).astype(BF16), (qt[p] * x_b).astype(BF16)], axis=0))
    yield

    ht = [_dot(lhs[p], rhs[p]) for p in range(8)]
    yield

    for p in range(8):
        sl = slice((p % 4) * LANES, (p % 4 + 1) * LANES)
        gate_ref = r.rg_ref if p < 4 else r.mo_ref
        y = _heads_out(ht[p], r.nw_ref[p])
        r.mix_scr[rows, p * LANES:(p + 1) * LANES] = (y * gate_ref[rows, sl].astype(F32)).astype(BF16)
    yield

    for p in range(4):
        cd = jnp.exp(r.lgk_ref[p:p + 1, :] * float(CHUNK))
        _ret_state_update(r.s_scr, p, kb[p], vb[p], r.dec_scr[2, p], cd, bd_ret)
    _mlstm_state_update(r.c_scr, r.n_scr, r.m_state, kf32[4:], vb[4:], cf_all, bal_f[CHUNK - 1:CHUNK, :],
                        0, lo, bd_m)
    yield


def _ffn_splits(d_ff, pieces):
    blocks = d_ff // (2 * LANES)
    assert blocks * 2 * LANES == d_ff and blocks >= pieces
    cuts = [((i * blocks) // pieces) * 2 * LANES for i in range(pieces)]
    return cuts + [d_ff]


def _mixer_ffn_kernel(lg_ref, rq_ref, rk_ref, rv_ref, rg_ref, qk_ref, mv_ref, mo_ref,
                      g_ref, gt_ref, sretb_ref, cmb_ref, nmb_ref, mmb_ref,
                      lgk_ref, lgkt_ref, nw_ref,
                      ctx_ref, x_ref, p_ref, wo_ref, wi_ref, w2_ref,
                      o_ref,
                      s_scr, c_scr, n_scr, m_state, dm_scr, dec_scr, mix_scr, act_scr, *, tiles, tpb, d_ff):
    s = pl.program_id(0)
    jt = jnp.minimum(s, tiles - 1) % tpb
    lane = _lane()
    sub = lax.broadcasted_iota(jnp.int32, (CHUNK, LANES), 0)

    @pl.when(s == 0)
    def _():
        mix_scr[...] = jnp.zeros_like(mix_scr)
        le = sub <= lane
        ge = sub >= lane
        spos = sub.astype(F32)
        tpos = lane.astype(F32)
        diff = (lane - sub).astype(F32)
        for h in range(8):
            dm_scr[h] = (jnp.where(le, jnp.exp(lg_ref[h] * diff), 0.0)
                         + jnp.where(ge, jnp.exp(lg_ref[8 + h] * (-diff)), 0.0))
        for p in range(4):
            dec_scr[0, p] = jnp.exp(lgkt_ref[:, p:p + 1] * (tpos + 1.0))
            dec_scr[1, p] = jnp.exp(lgkt_ref[:, 4 + p:5 + p] * (float(CHUNK) - tpos))
            dec_scr[2, p] = jnp.exp(lgk_ref[p:p + 1, :] * (float(CHUNK) - 1.0 - spos))

    @pl.when(jt == 0)
    def _():
        s_scr[...] = jnp.zeros_like(s_scr)
        c_scr[...] = jnp.zeros_like(c_scr)
        n_scr[...] = jnp.zeros_like(n_scr)
        m_state[...] = jnp.zeros_like(m_state)

    r = _Bag(rq_ref=rq_ref, rk_ref=rk_ref, rv_ref=rv_ref, rg_ref=rg_ref, qk_ref=qk_ref, mv_ref=mv_ref, mo_ref=mo_ref,
             g_ref=g_ref, gt_ref=gt_ref, sretb_ref=sretb_ref, cmb_ref=cmb_ref, nmb_ref=nmb_ref, mmb_ref=mmb_ref,
             lgk_ref=lgk_ref, nw_ref=nw_ref, s_scr=s_scr, c_scr=c_scr, n_scr=n_scr, m_state=m_state,
             dm_scr=dm_scr, dec_scr=dec_scr, mix_scr=mix_scr)
    cuts = _ffn_splits(d_ff, 4)
    ffn_piece = lambda i: _ffn_cols(h, wi_ref, act_scr, cuts[i], cuts[i + 1], d_ff)

    x = _ctx_or_latent_rows(ctx_ref, x_ref, jnp.maximum(s - 1, 0), tpb)
    x1, h = _ffn_in(x, mix_scr[...], p_ref, wo_ref)

    chunk_a, chunk_b = _mixer_chunk_stages(0, r), _mixer_chunk_stages(1, r)
    next(chunk_a), next(chunk_b)
    next(chunk_a), next(chunk_b)
    ffn_piece(0)
    next(chunk_a), next(chunk_a)
    ffn_piece(1)
    next(chunk_a), next(chunk_a)
    ffn_piece(2)
    next(chunk_b), next(chunk_b)
    ffn_piece(3)
    next(chunk_b), next(chunk_b)
    o_ref[...] = x1 + p_ref[3:4, :] * _dot(act_scr[...], w2_ref[...])


def _mixer_ffn(y, g, gt, states, lg_smem, lgk, lgkt, nw, xs, prm, wo, wi_all, w2_all, layer, nb, tpb):
    gw = 4 * LANES
    tm = ROW_TILE
    cpt = tm // CHUNK
    sretb, cmb, nmb, mmb, qk_act = states
    tiles = nb * tpb
    d = xs[-1].shape[1]
    d_ff = w2_all.shape[1]
    mix = lambda s: jnp.minimum(s, tiles - 1)
    ffn = lambda s: jnp.maximum(s - 1, 0)
    blk =lambda j: pl.BlockSpec((tm, gw), lambda s: (mix(s), j))
    state_spec = lambda a: pl.BlockSpec((cpt,) + a.shape[1:], lambda s: (mix(s),) + (0,) * (a.ndim - 1))
    sel = lambda t: ((t // tpb) * 2 + jnp.minimum(t % tpb, 1), 0, 0)
    layer_spec = lambda a: pl.BlockSpec((None,) + a.shape[1:], lambda s: (layer, 0, 0),
                                        pipeline_mode=pl.Buffered(1))
    return pl.pallas_call(
        functools.partial(_mixer_ffn_kernel, tiles=tiles, tpb=tpb, d_ff=d_ff),
        grid=(tiles + 1,),
        in_specs=[_smem_spec(),
                  blk(0), blk(1), blk(2), blk(3),
                  pl.BlockSpec((tm, 2 * gw), lambda s: (mix(s), 0)),
                  blk(6), blk(7),
                  pl.BlockSpec((tm, LANES), lambda s: (mix(s), 0)),
                  pl.BlockSpec((32, tm), lambda s: (0, mix(s))),
                  state_spec(sretb), state_spec(cmb), state_spec(nmb), state_spec(mmb),
                  _const_spec(lgk.shape), _const_spec(lgkt.shape), _const_spec(nw.shape)]
                 + _split_row_specs(tm, d, tpb, ffn)
                 + [pl.BlockSpec((None, 8, d), lambda s: sel(ffn(s))),
                    _const_spec(wo.shape), layer_spec(wi_all), layer_spec(w2_all)],
        out_specs=pl.BlockSpec((tm, d), lambda s: (ffn(s), 0)),
        out_shape=jax.ShapeDtypeStruct((tiles * tm, d), F32),
        scratch_shapes=[pltpu.VMEM((4, CHUNK, LANES), F32), pltpu.VMEM((4, CHUNK, LANES), F32),
                        pltpu.VMEM((4, HALO, LANES), F32), pltpu.VMEM((8, LANES), F32),
                        pltpu.VMEM((8, CHUNK, LANES), F32), pltpu.VMEM((3, 4, CHUNK, LANES), F32),
                        pltpu.VMEM((tm, 2 * gw), BF16), pltpu.VMEM((tm, d_ff), BF16)],
        compiler_params=_cparams("arbitrary"),
        name="ret_mlstm_mixer_ffn",
    )(lg_smem, y, y, y, y, qk_act, y, y, g, gt, sretb, cmb, nmb, mmb, lgk, lgkt, nw,
      *xs, prm, wo, wi_all, w2_all)


def _attn_kernel(sink_ref, q_ref, kp_ref, kc_ref, kn_ref, kx_ref, vp_ref, vc_ref, vn_ref, vx_ref,
                 bias_ref, o_ref):
    grp = lax.broadcasted_iota(jnp.int32, (1, 4 * CHUNK), 1) // CHUNK
    lane = _lane()
    mask_q = [(lane & 32) == 0, (lane & 32) != 0]
    bias = jnp.concatenate([bias_ref[...]] * 4, axis=1)
    vts, sts = [], []
    for kvp in range(2):
        sl = slice(kvp * LANES, (kvp + 1) * LANES)
        v_blocks = [vp_ref[:, sl], vc_ref[:, sl], vn_ref[:, sl], vx_ref[:, sl]]
        vts.append(jnp.concatenate(
            [vb[r * CHUNK:(r + 1) * CHUNK, :].astype(F32).T.astype(BF16)
             for vb in v_blocks for r in range(vb.shape[0] // CHUNK)], axis=1))
    for kvp in range(2):
        sl = slice(kvp * LANES, (kvp + 1) * LANES)
        kcat = jnp.concatenate([kp_ref[:, sl], kc_ref[:, sl], kn_ref[:, sl], kx_ref[:, sl]], axis=0)
        for a in range(2):
            qs = jnp.concatenate(
                [jnp.where(mask_q[a], q_ref[:, (kvp * 4 + g) * LANES:(kvp * 4 + g + 1) * LANES],
                           jnp.zeros((CHUNK, LANES), BF16)) for g in range(4)], axis=0)
            sts.append(_dot_nt(kcat, qs))
    outs = []
    for kv in range(4):
        st = sts[kv]
        st = jnp.concatenate([st[0:CHUNK] + bias[0:CHUNK], st[CHUNK:2 * CHUNK],
                              st[2 * CHUNK:3 * CHUNK] + bias[CHUNK:2 * CHUNK], st[3 * CHUNK:]], axis=0)
        snk = jnp.where(grp == 0, sink_ref[kv * 4],
                        jnp.where(grp == 1, sink_ref[kv * 4 + 1],
                                  jnp.where(grp == 2, sink_ref[kv * 4 + 2], sink_ref[kv * 4 + 3])))
        m = jnp.maximum(jnp.max(st, axis=0, keepdims=True), snk)
        e = jnp.exp(st - m)
        denom = jnp.exp(snk - m) + jnp.sum(e, axis=0, keepdims=True)
        a = kv % 2
        outs.append(_dot(vts[kv // 2][a * HEAD_DIM:(a + 1) * HEAD_DIM, :], e.astype(BF16)) * (1.0 / denom))
    for kvp in range(2):
        full = jnp.concatenate(outs[2 * kvp:2 * kvp + 2], axis=0)
        for g in range(4):
            o_ref[:, (kvp * 4 + g) * LANES:(kvp * 4 + g + 1) * LANES] = (
                full[:, g * CHUNK:(g + 1) * CHUNK].T.astype(BF16))


def _window_bias():
    kk = np.arange(CHUNK)[:, None]
    t = np.arange(CHUNK)[None, :]
    tabs = []
    for has_prev, has_next in ((False, True), (True, True), (True, False)):
        prev_ok = (kk >= t) & has_prev
        next_ok = (kk <= t) & has_next
        tabs.append(np.where(np.concatenate([prev_ok, next_ok], axis=0), 0.0, NEG))
    return jnp.asarray(np.stack(tabs), F32)


def _window_attn(y, sink, nb, nc, ctx_chunks):
    nq = nc - ctx_chunks
    assert nq >= 2
    ctx_len = ctx_chunks * CHUNK
    bias = _window_bias()
    d = 8 * LANES
    kcol, vcol = d // (2 * LANES), d // (2 * LANES) + 1
    qrow = lambda b, i: b * nc + ctx_chunks + i
    prow = lambda b, i: b * nc + ctx_chunks + jnp.maximum(i - 1, 0)
    nrow = lambda b, i: b * nc + ctx_chunks + jnp.minimum(i + 1, nq - 1)
    xrow = lambda b, i: (b * nc * CHUNK) // ctx_len
    kv_spec = lambda rowf, col: pl.BlockSpec((CHUNK, 2 * LANES), lambda b, i: (rowf(b, i), col))
    x_spec = lambda col: pl.BlockSpec((ctx_len, 2 * LANES), lambda b, i: (xrow(b, i), col))
    bias_spec = pl.BlockSpec((None,) + bias.shape[1:],
                             lambda b, i: (jnp.where(i == 0, 0, jnp.where(i == nq - 1, 2, 1)), 0, 0))
    return pl.pallas_call(
        _attn_kernel,
        grid=(nb, nq),
        in_specs=[_smem_spec(),
                  pl.BlockSpec((CHUNK, d), lambda b, i: (qrow(b, i), 0)),
                  kv_spec(prow, kcol), kv_spec(qrow, kcol), kv_spec(nrow, kcol), x_spec(kcol),
                  kv_spec(prow, vcol), kv_spec(qrow, vcol), kv_spec(nrow, vcol), x_spec(vcol),
                  bias_spec],
        out_specs=pl.BlockSpec((CHUNK, d), lambda b, i: (b * nq + i, 0)),
        out_shape=jax.ShapeDtypeStruct((nb * nq * CHUNK, d), BF16),
        compiler_params=_cparams("arbitrary", "arbitrary"),
        name="window_gqa",
    )(sink, y, y, y, y, y, y, y, y, y, bias)


def _ffn_in(x, m, p_ref, wo_ref):
    x1 = x + p_ref[0:1, :] * _dot(m, wo_ref[...])
    ms = jnp.mean(x1 * x1, axis=-1, keepdims=True)
    return x1, ((x1 * lax.rsqrt(ms + EPS)) * p_ref[1:2, :] + p_ref[2:3, :]).astype(BF16)


def _ffn_cols(h, wi_ref, act_scr, lo, hi, d_ff):
    gate = _dot(h, wi_ref[:, lo:hi])
    up = _dot(h, wi_ref[:, d_ff + lo:d_ff + hi])
    act_scr[:, lo:hi] = (gate * jax.nn.sigmoid(gate) * up).astype(BF16)


def _attn_ffn_kernel(sink_ref, q_ref, kp_ref, kc_ref, kn_ref, kx_ref, vp_ref, vc_ref, vn_ref, vx_ref, bias_ref,
                     x_ref, p_ref, wo_ref, wi_ref, w2_ref, o_ref, m_scr, act_scr, *, tiles, lat_tiles, d_ff):
    s = pl.program_id(0)

    @pl.when(s == 0)
    def _():
        m_scr[...] = jnp.zeros_like(m_scr)

    j = jnp.minimum(s, tiles - 1) % lat_tiles
    grp = lax.broadcasted_iota(jnp.int32, (1, 4 * CHUNK), 1) // CHUNK
    lane = _lane()
    mask_q = [(lane & 32) == 0, (lane & 32) != 0]
    tile4 = lambda b: jnp.concatenate([b] * 4, axis=1)
    biases = [tile4(jnp.where(j == 0, bias_ref[0], bias_ref[1])),
              tile4(jnp.where(j == lat_tiles - 1, bias_ref[2], bias_ref[1]))]
    cuts = _ffn_splits(d_ff, 6)
    d = o_ref.shape[1]

    def ffn_up(i):
        _ffn_cols(h, wi_ref, act_scr, cuts[i], cuts[i + 1], d_ff)

    def ffn_down(i):
        cs = slice(i * d // 4, (i + 1) * d // 4)
        o_ref[:, cs] = x1[:, cs] + p_ref[3:4, cs] * _dot(act_scr[...], w2_ref[:, cs])

    def scores(u):
        blk, kvp, a = u // 4, (u % 4) // 2, u % 2
        rows = slice(blk * CHUNK, (blk + 1) * CHUNK)
        qs = jnp.concatenate(
            [jnp.where(mask_q[a], q_ref[rows, (kvp * 4 + g) * LANES:(kvp * 4 + g + 1) * LANES],
                       jnp.zeros((CHUNK, LANES), BF16)) for g in range(4)], axis=0)
        return _dot_nt(kcats[kvp][blk], qs)

    def softmax_pv(u, st):
        blk, kv = u // 4, u % 4
        bias = biases[blk]
        st = jnp.concatenate([st[0:CHUNK] + bias[0:CHUNK], st[CHUNK:2 * CHUNK],
                              st[2 * CHUNK:3 * CHUNK] + bias[CHUNK:2 * CHUNK], st[3 * CHUNK:]], axis=0)
        snk = jnp.where(grp == 0, sink_ref[kv * 4],
                        jnp.where(grp == 1, sink_ref[kv * 4 + 1],
                                  jnp.where(grp == 2, sink_ref[kv * 4 + 2], sink_ref[kv * 4 + 3])))
        m = jnp.maximum(jnp.max(st, axis=0, keepdims=True), snk)
        e = jnp.exp(st - m)
        denom = jnp.exp(snk - m) + jnp.sum(e, axis=0, keepdims=True)
        a = kv % 2
        return _dot(vts[kv // 2][blk][a * HEAD_DIM:(a + 1) * HEAD_DIM, :], e.astype(BF16)) * (1.0 / denom)

    def hand_over(blk, outs):
        for kvp in range(2):
            full = jnp.concatenate(outs[2 * kvp:2 * kvp + 2], axis=0)
            for g in range(4):
                m_scr[blk * CHUNK:(blk + 1) * CHUNK, (kvp * 4 + g) * LANES:(kvp * 4 + g + 1) * LANES] = (
                    full[:, g * CHUNK:(g + 1) * CHUNK].T.astype(BF16))

    x1, h = _ffn_in(x_ref[...], m_scr[...], p_ref, wo_ref)

    kcats, vts = [], []
    for kvp in range(2):
        sl = slice(kvp * LANES, (kvp + 1) * LANES)
        k_chunks = [kp_ref[:, sl], kc_ref[0:CHUNK, sl], kc_ref[CHUNK:2 * CHUNK, sl], kn_ref[:, sl]]
        v_chunks = [vp_ref[:, sl], vc_ref[0:CHUNK, sl], vc_ref[CHUNK:2 * CHUNK, sl], vn_ref[:, sl],
                    vx_ref[0:CHUNK, sl], vx_ref[CHUNK:2 * CHUNK, sl]]
        v_t = [v.astype(F32).T.astype(BF16) for v in v_chunks]
        kcats.append([jnp.concatenate(k_chunks[b:b + 3] + [kx_ref[:, sl]], axis=0) for b in range(2)])
        vts.append([jnp.concatenate(v_t[b:b + 3] + v_t[4:], axis=1) for b in range(2)])

    ffn_pieces = [functools.partial(ffn_up, i) for i in range(6)] + [functools.partial(ffn_down, i) for i in range(2)]
    sts = {0: scores(0)}
    outs = []
    for u in range(8):
        if u + 1 < 8:
            sts[u + 1] = scores(u + 1)
        ffn_pieces[u]()
        outs.append(softmax_pv(u, sts.pop(u)))
        if u % 4 == 3:
            hand_over(u // 4, outs[u - 3:u + 1])
    ffn_down(2)
    ffn_down(3)


def _attn_ffn(y, sink, xc, prm, wo, wi_all, w2_all, layer, nb, nc, ctx_chunks):
    tm = ROW_TILE
    cpt = tm // CHUNK
    tpb = nc // cpt
    lat_tiles = (nc - ctx_chunks) // cpt
    tiles = nb * lat_tiles
    ctx_tiles = ctx_chunks // cpt
    assert ctx_tiles == 1 and lat_tiles >= 2
    d = xc.shape[1]
    d_ff = w2_all.shape[1]
    bias = _window_bias()
    kcol, vcol = d // (2 * LANES), d // (2 * LANES) + 1
    att = lambda s: jnp.minimum(s, tiles - 1)
    ffn = lambda s: jnp.maximum(s - 1, 0)
    row_tile = lambda t: (t // lat_tiles) * tpb + ctx_tiles + t % lat_tiles
    chunk0 = lambda t: (t // lat_tiles) * nc + ctx_chunks
    prev_c = lambda s: chunk0(att(s)) + jnp.maximum((att(s) % lat_tiles) * cpt - 1, 0)
    next_c = lambda s: chunk0(att(s)) + jnp.minimum((att(s) % lat_tiles) * cpt + cpt, lat_tiles * cpt - 1)
    cur_spec = lambda col: pl.BlockSpec((tm, 2 * LANES), lambda s: (row_tile(att(s)), col))
    edge_spec = lambda f, col: pl.BlockSpec((CHUNK, 2 * LANES), lambda s: (f(s), col))
    ctx_spec = lambda col: pl.BlockSpec((tm, 2 * LANES), lambda s: ((att(s) // lat_tiles) * tpb, col))
    layer_spec = lambda a: pl.BlockSpec((None,) + a.shape[1:], lambda s: (layer, 0, 0),
                                        pipeline_mode=pl.Buffered(1))
    return pl.pallas_call(
        functools.partial(_attn_ffn_kernel, tiles=tiles, lat_tiles=lat_tiles, d_ff=d_ff),
        grid=(tiles + 1,),
        in_specs=[_smem_spec(),
                  pl.BlockSpec((tm, d), lambda s: (row_tile(att(s)), 0)),
                  edge_spec(prev_c, kcol), cur_spec(kcol), edge_spec(next_c, kcol), ctx_spec(kcol),
                  edge_spec(prev_c, vcol), cur_spec(vcol), edge_spec(next_c, vcol), ctx_spec(vcol),
                  _const_spec(bias.shape),
                  pl.BlockSpec((tm, d), lambda s: (row_tile(ffn(s)), 0)),
                  pl.BlockSpec((None, 8, d), lambda s: ((ffn(s) // lat_tiles) * 2 + 1, 0, 0)),
                  _const_spec(wo.shape), layer_spec(wi_all), layer_spec(w2_all)],
        out_specs=pl.BlockSpec((tm, d), lambda s: (ffn(s), 0)),
        out_shape=jax.ShapeDtypeStruct((tiles * tm, d), F32),
        scratch_shapes=[pltpu.VMEM((tm, d), BF16), pltpu.VMEM((tm, d_ff), BF16)],
        compiler_params=_cparams("arbitrary"),
        name="window_gqa_ffn",
    )(sink, y, y, y, y, y, y, y, y, y, bias, xc, prm, wo, wi_all, w2_all)


def _post_kernel(*refs, d_ff, split_tpb):
    tps = TILES_PER_STEP
    n_x = 2 if split_tpb else 1
    x_refs, refs = refs[:n_x * tps], refs[n_x * tps:]
    m_refs, p_refs = refs[:tps], refs[tps:2 * tps]
    wo_ref, wi_ref, w2_ref, o_ref, act_scr = refs[2 * tps:]
    tm = m_refs[0].shape[0]
    ys = [_dot(m_refs[u][...], wo_ref[...]) for u in range(tps)]
    x1s, hs = [], []
    for u in range(tps):
        if split_tpb:
            x = _ctx_or_latent_rows(x_refs[2 * u], x_refs[2 * u + 1], pl.program_id(0) * tps + u, split_tpb)
        else:
            x = x_refs[u][...]
        p_ref = p_refs[u]
        x1 = x + p_ref[0:1, :] * ys[u]
        ms = jnp.mean(x1 * x1, axis=-1, keepdims=True)
        hs.append(((x1 * lax.rsqrt(ms + EPS)) * p_ref[1:2, :] + p_ref[2:3, :]).astype(BF16))
        x1s.append(x1)
    for u in range(tps):
        gate = _dot(hs[u], wi_ref[:, :d_ff])
        up = _dot(hs[u], wi_ref[:, d_ff:])
        act_scr[u] = (gate * jax.nn.sigmoid(gate) * up).astype(BF16)
    for u in range(tps):
        o_ref[u * tm:(u + 1) * tm, :] = x1s[u] + p_refs[u][3:4, :] * _dot(act_scr[u], w2_ref[...])


def _post(xs, m2, prm, wo, wi_all, w2_all, layer, tiles_per_batch, skip_tiles, m_has_ctx):
    split = len(xs) == 2
    d = xs[-1].shape[1]
    r = sum(a.shape[0] for a in xs)
    tm = ROW_TILE
    tps = TILES_PER_STEP
    d_ff = w2_all.shape[1]
    tpb = tiles_per_batch
    kept = tpb - skip_tiles
    nb = r // (tm * tpb)
    assert not (split and skip_tiles) and (nb * kept) % tps == 0
    xrow = lambda t: (t // kept) * tpb + skip_tiles + t % kept
    mrow = xrow if m_has_ctx else (lambda t: t)
    sel = lambda t: ((t // kept) * 2 + jnp.minimum(skip_tiles + t % kept, 1), 0, 0)
    at = lambda f, u: (lambda i: f(i * tps + u))
    row_spec = lambda f, u: pl.BlockSpec((tm, d), lambda i: (f(i * tps + u), 0))
    x_specs = []
    for u in range(tps):
        x_specs += _split_row_specs(tm, d, tpb, at(lambda t: t, u)) if split else [row_spec(xrow, u)]
    layer_spec = lambda a: pl.BlockSpec((None,) + a.shape[1:], lambda i: (layer, 0, 0),
                                        pipeline_mode=pl.Buffered(1))
    return pl.pallas_call(
        functools.partial(_post_kernel, d_ff=d_ff, split_tpb=tpb if split else 0),
        grid=(nb * kept // tps,),
        in_specs=x_specs + [row_spec(mrow, u) for u in range(tps)]
                 + [pl.BlockSpec((None, 8, d), at(sel, u)) for u in range(tps)]
                 + [_const_spec(wo.shape), layer_spec(wi_all), layer_spec(w2_all)],
        out_specs=pl.BlockSpec((tps * tm, d), lambda i: (i, 0)),
        out_shape=jax.ShapeDtypeStruct((nb * kept * tm, d), F32),
        scratch_shapes=[pltpu.VMEM((tps, tm, d_ff), BF16)],
        compiler_params=_cparams("arbitrary"),
        name="outproj_swiglu",
    )(*(list(xs) * tps), *([m2] * tps), *([prm] * tps), wo, wi_all, w2_all)


def _pair_cols(w):
    rows, cols = w.shape
    return w.reshape(rows, cols // LANES, 2, 2, 32).transpose(0, 1, 3, 2, 4).reshape(rows, cols)


def _attn_q_cols(w):
    rows = w.shape[0]
    g_per = w.shape[1] // (H_KV * HEAD_DIM)
    return (w.reshape(rows, H_KV // 2, 2, g_per, 2, 32).transpose(0, 1, 3, 4, 2, 5)
            .reshape(rows, w.shape[1]))


def _attn_o_rows(w):
    cols = w.shape[1]
    g_per = w.shape[0] // (H_KV * HEAD_DIM)
    return (w.reshape(H_KV // 2, 2, g_per, HEAD_DIM, cols).transpose(0, 2, 1, 3, 4)
            .reshape(w.shape[0], cols))


def _rope_tables(seq, ctx_len):
    rows = seq // GRID_W
    row = np.repeat(np.arange(rows, dtype=np.float32), GRID_W)
    col = np.tile(np.arange(GRID_W, dtype=np.float32), rows)
    n = HEAD_DIM // 4
    inv = (np.float32(ROPE_BASE) ** (-np.arange(n, dtype=np.float32) / np.float32(n))).astype(np.float32)
    ang = np.concatenate([row[:, None] * inv, col[:, None] * inv], axis=-1).astype(np.float32)
    cos, sin = np.cos(ang), np.sin(ang)
    cos_t = np.concatenate([np.ones((ctx_len, LANES), np.float32), np.tile(cos, (1, 4))], axis=0)
    sin_t = np.concatenate([np.zeros((ctx_len, LANES), np.float32),
                            np.concatenate([-sin, -sin, sin, sin], axis=-1)], axis=0)
    return jnp.asarray(cos_t, F32), jnp.asarray(sin_t, F32)


def _mod_tables(mod, nb, norm_w):
    d = norm_w.shape[-1]
    lat = mod[:nb].reshape(nb, 6, d)
    ctx = jnp.broadcast_to(mod[nb].reshape(1, 6, d), (nb, 6, d))
    both = jnp.stack([ctx, lat], axis=1).reshape(nb * 2, 6, d)
    sh1, sc1, g1, sh2, sc2, g2 = [both[:, k] for k in range(6)]
    ab1 = jnp.stack([norm_w[0] * (1.0 + sc1), sh1], axis=1)
    zeros = jnp.zeros_like(g1)
    prm = jnp.stack([g1, norm_w[1] * (1.0 + sc2), sh2, g2, zeros, zeros, zeros, zeros], axis=1)
    return ab1, prm


def kernel(x, c, ctx, c_ctx, ada_w, ada_b, norm_w, ffn_w_in, ffn_w_out, ab_w_in, ab_w_out,
           ret_log_gamma, ret_norm_w, mlstm_conv_w, mlstm_conv_b, mlstm_gate_b, mlstm_norm_w,
           attn_w_in, attn_w_out, attn_q_norm_w, attn_k_norm_w, attn_sink):
    nb, seq, d = x.shape
    ctx_len = ctx.shape[1]
    depth = ada_w.shape[0]
    assert ctx_len == ROW_TILE and seq % ROW_TILE == 0 and d == 8 * LANES and nb < 8
    t_all = ctx_len + seq
    nc = t_all // CHUNK
    ctx_chunks = ctx_len // CHUNK
    tpb = t_all // ROW_TILE
    dr = d // 2

    rows = jnp.zeros((8, d), F32).at[:nb].set(c).at[nb].set(c_ctx)
    mod_all = _modulation(rows, ada_w, ada_b)
    cos_t, sin_t = _rope_tables(seq, ctx_len)
    wi_all = ffn_w_in.astype(BF16)
    w2_all = ffn_w_out.astype(BF16)
    xs = (ctx.reshape(nb * ctx_len, d), x.reshape(nb * seq, d))

    out = None
    for layer in range(depth):
        last = layer == depth - 1
        ab1, prm = _mod_tables(mod_all[layer], nb, norm_w[layer])
        if layer % 2 == 0:
            assert layer == 0
            e = layer // 2
            w = ab_w_in[e]
            w_main = jnp.concatenate([_pair_cols(w[:, :dr]), _pair_cols(w[:, dr:2 * dr]), w[:, 2 * dr:8 * dr]],
                                     axis=1).astype(BF16)
            wg = jnp.zeros((d, LANES), F32).at[:, :32].set(w[:, 8 * dr:]).astype(BF16)
            wgt = w[:, 8 * dr:].T.astype(BF16)
            gb = jnp.zeros((1, LANES), F32).at[0, :32].set(mlstm_gate_b[e].reshape(-1))
            gbt = mlstm_gate_b[e].reshape(32, 1)
            y, g, gt = _inproj_ab(*xs, ab1, w_main, wg, wgt, gb, gbt, cos_t, sin_t, tpb)

            lg = ret_log_gamma[e].astype(F32)
            lgk = jnp.tile(jnp.repeat(lg.reshape(2, 4, 2), 32, axis=-1), (1, 1, 2)).reshape(8, LANES)
            cw =jnp.concatenate([mlstm_conv_w[e], mlstm_conv_b[e][None],
                                  jnp.zeros((4, 2 * dr), F32)], axis=0)
            nw = jnp.broadcast_to(jnp.concatenate([ret_norm_w[e], mlstm_norm_w[e]]).reshape(8, LANES, 1),
                                  (8, LANES, LANES))
            lg_s = lg.reshape(-1)
            states = _bwd_states(y, g, lgk, cw, nb, tpb)
            wo = ab_w_out[e].astype(BF16)
            out = _mixer_ffn(y, g, gt, states, lg_s, lgk, lgk.T, nw, xs, prm, wo, wi_all, w2_all, layer,
                             nb, tpb)
        else:
            o = layer // 2
            w = attn_w_in[o]
            w_main = jnp.concatenate([_attn_q_cols(w[:, :d]), _pair_cols(w[:, d:d + 2 * LANES]),
                                      w[:, d + 2 * LANES:]], axis=1).astype(BF16)
            lane_w = lambda v: jnp.concatenate([v[:32], v[:32], v[32:], v[32:]])
            nwq = jnp.stack([lane_w(attn_q_norm_w[o]) * (HEAD_DIM ** -0.5), lane_w(attn_k_norm_w[o])]
                            + [jnp.zeros((LANES,), F32)] * 6)
            assert last
            y = _inproj_attn(xs[0], ab1, w_main, nwq, cos_t, sin_t, tpb)
            wo = _attn_o_rows(attn_w_out[o]).astype(BF16)
            out = _attn_ffn(y, attn_sink[o].astype(F32), xs[0], prm, wo, wi_all, w2_all, layer,
                            nb, nc, ctx_chunks)
        xs = (out,)
    return out.reshape(nb, seq, d)
```

```python
import functools

import numpy as np
import jax
import jax.numpy as jnp
from jax import lax
from jax.experimental import pallas as pl
from jax.experimental.pallas import tpu as pltpu

F32 = jnp.float32
BF16 = jnp.bfloat16

HEAD_DIM = 64
CHUNK = 128
GRID_W = 64
ROPE_BASE = 10000.0
EPS = 1e-6
H_KV = 4
LANES = 128
ROW_TILE = 256
TILES_PER_STEP = 2
HALO = 16
NEG = -1e30
VMEM_LIMIT = 56 * 1024 * 1024


def _cparams(*sem):
    return pltpu.CompilerParams(dimension_semantics=sem, vmem_limit_bytes=VMEM_LIMIT)


def _const_spec(shape):
    nd = len(shape)
    return pl.BlockSpec(shape, lambda *_: (0,) * nd, pipeline_mode=pl.Buffered(1))


def _smem_spec():
    return pl.BlockSpec(memory_space=pltpu.SMEM)


def _lane(shape=(CHUNK, LANES)):
    return lax.broadcasted_iota(jnp.int32, shape, len(shape) - 1)


def _dot(a, b):
    return jnp.dot(a, b, preferred_element_type=F32)


def _dot_nt(a, b):
    return lax.dot_general(a, b, (((1,), (1,)), ((), ())), preferred_element_type=F32)


def _dot_tn(a, b):
    return lax.dot_general(a, b, (((0,), (0,)), ((), ())), preferred_element_type=F32)


def _split3(x):
    hi = x.astype(BF16)
    r = x - hi.astype(F32)
    mid = r.astype(BF16)
    lo = (r - mid.astype(F32)).astype(BF16)
    return hi, mid, lo


def _log_sigmoid(x):
    return jnp.minimum(x, 0.0) - jnp.log1p(jnp.exp(-jnp.abs(x)))


def _rope(x, cos, sin_signed):
    return x * cos + pltpu.roll(x, LANES // 2, 1) * sin_signed


def _mod_kernel(rows_ref, w_ref, b_ref, o_ref):
    a = rows_ref[...]
    a = a * jax.nn.sigmoid(a)
    a_hi = a.astype(BF16)
    a_lo = (a - a_hi.astype(F32)).astype(BF16)
    w = w_ref[...]
    w_hi = w.astype(BF16)
    w_lo = (w - w_hi.astype(F32)).astype(BF16)
    o_ref[...] = _dot(a_hi, w_hi) + _dot(a_hi, w_lo) + _dot(a_lo, w_hi) + b_ref[...]


def _modulation(rows, ada_w, ada_b):
    depth, d, n = ada_w.shape
    tn = n // 4
    return pl.pallas_call(
        _mod_kernel,
        grid=(depth, n // tn),
        in_specs=[pl.BlockSpec((8, d), lambda l, j: (0, 0)),
                  pl.BlockSpec((None, d, tn), lambda l, j: (l, 0, j)),
                  pl.BlockSpec((None, 1, tn), lambda l, j: (l, 0, j))],
        out_specs=pl.BlockSpec((None, 8, tn), lambda l, j: (l, 0, j)),
        out_shape=jax.ShapeDtypeStruct((depth, 8, n), F32),
        compiler_params=_cparams("arbitrary", "arbitrary"),
        name="adaln_modulation",
    )(rows, ada_w, ada_b.reshape(depth, 1, n))


def _norm_mod(x, ab_ref):
    ms = jnp.mean(x * x, axis=-1, keepdims=True)
    h = (x * lax.rsqrt(ms + EPS)) * ab_ref[0:1, :] + ab_ref[1:2, :]
    return h.astype(BF16)


def _ctx_or_latent_rows(ctx_ref, x_ref, tile, tiles_per_batch):
    return jnp.where(tile % tiles_per_batch == 0, ctx_ref[...], x_ref[...])


def _split_row_specs(tm, d, tpb, tile_of):
    lat = tpb - 1
    return [pl.BlockSpec((tm, d), lambda i: (tile_of(i) // tpb, 0)),
            pl.BlockSpec((tm, d), lambda i: ((tile_of(i) // tpb) * lat + jnp.maximum(tile_of(i) % tpb - 1, 0), 0))]


def _inproj_ab_kernel(*refs, tpb):
    tps = TILES_PER_STEP
    x_refs, ab_refs, refs = refs[:2 * tps], refs[2 * tps:3 * tps], refs[3 * tps:]
    w_ref, wg_ref, wgt_ref, gb_ref, gbt_ref = refs[:5]
    rope_refs, (y_ref, g_ref, gt_ref) = refs[5:5 + 2 * tps], refs[5 + 2 * tps:]
    tm = y_ref.shape[0] // tps
    hbs = [_norm_mod(_ctx_or_latent_rows(x_refs[2 * u], x_refs[2 * u + 1], pl.program_id(0) * tps + u, tpb),
                     ab_refs[u]) for u in range(tps)]
    gw = 4 * LANES
    for j in range(8):
        for u in range(tps):
            acc = _dot(hbs[u], w_ref[:, j * gw:(j + 1) * gw])
            if j in (0, 1):
                if j == 0:
                    acc = acc * (HEAD_DIM ** -0.5)
                cos, sin = rope_refs[2 * u][...], rope_refs[2 * u + 1][...]
                acc = jnp.concatenate(
                    [_rope(acc[:, p * LANES:(p + 1) * LANES], cos, sin) for p in range(4)], axis=1)
            elif j == 3:
                acc = acc * jax.nn.sigmoid(acc)
            elif j == 7:
                acc = jax.nn.sigmoid(acc)
            y_ref[u * tm:(u + 1) * tm, j * gw:(j + 1) * gw] = acc.astype(BF16)
    for u in range(tps):
        g_ref[u * tm:(u + 1) * tm, :] = _dot(hbs[u], wg_ref[...]) + gb_ref[...]
        gt_ref[:, u * tm:(u + 1) * tm] = _dot_nt(wgt_ref[...], hbs[u]) + gbt_ref[...]


def _tile_specs(tm, tpb, tps):
    at = lambda f, u: (lambda i: f(i * tps + u))
    sel = lambda t: ((t // tpb) * 2 + jnp.minimum(t % tpb, 1), 0, 0)
    mod_spec = lambda d, u: pl.BlockSpec((None, 2, d), at(sel, u))
    rope_spec = lambda u: pl.BlockSpec((tm, LANES), at(lambda t: (t % tpb, 0), u))
    return at, mod_spec, rope_spec


def _inproj_ab(ctx2, x2, ab, w, wg, wgt, gb, gbt, cos, sin, tiles_per_batch):
    d = x2.shape[1]
    r = ctx2.shape[0] + x2.shape[0]
    tm, tps = ROW_TILE, TILES_PER_STEP
    n = w.shape[1]
    tpb = tiles_per_batch
    assert (r // tm) % tps == 0
    at, mod_spec, rope_spec = _tile_specs(tm, tpb, tps)
    x_specs, rope_specs = [], []
    for u in range(tps):
        x_specs += _split_row_specs(tm, d, tpb, at(lambda t: t, u))
        rope_specs += [rope_spec(u), rope_spec(u)]
    return pl.pallas_call(
        functools.partial(_inproj_ab_kernel, tpb=tpb),
        grid=(r // (tm * tps),),
        in_specs=x_specs + [mod_spec(d, u) for u in range(tps)] + [
                  _const_spec(w.shape), _const_spec(wg.shape), _const_spec(wgt.shape),
                  _const_spec(gb.shape), _const_spec(gbt.shape)] + rope_specs,
        out_specs=[pl.BlockSpec((tps * tm, n), lambda i: (i, 0)),
                   pl.BlockSpec((tps * tm, LANES), lambda i: (i, 0)),
                   pl.BlockSpec((32, tps * tm), lambda i: (0, i))],
        out_shape=[jax.ShapeDtypeStruct((r, n), BF16),
                   jax.ShapeDtypeStruct((r, LANES), F32),
                   jax.ShapeDtypeStruct((32, r), F32)],
        compiler_params=_cparams("arbitrary"),
        name="inproj_ret_mlstm",
    )(*([ctx2, x2] * tps), *([ab] * tps), w, wg, wgt, gb, gbt, *([cos, sin] * tps))


def _inproj_attn_kernel(*refs):
    tps = TILES_PER_STEP
    x_ref, ab_refs, refs = refs[0], refs[1:1 + tps], refs[1 + tps:]
    w_ref, nw_ref = refs[:2]
    rope_refs, y_ref = refs[2:2 + 2 * tps], refs[2 + 2 * tps]
    tm = y_ref.shape[0] // tps
    hbs = [_norm_mod(x_ref[u * tm:(u + 1) * tm, :], ab_refs[u]) for u in range(tps)]
    r2 = lax.broadcasted_iota(jnp.int32, (2 * LANES, 2 * LANES), 0)
    c2 = lax.broadcasted_iota(jnp.int32, (2 * LANES, 2 * LANES), 1)
    same_head = (((r2 ^ c2) & (LANES | 32)) == 0).astype(BF16)
    acc_q = [_dot(hbs[u], w_ref[:, 0:8 * LANES]) for u in range(tps)]
    acc_k = [_dot(hbs[u], w_ref[:, 8 * LANES:10 * LANES]) for u in range(tps)]
    acc_v = [_dot(hbs[u], w_ref[:, 10 * LANES:12 * LANES]) for u in range(tps)]
    for j in range(5):
        for u in range(tps):
            acc = acc_q[u][:, j * 2 * LANES:(j + 1) * 2 * LANES] if j < 4 else acc_k[u]
            sq = acc * acc
            hi = sq.astype(BF16)
            lo = (sq - hi.astype(F32)).astype(BF16)
            ms = (_dot(hi, same_head) + _dot(lo, same_head)) * (1.0 / HEAD_DIM)
            nrm = acc * lax.rsqrt(ms + EPS)
            nw = nw_ref[0:1, :] if j < 4 else nw_ref[1:2, :]
            cos, sin = rope_refs[2 * u][...], rope_refs[2 * u + 1][...]
            for v in range(2):
                ls = slice(v * LANES, (v + 1) * LANES)
                y_ref[u * tm:(u + 1) * tm, (2 * j + v) * LANES:(2 * j + v + 1) * LANES] = (
                    _rope(nrm[:, ls] * nw, cos, sin).astype(BF16))
    for u in range(tps):
        y_ref[u * tm:(u + 1) * tm, 10 * LANES:12 * LANES] = acc_v[u].astype(BF16)


def _inproj_attn(x2, ab, w, nw, cos, sin, tiles_per_batch):
    r, d = x2.shape
    tm, tps = ROW_TILE, TILES_PER_STEP
    n = w.shape[1]
    tpb = tiles_per_batch
    assert (r // tm) % tps == 0
    _, mod_spec, rope_spec = _tile_specs(tm, tpb, tps)
    rope_specs = []
    for u in range(tps):
        rope_specs += [rope_spec(u), rope_spec(u)]
    return pl.pallas_call(
        _inproj_attn_kernel,
        grid=(r // (tm * tps),),
        in_specs=[pl.BlockSpec((tps * tm, d), lambda i: (i, 0))] + [mod_spec(d, u) for u in range(tps)]
                 + [_const_spec(w.shape), _const_spec(nw.shape)] + rope_specs,
        out_specs=pl.BlockSpec((tps * tm, n), lambda i: (i, 0)),
        out_shape=jax.ShapeDtypeStruct((r, n), BF16),
        compiler_params=_cparams("arbitrary"),
        name="inproj_attn",
    )(x2, *([ab] * tps), w, nw, *([cos, sin] * tps))


def _conv_silu(cur_ref, prev_ref, next_ref, cw_ref, prev_on, next_on):
    cur = cur_ref[...].astype(F32)
    n = cur.shape[0]
    row = lax.broadcasted_iota(jnp.int32, cur.shape, 0)
    prev_row = prev_ref[HALO - 1:HALO, :].astype(F32) * prev_on
    next_row = next_ref[0:1, :].astype(F32) * next_on
    xm = jnp.where(row == 0, prev_row, pltpu.roll(cur, 1, 0))
    xp = jnp.where(row == n - 1, next_row, pltpu.roll(cur, n - 1, 0))
    y = cw_ref[3:4, :] + cw_ref[0:1, :] * xm + cw_ref[1:2, :] * cur + cw_ref[2:3, :] * xp
    return y * jax.nn.sigmoid(y)


def _segment_flags(c, nc, ctx_chunks):
    prev_on = jnp.where((c == 0) | (c == ctx_chunks), 0.0, 1.0).astype(F32)
    next_on = jnp.where((c == ctx_chunks - 1) | (c == nc - 1), 0.0, 1.0).astype(F32)
    return prev_on, next_on


def _cumsum_cols(tri_bf, lf):
    hi, mid, lo = _split3(lf)
    return _dot(tri_bf, hi) + _dot(tri_bf, mid) + _dot(tri_bf, lo)


def _cumsum_rows(lf, tri_bf):
    hi, mid, lo = _split3(lf)
    return _dot(hi, tri_bf) + _dot(mid, tri_bf) + _dot(lo, tri_bf)


def _ret_state_update(s_ref, p, k2, v2, kdec, cd_lanes, bd):
    kf = (k2.astype(F32) * kdec).astype(BF16)
    s_ref[p] = s_ref[p] * cd_lanes + jnp.where(bd, _dot_tn(v2, kf), 0.0)


def _mlstm_state_update(c_ref, n_ref, m_ref, k_pairs, v_pairs, c_all, bend, col0, lo, bd):
    cmax = jnp.max(c_all, axis=0, keepdims=True)
    w_all = jnp.exp(c_all - cmax)
    m_old = m_ref[0:1, :]
    mrel = jnp.maximum(m_old, cmax)
    a_row = jnp.exp(m_old - mrel)
    bb_row = jnp.exp(cmax - mrel)
    m_ref[0:1, :] = bend + mrel
    lo_row = lo[0:1, :]
    for p in range(4):
        h0 = col0 + 2 * p
        kw = k_pairs[p] * jnp.where(lo, w_all[:, h0:h0 + 1], w_all[:, h0 + 1:h0 + 2])
        kvt = _dot_tn(v_pairs[p], kw.astype(BF16))
        nloc = jnp.sum(kw, axis=0, keepdims=True)
        a_l = jnp.where(lo_row, a_row[:, h0:h0 + 1], a_row[:, h0 + 1:h0 + 2])
        bb_l = jnp.where(lo_row, bb_row[:, h0:h0 + 1], bb_row[:, h0 + 1:h0 + 2])
        c_ref[p] = c_ref[p] * a_l + jnp.where(bd, kvt, 0.0) * bb_l
        n_new = (n_ref[p, 0:1, :] + n_ref[p, 1:2, :]) * a_l + nloc * bb_l
        n_ref[p, 0:1, :] = jnp.where(lo_row, n_new, 0.0)
        n_ref[p, 1:2, :] = jnp.where(lo_row, 0.0, n_new)


def _bwd_state_kernel(rk_ref, rv_ref, mqk_ref, mqkp_ref, mqkn_ref, mv_ref, g_ref, lgk_ref, cw_ref,
                      sret_ref, cm_ref, nm_ref, mm_ref, qk_ref,
                      s_scr, c_scr, n_scr, m_scr, kdec_scr, *, tpb):
    i = pl.program_id(1)
    jt = jnp.where(i == 0, 0, tpb - i)
    lane = _lane()
    sub = lax.broadcasted_iota(jnp.int32, (CHUNK, LANES), 0)
    lo = lane < HEAD_DIM
    lo_row = lo[0:1, :]
    bd_ret = (sub >= HEAD_DIM) == ((lane & 32) != 0)
    bd_m = (sub >= HEAD_DIM) == (lane >= HEAD_DIM)

    @pl.when(i == 0)
    def _():
        s_scr[...] = jnp.zeros_like(s_scr)
        c_scr[...] = jnp.zeros_like(c_scr)
        n_scr[...] = jnp.zeros_like(n_scr)
        m_scr[...] = jnp.zeros_like(m_scr)
        pos = sub.astype(F32)
        for p in range(4):
            kdec_scr[p] = jnp.exp(lgk_ref[4 + p:5 + p, :] * pos)

    prev_on = jnp.where(jt <= 1, 0.0, 1.0).astype(F32)
    next_on = jnp.where((jt == 0) | (jt == tpb - 1), 0.0, 1.0).astype(F32)
    qk = _conv_silu(mqk_ref, mqkp_ref, mqkn_ref, cw_ref, prev_on, next_on)
    qk_ref[:, 0:4 * LANES] = (qk[:, 0:4 * LANES] * (HEAD_DIM ** -0.5)).astype(BF16)
    qk_ref[:, 4 * LANES:] = qk[:, 4 * LANES:].astype(BF16)
    le_bf = (sub <= lane).astype(BF16)

    order = (1, 0)
    pre = {}
    for blk in order:
        rows = slice(blk * CHUNK, (blk + 1) * CHUNK)
        g = g_ref[rows, :]
        bal = pltpu.roll(_cumsum_cols(le_bf, _log_sigmoid(g)), LANES - 8, 1)
        c_all = g - bal
        cmax = jnp.max(c_all, axis=0, keepdims=True)
        w_all = jnp.exp(c_all - cmax)
        vts, ks, nlocs = [], [], []
        for p in range(8):
            sl = slice((p % 4) * LANES, (p % 4 + 1) * LANES)
            if p < 4:
                v2 = rv_ref[rows, sl]
                ks.append((rk_ref[rows, sl].astype(F32) * kdec_scr[p]).astype(BF16))
            else:
                v2 = mv_ref[rows, sl]
                h0 = 16 + 2 * (p - 4)
                kw = qk[rows, 4 * LANES + (p - 4) * LANES:4 * LANES + (p - 3) * LANES] * jnp.where(
                    lo, w_all[:, h0:h0 + 1], w_all[:, h0 + 1:h0 + 2])
                ks.append(kw.astype(BF16))
                nlocs.append(jnp.sum(kw, axis=0, keepdims=True))
            vts.append(v2.astype(F32).T.astype(BF16))
        pre[blk] = (vts, ks, nlocs, cmax, bal[0:1, :])
    kvs = {blk: [_dot(pre[blk][0][p], pre[blk][1][p]) for p in range(8)] for blk in order}

    for blk in order:
        _, _, nlocs, cmax, bend = pre[blk]
        sret_ref[blk] = s_scr[...].astype(BF16)
        cm_ref[blk] = c_scr[...].astype(BF16)
        nm_ref[blk] = n_scr[...].astype(BF16)
        mm_ref[blk] = m_scr[...]
        m_old = m_scr[0:1, :]
        mrel = jnp.maximum(m_old, cmax)
        a_row = jnp.exp(m_old - mrel)
        bb_row = jnp.exp(cmax - mrel)
        m_scr[0:1, :] = bend + mrel
        for p in range(4):
            cd = jnp.exp(lgk_ref[4 + p:5 + p, :] * float(CHUNK))
            s_scr[p] = s_scr[p] * cd + jnp.where(bd_ret, kvs[blk][p], 0.0)
            h0 = 16 + 2 * p
            a_l = jnp.where(lo_row, a_row[:, h0:h0 + 1], a_row[:, h0 + 1:h0 + 2])
            bb_l = jnp.where(lo_row, bb_row[:, h0:h0 + 1], bb_row[:, h0 + 1:h0 + 2])
            c_scr[p] = c_scr[p] * a_l + jnp.where(bd_m, kvs[blk][4 + p], 0.0) * bb_l
            n_new = (n_scr[p, 0:1, :] + n_scr[p, 1:2, :]) * a_l + nlocs[p] * bb_l
            n_scr[p, 0:1, :] = jnp.where(lo_row, n_new, 0.0)
            n_scr[p, 1:2, :] = jnp.where(lo_row, 0.0, n_new)


def _bwd_states(y, g, lgk, cw, nb, tpb):
    gw = 4 * LANES
    tm = ROW_TILE
    cpt = tm // CHUNK
    tile = lambda b, i: b * tpb + jnp.where(i == 0, 0, tpb - i)
    hb = tm // HALO
    nhalo = y.shape[0] // HALO
    blk = lambda j: pl.BlockSpec((tm, gw), lambda b, i: (tile(b, i), j))
    state = lambda *dims: pl.BlockSpec((cpt,) + dims, lambda b, i: (tile(b, i),) + (0,) * len(dims))
    nchunks = nb * tpb * cpt
    return pl.pallas_call(
        functools.partial(_bwd_state_kernel, tpb=tpb),
        grid=(nb, tpb),
        in_specs=[blk(1), blk(2),
                  pl.BlockSpec((tm, 2 * gw), lambda b, i: (tile(b, i), 2)),
                  pl.BlockSpec((HALO, 2 * gw), lambda b, i: (jnp.maximum(tile(b, i) * hb - 1, 0), 2)),
                  pl.BlockSpec((HALO, 2 * gw), lambda b, i: (jnp.minimum((tile(b, i) + 1) * hb, nhalo - 1), 2)),
                  blk(6),
                  pl.BlockSpec((tm, LANES), lambda b, i: (tile(b, i), 0)),
                  _const_spec(lgk.shape), _const_spec(cw.shape)],
        out_specs=[state(4, CHUNK, LANES), state(4, CHUNK, LANES), state(4, HALO, LANES), state(8, LANES),
                   pl.BlockSpec((tm, 2 * gw), lambda b, i: (tile(b, i), 0))],
        out_shape=[jax.ShapeDtypeStruct((nchunks, 4, CHUNK, LANES), BF16),
                   jax.ShapeDtypeStruct((nchunks, 4, CHUNK, LANES), BF16),
                   jax.ShapeDtypeStruct((nchunks, 4, HALO, LANES), BF16),
                   jax.ShapeDtypeStruct((nchunks, 8, LANES), F32),
                   jax.ShapeDtypeStruct((y.shape[0], 2 * gw), BF16)],
        scratch_shapes=[pltpu.VMEM((4, CHUNK, LANES), F32), pltpu.VMEM((4, CHUNK, LANES), F32),
                        pltpu.VMEM((4, HALO, LANES), F32), pltpu.VMEM((8, LANES), F32),
                        pltpu.VMEM((4, CHUNK, LANES), F32)],
        compiler_params=_cparams("arbitrary", "arbitrary"),
        name="bwd_state_sweep",
    )(y, y, y, y, y, y, g, lgk, cw)


def _mlstm_dir_weights(st, qn_row, c_col, bt_row, m_prev, tri):
    dl = jnp.where(tri, c_col + bt_row, NEG)
    mx = jnp.max(dl, axis=0, keepdims=True)
    al = bt_row + m_prev
    m_t = jnp.maximum(al, mx)
    w = jnp.exp(dl - m_t)
    a_t = jnp.exp(al - m_t)
    sw = st * w
    den = jnp.sum(sw, axis=0, keepdims=True) + a_t * qn_row
    r = 1.0 / jnp.maximum(jnp.abs(den), jnp.exp(-m_t))
    return sw * r, a_t * r


def _heads_out(ht, nw_tab):
    rows = []
    for a in range(2):
        ha = ht[a * HEAD_DIM:(a + 1) * HEAD_DIM, :]
        ms = jnp.mean(ha * ha, axis=0, keepdims=True)
        rows.append(ha * lax.rsqrt(ms + EPS))
    return (jnp.concatenate(rows, axis=0) * nw_tab).T


def _mixer_kernel(lg_ref, rq_ref, rk_ref, rv_ref, rg_ref, mqk_ref, mqkp_ref, mqkn_ref, mv_ref, mo_ref,
                  g_ref, gt_ref, sretb_ref, cmb_ref, nmb_ref, mmb_ref,
                  lgk_ref, lgkt_ref, cw_ref, nw_ref,
                  out_ref,
                  s_scr, c_scr, n_scr, m_scr, dm_scr, dec_scr, *, nc, ctx_chunks):
    c = pl.program_id(1)
    lane = _lane()
    sub = lax.broadcasted_iota(jnp.int32, (CHUNK, LANES), 0)
    lo = lane < HEAD_DIM
    sub_lo = sub < HEAD_DIM
    mask_ret = [(lane & 32) == 0, (lane & 32) != 0]
    mask_nat = [lo, lane >= HEAD_DIM]
    bd_ret = (sub >= HEAD_DIM) == ((lane & 32) != 0)
    bd_m = (sub >= HEAD_DIM) == (lane >= HEAD_DIM)
    le = sub <= lane
    ge = sub >= lane

    @pl.when(c == 0)
    def _():
        s_scr[...] = jnp.zeros_like(s_scr)
        c_scr[...] = jnp.zeros_like(c_scr)
        n_scr[...] = jnp.zeros_like(n_scr)
        m_scr[...] = jnp.zeros_like(m_scr)
        spos = sub.astype(F32)
        tpos = lane.astype(F32)
        diff = (lane - sub).astype(F32)
        for h in range(8):
            dm_scr[h] = (jnp.where(le, jnp.exp(lg_ref[h] * diff), 0.0)
                         + jnp.where(ge, jnp.exp(lg_ref[8 + h] * (-diff)), 0.0))
        for p in range(4):
            dec_scr[0, p] = jnp.exp(lgkt_ref[:, p:p + 1] * (tpos + 1.0))
            dec_scr[1, p] = jnp.exp(lgkt_ref[:, 4 + p:5 + p] * (float(CHUNK) - tpos))
            dec_scr[2, p] = jnp.exp(lgk_ref[p:p + 1, :] * (float(CHUNK) - 1.0 - spos))


    prev_on, next_on = _segment_flags(c, nc, ctx_chunks)
    qk = _conv_silu(mqk_ref, mqkp_ref, mqkn_ref, cw_ref, prev_on, next_on)
    g = g_ref[...]
    gt = gt_ref[...]
    lf_col = _log_sigmoid(g)
    lf_row = _log_sigmoid(gt)
    le_bf = le.astype(BF16)
    ge_bf = ge.astype(BF16)
    bal_f = pltpu.roll(_cumsum_cols(ge_bf, lf_col), LANES - 8, 1)
    bal_b = pltpu.roll(_cumsum_cols(le_bf, lf_col), LANES - 8, 1)
    cf_all = g - bal_f
    cb_all = g - bal_b
    bf_row = _cumsum_rows(lf_row, le_bf)
    bb_row = _cumsum_rows(lf_row, ge_bf)

    qb, kb, vb, kf32, qt, vbd = [], [], [], [], [], []
    for p in range(8):
        sl = slice((p % 4) * LANES, (p % 4 + 1) * LANES)
        if p < 4:
            q2, k2, v2 = rq_ref[:, sl], rk_ref[:, sl], rv_ref[:, sl]
            qf = q2.astype(F32)
            kf = None
            old =jnp.concatenate([s_scr[p].astype(BF16), sretb_ref[p]], axis=1)
        else:
            qf = qk[:, sl] * (HEAD_DIM ** -0.5)
            kf = qk[:, 4 * LANES + (p - 4) * LANES:4 * LANES + (p - 3) * LANES]
            q2, k2, v2 = qf.astype(BF16), kf.astype(BF16), mv_ref[:, sl]
            old = jnp.concatenate([c_scr[p - 4].astype(BF16), cmb_ref[p - 4]], axis=1)
        vt = v2.astype(F32).T.astype(BF16)
        qb.append(q2)
        kb.append(k2)
        vb.append(v2)
        kf32.append(kf)
        qt.append(qf.T)
        vbd.append(jnp.concatenate([jnp.where(sub_lo, vt, jnp.zeros_like(vt)),
                                    jnp.where(sub_lo, jnp.zeros_like(vt), vt), old], axis=1))

    st2, qn_f, qn_b = [], [], []
    for p in range(8):
        masks = mask_ret if p < 4 else mask_nat
        zero = jnp.zeros_like(qb[p])
        qstack = jnp.concatenate([jnp.where(masks[0], qb[p], zero), jnp.where(masks[1], qb[p], zero)], axis=0)
        st2.append(_dot_nt(kb[p], qstack))
        if p >= 4:
            qn_f.append(_dot_nt(n_scr[p - 4].astype(BF16), qb[p]))
            qn_b.append(_dot_nt(nmb_ref[p - 4], qb[p]))

    rhs = []
    for p in range(8):
        if p < 4:
            pts = [(st2[p][:, a * LANES:(a + 1) * LANES] * dm_scr[2 * p + a]).astype(BF16) for a in range(2)]
            x_f, x_b = dec_scr[0, p], dec_scr[1, p]
        else:
            pts, cf, cb = [], [], []
            for a in range(2):
                h = 2 * (p - 4) + a
                st = st2[p][:, a * LANES:(a + 1) * LANES]
                pf, coef_f = _mlstm_dir_weights(st, qn_f[p - 4][a:a + 1, :], cf_all[:, h:h + 1],
                                                bf_row[8 + h:9 + h, :], m_scr[0:1, h:h + 1], le)
                pb, coef_b = _mlstm_dir_weights(st, qn_b[p - 4][a:a + 1, :], cb_all[:, 16 + h:17 + h],
                                                bb_row[24 + h:25 + h, :], mmb_ref[0:1, 16 + h:17 + h], ge)
                pts.append((pf + pb).astype(BF16))
                cf.append(coef_f)
                cb.append(coef_b)
            x_f = jnp.where(sub_lo, cf[0], cf[1])
            x_b = jnp.where(sub_lo, cb[0], cb[1])
        rhs.append(jnp.concatenate(pts + [(qt[p] * x_f).astype(BF16), (qt[p] * x_b).astype(BF16)], axis=0))

    ht = [_dot(vbd[p], rhs[p]) for p in range(8)]

    for p in range(8):
        sl = slice((p % 4) * LANES, (p % 4 + 1) * LANES)
        gate_ref = rg_ref if p < 4 else mo_ref
        y = _heads_out(ht[p], nw_ref[p])
        out_ref[:, p * LANES:(p + 1) * LANES] = (y * gate_ref[:, sl].astype(F32)).astype(BF16)

    for p in range(4):
        cd = jnp.exp(lgk_ref[p:p + 1, :] * float(CHUNK))
        _ret_state_update(s_scr, p, kb[p], vb[p], dec_scr[2, p], cd, bd_ret)
    _mlstm_state_update(c_scr, n_scr, m_scr, kf32[4:], vb[4:], cf_all, bal_f[CHUNK - 1:CHUNK, :], 0, lo, bd_m)


def _mixer(y, g, gt, states, lg_smem, lgk, lgkt, cw_qk, nw, nb, nc, ctx_chunks):
    gw = 4 * LANES
    sretb, cmb, nmb, mmb = states
    rc = lambda b, c: b * nc + c
    hb = CHUNK // HALO
    nhalo = y.shape[0] // HALO
    blk = lambda j: pl.BlockSpec((CHUNK, gw), lambda b, c: (rc(b, c), j))
    return pl.pallas_call(
        functools.partial(_mixer_kernel, nc=nc, ctx_chunks=ctx_chunks),
        grid=(nb, nc),
        in_specs=[_smem_spec(),
                  blk(0), blk(1), blk(2), blk(3),
                  pl.BlockSpec((CHUNK, 2 * gw), lambda b, c: (rc(b, c), 2)),
                  pl.BlockSpec((HALO, 2 * gw), lambda b, c: (jnp.maximum(rc(b, c) * hb - 1, 0), 2)),
                  pl.BlockSpec((HALO, 2 * gw), lambda b, c: (jnp.minimum((rc(b, c) + 1) * hb, nhalo - 1), 2)),
                  blk(6), blk(7),
                  pl.BlockSpec((CHUNK, LANES), lambda b, c: (rc(b, c), 0)),
                  pl.BlockSpec((32, CHUNK), lambda b, c: (0, rc(b, c))),
                  pl.BlockSpec((None, 4, CHUNK, LANES), lambda b, c: (rc(b, c), 0, 0, 0)),
                  pl.BlockSpec((None, 4, CHUNK, LANES), lambda b, c: (rc(b, c), 0, 0, 0)),
                  pl.BlockSpec((None, 4, HALO, LANES), lambda b, c: (rc(b, c), 0, 0, 0)),
                  pl.BlockSpec((None, 8, LANES), lambda b, c: (rc(b, c), 0, 0)),
                  _const_spec(lgk.shape), _const_spec(lgkt.shape), _const_spec(cw_qk.shape),
                  _const_spec(nw.shape)],
        out_specs=pl.BlockSpec((CHUNK, 2 * gw), lambda b, c: (rc(b, c), 0)),
        out_shape=jax.ShapeDtypeStruct((nb * nc * CHUNK, 2 * gw), BF16),
        scratch_shapes=[pltpu.VMEM((4, CHUNK, LANES), F32), pltpu.VMEM((4, CHUNK, LANES), F32),
                        pltpu.VMEM((4, HALO, LANES), F32), pltpu.VMEM((8, LANES), F32),
                        pltpu.VMEM((8, CHUNK, LANES), F32), pltpu.VMEM((3, 4, CHUNK, LANES), F32)],
        compiler_params=_cparams("arbitrary", "arbitrary"),
        name="ret_mlstm_mixer",
    )(lg_smem, y, y, y, y, y, y, y, y, y, g, gt, sretb, cmb, nmb, mmb, lgk, lgkt, cw_qk, nw)


def _inproj_sweep_kernel(ctx_ref, x_ref, ab_ref, w_ref, wg_ref, wgt_ref, gb_ref, gbt_ref, cos_ref, sin_ref,
                         lgk_ref, cw_ref,
                         y_ref, g_ref, gt_ref, sret_ref, cm_ref, nm_ref, mm_ref, qk_ref,
                         s_scr, c_scr, n_scr, m_scr, kdec_scr,
                         p_rk, p_rv, p_mv, p_mqk, p_g, next_row_scr, *, tiles, tpb):
    i = pl.program_id(0)
    order = lambda t: jnp.where(t % tpb == 0, 0, tpb - t % tpb)
    jt_a = order(jnp.maximum(i - 1, 0))
    lane = _lane()
    sub = lax.broadcasted_iota(jnp.int32, (CHUNK, LANES), 0)
    lo = lane < HEAD_DIM
    lo_row = lo[0:1, :]
    bd_ret = (sub >= HEAD_DIM) == ((lane & 32) != 0)
    bd_m = (sub >= HEAD_DIM) == (lane >= HEAD_DIM)
    gw = 4 * LANES

    @pl.when(i == 0)
    def _():
        pos = sub.astype(F32)
        for p in range(4):
            kdec_scr[p] = jnp.exp(lgk_ref[4 + p:5 + p, :] * pos)
        for ref in (p_rk, p_rv, p_mv, p_mqk, p_g, next_row_scr):
            ref[...] = jnp.zeros_like(ref)

    @pl.when(jt_a == 0)
    def _():
        s_scr[...] = jnp.zeros_like(s_scr)
        c_scr[...] = jnp.zeros_like(c_scr)
        n_scr[...] = jnp.zeros_like(n_scr)
        m_scr[...] = jnp.zeros_like(m_scr)

    tile_i = jnp.minimum(i, tiles - 1)
    jt_i = order(tile_i)
    hb = _norm_mod(jnp.where(jt_i == 0, ctx_ref[...], x_ref[...]), ab_ref)
    cos, sin = cos_ref[...], sin_ref[...]

    def project(j):
        acc = _dot(hb, w_ref[:, j * gw:(j + 1) * gw])
        if j in (0, 1):
            if j == 0:
                acc = acc * (HEAD_DIM ** -0.5)
            acc = jnp.concatenate([_rope(acc[:, p * LANES:(p + 1) * LANES], cos, sin) for p in range(4)], axis=1)
        elif j == 3:
            acc = acc * jax.nn.sigmoid(acc)
        elif j == 7:
            acc = jax.nn.sigmoid(acc)
        return acc.astype(BF16)

    raw_q, raw_k = project(4), project(5)

    prev_on = jnp.where(jt_a <= 1, 0.0, 1.0).astype(F32)
    next_on = jnp.where((jt_a == 0) | (jt_a == tpb - 1), 0.0, 1.0).astype(F32)
    cur = p_mqk[...].astype(F32)
    n = cur.shape[0]
    row = lax.broadcasted_iota(jnp.int32, cur.shape, 0)
    prev_row = jnp.concatenate([raw_q[n - HALO:, :], raw_k[n - HALO:, :]], axis=1)[HALO - 1:HALO, :].astype(F32)
    xm = jnp.where(row == 0, prev_row * prev_on, pltpu.roll(cur, 1, 0))
    xp = jnp.where(row == n - 1, next_row_scr[0:1, :] * next_on, pltpu.roll(cur, n - 1, 0))
    conv = cw_ref[3:4, :] + cw_ref[0:1, :] * xm + cw_ref[1:2, :] * cur + cw_ref[2:3, :] * xp
    qk = conv * jax.nn.sigmoid(conv)
    qk_ref[:, 0:gw] = (qk[:, 0:gw] * (HEAD_DIM ** -0.5)).astype(BF16)
    qk_ref[:, gw:] = qk[:, gw:].astype(BF16)
    le_bf = (sub <= lane).astype(BF16)
    chunks = (1, 0)
    pre = {}
    for blk in chunks:
        rows = slice(blk * CHUNK, (blk + 1) * CHUNK)
        g = p_g[rows, :]
        bal = pltpu.roll(_cumsum_cols(le_bf, _log_sigmoid(g)), LANES - 8, 1)
        c_all = g - bal
        cmax = jnp.max(c_all, axis=0, keepdims=True)
        w_all = jnp.exp(c_all - cmax)
        vts, ks, nlocs = [], [], []
        for p in range(8):
            sl = slice((p % 4) * LANES, (p % 4 + 1) * LANES)
            if p < 4:
                v2 = p_rv[rows, sl]
                ks.append((p_rk[rows, sl].astype(F32) * kdec_scr[p]).astype(BF16))
            else:
                v2 = p_mv[rows, sl]
                h0 = 16 + 2 * (p - 4)
                kw = qk[rows, gw + (p - 4) * LANES:gw + (p - 3) * LANES] * jnp.where(
                    lo, w_all[:, h0:h0 + 1], w_all[:, h0 + 1:h0 + 2])
                ks.append(kw.astype(BF16))
                nlocs.append(jnp.sum(kw, axis=0, keepdims=True))
            vts.append(v2.astype(F32).T.astype(BF16))
        pre[blk] = (vts, ks, nlocs, cmax, bal[0:1, :])

    cur_rk = project(1)
    y_ref[:, 1 * gw:2 * gw] = cur_rk
    y_ref[:, 0:gw] = project(0)
    kvs = {blk: [_dot(pre[blk][0][p], pre[blk][1][p]) for p in range(8)] for blk in chunks}
    cur_rv = project(2)
    y_ref[:, 2 * gw:3 * gw] = cur_rv
    y_ref[:, 3 * gw:4 * gw] = project(3)

    for blk in chunks:
        _, _, nlocs, cmax, bend = pre[blk]
        sret_ref[blk] = s_scr[...].astype(BF16)
        cm_ref[blk] = c_scr[...].astype(BF16)
        nm_ref[blk] = n_scr[...].astype(BF16)
        mm_ref[blk] = m_scr[...]
        m_old = m_scr[0:1, :]
        mrel = jnp.maximum(m_old, cmax)
        a_row = jnp.exp(m_old - mrel)
        bb_row = jnp.exp(cmax - mrel)
        m_scr[0:1, :] = bend + mrel
        for p in range(4):
            cd = jnp.exp(lgk_ref[4 + p:5 + p, :] * float(CHUNK))
            s_scr[p] = s_scr[p] * cd + jnp.where(bd_ret, kvs[blk][p], 0.0)
            h0 = 16 + 2 * p
            a_l = jnp.where(lo_row, a_row[:, h0:h0 + 1], a_row[:, h0 + 1:h0 + 2])
            bb_l = jnp.where(lo_row, bb_row[:, h0:h0 + 1], bb_row[:, h0 + 1:h0 + 2])
            c_scr[p] = c_scr[p] * a_l + jnp.where(bd_m, kvs[blk][4 + p], 0.0) * bb_l
            n_new = (n_scr[p, 0:1, :] + n_scr[p, 1:2, :]) * a_l + nlocs[p] * bb_l
            n_scr[p, 0:1, :] = jnp.where(lo_row, n_new, 0.0)
            n_scr[p, 1:2, :] = jnp.where(lo_row, 0.0, n_new)

    cur_mv = project(6)
    y_ref[:, 4 * gw:5 * gw] = cur_mv
    y_ref[:, 5 * gw:6 * gw] = project(7)
    gates = _dot(hb, wg_ref[...]) + gb_ref[...]
    g_ref[...] = gates
    gt_ref[...] = _dot_nt(wgt_ref[...], hb) + gbt_ref[...]
    next_row_scr[...] = p_mqk[0:HALO, :].astype(F32)
    p_mqk[:, 0:gw] = raw_q
    p_mqk[:, gw:] = raw_k
    p_rk[...] = cur_rk
    p_rv[...] = cur_rv
    p_mv[...] = cur_mv
    p_g[...] = gates


def _inproj_sweep(ctx2, x2, ab, w, wg, wgt, gb, gbt, cos, sin, lgk, cw, nb, tpb):
    d = x2.shape[1]
    tm = ROW_TILE
    cpt = tm // CHUNK
    gw = 4 * LANES
    tiles = nb * tpb
    r = tiles * tm
    lat = tpb - 1
    seq_tile = lambda t: (t // tpb) * tpb + jnp.where(t % tpb == 0, 0, tpb - t % tpb)
    cur = lambda i: jnp.minimum(i, tiles - 1)
    tile_i = lambda i: seq_tile(cur(i))
    tile_a = lambda i: seq_tile(jnp.maximum(i - 1, 0))
    in_batch = lambda i: tile_i(i) % tpb
    sel = lambda i: ((tile_i(i) // tpb) * 2 + jnp.minimum(in_batch(i), 1), 0, 0)
    state = lambda *dims: pl.BlockSpec((cpt,) + dims, lambda i: (tile_a(i),) + (0,) * len(dims))
    nchunks = tiles * cpt
    return pl.pallas_call(
        functools.partial(_inproj_sweep_kernel, tiles=tiles, tpb=tpb),
        grid=(tiles + 1,),
        in_specs=[pl.BlockSpec((tm, d), lambda i: (tile_i(i) // tpb, 0)),
                  pl.BlockSpec((tm, d), lambda i: ((tile_i(i) // tpb) * lat + jnp.maximum(in_batch(i) - 1, 0), 0)),
                  pl.BlockSpec((None, 2, d), sel),
                  _const_spec(w.shape), _const_spec(wg.shape), _const_spec(wgt.shape),
                  _const_spec(gb.shape), _const_spec(gbt.shape),
                  pl.BlockSpec((tm, LANES), lambda i: (in_batch(i), 0)),
                  pl.BlockSpec((tm, LANES), lambda i: (in_batch(i), 0)),
                  _const_spec(lgk.shape), _const_spec(cw.shape)],
        out_specs=[pl.BlockSpec((tm, 6 * gw), lambda i: (tile_i(i), 0)),
                   pl.BlockSpec((tm, LANES), lambda i: (tile_i(i), 0)),
                   pl.BlockSpec((32, tm), lambda i: (0, tile_i(i))),
                   state(4, CHUNK, LANES), state(4, CHUNK, LANES), state(4, HALO, LANES), state(8, LANES),
                   pl.BlockSpec((tm, 2 * gw), lambda i: (tile_a(i), 0))],
        out_shape=[jax.ShapeDtypeStruct((r, 6 * gw), BF16),
                   jax.ShapeDtypeStruct((r, LANES), F32),
                   jax.ShapeDtypeStruct((32, r), F32),
                   jax.ShapeDtypeStruct((nchunks, 4, CHUNK, LANES), BF16),
                   jax.ShapeDtypeStruct((nchunks, 4, CHUNK, LANES), BF16),
                   jax.ShapeDtypeStruct((nchunks, 4, HALO, LANES), BF16),
                   jax.ShapeDtypeStruct((nchunks, 8, LANES), F32),
                   jax.ShapeDtypeStruct((r, 2 * gw), BF16)],
        scratch_shapes=[pltpu.VMEM((4, CHUNK, LANES), F32), pltpu.VMEM((4, CHUNK, LANES), F32),
                        pltpu.VMEM((4, HALO, LANES), F32), pltpu.VMEM((8, LANES), F32),
                        pltpu.VMEM((4, CHUNK, LANES), F32),
                        pltpu.VMEM((tm, gw), BF16), pltpu.VMEM((tm, gw), BF16), pltpu.VMEM((tm, gw), BF16),
                        pltpu.VMEM((tm, 2 * gw), BF16), pltpu.VMEM((tm, LANES), F32),
                        pltpu.VMEM((HALO, 2 * gw), F32)],
        compiler_params=_cparams("arbitrary"),
        name="inproj_bwd_sweep",
    )(ctx2, x2, ab, w, wg, wgt, gb, gbt, cos, sin, lgk, cw)


class _Bag:
    def __init__(self, **kw):
        self.__dict__.update(kw)


def _mixer_chunk_stages(blk, r):
    rows = slice(blk * CHUNK, (blk + 1) * CHUNK)
    lane = _lane()
    sub = lax.broadcasted_iota(jnp.int32, (CHUNK, LANES), 0)
    lo = lane < HEAD_DIM
    sub_lo = sub < HEAD_DIM
    mask_ret = [(lane & 32) == 0, (lane & 32) != 0]
    mask_nat = [lo, lane >= HEAD_DIM]
    bd_ret = (sub >= HEAD_DIM) == ((lane & 32) != 0)
    bd_m = (sub >= HEAD_DIM) == (lane >= HEAD_DIM)
    le = sub <= lane
    ge = sub >= lane

    g = r.g_ref[rows, :]
    gt = r.gt_ref[:, rows]
    lf_col = _log_sigmoid(g)
    lf_row = _log_sigmoid(gt)
    le_bf = le.astype(BF16)
    ge_bf = ge.astype(BF16)
    bal_f = pltpu.roll(_cumsum_cols(ge_bf, lf_col), LANES - 8, 1)
    bal_b = pltpu.roll(_cumsum_cols(le_bf, lf_col), LANES - 8, 1)
    cf_all = g - bal_f
    cb_all = g - bal_b
    bf_row = _cumsum_rows(lf_row, le_bf)
    bb_row = _cumsum_rows(lf_row, ge_bf)
    qb, kb, vb, kf32, qt, vbd = [], [], [], [], [], []
    for p in range(8):
        sl = slice((p % 4) * LANES, (p % 4 + 1) * LANES)
        if p < 4:
            q2, k2, v2 = r.rq_ref[rows, sl], r.rk_ref[rows, sl], r.rv_ref[rows, sl]
            qf = q2.astype(F32)
            kf = None
        else:
            q2 = r.qk_ref[rows, sl]
            k2 = r.qk_ref[rows, 4 * LANES + (p - 4) * LANES:4 * LANES + (p - 3) * LANES]
            v2 = r.mv_ref[rows, sl]
            qf, kf = q2.astype(F32), k2.astype(F32)
        vt = v2.astype(F32).T.astype(BF16)
        qb.append(q2)
        kb.append(k2)
        vb.append(v2)
        kf32.append(kf)
        qt.append(qf.T)
        vbd.append([jnp.where(sub_lo, vt, jnp.zeros_like(vt)), jnp.where(sub_lo, jnp.zeros_like(vt), vt)])
    yield

    st2 = []
    for p in range(8):
        masks = mask_ret if p < 4 else mask_nat
        zero = jnp.zeros_like(qb[p])
        qstack = jnp.concatenate([jnp.where(masks[0], qb[p], zero), jnp.where(masks[1], qb[p], zero)], axis=0)
        st2.append(_dot_nt(kb[p], qstack))
    yield

    qn_f = [_dot_nt(r.n_scr[p].astype(BF16), qb[4 + p]) for p in range(4)]
    qn_b = [_dot_nt(r.nmb_ref[blk, p], qb[4 + p]) for p in range(4)]
    lhs, rhs = [], []
    for p in range(8):
        if p < 4:
            pts = [(st2[p][:, a * LANES:(a + 1) * LANES] * r.dm_scr[2 * p + a]).astype(BF16) for a in range(2)]
            x_f, x_b = r.dec_scr[0, p], r.dec_scr[1, p]
            old = [r.s_scr[p].astype(BF16), r.sretb_ref[blk, p]]
        else:
            pts, cf, cb = [], [], []
            for a in range(2):
                h = 2 * (p - 4) + a
                st = st2[p][:, a * LANES:(a + 1) * LANES]
                pf, coef_f = _mlstm_dir_weights(st, qn_f[p - 4][a:a + 1, :], cf_all[:, h:h + 1],
                                                bf_row[8 + h:9 + h, :], r.m_state[0:1, h:h + 1], le)
                pb, coef_b = _mlstm_dir_weights(st, qn_b[p - 4][a:a + 1, :], cb_all[:, 16 + h:17 + h],
                                                bb_row[24 + h:25 + h, :], r.mmb_ref[blk, 0:1, 16 + h:17 + h], ge)
                pts.append((pf + pb).astype(BF16))
                cf.append(coef_f)
                cb.append(coef_b)
            x_f = jnp.where(sub_lo, cf[0], cf[1])
            x_b = jnp.where(sub_lo, cb[0], cb[1])
            old = [r.c_scr[p - 4].astype(BF16), r.cmb_ref[blk, p - 4]]
        lhs.append(jnp.concatenate(vbd[p] + old, axis=1))
        rhs.append(jnp.concatenate(pts + [(qt[p] * x_f).astype(BF16), (qt[p] * x_b).astype(BF16)], axis=0))
    yield

    ht = [_dot(lhs[p], rhs[p]) for p in range(8)]
    yield

    for p in range(8):
        sl = slice((p % 4) * LANES, (p % 4 + 1) * LANES)
        gate_ref = r.rg_ref if p < 4 else r.mo_ref
        y = _heads_out(ht[p], r.nw_ref[p])
        r.mix_scr[rows, p * LANES:(p + 1) * LANES] = (y * gate_ref[rows, sl].astype(F32)).astype(BF16)
    yield

    for p in range(4):
        cd = jnp.exp(r.lgk_ref[p:p + 1, :] * float(CHUNK))
        _ret_state_update(r.s_scr, p, kb[p], vb[p], r.dec_scr[2, p], cd, bd_ret)
    _mlstm_state_update(r.c_scr, r.n_scr, r.m_state, kf32[4:], vb[4:], cf_all, bal_f[CHUNK - 1:CHUNK, :],
                        0, lo, bd_m)
    yield


def _ffn_splits(d_ff, pieces):
    blocks = d_ff // (2 * LANES)
    assert blocks * 2 * LANES == d_ff and blocks >= pieces
    cuts = [((i * blocks) // pieces) * 2 * LANES for i in range(pieces)]
    return cuts + [d_ff]


def _mixer_ffn_kernel(lg_ref, rq_ref, rk_ref, rv_ref, rg_ref, qk_ref, mv_ref, mo_ref,
                      g_ref, gt_ref, sretb_ref, cmb_ref, nmb_ref, mmb_ref,
                      lgk_ref, lgkt_ref, nw_ref,
                      ctx_ref, x_ref, p_ref, wo_ref, wi_ref, w2_ref,
                      o_ref,
                      s_scr, c_scr, n_scr, m_state, dm_scr, dec_scr, mix_scr, act_scr, *, tiles, tpb, d_ff):
    s = pl.program_id(0)
    jt = jnp.minimum(s, tiles - 1) % tpb
    lane = _lane()
    sub = lax.broadcasted_iota(jnp.int32, (CHUNK, LANES), 0)

    @pl.when(s == 0)
    def _():
        mix_scr[...] = jnp.zeros_like(mix_scr)
        le = sub <= lane
        ge = sub >= lane
        spos = sub.astype(F32)
        tpos = lane.astype(F32)
        diff = (lane - sub).astype(F32)
        for h in range(8):
            dm_scr[h] = (jnp.where(le, jnp.exp(lg_ref[h] * diff), 0.0)
                         + jnp.where(ge, jnp.exp(lg_ref[8 + h] * (-diff)), 0.0))
        for p in range(4):
            dec_scr[0, p] = jnp.exp(lgkt_ref[:, p:p + 1] * (tpos + 1.0))
            dec_scr[1, p] = jnp.exp(lgkt_ref[:, 4 + p:5 + p] * (float(CHUNK) - tpos))
            dec_scr[2, p] = jnp.exp(lgk_ref[p:p + 1, :] * (float(CHUNK) - 1.0 - spos))

    @pl.when(jt == 0)
    def _():
        s_scr[...] = jnp.zeros_like(s_scr)
        c_scr[...] = jnp.zeros_like(c_scr)
        n_scr[...] = jnp.zeros_like(n_scr)
        m_state[...] = jnp.zeros_like(m_state)

    r = _Bag(rq_ref=rq_ref, rk_ref=rk_ref, rv_ref=rv_ref, rg_ref=rg_ref, qk_ref=qk_ref, mv_ref=mv_ref, mo_ref=mo_ref,
             g_ref=g_ref, gt_ref=gt_ref, sretb_ref=sretb_ref, cmb_ref=cmb_ref, nmb_ref=nmb_ref, mmb_ref=mmb_ref,
             lgk_ref=lgk_ref, nw_ref=nw_ref, s_scr=s_scr, c_scr=c_scr, n_scr=n_scr, m_state=m_state,
             dm_scr=dm_scr, dec_scr=dec_scr, mix_scr=mix_scr)
    cuts = _ffn_splits(d_ff, 4)
    ffn_piece = lambda i: _ffn_cols(h, wi_ref, act_scr, cuts[i], cuts[i + 1], d_ff)

    x = _ctx_or_latent_rows(ctx_ref, x_ref, jnp.maximum(s - 1, 0), tpb)
    x1, h = _ffn_in(x, mix_scr[...], p_ref, wo_ref)

    chunk_a, chunk_b = _mixer_chunk_stages(0, r), _mixer_chunk_stages(1, r)
    next(chunk_a), next(chunk_b)
    next(chunk_a), next(chunk_b)
    ffn_piece(0)
    next(chunk_a), next(chunk_a)
    ffn_piece(1)
    next(chunk_a), next(chunk_a)
    ffn_piece(2)
    next(chunk_b), next(chunk_b)
    ffn_piece(3)
    next(chunk_b), next(chunk_b)
    o_ref[...] = x1 + p_ref[3:4, :] * _dot(act_scr[...], w2_ref[...])


def _mixer_ffn(y, g, gt, states, lg_smem, lgk, lgkt, nw, xs, prm, wo, wi_all, w2_all, layer, nb, tpb):
    gw = 4 * LANES
    tm = ROW_TILE
    cpt = tm // CHUNK
    sretb, cmb, nmb, mmb, qk_act = states
    tiles = nb * tpb
    d = xs[-1].shape[1]
    d_ff = w2_all.shape[1]
    mix = lambda s: jnp.minimum(s, tiles - 1)
    ffn = lambda s: jnp.maximum(s - 1, 0)
    blk =lambda j: pl.BlockSpec((tm, gw), lambda s: (mix(s), j))
    state_spec = lambda a: pl.BlockSpec((cpt,) + a.shape[1:], lambda s: (mix(s),) + (0,) * (a.ndim - 1))
    sel = lambda t: ((t // tpb) * 2 + jnp.minimum(t % tpb, 1), 0, 0)
    layer_spec = lambda a: pl.BlockSpec((None,) + a.shape[1:], lambda s: (layer, 0, 0),
                                        pipeline_mode=pl.Buffered(1))
    return pl.pallas_call(
        functools.partial(_mixer_ffn_kernel, tiles=tiles, tpb=tpb, d_ff=d_ff),
        grid=(tiles + 1,),
        in_specs=[_smem_spec(),
                  blk(0), blk(1), blk(2), blk(3),
                  pl.BlockSpec((tm, 2 * gw), lambda s: (mix(s), 0)),
                  blk(4), blk(5),
                  pl.BlockSpec((tm, LANES), lambda s: (mix(s), 0)),
                  pl.BlockSpec((32, tm), lambda s: (0, mix(s))),
                  state_spec(sretb), state_spec(cmb), state_spec(nmb), state_spec(mmb),
                  _const_spec(lgk.shape), _const_spec(lgkt.shape), _const_spec(nw.shape)]
                 + _split_row_specs(tm, d, tpb, ffn)
                 + [pl.BlockSpec((None, 8, d), lambda s: sel(ffn(s))),
                    _const_spec(wo.shape), layer_spec(wi_all), layer_spec(w2_all)],
        out_specs=pl.BlockSpec((tm, d), lambda s: (ffn(s), 0)),
        out_shape=jax.ShapeDtypeStruct((tiles * tm, d), F32),
        scratch_shapes=[pltpu.VMEM((4, CHUNK, LANES), F32), pltpu.VMEM((4, CHUNK, LANES), F32),
                        pltpu.VMEM((4, HALO, LANES), F32), pltpu.VMEM((8, LANES), F32),
                        pltpu.VMEM((8, CHUNK, LANES), F32), pltpu.VMEM((3, 4, CHUNK, LANES), F32),
                        pltpu.VMEM((tm, 2 * gw), BF16), pltpu.VMEM((tm, d_ff), BF16)],
        compiler_params=_cparams("arbitrary"),
        name="ret_mlstm_mixer_ffn",
    )(lg_smem, y, y, y, y, qk_act, y, y, g, gt, sretb, cmb, nmb, mmb, lgk, lgkt, nw,
      *xs, prm, wo, wi_all, w2_all)


def _attn_kernel(sink_ref, q_ref, kp_ref, kc_ref, kn_ref, kx_ref, vp_ref, vc_ref, vn_ref, vx_ref,
                 bias_ref, o_ref):
    grp = lax.broadcasted_iota(jnp.int32, (1, 4 * CHUNK), 1) // CHUNK
    lane = _lane()
    mask_q = [(lane & 32) == 0, (lane & 32) != 0]
    bias = jnp.concatenate([bias_ref[...]] * 4, axis=1)
    vts, sts = [], []
    for kvp in range(2):
        sl = slice(kvp * LANES, (kvp + 1) * LANES)
        v_blocks = [vp_ref[:, sl], vc_ref[:, sl], vn_ref[:, sl], vx_ref[:, sl]]
        vts.append(jnp.concatenate(
            [vb[r * CHUNK:(r + 1) * CHUNK, :].astype(F32).T.astype(BF16)
             for vb in v_blocks for r in range(vb.shape[0] // CHUNK)], axis=1))
    for kvp in range(2):
        sl = slice(kvp * LANES, (kvp + 1) * LANES)
        kcat = jnp.concatenate([kp_ref[:, sl], kc_ref[:, sl], kn_ref[:, sl], kx_ref[:, sl]], axis=0)
        for a in range(2):
            qs = jnp.concatenate(
                [jnp.where(mask_q[a], q_ref[:, (kvp * 4 + g) * LANES:(kvp * 4 + g + 1) * LANES],
                           jnp.zeros((CHUNK, LANES), BF16)) for g in range(4)], axis=0)
            sts.append(_dot_nt(kcat, qs))
    outs = []
    for kv in range(4):
        st = sts[kv]
        st = jnp.concatenate([st[0:CHUNK] + bias[0:CHUNK], st[CHUNK:2 * CHUNK],
                              st[2 * CHUNK:3 * CHUNK] + bias[CHUNK:2 * CHUNK], st[3 * CHUNK:]], axis=0)
        snk = jnp.where(grp == 0, sink_ref[kv * 4],
                        jnp.where(grp == 1, sink_ref[kv * 4 + 1],
                                  jnp.where(grp == 2, sink_ref[kv * 4 + 2], sink_ref[kv * 4 + 3])))
        m = jnp.maximum(jnp.max(st, axis=0, keepdims=True), snk)
        e = jnp.exp(st - m)
        denom = jnp.exp(snk - m) + jnp.sum(e, axis=0, keepdims=True)
        a = kv % 2
        outs.append(_dot(vts[kv // 2][a * HEAD_DIM:(a + 1) * HEAD_DIM, :], e.astype(BF16)) * (1.0 / denom))
    for kvp in range(2):
        full = jnp.concatenate(outs[2 * kvp:2 * kvp + 2], axis=0)
        for g in range(4):
            o_ref[:, (kvp * 4 + g) * LANES:(kvp * 4 + g + 1) * LANES] = (
                full[:, g * CHUNK:(g + 1) * CHUNK].T.astype(BF16))


def _window_bias():
    kk = np.arange(CHUNK)[:, None]
    t = np.arange(CHUNK)[None, :]
    tabs = []
    for has_prev, has_next in ((False, True), (True, True), (True, False)):
        prev_ok = (kk >= t) & has_prev
        next_ok = (kk <= t) & has_next
        tabs.append(np.where(np.concatenate([prev_ok, next_ok], axis=0), 0.0, NEG))
    return jnp.asarray(np.stack(tabs), F32)


def _window_attn(y, sink, nb, nc, ctx_chunks):
    nq = nc - ctx_chunks
    assert nq >= 2
    ctx_len = ctx_chunks * CHUNK
    bias = _window_bias()
    d = 8 * LANES
    kcol, vcol = d // (2 * LANES), d // (2 * LANES) + 1
    qrow = lambda b, i: b * nc + ctx_chunks + i
    prow = lambda b, i: b * nc + ctx_chunks + jnp.maximum(i - 1, 0)
    nrow = lambda b, i: b * nc + ctx_chunks + jnp.minimum(i + 1, nq - 1)
    xrow = lambda b, i: (b * nc * CHUNK) // ctx_len
    kv_spec = lambda rowf, col: pl.BlockSpec((CHUNK, 2 * LANES), lambda b, i: (rowf(b, i), col))
    x_spec = lambda col: pl.BlockSpec((ctx_len, 2 * LANES), lambda b, i: (xrow(b, i), col))
    bias_spec = pl.BlockSpec((None,) + bias.shape[1:],
                             lambda b, i: (jnp.where(i == 0, 0, jnp.where(i == nq - 1, 2, 1)), 0, 0))
    return pl.pallas_call(
        _attn_kernel,
        grid=(nb, nq),
        in_specs=[_smem_spec(),
                  pl.BlockSpec((CHUNK, d), lambda b, i: (qrow(b, i), 0)),
                  kv_spec(prow, kcol), kv_spec(qrow, kcol), kv_spec(nrow, kcol), x_spec(kcol),
                  kv_spec(prow, vcol), kv_spec(qrow, vcol), kv_spec(nrow, vcol), x_spec(vcol),
                  bias_spec],
        out_specs=pl.BlockSpec((CHUNK, d), lambda b, i: (b * nq + i, 0)),
        out_shape=jax.ShapeDtypeStruct((nb * nq * CHUNK, d), BF16),
        compiler_params=_cparams("arbitrary", "arbitrary"),
        name="window_gqa",
    )(sink, y, y, y, y, y, y, y, y, y, bias)


def _ffn_in(x, m, p_ref, wo_ref):
    x1 = x + p_ref[0:1, :] * _dot(m, wo_ref[...])
    ms = jnp.mean(x1 * x1, axis=-1, keepdims=True)
    return x1, ((x1 * lax.rsqrt(ms + EPS)) * p_ref[1:2, :] + p_ref[2:3, :]).astype(BF16)


def _ffn_cols(h, wi_ref, act_scr, lo, hi, d_ff):
    gate = _dot(h, wi_ref[:, lo:hi])
    up = _dot(h, wi_ref[:, d_ff + lo:d_ff + hi])
    act_scr[:, lo:hi] = (gate * jax.nn.sigmoid(gate) * up).astype(BF16)


def _attn_ffn_kernel(sink_ref, q_ref, kp_ref, kc_ref, kn_ref, kx_ref, vp_ref, vc_ref, vn_ref, vx_ref, bias_ref,
                     x_ref, p_ref, wo_ref, wi_ref, w2_ref, o_ref, m_scr, act_scr, *, tiles, lat_tiles, d_ff):
    s = pl.program_id(0)

    @pl.when(s == 0)
    def _():
        m_scr[...] = jnp.zeros_like(m_scr)

    j = jnp.minimum(s, tiles - 1) % lat_tiles
    grp = lax.broadcasted_iota(jnp.int32, (1, 4 * CHUNK), 1) // CHUNK
    lane = _lane()
    mask_q = [(lane & 32) == 0, (lane & 32) != 0]
    tile4 = lambda b: jnp.concatenate([b] * 4, axis=1)
    biases = [tile4(jnp.where(j == 0, bias_ref[0], bias_ref[1])),
              tile4(jnp.where(j == lat_tiles - 1, bias_ref[2], bias_ref[1]))]
    cuts = _ffn_splits(d_ff, 6)
    d = o_ref.shape[1]

    def ffn_up(i):
        _ffn_cols(h, wi_ref, act_scr, cuts[i], cuts[i + 1], d_ff)

    def ffn_down(i):
        cs = slice(i * d // 4, (i + 1) * d // 4)
        o_ref[:, cs] = x1[:, cs] + p_ref[3:4, cs] * _dot(act_scr[...], w2_ref[:, cs])

    def scores(u):
        blk, kvp, a = u // 4, (u % 4) // 2, u % 2
        rows = slice(blk * CHUNK, (blk + 1) * CHUNK)
        qs = jnp.concatenate(
            [jnp.where(mask_q[a], q_ref[rows, (kvp * 4 + g) * LANES:(kvp * 4 + g + 1) * LANES],
                       jnp.zeros((CHUNK, LANES), BF16)) for g in range(4)], axis=0)
        return _dot_nt(kcats[kvp][blk], qs)

    def softmax_pv(u, st):
        blk, kv = u // 4, u % 4
        bias = biases[blk]
        st = jnp.concatenate([st[0:CHUNK] + bias[0:CHUNK], st[CHUNK:2 * CHUNK],
                              st[2 * CHUNK:3 * CHUNK] + bias[CHUNK:2 * CHUNK], st[3 * CHUNK:]], axis=0)
        snk = jnp.where(grp == 0, sink_ref[kv * 4],
                        jnp.where(grp == 1, sink_ref[kv * 4 + 1],
                                  jnp.where(grp == 2, sink_ref[kv * 4 + 2], sink_ref[kv * 4 + 3])))
        m = jnp.maximum(jnp.max(st, axis=0, keepdims=True), snk)
        e = jnp.exp(st - m)
        denom = jnp.exp(snk - m) + jnp.sum(e, axis=0, keepdims=True)
        a = kv % 2
        return _dot(vts[kv // 2][blk][a * HEAD_DIM:(a + 1) * HEAD_DIM, :], e.astype(BF16)) * (1.0 / denom)

    def hand_over(blk, outs):
        for kvp in range(2):
            full = jnp.concatenate(outs[2 * kvp:2 * kvp + 2], axis=0)
            for g in range(4):
                m_scr[blk * CHUNK:(blk + 1) * CHUNK, (kvp * 4 + g) * LANES:(kvp * 4 + g + 1) * LANES] = (
                    full[:, g * CHUNK:(g + 1) * CHUNK].T.astype(BF16))

    x1, h = _ffn_in(x_ref[...], m_scr[...], p_ref, wo_ref)

    kcats, vts = [], []
    for kvp in range(2):
        sl = slice(kvp * LANES, (kvp + 1) * LANES)
        k_chunks = [kp_ref[:, sl], kc_ref[0:CHUNK, sl], kc_ref[CHUNK:2 * CHUNK, sl], kn_ref[:, sl]]
        v_chunks = [vp_ref[:, sl], vc_ref[0:CHUNK, sl], vc_ref[CHUNK:2 * CHUNK, sl], vn_ref[:, sl],
                    vx_ref[0:CHUNK, sl], vx_ref[CHUNK:2 * CHUNK, sl]]
        v_t = [v.astype(F32).T.astype(BF16) for v in v_chunks]
        kcats.append([jnp.concatenate(k_chunks[b:b + 3] + [kx_ref[:, sl]], axis=0) for b in range(2)])
        vts.append([jnp.concatenate(v_t[b:b + 3] + v_t[4:], axis=1) for b in range(2)])

    ffn_pieces = [functools.partial(ffn_up, i) for i in range(6)] + [functools.partial(ffn_down, i) for i in range(2)]
    sts = {0: scores(0)}
    outs = []
    for u in range(8):
        if u + 1 < 8:
            sts[u + 1] = scores(u + 1)
        ffn_pieces[u]()
        outs.append(softmax_pv(u, sts.pop(u)))
        if u % 4 == 3:
            hand_over(u // 4, outs[u - 3:u + 1])
    ffn_down(2)
    ffn_down(3)


def _attn_ffn(y, sink, xc, prm, wo, wi_all, w2_all, layer, nb, nc, ctx_chunks):
    tm = ROW_TILE
    cpt = tm // CHUNK
    tpb = nc // cpt
    lat_tiles = (nc - ctx_chunks) // cpt
    tiles = nb * lat_tiles
    ctx_tiles = ctx_chunks // cpt
    assert ctx_tiles == 1 and lat_tiles >= 2
    d = xc.shape[1]
    d_ff = w2_all.shape[1]
    bias = _window_bias()
    kcol, vcol = d // (2 * LANES), d // (2 * LANES) + 1
    att = lambda s: jnp.minimum(s, tiles - 1)
    ffn = lambda s: jnp.maximum(s - 1, 0)
    row_tile = lambda t: (t // lat_tiles) * tpb + ctx_tiles + t % lat_tiles
    chunk0 = lambda t: (t // lat_tiles) * nc + ctx_chunks
    prev_c = lambda s: chunk0(att(s)) + jnp.maximum((att(s) % lat_tiles) * cpt - 1, 0)
    next_c = lambda s: chunk0(att(s)) + jnp.minimum((att(s) % lat_tiles) * cpt + cpt, lat_tiles * cpt - 1)
    cur_spec = lambda col: pl.BlockSpec((tm, 2 * LANES), lambda s: (row_tile(att(s)), col))
    edge_spec = lambda f, col: pl.BlockSpec((CHUNK, 2 * LANES), lambda s: (f(s), col))
    ctx_spec = lambda col: pl.BlockSpec((tm, 2 * LANES), lambda s: ((att(s) // lat_tiles) * tpb, col))
    layer_spec = lambda a: pl.BlockSpec((None,) + a.shape[1:], lambda s: (layer, 0, 0),
                                        pipeline_mode=pl.Buffered(1))
    return pl.pallas_call(
        functools.partial(_attn_ffn_kernel, tiles=tiles, lat_tiles=lat_tiles, d_ff=d_ff),
        grid=(tiles + 1,),
        in_specs=[_smem_spec(),
                  pl.BlockSpec((tm, d), lambda s: (row_tile(att(s)), 0)),
                  edge_spec(prev_c, kcol), cur_spec(kcol), edge_spec(next_c, kcol), ctx_spec(kcol),
                  edge_spec(prev_c, vcol), cur_spec(vcol), edge_spec(next_c, vcol), ctx_spec(vcol),
                  _const_spec(bias.shape),
                  pl.BlockSpec((tm, d), lambda s: (row_tile(ffn(s)), 0)),
                  pl.BlockSpec((None, 8, d), lambda s: ((ffn(s) // lat_tiles) * 2 + 1, 0, 0)),
                  _const_spec(wo.shape), layer_spec(wi_all), layer_spec(w2_all)],
        out_specs=pl.BlockSpec((tm, d), lambda s: (ffn(s), 0)),
        out_shape=jax.ShapeDtypeStruct((tiles * tm, d), F32),
        scratch_shapes=[pltpu.VMEM((tm, d), BF16), pltpu.VMEM((tm, d_ff), BF16)],
        compiler_params=_cparams("arbitrary"),
        name="window_gqa_ffn",
    )(sink, y, y, y, y, y, y, y, y, y, bias, xc, prm, wo, wi_all, w2_all)


def _post_kernel(*refs, d_ff, split_tpb):
    tps = TILES_PER_STEP
    n_x = 2 if split_tpb else 1
    x_refs, refs = refs[:n_x * tps], refs[n_x * tps:]
    m_refs, p_refs = refs[:tps], refs[tps:2 * tps]
    wo_ref, wi_ref, w2_ref, o_ref, act_scr = refs[2 * tps:]
    tm = m_refs[0].shape[0]
    ys = [_dot(m_refs[u][...], wo_ref[...]) for u in range(tps)]
    x1s, hs = [], []
    for u in range(tps):
        if split_tpb:
            x = _ctx_or_latent_rows(x_refs[2 * u], x_refs[2 * u + 1], pl.program_id(0) * tps + u, split_tpb)
        else:
            x = x_refs[u][...]
        p_ref = p_refs[u]
        x1 = x + p_ref[0:1, :] * ys[u]
        ms = jnp.mean(x1 * x1, axis=-1, keepdims=True)
        hs.append(((x1 * lax.rsqrt(ms + EPS)) * p_ref[1:2, :] + p_ref[2:3, :]).astype(BF16))
        x1s.append(x1)
    for u in range(tps):
        gate = _dot(hs[u], wi_ref[:, :d_ff])
        up = _dot(hs[u], wi_ref[:, d_ff:])
        act_scr[u] = (gate * jax.nn.sigmoid(gate) * up).astype(BF16)
    for u in range(tps):
        o_ref[u * tm:(u + 1) * tm, :] = x1s[u] + p_refs[u][3:4, :] * _dot(act_scr[u], w2_ref[...])


def _post(xs, m2, prm, wo, wi_all, w2_all, layer, tiles_per_batch, skip_tiles, m_has_ctx):
    split = len(xs) == 2
    d = xs[-1].shape[1]
    r = sum(a.shape[0] for a in xs)
    tm = ROW_TILE
    tps = TILES_PER_STEP
    d_ff = w2_all.shape[1]
    tpb = tiles_per_batch
    kept = tpb - skip_tiles
    nb = r // (tm * tpb)
    assert not (split and skip_tiles) and (nb * kept) % tps == 0
    xrow = lambda t: (t // kept) * tpb + skip_tiles + t % kept
    mrow = xrow if m_has_ctx else (lambda t: t)
    sel = lambda t: ((t // kept) * 2 + jnp.minimum(skip_tiles + t % kept, 1), 0, 0)
    at = lambda f, u: (lambda i: f(i * tps + u))
    row_spec = lambda f, u: pl.BlockSpec((tm, d), lambda i: (f(i * tps + u), 0))
    x_specs = []
    for u in range(tps):
        x_specs += _split_row_specs(tm, d, tpb, at(lambda t: t, u)) if split else [row_spec(xrow, u)]
    layer_spec = lambda a: pl.BlockSpec((None,) + a.shape[1:], lambda i: (layer, 0, 0),
                                        pipeline_mode=pl.Buffered(1))
    return pl.pallas_call(
        functools.partial(_post_kernel, d_ff=d_ff, split_tpb=tpb if split else 0),
        grid=(nb * kept // tps,),
        in_specs=x_specs + [row_spec(mrow, u) for u in range(tps)]
                 + [pl.BlockSpec((None, 8, d), at(sel, u)) for u in range(tps)]
                 + [_const_spec(wo.shape), layer_spec(wi_all), layer_spec(w2_all)],
        out_specs=pl.BlockSpec((tps * tm, d), lambda i: (i, 0)),
        out_shape=jax.ShapeDtypeStruct((nb * kept * tm, d), F32),
        scratch_shapes=[pltpu.VMEM((tps, tm, d_ff), BF16)],
        compiler_params=_cparams("arbitrary"),
        name="outproj_swiglu",
    )(*(list(xs) * tps), *([m2] * tps), *([prm] * tps), wo, wi_all, w2_all)


def _pair_cols(w):
    rows, cols = w.shape
    return w.reshape(rows, cols // LANES, 2, 2, 32).transpose(0, 1, 3, 2, 4).reshape(rows, cols)


def _attn_q_cols(w):
    rows = w.shape[0]
    g_per = w.shape[1] // (H_KV * HEAD_DIM)
    return (w.reshape(rows, H_KV // 2, 2, g_per, 2, 32).transpose(0, 1, 3, 4, 2, 5)
            .reshape(rows, w.shape[1]))


def _attn_o_rows(w):
    cols = w.shape[1]
    g_per = w.shape[0] // (H_KV * HEAD_DIM)
    return (w.reshape(H_KV // 2, 2, g_per, HEAD_DIM, cols).transpose(0, 2, 1, 3, 4)
            .reshape(w.shape[0], cols))


def _rope_tables(seq, ctx_len):
    rows = seq // GRID_W
    row = np.repeat(np.arange(rows, dtype=np.float32), GRID_W)
    col = np.tile(np.arange(GRID_W, dtype=np.float32), rows)
    n = HEAD_DIM // 4
    inv = (np.float32(ROPE_BASE) ** (-np.arange(n, dtype=np.float32) / np.float32(n))).astype(np.float32)
    ang = np.concatenate([row[:, None] * inv, col[:, None] * inv], axis=-1).astype(np.float32)
    cos, sin = np.cos(ang), np.sin(ang)
    cos_t = np.concatenate([np.ones((ctx_len, LANES), np.float32), np.tile(cos, (1, 4))], axis=0)
    sin_t = np.concatenate([np.zeros((ctx_len, LANES), np.float32),
                            np.concatenate([-sin, -sin, sin, sin], axis=-1)], axis=0)
    return jnp.asarray(cos_t, F32), jnp.asarray(sin_t, F32)


def _mod_tables(mod, nb, norm_w):
    d = norm_w.shape[-1]
    lat = mod[:nb].reshape(nb, 6, d)
    ctx = jnp.broadcast_to(mod[nb].reshape(1, 6, d), (nb, 6, d))
    both = jnp.stack([ctx, lat], axis=1).reshape(nb * 2, 6, d)
    sh1, sc1, g1, sh2, sc2, g2 = [both[:, k] for k in range(6)]
    ab1 = jnp.stack([norm_w[0] * (1.0 + sc1), sh1], axis=1)
    zeros = jnp.zeros_like(g1)
    prm = jnp.stack([g1, norm_w[1] * (1.0 + sc2), sh2, g2, zeros, zeros, zeros, zeros], axis=1)
    return ab1, prm


def kernel(x, c, ctx, c_ctx, ada_w, ada_b, norm_w, ffn_w_in, ffn_w_out, ab_w_in, ab_w_out,
           ret_log_gamma, ret_norm_w, mlstm_conv_w, mlstm_conv_b, mlstm_gate_b, mlstm_norm_w,
           attn_w_in, attn_w_out, attn_q_norm_w, attn_k_norm_w, attn_sink):
    nb, seq, d = x.shape
    ctx_len = ctx.shape[1]
    depth = ada_w.shape[0]
    assert ctx_len == ROW_TILE and seq % ROW_TILE == 0 and d == 8 * LANES and nb < 8
    t_all = ctx_len + seq
    nc = t_all // CHUNK
    ctx_chunks = ctx_len // CHUNK
    tpb = t_all // ROW_TILE
    dr = d // 2

    rows = jnp.zeros((8, d), F32).at[:nb].set(c).at[nb].set(c_ctx)
    mod_all = _modulation(rows, ada_w, ada_b)
    cos_t, sin_t = _rope_tables(seq, ctx_len)
    wi_all = ffn_w_in.astype(BF16)
    w2_all = ffn_w_out.astype(BF16)
    xs = (ctx.reshape(nb * ctx_len, d), x.reshape(nb * seq, d))

    out = None
    for layer in range(depth):
        last = layer == depth - 1
        ab1, prm = _mod_tables(mod_all[layer], nb, norm_w[layer])
        if layer % 2 == 0:
            assert layer == 0
            e = layer // 2
            w = ab_w_in[e]
            w_main = jnp.concatenate([_pair_cols(w[:, :dr]), _pair_cols(w[:, dr:2 * dr]), w[:, 2 * dr:8 * dr]],
                                     axis=1).astype(BF16)
            wg = jnp.zeros((d, LANES), F32).at[:, :32].set(w[:, 8 * dr:]).astype(BF16)
            wgt = w[:, 8 * dr:].T.astype(BF16)
            gb = jnp.zeros((1, LANES), F32).at[0, :32].set(mlstm_gate_b[e].reshape(-1))
            gbt = mlstm_gate_b[e].reshape(32, 1)
            lg = ret_log_gamma[e].astype(F32)
            lgk = jnp.tile(jnp.repeat(lg.reshape(2, 4, 2), 32, axis=-1), (1, 1, 2)).reshape(8, LANES)
            cw =jnp.concatenate([mlstm_conv_w[e], mlstm_conv_b[e][None],
                                  jnp.zeros((4, 2 * dr), F32)], axis=0)
            nw = jnp.broadcast_to(jnp.concatenate([ret_norm_w[e], mlstm_norm_w[e]]).reshape(8, LANES, 1),
                                  (8, LANES, LANES))
            lg_s = lg.reshape(-1)
            y, g, gt, *states = _inproj_sweep(*xs, ab1, w_main, wg, wgt, gb, gbt, cos_t, sin_t, lgk, cw, nb, tpb)
            wo = ab_w_out[e].astype(BF16)
            out = _mixer_ffn(y, g, gt, states, lg_s, lgk, lgk.T, nw, xs, prm, wo, wi_all, w2_all, layer,
                             nb, tpb)
        else:
            o = layer // 2
            w = attn_w_in[o]
            w_main = jnp.concatenate([_attn_q_cols(w[:, :d]), _pair_cols(w[:, d:d + 2 * LANES]),
                                      w[:, d + 2 * LANES:]], axis=1).astype(BF16)
            lane_w = lambda v: jnp.concatenate([v[:32], v[:32], v[32:], v[32:]])
            nwq = jnp.stack([lane_w(attn_q_norm_w[o]) * (HEAD_DIM ** -0.5), lane_w(attn_k_norm_w[o])]
                            + [jnp.zeros((LANES,), F32)] * 6)
            assert last
            y = _inproj_attn(xs[0], ab1, w_main, nwq, cos_t, sin_t, tpb)
            wo = _attn_o_rows(attn_w_out[o]).astype(BF16)
            out = _attn_ffn(y, attn_sink[o].astype(F32), xs[0], prm, wo, wi_all, w2_all, layer,
                            nb, nc, ctx_chunks)
        xs = (out,)
    return out.reshape(nb, seq, d)
```

```python
import functools

import numpy as np
import jax
import jax.numpy as jnp
from jax import lax
from jax.experimental import pallas as pl
from jax.experimental.pallas import tpu as pltpu

F32 = jnp.float32
BF16 = jnp.bfloat16

HEAD_DIM = 64
CHUNK = 128
GRID_W = 64
ROPE_BASE = 10000.0
EPS = 1e-6
H_KV = 4
LANES = 128
ROW_TILE = 256
TILES_PER_STEP = 2
HALO = 16
NEG = -1e30
VMEM_LIMIT = 56 * 1024 * 1024


def _cparams(*sem):
    return pltpu.CompilerParams(dimension_semantics=sem, vmem_limit_bytes=VMEM_LIMIT)


def _const_spec(shape):
    nd = len(shape)
    return pl.BlockSpec(shape, lambda *_: (0,) * nd, pipeline_mode=pl.Buffered(1))


def _smem_spec():
    return pl.BlockSpec(memory_space=pltpu.SMEM)


def _lane(shape=(CHUNK, LANES)):
    return lax.broadcasted_iota(jnp.int32, shape, len(shape) - 1)


def _dot(a, b):
    return jnp.dot(a, b, preferred_element_type=F32)


def _dot_nt(a, b):
    return lax.dot_general(a, b, (((1,), (1,)), ((), ())), preferred_element_type=F32)


def _dot_tn(a, b):
    return lax.dot_general(a, b, (((0,), (0,)), ((), ())), preferred_element_type=F32)


def _split3(x):
    hi = x.astype(BF16)
    r = x - hi.astype(F32)
    mid = r.astype(BF16)
    lo = (r - mid.astype(F32)).astype(BF16)
    return hi, mid, lo


def _log_sigmoid(x):
    return jnp.minimum(x, 0.0) - jnp.log1p(jnp.exp(-jnp.abs(x)))


def _rope(x, cos, sin_signed):
    return x * cos + pltpu.roll(x, LANES // 2, 1) * sin_signed


def _mod_kernel(rows_ref, w_ref, b_ref, o_ref):
    a = rows_ref[...]
    a = a * jax.nn.sigmoid(a)
    a_hi = a.astype(BF16)
    a_lo = (a - a_hi.astype(F32)).astype(BF16)
    w = w_ref[...]
    w_hi = w.astype(BF16)
    w_lo = (w - w_hi.astype(F32)).astype(BF16)
    o_ref[...] = _dot(a_hi, w_hi) + _dot(a_hi, w_lo) + _dot(a_lo, w_hi) + b_ref[...]


def _modulation(rows, ada_w, ada_b):
    depth, d, n = ada_w.shape
    tn = n // 4
    return pl.pallas_call(
        _mod_kernel,
        grid=(depth, n // tn),
        in_specs=[pl.BlockSpec((8, d), lambda l, j: (0, 0)),
                  pl.BlockSpec((None, d, tn), lambda l, j: (l, 0, j)),
                  pl.BlockSpec((None, 1, tn), lambda l, j: (l, 0, j))],
        out_specs=pl.BlockSpec((None, 8, tn), lambda l, j: (l, 0, j)),
        out_shape=jax.ShapeDtypeStruct((depth, 8, n), F32),
        compiler_params=_cparams("arbitrary", "arbitrary"),
        name="adaln_modulation",
    )(rows, ada_w, ada_b.reshape(depth, 1, n))


def _norm_mod(x, ab_ref):
    ms = jnp.mean(x * x, axis=-1, keepdims=True)
    h = (x * lax.rsqrt(ms + EPS)) * ab_ref[0:1, :] + ab_ref[1:2, :]
    return h.astype(BF16)


def _ctx_or_latent_rows(ctx_ref, x_ref, tile, tiles_per_batch):
    return jnp.where(tile % tiles_per_batch == 0, ctx_ref[...], x_ref[...])


def _split_row_specs(tm, d, tpb, tile_of):
    lat = tpb - 1
    return [pl.BlockSpec((tm, d), lambda i: (tile_of(i) // tpb, 0)),
            pl.BlockSpec((tm, d), lambda i: ((tile_of(i) // tpb) * lat + jnp.maximum(tile_of(i) % tpb - 1, 0), 0))]


def _inproj_ab_kernel(*refs, tpb):
    tps = TILES_PER_STEP
    x_refs, ab_refs, refs = refs[:2 * tps], refs[2 * tps:3 * tps], refs[3 * tps:]
    w_ref, wg_ref, wgt_ref, gb_ref, gbt_ref = refs[:5]
    rope_refs, (y_ref, g_ref, gt_ref) = refs[5:5 + 2 * tps], refs[5 + 2 * tps:]
    tm = y_ref.shape[0] // tps
    hbs = [_norm_mod(_ctx_or_latent_rows(x_refs[2 * u], x_refs[2 * u + 1], pl.program_id(0) * tps + u, tpb),
                     ab_refs[u]) for u in range(tps)]
    gw = 4 * LANES
    for j in range(8):
        for u in range(tps):
            acc = _dot(hbs[u], w_ref[:, j * gw:(j + 1) * gw])
            if j in (0, 1):
                if j == 0:
                    acc = acc * (HEAD_DIM ** -0.5)
                cos, sin = rope_refs[2 * u][...], rope_refs[2 * u + 1][...]
                acc = jnp.concatenate(
                    [_rope(acc[:, p * LANES:(p + 1) * LANES], cos, sin) for p in range(4)], axis=1)
            elif j == 3:
                acc = acc * jax.nn.sigmoid(acc)
            elif j == 7:
                acc = jax.nn.sigmoid(acc)
            y_ref[u * tm:(u + 1) * tm, j * gw:(j + 1) * gw] = acc.astype(BF16)
    for u in range(tps):
        g_ref[u * tm:(u + 1) * tm, :] = _dot(hbs[u], wg_ref[...]) + gb_ref[...]
        gt_ref[:, u * tm:(u + 1) * tm] = _dot_nt(wgt_ref[...], hbs[u]) + gbt_ref[...]


def _tile_specs(tm, tpb, tps):
    at = lambda f, u: (lambda i: f(i * tps + u))
    sel = lambda t: ((t // tpb) * 2 + jnp.minimum(t % tpb, 1), 0, 0)
    mod_spec = lambda d, u: pl.BlockSpec((None, 2, d), at(sel, u))
    rope_spec = lambda u: pl.BlockSpec((tm, LANES), at(lambda t: (t % tpb, 0), u))
    return at, mod_spec, rope_spec


def _inproj_ab(ctx2, x2, ab, w, wg, wgt, gb, gbt, cos, sin, tiles_per_batch):
    d = x2.shape[1]
    r = ctx2.shape[0] + x2.shape[0]
    tm, tps = ROW_TILE, TILES_PER_STEP
    n = w.shape[1]
    tpb = tiles_per_batch
    assert (r // tm) % tps == 0
    at, mod_spec, rope_spec = _tile_specs(tm, tpb, tps)
    x_specs, rope_specs = [], []
    for u in range(tps):
        x_specs += _split_row_specs(tm, d, tpb, at(lambda t: t, u))
        rope_specs += [rope_spec(u), rope_spec(u)]
    return pl.pallas_call(
        functools.partial(_inproj_ab_kernel, tpb=tpb),
        grid=(r // (tm * tps),),
        in_specs=x_specs + [mod_spec(d, u) for u in range(tps)] + [
                  _const_spec(w.shape), _const_spec(wg.shape), _const_spec(wgt.shape),
                  _const_spec(gb.shape), _const_spec(gbt.shape)] + rope_specs,
        out_specs=[pl.BlockSpec((tps * tm, n), lambda i: (i, 0)),
                   pl.BlockSpec((tps * tm, LANES), lambda i: (i, 0)),
                   pl.BlockSpec((32, tps * tm), lambda i: (0, i))],
        out_shape=[jax.ShapeDtypeStruct((r, n), BF16),
                   jax.ShapeDtypeStruct((r, LANES), F32),
                   jax.ShapeDtypeStruct((32, r), F32)],
        compiler_params=_cparams("arbitrary"),
        name="inproj_ret_mlstm",
    )(*([ctx2, x2] * tps), *([ab] * tps), w, wg, wgt, gb, gbt, *([cos, sin] * tps))


def _inproj_attn_kernel(*refs):
    tps = TILES_PER_STEP
    x_ref, ab_refs, refs = refs[0], refs[1:1 + tps], refs[1 + tps:]
    w_ref, nw_ref = refs[:2]
    rope_refs, y_ref = refs[2:2 + 2 * tps], refs[2 + 2 * tps]
    tm = y_ref.shape[0] // tps
    hbs = [_norm_mod(x_ref[u * tm:(u + 1) * tm, :], ab_refs[u]) for u in range(tps)]
    r2 = lax.broadcasted_iota(jnp.int32, (2 * LANES, 2 * LANES), 0)
    c2 = lax.broadcasted_iota(jnp.int32, (2 * LANES, 2 * LANES), 1)
    same_head = (((r2 ^ c2) & (LANES | 32)) == 0).astype(BF16)
    acc_q = [_dot(hbs[u], w_ref[:, 0:8 * LANES]) for u in range(tps)]
    acc_k = [_dot(hbs[u], w_ref[:, 8 * LANES:10 * LANES]) for u in range(tps)]
    acc_v = [_dot(hbs[u], w_ref[:, 10 * LANES:12 * LANES]) for u in range(tps)]
    for j in range(5):
        for u in range(tps):
            acc = acc_q[u][:, j * 2 * LANES:(j + 1) * 2 * LANES] if j < 4 else acc_k[u]
            sq = acc * acc
            hi = sq.astype(BF16)
            lo = (sq - hi.astype(F32)).astype(BF16)
            ms = (_dot(hi, same_head) + _dot(lo, same_head)) * (1.0 / HEAD_DIM)
            nrm = acc * lax.rsqrt(ms + EPS)
            nw = nw_ref[0:1, :] if j < 4 else nw_ref[1:2, :]
            cos, sin = rope_refs[2 * u][...], rope_refs[2 * u + 1][...]
            for v in range(2):
                ls = slice(v * LANES, (v + 1) * LANES)
                y_ref[u * tm:(u + 1) * tm, (2 * j + v) * LANES:(2 * j + v + 1) * LANES] = (
                    _rope(nrm[:, ls] * nw, cos, sin).astype(BF16))
    for u in range(tps):
        y_ref[u * tm:(u + 1) * tm, 10 * LANES:12 * LANES] = acc_v[u].astype(BF16)


def _inproj_attn(x2, ab, w, nw, cos, sin, tiles_per_batch):
    r, d = x2.shape
    tm, tps = ROW_TILE, TILES_PER_STEP
    n = w.shape[1]
    tpb = tiles_per_batch
    assert (r // tm) % tps == 0
    _, mod_spec, rope_spec = _tile_specs(tm, tpb, tps)
    rope_specs = []
    for u in range(tps):
        rope_specs += [rope_spec(u), rope_spec(u)]
    return pl.pallas_call(
        _inproj_attn_kernel,
        grid=(r // (tm * tps),),
        in_specs=[pl.BlockSpec((tps * tm, d), lambda i: (i, 0))] + [mod_spec(d, u) for u in range(tps)]
                 + [_const_spec(w.shape), _const_spec(nw.shape)] + rope_specs,
        out_specs=pl.BlockSpec((tps * tm, n), lambda i: (i, 0)),
        out_shape=jax.ShapeDtypeStruct((r, n), BF16),
        compiler_params=_cparams("arbitrary"),
        name="inproj_attn",
    )(x2, *([ab] * tps), w, nw, *([cos, sin] * tps))


def _conv_silu(cur_ref, prev_ref, next_ref, cw_ref, prev_on, next_on):
    cur = cur_ref[...].astype(F32)
    n = cur.shape[0]
    row = lax.broadcasted_iota(jnp.int32, cur.shape, 0)
    prev_row = prev_ref[HALO - 1:HALO, :].astype(F32) * prev_on
    next_row = next_ref[0:1, :].astype(F32) * next_on
    xm = jnp.where(row == 0, prev_row, pltpu.roll(cur, 1, 0))
    xp = jnp.where(row == n - 1, next_row, pltpu.roll(cur, n - 1, 0))
    y = cw_ref[3:4, :] + cw_ref[0:1, :] * xm + cw_ref[1:2, :] * cur + cw_ref[2:3, :] * xp
    return y * jax.nn.sigmoid(y)


def _segment_flags(c, nc, ctx_chunks):
    prev_on = jnp.where((c == 0) | (c == ctx_chunks), 0.0, 1.0).astype(F32)
    next_on = jnp.where((c == ctx_chunks - 1) | (c == nc - 1), 0.0, 1.0).astype(F32)
    return prev_on, next_on


def _cumsum_cols(tri_bf, lf):
    hi, mid, lo = _split3(lf)
    return _dot(tri_bf, hi) + _dot(tri_bf, mid) + _dot(tri_bf, lo)


def _cumsum_rows(lf, tri_bf):
    hi, mid, lo = _split3(lf)
    return _dot(hi, tri_bf) + _dot(mid, tri_bf) + _dot(lo, tri_bf)


def _ret_state_update(s_ref, p, k2, v2, kdec, cd_lanes, bd):
    kf = (k2.astype(F32) * kdec).astype(BF16)
    s_ref[p] = s_ref[p] * cd_lanes + jnp.where(bd, _dot_tn(v2, kf), 0.0)


def _mlstm_state_update(c_ref, n_ref, m_ref, k_pairs, v_pairs, c_all, bend, col0, lo, bd):
    cmax = jnp.max(c_all, axis=0, keepdims=True)
    w_all = jnp.exp(c_all - cmax)
    m_old = m_ref[0:1, :]
    mrel = jnp.maximum(m_old, cmax)
    a_row = jnp.exp(m_old - mrel)
    bb_row = jnp.exp(cmax - mrel)
    m_ref[0:1, :] = bend + mrel
    lo_row = lo[0:1, :]
    for p in range(4):
        h0 = col0 + 2 * p
        kw = k_pairs[p] * jnp.where(lo, w_all[:, h0:h0 + 1], w_all[:, h0 + 1:h0 + 2])
        kvt = _dot_tn(v_pairs[p], kw.astype(BF16))
        nloc = jnp.sum(kw, axis=0, keepdims=True)
        a_l = jnp.where(lo_row, a_row[:, h0:h0 + 1], a_row[:, h0 + 1:h0 + 2])
        bb_l = jnp.where(lo_row, bb_row[:, h0:h0 + 1], bb_row[:, h0 + 1:h0 + 2])
        c_ref[p] = c_ref[p] * a_l + jnp.where(bd, kvt, 0.0) * bb_l
        n_new = (n_ref[p, 0:1, :] + n_ref[p, 1:2, :]) * a_l + nloc * bb_l
        n_ref[p, 0:1, :] = jnp.where(lo_row, n_new, 0.0)
        n_ref[p, 1:2, :] = jnp.where(lo_row, 0.0, n_new)


def _bwd_state_kernel(rk_ref, rv_ref, mqk_ref, mqkp_ref, mqkn_ref, mv_ref, g_ref, lgk_ref, cw_ref,
                      sret_ref, cm_ref, nm_ref, mm_ref, qk_ref,
                      s_scr, c_scr, n_scr, m_scr, kdec_scr, *, tpb):
    i = pl.program_id(1)
    jt = jnp.where(i == 0, 0, tpb - i)
    lane = _lane()
    sub = lax.broadcasted_iota(jnp.int32, (CHUNK, LANES), 0)
    lo = lane < HEAD_DIM
    lo_row = lo[0:1, :]
    bd_ret = (sub >= HEAD_DIM) == ((lane & 32) != 0)
    bd_m = (sub >= HEAD_DIM) == (lane >= HEAD_DIM)

    @pl.when(i == 0)
    def _():
        s_scr[...] = jnp.zeros_like(s_scr)
        c_scr[...] = jnp.zeros_like(c_scr)
        n_scr[...] = jnp.zeros_like(n_scr)
        m_scr[...] = jnp.zeros_like(m_scr)
        pos = sub.astype(F32)
        for p in range(4):
            kdec_scr[p] = jnp.exp(lgk_ref[4 + p:5 + p, :] * pos)

    prev_on = jnp.where(jt <= 1, 0.0, 1.0).astype(F32)
    next_on = jnp.where((jt == 0) | (jt == tpb - 1), 0.0, 1.0).astype(F32)
    qk = _conv_silu(mqk_ref, mqkp_ref, mqkn_ref, cw_ref, prev_on, next_on)
    qk_ref[:, 0:4 * LANES] = (qk[:, 0:4 * LANES] * (HEAD_DIM ** -0.5)).astype(BF16)
    qk_ref[:, 4 * LANES:] = qk[:, 4 * LANES:].astype(BF16)
    le_bf = (sub <= lane).astype(BF16)

    order = (1, 0)
    pre = {}
    for blk in order:
        rows = slice(blk * CHUNK, (blk + 1) * CHUNK)
        g = g_ref[rows, :]
        bal = pltpu.roll(_cumsum_cols(le_bf, _log_sigmoid(g)), LANES - 8, 1)
        c_all = g - bal
        cmax = jnp.max(c_all, axis=0, keepdims=True)
        w_all = jnp.exp(c_all - cmax)
        vts, ks, nlocs = [], [], []
        for p in range(8):
            sl = slice((p % 4) * LANES, (p % 4 + 1) * LANES)
            if p < 4:
                v2 = rv_ref[rows, sl]
                ks.append((rk_ref[rows, sl].astype(F32) * kdec_scr[p]).astype(BF16))
            else:
                v2 = mv_ref[rows, sl]
                h0 = 16 + 2 * (p - 4)
                kw = qk[rows, 4 * LANES + (p - 4) * LANES:4 * LANES + (p - 3) * LANES] * jnp.where(
                    lo, w_all[:, h0:h0 + 1], w_all[:, h0 + 1:h0 + 2])
                ks.append(kw.astype(BF16))
                nlocs.append(jnp.sum(kw, axis=0, keepdims=True))
            vts.append(v2.astype(F32).T.astype(BF16))
        pre[blk] = (vts, ks, nlocs, cmax, bal[0:1, :])
    kvs = {blk: [_dot(pre[blk][0][p], pre[blk][1][p]) for p in range(8)] for blk in order}

    for blk in order:
        _, _, nlocs, cmax, bend = pre[blk]
        sret_ref[blk] = s_scr[...].astype(BF16)
        cm_ref[blk] = c_scr[...].astype(BF16)
        nm_ref[blk] = n_scr[...].astype(BF16)
        mm_ref[blk] = m_scr[...]
        m_old = m_scr[0:1, :]
        mrel = jnp.maximum(m_old, cmax)
        a_row = jnp.exp(m_old - mrel)
        bb_row = jnp.exp(cmax - mrel)
        m_scr[0:1, :] = bend + mrel
        for p in range(4):
            cd = jnp.exp(lgk_ref[4 + p:5 + p, :] * float(CHUNK))
            s_scr[p] = s_scr[p] * cd + jnp.where(bd_ret, kvs[blk][p], 0.0)
            h0 = 16 + 2 * p
            a_l = jnp.where(lo_row, a_row[:, h0:h0 + 1], a_row[:, h0 + 1:h0 + 2])
            bb_l = jnp.where(lo_row, bb_row[:, h0:h0 + 1], bb_row[:, h0 + 1:h0 + 2])
            c_scr[p] = c_scr[p] * a_l + jnp.where(bd_m, kvs[blk][4 + p], 0.0) * bb_l
            n_new = (n_scr[p, 0:1, :] + n_scr[p, 1:2, :]) * a_l + nlocs[p] * bb_l
            n_scr[p, 0:1, :] = jnp.where(lo_row, n_new, 0.0)
            n_scr[p, 1:2, :] = jnp.where(lo_row, 0.0, n_new)


def _bwd_states(y, g, lgk, cw, nb, tpb):
    gw = 4 * LANES
    tm = ROW_TILE
    cpt = tm // CHUNK
    tile = lambda b, i: b * tpb + jnp.where(i == 0, 0, tpb - i)
    hb = tm // HALO
    nhalo = y.shape[0] // HALO
    blk = lambda j: pl.BlockSpec((tm, gw), lambda b, i: (tile(b, i), j))
    state = lambda *dims: pl.BlockSpec((cpt,) + dims, lambda b, i: (tile(b, i),) + (0,) * len(dims))
    nchunks = nb * tpb * cpt
    return pl.pallas_call(
        functools.partial(_bwd_state_kernel, tpb=tpb),
        grid=(nb, tpb),
        in_specs=[blk(1), blk(2),
                  pl.BlockSpec((tm, 2 * gw), lambda b, i: (tile(b, i), 2)),
                  pl.BlockSpec((HALO, 2 * gw), lambda b, i: (jnp.maximum(tile(b, i) * hb - 1, 0), 2)),
                  pl.BlockSpec((HALO, 2 * gw), lambda b, i: (jnp.minimum((tile(b, i) + 1) * hb, nhalo - 1), 2)),
                  blk(6),
                  pl.BlockSpec((tm, LANES), lambda b, i: (tile(b, i), 0)),
                  _const_spec(lgk.shape), _const_spec(cw.shape)],
        out_specs=[state(4, CHUNK, LANES), state(4, CHUNK, LANES), state(4, HALO, LANES), state(8, LANES),
                   pl.BlockSpec((tm, 2 * gw), lambda b, i: (tile(b, i), 0))],
        out_shape=[jax.ShapeDtypeStruct((nchunks, 4, CHUNK, LANES), BF16),
                   jax.ShapeDtypeStruct((nchunks, 4, CHUNK, LANES), BF16),
                   jax.ShapeDtypeStruct((nchunks, 4, HALO, LANES), BF16),
                   jax.ShapeDtypeStruct((nchunks, 8, LANES), F32),
                   jax.ShapeDtypeStruct((y.shape[0], 2 * gw), BF16)],
        scratch_shapes=[pltpu.VMEM((4, CHUNK, LANES), F32), pltpu.VMEM((4, CHUNK, LANES), F32),
                        pltpu.VMEM((4, HALO, LANES), F32), pltpu.VMEM((8, LANES), F32),
                        pltpu.VMEM((4, CHUNK, LANES), F32)],
        compiler_params=_cparams("arbitrary", "arbitrary"),
        name="bwd_state_sweep",
    )(y, y, y, y, y, y, g, lgk, cw)


def _mlstm_dir_weights(st, qn_row, c_col, bt_row, m_prev, tri):
    dl = jnp.where(tri, c_col + bt_row, NEG)
    mx = jnp.max(dl, axis=0, keepdims=True)
    al = bt_row + m_prev
    m_t = jnp.maximum(al, mx)
    w = jnp.exp(dl - m_t)
    a_t = jnp.exp(al - m_t)
    sw = st * w
    den = jnp.sum(sw, axis=0, keepdims=True) + a_t * qn_row
    r = 1.0 / jnp.maximum(jnp.abs(den), jnp.exp(-m_t))
    return sw * r, a_t * r


def _heads_out(ht, nw_tab):
    rows = []
    for a in range(2):
        ha = ht[a * HEAD_DIM:(a + 1) * HEAD_DIM, :]
        ms = jnp.mean(ha * ha, axis=0, keepdims=True)
        rows.append(ha * lax.rsqrt(ms + EPS))
    return (jnp.concatenate(rows, axis=0) * nw_tab).T


def _mixer_kernel(lg_ref, rq_ref, rk_ref, rv_ref, rg_ref, mqk_ref, mqkp_ref, mqkn_ref, mv_ref, mo_ref,
                  g_ref, gt_ref, sretb_ref, cmb_ref, nmb_ref, mmb_ref,
                  lgk_ref, lgkt_ref, cw_ref, nw_ref,
                  out_ref,
                  s_scr, c_scr, n_scr, m_scr, dm_scr, dec_scr, *, nc, ctx_chunks):
    c = pl.program_id(1)
    lane = _lane()
    sub = lax.broadcasted_iota(jnp.int32, (CHUNK, LANES), 0)
    lo = lane < HEAD_DIM
    sub_lo = sub < HEAD_DIM
    mask_ret = [(lane & 32) == 0, (lane & 32) != 0]
    mask_nat = [lo, lane >= HEAD_DIM]
    bd_ret = (sub >= HEAD_DIM) == ((lane & 32) != 0)
    bd_m = (sub >= HEAD_DIM) == (lane >= HEAD_DIM)
    le = sub <= lane
    ge = sub >= lane

    @pl.when(c == 0)
    def _():
        s_scr[...] = jnp.zeros_like(s_scr)
        c_scr[...] = jnp.zeros_like(c_scr)
        n_scr[...] = jnp.zeros_like(n_scr)
        m_scr[...] = jnp.zeros_like(m_scr)
        spos = sub.astype(F32)
        tpos = lane.astype(F32)
        diff = (lane - sub).astype(F32)
        for h in range(8):
            dm_scr[h] = (jnp.where(le, jnp.exp(lg_ref[h] * diff), 0.0)
                         + jnp.where(ge, jnp.exp(lg_ref[8 + h] * (-diff)), 0.0))
        for p in range(4):
            dec_scr[0, p] = jnp.exp(lgkt_ref[:, p:p + 1] * (tpos + 1.0))
            dec_scr[1, p] = jnp.exp(lgkt_ref[:, 4 + p:5 + p] * (float(CHUNK) - tpos))
            dec_scr[2, p] = jnp.exp(lgk_ref[p:p + 1, :] * (float(CHUNK) - 1.0 - spos))


    prev_on, next_on = _segment_flags(c, nc, ctx_chunks)
    qk = _conv_silu(mqk_ref, mqkp_ref, mqkn_ref, cw_ref, prev_on, next_on)
    g = g_ref[...]
    gt = gt_ref[...]
    lf_col = _log_sigmoid(g)
    lf_row = _log_sigmoid(gt)
    le_bf = le.astype(BF16)
    ge_bf = ge.astype(BF16)
    bal_f = pltpu.roll(_cumsum_cols(ge_bf, lf_col), LANES - 8, 1)
    bal_b = pltpu.roll(_cumsum_cols(le_bf, lf_col), LANES - 8, 1)
    cf_all = g - bal_f
    cb_all = g - bal_b
    bf_row = _cumsum_rows(lf_row, le_bf)
    bb_row = _cumsum_rows(lf_row, ge_bf)

    qb, kb, vb, kf32, qt, vbd = [], [], [], [], [], []
    for p in range(8):
        sl = slice((p % 4) * LANES, (p % 4 + 1) * LANES)
        if p < 4:
            q2, k2, v2 = rq_ref[:, sl], rk_ref[:, sl], rv_ref[:, sl]
            qf = q2.astype(F32)
            kf = None
            old =jnp.concatenate([s_scr[p].astype(BF16), sretb_ref[p]], axis=1)
        else:
            qf = qk[:, sl] * (HEAD_DIM ** -0.5)
            kf = qk[:, 4 * LANES + (p - 4) * LANES:4 * LANES + (p - 3) * LANES]
            q2, k2, v2 = qf.astype(BF16), kf.astype(BF16), mv_ref[:, sl]
            old = jnp.concatenate([c_scr[p - 4].astype(BF16), cmb_ref[p - 4]], axis=1)
        vt = v2.astype(F32).T.astype(BF16)
        qb.append(q2)
        kb.append(k2)
        vb.append(v2)
        kf32.append(kf)
        qt.append(qf.T)
        vbd.append(jnp.concatenate([jnp.where(sub_lo, vt, jnp.zeros_like(vt)),
                                    jnp.where(sub_lo, jnp.zeros_like(vt), vt), old], axis=1))

    st2, qn_f, qn_b = [], [], []
    for p in range(8):
        masks = mask_ret if p < 4 else mask_nat
        zero = jnp.zeros_like(qb[p])
        qstack = jnp.concatenate([jnp.where(masks[0], qb[p], zero), jnp.where(masks[1], qb[p], zero)], axis=0)
        st2.append(_dot_nt(kb[p], qstack))
        if p >= 4:
            qn_f.append(_dot_nt(n_scr[p - 4].astype(BF16), qb[p]))
            qn_b.append(_dot_nt(nmb_ref[p - 4], qb[p]))

    rhs = []
    for p in range(8):
        if p < 4:
            pts = [(st2[p][:, a * LANES:(a + 1) * LANES] * dm_scr[2 * p + a]).astype(BF16) for a in range(2)]
            x_f, x_b = dec_scr[0, p], dec_scr[1, p]
        else:
            pts, cf, cb = [], [], []
            for a in range(2):
                h = 2 * (p - 4) + a
                st = st2[p][:, a * LANES:(a + 1) * LANES]
                pf, coef_f = _mlstm_dir_weights(st, qn_f[p - 4][a:a + 1, :], cf_all[:, h:h + 1],
                                                bf_row[8 + h:9 + h, :], m_scr[0:1, h:h + 1], le)
                pb, coef_b = _mlstm_dir_weights(st, qn_b[p - 4][a:a + 1, :], cb_all[:, 16 + h:17 + h],
                                                bb_row[24 + h:25 + h, :], mmb_ref[0:1, 16 + h:17 + h], ge)
                pts.append((pf + pb).astype(BF16))
                cf.append(coef_f)
                cb.append(coef_b)
            x_f = jnp.where(sub_lo, cf[0], cf[1])
            x_b = jnp.where(sub_lo, cb[0], cb[1])
        rhs.append(jnp.concatenate(pts + [(qt[p] * x_f).astype(BF16), (qt[p] * x_b).astype(BF16)], axis=0))

    ht = [_dot(vbd[p], rhs[p]) for p in range(8)]

    for p in range(8):
        sl = slice((p % 4) * LANES, (p % 4 + 1) * LANES)
        gate_ref = rg_ref if p < 4 else mo_ref
        y = _heads_out(ht[p], nw_ref[p])
        out_ref[:, p * LANES:(p + 1) * LANES] = (y * gate_ref[:, sl].astype(F32)).astype(BF16)

    for p in range(4):
        cd = jnp.exp(lgk_ref[p:p + 1, :] * float(CHUNK))
        _ret_state_update(s_scr, p, kb[p], vb[p], dec_scr[2, p], cd, bd_ret)
    _mlstm_state_update(c_scr, n_scr, m_scr, kf32[4:], vb[4:], cf_all, bal_f[CHUNK - 1:CHUNK, :], 0, lo, bd_m)


def _mixer(y, g, gt, states, lg_smem, lgk, lgkt, cw_qk, nw, nb, nc, ctx_chunks):
    gw = 4 * LANES
    sretb, cmb, nmb, mmb = states
    rc = lambda b, c: b * nc + c
    hb = CHUNK // HALO
    nhalo = y.shape[0] // HALO
    blk = lambda j: pl.BlockSpec((CHUNK, gw), lambda b, c: (rc(b, c), j))
    return pl.pallas_call(
        functools.partial(_mixer_kernel, nc=nc, ctx_chunks=ctx_chunks),
        grid=(nb, nc),
        in_specs=[_smem_spec(),
                  blk(0), blk(1), blk(2), blk(3),
                  pl.BlockSpec((CHUNK, 2 * gw), lambda b, c: (rc(b, c), 2)),
                  pl.BlockSpec((HALO, 2 * gw), lambda b, c: (jnp.maximum(rc(b, c) * hb - 1, 0), 2)),
                  pl.BlockSpec((HALO, 2 * gw), lambda b, c: (jnp.minimum((rc(b, c) + 1) * hb, nhalo - 1), 2)),
                  blk(6), blk(7),
                  pl.BlockSpec((CHUNK, LANES), lambda b, c: (rc(b, c), 0)),
                  pl.BlockSpec((32, CHUNK), lambda b, c: (0, rc(b, c))),
                  pl.BlockSpec((None, 4, CHUNK, LANES), lambda b, c: (rc(b, c), 0, 0, 0)),
                  pl.BlockSpec((None, 4, CHUNK, LANES), lambda b, c: (rc(b, c), 0, 0, 0)),
                  pl.BlockSpec((None, 4, HALO, LANES), lambda b, c: (rc(b, c), 0, 0, 0)),
                  pl.BlockSpec((None, 8, LANES), lambda b, c: (rc(b, c), 0, 0)),
                  _const_spec(lgk.shape), _const_spec(lgkt.shape), _const_spec(cw_qk.shape),
                  _const_spec(nw.shape)],
        out_specs=pl.BlockSpec((CHUNK, 2 * gw), lambda b, c: (rc(b, c), 0)),
        out_shape=jax.ShapeDtypeStruct((nb * nc * CHUNK, 2 * gw), BF16),
        scratch_shapes=[pltpu.VMEM((4, CHUNK, LANES), F32), pltpu.VMEM((4, CHUNK, LANES), F32),
                        pltpu.VMEM((4, HALO, LANES), F32), pltpu.VMEM((8, LANES), F32),
                        pltpu.VMEM((8, CHUNK, LANES), F32), pltpu.VMEM((3, 4, CHUNK, LANES), F32)],
        compiler_params=_cparams("arbitrary", "arbitrary"),
        name="ret_mlstm_mixer",
    )(lg_smem, y, y, y, y, y, y, y, y, y, g, gt, sretb, cmb, nmb, mmb, lgk, lgkt, cw_qk, nw)


def _inproj_sweep_kernel(ctx_ref, x_ref, ab_ref, w_ref, wg_ref, wgt_ref, gb_ref, gbt_ref, cos_ref, sin_ref,
                         lgk_ref, cw_ref,
                         y_ref, g_ref, gt_ref, sret_ref, cm_ref, nm_ref, mm_ref, qk_ref,
                         s_scr, c_scr, n_scr, m_scr, kdec_scr,
                         p_rk, p_rv, p_mv, p_mqk, p_g, next_row_scr, *, tiles, tpb):
    i = pl.program_id(0)
    order = lambda t: jnp.where(t % tpb == 0, 0, tpb - t % tpb)
    jt_a = order(jnp.maximum(i - 1, 0))
    lane = _lane()
    sub = lax.broadcasted_iota(jnp.int32, (CHUNK, LANES), 0)
    lo = lane < HEAD_DIM
    lo_row = lo[0:1, :]
    bd_ret = (sub >= HEAD_DIM) == ((lane & 32) != 0)
    bd_m = (sub >= HEAD_DIM) == (lane >= HEAD_DIM)
    gw = 4 * LANES

    @pl.when(i == 0)
    def _():
        pos = sub.astype(F32)
        for p in range(4):
            kdec_scr[p] = jnp.exp(lgk_ref[4 + p:5 + p, :] * pos)
        for ref in (p_rk, p_rv, p_mv, p_mqk, p_g, next_row_scr):
            ref[...] = jnp.zeros_like(ref)

    @pl.when(jt_a == 0)
    def _():
        s_scr[...] = jnp.zeros_like(s_scr)
        c_scr[...] = jnp.zeros_like(c_scr)
        n_scr[...] = jnp.zeros_like(n_scr)
        m_scr[...] = jnp.zeros_like(m_scr)

    tile_i = jnp.minimum(i, tiles - 1)
    jt_i = order(tile_i)
    hb = _norm_mod(jnp.where(jt_i == 0, ctx_ref[...], x_ref[...]), ab_ref)
    cos, sin = cos_ref[...], sin_ref[...]

    def project(j):
        acc = _dot(hb, w_ref[:, j * gw:(j + 1) * gw])
        if j in (0, 1):
            if j == 0:
                acc = acc * (HEAD_DIM ** -0.5)
            acc = jnp.concatenate([_rope(acc[:, p * LANES:(p + 1) * LANES], cos, sin) for p in range(4)], axis=1)
        elif j == 3:
            acc = acc * jax.nn.sigmoid(acc)
        elif j == 7:
            acc = jax.nn.sigmoid(acc)
        return acc.astype(BF16)

    raw_q, raw_k = project(4), project(5)

    prev_on = jnp.where(jt_a <= 1, 0.0, 1.0).astype(F32)
    next_on = jnp.where((jt_a == 0) | (jt_a == tpb - 1), 0.0, 1.0).astype(F32)
    cur = p_mqk[...].astype(F32)
    n = cur.shape[0]
    row = lax.broadcasted_iota(jnp.int32, cur.shape, 0)
    prev_row = jnp.concatenate([raw_q[n - HALO:, :], raw_k[n - HALO:, :]], axis=1)[HALO - 1:HALO, :].astype(F32)
    xm = jnp.where(row == 0, prev_row * prev_on, pltpu.roll(cur, 1, 0))
    xp = jnp.where(row == n - 1, next_row_scr[0:1, :] * next_on, pltpu.roll(cur, n - 1, 0))
    conv = cw_ref[3:4, :] + cw_ref[0:1, :] * xm + cw_ref[1:2, :] * cur + cw_ref[2:3, :] * xp
    qk = conv * jax.nn.sigmoid(conv)
    qk_ref[:, 0:gw] = (qk[:, 0:gw] * (HEAD_DIM ** -0.5)).astype(BF16)
    qk_ref[:, gw:] = qk[:, gw:].astype(BF16)
    le_bf = (sub <= lane).astype(BF16)
    chunks = (1, 0)
    pre = {}
    for blk in chunks:
        rows = slice(blk * CHUNK, (blk + 1) * CHUNK)
        g = p_g[rows, :]
        bal = pltpu.roll(_cumsum_cols(le_bf, _log_sigmoid(g)), LANES - 8, 1)
        c_all = g - bal
        cmax = jnp.max(c_all, axis=0, keepdims=True)
        w_all = jnp.exp(c_all - cmax)
        vts, ks, nlocs = [], [], []
        for p in range(8):
            sl = slice((p % 4) * LANES, (p % 4 + 1) * LANES)
            if p < 4:
                v2 = p_rv[rows, sl]
                ks.append((p_rk[rows, sl].astype(F32) * kdec_scr[p]).astype(BF16))
            else:
                v2 = p_mv[rows, sl]
                h0 = 16 + 2 * (p - 4)
                kw = qk[rows, gw + (p - 4) * LANES:gw + (p - 3) * LANES] * jnp.where(
                    lo, w_all[:, h0:h0 + 1], w_all[:, h0 + 1:h0 + 2])
                ks.append(kw.astype(BF16))
                nlocs.append(jnp.sum(kw, axis=0, keepdims=True))
            vts.append(v2.astype(F32).T.astype(BF16))
        pre[blk] = (vts, ks, nlocs, cmax, bal[0:1, :])

    cur_rk = project(1)
    y_ref[:, 1 * gw:2 * gw] = cur_rk
    y_ref[:, 0:gw] = project(0)
    kvs = {blk: [_dot(pre[blk][0][p], pre[blk][1][p]) for p in range(8)] for blk in chunks}
    cur_rv = project(2)
    y_ref[:, 2 * gw:3 * gw] = cur_rv
    y_ref[:, 3 * gw:4 * gw] = project(3)

    for blk in chunks:
        _, _, nlocs, cmax, bend = pre[blk]
        sret_ref[blk] = s_scr[...].astype(BF16)
        cm_ref[blk] = c_scr[...].astype(BF16)
        nm_ref[blk] = n_scr[...].astype(BF16)
        mm_ref[blk] = m_scr[...]
        m_old = m_scr[0:1, :]
        mrel = jnp.maximum(m_old, cmax)
        a_row = jnp.exp(m_old - mrel)
        bb_row = jnp.exp(cmax - mrel)
        m_scr[0:1, :] = bend + mrel
        for p in range(4):
            cd = jnp.exp(lgk_ref[4 + p:5 + p, :] * float(CHUNK))
            s_scr[p] = s_scr[p] * cd + jnp.where(bd_ret, kvs[blk][p], 0.0)
            h0 = 16 + 2 * p
            a_l = jnp.where(lo_row, a_row[:, h0:h0 + 1], a_row[:, h0 + 1:h0 + 2])
            bb_l = jnp.where(lo_row, bb_row[:, h0:h0 + 1], bb_row[:, h0 + 1:h0 + 2])
            c_scr[p] = c_scr[p] * a_l + jnp.where(bd_m, kvs[blk][4 + p], 0.0) * bb_l
            n_new = (n_scr[p, 0:1, :] + n_scr[p, 1:2, :]) * a_l + nlocs[p] * bb_l
            n_scr[p, 0:1, :] = jnp.where(lo_row, n_new, 0.0)
            n_scr[p, 1:2, :] = jnp.where(lo_row, 0.0, n_new)

    cur_mv = project(6)
    y_ref[:, 4 * gw:5 * gw] = cur_mv
    y_ref[:, 5 * gw:6 * gw] = project(7)
    gates = _dot(hb, wg_ref[...]) + gb_ref[...]
    g_ref[...] = gates
    gt_ref[...] = _dot_nt(wgt_ref[...], hb) + gbt_ref[...]
    next_row_scr[...] = p_mqk[0:HALO, :].astype(F32)
    p_mqk[:, 0:gw] = raw_q
    p_mqk[:, gw:] = raw_k
    p_rk[...] = cur_rk
    p_rv[...] = cur_rv
    p_mv[...] = cur_mv
    p_g[...] = gates


def _inproj_sweep(ctx2, x2, ab, w, wg, wgt, gb, gbt, cos, sin, lgk, cw, nb, tpb):
    d = x2.shape[1]
    tm = ROW_TILE
    cpt = tm // CHUNK
    gw = 4 * LANES
    tiles = nb * tpb
    r = tiles * tm
    lat = tpb - 1
    seq_tile = lambda t: (t // tpb) * tpb + jnp.where(t % tpb == 0, 0, tpb - t % tpb)
    cur = lambda i: jnp.minimum(i, tiles - 1)
    tile_i = lambda i: seq_tile(cur(i))
    tile_a = lambda i: seq_tile(jnp.maximum(i - 1, 0))
    in_batch = lambda i: tile_i(i) % tpb
    sel = lambda i: ((tile_i(i) // tpb) * 2 + jnp.minimum(in_batch(i), 1), 0, 0)
    state = lambda *dims: pl.BlockSpec((cpt,) + dims, lambda i: (tile_a(i),) + (0,) * len(dims))
    nchunks = tiles * cpt
    return pl.pallas_call(
        functools.partial(_inproj_sweep_kernel, tiles=tiles, tpb=tpb),
        grid=(tiles + 1,),
        in_specs=[pl.BlockSpec((tm, d), lambda i: (tile_i(i) // tpb, 0)),
                  pl.BlockSpec((tm, d), lambda i: ((tile_i(i) // tpb) * lat + jnp.maximum(in_batch(i) - 1, 0), 0)),
                  pl.BlockSpec((None, 2, d), sel),
                  _const_spec(w.shape), _const_spec(wg.shape), _const_spec(wgt.shape),
                  _const_spec(gb.shape), _const_spec(gbt.shape),
                  pl.BlockSpec((tm, LANES), lambda i: (in_batch(i), 0)),
                  pl.BlockSpec((tm, LANES), lambda i: (in_batch(i), 0)),
                  _const_spec(lgk.shape), _const_spec(cw.shape)],
        out_specs=[pl.BlockSpec((tm, 6 * gw), lambda i: (tile_i(i), 0)),
                   pl.BlockSpec((tm, LANES), lambda i: (tile_i(i), 0)),
                   pl.BlockSpec((32, tm), lambda i: (0, tile_i(i))),
                   state(4, CHUNK, LANES), state(4, CHUNK, LANES), state(4, HALO, LANES), state(8, LANES),
                   pl.BlockSpec((tm, 2 * gw), lambda i: (tile_a(i), 0))],
        out_shape=[jax.ShapeDtypeStruct((r, 6 * gw), BF16),
                   jax.ShapeDtypeStruct((r, LANES), F32),
                   jax.ShapeDtypeStruct((32, r), F32),
                   jax.ShapeDtypeStruct((nchunks, 4, CHUNK, LANES), BF16),
                   jax.ShapeDtypeStruct((nchunks, 4, CHUNK, LANES), BF16),
                   jax.ShapeDtypeStruct((nchunks, 4, HALO, LANES), BF16),
                   jax.ShapeDtypeStruct((nchunks, 8, LANES), F32),
                   jax.ShapeDtypeStruct((r, 2 * gw), BF16)],
        scratch_shapes=[pltpu.VMEM((4, CHUNK, LANES), F32), pltpu.VMEM((4, CHUNK, LANES), F32),
                        pltpu.VMEM((4, HALO, LANES), F32), pltpu.VMEM((8, LANES), F32),
                        pltpu.VMEM((4, CHUNK, LANES), F32),
                        pltpu.VMEM((tm, gw), BF16), pltpu.VMEM((tm, gw), BF16), pltpu.VMEM((tm, gw), BF16),
                        pltpu.VMEM((tm, 2 * gw), BF16), pltpu.VMEM((tm, LANES), F32),
                        pltpu.VMEM((HALO, 2 * gw), F32)],
        compiler_params=_cparams("arbitrary"),
        name="inproj_bwd_sweep",
    )(ctx2, x2, ab, w, wg, wgt, gb, gbt, cos, sin, lgk, cw)


class _Bag:
    def __init__(self, **kw):
        self.__dict__.update(kw)


def _mixer_chunk_stages(blk, r):
    rows = slice(blk * CHUNK, (blk + 1) * CHUNK)
    lane = _lane()
    sub = lax.broadcasted_iota(jnp.int32, (CHUNK, LANES), 0)
    lo = lane < HEAD_DIM
    sub_lo = sub < HEAD_DIM
    mask_ret = [(lane & 32) == 0, (lane & 32) != 0]
    mask_nat = [lo, lane >= HEAD_DIM]
    bd_ret = (sub >= HEAD_DIM) == ((lane & 32) != 0)
    bd_m = (sub >= HEAD_DIM) == (lane >= HEAD_DIM)
    le = sub <= lane
    ge = sub >= lane

    g = r.g_ref[rows, :]
    gt = r.gt_ref[:, rows]
    lf_col = _log_sigmoid(g)
    lf_row = _log_sigmoid(gt)
    le_bf = le.astype(BF16)
    ge_bf = ge.astype(BF16)
    bal_f = pltpu.roll(_cumsum_cols(ge_bf, lf_col), LANES - 8, 1)
    bal_b = pltpu.roll(_cumsum_cols(le_bf, lf_col), LANES - 8, 1)
    cf_all = g - bal_f
    cb_all = g - bal_b
    bf_row = _cumsum_rows(lf_row, le_bf)
    bb_row = _cumsum_rows(lf_row, ge_bf)
    qb, kb, vb, kf32, qt, vbd = [], [], [], [], [], []
    for p in range(8):
        sl = slice((p % 4) * LANES, (p % 4 + 1) * LANES)
        if p < 4:
            q2, k2, v2 = r.rq_ref[rows, sl], r.rk_ref[rows, sl], r.rv_ref[rows, sl]
            qf = q2.astype(F32)
            kf = None
        else:
            q2 = r.qk_ref[rows, sl]
            k2 = r.qk_ref[rows, 4 * LANES + (p - 4) * LANES:4 * LANES + (p - 3) * LANES]
            v2 = r.mv_ref[rows, sl]
            qf, kf = q2.astype(F32), k2.astype(F32)
        vt = v2.astype(F32).T.astype(BF16)
        qb.append(q2)
        kb.append(k2)
        vb.append(v2)
        kf32.append(kf)
        qt.append(qf.T)
        vbd.append([jnp.where(sub_lo, vt, jnp.zeros_like(vt)), jnp.where(sub_lo, jnp.zeros_like(vt), vt)])
    yield

    st2 = []
    for p in range(8):
        masks = mask_ret if p < 4 else mask_nat
        zero = jnp.zeros_like(qb[p])
        qstack = jnp.concatenate([jnp.where(masks[0], qb[p], zero), jnp.where(masks[1], qb[p], zero)], axis=0)
        st2.append(_dot_nt(kb[p], qstack))
    yield

    qn_f = [_dot_nt(r.n_scr[p].astype(BF16), qb[4 + p]) for p in range(4)]
    qn_b = [_dot_nt(r.nmb_ref[blk, p], qb[4 + p]) for p in range(4)]
    lhs, rhs = [], []
    for p in range(8):
        if p < 4:
            pts = [(st2[p][:, a * LANES:(a + 1) * LANES] * r.dm_scr[2 * p + a]).astype(BF16) for a in range(2)]
            x_f, x_b = r.dec_scr[0, p], r.dec_scr[1, p]
            old = [r.s_scr[p].astype(BF16), r.sretb_ref[blk, p]]
        else:
            pts, cf, cb = [], [], []
            for a in range(2):
                h = 2 * (p - 4) + a
                st = st2[p][:, a * LANES:(a + 1) * LANES]
                pf, coef_f = _mlstm_dir_weights(st, qn_f[p - 4][a:a + 1, :], cf_all[:, h:h + 1],
                                                bf_row[8 + h:9 + h, :], r.m_state[0:1, h:h + 1], le)
                pb, coef_b = _mlstm_dir_weights(st, qn_b[p - 4][a:a + 1, :], cb_all[:, 16 + h:17 + h],
                                                bb_row[24 + h:25 + h, :], r.mmb_ref[blk, 0:1, 16 + h:17 + h], ge)
                pts.append((pf + pb).astype(BF16))
                cf.append(coef_f)
                cb.append(coef_b)
            x_f = jnp.where(sub_lo, cf[0], cf[1])
            x_b = jnp.where(sub_lo, cb[0], cb[1])
            old = [r.c_scr[p - 4].astype(BF16), r.cmb_ref[blk, p - 4]]
        lhs.append(jnp.concatenate(vbd[p] + old, axis=1))
        rhs.append(jnp.concatenate(pts + [(qt[p] * x_f).astype(BF16), (qt[p] * x_b).astype(BF16)], axis=0))
    yield

    ht = [_dot(lhs[p], rhs[p]) for p in range(8)]
    yield

    for p in range(8):
        sl = slice((p % 4) * LANES, (p % 4 + 1) * LANES)
        gate_ref = r.rg_ref if p < 4 else r.mo_ref
        y = _heads_out(ht[p], r.nw_ref[p])
        new = (y * gate_ref[rows, sl].astype(F32)).astype(BF16)
        r.mix_scr[rows, p * LANES:(p + 1) * LANES] = jnp.where(r.live, new, r.mix_scr[rows, p * LANES:(p + 1) * LANES])
    yield

    for p in range(4):
        cd = jnp.exp(r.lgk_ref[p:p + 1, :] * float(CHUNK))
        _ret_state_update(r.s_scr, p, kb[p], vb[p], r.dec_scr[2, p], cd, bd_ret)
    _mlstm_state_update(r.c_scr, r.n_scr, r.m_state, kf32[4:], vb[4:], cf_all, bal_f[CHUNK - 1:CHUNK, :],
                        0, lo, bd_m)
    yield


def _attn_inproj_stages(x, ab_ref, w_ref, nw_ref, cos, sin, y_ref):
    hb = _norm_mod(x, ab_ref)
    r2 = lax.broadcasted_iota(jnp.int32, (2 * LANES, 2 * LANES), 0)
    c2 = lax.broadcasted_iota(jnp.int32, (2 * LANES, 2 * LANES), 1)
    same_head = (((r2 ^ c2) & (LANES | 32)) == 0).astype(BF16)
    acc_q = _dot(hb, w_ref[:, 0:8 * LANES])
    acc_k = _dot(hb, w_ref[:, 8 * LANES:10 * LANES])
    y_ref[:, 10 * LANES:12 * LANES] = _dot(hb, w_ref[:, 10 * LANES:12 * LANES]).astype(BF16)
    yield
    for j in range(5):
        acc = acc_q[:, j * 2 * LANES:(j + 1) * 2 * LANES] if j < 4 else acc_k
        sq = acc * acc
        hi = sq.astype(BF16)
        lo = (sq - hi.astype(F32)).astype(BF16)
        ms = (_dot(hi, same_head) + _dot(lo, same_head)) * (1.0 / HEAD_DIM)
        nrm = acc * lax.rsqrt(ms + EPS)
        nw = nw_ref[0:1, :] if j < 4 else nw_ref[1:2, :]
        for v in range(2):
            ls = slice(v * LANES, (v + 1) * LANES)
            y_ref[:, (2 * j + v) * LANES:(2 * j + v + 1) * LANES] = _rope(nrm[:, ls] * nw, cos, sin).astype(BF16)
        if j in (1, 4):
            yield


def _ffn_splits(d_ff, pieces):
    blocks = d_ff // (2 * LANES)
    assert blocks * 2 * LANES == d_ff and blocks >= pieces
    cuts = [((i * blocks) // pieces) * 2 * LANES for i in range(pieces)]
    return cuts + [d_ff]


def _mixer_ffn_kernel(lg_ref, rq_ref, rk_ref, rv_ref, rg_ref, qk_ref, mv_ref, mo_ref,
                      g_ref, gt_ref, sretb_ref, cmb_ref, nmb_ref, mmb_ref,
                      lgk_ref, lgkt_ref, nw_ref,
                      ctx_ref, x_ref, p_ref, wo_ref, wi_ref, w2_ref,
                      ab2_ref, wa_ref, nwa_ref, cos_ref, sin_ref,
                      o_ref, y2_ref,
                      s_scr, c_scr, n_scr, m_state, dm_scr, dec_scr, mix_scr, act_scr, x2_scr,
                      *, tiles, tpb, d_ff):
    s = pl.program_id(0)
    jt = jnp.minimum(s, tiles - 1) % tpb
    lane = _lane()
    sub = lax.broadcasted_iota(jnp.int32, (CHUNK, LANES), 0)

    @pl.when(s == 0)
    def _():
        mix_scr[...] = jnp.zeros_like(mix_scr)
        x2_scr[...] = jnp.zeros_like(x2_scr)
        le = sub <= lane
        ge = sub >= lane
        spos = sub.astype(F32)
        tpos = lane.astype(F32)
        diff = (lane - sub).astype(F32)
        for h in range(8):
            dm_scr[h] = (jnp.where(le, jnp.exp(lg_ref[h] * diff), 0.0)
                         + jnp.where(ge, jnp.exp(lg_ref[8 + h] * (-diff)), 0.0))
        for p in range(4):
            dec_scr[0, p] = jnp.exp(lgkt_ref[:, p:p + 1] * (tpos + 1.0))
            dec_scr[1, p] = jnp.exp(lgkt_ref[:, 4 + p:5 + p] * (float(CHUNK) - tpos))
            dec_scr[2, p] = jnp.exp(lgk_ref[p:p + 1, :] * (float(CHUNK) - 1.0 - spos))

    @pl.when(jt == 0)
    def _():
        s_scr[...] = jnp.zeros_like(s_scr)
        c_scr[...] = jnp.zeros_like(c_scr)
        n_scr[...] = jnp.zeros_like(n_scr)
        m_state[...] = jnp.zeros_like(m_state)

    r = _Bag(rq_ref=rq_ref, rk_ref=rk_ref, rv_ref=rv_ref, rg_ref=rg_ref, qk_ref=qk_ref, mv_ref=mv_ref, mo_ref=mo_ref,
             g_ref=g_ref, gt_ref=gt_ref, sretb_ref=sretb_ref, cmb_ref=cmb_ref, nmb_ref=nmb_ref, mmb_ref=mmb_ref,
             lgk_ref=lgk_ref, nw_ref=nw_ref, s_scr=s_scr, c_scr=c_scr, n_scr=n_scr, m_state=m_state,
             dm_scr=dm_scr, dec_scr=dec_scr, mix_scr=mix_scr, live=s < tiles)
    cuts = _ffn_splits(d_ff, 4)
    ffn_piece = lambda i: _ffn_cols(h, wi_ref, act_scr, cuts[i], cuts[i + 1], d_ff)

    x = _ctx_or_latent_rows(ctx_ref, x_ref, jnp.clip(s - 1, 0, tiles - 1), tpb)
    x1, h = _ffn_in(x, mix_scr[...], p_ref, wo_ref)

    nxt = _attn_inproj_stages(x2_scr[...], ab2_ref, wa_ref, nwa_ref, cos_ref[...], sin_ref[...], y2_ref)
    chunk_a, chunk_b = _mixer_chunk_stages(0, r), _mixer_chunk_stages(1, r)
    next(chunk_a), next(chunk_b)
    next(chunk_a), next(chunk_b)
    ffn_piece(0)
    next(chunk_a), next(chunk_a)
    next(nxt)
    ffn_piece(1)
    next(chunk_a), next(chunk_a)
    next(nxt)
    ffn_piece(2)
    next(chunk_b), next(chunk_b)
    next(nxt)
    ffn_piece(3)
    next(chunk_b), next(chunk_b)
    x2 = x1 + p_ref[3:4, :] * _dot(act_scr[...], w2_ref[...])
    o_ref[...] = x2
    x2_scr[...] = x2


def _mixer_ffn(y, g, gt, states, lg_smem, lgk, lgkt, nw, xs, prm, wo, wi_all, w2_all, layer,
               ab_next, w_next, nw_next, cos, sin, nb, tpb):
    gw = 4 * LANES
    tm = ROW_TILE
    cpt = tm // CHUNK
    sretb, cmb, nmb, mmb, qk_act = states
    tiles = nb * tpb
    d = xs[-1].shape[1]
    d_ff = w2_all.shape[1]
    n_next = w_next.shape[1]
    mix = lambda s: jnp.minimum(s, tiles - 1)
    ffn = lambda s: jnp.clip(s - 1, 0, tiles - 1)
    nxt = lambda s: jnp.maximum(s - 2, 0)
    blk = lambda j: pl.BlockSpec((tm, gw), lambda s: (mix(s), j))
    state_spec = lambda a: pl.BlockSpec((cpt,) + a.shape[1:], lambda s: (mix(s),) + (0,) * (a.ndim - 1))
    sel = lambda t: ((t // tpb) * 2 + jnp.minimum(t % tpb, 1), 0, 0)
    layer_spec = lambda a: pl.BlockSpec((None,) + a.shape[1:], lambda s: (layer, 0, 0),
                                        pipeline_mode=pl.Buffered(1))
    return pl.pallas_call(
        functools.partial(_mixer_ffn_kernel, tiles=tiles, tpb=tpb, d_ff=d_ff),
        grid=(tiles + 2,),
        in_specs=[_smem_spec(),
                  blk(0), blk(1), blk(2), blk(3),
                  pl.BlockSpec((tm, 2 * gw), lambda s: (mix(s), 0)),
                  blk(4), blk(5),
                  pl.BlockSpec((tm, LANES), lambda s: (mix(s), 0)),
                  pl.BlockSpec((32, tm), lambda s: (0, mix(s))),
                  state_spec(sretb), state_spec(cmb), state_spec(nmb), state_spec(mmb),
                  _const_spec(lgk.shape), _const_spec(lgkt.shape), _const_spec(nw.shape)]
                 + _split_row_specs(tm, d, tpb, ffn)
                 + [pl.BlockSpec((None, 8, d), lambda s: sel(ffn(s))),
                    _const_spec(wo.shape), layer_spec(wi_all), layer_spec(w2_all),
                    pl.BlockSpec((None, 2, d), lambda s: sel(nxt(s))),
                    _const_spec(w_next.shape), _const_spec(nw_next.shape),
                    pl.BlockSpec((tm, LANES), lambda s: (nxt(s) % tpb, 0)),
                    pl.BlockSpec((tm, LANES), lambda s: (nxt(s) % tpb, 0))],
        out_specs=[pl.BlockSpec((tm, d), lambda s: (ffn(s), 0)),
                   pl.BlockSpec((tm, n_next), lambda s: (nxt(s), 0))],
        out_shape=[jax.ShapeDtypeStruct((tiles * tm, d), F32),
                   jax.ShapeDtypeStruct((tiles * tm, n_next), BF16)],
        scratch_shapes=[pltpu.VMEM((4, CHUNK, LANES), F32), pltpu.VMEM((4, CHUNK, LANES), F32),
                        pltpu.VMEM((4, HALO, LANES), F32), pltpu.VMEM((8, LANES), F32),
                        pltpu.VMEM((8, CHUNK, LANES), F32), pltpu.VMEM((3, 4, CHUNK, LANES), F32),
                        pltpu.VMEM((tm, 2 * gw), BF16), pltpu.VMEM((tm, d_ff), BF16),
                        pltpu.VMEM((tm, d), F32)],
        compiler_params=_cparams("arbitrary"),
        name="ret_mlstm_mixer_ffn",
    )(lg_smem, y, y, y, y, qk_act, y, y, g, gt, sretb, cmb, nmb, mmb, lgk, lgkt, nw,
      *xs, prm, wo, wi_all, w2_all, ab_next, w_next, nw_next, cos, sin)


def _attn_kernel(sink_ref, q_ref, kp_ref, kc_ref, kn_ref, kx_ref, vp_ref, vc_ref, vn_ref, vx_ref,
                 bias_ref, o_ref):
    grp = lax.broadcasted_iota(jnp.int32, (1, 4 * CHUNK), 1) // CHUNK
    lane = _lane()
    mask_q = [(lane & 32) == 0, (lane & 32) != 0]
    bias = jnp.concatenate([bias_ref[...]] * 4, axis=1)
    vts, sts = [], []
    for kvp in range(2):
        sl = slice(kvp * LANES, (kvp + 1) * LANES)
        v_blocks = [vp_ref[:, sl], vc_ref[:, sl], vn_ref[:, sl], vx_ref[:, sl]]
        vts.append(jnp.concatenate(
            [vb[r * CHUNK:(r + 1) * CHUNK, :].astype(F32).T.astype(BF16)
             for vb in v_blocks for r in range(vb.shape[0] // CHUNK)], axis=1))
    for kvp in range(2):
        sl = slice(kvp * LANES, (kvp + 1) * LANES)
        kcat = jnp.concatenate([kp_ref[:, sl], kc_ref[:, sl], kn_ref[:, sl], kx_ref[:, sl]], axis=0)
        for a in range(2):
            qs = jnp.concatenate(
                [jnp.where(mask_q[a], q_ref[:, (kvp * 4 + g) * LANES:(kvp * 4 + g + 1) * LANES],
                           jnp.zeros((CHUNK, LANES), BF16)) for g in range(4)], axis=0)
            sts.append(_dot_nt(kcat, qs))
    outs = []
    for kv in range(4):
        st = sts[kv]
        st = jnp.concatenate([st[0:CHUNK] + bias[0:CHUNK], st[CHUNK:2 * CHUNK],
                              st[2 * CHUNK:3 * CHUNK] + bias[CHUNK:2 * CHUNK], st[3 * CHUNK:]], axis=0)
        snk = jnp.where(grp == 0, sink_ref[kv * 4],
                        jnp.where(grp == 1, sink_ref[kv * 4 + 1],
                                  jnp.where(grp == 2, sink_ref[kv * 4 + 2], sink_ref[kv * 4 + 3])))
        m = jnp.maximum(jnp.max(st, axis=0, keepdims=True), snk)
        e = jnp.exp(st - m)
        denom = jnp.exp(snk - m) + jnp.sum(e, axis=0, keepdims=True)
        a = kv % 2
        outs.append(_dot(vts[kv // 2][a * HEAD_DIM:(a + 1) * HEAD_DIM, :], e.astype(BF16)) * (1.0 / denom))
    for kvp in range(2):
        full = jnp.concatenate(outs[2 * kvp:2 * kvp + 2], axis=0)
        for g in range(4):
            o_ref[:, (kvp * 4 + g) * LANES:(kvp * 4 + g + 1) * LANES] = (
                full[:, g * CHUNK:(g + 1) * CHUNK].T.astype(BF16))


def _window_bias():
    kk = np.arange(CHUNK)[:, None]
    t = np.arange(CHUNK)[None, :]
    tabs = []
    for has_prev, has_next in ((False, True), (True, True), (True, False)):
        prev_ok = (kk >= t) & has_prev
        next_ok = (kk <= t) & has_next
        tabs.append(np.where(np.concatenate([prev_ok, next_ok], axis=0), 0.0, NEG))
    return jnp.asarray(np.stack(tabs), F32)


def _window_attn(y, sink, nb, nc, ctx_chunks):
    nq = nc - ctx_chunks
    assert nq >= 2
    ctx_len = ctx_chunks * CHUNK
    bias = _window_bias()
    d = 8 * LANES
    kcol, vcol = d // (2 * LANES), d // (2 * LANES) + 1
    qrow = lambda b, i: b * nc + ctx_chunks + i
    prow = lambda b, i: b * nc + ctx_chunks + jnp.maximum(i - 1, 0)
    nrow = lambda b, i: b * nc + ctx_chunks + jnp.minimum(i + 1, nq - 1)
    xrow = lambda b, i: (b * nc * CHUNK) // ctx_len
    kv_spec = lambda rowf, col: pl.BlockSpec((CHUNK, 2 * LANES), lambda b, i: (rowf(b, i), col))
    x_spec = lambda col: pl.BlockSpec((ctx_len, 2 * LANES), lambda b, i: (xrow(b, i), col))
    bias_spec = pl.BlockSpec((None,) + bias.shape[1:],
                             lambda b, i: (jnp.where(i == 0, 0, jnp.where(i == nq - 1, 2, 1)), 0, 0))
    return pl.pallas_call(
        _attn_kernel,
        grid=(nb, nq),
        in_specs=[_smem_spec(),
                  pl.BlockSpec((CHUNK, d), lambda b, i: (qrow(b, i), 0)),
                  kv_spec(prow, kcol), kv_spec(qrow, kcol), kv_spec(nrow, kcol), x_spec(kcol),
                  kv_spec(prow, vcol), kv_spec(qrow, vcol), kv_spec(nrow, vcol), x_spec(vcol),
                  bias_spec],
        out_specs=pl.BlockSpec((CHUNK, d), lambda b, i: (b * nq + i, 0)),
        out_shape=jax.ShapeDtypeStruct((nb * nq * CHUNK, d), BF16),
        compiler_params=_cparams("arbitrary", "arbitrary"),
        name="window_gqa",
    )(sink, y, y, y, y, y, y, y, y, y, bias)


def _ffn_in(x, m, p_ref, wo_ref):
    x1 = x + p_ref[0:1, :] * _dot(m, wo_ref[...])
    ms = jnp.mean(x1 * x1, axis=-1, keepdims=True)
    return x1, ((x1 * lax.rsqrt(ms + EPS)) * p_ref[1:2, :] + p_ref[2:3, :]).astype(BF16)


def _ffn_cols(h, wi_ref, act_scr, lo, hi, d_ff):
    gate = _dot(h, wi_ref[:, lo:hi])
    up = _dot(h, wi_ref[:, d_ff + lo:d_ff + hi])
    act_scr[:, lo:hi] = (gate * jax.nn.sigmoid(gate) * up).astype(BF16)


def _attn_ffn_kernel(sink_ref, q_ref, kp_ref, kc_ref, kn_ref, kx_ref, vp_ref, vc_ref, vn_ref, vx_ref, bias_ref,
                     x_ref, p_ref, wo_ref, wi_ref, w2_ref, o_ref, m_scr, act_scr, *, tiles, lat_tiles, d_ff):
    s = pl.program_id(0)

    @pl.when(s == 0)
    def _():
        m_scr[...] = jnp.zeros_like(m_scr)

    j = jnp.minimum(s, tiles - 1) % lat_tiles
    grp = lax.broadcasted_iota(jnp.int32, (1, 4 * CHUNK), 1) // CHUNK
    lane = _lane()
    mask_q = [(lane & 32) == 0, (lane & 32) != 0]
    tile4 = lambda b: jnp.concatenate([b] * 4, axis=1)
    biases = [tile4(jnp.where(j == 0, bias_ref[0], bias_ref[1])),
              tile4(jnp.where(j == lat_tiles - 1, bias_ref[2], bias_ref[1]))]
    cuts = _ffn_splits(d_ff, 6)
    d = o_ref.shape[1]

    def ffn_up(i):
        _ffn_cols(h, wi_ref, act_scr, cuts[i], cuts[i + 1], d_ff)

    def ffn_down(i):
        cs = slice(i * d // 4, (i + 1) * d // 4)
        o_ref[:, cs] = x1[:, cs] + p_ref[3:4, cs] * _dot(act_scr[...], w2_ref[:, cs])

    def scores(u):
        blk, kvp, a = u // 4, (u % 4) // 2, u % 2
        rows = slice(blk * CHUNK, (blk + 1) * CHUNK)
        qs = jnp.concatenate(
            [jnp.where(mask_q[a], q_ref[rows, (kvp * 4 + g) * LANES:(kvp * 4 + g + 1) * LANES],
                       jnp.zeros((CHUNK, LANES), BF16)) for g in range(4)], axis=0)
        return _dot_nt(kcats[kvp][blk], qs)

    def softmax_pv(u, st):
        blk, kv = u // 4, u % 4
        bias = biases[blk]
        st = jnp.concatenate([st[0:CHUNK] + bias[0:CHUNK], st[CHUNK:2 * CHUNK],
                              st[2 * CHUNK:3 * CHUNK] + bias[CHUNK:2 * CHUNK], st[3 * CHUNK:]], axis=0)
        snk = jnp.where(grp == 0, sink_ref[kv * 4],
                        jnp.where(grp == 1, sink_ref[kv * 4 + 1],
                                  jnp.where(grp == 2, sink_ref[kv * 4 + 2], sink_ref[kv * 4 + 3])))
        m = jnp.maximum(jnp.max(st, axis=0, keepdims=True), snk)
        e = jnp.exp(st - m)
        denom = jnp.exp(snk - m) + jnp.sum(e, axis=0, keepdims=True)
        a = kv % 2
        return _dot(vts[kv // 2][blk][a * HEAD_DIM:(a + 1) * HEAD_DIM, :], e.astype(BF16)) * (1.0 / denom)

    def hand_over(blk, outs):
        for kvp in range(2):
            full = jnp.concatenate(outs[2 * kvp:2 * kvp + 2], axis=0)
            for g in range(4):
                m_scr[blk * CHUNK:(blk + 1) * CHUNK, (kvp * 4 + g) * LANES:(kvp * 4 + g + 1) * LANES] = (
                    full[:, g * CHUNK:(g + 1) * CHUNK].T.astype(BF16))

    x1, h = _ffn_in(x_ref[...], m_scr[...], p_ref, wo_ref)

    kcats, vts = [], []
    for kvp in range(2):
        sl = slice(kvp * LANES, (kvp + 1) * LANES)
        k_chunks = [kp_ref[:, sl], kc_ref[0:CHUNK, sl], kc_ref[CHUNK:2 * CHUNK, sl], kn_ref[:, sl]]
        v_chunks = [vp_ref[:, sl], vc_ref[0:CHUNK, sl], vc_ref[CHUNK:2 * CHUNK, sl], vn_ref[:, sl],
                    vx_ref[0:CHUNK, sl], vx_ref[CHUNK:2 * CHUNK, sl]]
        v_t = [v.astype(F32).T.astype(BF16) for v in v_chunks]
        kcats.append([jnp.concatenate(k_chunks[b:b + 3] + [kx_ref[:, sl]], axis=0) for b in range(2)])
        vts.append([jnp.concatenate(v_t[b:b + 3] + v_t[4:], axis=1) for b in range(2)])

    ffn_pieces = [functools.partial(ffn_up, i) for i in range(6)] + [functools.partial(ffn_down, i) for i in range(2)]
    sts = {0: scores(0)}
    outs = []
    for u in range(8):
        if u + 1 < 8:
            sts[u + 1] = scores(u + 1)
        ffn_pieces[u]()
        outs.append(softmax_pv(u, sts.pop(u)))
        if u % 4 == 3:
            hand_over(u // 4, outs[u - 3:u + 1])
    ffn_down(2)
    ffn_down(3)


def _attn_ffn(y, sink, xc, prm, wo, wi_all, w2_all, layer, nb, nc, ctx_chunks):
    tm = ROW_TILE
    cpt = tm // CHUNK
    tpb = nc // cpt
    lat_tiles = (nc - ctx_chunks) // cpt
    tiles = nb * lat_tiles
    ctx_tiles = ctx_chunks // cpt
    assert ctx_tiles == 1 and lat_tiles >= 2
    d = xc.shape[1]
    d_ff = w2_all.shape[1]
    bias = _window_bias()
    kcol, vcol = d // (2 * LANES), d // (2 * LANES) + 1
    att = lambda s: jnp.minimum(s, tiles - 1)
    ffn = lambda s: jnp.maximum(s - 1, 0)
    row_tile = lambda t: (t // lat_tiles) * tpb + ctx_tiles + t % lat_tiles
    chunk0 = lambda t: (t // lat_tiles) * nc + ctx_chunks
    prev_c = lambda s: chunk0(att(s)) + jnp.maximum((att(s) % lat_tiles) * cpt - 1, 0)
    next_c = lambda s: chunk0(att(s)) + jnp.minimum((att(s) % lat_tiles) * cpt + cpt, lat_tiles * cpt - 1)
    cur_spec = lambda col: pl.BlockSpec((tm, 2 * LANES), lambda s: (row_tile(att(s)), col))
    edge_spec = lambda f, col: pl.BlockSpec((CHUNK, 2 * LANES), lambda s: (f(s), col))
    ctx_spec = lambda col: pl.BlockSpec((tm, 2 * LANES), lambda s: ((att(s) // lat_tiles) * tpb, col))
    layer_spec = lambda a: pl.BlockSpec((None,) + a.shape[1:], lambda s: (layer, 0, 0),
                                        pipeline_mode=pl.Buffered(1))
    return pl.pallas_call(
        functools.partial(_attn_ffn_kernel, tiles=tiles, lat_tiles=lat_tiles, d_ff=d_ff),
        grid=(tiles + 1,),
        in_specs=[_smem_spec(),
                  pl.BlockSpec((tm, d), lambda s: (row_tile(att(s)), 0)),
                  edge_spec(prev_c, kcol), cur_spec(kcol), edge_spec(next_c, kcol), ctx_spec(kcol),
                  edge_spec(prev_c, vcol), cur_spec(vcol), edge_spec(next_c, vcol), ctx_spec(vcol),
                  _const_spec(bias.shape),
                  pl.BlockSpec((tm, d), lambda s: (row_tile(ffn(s)), 0)),
                  pl.BlockSpec((None, 8, d), lambda s: ((ffn(s) // lat_tiles) * 2 + 1, 0, 0)),
                  _const_spec(wo.shape), layer_spec(wi_all), layer_spec(w2_all)],
        out_specs=pl.BlockSpec((tm, d), lambda s: (ffn(s), 0)),
        out_shape=jax.ShapeDtypeStruct((tiles * tm, d), F32),
        scratch_shapes=[pltpu.VMEM((tm, d), BF16), pltpu.VMEM((tm, d_ff), BF16)],
        compiler_params=_cparams("arbitrary"),
        name="window_gqa_ffn",
    )(sink, y, y, y, y, y, y, y, y, y, bias, xc, prm, wo, wi_all, w2_all)


def _post_kernel(*refs, d_ff, split_tpb):
    tps = TILES_PER_STEP
    n_x = 2 if split_tpb else 1
    x_refs, refs = refs[:n_x * tps], refs[n_x * tps:]
    m_refs, p_refs = refs[:tps], refs[tps:2 * tps]
    wo_ref, wi_ref, w2_ref, o_ref, act_scr = refs[2 * tps:]
    tm = m_refs[0].shape[0]
    ys = [_dot(m_refs[u][...], wo_ref[...]) for u in range(tps)]
    x1s, hs = [], []
    for u in range(tps):
        if split_tpb:
            x = _ctx_or_latent_rows(x_refs[2 * u], x_refs[2 * u + 1], pl.program_id(0) * tps + u, split_tpb)
        else:
            x = x_refs[u][...]
        p_ref = p_refs[u]
        x1 = x + p_ref[0:1, :] * ys[u]
        ms = jnp.mean(x1 * x1, axis=-1, keepdims=True)
        hs.append(((x1 * lax.rsqrt(ms + EPS)) * p_ref[1:2, :] + p_ref[2:3, :]).astype(BF16))
        x1s.append(x1)
    for u in range(tps):
        gate = _dot(hs[u], wi_ref[:, :d_ff])
        up = _dot(hs[u], wi_ref[:, d_ff:])
        act_scr[u] = (gate * jax.nn.sigmoid(gate) * up).astype(BF16)
    for u in range(tps):
        o_ref[u * tm:(u + 1) * tm, :] = x1s[u] + p_refs[u][3:4, :] * _dot(act_scr[u], w2_ref[...])


def _post(xs, m2, prm, wo, wi_all, w2_all, layer, tiles_per_batch, skip_tiles, m_has_ctx):
    split = len(xs) == 2
    d = xs[-1].shape[1]
    r = sum(a.shape[0] for a in xs)
    tm = ROW_TILE
    tps = TILES_PER_STEP
    d_ff = w2_all.shape[1]
    tpb = tiles_per_batch
    kept = tpb - skip_tiles
    nb = r // (tm * tpb)
    assert not (split and skip_tiles) and (nb * kept) % tps == 0
    xrow = lambda t: (t // kept) * tpb + skip_tiles + t % kept
    mrow = xrow if m_has_ctx else (lambda t: t)
    sel = lambda t: ((t // kept) * 2 + jnp.minimum(skip_tiles + t % kept, 1), 0, 0)
    at = lambda f, u: (lambda i: f(i * tps + u))
    row_spec = lambda f, u: pl.BlockSpec((tm, d), lambda i: (f(i * tps + u), 0))
    x_specs = []
    for u in range(tps):
        x_specs += _split_row_specs(tm, d, tpb, at(lambda t: t, u)) if split else [row_spec(xrow, u)]
    layer_spec = lambda a: pl.BlockSpec((None,) + a.shape[1:], lambda i: (layer, 0, 0),
                                        pipeline_mode=pl.Buffered(1))
    return pl.pallas_call(
        functools.partial(_post_kernel, d_ff=d_ff, split_tpb=tpb if split else 0),
        grid=(nb * kept // tps,),
        in_specs=x_specs + [row_spec(mrow, u) for u in range(tps)]
                 + [pl.BlockSpec((None, 8, d), at(sel, u)) for u in range(tps)]
                 + [_const_spec(wo.shape), layer_spec(wi_all), layer_spec(w2_all)],
        out_specs=pl.BlockSpec((tps * tm, d), lambda i: (i, 0)),
        out_shape=jax.ShapeDtypeStruct((nb * kept * tm, d), F32),
        scratch_shapes=[pltpu.VMEM((tps, tm, d_ff), BF16)],
        compiler_params=_cparams("arbitrary"),
        name="outproj_swiglu",
    )(*(list(xs) * tps), *([m2] * tps), *([prm] * tps), wo, wi_all, w2_all)


def _pair_cols(w):
    rows, cols = w.shape
    return w.reshape(rows, cols // LANES, 2, 2, 32).transpose(0, 1, 3, 2, 4).reshape(rows, cols)


def _attn_q_cols(w):
    rows = w.shape[0]
    g_per = w.shape[1] // (H_KV * HEAD_DIM)
    return (w.reshape(rows, H_KV // 2, 2, g_per, 2, 32).transpose(0, 1, 3, 4, 2, 5)
            .reshape(rows, w.shape[1]))


def _attn_o_rows(w):
    cols = w.shape[1]
    g_per = w.shape[0] // (H_KV * HEAD_DIM)
    return (w.reshape(H_KV // 2, 2, g_per, HEAD_DIM, cols).transpose(0, 2, 1, 3, 4)
            .reshape(w.shape[0], cols))


def _rope_tables(seq, ctx_len):
    rows = seq // GRID_W
    row = np.repeat(np.arange(rows, dtype=np.float32), GRID_W)
    col = np.tile(np.arange(GRID_W, dtype=np.float32), rows)
    n = HEAD_DIM // 4
    inv = (np.float32(ROPE_BASE) ** (-np.arange(n, dtype=np.float32) / np.float32(n))).astype(np.float32)
    ang = np.concatenate([row[:, None] * inv, col[:, None] * inv], axis=-1).astype(np.float32)
    cos, sin = np.cos(ang), np.sin(ang)
    cos_t = np.concatenate([np.ones((ctx_len, LANES), np.float32), np.tile(cos, (1, 4))], axis=0)
    sin_t = np.concatenate([np.zeros((ctx_len, LANES), np.float32),
                            np.concatenate([-sin, -sin, sin, sin], axis=-1)], axis=0)
    return jnp.asarray(cos_t, F32), jnp.asarray(sin_t, F32)


def _mod_tables(mod, nb, norm_w):
    d = norm_w.shape[-1]
    lat = mod[:nb].reshape(nb, 6, d)
    ctx = jnp.broadcast_to(mod[nb].reshape(1, 6, d), (nb, 6, d))
    both = jnp.stack([ctx, lat], axis=1).reshape(nb * 2, 6, d)
    sh1, sc1, g1, sh2, sc2, g2 = [both[:, k] for k in range(6)]
    ab1 = jnp.stack([norm_w[0] * (1.0 + sc1), sh1], axis=1)
    zeros = jnp.zeros_like(g1)
    prm = jnp.stack([g1, norm_w[1] * (1.0 + sc2), sh2, g2, zeros, zeros, zeros, zeros], axis=1)
    return ab1, prm


def kernel(x, c, ctx, c_ctx, ada_w, ada_b, norm_w, ffn_w_in, ffn_w_out, ab_w_in, ab_w_out,
           ret_log_gamma, ret_norm_w, mlstm_conv_w, mlstm_conv_b, mlstm_gate_b, mlstm_norm_w,
           attn_w_in, attn_w_out, attn_q_norm_w, attn_k_norm_w, attn_sink):
    nb, seq, d = x.shape
    ctx_len = ctx.shape[1]
    depth = ada_w.shape[0]
    assert ctx_len == ROW_TILE and seq % ROW_TILE == 0 and d == 8 * LANES and nb < 8
    t_all = ctx_len + seq
    nc = t_all // CHUNK
    ctx_chunks = ctx_len // CHUNK
    tpb = t_all // ROW_TILE
    dr = d // 2

    rows = jnp.zeros((8, d), F32).at[:nb].set(c).at[nb].set(c_ctx)
    mod_all = _modulation(rows, ada_w, ada_b)
    cos_t, sin_t = _rope_tables(seq, ctx_len)
    wi_all = ffn_w_in.astype(BF16)
    w2_all = ffn_w_out.astype(BF16)
    xs = (ctx.reshape(nb * ctx_len, d), x.reshape(nb * seq, d))

    assert depth == 2 and ab_w_in.shape[0] == 1 and attn_w_in.shape[0] == 1
    ab_0, prm_0 = _mod_tables(mod_all[0], nb, norm_w[0])
    ab_1, prm_1 = _mod_tables(mod_all[1], nb, norm_w[1])

    w = ab_w_in[0]
    w_main = jnp.concatenate([_pair_cols(w[:, :dr]), _pair_cols(w[:, dr:2 * dr]), w[:, 2 * dr:8 * dr]],
                             axis=1).astype(BF16)
    wg = jnp.zeros((d, LANES), F32).at[:, :32].set(w[:, 8 * dr:]).astype(BF16)
    wgt = w[:, 8 * dr:].T.astype(BF16)
    gb = jnp.zeros((1, LANES), F32).at[0, :32].set(mlstm_gate_b[0].reshape(-1))
    gbt = mlstm_gate_b[0].reshape(32, 1)
    lg = ret_log_gamma[0].astype(F32)
    lgk = jnp.tile(jnp.repeat(lg.reshape(2, 4, 2), 32, axis=-1), (1, 1, 2)).reshape(8, LANES)
    cw = jnp.concatenate([mlstm_conv_w[0], mlstm_conv_b[0][None], jnp.zeros((4, 2 * dr), F32)], axis=0)
    nw = jnp.broadcast_to(jnp.concatenate([ret_norm_w[0], mlstm_norm_w[0]]).reshape(8, LANES, 1),
                          (8, LANES, LANES))
    wo_0 = ab_w_out[0].astype(BF16)

    w = attn_w_in[0]
    w_attn = jnp.concatenate([_attn_q_cols(w[:, :d]), _pair_cols(w[:, d:d + 2 * LANES]), w[:, d + 2 * LANES:]],
                             axis=1).astype(BF16)
    lane_w = lambda v: jnp.concatenate([v[:32], v[:32], v[32:], v[32:]])
    nwq = jnp.stack([lane_w(attn_q_norm_w[0]) * (HEAD_DIM ** -0.5), lane_w(attn_k_norm_w[0])]
                    + [jnp.zeros((LANES,), F32)] * 6)
    wo_1 = _attn_o_rows(attn_w_out[0]).astype(BF16)

    y, g, gt, *states = _inproj_sweep(*xs, ab_0, w_main, wg, wgt, gb, gbt, cos_t, sin_t, lgk, cw, nb, tpb)
    x_mid, y_attn = _mixer_ffn(y, g, gt, states, lg.reshape(-1), lgk, lgk.T, nw, xs, prm_0, wo_0, wi_all, w2_all, 0,
                               ab_1, w_attn, nwq, cos_t, sin_t, nb, tpb)
    out = _attn_ffn(y_attn, attn_sink[0].astype(F32), x_mid, prm_1, wo_1, wi_all, w2_all, 1, nb, nc, ctx_chunks)
    return out.reshape(nb, seq, d)
```

```python
import functools

import numpy as np
import jax
import jax.numpy as jnp
from jax import lax
from jax.experimental import pallas as pl
from jax.experimental.pallas import tpu as pltpu

F32 = jnp.float32
BF16 = jnp.bfloat16

HEAD_DIM = 64
CHUNK = 128
GRID_W = 64
ROPE_BASE = 10000.0
EPS = 1e-6
H_KV = 4
LANES = 128
ROW_TILE = 256
HALO = 16
NEG = -1e30
VMEM_LIMIT = 56 * 1024 * 1024


def _cparams(*sem):
    return pltpu.CompilerParams(dimension_semantics=sem, vmem_limit_bytes=VMEM_LIMIT)


def _const_spec(shape):
    nd = len(shape)
    return pl.BlockSpec(shape, lambda *_: (0,) * nd, pipeline_mode=pl.Buffered(1))


def _smem_spec():
    return pl.BlockSpec(memory_space=pltpu.SMEM)


def _lane(shape=(CHUNK, LANES)):
    return lax.broadcasted_iota(jnp.int32, shape, len(shape) - 1)


def _dot(a, b):
    return jnp.dot(a, b, preferred_element_type=F32)


def _dot_nt(a, b):
    return lax.dot_general(a, b, (((1,), (1,)), ((), ())), preferred_element_type=F32)


def _split3(x):
    hi = x.astype(BF16)
    r = x - hi.astype(F32)
    mid = r.astype(BF16)
    lo = (r - mid.astype(F32)).astype(BF16)
    return hi, mid, lo


def _log_sigmoid(x):
    return jnp.minimum(x, 0.0) - jnp.log1p(jnp.exp(-jnp.abs(x)))


def _rope(x, cos, sin_signed):
    return x * cos + pltpu.roll(x, LANES // 2, 1) * sin_signed


def _mod_kernel(rows_ref, w_ref, b_ref, o_ref):
    a = rows_ref[...]
    a = a * jax.nn.sigmoid(a)
    a_hi = a.astype(BF16)
    a_lo = (a - a_hi.astype(F32)).astype(BF16)
    w = w_ref[...]
    w_hi = w.astype(BF16)
    w_lo = (w - w_hi.astype(F32)).astype(BF16)
    o_ref[...] = _dot(a_hi, w_hi) + _dot(a_hi, w_lo) + _dot(a_lo, w_hi) + b_ref[...]


def _modulation(rows, ada_w, ada_b):
    depth, d, n = ada_w.shape
    tn = n // 4
    return pl.pallas_call(
        _mod_kernel,
        grid=(depth, n // tn),
        in_specs=[pl.BlockSpec((8, d), lambda l, j: (0, 0)),
                  pl.BlockSpec((None, d, tn), lambda l, j: (l, 0, j)),
                  pl.BlockSpec((None, 1, tn), lambda l, j: (l, 0, j))],
        out_specs=pl.BlockSpec((None, 8, tn), lambda l, j: (l, 0, j)),
        out_shape=jax.ShapeDtypeStruct((depth, 8, n), F32),
        compiler_params=_cparams("arbitrary", "arbitrary"),
        name="adaln_modulation",
    )(rows, ada_w, ada_b.reshape(depth, 1, n))


def _norm_mod(x, ab_ref):
    ms = jnp.mean(x * x, axis=-1, keepdims=True)
    h = (x * lax.rsqrt(ms + EPS)) * ab_ref[0:1, :] + ab_ref[1:2, :]
    return h.astype(BF16)


def _ctx_or_latent_rows(ctx_ref, x_ref, tile, tiles_per_batch):
    return jnp.where(tile % tiles_per_batch == 0, ctx_ref[...], x_ref[...])


def _split_row_specs(tm, d, tpb, tile_of):
    lat = tpb - 1
    return [pl.BlockSpec((tm, d), lambda i: (tile_of(i) // tpb, 0)),
            pl.BlockSpec((tm, d), lambda i: ((tile_of(i) // tpb) * lat + jnp.maximum(tile_of(i) % tpb - 1, 0), 0))]


def _cumsum_cols(tri_bf, lf):
    hi, mid, lo = _split3(lf)
    return _dot(tri_bf, hi) + _dot(tri_bf, mid) + _dot(tri_bf, lo)


def _cumsum_rows(lf, tri_bf):
    hi, mid, lo = _split3(lf)
    return _dot(hi, tri_bf) + _dot(mid, tri_bf) + _dot(lo, tri_bf)


def _ret_state_update(s_ref, p, k2, vt, kdec, cd_lanes, bd):
    kf = (k2.astype(F32) * kdec).astype(BF16)
    s_ref[p] = s_ref[p] * cd_lanes + jnp.where(bd, _dot(vt, kf), 0.0)


def _mlstm_state_update(c_ref, n_ref, m_ref, k_pairs, vt_pairs, c_all, bend, col0, lo, bd):
    cmax = jnp.max(c_all, axis=0, keepdims=True)
    w_all = jnp.exp(c_all - cmax)
    m_old = m_ref[0:1, :]
    mrel = jnp.maximum(m_old, cmax)
    a_row = jnp.exp(m_old - mrel)
    bb_row = jnp.exp(cmax - mrel)
    m_ref[0:1, :] = bend + mrel
    lo_row = lo[0:1, :]
    for p in range(4):
        h0 = col0 + 2 * p
        kw = k_pairs[p] * jnp.where(lo, w_all[:, h0:h0 + 1], w_all[:, h0 + 1:h0 + 2])
        kvt = _dot(vt_pairs[p], kw.astype(BF16))
        nloc = jnp.sum(kw, axis=0, keepdims=True)
        a_l = jnp.where(lo_row, a_row[:, h0:h0 + 1], a_row[:, h0 + 1:h0 + 2])
        bb_l = jnp.where(lo_row, bb_row[:, h0:h0 + 1], bb_row[:, h0 + 1:h0 + 2])
        c_ref[p] = c_ref[p] * a_l + jnp.where(bd, kvt, 0.0) * bb_l
        n_new = (n_ref[p, 0:1, :] + n_ref[p, 1:2, :]) * a_l + nloc * bb_l
        n_ref[p, 0:1, :] = jnp.where(lo_row, n_new, 0.0)
        n_ref[p, 1:2, :] = jnp.where(lo_row, 0.0, n_new)


def _mlstm_dir_weights(st, qn_row, c_col, bt_row, m_prev, tri):
    dl = jnp.where(tri, c_col + bt_row, NEG)
    mx = jnp.max(dl, axis=0, keepdims=True)
    al = bt_row + m_prev
    m_t = jnp.maximum(al, mx)
    w = jnp.exp(dl - m_t)
    a_t = jnp.exp(al - m_t)
    sw = st * w
    den = jnp.sum(sw, axis=0, keepdims=True) + a_t * qn_row
    r = 1.0 / jnp.maximum(jnp.abs(den), jnp.exp(-m_t))
    return sw * r, a_t * r


def _heads_out(ht, nw_tab):
    rows = []
    for a in range(2):
        ha = ht[a * HEAD_DIM:(a + 1) * HEAD_DIM, :]
        ms = jnp.mean(ha * ha, axis=0, keepdims=True)
        rows.append(ha * lax.rsqrt(ms + EPS))
    return (jnp.concatenate(rows, axis=0) * nw_tab).T


def _inproj_sweep_kernel(ctx_ref, x_ref, ab_ref, w_ref, wg_ref, wgt_ref, gb_ref, gbt_ref, cos_ref, sin_ref,
                         lgk_ref, cw_ref,
                         y_ref, g_ref, gt_ref, sret_ref, cm_ref, nm_ref, mm_ref, qk_ref,
                         s_scr, c_scr, n_scr, m_scr, kdec_scr,
                         p_rk, p_rv, p_mv, p_mqk, p_g, next_row_scr, *, tiles, tpb):
    i = pl.program_id(0)
    order = lambda t: jnp.where(t % tpb == 0, 0, tpb - t % tpb)
    jt_a = order(jnp.maximum(i - 1, 0))
    lane = _lane()
    sub = lax.broadcasted_iota(jnp.int32, (CHUNK, LANES), 0)
    lo = lane < HEAD_DIM
    lo_row = lo[0:1, :]
    bd_ret = (sub >= HEAD_DIM) == ((lane & 32) != 0)
    bd_m = (sub >= HEAD_DIM) == (lane >= HEAD_DIM)
    gw = 4 * LANES

    @pl.when(i == 0)
    def _():
        pos = sub.astype(F32)
        for p in range(4):
            kdec_scr[p] = jnp.exp(lgk_ref[4 + p:5 + p, :] * pos)
        for ref in (p_rk, p_rv, p_mv, p_mqk, p_g, next_row_scr):
            ref[...] = jnp.zeros_like(ref)

    @pl.when(jt_a == 0)
    def _():
        s_scr[...] = jnp.zeros_like(s_scr)
        c_scr[...] = jnp.zeros_like(c_scr)
        n_scr[...] = jnp.zeros_like(n_scr)
        m_scr[...] = jnp.zeros_like(m_scr)

    tile_i = jnp.minimum(i, tiles - 1)
    jt_i = order(tile_i)
    hb = _norm_mod(jnp.where(jt_i == 0, ctx_ref[...], x_ref[...]), ab_ref)
    cos, sin = cos_ref[...], sin_ref[...]

    def project(j):
        acc = _dot(hb, w_ref[:, j * gw:(j + 1) * gw])
        if j in (0, 1):
            if j == 0:
                acc = acc * (HEAD_DIM ** -0.5)
            acc = jnp.concatenate([_rope(acc[:, p * LANES:(p + 1) * LANES], cos, sin) for p in range(4)], axis=1)
        elif j == 3:
            acc = acc * jax.nn.sigmoid(acc)
        elif j == 7:
            acc = jax.nn.sigmoid(acc)
        return acc.astype(BF16)

    raw_q, raw_k = project(4), project(5)

    prev_on = jnp.where(jt_a <= 1, 0.0, 1.0).astype(F32)
    next_on = jnp.where((jt_a == 0) | (jt_a == tpb - 1), 0.0, 1.0).astype(F32)
    cur = p_mqk[...].astype(F32)
    n = cur.shape[0]
    row = lax.broadcasted_iota(jnp.int32, cur.shape, 0)
    prev_row = jnp.concatenate([raw_q[n - HALO:, :], raw_k[n - HALO:, :]], axis=1)[HALO - 1:HALO, :].astype(F32)
    xm = jnp.where(row == 0, prev_row * prev_on, pltpu.roll(cur, 1, 0))
    xp = jnp.where(row == n - 1, next_row_scr[0:1, :] * next_on, pltpu.roll(cur, n - 1, 0))
    conv = cw_ref[3:4, :] + cw_ref[0:1, :] * xm + cw_ref[1:2, :] * cur + cw_ref[2:3, :] * xp
    qk = conv * jax.nn.sigmoid(conv)
    qk_ref[:, 0:gw] = (qk[:, 0:gw] * (HEAD_DIM ** -0.5)).astype(BF16)
    qk_ref[:, gw:] = qk[:, gw:].astype(BF16)
    le_bf = (sub <= lane).astype(BF16)
    chunks = (1, 0)
    pre = {}
    for blk in chunks:
        rows = slice(blk * CHUNK, (blk + 1) * CHUNK)
        g = p_g[rows, :]
        bal = pltpu.roll(_cumsum_cols(le_bf, _log_sigmoid(g)), LANES - 8, 1)
        c_all = g - bal
        cmax = jnp.max(c_all, axis=0, keepdims=True)
        w_all = jnp.exp(c_all - cmax)
        vts, ks, nlocs = [], [], []
        for p in range(8):
            sl = slice((p % 4) * LANES, (p % 4 + 1) * LANES)
            if p < 4:
                v2 = p_rv[rows, sl]
                ks.append((p_rk[rows, sl].astype(F32) * kdec_scr[p]).astype(BF16))
            else:
                v2 = p_mv[rows, sl]
                h0 = 16 + 2 * (p - 4)
                kw = qk[rows, gw + (p - 4) * LANES:gw + (p - 3) * LANES] * jnp.where(
                    lo, w_all[:, h0:h0 + 1], w_all[:, h0 + 1:h0 + 2])
                ks.append(kw.astype(BF16))
                nlocs.append(jnp.sum(kw, axis=0, keepdims=True))
            vts.append(v2.astype(F32).T.astype(BF16))
        pre[blk] = (vts, ks, nlocs, cmax, bal[0:1, :])

    cur_rk = project(1)
    y_ref[:, 1 * gw:2 * gw] = cur_rk
    y_ref[:, 0:gw] = project(0)
    kvs = {blk: [_dot(pre[blk][0][p], pre[blk][1][p]) for p in range(8)] for blk in chunks}
    cur_rv = project(2)
    y_ref[:, 2 * gw:3 * gw] = cur_rv
    y_ref[:, 3 * gw:4 * gw] = project(3)

    for blk in chunks:
        _, _, nlocs, cmax, bend = pre[blk]
        sret_ref[blk] = s_scr[...].astype(BF16)
        cm_ref[blk] = c_scr[...].astype(BF16)
        nm_ref[blk] = n_scr[...].astype(BF16)
        mm_ref[blk] = m_scr[...]
        m_old = m_scr[0:1, :]
        mrel = jnp.maximum(m_old, cmax)
        a_row = jnp.exp(m_old - mrel)
        bb_row = jnp.exp(cmax - mrel)
        m_scr[0:1, :] = bend + mrel
        for p in range(4):
            cd = jnp.exp(lgk_ref[4 + p:5 + p, :] * float(CHUNK))
            s_scr[p] = s_scr[p] * cd + jnp.where(bd_ret, kvs[blk][p], 0.0)
            h0 = 16 + 2 * p
            a_l = jnp.where(lo_row, a_row[:, h0:h0 + 1], a_row[:, h0 + 1:h0 + 2])
            bb_l = jnp.where(lo_row, bb_row[:, h0:h0 + 1], bb_row[:, h0 + 1:h0 + 2])
            c_scr[p] = c_scr[p] * a_l + jnp.where(bd_m, kvs[blk][4 + p], 0.0) * bb_l
            n_new = (n_scr[p, 0:1, :] + n_scr[p, 1:2, :]) * a_l + nlocs[p] * bb_l
            n_scr[p, 0:1, :] = jnp.where(lo_row, n_new, 0.0)
            n_scr[p, 1:2, :] = jnp.where(lo_row, 0.0, n_new)

    cur_mv = project(6)
    y_ref[:, 4 * gw:5 * gw] = cur_mv
    y_ref[:, 5 * gw:6 * gw] = project(7)
    gates = _dot(hb, wg_ref[...]) + gb_ref[...]
    g_ref[...] = gates
    gt_ref[...] = _dot_nt(wgt_ref[...], hb) + gbt_ref[...]
    next_row_scr[...] = p_mqk[0:HALO, :].astype(F32)
    p_mqk[:, 0:gw] = raw_q
    p_mqk[:, gw:] = raw_k
    p_rk[...] = cur_rk
    p_rv[...] = cur_rv
    p_mv[...] = cur_mv
    p_g[...] = gates


def _inproj_sweep(ctx2, x2, ab, w, wg, wgt, gb, gbt, cos, sin, lgk, cw, nb, tpb):
    d = x2.shape[1]
    tm = ROW_TILE
    cpt = tm // CHUNK
    gw = 4 * LANES
    tiles = nb * tpb
    r = tiles * tm
    lat = tpb - 1
    seq_tile = lambda t: (t // tpb) * tpb + jnp.where(t % tpb == 0, 0, tpb - t % tpb)
    cur = lambda i: jnp.minimum(i, tiles - 1)
    tile_i = lambda i: seq_tile(cur(i))
    tile_a = lambda i: seq_tile(jnp.maximum(i - 1, 0))
    in_batch = lambda i: tile_i(i) % tpb
    sel = lambda i: ((tile_i(i) // tpb) * 2 + jnp.minimum(in_batch(i), 1), 0, 0)
    state = lambda *dims: pl.BlockSpec((cpt,) + dims, lambda i: (tile_a(i),) + (0,) * len(dims))
    nchunks = tiles * cpt
    return pl.pallas_call(
        functools.partial(_inproj_sweep_kernel, tiles=tiles, tpb=tpb),
        grid=(tiles + 1,),
        in_specs=[pl.BlockSpec((tm, d), lambda i: (tile_i(i) // tpb, 0)),
                  pl.BlockSpec((tm, d), lambda i: ((tile_i(i) // tpb) * lat + jnp.maximum(in_batch(i) - 1, 0), 0)),
                  pl.BlockSpec((None, 2, d), sel),
                  _const_spec(w.shape), _const_spec(wg.shape), _const_spec(wgt.shape),
                  _const_spec(gb.shape), _const_spec(gbt.shape),
                  pl.BlockSpec((tm, LANES), lambda i: (in_batch(i), 0)),
                  pl.BlockSpec((tm, LANES), lambda i: (in_batch(i), 0)),
                  _const_spec(lgk.shape), _const_spec(cw.shape)],
        out_specs=[pl.BlockSpec((tm, 6 * gw), lambda i: (tile_i(i), 0)),
                   pl.BlockSpec((tm, LANES), lambda i: (tile_i(i), 0)),
                   pl.BlockSpec((32, tm), lambda i: (0, tile_i(i))),
                   state(4, CHUNK, LANES), state(4, CHUNK, LANES), state(4, HALO, LANES), state(8, LANES),
                   pl.BlockSpec((tm, 2 * gw), lambda i: (tile_a(i), 0))],
        out_shape=[jax.ShapeDtypeStruct((r, 6 * gw), BF16),
                   jax.ShapeDtypeStruct((r, LANES), F32),
                   jax.ShapeDtypeStruct((32, r), F32),
                   jax.ShapeDtypeStruct((nchunks, 4, CHUNK, LANES), BF16),
                   jax.ShapeDtypeStruct((nchunks, 4, CHUNK, LANES), BF16),
                   jax.ShapeDtypeStruct((nchunks, 4, HALO, LANES), BF16),
                   jax.ShapeDtypeStruct((nchunks, 8, LANES), F32),
                   jax.ShapeDtypeStruct((r, 2 * gw), BF16)],
        scratch_shapes=[pltpu.VMEM((4, CHUNK, LANES), F32), pltpu.VMEM((4, CHUNK, LANES), F32),
                        pltpu.VMEM((4, HALO, LANES), F32), pltpu.VMEM((8, LANES), F32),
                        pltpu.VMEM((4, CHUNK, LANES), F32),
                        pltpu.VMEM((tm, gw), BF16), pltpu.VMEM((tm, gw), BF16), pltpu.VMEM((tm, gw), BF16),
                        pltpu.VMEM((tm, 2 * gw), BF16), pltpu.VMEM((tm, LANES), F32),
                        pltpu.VMEM((HALO, 2 * gw), F32)],
        compiler_params=_cparams("arbitrary"),
        name="inproj_bwd_sweep",
    )(ctx2, x2, ab, w, wg, wgt, gb, gbt, cos, sin, lgk, cw)


class _Bag:
    def __init__(self, **kw):
        self.__dict__.update(kw)


def _mixer_chunk_stages(blk, r):
    rows = slice(blk * CHUNK, (blk + 1) * CHUNK)
    lane = _lane()
    sub = lax.broadcasted_iota(jnp.int32, (CHUNK, LANES), 0)
    lo = lane < HEAD_DIM
    sub_lo = sub < HEAD_DIM
    mask_ret = [(lane & 32) == 0, (lane & 32) != 0]
    mask_nat = [lo, lane >= HEAD_DIM]
    bd_ret = (sub >= HEAD_DIM) == ((lane & 32) != 0)
    bd_m = (sub >= HEAD_DIM) == (lane >= HEAD_DIM)
    le = sub <= lane
    ge = sub >= lane

    g = r.g_ref[rows, :]
    gt = r.gt_ref[:, rows]
    lf_col = _log_sigmoid(g)
    lf_row = _log_sigmoid(gt)
    le_bf = le.astype(BF16)
    ge_bf = ge.astype(BF16)
    bal_f = pltpu.roll(_cumsum_cols(ge_bf, lf_col), LANES - 8, 1)
    bal_b = pltpu.roll(_cumsum_cols(le_bf, lf_col), LANES - 8, 1)
    cf_all = g - bal_f
    cb_all = g - bal_b
    bf_row = _cumsum_rows(lf_row, le_bf)
    bb_row = _cumsum_rows(lf_row, ge_bf)
    qb, kb, vts, kf32, qt, vbd = [], [], [], [], [], []
    for p in range(8):
        sl = slice((p % 4) * LANES, (p % 4 + 1) * LANES)
        if p < 4:
            q2, k2, v2 = r.rq_ref[rows, sl], r.rk_ref[rows, sl], r.rv_ref[rows, sl]
            qf = q2.astype(F32)
            kf = None
        else:
            q2 = r.qk_ref[rows, sl]
            k2 = r.qk_ref[rows, 4 * LANES + (p - 4) * LANES:4 * LANES + (p - 3) * LANES]
            v2 = r.mv_ref[rows, sl]
            qf, kf = q2.astype(F32), k2.astype(F32)
        vt = v2.astype(F32).T.astype(BF16)
        qb.append(q2)
        kb.append(k2)
        vts.append(vt)
        kf32.append(kf)
        qt.append(qf.T)
        vbd.append([jnp.where(sub_lo, vt, jnp.zeros_like(vt)), jnp.where(sub_lo, jnp.zeros_like(vt), vt)])
    yield

    st2 = []
    for p in range(8):
        masks = mask_ret if p < 4 else mask_nat
        zero = jnp.zeros_like(qb[p])
        qstack = jnp.concatenate([jnp.where(masks[0], qb[p], zero), jnp.where(masks[1], qb[p], zero)], axis=0)
        st2.append(_dot_nt(kb[p], qstack))
    yield

    qn_f = [_dot_nt(r.n_scr[p].astype(BF16), qb[4 + p]) for p in range(4)]
    qn_b = [_dot_nt(r.nmb_ref[blk, p], qb[4 + p]) for p in range(4)]
    lhs, rhs = [], []
    for p in range(8):
        if p < 4:
            pts = [(st2[p][:, a * LANES:(a + 1) * LANES] * r.dm_scr[2 * p + a]).astype(BF16) for a in range(2)]
            x_f, x_b = r.dec_scr[0, p], r.dec_scr[1, p]
            old = [r.s_scr[p].astype(BF16), r.sretb_ref[blk, p]]
        else:
            pts, cf, cb = [], [], []
            for a in range(2):
                h = 2 * (p - 4) + a
                st = st2[p][:, a * LANES:(a + 1) * LANES]
                pf, coef_f = _mlstm_dir_weights(st, qn_f[p - 4][a:a + 1, :], cf_all[:, h:h + 1],
                                                bf_row[8 + h:9 + h, :], r.m_state[0:1, h:h + 1], le)
                pb, coef_b = _mlstm_dir_weights(st, qn_b[p - 4][a:a + 1, :], cb_all[:, 16 + h:17 + h],
                                                bb_row[24 + h:25 + h, :], r.mmb_ref[blk, 0:1, 16 + h:17 + h], ge)
                pts.append((pf + pb).astype(BF16))
                cf.append(coef_f)
                cb.append(coef_b)
            x_f = jnp.where(sub_lo, cf[0], cf[1])
            x_b = jnp.where(sub_lo, cb[0], cb[1])
            old = [r.c_scr[p - 4].astype(BF16), r.cmb_ref[blk, p - 4]]
        lhs.append(jnp.concatenate(vbd[p] + old, axis=1))
        rhs.append(jnp.concatenate(pts + [(qt[p] * x_f).astype(BF16), (qt[p] * x_b).astype(BF16)], axis=0))
    yield

    ht = [_dot(lhs[p], rhs[p]) for p in range(8)]
    yield

    for p in range(8):
        sl = slice((p % 4) * LANES, (p % 4 + 1) * LANES)
        gate_ref = r.rg_ref if p < 4 else r.mo_ref
        y = _heads_out(ht[p], r.nw_ref[p])
        new = (y * gate_ref[rows, sl].astype(F32)).astype(BF16)
        r.mix_scr[rows, p * LANES:(p + 1) * LANES] = jnp.where(r.live, new, r.mix_scr[rows, p * LANES:(p + 1) * LANES])
    yield

    for p in range(4):
        cd = jnp.exp(r.lgk_ref[p:p + 1, :] * float(CHUNK))
        _ret_state_update(r.s_scr, p, kb[p], vts[p], r.dec_scr[2, p], cd, bd_ret)
    _mlstm_state_update(r.c_scr, r.n_scr, r.m_state, kf32[4:], vts[4:], cf_all, bal_f[CHUNK - 1:CHUNK, :],
                        0, lo, bd_m)
    yield


def _attn_inproj_stages(x, ab_ref, w_ref, nw_ref, cos, sin, y_ref):
    hb = _norm_mod(x, ab_ref)
    r2 = lax.broadcasted_iota(jnp.int32, (2 * LANES, 2 * LANES), 0)
    c2 = lax.broadcasted_iota(jnp.int32, (2 * LANES, 2 * LANES), 1)
    same_head = (((r2 ^ c2) & (LANES | 32)) == 0).astype(BF16)
    acc_q = _dot(hb, w_ref[:, 0:8 * LANES])
    acc_k = _dot(hb, w_ref[:, 8 * LANES:10 * LANES])
    y_ref[:, 10 * LANES:12 * LANES] = _dot(hb, w_ref[:, 10 * LANES:12 * LANES]).astype(BF16)
    yield
    for j in range(5):
        acc = acc_q[:, j * 2 * LANES:(j + 1) * 2 * LANES] if j < 4 else acc_k
        sq = acc * acc
        hi = sq.astype(BF16)
        lo = (sq - hi.astype(F32)).astype(BF16)
        ms = (_dot(hi, same_head) + _dot(lo, same_head)) * (1.0 / HEAD_DIM)
        nrm = acc * lax.rsqrt(ms + EPS)
        nw = nw_ref[0:1, :] if j < 4 else nw_ref[1:2, :]
        for v in range(2):
            ls = slice(v * LANES, (v + 1) * LANES)
            y_ref[:, (2 * j + v) * LANES:(2 * j + v + 1) * LANES] = _rope(nrm[:, ls] * nw, cos, sin).astype(BF16)
        if j in (1, 4):
            yield


def _ffn_splits(d_ff, pieces):
    blocks = d_ff // (2 * LANES)
    assert blocks * 2 * LANES == d_ff and blocks >= pieces
    cuts = [((i * blocks) // pieces) * 2 * LANES for i in range(pieces)]
    return cuts + [d_ff]


def _mixer_ffn_kernel(lg_ref, rq_ref, rk_ref, rv_ref, rg_ref, qk_ref, mv_ref, mo_ref,
                      g_ref, gt_ref, sretb_ref, cmb_ref, nmb_ref, mmb_ref,
                      lgk_ref, lgkt_ref, nw_ref,
                      ctx_ref, x_ref, p_ref, wo_ref, wi_ref, w2_ref,
                      ab2_ref, wa_ref, nwa_ref, cos_ref, sin_ref,
                      o_ref, y2_ref,
                      s_scr, c_scr, n_scr, m_state, dm_scr, dec_scr, mix_scr, act_scr, x2_scr,
                      *, tiles, tpb, d_ff):
    s = pl.program_id(0)
    jt = jnp.minimum(s, tiles - 1) % tpb
    lane = _lane()
    sub = lax.broadcasted_iota(jnp.int32, (CHUNK, LANES), 0)

    @pl.when(s == 0)
    def _():
        mix_scr[...] = jnp.zeros_like(mix_scr)
        x2_scr[...] = jnp.zeros_like(x2_scr)
        le = sub <= lane
        ge = sub >= lane
        spos = sub.astype(F32)
        tpos = lane.astype(F32)
        diff = (lane - sub).astype(F32)
        for h in range(8):
            dm_scr[h] = (jnp.where(le, jnp.exp(lg_ref[h] * diff), 0.0)
                         + jnp.where(ge, jnp.exp(lg_ref[8 + h] * (-diff)), 0.0))
        for p in range(4):
            dec_scr[0, p] = jnp.exp(lgkt_ref[:, p:p + 1] * (tpos + 1.0))
            dec_scr[1, p] = jnp.exp(lgkt_ref[:, 4 + p:5 + p] * (float(CHUNK) - tpos))
            dec_scr[2, p] = jnp.exp(lgk_ref[p:p + 1, :] * (float(CHUNK) - 1.0 - spos))

    @pl.when(jt == 0)
    def _():
        s_scr[...] = jnp.zeros_like(s_scr)
        c_scr[...] = jnp.zeros_like(c_scr)
        n_scr[...] = jnp.zeros_like(n_scr)
        m_state[...] = jnp.zeros_like(m_state)

    r = _Bag(rq_ref=rq_ref, rk_ref=rk_ref, rv_ref=rv_ref, rg_ref=rg_ref, qk_ref=qk_ref, mv_ref=mv_ref, mo_ref=mo_ref,
             g_ref=g_ref, gt_ref=gt_ref, sretb_ref=sretb_ref, cmb_ref=cmb_ref, nmb_ref=nmb_ref, mmb_ref=mmb_ref,
             lgk_ref=lgk_ref, nw_ref=nw_ref, s_scr=s_scr, c_scr=c_scr, n_scr=n_scr, m_state=m_state,
             dm_scr=dm_scr, dec_scr=dec_scr, mix_scr=mix_scr, live=s < tiles)
    cuts = _ffn_splits(d_ff, 4)
    ffn_piece = lambda i: _ffn_cols(h, wi_ref, act_scr, cuts[i], cuts[i + 1], d_ff)

    x = _ctx_or_latent_rows(ctx_ref, x_ref, jnp.clip(s - 1, 0, tiles - 1), tpb)
    x1, h = _ffn_in(x, mix_scr[...], p_ref, wo_ref)

    nxt = _attn_inproj_stages(x2_scr[...], ab2_ref, wa_ref, nwa_ref, cos_ref[...], sin_ref[...], y2_ref)
    chunk_a, chunk_b = _mixer_chunk_stages(0, r), _mixer_chunk_stages(1, r)
    next(chunk_a), next(chunk_b)
    next(chunk_a), next(chunk_b)
    ffn_piece(0)
    next(chunk_a), next(chunk_a)
    next(nxt)
    ffn_piece(1)
    next(chunk_a), next(chunk_a)
    next(nxt)
    ffn_piece(2)
    next(chunk_b), next(chunk_b)
    next(nxt)
    ffn_piece(3)
    next(chunk_b), next(chunk_b)
    x2 = x1 + p_ref[3:4, :] * _dot(act_scr[...], w2_ref[...])
    o_ref[...] = x2
    x2_scr[...] = x2


def _mixer_ffn(y, g, gt, states, lg_smem, lgk, lgkt, nw, xs, prm, wo, wi_all, w2_all, layer,
               ab_next, w_next, nw_next, cos, sin, nb, tpb):
    gw = 4 * LANES
    tm = ROW_TILE
    cpt = tm // CHUNK
    sretb, cmb, nmb, mmb, qk_act = states
    tiles = nb * tpb
    d = xs[-1].shape[1]
    d_ff = w2_all.shape[1]
    n_next = w_next.shape[1]
    mix = lambda s: jnp.minimum(s, tiles - 1)
    ffn = lambda s: jnp.clip(s - 1, 0, tiles - 1)
    nxt = lambda s: jnp.maximum(s - 2, 0)
    blk = lambda j: pl.BlockSpec((tm, gw), lambda s: (mix(s), j))
    state_spec = lambda a: pl.BlockSpec((cpt,) + a.shape[1:], lambda s: (mix(s),) + (0,) * (a.ndim - 1))
    sel = lambda t: ((t // tpb) * 2 + jnp.minimum(t % tpb, 1), 0, 0)
    layer_spec = lambda a: pl.BlockSpec((None,) + a.shape[1:], lambda s: (layer, 0, 0),
                                        pipeline_mode=pl.Buffered(1))
    return pl.pallas_call(
        functools.partial(_mixer_ffn_kernel, tiles=tiles, tpb=tpb, d_ff=d_ff),
        grid=(tiles + 2,),
        in_specs=[_smem_spec(),
                  blk(0), blk(1), blk(2), blk(3),
                  pl.BlockSpec((tm, 2 * gw), lambda s: (mix(s), 0)),
                  blk(4), blk(5),
                  pl.BlockSpec((tm, LANES), lambda s: (mix(s), 0)),
                  pl.BlockSpec((32, tm), lambda s: (0, mix(s))),
                  state_spec(sretb), state_spec(cmb), state_spec(nmb), state_spec(mmb),
                  _const_spec(lgk.shape), _const_spec(lgkt.shape), _const_spec(nw.shape)]
                 + _split_row_specs(tm, d, tpb, ffn)
                 + [pl.BlockSpec((None, 8, d), lambda s: sel(ffn(s))),
                    _const_spec(wo.shape), layer_spec(wi_all), layer_spec(w2_all),
                    pl.BlockSpec((None, 2, d), lambda s: sel(nxt(s))),
                    _const_spec(w_next.shape), _const_spec(nw_next.shape),
                    pl.BlockSpec((tm, LANES), lambda s: (nxt(s) % tpb, 0)),
                    pl.BlockSpec((tm, LANES), lambda s: (nxt(s) % tpb, 0))],
        out_specs=[pl.BlockSpec((tm, d), lambda s: (ffn(s), 0)),
                   pl.BlockSpec((tm, n_next), lambda s: (nxt(s), 0))],
        out_shape=[jax.ShapeDtypeStruct((tiles * tm, d), F32),
                   jax.ShapeDtypeStruct((tiles * tm, n_next), BF16)],
        scratch_shapes=[pltpu.VMEM((4, CHUNK, LANES), F32), pltpu.VMEM((4, CHUNK, LANES), F32),
                        pltpu.VMEM((4, HALO, LANES), F32), pltpu.VMEM((8, LANES), F32),
                        pltpu.VMEM((8, CHUNK, LANES), F32), pltpu.VMEM((3, 4, CHUNK, LANES), F32),
                        pltpu.VMEM((tm, 2 * gw), BF16), pltpu.VMEM((tm, d_ff), BF16),
                        pltpu.VMEM((tm, d), F32)],
        compiler_params=_cparams("arbitrary"),
        name="ret_mlstm_mixer_ffn",
    )(lg_smem, y, y, y, y, qk_act, y, y, g, gt, sretb, cmb, nmb, mmb, lgk, lgkt, nw,
      *xs, prm, wo, wi_all, w2_all, ab_next, w_next, nw_next, cos, sin)


def _ffn_in(x, m, p_ref, wo_ref):
    x1 = x + p_ref[0:1, :] * _dot(m, wo_ref[...])
    ms = jnp.mean(x1 * x1, axis=-1, keepdims=True)
    return x1, ((x1 * lax.rsqrt(ms + EPS)) * p_ref[1:2, :] + p_ref[2:3, :]).astype(BF16)


def _ffn_cols(h, wi_ref, act_scr, lo, hi, d_ff):
    gate = _dot(h, wi_ref[:, lo:hi])
    up = _dot(h, wi_ref[:, d_ff + lo:d_ff + hi])
    act_scr[:, lo:hi] = (gate * jax.nn.sigmoid(gate) * up).astype(BF16)


def _window_bias():
    kk = np.arange(CHUNK)[:, None]
    t = np.arange(CHUNK)[None, :]
    tabs = []
    for has_prev, has_next in ((False, True), (True, True), (True, False)):
        prev_ok = (kk >= t) & has_prev
        next_ok = (kk <= t) & has_next
        tabs.append(np.where(np.concatenate([prev_ok, next_ok], axis=0), 0.0, NEG))
    return jnp.asarray(np.stack(tabs), F32)


def _attn_ffn_kernel(sink_ref, q_ref, kp_ref, kc_ref, kn_ref, kx_ref, vp_ref, vc_ref, vn_ref, vx_ref, bias_ref,
                     x_ref, p_ref, wo_ref, wi_ref, w2_ref, o_ref, m_scr, act_scr, *, tiles, lat_tiles, d_ff):
    s = pl.program_id(0)

    @pl.when(s == 0)
    def _():
        m_scr[...] = jnp.zeros_like(m_scr)

    j = jnp.minimum(s, tiles - 1) % lat_tiles
    grp = lax.broadcasted_iota(jnp.int32, (1, 4 * CHUNK), 1) // CHUNK
    lane = _lane()
    mask_q = [(lane & 32) == 0, (lane & 32) != 0]
    tile4 = lambda b: jnp.concatenate([b] * 4, axis=1)
    biases = [tile4(jnp.where(j == 0, bias_ref[0], bias_ref[1])),
              tile4(jnp.where(j == lat_tiles - 1, bias_ref[2], bias_ref[1]))]
    cuts = _ffn_splits(d_ff, 6)
    d = o_ref.shape[1]

    def ffn_up(i):
        _ffn_cols(h, wi_ref, act_scr, cuts[i], cuts[i + 1], d_ff)

    def ffn_down(i):
        cs = slice(i * d // 4, (i + 1) * d // 4)
        o_ref[:, cs] = x1[:, cs] + p_ref[3:4, cs] * _dot(act_scr[...], w2_ref[:, cs])

    def scores(u):
        blk, kvp, a = u // 4, (u % 4) // 2, u % 2
        rows = slice(blk * CHUNK, (blk + 1) * CHUNK)
        qs = jnp.concatenate(
            [jnp.where(mask_q[a], q_ref[rows, (kvp * 4 + g) * LANES:(kvp * 4 + g + 1) * LANES],
                       jnp.zeros((CHUNK, LANES), BF16)) for g in range(4)], axis=0)
        return _dot_nt(kcats[kvp][blk], qs)

    def softmax_pv(u, st):
        blk, kv = u // 4, u % 4
        bias = biases[blk]
        st = jnp.concatenate([st[0:CHUNK] + bias[0:CHUNK], st[CHUNK:2 * CHUNK],
                              st[2 * CHUNK:3 * CHUNK] + bias[CHUNK:2 * CHUNK], st[3 * CHUNK:]], axis=0)
        snk = jnp.where(grp == 0, sink_ref[kv * 4],
                        jnp.where(grp == 1, sink_ref[kv * 4 + 1],
                                  jnp.where(grp == 2, sink_ref[kv * 4 + 2], sink_ref[kv * 4 + 3])))
        m = jnp.maximum(jnp.max(st, axis=0, keepdims=True), snk)
        e = jnp.exp(st - m)
        denom = jnp.exp(snk - m) + jnp.sum(e, axis=0, keepdims=True)
        a = kv % 2
        return _dot(vts[kv // 2][blk][a * HEAD_DIM:(a + 1) * HEAD_DIM, :], e.astype(BF16)) * (1.0 / denom)

    def hand_over(blk, outs):
        for kvp in range(2):
            full = jnp.concatenate(outs[2 * kvp:2 * kvp + 2], axis=0)
            for g in range(4):
                m_scr[blk * CHUNK:(blk + 1) * CHUNK, (kvp * 4 + g) * LANES:(kvp * 4 + g + 1) * LANES] = (
                    full[:, g * CHUNK:(g + 1) * CHUNK].T.astype(BF16))

    x1, h = _ffn_in(x_ref[...], m_scr[...], p_ref, wo_ref)

    kcats, vts = [], []
    for kvp in range(2):
        sl = slice(kvp * LANES, (kvp + 1) * LANES)
        k_chunks = [kp_ref[:, sl], kc_ref[0:CHUNK, sl], kc_ref[CHUNK:2 * CHUNK, sl], kn_ref[:, sl]]
        v_chunks = [vp_ref[:, sl], vc_ref[0:CHUNK, sl], vc_ref[CHUNK:2 * CHUNK, sl], vn_ref[:, sl],
                    vx_ref[0:CHUNK, sl], vx_ref[CHUNK:2 * CHUNK, sl]]
        v_t = [v.astype(F32).T.astype(BF16) for v in v_chunks]
        kcats.append([jnp.concatenate(k_chunks[b:b + 3] + [kx_ref[:, sl]], axis=0) for b in range(2)])
        vts.append([jnp.concatenate(v_t[b:b + 3] + v_t[4:], axis=1) for b in range(2)])

    ffn_pieces = [functools.partial(ffn_up, i) for i in range(6)] + [functools.partial(ffn_down, i) for i in range(2)]
    sts = {0: scores(0)}
    outs = []
    for u in range(8):
        if u + 1 < 8:
            sts[u + 1] = scores(u + 1)
        ffn_pieces[u]()
        outs.append(softmax_pv(u, sts.pop(u)))
        if u % 4 == 3:
            hand_over(u // 4, outs[u - 3:u + 1])
    ffn_down(2)
    ffn_down(3)


def _attn_ffn(y, sink, xc, prm, wo, wi_all, w2_all, layer, nb, nc, ctx_chunks):
    tm = ROW_TILE
    cpt = tm // CHUNK
    tpb = nc // cpt
    lat_tiles = (nc - ctx_chunks) // cpt
    tiles = nb * lat_tiles
    ctx_tiles = ctx_chunks // cpt
    assert ctx_tiles == 1 and lat_tiles >= 2
    d = xc.shape[1]
    d_ff = w2_all.shape[1]
    bias = _window_bias()
    kcol, vcol = d // (2 * LANES), d // (2 * LANES) + 1
    att = lambda s: jnp.minimum(s, tiles - 1)
    ffn = lambda s: jnp.maximum(s - 1, 0)
    row_tile = lambda t: (t // lat_tiles) * tpb + ctx_tiles + t % lat_tiles
    chunk0 = lambda t: (t // lat_tiles) * nc + ctx_chunks
    prev_c = lambda s: chunk0(att(s)) + jnp.maximum((att(s) % lat_tiles) * cpt - 1, 0)
    next_c = lambda s: chunk0(att(s)) + jnp.minimum((att(s) % lat_tiles) * cpt + cpt, lat_tiles * cpt - 1)
    cur_spec = lambda col: pl.BlockSpec((tm, 2 * LANES), lambda s: (row_tile(att(s)), col))
    edge_spec = lambda f, col: pl.BlockSpec((CHUNK, 2 * LANES), lambda s: (f(s), col))
    ctx_spec = lambda col: pl.BlockSpec((tm, 2 * LANES), lambda s: ((att(s) // lat_tiles) * tpb, col))
    layer_spec = lambda a: pl.BlockSpec((None,) + a.shape[1:], lambda s: (layer, 0, 0),
                                        pipeline_mode=pl.Buffered(1))
    return pl.pallas_call(
        functools.partial(_attn_ffn_kernel, tiles=tiles, lat_tiles=lat_tiles, d_ff=d_ff),
        grid=(tiles + 1,),
        in_specs=[_smem_spec(),
                  pl.BlockSpec((tm, d), lambda s: (row_tile(att(s)), 0)),
                  edge_spec(prev_c, kcol), cur_spec(kcol), edge_spec(next_c, kcol), ctx_spec(kcol),
                  edge_spec(prev_c, vcol), cur_spec(vcol), edge_spec(next_c, vcol), ctx_spec(vcol),
                  _const_spec(bias.shape),
                  pl.BlockSpec((tm, d), lambda s: (row_tile(ffn(s)), 0)),
                  pl.BlockSpec((None, 8, d), lambda s: ((ffn(s) // lat_tiles) * 2 + 1, 0, 0)),
                  _const_spec(wo.shape), layer_spec(wi_all), layer_spec(w2_all)],
        out_specs=pl.BlockSpec((tm, d), lambda s: (ffn(s), 0)),
        out_shape=jax.ShapeDtypeStruct((tiles * tm, d), F32),
        scratch_shapes=[pltpu.VMEM((tm, d), BF16), pltpu.VMEM((tm, d_ff), BF16)],
        compiler_params=_cparams("arbitrary"),
        name="window_gqa_ffn",
    )(sink, y, y, y, y, y, y, y, y, y, bias, xc, prm, wo, wi_all, w2_all)


def _pair_cols(w):
    rows, cols = w.shape
    return w.reshape(rows, cols // LANES, 2, 2, 32).transpose(0, 1, 3, 2, 4).reshape(rows, cols)


def _attn_q_cols(w):
    rows = w.shape[0]
    g_per = w.shape[1] // (H_KV * HEAD_DIM)
    return (w.reshape(rows, H_KV // 2, 2, g_per, 2, 32).transpose(0, 1, 3, 4, 2, 5)
            .reshape(rows, w.shape[1]))


def _attn_o_rows(w):
    cols = w.shape[1]
    g_per = w.shape[0] // (H_KV * HEAD_DIM)
    return (w.reshape(H_KV // 2, 2, g_per, HEAD_DIM, cols).transpose(0, 2, 1, 3, 4)
            .reshape(w.shape[0], cols))


def _rope_tables(seq, ctx_len):
    rows = seq // GRID_W
    row = np.repeat(np.arange(rows, dtype=np.float32), GRID_W)
    col = np.tile(np.arange(GRID_W, dtype=np.float32), rows)
    n = HEAD_DIM // 4
    inv = (np.float32(ROPE_BASE) ** (-np.arange(n, dtype=np.float32) / np.float32(n))).astype(np.float32)
    ang = np.concatenate([row[:, None] * inv, col[:, None] * inv], axis=-1).astype(np.float32)
    cos, sin = np.cos(ang), np.sin(ang)
    cos_t = np.concatenate([np.ones((ctx_len, LANES), np.float32), np.tile(cos, (1, 4))], axis=0)
    sin_t = np.concatenate([np.zeros((ctx_len, LANES), np.float32),
                            np.concatenate([-sin, -sin, sin, sin], axis=-1)], axis=0)
    return jnp.asarray(cos_t, F32), jnp.asarray(sin_t, F32)


def _mod_tables(mod, nb, norm_w):
    d = norm_w.shape[-1]
    lat = mod[:nb].reshape(nb, 6, d)
    ctx = jnp.broadcast_to(mod[nb].reshape(1, 6, d), (nb, 6, d))
    both = jnp.stack([ctx, lat], axis=1).reshape(nb * 2, 6, d)
    sh1, sc1, g1, sh2, sc2, g2 = [both[:, k] for k in range(6)]
    ab1 = jnp.stack([norm_w[0] * (1.0 + sc1), sh1], axis=1)
    zeros = jnp.zeros_like(g1)
    prm = jnp.stack([g1, norm_w[1] * (1.0 + sc2), sh2, g2, zeros, zeros, zeros, zeros], axis=1)
    return ab1, prm


def kernel(x, c, ctx, c_ctx, ada_w, ada_b, norm_w, ffn_w_in, ffn_w_out, ab_w_in, ab_w_out,
           ret_log_gamma, ret_norm_w, mlstm_conv_w, mlstm_conv_b, mlstm_gate_b, mlstm_norm_w,
           attn_w_in, attn_w_out, attn_q_norm_w, attn_k_norm_w, attn_sink):
    nb, seq, d = x.shape
    ctx_len = ctx.shape[1]
    depth = ada_w.shape[0]
    assert ctx_len == ROW_TILE and seq % ROW_TILE == 0 and d == 8 * LANES and nb < 8
    t_all = ctx_len + seq
    nc = t_all // CHUNK
    ctx_chunks = ctx_len // CHUNK
    tpb = t_all // ROW_TILE
    dr = d // 2

    rows = jnp.zeros((8, d), F32).at[:nb].set(c).at[nb].set(c_ctx)
    mod_all = _modulation(rows, ada_w, ada_b)
    cos_t, sin_t = _rope_tables(seq, ctx_len)
    wi_all = ffn_w_in.astype(BF16)
    w2_all = ffn_w_out.astype(BF16)
    xs = (ctx.reshape(nb * ctx_len, d), x.reshape(nb * seq, d))

    assert depth == 2 and ab_w_in.shape[0] == 1 and attn_w_in.shape[0] == 1
    ab_0, prm_0 = _mod_tables(mod_all[0], nb, norm_w[0])
    ab_1, prm_1 = _mod_tables(mod_all[1], nb, norm_w[1])

    w = ab_w_in[0]
    w_main = jnp.concatenate([_pair_cols(w[:, :dr]), _pair_cols(w[:, dr:2 * dr]), w[:, 2 * dr:8 * dr]],
                             axis=1).astype(BF16)
    wg = jnp.zeros((d, LANES), F32).at[:, :32].set(w[:, 8 * dr:]).astype(BF16)
    wgt = w[:, 8 * dr:].T.astype(BF16)
    gb = jnp.zeros((1, LANES), F32).at[0, :32].set(mlstm_gate_b[0].reshape(-1))
    gbt = mlstm_gate_b[0].reshape(32, 1)
    lg = ret_log_gamma[0].astype(F32)
    lgk = jnp.tile(jnp.repeat(lg.reshape(2, 4, 2), 32, axis=-1), (1, 1, 2)).reshape(8, LANES)
    cw = jnp.concatenate([mlstm_conv_w[0], mlstm_conv_b[0][None], jnp.zeros((4, 2 * dr), F32)], axis=0)
    nw = jnp.broadcast_to(jnp.concatenate([ret_norm_w[0], mlstm_norm_w[0]]).reshape(8, LANES, 1),
                          (8, LANES, LANES))
    wo_0 = ab_w_out[0].astype(BF16)

    w = attn_w_in[0]
    w_attn = jnp.concatenate([_attn_q_cols(w[:, :d]), _pair_cols(w[:, d:d + 2 * LANES]), w[:, d + 2 * LANES:]],
                             axis=1).astype(BF16)
    lane_w = lambda v: jnp.concatenate([v[:32], v[:32], v[32:], v[32:]])
    nwq = jnp.stack([lane_w(attn_q_norm_w[0]) * (HEAD_DIM ** -0.5), lane_w(attn_k_norm_w[0])]
                    + [jnp.zeros((LANES,), F32)] * 6)
    wo_1 = _attn_o_rows(attn_w_out[0]).astype(BF16)

    y, g, gt, *states = _inproj_sweep(*xs, ab_0, w_main, wg, wgt, gb, gbt, cos_t, sin_t, lgk, cw, nb, tpb)
    x_mid, y_attn = _mixer_ffn(y, g, gt, states, lg.reshape(-1), lgk, lgk.T, nw, xs, prm_0, wo_0, wi_all, w2_all, 0,
                               ab_1, w_attn, nwq, cos_t, sin_t, nb, tpb)
    out = _attn_ffn(y_attn, attn_sink[0].astype(F32), x_mid, prm_1, wo_1, wi_all, w2_all, 1, nb, nc, ctx_chunks)
    return out.reshape(nb, seq, d)
```

```python
import functools

import numpy as np
import jax
import jax.numpy as jnp
from jax import lax
from jax.experimental import pallas as pl
from jax.experimental.pallas import tpu as pltpu

F32 = jnp.float32
BF16 = jnp.bfloat16

HEAD_DIM = 64
CHUNK = 128
GRID_W = 64
ROPE_BASE = 10000.0
EPS = 1e-6
H_KV = 4
LANES = 128
ROW_TILE = 256
HALO = 16
NEG = -1e30
VMEM_LIMIT = 56 * 1024 * 1024


def _cparams(*sem):
    return pltpu.CompilerParams(dimension_semantics=sem, vmem_limit_bytes=VMEM_LIMIT)


def _const_spec(shape):
    nd = len(shape)
    return pl.BlockSpec(shape, lambda *_: (0,) * nd, pipeline_mode=pl.Buffered(1))


def _smem_spec():
    return pl.BlockSpec(memory_space=pltpu.SMEM)


def _lane(shape=(CHUNK, LANES)):
    return lax.broadcasted_iota(jnp.int32, shape, len(shape) - 1)


def _dot(a, b):
    return jnp.dot(a, b, preferred_element_type=F32)


def _dot_nt(a, b):
    return lax.dot_general(a, b, (((1,), (1,)), ((), ())), preferred_element_type=F32)


def _split3(x):
    hi = x.astype(BF16)
    r = x - hi.astype(F32)
    mid = r.astype(BF16)
    lo = (r - mid.astype(F32)).astype(BF16)
    return hi, mid, lo


def _log_sigmoid(x):
    return jnp.minimum(x, 0.0) - jnp.log1p(jnp.exp(-jnp.abs(x)))


def _rope(x, cos, sin_signed):
    return x * cos + pltpu.roll(x, LANES // 2, 1) * sin_signed


def _mod_kernel(rows_ref, w_ref, b_ref, o_ref):
    a = rows_ref[...]
    a = a * jax.nn.sigmoid(a)
    a_hi = a.astype(BF16)
    a_lo = (a - a_hi.astype(F32)).astype(BF16)
    w = w_ref[...]
    w_hi = w.astype(BF16)
    w_lo = (w - w_hi.astype(F32)).astype(BF16)
    o_ref[...] = _dot(a_hi, w_hi) + _dot(a_hi, w_lo) + _dot(a_lo, w_hi) + b_ref[...]


def _modulation(rows, ada_w, ada_b):
    depth, d, n = ada_w.shape
    tn = n // 4
    return pl.pallas_call(
        _mod_kernel,
        grid=(depth, n // tn),
        in_specs=[pl.BlockSpec((8, d), lambda l, j: (0, 0)),
                  pl.BlockSpec((None, d, tn), lambda l, j: (l, 0, j)),
                  pl.BlockSpec((None, 1, tn), lambda l, j: (l, 0, j))],
        out_specs=pl.BlockSpec((None, 8, tn), lambda l, j: (l, 0, j)),
        out_shape=jax.ShapeDtypeStruct((depth, 8, n), F32),
        compiler_params=_cparams("arbitrary", "arbitrary"),
        name="adaln_modulation",
    )(rows, ada_w, ada_b.reshape(depth, 1, n))


def _norm_mod(x, ab_ref):
    ms = jnp.mean(x * x, axis=-1, keepdims=True)
    h = (x * lax.rsqrt(ms + EPS)) * ab_ref[0:1, :] + ab_ref[1:2, :]
    return h.astype(BF16)


def _ctx_or_latent_rows(ctx_ref, x_ref, tile, tiles_per_batch):
    return jnp.where(tile % tiles_per_batch == 0, ctx_ref[...], x_ref[...])


def _split_row_specs(tm, d, tpb, tile_of):
    lat = tpb - 1
    return [pl.BlockSpec((tm, d), lambda i: (tile_of(i) // tpb, 0)),
            pl.BlockSpec((tm, d), lambda i: ((tile_of(i) // tpb) * lat + jnp.maximum(tile_of(i) % tpb - 1, 0), 0))]


def _cumsum_cols(tri_bf, lf):
    hi, mid, lo = _split3(lf)
    return _dot(tri_bf, hi) + _dot(tri_bf, mid) + _dot(tri_bf, lo)


def _cumsum_rows(lf, tri_bf):
    hi, mid, lo = _split3(lf)
    return _dot(hi, tri_bf) + _dot(mid, tri_bf) + _dot(lo, tri_bf)


def _ret_state_update(s_ref, p, k2, vt, kdec, cd_lanes, bd):
    kf = (k2.astype(F32) * kdec).astype(BF16)
    s_ref[p] = s_ref[p] * cd_lanes + jnp.where(bd, _dot(vt, kf), 0.0)


def _mlstm_state_update(c_ref, n_ref, m_ref, k_pairs, vt_pairs, c_all, bend, col0, lo, bd):
    cmax = jnp.max(c_all, axis=0, keepdims=True)
    w_all = jnp.exp(c_all - cmax)
    m_old = m_ref[0:1, :]
    mrel = jnp.maximum(m_old, cmax)
    a_row = jnp.exp(m_old - mrel)
    bb_row = jnp.exp(cmax - mrel)
    m_ref[0:1, :] = bend + mrel
    lo_row = lo[0:1, :]
    for p in range(4):
        h0 = col0 + 2 * p
        kw = k_pairs[p] * jnp.where(lo, w_all[:, h0:h0 + 1], w_all[:, h0 + 1:h0 + 2])
        kvt = _dot(vt_pairs[p], kw.astype(BF16))
        nloc = jnp.sum(kw, axis=0, keepdims=True)
        a_l = jnp.where(lo_row, a_row[:, h0:h0 + 1], a_row[:, h0 + 1:h0 + 2])
        bb_l = jnp.where(lo_row, bb_row[:, h0:h0 + 1], bb_row[:, h0 + 1:h0 + 2])
        c_ref[p] = c_ref[p] * a_l + jnp.where(bd, kvt, 0.0) * bb_l
        n_new = (n_ref[p, 0:1, :] + n_ref[p, 1:2, :]) * a_l + nloc * bb_l
        n_ref[p, 0:1, :] = jnp.where(lo_row, n_new, 0.0)
        n_ref[p, 1:2, :] = jnp.where(lo_row, 0.0, n_new)


def _mlstm_dir_weights(st, qn_row, c_col, bt_row, m_prev, tri):
    dl = jnp.where(tri, c_col + bt_row, NEG)
    mx = jnp.max(dl, axis=0, keepdims=True)
    al = bt_row + m_prev
    m_t = jnp.maximum(al, mx)
    w = jnp.exp(dl - m_t)
    a_t = jnp.exp(al - m_t)
    sw = st * w
    den = jnp.sum(sw, axis=0, keepdims=True) + a_t * qn_row
    r = 1.0 / jnp.maximum(jnp.abs(den), jnp.exp(-m_t))
    return sw * r, a_t * r


def _heads_out(ht, nw_tab):
    rows = []
    for a in range(2):
        ha = ht[a * HEAD_DIM:(a + 1) * HEAD_DIM, :]
        ms = jnp.mean(ha * ha, axis=0, keepdims=True)
        rows.append(ha * lax.rsqrt(ms + EPS))
    return (jnp.concatenate(rows, axis=0) * nw_tab).T


def _inproj_sweep_kernel(ctx_ref, x_ref, ab_ref, w_ref, wg_ref, wgt_ref, gb_ref, gbt_ref, cos_ref, sin_ref,
                         lgk_ref, cw_ref,
                         y_ref, g_ref, gt_ref, sret_ref, cm_ref, nm_ref, mm_ref, qk_ref,
                         s_scr, c_scr, n_scr, m_scr, kdec_scr,
                         p_rk, p_rv, p_mv, p_mqk, p_g, next_row_scr, *, tiles, tpb):
    i = pl.program_id(0)
    order = lambda t: jnp.where(t % tpb == 0, 0, tpb - t % tpb)
    jt_a = order(jnp.maximum(i - 1, 0))
    lane = _lane()
    sub = lax.broadcasted_iota(jnp.int32, (CHUNK, LANES), 0)
    lo = lane < HEAD_DIM
    lo_row = lo[0:1, :]
    bd_ret = (sub >= HEAD_DIM) == ((lane & 32) != 0)
    bd_m = (sub >= HEAD_DIM) == (lane >= HEAD_DIM)
    gw = 4 * LANES

    @pl.when(i == 0)
    def _():
        pos = sub.astype(F32)
        for p in range(4):
            kdec_scr[p] = jnp.exp(lgk_ref[4 + p:5 + p, :] * pos)
        for ref in (p_rk, p_rv, p_mv, p_mqk, p_g, next_row_scr):
            ref[...] = jnp.zeros_like(ref)

    @pl.when(jt_a == 0)
    def _():
        s_scr[...] = jnp.zeros_like(s_scr)
        c_scr[...] = jnp.zeros_like(c_scr)
        n_scr[...] = jnp.zeros_like(n_scr)
        m_scr[...] = jnp.zeros_like(m_scr)

    tile_i = jnp.minimum(i, tiles - 1)
    jt_i = order(tile_i)
    hb = _norm_mod(jnp.where(jt_i == 0, ctx_ref[...], x_ref[...]), ab_ref)
    cos, sin = cos_ref[...], sin_ref[...]

    def project(j):
        acc = _dot(hb, w_ref[:, j * gw:(j + 1) * gw])
        if j in (0, 1):
            if j == 0:
                acc = acc * (HEAD_DIM ** -0.5)
            acc = jnp.concatenate([_rope(acc[:, p * LANES:(p + 1) * LANES], cos, sin) for p in range(4)], axis=1)
        elif j == 3:
            acc = acc * jax.nn.sigmoid(acc)
        elif j == 7:
            acc = jax.nn.sigmoid(acc)
        return acc.astype(BF16)

    raw_q, raw_k = project(4), project(5)

    prev_on = jnp.where(jt_a <= 1, 0.0, 1.0).astype(F32)
    next_on = jnp.where((jt_a == 0) | (jt_a == tpb - 1), 0.0, 1.0).astype(F32)
    cur = p_mqk[...].astype(F32)
    n = cur.shape[0]
    row = lax.broadcasted_iota(jnp.int32, cur.shape, 0)
    prev_row = jnp.concatenate([raw_q[n - HALO:, :], raw_k[n - HALO:, :]], axis=1)[HALO - 1:HALO, :].astype(F32)
    xm = jnp.where(row == 0, prev_row * prev_on, pltpu.roll(cur, 1, 0))
    xp = jnp.where(row == n - 1, next_row_scr[0:1, :] * next_on, pltpu.roll(cur, n - 1, 0))
    conv = cw_ref[3:4, :] + cw_ref[0:1, :] * xm + cw_ref[1:2, :] * cur + cw_ref[2:3, :] * xp
    qk = conv * jax.nn.sigmoid(conv)
    qk_ref[:, 0:gw] = (qk[:, 0:gw] * (HEAD_DIM ** -0.5)).astype(BF16)
    qk_ref[:, gw:] = qk[:, gw:].astype(BF16)
    le_bf = (sub <= lane).astype(BF16)
    chunks = (1, 0)
    pre = {}
    for blk in chunks:
        rows = slice(blk * CHUNK, (blk + 1) * CHUNK)
        g = p_g[rows, :]
        bal = pltpu.roll(_cumsum_cols(le_bf, _log_sigmoid(g)), LANES - 8, 1)
        c_all = g - bal
        cmax = jnp.max(c_all, axis=0, keepdims=True)
        w_all = jnp.exp(c_all - cmax)
        vts, ks, nlocs = [], [], []
        for p in range(8):
            sl = slice((p % 4) * LANES, (p % 4 + 1) * LANES)
            if p < 4:
                v2 = p_rv[rows, sl]
                ks.append((p_rk[rows, sl].astype(F32) * kdec_scr[p]).astype(BF16))
            else:
                v2 = p_mv[rows, sl]
                h0 = 16 + 2 * (p - 4)
                kw = qk[rows, gw + (p - 4) * LANES:gw + (p - 3) * LANES] * jnp.where(
                    lo, w_all[:, h0:h0 + 1], w_all[:, h0 + 1:h0 + 2])
                ks.append(kw.astype(BF16))
                nlocs.append(jnp.sum(kw, axis=0, keepdims=True))
            vts.append(v2.astype(F32).T.astype(BF16))
        pre[blk] = (vts, ks, nlocs, cmax, bal[0:1, :])

    cur_rk = project(1)
    y_ref[:, 1 * gw:2 * gw] = cur_rk
    y_ref[:, 0:gw] = project(0)
    kvs = {blk: [_dot(pre[blk][0][p], pre[blk][1][p]) for p in range(8)] for blk in chunks}
    cur_rv = project(2)
    y_ref[:, 2 * gw:3 * gw] = cur_rv
    y_ref[:, 3 * gw:4 * gw] = project(3)

    for blk in chunks:
        _, _, nlocs, cmax, bend = pre[blk]
        sret_ref[blk] = s_scr[...].astype(BF16)
        cm_ref[blk] = c_scr[...].astype(BF16)
        nm_ref[blk] = n_scr[...].astype(BF16)
        mm_ref[blk] = m_scr[...]
        m_old = m_scr[0:1, :]
        mrel = jnp.maximum(m_old, cmax)
        a_row = jnp.exp(m_old - mrel)
        bb_row = jnp.exp(cmax - mrel)
        m_scr[0:1, :] = bend + mrel
        for p in range(4):
            cd = jnp.exp(lgk_ref[4 + p:5 + p, :] * float(CHUNK))
            s_scr[p] = s_scr[p] * cd + jnp.where(bd_ret, kvs[blk][p], 0.0)
            h0 = 16 + 2 * p
            a_l = jnp.where(lo_row, a_row[:, h0:h0 + 1], a_row[:, h0 + 1:h0 + 2])
            bb_l = jnp.where(lo_row, bb_row[:, h0:h0 + 1], bb_row[:, h0 + 1:h0 + 2])
            c_scr[p] = c_scr[p] * a_l + jnp.where(bd_m, kvs[blk][4 + p], 0.0) * bb_l
            n_new = (n_scr[p, 0:1, :] + n_scr[p, 1:2, :]) * a_l + nlocs[p] * bb_l
            n_scr[p, 0:1, :] = jnp.where(lo_row, n_new, 0.0)
            n_scr[p, 1:2, :] = jnp.where(lo_row, 0.0, n_new)

    cur_mv = project(6)
    y_ref[:, 4 * gw:5 * gw] = cur_mv
    y_ref[:, 5 * gw:6 * gw] = project(7)
    gates = _dot(hb, wg_ref[...]) + gb_ref[...]
    g_ref[...] = gates
    gt_ref[...] = _dot_nt(wgt_ref[...], hb) + gbt_ref[...]
    next_row_scr[...] = p_mqk[0:HALO, :].astype(F32)
    p_mqk[:, 0:gw] = raw_q
    p_mqk[:, gw:] = raw_k
    p_rk[...] = cur_rk
    p_rv[...] = cur_rv
    p_mv[...] = cur_mv
    p_g[...] = gates


def _inproj_sweep(ctx2, x2, ab, w, wg, wgt, gb, gbt, cos, sin, lgk, cw, nb, tpb):
    d = x2.shape[1]
    tm = ROW_TILE
    cpt = tm // CHUNK
    gw = 4 * LANES
    tiles = nb * tpb
    r = tiles * tm
    lat = tpb - 1
    seq_tile = lambda t: (t // tpb) * tpb + jnp.where(t % tpb == 0, 0, tpb - t % tpb)
    cur = lambda i: jnp.minimum(i, tiles - 1)
    tile_i = lambda i: seq_tile(cur(i))
    tile_a = lambda i: seq_tile(jnp.maximum(i - 1, 0))
    in_batch = lambda i: tile_i(i) % tpb
    sel = lambda i: ((tile_i(i) // tpb) * 2 + jnp.minimum(in_batch(i), 1), 0, 0)
    state = lambda *dims: pl.BlockSpec((cpt,) + dims, lambda i: (tile_a(i),) + (0,) * len(dims))
    nchunks = tiles * cpt
    return pl.pallas_call(
        functools.partial(_inproj_sweep_kernel, tiles=tiles, tpb=tpb),
        grid=(tiles + 1,),
        in_specs=[pl.BlockSpec((tm, d), lambda i: (tile_i(i) // tpb, 0)),
                  pl.BlockSpec((tm, d), lambda i: ((tile_i(i) // tpb) * lat + jnp.maximum(in_batch(i) - 1, 0), 0)),
                  pl.BlockSpec((None, 2, d), sel),
                  _const_spec(w.shape), _const_spec(wg.shape), _const_spec(wgt.shape),
                  _const_spec(gb.shape), _const_spec(gbt.shape),
                  pl.BlockSpec((tm, LANES), lambda i: (in_batch(i), 0)),
                  pl.BlockSpec((tm, LANES), lambda i: (in_batch(i), 0)),
                  _const_spec(lgk.shape), _const_spec(cw.shape)],
        out_specs=[pl.BlockSpec((tm, 6 * gw), lambda i: (tile_i(i), 0)),
                   pl.BlockSpec((tm, LANES), lambda i: (tile_i(i), 0)),
                   pl.BlockSpec((32, tm), lambda i: (0, tile_i(i))),
                   state(4, CHUNK, LANES), state(4, CHUNK, LANES), state(4, HALO, LANES), state(8, LANES),
                   pl.BlockSpec((tm, 2 * gw), lambda i: (tile_a(i), 0))],
        out_shape=[jax.ShapeDtypeStruct((r, 6 * gw), BF16),
                   jax.ShapeDtypeStruct((r, LANES), F32),
                   jax.ShapeDtypeStruct((32, r), F32),
                   jax.ShapeDtypeStruct((nchunks, 4, CHUNK, LANES), BF16),
                   jax.ShapeDtypeStruct((nchunks, 4, CHUNK, LANES), BF16),
                   jax.ShapeDtypeStruct((nchunks, 4, HALO, LANES), BF16),
                   jax.ShapeDtypeStruct((nchunks, 8, LANES), F32),
                   jax.ShapeDtypeStruct((r, 2 * gw), BF16)],
        scratch_shapes=[pltpu.VMEM((4, CHUNK, LANES), F32), pltpu.VMEM((4, CHUNK, LANES), F32),
                        pltpu.VMEM((4, HALO, LANES), F32), pltpu.VMEM((8, LANES), F32),
                        pltpu.VMEM((4, CHUNK, LANES), F32),
                        pltpu.VMEM((tm, gw), BF16), pltpu.VMEM((tm, gw), BF16), pltpu.VMEM((tm, gw), BF16),
                        pltpu.VMEM((tm, 2 * gw), BF16), pltpu.VMEM((tm, LANES), F32),
                        pltpu.VMEM((HALO, 2 * gw), F32)],
        compiler_params=_cparams("arbitrary"),
        name="inproj_bwd_sweep",
    )(ctx2, x2, ab, w, wg, wgt, gb, gbt, cos, sin, lgk, cw)


class _Bag:
    def __init__(self, **kw):
        self.__dict__.update(kw)


def _mixer_chunk_stages(blk, r):
    rows = slice(blk * CHUNK, (blk + 1) * CHUNK)
    lane = _lane()
    sub = lax.broadcasted_iota(jnp.int32, (CHUNK, LANES), 0)
    lo = lane < HEAD_DIM
    sub_lo = sub < HEAD_DIM
    mask_ret = [(lane & 32) == 0, (lane & 32) != 0]
    mask_nat = [lo, lane >= HEAD_DIM]
    bd_ret = (sub >= HEAD_DIM) == ((lane & 32) != 0)
    bd_m = (sub >= HEAD_DIM) == (lane >= HEAD_DIM)
    le = sub <= lane
    ge = sub >= lane

    g = r.g_ref[rows, :]
    gt = r.gt_ref[:, rows]
    lf_col = _log_sigmoid(g)
    lf_row = _log_sigmoid(gt)
    le_bf = le.astype(BF16)
    ge_bf = ge.astype(BF16)
    bal_f = pltpu.roll(_cumsum_cols(ge_bf, lf_col), LANES - 8, 1)
    bal_b = pltpu.roll(_cumsum_cols(le_bf, lf_col), LANES - 8, 1)
    cf_all = g - bal_f
    cb_all = g - bal_b
    bf_row = _cumsum_rows(lf_row, le_bf)
    bb_row = _cumsum_rows(lf_row, ge_bf)
    qb, kb, vts, kf32, qt, vbd = [], [], [], [], [], []
    for p in range(8):
        sl = slice((p % 4) * LANES, (p % 4 + 1) * LANES)
        if p < 4:
            q2, k2, v2 = r.rq_ref[rows, sl], r.rk_ref[rows, sl], r.rv_ref[rows, sl]
            qf = q2.astype(F32)
            kf = None
        else:
            q2 = r.qk_ref[rows, sl]
            k2 = r.qk_ref[rows, 4 * LANES + (p - 4) * LANES:4 * LANES + (p - 3) * LANES]
            v2 = r.mv_ref[rows, sl]
            qf, kf = q2.astype(F32), k2.astype(F32)
        vt = v2.astype(F32).T.astype(BF16)
        qb.append(q2)
        kb.append(k2)
        vts.append(vt)
        kf32.append(kf)
        qt.append(qf.T)
        vbd.append([jnp.where(sub_lo, vt, jnp.zeros_like(vt)), jnp.where(sub_lo, jnp.zeros_like(vt), vt)])
    yield

    st2 = []
    for p in range(8):
        masks = mask_ret if p < 4 else mask_nat
        zero = jnp.zeros_like(qb[p])
        qstack = jnp.concatenate([jnp.where(masks[0], qb[p], zero), jnp.where(masks[1], qb[p], zero)], axis=0)
        st2.append(_dot_nt(kb[p], qstack))
    yield

    qn_f = [_dot_nt(r.n_scr[p].astype(BF16), qb[4 + p]) for p in range(4)]
    qn_b = [_dot_nt(r.nmb_ref[blk, p], qb[4 + p]) for p in range(4)]
    lhs, rhs = [], []
    for p in range(8):
        if p < 4:
            pts = [(st2[p][:, a * LANES:(a + 1) * LANES] * r.dm_scr[2 * p + a]).astype(BF16) for a in range(2)]
            x_f, x_b = r.dec_scr[0, p], r.dec_scr[1, p]
            old = [r.s_scr[p].astype(BF16), r.sretb_ref[blk, p]]
        else:
            pts, cf, cb = [], [], []
            for a in range(2):
                h = 2 * (p - 4) + a
                st = st2[p][:, a * LANES:(a + 1) * LANES]
                pf, coef_f = _mlstm_dir_weights(st, qn_f[p - 4][a:a + 1, :], cf_all[:, h:h + 1],
                                                bf_row[8 + h:9 + h, :], r.m_state[0:1, h:h + 1], le)
                pb, coef_b = _mlstm_dir_weights(st, qn_b[p - 4][a:a + 1, :], cb_all[:, 16 + h:17 + h],
                                                bb_row[24 + h:25 + h, :], r.mmb_ref[blk, 0:1, 16 + h:17 + h], ge)
                pts.append((pf + pb).astype(BF16))
                cf.append(coef_f)
                cb.append(coef_b)
            x_f = jnp.where(sub_lo, cf[0], cf[1])
            x_b = jnp.where(sub_lo, cb[0], cb[1])
            old = [r.c_scr[p - 4].astype(BF16), r.cmb_ref[blk, p - 4]]
        lhs.append(jnp.concatenate(vbd[p] + old, axis=1))
        rhs.append(jnp.concatenate(pts + [(qt[p] * x_f).astype(BF16), (qt[p] * x_b).astype(BF16)], axis=0))
    yield

    ht = [_dot(lhs[p], rhs[p]) for p in range(8)]
    yield

    for p in range(8):
        sl = slice((p % 4) * LANES, (p % 4 + 1) * LANES)
        gate_ref = r.rg_ref if p < 4 else r.mo_ref
        y = _heads_out(ht[p], r.nw_ref[p])
        new = (y * gate_ref[rows, sl].astype(F32)).astype(BF16)
        r.mix_scr[rows, p * LANES:(p + 1) * LANES] = jnp.where(r.live, new, r.mix_scr[rows, p * LANES:(p + 1) * LANES])
    yield

    for p in range(4):
        cd = jnp.exp(r.lgk_ref[p:p + 1, :] * float(CHUNK))
        _ret_state_update(r.s_scr, p, kb[p], vts[p], r.dec_scr[2, p], cd, bd_ret)
    _mlstm_state_update(r.c_scr, r.n_scr, r.m_state, kf32[4:], vts[4:], cf_all, bal_f[CHUNK - 1:CHUNK, :],
                        0, lo, bd_m)
    yield


def _attn_inproj_stages(x, ab_ref, w_ref, nw_ref, cos, sin, y_ref):
    hb = _norm_mod(x, ab_ref)
    r2 = lax.broadcasted_iota(jnp.int32, (2 * LANES, 2 * LANES), 0)
    c2 = lax.broadcasted_iota(jnp.int32, (2 * LANES, 2 * LANES), 1)
    same_head = (((r2 ^ c2) & (LANES | 32)) == 0).astype(BF16)
    acc_q = _dot(hb, w_ref[:, 0:8 * LANES])
    acc_k = _dot(hb, w_ref[:, 8 * LANES:10 * LANES])
    y_ref[:, 10 * LANES:12 * LANES] = _dot(hb, w_ref[:, 10 * LANES:12 * LANES]).astype(BF16)
    yield
    for j in range(5):
        acc = acc_q[:, j * 2 * LANES:(j + 1) * 2 * LANES] if j < 4 else acc_k
        sq = acc * acc
        hi = sq.astype(BF16)
        lo = (sq - hi.astype(F32)).astype(BF16)
        ms = (_dot(hi, same_head) + _dot(lo, same_head)) * (1.0 / HEAD_DIM)
        nrm = acc * lax.rsqrt(ms + EPS)
        nw = nw_ref[0:1, :] if j < 4 else nw_ref[1:2, :]
        for v in range(2):
            ls = slice(v * LANES, (v + 1) * LANES)
            y_ref[:, (2 * j + v) * LANES:(2 * j + v + 1) * LANES] = _rope(nrm[:, ls] * nw, cos, sin).astype(BF16)
        if j in (1, 4):
            yield


def _ffn_splits(d_ff, pieces):
    blocks = d_ff // (2 * LANES)
    assert blocks * 2 * LANES == d_ff and blocks >= pieces
    cuts = [((i * blocks) // pieces) * 2 * LANES for i in range(pieces)]
    return cuts + [d_ff]


def _mixer_ffn_kernel(lg_ref, y_ref, qk_ref,
                      g_ref, gt_ref, sretb_ref, cmb_ref, nmb_ref, mmb_ref,
                      lgk_ref, lgkt_ref, nw_ref,
                      ctx_ref, x_ref, p_ref, wo_ref, wi_ref, w2_ref,
                      ab2_ref, wa_ref, nwa_ref, cos_ref, sin_ref,
                      o_ref, y2_ref,
                      s_scr, c_scr, n_scr, m_state, dm_scr, dec_scr, mix_scr, act_scr, x2_scr,
                      *, tiles, tpb, d_ff):
    s = pl.program_id(0)
    jt = jnp.minimum(s, tiles - 1) % tpb
    gw = 4 * LANES
    rq_ref, rk_ref, rv_ref, rg_ref, mv_ref, mo_ref = [y_ref.at[:, j * gw:(j + 1) * gw] for j in range(6)]
    lane = _lane()
    sub = lax.broadcasted_iota(jnp.int32, (CHUNK, LANES), 0)

    @pl.when(s == 0)
    def _():
        mix_scr[...] = jnp.zeros_like(mix_scr)
        x2_scr[...] = jnp.zeros_like(x2_scr)
        le = sub <= lane
        ge = sub >= lane
        spos = sub.astype(F32)
        tpos = lane.astype(F32)
        diff = (lane - sub).astype(F32)
        for h in range(8):
            dm_scr[h] = (jnp.where(le, jnp.exp(lg_ref[h] * diff), 0.0)
                         + jnp.where(ge, jnp.exp(lg_ref[8 + h] * (-diff)), 0.0))
        for p in range(4):
            dec_scr[0, p] = jnp.exp(lgkt_ref[:, p:p + 1] * (tpos + 1.0))
            dec_scr[1, p] = jnp.exp(lgkt_ref[:, 4 + p:5 + p] * (float(CHUNK) - tpos))
            dec_scr[2, p] = jnp.exp(lgk_ref[p:p + 1, :] * (float(CHUNK) - 1.0 - spos))

    @pl.when(jt == 0)
    def _():
        s_scr[...] = jnp.zeros_like(s_scr)
        c_scr[...] = jnp.zeros_like(c_scr)
        n_scr[...] = jnp.zeros_like(n_scr)
        m_state[...] = jnp.zeros_like(m_state)

    r = _Bag(rq_ref=rq_ref, rk_ref=rk_ref, rv_ref=rv_ref, rg_ref=rg_ref, qk_ref=qk_ref, mv_ref=mv_ref, mo_ref=mo_ref,
             g_ref=g_ref, gt_ref=gt_ref, sretb_ref=sretb_ref, cmb_ref=cmb_ref, nmb_ref=nmb_ref, mmb_ref=mmb_ref,
             lgk_ref=lgk_ref, nw_ref=nw_ref, s_scr=s_scr, c_scr=c_scr, n_scr=n_scr, m_state=m_state,
             dm_scr=dm_scr, dec_scr=dec_scr, mix_scr=mix_scr, live=s < tiles)
    cuts = _ffn_splits(d_ff, 4)
    ffn_piece = lambda i: _ffn_cols(h, wi_ref, act_scr, cuts[i], cuts[i + 1], d_ff)

    x = _ctx_or_latent_rows(ctx_ref, x_ref, jnp.clip(s - 1, 0, tiles - 1), tpb)
    x1, h = _ffn_in(x, mix_scr[...], p_ref, wo_ref)

    nxt = _attn_inproj_stages(x2_scr[...], ab2_ref, wa_ref, nwa_ref, cos_ref[...], sin_ref[...], y2_ref)
    chunk_a, chunk_b = _mixer_chunk_stages(0, r), _mixer_chunk_stages(1, r)
    next(chunk_a), next(chunk_b)
    next(chunk_a), next(chunk_b)
    ffn_piece(0)
    next(chunk_a), next(chunk_a)
    next(nxt)
    ffn_piece(1)
    next(chunk_a), next(chunk_a)
    next(nxt)
    ffn_piece(2)
    next(chunk_b), next(chunk_b)
    next(nxt)
    ffn_piece(3)
    next(chunk_b), next(chunk_b)
    x2 = x1 + p_ref[3:4, :] * _dot(act_scr[...], w2_ref[...])
    o_ref[...] = x2
    x2_scr[...] = x2


def _mixer_ffn(y, g, gt, states, lg_smem, lgk, lgkt, nw, xs, prm, wo, wi_all, w2_all, layer,
               ab_next, w_next, nw_next, cos, sin, nb, tpb):
    gw = 4 * LANES
    tm = ROW_TILE
    cpt = tm // CHUNK
    sretb, cmb, nmb, mmb, qk_act = states
    tiles = nb * tpb
    d = xs[-1].shape[1]
    d_ff = w2_all.shape[1]
    n_next = w_next.shape[1]
    mix = lambda s: jnp.minimum(s, tiles - 1)
    ffn = lambda s: jnp.clip(s - 1, 0, tiles - 1)
    nxt = lambda s: jnp.maximum(s - 2, 0)
    state_spec = lambda a: pl.BlockSpec((cpt,) + a.shape[1:], lambda s: (mix(s),) + (0,) * (a.ndim - 1))
    sel = lambda t: ((t // tpb) * 2 + jnp.minimum(t % tpb, 1), 0, 0)
    layer_spec = lambda a: pl.BlockSpec((None,) + a.shape[1:], lambda s: (layer, 0, 0),
                                        pipeline_mode=pl.Buffered(1))
    return pl.pallas_call(
        functools.partial(_mixer_ffn_kernel, tiles=tiles, tpb=tpb, d_ff=d_ff),
        grid=(tiles + 2,),
        in_specs=[_smem_spec(),
                  pl.BlockSpec((tm, 6 * gw), lambda s: (mix(s), 0)),
                  pl.BlockSpec((tm, 2 * gw), lambda s: (mix(s), 0)),
                  pl.BlockSpec((tm, LANES), lambda s: (mix(s), 0)),
                  pl.BlockSpec((32, tm), lambda s: (0, mix(s))),
                  state_spec(sretb), state_spec(cmb), state_spec(nmb), state_spec(mmb),
                  _const_spec(lgk.shape), _const_spec(lgkt.shape), _const_spec(nw.shape)]
                 + _split_row_specs(tm, d, tpb, ffn)
                 + [pl.BlockSpec((None, 8, d), lambda s: sel(ffn(s))),
                    _const_spec(wo.shape), layer_spec(wi_all), layer_spec(w2_all),
                    pl.BlockSpec((None, 2, d), lambda s: sel(nxt(s))),
                    _const_spec(w_next.shape), _const_spec(nw_next.shape),
                    pl.BlockSpec((tm, LANES), lambda s: (nxt(s) % tpb, 0)),
                    pl.BlockSpec((tm, LANES), lambda s: (nxt(s) % tpb, 0))],
        out_specs=[pl.BlockSpec((tm, d), lambda s: (ffn(s), 0)),
                   pl.BlockSpec((tm, n_next), lambda s: (nxt(s), 0))],
        out_shape=[jax.ShapeDtypeStruct((tiles * tm, d), F32),
                   jax.ShapeDtypeStruct((tiles * tm, n_next), BF16)],
        scratch_shapes=[pltpu.VMEM((4, CHUNK, LANES), F32), pltpu.VMEM((4, CHUNK, LANES), F32),
                        pltpu.VMEM((4, HALO, LANES), F32), pltpu.VMEM((8, LANES), F32),
                        pltpu.VMEM((8, CHUNK, LANES), F32), pltpu.VMEM((3, 4, CHUNK, LANES), F32),
                        pltpu.VMEM((tm, 2 * gw), BF16), pltpu.VMEM((tm, d_ff), BF16),
                        pltpu.VMEM((tm, d), F32)],
        compiler_params=_cparams("arbitrary"),
        name="ret_mlstm_mixer_ffn",
    )(lg_smem, y, qk_act, g, gt, sretb, cmb, nmb, mmb, lgk, lgkt, nw,
      *xs, prm, wo, wi_all, w2_all, ab_next, w_next, nw_next, cos, sin)


def _ffn_in(x, m, p_ref, wo_ref):
    x1 = x + p_ref[0:1, :] * _dot(m, wo_ref[...])
    ms = jnp.mean(x1 * x1, axis=-1, keepdims=True)
    return x1, ((x1 * lax.rsqrt(ms + EPS)) * p_ref[1:2, :] + p_ref[2:3, :]).astype(BF16)


def _ffn_cols(h, wi_ref, act_scr, lo, hi, d_ff):
    gate = _dot(h, wi_ref[:, lo:hi])
    up = _dot(h, wi_ref[:, d_ff + lo:d_ff + hi])
    act_scr[:, lo:hi] = (gate * jax.nn.sigmoid(gate) * up).astype(BF16)


def _window_bias():
    kk = np.arange(CHUNK)[:, None]
    t = np.arange(CHUNK)[None, :]
    tabs = []
    for has_prev, has_next in ((False, True), (True, True), (True, False)):
        prev_ok = (kk >= t) & has_prev
        next_ok = (kk <= t) & has_next
        tabs.append(np.where(np.concatenate([prev_ok, next_ok], axis=0), 0.0, NEG))
    return jnp.asarray(np.stack(tabs), F32)


def _attn_ffn_kernel(sink_ref, qkv_ref, kvp_ref, kvn_ref, kvx_ref, bias_ref,
                     x_ref, p_ref, wo_ref, wi_ref, w2_ref, o_ref, m_scr, act_scr, *, tiles, lat_tiles, d_ff):
    s = pl.program_id(0)
    q_ref = qkv_ref
    kw = 2 * LANES
    kc_ref, vc_ref = qkv_ref.at[:, 8 * LANES:8 * LANES + kw], qkv_ref.at[:, 8 * LANES + kw:8 * LANES + 2 * kw]
    kp_ref, vp_ref = kvp_ref.at[:, 0:kw], kvp_ref.at[:, kw:2 * kw]
    kn_ref, vn_ref = kvn_ref.at[:, 0:kw], kvn_ref.at[:, kw:2 * kw]
    kx_ref, vx_ref = kvx_ref.at[:, 0:kw], kvx_ref.at[:, kw:2 * kw]

    @pl.when(s == 0)
    def _():
        m_scr[...] = jnp.zeros_like(m_scr)

    j = jnp.minimum(s, tiles - 1) % lat_tiles
    grp = lax.broadcasted_iota(jnp.int32, (1, 4 * CHUNK), 1) // CHUNK
    lane = _lane()
    mask_q = [(lane & 32) == 0, (lane & 32) != 0]
    tile4 = lambda b: jnp.concatenate([b] * 4, axis=1)
    biases = [tile4(jnp.where(j == 0, bias_ref[0], bias_ref[1])),
              tile4(jnp.where(j == lat_tiles - 1, bias_ref[2], bias_ref[1]))]
    cuts = _ffn_splits(d_ff, 3)

    def ffn_up(i):
        _ffn_cols(h, wi_ref, act_scr, cuts[i], cuts[i + 1], d_ff)

    def scores(u):
        blk, kvp, a = u // 4, (u % 4) // 2, u % 2
        rows = slice(blk * CHUNK, (blk + 1) * CHUNK)
        qs = jnp.concatenate(
            [jnp.where(mask_q[a], q_ref[rows, (kvp * 4 + g) * LANES:(kvp * 4 + g + 1) * LANES],
                       jnp.zeros((CHUNK, LANES), BF16)) for g in range(4)], axis=0)
        return _dot_nt(kcats[kvp][blk], qs)

    def softmax_pv(u, st):
        blk, kv = u // 4, u % 4
        bias = biases[blk]
        st = jnp.concatenate([st[0:CHUNK] + bias[0:CHUNK], st[CHUNK:2 * CHUNK],
                              st[2 * CHUNK:3 * CHUNK] + bias[CHUNK:2 * CHUNK], st[3 * CHUNK:]], axis=0)
        snk = jnp.where(grp == 0, sink_ref[kv * 4],
                        jnp.where(grp == 1, sink_ref[kv * 4 + 1],
                                  jnp.where(grp == 2, sink_ref[kv * 4 + 2], sink_ref[kv * 4 + 3])))
        m = jnp.maximum(jnp.max(st, axis=0, keepdims=True), snk)
        e = jnp.exp(st - m)
        denom = jnp.exp(snk - m) + jnp.sum(e, axis=0, keepdims=True)
        a = kv % 2
        return _dot(vts[kv // 2][blk][a * HEAD_DIM:(a + 1) * HEAD_DIM, :], e.astype(BF16)) * (1.0 / denom)

    def hand_over(blk, outs):
        for kvp in range(2):
            full = jnp.concatenate(outs[2 * kvp:2 * kvp + 2], axis=0)
            for g in range(4):
                m_scr[blk * CHUNK:(blk + 1) * CHUNK, (kvp * 4 + g) * LANES:(kvp * 4 + g + 1) * LANES] = (
                    full[:, g * CHUNK:(g + 1) * CHUNK].T.astype(BF16))

    x1, h = _ffn_in(x_ref[...], m_scr[...], p_ref, wo_ref)

    kcats, vts = [], []
    for kvp in range(2):
        sl = slice(kvp * LANES, (kvp + 1) * LANES)
        k_chunks = [kp_ref[:, sl], kc_ref[0:CHUNK, sl], kc_ref[CHUNK:2 * CHUNK, sl], kn_ref[:, sl]]
        v_chunks = [vp_ref[:, sl], vc_ref[0:CHUNK, sl], vc_ref[CHUNK:2 * CHUNK, sl], vn_ref[:, sl],
                    vx_ref[0:CHUNK, sl], vx_ref[CHUNK:2 * CHUNK, sl]]
        v_t = [v.astype(F32).T.astype(BF16) for v in v_chunks]
        kcats.append([jnp.concatenate(k_chunks[b:b + 3] + [kx_ref[:, sl]], axis=0) for b in range(2)])
        vts.append([jnp.concatenate(v_t[b:b + 3] + v_t[4:], axis=1) for b in range(2)])

    for blk in range(2):
        sts = [scores(4 * blk + kv) for kv in range(4)]
        ffn_up(blk)
        hand_over(blk, [softmax_pv(4 * blk + kv, sts[kv]) for kv in range(4)])
    ffn_up(2)
    o_ref[...] = x1 + p_ref[3:4, :] * _dot(act_scr[...], w2_ref[...])


def _attn_ffn(y, sink, xc, prm, wo, wi_all, w2_all, layer, nb, nc, ctx_chunks):
    tm = ROW_TILE
    cpt = tm // CHUNK
    tpb = nc // cpt
    lat_tiles = (nc - ctx_chunks) // cpt
    tiles = nb * lat_tiles
    ctx_tiles = ctx_chunks // cpt
    assert ctx_tiles == 1 and lat_tiles >= 2
    d = xc.shape[1]
    d_ff = w2_all.shape[1]
    bias = _window_bias()
    n_qkv = y.shape[1]
    kv_w = n_qkv - d
    assert d % kv_w == 0
    kv_col = d // kv_w
    att = lambda s: jnp.minimum(s, tiles - 1)
    ffn = lambda s: jnp.maximum(s - 1, 0)
    row_tile = lambda t: (t // lat_tiles) * tpb + ctx_tiles + t % lat_tiles
    chunk0 = lambda t: (t // lat_tiles) * nc + ctx_chunks
    prev_c = lambda s: chunk0(att(s)) + jnp.maximum((att(s) % lat_tiles) * cpt - 1, 0)
    next_c = lambda s: chunk0(att(s)) + jnp.minimum((att(s) % lat_tiles) * cpt + cpt, lat_tiles * cpt - 1)
    edge_spec = lambda f: pl.BlockSpec((CHUNK, kv_w), lambda s: (f(s), kv_col))
    layer_spec = lambda a: pl.BlockSpec((None,) + a.shape[1:], lambda s: (layer, 0, 0),
                                        pipeline_mode=pl.Buffered(1))
    return pl.pallas_call(
        functools.partial(_attn_ffn_kernel, tiles=tiles, lat_tiles=lat_tiles, d_ff=d_ff),
        grid=(tiles + 1,),
        in_specs=[_smem_spec(),
                  pl.BlockSpec((tm, n_qkv), lambda s: (row_tile(att(s)), 0)),
                  edge_spec(prev_c), edge_spec(next_c),
                  pl.BlockSpec((tm, kv_w), lambda s: ((att(s) // lat_tiles) * tpb, kv_col)),
                  _const_spec(bias.shape),
                  pl.BlockSpec((tm, d), lambda s: (row_tile(ffn(s)), 0)),
                  pl.BlockSpec((None, 8, d), lambda s: ((ffn(s) // lat_tiles) * 2 + 1, 0, 0)),
                  _const_spec(wo.shape), layer_spec(wi_all), layer_spec(w2_all)],
        out_specs=pl.BlockSpec((tm, d), lambda s: (ffn(s), 0)),
        out_shape=jax.ShapeDtypeStruct((tiles * tm, d), F32),
        scratch_shapes=[pltpu.VMEM((tm, d), BF16), pltpu.VMEM((tm, d_ff), BF16)],
        compiler_params=_cparams("arbitrary"),
        name="window_gqa_ffn",
    )(sink, y, y, y, y, bias, xc, prm, wo, wi_all, w2_all)


def _pair_cols(w):
    rows, cols = w.shape
    return w.reshape(rows, cols // LANES, 2, 2, 32).transpose(0, 1, 3, 2, 4).reshape(rows, cols)


def _attn_q_cols(w):
    rows = w.shape[0]
    g_per = w.shape[1] // (H_KV * HEAD_DIM)
    return (w.reshape(rows, H_KV // 2, 2, g_per, 2, 32).transpose(0, 1, 3, 4, 2, 5)
            .reshape(rows, w.shape[1]))


def _attn_o_rows(w):
    cols = w.shape[1]
    g_per = w.shape[0] // (H_KV * HEAD_DIM)
    return (w.reshape(H_KV // 2, 2, g_per, HEAD_DIM, cols).transpose(0, 2, 1, 3, 4)
            .reshape(w.shape[0], cols))


def _rope_tables(seq, ctx_len):
    rows = seq // GRID_W
    row = np.repeat(np.arange(rows, dtype=np.float32), GRID_W)
    col = np.tile(np.arange(GRID_W, dtype=np.float32), rows)
    n = HEAD_DIM // 4
    inv = (np.float32(ROPE_BASE) ** (-np.arange(n, dtype=np.float32) / np.float32(n))).astype(np.float32)
    ang = np.concatenate([row[:, None] * inv, col[:, None] * inv], axis=-1).astype(np.float32)
    cos, sin = np.cos(ang), np.sin(ang)
    cos_t = np.concatenate([np.ones((ctx_len, LANES), np.float32), np.tile(cos, (1, 4))], axis=0)
    sin_t = np.concatenate([np.zeros((ctx_len, LANES), np.float32),
                            np.concatenate([-sin, -sin, sin, sin], axis=-1)], axis=0)
    return jnp.asarray(cos_t, F32), jnp.asarray(sin_t, F32)


def _mod_tables(mod, nb, norm_w):
    d = norm_w.shape[-1]
    lat = mod[:nb].reshape(nb, 6, d)
    ctx = jnp.broadcast_to(mod[nb].reshape(1, 6, d), (nb, 6, d))
    both = jnp.stack([ctx, lat], axis=1).reshape(nb * 2, 6, d)
    sh1, sc1, g1, sh2, sc2, g2 = [both[:, k] for k in range(6)]
    ab1 = jnp.stack([norm_w[0] * (1.0 + sc1), sh1], axis=1)
    zeros = jnp.zeros_like(g1)
    prm = jnp.stack([g1, norm_w[1] * (1.0 + sc2), sh2, g2, zeros, zeros, zeros, zeros], axis=1)
    return ab1, prm


def kernel(x, c, ctx, c_ctx, ada_w, ada_b, norm_w, ffn_w_in, ffn_w_out, ab_w_in, ab_w_out,
           ret_log_gamma, ret_norm_w, mlstm_conv_w, mlstm_conv_b, mlstm_gate_b, mlstm_norm_w,
           attn_w_in, attn_w_out, attn_q_norm_w, attn_k_norm_w, attn_sink):
    nb, seq, d = x.shape
    ctx_len = ctx.shape[1]
    depth = ada_w.shape[0]
    assert ctx_len == ROW_TILE and seq % ROW_TILE == 0 and d == 8 * LANES and nb < 8
    t_all = ctx_len + seq
    nc = t_all // CHUNK
    ctx_chunks = ctx_len // CHUNK
    tpb = t_all // ROW_TILE
    dr = d // 2

    rows = jnp.zeros((8, d), F32).at[:nb].set(c).at[nb].set(c_ctx)
    mod_all = _modulation(rows, ada_w, ada_b)
    cos_t, sin_t = _rope_tables(seq, ctx_len)
    wi_all = ffn_w_in.astype(BF16)
    w2_all = ffn_w_out.astype(BF16)
    xs = (ctx.reshape(nb * ctx_len, d), x.reshape(nb * seq, d))

    assert depth == 2 and ab_w_in.shape[0] == 1 and attn_w_in.shape[0] == 1
    ab_0, prm_0 = _mod_tables(mod_all[0], nb, norm_w[0])
    ab_1, prm_1 = _mod_tables(mod_all[1], nb, norm_w[1])

    w = ab_w_in[0]
    w_main = jnp.concatenate([_pair_cols(w[:, :dr]), _pair_cols(w[:, dr:2 * dr]), w[:, 2 * dr:8 * dr]],
                             axis=1).astype(BF16)
    wg = jnp.zeros((d, LANES), F32).at[:, :32].set(w[:, 8 * dr:]).astype(BF16)
    wgt = w[:, 8 * dr:].T.astype(BF16)
    gb = jnp.zeros((1, LANES), F32).at[0, :32].set(mlstm_gate_b[0].reshape(-1))
    gbt = mlstm_gate_b[0].reshape(32, 1)
    lg = ret_log_gamma[0].astype(F32)
    lgk = jnp.tile(jnp.repeat(lg.reshape(2, 4, 2), 32, axis=-1), (1, 1, 2)).reshape(8, LANES)
    cw = jnp.concatenate([mlstm_conv_w[0], mlstm_conv_b[0][None], jnp.zeros((4, 2 * dr), F32)], axis=0)
    nw = jnp.broadcast_to(jnp.concatenate([ret_norm_w[0], mlstm_norm_w[0]]).reshape(8, LANES, 1),
                          (8, LANES, LANES))
    wo_0 = ab_w_out[0].astype(BF16)

    w = attn_w_in[0]
    w_attn = jnp.concatenate([_attn_q_cols(w[:, :d]), _pair_cols(w[:, d:d + 2 * LANES]), w[:, d + 2 * LANES:]],
                             axis=1).astype(BF16)
    lane_w = lambda v: jnp.concatenate([v[:32], v[:32], v[32:], v[32:]])
    nwq = jnp.stack([lane_w(attn_q_norm_w[0]) * (HEAD_DIM ** -0.5), lane_w(attn_k_norm_w[0])]
                    + [jnp.zeros((LANES,), F32)] * 6)
    wo_1 = _attn_o_rows(attn_w_out[0]).astype(BF16)

    y, g, gt, *states = _inproj_sweep(*xs, ab_0, w_main, wg, wgt, gb, gbt, cos_t, sin_t, lgk, cw, nb, tpb)
    x_mid, y_attn = _mixer_ffn(y, g, gt, states, lg.reshape(-1), lgk, lgk.T, nw, xs, prm_0, wo_0, wi_all, w2_all, 0,
                               ab_1, w_attn, nwq, cos_t, sin_t, nb, tpb)
    out = _attn_ffn(y_attn, attn_sink[0].astype(F32), x_mid, prm_1, wo_1, wi_all, w2_all, 1, nb, nc, ctx_chunks)
    return out.reshape(nb, seq, d)
```

```python
import functools

import numpy as np
import jax
import jax.numpy as jnp
from jax import lax
from jax.experimental import pallas as pl
from jax.experimental.pallas import tpu as pltpu

F32 = jnp.float32
BF16 = jnp.bfloat16

HEAD_DIM = 64
CHUNK = 128
GRID_W = 64
ROPE_BASE = 10000.0
EPS = 1e-6
H_KV = 4
LANES = 128
ROW_TILE = 256
HALO = 16
NEG = -1e30
VMEM_LIMIT = 56 * 1024 * 1024


def _cparams(*sem):
    return pltpu.CompilerParams(dimension_semantics=sem, vmem_limit_bytes=VMEM_LIMIT)


def _const_spec(shape):
    nd = len(shape)
    return pl.BlockSpec(shape, lambda *_: (0,) * nd, pipeline_mode=pl.Buffered(1))


def _smem_spec():
    return pl.BlockSpec(memory_space=pltpu.SMEM)


def _lane(shape=(CHUNK, LANES)):
    return lax.broadcasted_iota(jnp.int32, shape, len(shape) - 1)


def _dot(a, b):
    return jnp.dot(a, b, preferred_element_type=F32)


def _dot_nt(a, b):
    return lax.dot_general(a, b, (((1,), (1,)), ((), ())), preferred_element_type=F32)


def _div_small(t, m, n):
    q = 0
    for b in range(1, n):
        q = q + jnp.where(t >= b * m, 1, 0)
    return q, t - q * m


def _split3(x):
    hi = x.astype(BF16)
    r = x - hi.astype(F32)
    mid = r.astype(BF16)
    lo = (r - mid.astype(F32)).astype(BF16)
    return hi, mid, lo


def _log_sigmoid(x):
    return jnp.minimum(x, 0.0) - jnp.log1p(jnp.exp(-jnp.abs(x)))


def _rope(x, cos, sin_signed):
    return x * cos + pltpu.roll(x, LANES // 2, 1) * sin_signed


def _mod_kernel(rows_ref, w_ref, b_ref, o_ref):
    a = rows_ref[...]
    a = a * jax.nn.sigmoid(a)
    a_hi = a.astype(BF16)
    a_lo = (a - a_hi.astype(F32)).astype(BF16)
    w = w_ref[...]
    w_hi = w.astype(BF16)
    w_lo = (w - w_hi.astype(F32)).astype(BF16)
    o_ref[...] = _dot(a_hi, w_hi) + _dot(a_hi, w_lo) + _dot(a_lo, w_hi) + b_ref[...]


def _modulation(rows, ada_w, ada_b):
    depth, d, n = ada_w.shape
    tn = n // 4
    return pl.pallas_call(
        _mod_kernel,
        grid=(depth, n // tn),
        in_specs=[pl.BlockSpec((8, d), lambda l, j: (0, 0)),
                  pl.BlockSpec((None, d, tn), lambda l, j: (l, 0, j)),
                  pl.BlockSpec((None, 1, tn), lambda l, j: (l, 0, j))],
        out_specs=pl.BlockSpec((None, 8, tn), lambda l, j: (l, 0, j)),
        out_shape=jax.ShapeDtypeStruct((depth, 8, n), F32),
        compiler_params=_cparams("arbitrary", "arbitrary"),
        name="adaln_modulation",
    )(rows, ada_w, ada_b.reshape(depth, 1, n))


def _norm_mod(x, ab_ref):
    ms = jnp.mean(x * x, axis=-1, keepdims=True)
    h = (x * lax.rsqrt(ms + EPS)) * ab_ref[0:1, :] + ab_ref[1:2, :]
    return h.astype(BF16)


def _ctx_or_latent_rows(ctx_ref, x_ref, tile, nb, tiles_per_batch):
    return jnp.where(_div_small(tile, tiles_per_batch, nb)[1] == 0, ctx_ref[...], x_ref[...])


def _split_row_specs(tm, d, nb, tpb, tile_of):
    lat = tpb - 1

    def latent_row(i):
        b, j = _div_small(tile_of(i), tpb, nb)
        return b * lat + jnp.maximum(j - 1, 0), 0

    return [pl.BlockSpec((tm, d), lambda i: (_div_small(tile_of(i), tpb, nb)[0], 0)),
            pl.BlockSpec((tm, d), latent_row)]


def _cumsum_cols(tri_bf, lf):
    hi, mid, lo = _split3(lf)
    return _dot(tri_bf, hi) + _dot(tri_bf, mid) + _dot(tri_bf, lo)


def _cumsum_rows(lf, tri_bf):
    hi, mid, lo = _split3(lf)
    return _dot(hi, tri_bf) + _dot(mid, tri_bf) + _dot(lo, tri_bf)


def _ret_state_update(s_ref, p, k2, vt, kdec, cd_lanes, bd):
    kf = (k2.astype(F32) * kdec).astype(BF16)
    s_ref[p] = s_ref[p] * cd_lanes + jnp.where(bd, _dot(vt, kf), 0.0)


def _mlstm_state_update(c_ref, n_ref, m_ref, k_pairs, vt_pairs, c_all, bend, col0, lo, bd):
    cmax = jnp.max(c_all, axis=0, keepdims=True)
    w_all = jnp.exp(c_all - cmax)
    m_old = m_ref[0:1, :]
    mrel = jnp.maximum(m_old, cmax)
    a_row = jnp.exp(m_old - mrel)
    bb_row = jnp.exp(cmax - mrel)
    m_ref[0:1, :] = bend + mrel
    lo_row = lo[0:1, :]
    for p in range(4):
        h0 = col0 + 2 * p
        kw = k_pairs[p] * jnp.where(lo, w_all[:, h0:h0 + 1], w_all[:, h0 + 1:h0 + 2])
        kvt = _dot(vt_pairs[p], kw.astype(BF16))
        nloc = jnp.sum(kw, axis=0, keepdims=True)
        a_l = jnp.where(lo_row, a_row[:, h0:h0 + 1], a_row[:, h0 + 1:h0 + 2])
        bb_l = jnp.where(lo_row, bb_row[:, h0:h0 + 1], bb_row[:, h0 + 1:h0 + 2])
        c_ref[p] = c_ref[p] * a_l + jnp.where(bd, kvt, 0.0) * bb_l
        n_new = (n_ref[p, 0:1, :] + n_ref[p, 1:2, :]) * a_l + nloc * bb_l
        n_ref[p, 0:1, :] = jnp.where(lo_row, n_new, 0.0)
        n_ref[p, 1:2, :] = jnp.where(lo_row, 0.0, n_new)


def _mlstm_dir_weights(st, qn_row, c_col, bt_row, m_prev, tri):
    dl = jnp.where(tri, c_col + bt_row, NEG)
    mx = jnp.max(dl, axis=0, keepdims=True)
    al = bt_row + m_prev
    m_t = jnp.maximum(al, mx)
    w = jnp.exp(dl - m_t)
    a_t = jnp.exp(al - m_t)
    sw = st * w
    den = jnp.sum(sw, axis=0, keepdims=True) + a_t * qn_row
    r = 1.0 / jnp.maximum(jnp.abs(den), jnp.exp(-m_t))
    return sw * r, a_t * r


def _heads_out(ht, nw_tab):
    rows = []
    for a in range(2):
        ha = ht[a * HEAD_DIM:(a + 1) * HEAD_DIM, :]
        ms = jnp.mean(ha * ha, axis=0, keepdims=True)
        rows.append(ha * lax.rsqrt(ms + EPS))
    return (jnp.concatenate(rows, axis=0) * nw_tab).T


def _inproj_sweep_kernel(ctx_ref, x_ref, ab_ref, w_ref, wg_ref, wgt_ref, gb_ref, gbt_ref, cos_ref, sin_ref,
                         lgk_ref, cw_ref,
                         y_ref, g_ref, gt_ref, sret_ref, cm_ref, nm_ref, mm_ref, qk_ref,
                         s_scr, c_scr, n_scr, m_scr, kdec_scr,
                         p_rk, p_rv, p_mv, p_mqk, p_g, next_row_scr, *, tiles, tpb):
    i = pl.program_id(0)

    def order(t):
        v = _div_small(t, tpb, tiles // tpb)[1]
        return jnp.where(v == 0, 0, tpb - v)

    jt_a = order(jnp.maximum(i - 1, 0))
    lane = _lane()
    sub = lax.broadcasted_iota(jnp.int32, (CHUNK, LANES), 0)
    lo = lane < HEAD_DIM
    lo_row = lo[0:1, :]
    bd_ret = (sub >= HEAD_DIM) == ((lane & 32) != 0)
    bd_m = (sub >= HEAD_DIM) == (lane >= HEAD_DIM)
    gw = 4 * LANES

    @pl.when(i == 0)
    def _():
        pos = sub.astype(F32)
        for p in range(4):
            kdec_scr[p] = jnp.exp(lgk_ref[4 + p:5 + p, :] * pos)
        for ref in (p_rk, p_rv, p_mv, p_mqk, p_g, next_row_scr):
            ref[...] = jnp.zeros_like(ref)

    @pl.when(jt_a == 0)
    def _():
        s_scr[...] = jnp.zeros_like(s_scr)
        c_scr[...] = jnp.zeros_like(c_scr)
        n_scr[...] = jnp.zeros_like(n_scr)
        m_scr[...] = jnp.zeros_like(m_scr)

    tile_i = jnp.minimum(i, tiles - 1)
    jt_i = order(tile_i)
    hb = _norm_mod(jnp.where(jt_i == 0, ctx_ref[...], x_ref[...]), ab_ref)
    cos, sin = cos_ref[...], sin_ref[...]

    def project(j):
        acc = _dot(hb, w_ref[:, j * gw:(j + 1) * gw])
        if j in (0, 1):
            if j == 0:
                acc = acc * (HEAD_DIM ** -0.5)
            acc = jnp.concatenate([_rope(acc[:, p * LANES:(p + 1) * LANES], cos, sin) for p in range(4)], axis=1)
        elif j == 3:
            acc = acc * jax.nn.sigmoid(acc)
        elif j == 7:
            acc = jax.nn.sigmoid(acc)
        return acc.astype(BF16)

    raw_q, raw_k = project(4), project(5)

    prev_on = jnp.where(jt_a <= 1, 0.0, 1.0).astype(F32)
    next_on = jnp.where((jt_a == 0) | (jt_a == tpb - 1), 0.0, 1.0).astype(F32)
    cur = p_mqk[...].astype(F32)
    n = cur.shape[0]
    row = lax.broadcasted_iota(jnp.int32, cur.shape, 0)
    prev_row = jnp.concatenate([raw_q[n - HALO:, :], raw_k[n - HALO:, :]], axis=1)[HALO - 1:HALO, :].astype(F32)
    xm = jnp.where(row == 0, prev_row * prev_on, pltpu.roll(cur, 1, 0))
    xp = jnp.where(row == n - 1, next_row_scr[0:1, :] * next_on, pltpu.roll(cur, n - 1, 0))
    conv = cw_ref[3:4, :] + cw_ref[0:1, :] * xm + cw_ref[1:2, :] * cur + cw_ref[2:3, :] * xp
    qk = conv * jax.nn.sigmoid(conv)
    qk_ref[:, 0:gw] = (qk[:, 0:gw] * (HEAD_DIM ** -0.5)).astype(BF16)
    qk_ref[:, gw:] = qk[:, gw:].astype(BF16)
    le_bf = (sub <= lane).astype(BF16)
    chunks = (1, 0)
    pre = {}
    for blk in chunks:
        rows = slice(blk * CHUNK, (blk + 1) * CHUNK)
        g = p_g[rows, :]
        bal = pltpu.roll(_cumsum_cols(le_bf, _log_sigmoid(g)), LANES - 8, 1)
        c_all = g - bal
        cmax = jnp.max(c_all, axis=0, keepdims=True)
        w_all = jnp.exp(c_all - cmax)
        vts, ks, nlocs = [], [], []
        for p in range(8):
            sl = slice((p % 4) * LANES, (p % 4 + 1) * LANES)
            if p < 4:
                v2 = p_rv[rows, sl]
                ks.append((p_rk[rows, sl].astype(F32) * kdec_scr[p]).astype(BF16))
            else:
                v2 = p_mv[rows, sl]
                h0 = 16 + 2 * (p - 4)
                kw = qk[rows, gw + (p - 4) * LANES:gw + (p - 3) * LANES] * jnp.where(
                    lo, w_all[:, h0:h0 + 1], w_all[:, h0 + 1:h0 + 2])
                ks.append(kw.astype(BF16))
                nlocs.append(jnp.sum(kw, axis=0, keepdims=True))
            vts.append(v2.astype(F32).T.astype(BF16))
        pre[blk] = (vts, ks, nlocs, cmax, bal[0:1, :])

    cur_rk = project(1)
    y_ref[:, 1 * gw:2 * gw] = cur_rk
    y_ref[:, 0:gw] = project(0)
    kvs = {blk: [_dot(pre[blk][0][p], pre[blk][1][p]) for p in range(8)] for blk in chunks}
    cur_rv = project(2)
    y_ref[:, 2 * gw:3 * gw] = cur_rv
    y_ref[:, 3 * gw:4 * gw] = project(3)

    for blk in chunks:
        _, _, nlocs, cmax, bend = pre[blk]
        sret_ref[blk] = s_scr[...].astype(BF16)
        cm_ref[blk] = c_scr[...].astype(BF16)
        nm_ref[blk] = n_scr[...].astype(BF16)
        mm_ref[blk] = m_scr[...]
        m_old = m_scr[0:1, :]
        mrel = jnp.maximum(m_old, cmax)
        a_row = jnp.exp(m_old - mrel)
        bb_row = jnp.exp(cmax - mrel)
        m_scr[0:1, :] = bend + mrel
        for p in range(4):
            cd = jnp.exp(lgk_ref[4 + p:5 + p, :] * float(CHUNK))
            s_scr[p] = s_scr[p] * cd + jnp.where(bd_ret, kvs[blk][p], 0.0)
            h0 = 16 + 2 * p
            a_l = jnp.where(lo_row, a_row[:, h0:h0 + 1], a_row[:, h0 + 1:h0 + 2])
            bb_l = jnp.where(lo_row, bb_row[:, h0:h0 + 1], bb_row[:, h0 + 1:h0 + 2])
            c_scr[p] = c_scr[p] * a_l + jnp.where(bd_m, kvs[blk][4 + p], 0.0) * bb_l
            n_new = (n_scr[p, 0:1, :] + n_scr[p, 1:2, :]) * a_l + nlocs[p] * bb_l
            n_scr[p, 0:1, :] = jnp.where(lo_row, n_new, 0.0)
            n_scr[p, 1:2, :] = jnp.where(lo_row, 0.0, n_new)

    cur_mv = project(6)
    y_ref[:, 4 * gw:5 * gw] = cur_mv
    y_ref[:, 5 * gw:6 * gw] = project(7)
    gates = _dot(hb, wg_ref[...]) + gb_ref[...]
    g_ref[...] = gates
    gt_ref[...] = _dot_nt(wgt_ref[...], hb) + gbt_ref[...]
    next_row_scr[...] = p_mqk[0:HALO, :].astype(F32)
    p_mqk[:, 0:gw] = raw_q
    p_mqk[:, gw:] = raw_k
    p_rk[...] = cur_rk
    p_rv[...] = cur_rv
    p_mv[...] = cur_mv
    p_g[...] = gates


def _inproj_sweep(ctx2, x2, ab, w, wg, wgt, gb, gbt, cos, sin, lgk, cw, nb, tpb):
    d = x2.shape[1]
    tm = ROW_TILE
    cpt = tm // CHUNK
    gw = 4 * LANES
    tiles = nb * tpb
    r = tiles * tm
    lat = tpb - 1
    def visit(t):
        b, v = _div_small(t, tpb, nb)
        return b, jnp.where(v == 0, 0, tpb - v)

    cur = lambda i: visit(jnp.minimum(i, tiles - 1))
    flat = lambda bj: bj[0] * tpb + bj[1]
    tile_i = lambda i: flat(cur(i))
    tile_a = lambda i: flat(visit(jnp.maximum(i - 1, 0)))
    in_batch = lambda i: cur(i)[1]
    sel = lambda i: (cur(i)[0] * 2 + jnp.minimum(cur(i)[1], 1), 0, 0)
    state = lambda *dims: pl.BlockSpec((cpt,) + dims, lambda i: (tile_a(i),) + (0,) * len(dims))
    nchunks = tiles * cpt
    return pl.pallas_call(
        functools.partial(_inproj_sweep_kernel, tiles=tiles, tpb=tpb),
        grid=(tiles + 1,),
        in_specs=[pl.BlockSpec((tm, d), lambda i: (cur(i)[0], 0)),
                  pl.BlockSpec((tm, d), lambda i: (cur(i)[0] * lat + jnp.maximum(cur(i)[1] - 1, 0), 0)),
                  pl.BlockSpec((None, 2, d), sel),
                  _const_spec(w.shape), _const_spec(wg.shape), _const_spec(wgt.shape),
                  _const_spec(gb.shape), _const_spec(gbt.shape),
                  pl.BlockSpec((tm, LANES), lambda i: (in_batch(i), 0)),
                  pl.BlockSpec((tm, LANES), lambda i: (in_batch(i), 0)),
                  _const_spec(lgk.shape), _const_spec(cw.shape)],
        out_specs=[pl.BlockSpec((tm, 6 * gw), lambda i: (tile_i(i), 0)),
                   pl.BlockSpec((tm, LANES), lambda i: (tile_i(i), 0)),
                   pl.BlockSpec((32, tm), lambda i: (0, tile_i(i))),
                   state(4, CHUNK, LANES), state(4, CHUNK, LANES), state(4, HALO, LANES), state(8, LANES),
                   pl.BlockSpec((tm, 2 * gw), lambda i: (tile_a(i), 0))],
        out_shape=[jax.ShapeDtypeStruct((r, 6 * gw), BF16),
                   jax.ShapeDtypeStruct((r, LANES), F32),
                   jax.ShapeDtypeStruct((32, r), F32),
                   jax.ShapeDtypeStruct((nchunks, 4, CHUNK, LANES), BF16),
                   jax.ShapeDtypeStruct((nchunks, 4, CHUNK, LANES), BF16),
                   jax.ShapeDtypeStruct((nchunks, 4, HALO, LANES), BF16),
                   jax.ShapeDtypeStruct((nchunks, 8, LANES), F32),
                   jax.ShapeDtypeStruct((r, 2 * gw), BF16)],
        scratch_shapes=[pltpu.VMEM((4, CHUNK, LANES), F32), pltpu.VMEM((4, CHUNK, LANES), F32),
                        pltpu.VMEM((4, HALO, LANES), F32), pltpu.VMEM((8, LANES), F32),
                        pltpu.VMEM((4, CHUNK, LANES), F32),
                        pltpu.VMEM((tm, gw), BF16), pltpu.VMEM((tm, gw), BF16), pltpu.VMEM((tm, gw), BF16),
                        pltpu.VMEM((tm, 2 * gw), BF16), pltpu.VMEM((tm, LANES), F32),
                        pltpu.VMEM((HALO, 2 * gw), F32)],
        compiler_params=_cparams("arbitrary"),
        name="inproj_bwd_sweep",
    )(ctx2, x2, ab, w, wg, wgt, gb, gbt, cos, sin, lgk, cw)


class _Bag:
    def __init__(self, **kw):
        self.__dict__.update(kw)


def _mixer_chunk_stages(blk, r):
    rows = slice(blk * CHUNK, (blk + 1) * CHUNK)
    lane = _lane()
    sub = lax.broadcasted_iota(jnp.int32, (CHUNK, LANES), 0)
    lo = lane < HEAD_DIM
    sub_lo = sub < HEAD_DIM
    mask_ret = [(lane & 32) == 0, (lane & 32) != 0]
    mask_nat = [lo, lane >= HEAD_DIM]
    bd_ret = (sub >= HEAD_DIM) == ((lane & 32) != 0)
    bd_m = (sub >= HEAD_DIM) == (lane >= HEAD_DIM)
    le = sub <= lane
    ge = sub >= lane

    g = r.g_ref[rows, :]
    gt = r.gt_ref[:, rows]
    lf_col = _log_sigmoid(g)
    lf_row = _log_sigmoid(gt)
    le_bf = le.astype(BF16)
    ge_bf = ge.astype(BF16)
    pre_col = _cumsum_cols(ge_bf, lf_col)
    bal_f = pltpu.roll(pre_col, LANES - 8, 1)
    bal_b = pltpu.roll(pre_col[CHUNK - 1:CHUNK, :] - pre_col + lf_col, LANES - 8, 1)
    cf_all = g - bal_f
    cb_all = g - bal_b
    bf_row = _cumsum_rows(lf_row, le_bf)
    bb_row = bf_row[:, CHUNK - 1:CHUNK] - bf_row + lf_row
    qb, kb, vts, kf32, qt, vbd = [], [], [], [], [], []
    for p in range(8):
        sl = slice((p % 4) * LANES, (p % 4 + 1) * LANES)
        if p < 4:
            q2, k2, v2 = r.rq_ref[rows, sl], r.rk_ref[rows, sl], r.rv_ref[rows, sl]
            qf = q2.astype(F32)
            kf = None
        else:
            q2 = r.qk_ref[rows, sl]
            k2 = r.qk_ref[rows, 4 * LANES + (p - 4) * LANES:4 * LANES + (p - 3) * LANES]
            v2 = r.mv_ref[rows, sl]
            qf, kf = q2.astype(F32), k2.astype(F32)
        vt = v2.astype(F32).T.astype(BF16)
        qb.append(q2)
        kb.append(k2)
        vts.append(vt)
        kf32.append(kf)
        qt.append(qf.T)
        vbd.append([jnp.where(sub_lo, vt, jnp.zeros_like(vt)), jnp.where(sub_lo, jnp.zeros_like(vt), vt)])
    yield

    st2 = []
    for p in range(8):
        masks = mask_ret if p < 4 else mask_nat
        zero = jnp.zeros_like(qb[p])
        qstack = jnp.concatenate([jnp.where(masks[0], qb[p], zero), jnp.where(masks[1], qb[p], zero)], axis=0)
        st2.append(_dot_nt(kb[p], qstack))
    yield

    qn = [_dot_nt(jnp.concatenate([r.n_scr[p].astype(BF16), r.nmb_ref[blk, p]], axis=0), qb[4 + p])
          for p in range(4)]
    qn_f = [x[0:HALO] for x in qn]
    qn_b = [x[HALO:2 * HALO] for x in qn]
    lhs, rhs = [], []
    for p in range(8):
        if p < 4:
            pts = [(st2[p][:, a * LANES:(a + 1) * LANES] * r.dm_scr[2 * p + a]).astype(BF16) for a in range(2)]
            x_f, x_b = r.dec_scr[0, p], r.dec_scr[1, p]
            old = [r.s_scr[p].astype(BF16), r.sretb_ref[blk, p]]
        else:
            pts, cf, cb = [], [], []
            for a in range(2):
                h = 2 * (p - 4) + a
                st = st2[p][:, a * LANES:(a + 1) * LANES]
                pf, coef_f = _mlstm_dir_weights(st, qn_f[p - 4][a:a + 1, :], cf_all[:, h:h + 1],
                                                bf_row[8 + h:9 + h, :], r.m_state[0:1, h:h + 1], le)
                pb, coef_b = _mlstm_dir_weights(st, qn_b[p - 4][a:a + 1, :], cb_all[:, 16 + h:17 + h],
                                                bb_row[24 + h:25 + h, :], r.mmb_ref[blk, 0:1, 16 + h:17 + h], ge)
                pts.append((pf + pb).astype(BF16))
                cf.append(coef_f)
                cb.append(coef_b)
            x_f = jnp.where(sub_lo, cf[0], cf[1])
            x_b = jnp.where(sub_lo, cb[0], cb[1])
            old = [r.c_scr[p - 4].astype(BF16), r.cmb_ref[blk, p - 4]]
        lhs.append(jnp.concatenate(vbd[p] + old, axis=1))
        rhs.append(jnp.concatenate(pts + [(qt[p] * x_f).astype(BF16), (qt[p] * x_b).astype(BF16)], axis=0))
    yield

    ht = [_dot(lhs[p], rhs[p]) for p in range(8)]
    yield

    for p in range(8):
        sl = slice((p % 4) * LANES, (p % 4 + 1) * LANES)
        gate_ref = r.rg_ref if p < 4 else r.mo_ref
        y = _heads_out(ht[p], r.nw_ref[p])
        new = (y * gate_ref[rows, sl].astype(F32)).astype(BF16)
        r.mix_scr[rows, p * LANES:(p + 1) * LANES] = jnp.where(r.live, new, r.mix_scr[rows, p * LANES:(p + 1) * LANES])
    yield

    for p in range(4):
        cd = jnp.exp(r.lgk_ref[p:p + 1, :] * float(CHUNK))
        _ret_state_update(r.s_scr, p, kb[p], vts[p], r.dec_scr[2, p], cd, bd_ret)
    _mlstm_state_update(r.c_scr, r.n_scr, r.m_state, kf32[4:], vts[4:], cf_all, bal_f[CHUNK - 1:CHUNK, :],
                        0, lo, bd_m)
    yield


def _attn_inproj_stages(x, ab_ref, w_ref, nw_ref, cos, sin, y_ref):
    hb = _norm_mod(x, ab_ref)
    r2 = lax.broadcasted_iota(jnp.int32, (2 * LANES, 2 * LANES), 0)
    c2 = lax.broadcasted_iota(jnp.int32, (2 * LANES, 2 * LANES), 1)
    same_head = (((r2 ^ c2) & (LANES | 32)) == 0).astype(BF16)
    acc_q = _dot(hb, w_ref[:, 0:8 * LANES])
    acc_k = _dot(hb, w_ref[:, 8 * LANES:10 * LANES])
    y_ref[:, 10 * LANES:12 * LANES] = _dot(hb, w_ref[:, 10 * LANES:12 * LANES]).astype(BF16)
    yield
    for j in range(5):
        acc = acc_q[:, j * 2 * LANES:(j + 1) * 2 * LANES] if j < 4 else acc_k
        sq = acc * acc
        hi = sq.astype(BF16)
        lo = (sq - hi.astype(F32)).astype(BF16)
        ms = (_dot(hi, same_head) + _dot(lo, same_head)) * (1.0 / HEAD_DIM)
        nrm = acc * lax.rsqrt(ms + EPS)
        nw = nw_ref[0:1, :] if j < 4 else nw_ref[1:2, :]
        for v in range(2):
            ls = slice(v * LANES, (v + 1) * LANES)
            y_ref[:, (2 * j + v) * LANES:(2 * j + v + 1) * LANES] = _rope(nrm[:, ls] * nw, cos, sin).astype(BF16)
        if j in (1, 4):
            yield


def _ffn_splits(d_ff, pieces):
    blocks = d_ff // (2 * LANES)
    assert blocks * 2 * LANES == d_ff and blocks >= pieces
    cuts = [((i * blocks) // pieces) * 2 * LANES for i in range(pieces)]
    return cuts + [d_ff]


def _mixer_ffn_kernel(lg_ref, y_ref, qk_ref,
                      g_ref, gt_ref, sretb_ref, cmb_ref, nmb_ref, mmb_ref,
                      lgk_ref, lgkt_ref, nw_ref,
                      ctx_ref, x_ref, p_ref, wo_ref, wi_ref, w2_ref,
                      ab2_ref, wa_ref, nwa_ref, cos_ref, sin_ref,
                      o_ref, y2_ref,
                      s_scr, c_scr, n_scr, m_state, dm_scr, dec_scr, mix_scr, act_scr, x2_scr,
                      *, tiles, tpb, d_ff):
    s = pl.program_id(0)
    nb = tiles // tpb
    jt = _div_small(jnp.minimum(s, tiles - 1), tpb, nb)[1]
    gw = 4 * LANES
    rq_ref, rk_ref, rv_ref, rg_ref, mv_ref, mo_ref = [y_ref.at[:, j * gw:(j + 1) * gw] for j in range(6)]
    lane = _lane()
    sub = lax.broadcasted_iota(jnp.int32, (CHUNK, LANES), 0)

    @pl.when(s == 0)
    def _():
        mix_scr[...] = jnp.zeros_like(mix_scr)
        x2_scr[...] = jnp.zeros_like(x2_scr)
        le = sub <= lane
        ge = sub >= lane
        spos = sub.astype(F32)
        tpos = lane.astype(F32)
        diff = (lane - sub).astype(F32)
        for h in range(8):
            dm_scr[h] = (jnp.where(le, jnp.exp(lg_ref[h] * diff), 0.0)
                         + jnp.where(ge, jnp.exp(lg_ref[8 + h] * (-diff)), 0.0))
        for p in range(4):
            dec_scr[0, p] = jnp.exp(lgkt_ref[:, p:p + 1] * (tpos + 1.0))
            dec_scr[1, p] = jnp.exp(lgkt_ref[:, 4 + p:5 + p] * (float(CHUNK) - tpos))
            dec_scr[2, p] = jnp.exp(lgk_ref[p:p + 1, :] * (float(CHUNK) - 1.0 - spos))

    @pl.when(jt == 0)
    def _():
        s_scr[...] = jnp.zeros_like(s_scr)
        c_scr[...] = jnp.zeros_like(c_scr)
        n_scr[...] = jnp.zeros_like(n_scr)
        m_state[...] = jnp.zeros_like(m_state)

    r = _Bag(rq_ref=rq_ref, rk_ref=rk_ref, rv_ref=rv_ref, rg_ref=rg_ref, qk_ref=qk_ref, mv_ref=mv_ref, mo_ref=mo_ref,
             g_ref=g_ref, gt_ref=gt_ref, sretb_ref=sretb_ref, cmb_ref=cmb_ref, nmb_ref=nmb_ref, mmb_ref=mmb_ref,
             lgk_ref=lgk_ref, nw_ref=nw_ref, s_scr=s_scr, c_scr=c_scr, n_scr=n_scr, m_state=m_state,
             dm_scr=dm_scr, dec_scr=dec_scr, mix_scr=mix_scr, live=s < tiles)
    cuts = _ffn_splits(d_ff, 4)
    ffn_piece = lambda i: _ffn_cols(h, wi_ref, act_scr, cuts[i], cuts[i + 1], d_ff)

    x = _ctx_or_latent_rows(ctx_ref, x_ref, jnp.clip(s - 1, 0, tiles - 1), nb, tpb)
    x1, h = _ffn_in(x, mix_scr[...], p_ref, wo_ref)

    nxt = _attn_inproj_stages(x2_scr[...], ab2_ref, wa_ref, nwa_ref, cos_ref[...], sin_ref[...], y2_ref)
    chunk_a, chunk_b = _mixer_chunk_stages(0, r), _mixer_chunk_stages(1, r)
    next(nxt)
    next(chunk_a), next(chunk_b)
    next(chunk_a), next(chunk_b)
    ffn_piece(0)
    next(chunk_a), next(chunk_a)
    ffn_piece(1)
    next(chunk_a), next(chunk_a)
    next(nxt)
    ffn_piece(2)
    ffn_piece(3)
    next(chunk_b), next(chunk_b)
    next(nxt)
    x2 = x1 + p_ref[3:4, :] * _dot(act_scr[...], w2_ref[...])
    o_ref[...] = x2
    next(chunk_b), next(chunk_b)
    x2_scr[...] = x2


def _mixer_ffn(y, g, gt, states, lg_smem, lgk, lgkt, nw, xs, prm, wo, wi_all, w2_all, layer,
               ab_next, w_next, nw_next, cos, sin, nb, tpb):
    gw = 4 * LANES
    tm = ROW_TILE
    cpt = tm // CHUNK
    sretb, cmb, nmb, mmb, qk_act = states
    tiles = nb * tpb
    d = xs[-1].shape[1]
    d_ff = w2_all.shape[1]
    n_next = w_next.shape[1]
    mix = lambda s: jnp.minimum(s, tiles - 1)
    ffn = lambda s: jnp.clip(s - 1, 0, tiles - 1)
    nxt = lambda s: jnp.maximum(s - 2, 0)
    state_spec = lambda a: pl.BlockSpec((cpt,) + a.shape[1:], lambda s: (mix(s),) + (0,) * (a.ndim - 1))
    def sel(t):
        b, j = _div_small(t, tpb, nb)
        return b * 2 + jnp.minimum(j, 1), 0, 0

    layer_spec = lambda a: pl.BlockSpec((None,) + a.shape[1:], lambda s: (layer, 0, 0),
                                        pipeline_mode=pl.Buffered(1))
    return pl.pallas_call(
        functools.partial(_mixer_ffn_kernel, tiles=tiles, tpb=tpb, d_ff=d_ff),
        grid=(tiles + 2,),
        in_specs=[_smem_spec(),
                  pl.BlockSpec((tm, 6 * gw), lambda s: (mix(s), 0)),
                  pl.BlockSpec((tm, 2 * gw), lambda s: (mix(s), 0)),
                  pl.BlockSpec((tm, LANES), lambda s: (mix(s), 0)),
                  pl.BlockSpec((32, tm), lambda s: (0, mix(s))),
                  state_spec(sretb), state_spec(cmb), state_spec(nmb), state_spec(mmb),
                  _const_spec(lgk.shape), _const_spec(lgkt.shape), _const_spec(nw.shape)]
                 + _split_row_specs(tm, d, nb, tpb, ffn)
                 + [pl.BlockSpec((None, 8, d), lambda s: sel(ffn(s))),
                    _const_spec(wo.shape), layer_spec(wi_all), layer_spec(w2_all),
                    pl.BlockSpec((None, 2, d), lambda s: sel(nxt(s))),
                    _const_spec(w_next.shape), _const_spec(nw_next.shape),
                    pl.BlockSpec((tm, LANES), lambda s: (_div_small(nxt(s), tpb, nb)[1], 0)),
                    pl.BlockSpec((tm, LANES), lambda s: (_div_small(nxt(s), tpb, nb)[1], 0))],
        out_specs=[pl.BlockSpec((tm, d), lambda s: (ffn(s), 0)),
                   pl.BlockSpec((tm, n_next), lambda s: (nxt(s), 0))],
        out_shape=[jax.ShapeDtypeStruct((tiles * tm, d), F32),
                   jax.ShapeDtypeStruct((tiles * tm, n_next), BF16)],
        scratch_shapes=[pltpu.VMEM((4, CHUNK, LANES), F32), pltpu.VMEM((4, CHUNK, LANES), F32),
                        pltpu.VMEM((4, HALO, LANES), F32), pltpu.VMEM((8, LANES), F32),
                        pltpu.VMEM((8, CHUNK, LANES), F32), pltpu.VMEM((3, 4, CHUNK, LANES), F32),
                        pltpu.VMEM((tm, 2 * gw), BF16), pltpu.VMEM((tm, d_ff), BF16),
                        pltpu.VMEM((tm, d), F32)],
        compiler_params=_cparams("arbitrary"),
        name="ret_mlstm_mixer_ffn",
    )(lg_smem, y, qk_act, g, gt, sretb, cmb, nmb, mmb, lgk, lgkt, nw,
      *xs, prm, wo, wi_all, w2_all, ab_next, w_next, nw_next, cos, sin)


def _ffn_in(x, m, p_ref, wo_ref):
    x1 = x + p_ref[0:1, :] * _dot(m, wo_ref[...])
    ms = jnp.mean(x1 * x1, axis=-1, keepdims=True)
    return x1, ((x1 * lax.rsqrt(ms + EPS)) * p_ref[1:2, :] + p_ref[2:3, :]).astype(BF16)


def _ffn_cols(h, wi_ref, act_scr, lo, hi, d_ff):
    gate = _dot(h, wi_ref[:, lo:hi])
    up = _dot(h, wi_ref[:, d_ff + lo:d_ff + hi])
    act_scr[:, lo:hi] = (gate * jax.nn.sigmoid(gate) * up).astype(BF16)


def _window_bias():
    kk = np.arange(CHUNK)[:, None]
    t = np.arange(CHUNK)[None, :]
    tabs = []
    for has_prev, has_next in ((False, True), (True, True), (True, False)):
        prev_ok = (kk >= t) & has_prev
        next_ok = (kk <= t) & has_next
        tabs.append(np.where(np.concatenate([prev_ok, next_ok], axis=0), 0.0, NEG))
    return jnp.asarray(np.stack(tabs), F32)


def _attn_ffn_kernel(sink_ref, qkv_ref, kvp_ref, kvn_ref, kvx_ref, bias_ref,
                     x_ref, p_ref, wo_ref, wi_ref, w2_ref, o_ref, m_scr, act_scr, *, tiles, lat_tiles, d_ff):
    s = pl.program_id(0)
    q_ref = qkv_ref
    kw = 2 * LANES
    kc_ref, vc_ref = qkv_ref.at[:, 8 * LANES:8 * LANES + kw], qkv_ref.at[:, 8 * LANES + kw:8 * LANES + 2 * kw]
    kp_ref, vp_ref = kvp_ref.at[:, 0:kw], kvp_ref.at[:, kw:2 * kw]
    kn_ref, vn_ref = kvn_ref.at[:, 0:kw], kvn_ref.at[:, kw:2 * kw]
    kx_ref, vx_ref = kvx_ref.at[:, 0:kw], kvx_ref.at[:, kw:2 * kw]

    @pl.when(s == 0)
    def _():
        m_scr[...] = jnp.zeros_like(m_scr)

    j = _div_small(jnp.minimum(s, tiles - 1), lat_tiles, tiles // lat_tiles)[1]
    grp = lax.broadcasted_iota(jnp.int32, (1, 4 * CHUNK), 1) // CHUNK
    lane = _lane()
    mask_q = [(lane & 32) == 0, (lane & 32) != 0]
    tile4 = lambda b: jnp.concatenate([b] * 4, axis=1)
    biases = [tile4(jnp.where(j == 0, bias_ref[0], bias_ref[1])),
              tile4(jnp.where(j == lat_tiles - 1, bias_ref[2], bias_ref[1]))]
    cuts = _ffn_splits(d_ff, 3)

    def ffn_up(i):
        _ffn_cols(h, wi_ref, act_scr, cuts[i], cuts[i + 1], d_ff)

    def scores(u):
        blk, kvp, a = u // 4, (u % 4) // 2, u % 2
        rows = slice(blk * CHUNK, (blk + 1) * CHUNK)
        qs = jnp.concatenate(
            [jnp.where(mask_q[a], q_ref[rows, (kvp * 4 + g) * LANES:(kvp * 4 + g + 1) * LANES],
                       jnp.zeros((CHUNK, LANES), BF16)) for g in range(4)], axis=0)
        return _dot_nt(kcats[kvp][blk], qs)

    def softmax_pv(u, st):
        blk, kv = u // 4, u % 4
        bias = biases[blk]
        st = jnp.concatenate([st[0:CHUNK] + bias[0:CHUNK], st[CHUNK:2 * CHUNK],
                              st[2 * CHUNK:3 * CHUNK] + bias[CHUNK:2 * CHUNK], st[3 * CHUNK:]], axis=0)
        snk = jnp.where(grp == 0, sink_ref[kv * 4],
                        jnp.where(grp == 1, sink_ref[kv * 4 + 1],
                                  jnp.where(grp == 2, sink_ref[kv * 4 + 2], sink_ref[kv * 4 + 3])))
        m = jnp.maximum(jnp.max(st, axis=0, keepdims=True), snk)
        e = jnp.exp(st - m)
        denom = jnp.exp(snk - m) + jnp.sum(e, axis=0, keepdims=True)
        a = kv % 2
        return _dot(vts[kv // 2][blk][a * HEAD_DIM:(a + 1) * HEAD_DIM, :], e.astype(BF16)) * (1.0 / denom)

    def hand_over(blk, outs):
        for kvp in range(2):
            full = jnp.concatenate(outs[2 * kvp:2 * kvp + 2], axis=0)
            for g in range(4):
                m_scr[blk * CHUNK:(blk + 1) * CHUNK, (kvp * 4 + g) * LANES:(kvp * 4 + g + 1) * LANES] = (
                    full[:, g * CHUNK:(g + 1) * CHUNK].T.astype(BF16))

    x1, h = _ffn_in(x_ref[...], m_scr[...], p_ref, wo_ref)

    kcats, vts = [], []
    for kvp in range(2):
        sl = slice(kvp * LANES, (kvp + 1) * LANES)
        k_chunks = [kp_ref[:, sl], kc_ref[0:CHUNK, sl], kc_ref[CHUNK:2 * CHUNK, sl], kn_ref[:, sl]]
        v_chunks = [vp_ref[:, sl], vc_ref[0:CHUNK, sl], vc_ref[CHUNK:2 * CHUNK, sl], vn_ref[:, sl],
                    vx_ref[0:CHUNK, sl], vx_ref[CHUNK:2 * CHUNK, sl]]
        v_t = [v.astype(F32).T.astype(BF16) for v in v_chunks]
        kcats.append([jnp.concatenate(k_chunks[b:b + 3] + [kx_ref[:, sl]], axis=0) for b in range(2)])
        vts.append([jnp.concatenate(v_t[b:b + 3] + v_t[4:], axis=1) for b in range(2)])

    for blk in range(2):
        sts = [scores(4 * blk + kv) for kv in range(4)]
        ffn_up(blk)
        hand_over(blk, [softmax_pv(4 * blk + kv, sts[kv]) for kv in range(4)])
    ffn_up(2)
    o_ref[...] = x1 + p_ref[3:4, :] * _dot(act_scr[...], w2_ref[...])


def _attn_ffn(y, sink, xc, prm, wo, wi_all, w2_all, layer, nb, nc, ctx_chunks):
    tm = ROW_TILE
    cpt = tm // CHUNK
    tpb = nc // cpt
    lat_tiles = (nc - ctx_chunks) // cpt
    tiles = nb * lat_tiles
    ctx_tiles = ctx_chunks // cpt
    assert ctx_tiles == 1 and lat_tiles >= 2
    d = xc.shape[1]
    d_ff = w2_all.shape[1]
    bias = _window_bias()
    n_qkv = y.shape[1]
    kv_w = n_qkv - d
    assert d % kv_w == 0
    kv_col = d // kv_w
    att = lambda s: jnp.minimum(s, tiles - 1)
    ffn = lambda s: jnp.maximum(s - 1, 0)
    split = lambda t: _div_small(t, lat_tiles, nb)
    row_tile = lambda t: split(t)[0] * tpb + ctx_tiles + split(t)[1]
    chunk0 = lambda t: split(t)[0] * nc + ctx_chunks
    prev_c = lambda s: chunk0(att(s)) + jnp.maximum(split(att(s))[1] * cpt - 1, 0)
    next_c = lambda s: chunk0(att(s)) + jnp.minimum(split(att(s))[1] * cpt + cpt, lat_tiles * cpt - 1)
    edge_spec = lambda f: pl.BlockSpec((CHUNK, kv_w), lambda s: (f(s), kv_col))
    layer_spec = lambda a: pl.BlockSpec((None,) + a.shape[1:], lambda s: (layer, 0, 0),
                                        pipeline_mode=pl.Buffered(1))
    return pl.pallas_call(
        functools.partial(_attn_ffn_kernel, tiles=tiles, lat_tiles=lat_tiles, d_ff=d_ff),
        grid=(tiles + 1,),
        in_specs=[_smem_spec(),
                  pl.BlockSpec((tm, n_qkv), lambda s: (row_tile(att(s)), 0)),
                  edge_spec(prev_c), edge_spec(next_c),
                  pl.BlockSpec((tm, kv_w), lambda s: (split(att(s))[0] * tpb, kv_col)),
                  _const_spec(bias.shape),
                  pl.BlockSpec((tm, d), lambda s: (row_tile(ffn(s)), 0)),
                  pl.BlockSpec((None, 8, d), lambda s: (split(ffn(s))[0] * 2 + 1, 0, 0)),
                  _const_spec(wo.shape), layer_spec(wi_all), layer_spec(w2_all)],
        out_specs=pl.BlockSpec((tm, d), lambda s: (ffn(s), 0)),
        out_shape=jax.ShapeDtypeStruct((tiles * tm, d), F32),
        scratch_shapes=[pltpu.VMEM((tm, d), BF16), pltpu.VMEM((tm, d_ff), BF16)],
        compiler_params=_cparams("arbitrary"),
        name="window_gqa_ffn",
    )(sink, y, y, y, y, bias, xc, prm, wo, wi_all, w2_all)


def _pair_cols(w):
    rows, cols = w.shape
    return w.reshape(rows, cols // LANES, 2, 2, 32).transpose(0, 1, 3, 2, 4).reshape(rows, cols)


def _attn_q_cols(w):
    rows = w.shape[0]
    g_per = w.shape[1] // (H_KV * HEAD_DIM)
    return (w.reshape(rows, H_KV // 2, 2, g_per, 2, 32).transpose(0, 1, 3, 4, 2, 5)
            .reshape(rows, w.shape[1]))


def _attn_o_rows(w):
    cols = w.shape[1]
    g_per = w.shape[0] // (H_KV * HEAD_DIM)
    return (w.reshape(H_KV // 2, 2, g_per, HEAD_DIM, cols).transpose(0, 2, 1, 3, 4)
            .reshape(w.shape[0], cols))


def _rope_tables(seq, ctx_len):
    rows = seq // GRID_W
    row = np.repeat(np.arange(rows, dtype=np.float32), GRID_W)
    col = np.tile(np.arange(GRID_W, dtype=np.float32), rows)
    n = HEAD_DIM // 4
    inv = (np.float32(ROPE_BASE) ** (-np.arange(n, dtype=np.float32) / np.float32(n))).astype(np.float32)
    ang = np.concatenate([row[:, None] * inv, col[:, None] * inv], axis=-1).astype(np.float32)
    cos, sin = np.cos(ang), np.sin(ang)
    cos_t = np.concatenate([np.ones((ctx_len, LANES), np.float32), np.tile(cos, (1, 4))], axis=0)
    sin_t = np.concatenate([np.zeros((ctx_len, LANES), np.float32),
                            np.concatenate([-sin, -sin, sin, sin], axis=-1)], axis=0)
    return jnp.asarray(cos_t, F32), jnp.asarray(sin_t, F32)


def _mod_tables(mod, nb, norm_w):
    d = norm_w.shape[-1]
    lat = mod[:nb].reshape(nb, 6, d)
    ctx = jnp.broadcast_to(mod[nb].reshape(1, 6, d), (nb, 6, d))
    both = jnp.stack([ctx, lat], axis=1).reshape(nb * 2, 6, d)
    sh1, sc1, g1, sh2, sc2, g2 = [both[:, k] for k in range(6)]
    ab1 = jnp.stack([norm_w[0] * (1.0 + sc1), sh1], axis=1)
    zeros = jnp.zeros_like(g1)
    prm = jnp.stack([g1, norm_w[1] * (1.0 + sc2), sh2, g2, zeros, zeros, zeros, zeros], axis=1)
    return ab1, prm


def kernel(x, c, ctx, c_ctx, ada_w, ada_b, norm_w, ffn_w_in, ffn_w_out, ab_w_in, ab_w_out,
           ret_log_gamma, ret_norm_w, mlstm_conv_w, mlstm_conv_b, mlstm_gate_b, mlstm_norm_w,
           attn_w_in, attn_w_out, attn_q_norm_w, attn_k_norm_w, attn_sink):
    nb, seq, d = x.shape
    ctx_len = ctx.shape[1]
    depth = ada_w.shape[0]
    assert ctx_len == ROW_TILE and seq % ROW_TILE == 0 and d == 8 * LANES and nb < 8
    t_all = ctx_len + seq
    nc = t_all // CHUNK
    ctx_chunks = ctx_len // CHUNK
    tpb = t_all // ROW_TILE
    dr = d // 2

    rows = jnp.zeros((8, d), F32).at[:nb].set(c).at[nb].set(c_ctx)
    mod_all = _modulation(rows, ada_w, ada_b)
    cos_t, sin_t = _rope_tables(seq, ctx_len)
    wi_all = ffn_w_in.astype(BF16)
    w2_all = ffn_w_out.astype(BF16)
    xs = (ctx.reshape(nb * ctx_len, d), x.reshape(nb * seq, d))

    assert depth == 2 and ab_w_in.shape[0] == 1 and attn_w_in.shape[0] == 1
    ab_0, prm_0 = _mod_tables(mod_all[0], nb, norm_w[0])
    ab_1, prm_1 = _mod_tables(mod_all[1], nb, norm_w[1])

    w = ab_w_in[0]
    w_main = jnp.concatenate([_pair_cols(w[:, :dr]), _pair_cols(w[:, dr:2 * dr]), w[:, 2 * dr:8 * dr]],
                             axis=1).astype(BF16)
    wg = jnp.zeros((d, LANES), F32).at[:, :32].set(w[:, 8 * dr:]).astype(BF16)
    wgt = w[:, 8 * dr:].T.astype(BF16)
    gb = jnp.zeros((1, LANES), F32).at[0, :32].set(mlstm_gate_b[0].reshape(-1))
    gbt = mlstm_gate_b[0].reshape(32, 1)
    lg = ret_log_gamma[0].astype(F32)
    lgk = jnp.tile(jnp.repeat(lg.reshape(2, 4, 2), 32, axis=-1), (1, 1, 2)).reshape(8, LANES)
    cw = jnp.concatenate([mlstm_conv_w[0], mlstm_conv_b[0][None], jnp.zeros((4, 2 * dr), F32)], axis=0)
    nw = jnp.broadcast_to(jnp.concatenate([ret_norm_w[0], mlstm_norm_w[0]]).reshape(8, LANES, 1),
                          (8, LANES, LANES))
    wo_0 = ab_w_out[0].astype(BF16)

    w = attn_w_in[0]
    w_attn = jnp.concatenate([_attn_q_cols(w[:, :d]), _pair_cols(w[:, d:d + 2 * LANES]), w[:, d + 2 * LANES:]],
                             axis=1).astype(BF16)
    lane_w = lambda v: jnp.concatenate([v[:32], v[:32], v[32:], v[32:]])
    nwq = jnp.stack([lane_w(attn_q_norm_w[0]) * (HEAD_DIM ** -0.5), lane_w(attn_k_norm_w[0])]
                    + [jnp.zeros((LANES,), F32)] * 6)
    wo_1 = _attn_o_rows(attn_w_out[0]).astype(BF16)

    y, g, gt, *states = _inproj_sweep(*xs, ab_0, w_main, wg, wgt, gb, gbt, cos_t, sin_t, lgk, cw, nb, tpb)
    x_mid, y_attn = _mixer_ffn(y, g, gt, states, lg.reshape(-1), lgk, lgk.T, nw, xs, prm_0, wo_0, wi_all, w2_all, 0,
                               ab_1, w_attn, nwq, cos_t, sin_t, nb, tpb)
    out = _attn_ffn(y_attn, attn_sink[0].astype(F32), x_mid, prm_1, wo_1, wi_all, w2_all, 1, nb, nc, ctx_chunks)
    return out.reshape(nb, seq, d)
```

```python
import functools

import numpy as np
import jax
import jax.numpy as jnp
from jax import lax
from jax.experimental import pallas as pl
from jax.experimental.pallas import tpu as pltpu

F32 = jnp.float32
BF16 = jnp.bfloat16

HEAD_DIM = 64
CHUNK = 128
GRID_W = 64
ROPE_BASE = 10000.0
EPS = 1e-6
H_KV = 4
LANES = 128
ROW_TILE = 256
HALO = 16
LOG2E = 1.4426950408889634
NEG = -1e30
VMEM_LIMIT = 56 * 1024 * 1024


def _cparams(*sem):
    return pltpu.CompilerParams(dimension_semantics=sem, vmem_limit_bytes=VMEM_LIMIT)


def _const_spec(shape):
    nd = len(shape)
    return pl.BlockSpec(shape, lambda *_: (0,) * nd, pipeline_mode=pl.Buffered(1))


def _smem_spec():
    return pl.BlockSpec(memory_space=pltpu.SMEM)


def _lane(shape=(CHUNK, LANES)):
    return lax.broadcasted_iota(jnp.int32, shape, len(shape) - 1)


def _dot(a, b):
    return jnp.dot(a, b, preferred_element_type=F32)


def _dot_nt(a, b):
    return lax.dot_general(a, b, (((1,), (1,)), ((), ())), preferred_element_type=F32)


def _div_small(t, m, n):
    q = 0
    for b in range(1, n):
        q = q + jnp.where(t >= b * m, 1, 0)
    return q, t - q * m


def _split3(x):
    hi = x.astype(BF16)
    r = x - hi.astype(F32)
    mid = r.astype(BF16)
    lo = (r - mid.astype(F32)).astype(BF16)
    return hi, mid, lo


def _log_sigmoid(x):
    return jnp.minimum(x, 0.0) - jnp.log1p(jnp.exp(-jnp.abs(x)))


def _rope(x, cos, sin_signed):
    return x * cos + pltpu.roll(x, LANES // 2, 1) * sin_signed


def _mod_kernel(rows_ref, w_ref, b_ref, o_ref):
    a = rows_ref[...]
    a = a * jax.nn.sigmoid(a)
    a_hi = a.astype(BF16)
    a_lo = (a - a_hi.astype(F32)).astype(BF16)
    w = w_ref[...]
    w_hi = w.astype(BF16)
    w_lo = (w - w_hi.astype(F32)).astype(BF16)
    o_ref[...] = _dot(a_hi, w_hi) + _dot(a_hi, w_lo) + _dot(a_lo, w_hi) + b_ref[...]


def _modulation(rows, ada_w, ada_b):
    depth, d, n = ada_w.shape
    tn = n // 4
    return pl.pallas_call(
        _mod_kernel,
        grid=(depth, n // tn),
        in_specs=[pl.BlockSpec((8, d), lambda l, j: (0, 0)),
                  pl.BlockSpec((None, d, tn), lambda l, j: (l, 0, j)),
                  pl.BlockSpec((None, 1, tn), lambda l, j: (l, 0, j))],
        out_specs=pl.BlockSpec((None, 8, tn), lambda l, j: (l, 0, j)),
        out_shape=jax.ShapeDtypeStruct((depth, 8, n), F32),
        compiler_params=_cparams("arbitrary", "arbitrary"),
        name="adaln_modulation",
    )(rows, ada_w, ada_b.reshape(depth, 1, n))


def _norm_mod(x, ab_ref):
    ms = jnp.mean(x * x, axis=-1, keepdims=True)
    h = (x * lax.rsqrt(ms + EPS)) * ab_ref[0:1, :] + ab_ref[1:2, :]
    return h.astype(BF16)


def _ctx_or_latent_rows(ctx_ref, x_ref, tile, nb, tiles_per_batch):
    return jnp.where(_div_small(tile, tiles_per_batch, nb)[1] == 0, ctx_ref[...], x_ref[...])


def _split_row_specs(tm, d, nb, tpb, tile_of):
    lat = tpb - 1

    def latent_row(i):
        b, j = _div_small(tile_of(i), tpb, nb)
        return b * lat + jnp.maximum(j - 1, 0), 0

    return [pl.BlockSpec((tm, d), lambda i: (_div_small(tile_of(i), tpb, nb)[0], 0)),
            pl.BlockSpec((tm, d), latent_row)]


def _cumsum_cols(tri_bf, lf):
    hi, mid, lo = _split3(lf)
    return _dot(tri_bf, hi) + _dot(tri_bf, mid) + _dot(tri_bf, lo)


def _cumsum_rows(lf, tri_bf):
    hi, mid, lo = _split3(lf)
    return _dot(hi, tri_bf) + _dot(mid, tri_bf) + _dot(lo, tri_bf)


def _ret_state_update(s_ref, p, k2, vt, kdec, cd_lanes, bd):
    kf = (k2.astype(F32) * kdec).astype(BF16)
    s_ref[p] = s_ref[p] * cd_lanes + jnp.where(bd, _dot(vt, kf), 0.0)


def _mlstm_state_update(c_ref, n_ref, m_ref, k_pairs, vt_pairs, c_all, bend, col0, lo, bd):
    cmax = jnp.max(c_all, axis=0, keepdims=True)
    w_all = jnp.exp(c_all - cmax)
    m_old = m_ref[0:1, :]
    mrel = jnp.maximum(m_old, cmax)
    a_row = jnp.exp(m_old - mrel)
    bb_row = jnp.exp(cmax - mrel)
    m_ref[0:1, :] = bend + mrel
    lo_row = lo[0:1, :]
    for p in range(4):
        h0 = col0 + 2 * p
        kw = k_pairs[p] * jnp.where(lo, w_all[:, h0:h0 + 1], w_all[:, h0 + 1:h0 + 2])
        kvt = _dot(vt_pairs[p], kw.astype(BF16))
        nloc = jnp.sum(kw, axis=0, keepdims=True)
        a_l = jnp.where(lo_row, a_row[:, h0:h0 + 1], a_row[:, h0 + 1:h0 + 2])
        bb_l = jnp.where(lo_row, bb_row[:, h0:h0 + 1], bb_row[:, h0 + 1:h0 + 2])
        c_ref[p] = c_ref[p] * a_l + jnp.where(bd, kvt, 0.0) * bb_l
        n_new = (n_ref[p, 0:1, :] + n_ref[p, 1:2, :]) * a_l + nloc * bb_l
        n_ref[p, 0:1, :] = jnp.where(lo_row, n_new, 0.0)
        n_ref[p, 1:2, :] = jnp.where(lo_row, 0.0, n_new)


def _mlstm_dir_weights(st, qn_row, c_col, bt_row, m_prev, tri):
    dl = jnp.where(tri, c_col + bt_row, NEG)
    mx = jnp.max(dl, axis=0, keepdims=True)
    al = bt_row + m_prev
    m_t = jnp.maximum(al, mx)
    w = jnp.exp2(dl - m_t)
    a_t = jnp.exp2(al - m_t)
    sw = st * w
    den = jnp.sum(sw, axis=0, keepdims=True) + a_t * qn_row
    r = 1.0 / jnp.maximum(jnp.abs(den), jnp.exp2(-m_t))
    return sw * r, a_t * r


def _heads_out(ht, nw_tab):
    rows = []
    for a in range(2):
        ha = ht[a * HEAD_DIM:(a + 1) * HEAD_DIM, :]
        ms = jnp.mean(ha * ha, axis=0, keepdims=True)
        rows.append(ha * lax.rsqrt(ms + EPS))
    return (jnp.concatenate(rows, axis=0) * nw_tab).T


def _inproj_sweep_kernel(ctx_ref, x_ref, ab_ref, w_ref, wg_ref, wgt_ref, gb_ref, gbt_ref, cos_ref, sin_ref,
                         lgk_ref, cw_ref,
                         y_ref, g_ref, gt_ref, sret_ref, cm_ref, nm_ref, mm_ref, qk_ref,
                         s_scr, c_scr, n_scr, m_scr, kdec_scr,
                         p_rk, p_rv, p_mv, p_mqk, p_g, next_row_scr, *, tiles, tpb):
    i = pl.program_id(0)

    def order(t):
        v = _div_small(t, tpb, tiles // tpb)[1]
        return jnp.where(v == 0, 0, tpb - v)

    jt_a = order(jnp.maximum(i - 1, 0))
    lane = _lane()
    sub = lax.broadcasted_iota(jnp.int32, (CHUNK, LANES), 0)
    lo = lane < HEAD_DIM
    lo_row = lo[0:1, :]
    bd_ret = (sub >= HEAD_DIM) == ((lane & 32) != 0)
    bd_m = (sub >= HEAD_DIM) == (lane >= HEAD_DIM)
    gw = 4 * LANES

    @pl.when(i == 0)
    def _():
        pos = sub.astype(F32)
        for p in range(4):
            kdec_scr[p] = jnp.exp(lgk_ref[4 + p:5 + p, :] * pos)
        for ref in (p_rk, p_rv, p_mv, p_mqk, p_g, next_row_scr):
            ref[...] = jnp.zeros_like(ref)

    @pl.when(jt_a == 0)
    def _():
        s_scr[...] = jnp.zeros_like(s_scr)
        c_scr[...] = jnp.zeros_like(c_scr)
        n_scr[...] = jnp.zeros_like(n_scr)
        m_scr[...] = jnp.zeros_like(m_scr)

    tile_i = jnp.minimum(i, tiles - 1)
    jt_i = order(tile_i)
    hb = _norm_mod(jnp.where(jt_i == 0, ctx_ref[...], x_ref[...]), ab_ref)
    cos, sin = cos_ref[...], sin_ref[...]

    def project(j):
        acc = _dot(hb, w_ref[:, j * gw:(j + 1) * gw])
        if j in (0, 1):
            if j == 0:
                acc = acc * (HEAD_DIM ** -0.5)
            acc = jnp.concatenate([_rope(acc[:, p * LANES:(p + 1) * LANES], cos, sin) for p in range(4)], axis=1)
        elif j == 3:
            acc = acc * jax.nn.sigmoid(acc)
        elif j == 7:
            acc = jax.nn.sigmoid(acc)
        return acc.astype(BF16)

    raw_q, raw_k = project(4), project(5)

    prev_on = jnp.where(jt_a <= 1, 0.0, 1.0).astype(F32)
    next_on = jnp.where((jt_a == 0) | (jt_a == tpb - 1), 0.0, 1.0).astype(F32)
    cur = p_mqk[...].astype(F32)
    n = cur.shape[0]
    row = lax.broadcasted_iota(jnp.int32, cur.shape, 0)
    prev_row = jnp.concatenate([raw_q[n - HALO:, :], raw_k[n - HALO:, :]], axis=1)[HALO - 1:HALO, :].astype(F32)
    xm = jnp.where(row == 0, prev_row * prev_on, pltpu.roll(cur, 1, 0))
    xp = jnp.where(row == n - 1, next_row_scr[0:1, :] * next_on, pltpu.roll(cur, n - 1, 0))
    conv = cw_ref[3:4, :] + cw_ref[0:1, :] * xm + cw_ref[1:2, :] * cur + cw_ref[2:3, :] * xp
    qk = conv * jax.nn.sigmoid(conv)
    qk_ref[:, 0:gw] = (qk[:, 0:gw] * (HEAD_DIM ** -0.5)).astype(BF16)
    qk_ref[:, gw:] = qk[:, gw:].astype(BF16)
    le_bf = (sub <= lane).astype(BF16)
    chunks = (1, 0)
    pre = {}
    for blk in chunks:
        rows = slice(blk * CHUNK, (blk + 1) * CHUNK)
        g = p_g[rows, :]
        bal = pltpu.roll(_cumsum_cols(le_bf, _log_sigmoid(g)), LANES - 8, 1)
        c_all = g - bal
        cmax = jnp.max(c_all, axis=0, keepdims=True)
        w_all = jnp.exp(c_all - cmax)
        vts, ks, nlocs = [], [], []
        for p in range(8):
            sl = slice((p % 4) * LANES, (p % 4 + 1) * LANES)
            if p < 4:
                v2 = p_rv[rows, sl]
                ks.append((p_rk[rows, sl].astype(F32) * kdec_scr[p]).astype(BF16))
            else:
                v2 = p_mv[rows, sl]
                h0 = 16 + 2 * (p - 4)
                kw = qk[rows, gw + (p - 4) * LANES:gw + (p - 3) * LANES] * jnp.where(
                    lo, w_all[:, h0:h0 + 1], w_all[:, h0 + 1:h0 + 2])
                ks.append(kw.astype(BF16))
                nlocs.append(jnp.sum(kw, axis=0, keepdims=True))
            vts.append(v2.astype(F32).T.astype(BF16))
        pre[blk] = (vts, ks, nlocs, cmax, bal[0:1, :])

    cur_rk = project(1)
    y_ref[:, 1 * gw:2 * gw] = cur_rk
    y_ref[:, 0:gw] = project(0)
    kvs = {blk: [_dot(pre[blk][0][p], pre[blk][1][p]) for p in range(8)] for blk in chunks}
    cur_rv = project(2)
    y_ref[:, 2 * gw:3 * gw] = cur_rv
    y_ref[:, 3 * gw:4 * gw] = project(3)

    for blk in chunks:
        _, _, nlocs, cmax, bend = pre[blk]
        sret_ref[blk] = s_scr[...].astype(BF16)
        cm_ref[blk] = c_scr[...].astype(BF16)
        nm_ref[blk] = n_scr[...].astype(BF16)
        mm_ref[blk] = m_scr[...]
        m_old = m_scr[0:1, :]
        mrel = jnp.maximum(m_old, cmax)
        a_row = jnp.exp(m_old - mrel)
        bb_row = jnp.exp(cmax - mrel)
        m_scr[0:1, :] = bend + mrel
        for p in range(4):
            cd = jnp.exp(lgk_ref[4 + p:5 + p, :] * float(CHUNK))
            s_scr[p] = s_scr[p] * cd + jnp.where(bd_ret, kvs[blk][p], 0.0)
            h0 = 16 + 2 * p
            a_l = jnp.where(lo_row, a_row[:, h0:h0 + 1], a_row[:, h0 + 1:h0 + 2])
            bb_l = jnp.where(lo_row, bb_row[:, h0:h0 + 1], bb_row[:, h0 + 1:h0 + 2])
            c_scr[p] = c_scr[p] * a_l + jnp.where(bd_m, kvs[blk][4 + p], 0.0) * bb_l
            n_new = (n_scr[p, 0:1, :] + n_scr[p, 1:2, :]) * a_l + nlocs[p] * bb_l
            n_scr[p, 0:1, :] = jnp.where(lo_row, n_new, 0.0)
            n_scr[p, 1:2, :] = jnp.where(lo_row, 0.0, n_new)

    cur_mv = project(6)
    y_ref[:, 4 * gw:5 * gw] = cur_mv
    y_ref[:, 5 * gw:6 * gw] = project(7)
    gates = _dot(hb, wg_ref[...]) + gb_ref[...]
    g_ref[...] = gates
    gt_ref[...] = _dot_nt(wgt_ref[...], hb) + gbt_ref[...]
    next_row_scr[...] = p_mqk[0:HALO, :].astype(F32)
    p_mqk[:, 0:gw] = raw_q
    p_mqk[:, gw:] = raw_k
    p_rk[...] = cur_rk
    p_rv[...] = cur_rv
    p_mv[...] = cur_mv
    p_g[...] = gates


def _inproj_sweep(ctx2, x2, ab, w, wg, wgt, gb, gbt, cos, sin, lgk, cw, nb, tpb):
    d = x2.shape[1]
    tm = ROW_TILE
    cpt = tm // CHUNK
    gw = 4 * LANES
    tiles = nb * tpb
    r = tiles * tm
    lat = tpb - 1
    def visit(t):
        b, v = _div_small(t, tpb, nb)
        return b, jnp.where(v == 0, 0, tpb - v)

    cur = lambda i: visit(jnp.minimum(i, tiles - 1))
    flat = lambda bj: bj[0] * tpb + bj[1]
    tile_i = lambda i: flat(cur(i))
    tile_a = lambda i: flat(visit(jnp.maximum(i - 1, 0)))
    in_batch = lambda i: cur(i)[1]
    sel = lambda i: (cur(i)[0] * 2 + jnp.minimum(cur(i)[1], 1), 0, 0)
    state = lambda *dims: pl.BlockSpec((cpt,) + dims, lambda i: (tile_a(i),) + (0,) * len(dims))
    nchunks = tiles * cpt
    return pl.pallas_call(
        functools.partial(_inproj_sweep_kernel, tiles=tiles, tpb=tpb),
        grid=(tiles + 1,),
        in_specs=[pl.BlockSpec((tm, d), lambda i: (cur(i)[0], 0)),
                  pl.BlockSpec((tm, d), lambda i: (cur(i)[0] * lat + jnp.maximum(cur(i)[1] - 1, 0), 0)),
                  pl.BlockSpec((None, 2, d), sel),
                  _const_spec(w.shape), _const_spec(wg.shape), _const_spec(wgt.shape),
                  _const_spec(gb.shape), _const_spec(gbt.shape),
                  pl.BlockSpec((tm, LANES), lambda i: (in_batch(i), 0)),
                  pl.BlockSpec((tm, LANES), lambda i: (in_batch(i), 0)),
                  _const_spec(lgk.shape), _const_spec(cw.shape)],
        out_specs=[pl.BlockSpec((tm, 6 * gw), lambda i: (tile_i(i), 0)),
                   pl.BlockSpec((tm, LANES), lambda i: (tile_i(i), 0)),
                   pl.BlockSpec((32, tm), lambda i: (0, tile_i(i))),
                   state(4, CHUNK, LANES), state(4, CHUNK, LANES), state(4, HALO, LANES), state(8, LANES),
                   pl.BlockSpec((tm, 2 * gw), lambda i: (tile_a(i), 0))],
        out_shape=[jax.ShapeDtypeStruct((r, 6 * gw), BF16),
                   jax.ShapeDtypeStruct((r, LANES), F32),
                   jax.ShapeDtypeStruct((32, r), F32),
                   jax.ShapeDtypeStruct((nchunks, 4, CHUNK, LANES), BF16),
                   jax.ShapeDtypeStruct((nchunks, 4, CHUNK, LANES), BF16),
                   jax.ShapeDtypeStruct((nchunks, 4, HALO, LANES), BF16),
                   jax.ShapeDtypeStruct((nchunks, 8, LANES), F32),
                   jax.ShapeDtypeStruct((r, 2 * gw), BF16)],
        scratch_shapes=[pltpu.VMEM((4, CHUNK, LANES), F32), pltpu.VMEM((4, CHUNK, LANES), F32),
                        pltpu.VMEM((4, HALO, LANES), F32), pltpu.VMEM((8, LANES), F32),
                        pltpu.VMEM((4, CHUNK, LANES), F32),
                        pltpu.VMEM((tm, gw), BF16), pltpu.VMEM((tm, gw), BF16), pltpu.VMEM((tm, gw), BF16),
                        pltpu.VMEM((tm, 2 * gw), BF16), pltpu.VMEM((tm, LANES), F32),
                        pltpu.VMEM((HALO, 2 * gw), F32)],
        compiler_params=_cparams("arbitrary"),
        name="inproj_bwd_sweep",
    )(ctx2, x2, ab, w, wg, wgt, gb, gbt, cos, sin, lgk, cw)


class _Bag:
    def __init__(self, **kw):
        self.__dict__.update(kw)


def _mixer_chunk_stages(blk, r):
    rows = slice(blk * CHUNK, (blk + 1) * CHUNK)
    lane = _lane()
    sub = lax.broadcasted_iota(jnp.int32, (CHUNK, LANES), 0)
    lo = lane < HEAD_DIM
    sub_lo = sub < HEAD_DIM
    mask_ret = [(lane & 32) == 0, (lane & 32) != 0]
    mask_nat = [lo, lane >= HEAD_DIM]
    bd_ret = (sub >= HEAD_DIM) == ((lane & 32) != 0)
    bd_m = (sub >= HEAD_DIM) == (lane >= HEAD_DIM)
    le = sub <= lane
    ge = sub >= lane

    g = r.g_ref[rows, :]
    gt = r.gt_ref[:, rows]
    lf_col = _log_sigmoid(g)
    lf_row = _log_sigmoid(gt)
    le_bf = le.astype(BF16)
    ge_bf = ge.astype(BF16)
    pre_col = _cumsum_cols(ge_bf, lf_col)
    bal_f = pltpu.roll(pre_col, LANES - 8, 1)
    bal_b = pltpu.roll(pre_col[CHUNK - 1:CHUNK, :] - pre_col + lf_col, LANES - 8, 1)
    cf_all = g - bal_f
    cb_all = g - bal_b
    bf_row = _cumsum_rows(lf_row, le_bf)
    bb_row = bf_row[:, CHUNK - 1:CHUNK] - bf_row + lf_row
    qb, kb, vts, kf32, qt, vbd = [], [], [], [], [], []
    for p in range(8):
        sl = slice((p % 4) * LANES, (p % 4 + 1) * LANES)
        if p < 4:
            q2, k2, v2 = r.rq_ref[rows, sl], r.rk_ref[rows, sl], r.rv_ref[rows, sl]
            qf = q2.astype(F32)
            kf = None
        else:
            q2 = r.qk_ref[rows, sl]
            k2 = r.qk_ref[rows, 4 * LANES + (p - 4) * LANES:4 * LANES + (p - 3) * LANES]
            v2 = r.mv_ref[rows, sl]
            qf, kf = q2.astype(F32), k2.astype(F32)
        vt = v2.astype(F32).T.astype(BF16)
        qb.append(q2)
        kb.append(k2)
        vts.append(vt)
        kf32.append(kf)
        qt.append(qf.T)
        vbd.append([jnp.where(sub_lo, vt, jnp.zeros_like(vt)), jnp.where(sub_lo, jnp.zeros_like(vt), vt)])
    yield

    st2 = []
    for p in range(8):
        masks = mask_ret if p < 4 else mask_nat
        zero = jnp.zeros_like(qb[p])
        qstack = jnp.concatenate([jnp.where(masks[0], qb[p], zero), jnp.where(masks[1], qb[p], zero)], axis=0)
        st2.append(_dot_nt(kb[p], qstack))
    yield

    qn = [_dot_nt(jnp.concatenate([r.n_scr[p].astype(BF16), r.nmb_ref[blk, p]], axis=0), qb[4 + p])
          for p in range(4)]
    qn_f = [x[0:HALO] for x in qn]
    qn_b = [x[HALO:2 * HALO] for x in qn]
    cf2, cb2, bf2, bb2 = cf_all * LOG2E, cb_all * LOG2E, bf_row * LOG2E, bb_row * LOG2E
    m_f2, m_b2 = r.m_state[0:1, :] * LOG2E, r.mmb_ref[blk, 0:1, :] * LOG2E
    lhs, rhs = [], []
    for p in range(8):
        if p < 4:
            pts = [(st2[p][:, a * LANES:(a + 1) * LANES] * r.dm_scr[2 * p + a]).astype(BF16) for a in range(2)]
            x_f, x_b = r.dec_scr[0, p], r.dec_scr[1, p]
            old = [r.s_scr[p].astype(BF16), r.sretb_ref[blk, p]]
        else:
            pts, cf, cb = [], [], []
            for a in range(2):
                h = 2 * (p - 4) + a
                st = st2[p][:, a * LANES:(a + 1) * LANES]
                pf, coef_f = _mlstm_dir_weights(st, qn_f[p - 4][a:a + 1, :], cf2[:, h:h + 1],
                                                bf2[8 + h:9 + h, :], m_f2[0:1, h:h + 1], le)
                pb, coef_b = _mlstm_dir_weights(st, qn_b[p - 4][a:a + 1, :], cb2[:, 16 + h:17 + h],
                                                bb2[24 + h:25 + h, :], m_b2[0:1, 16 + h:17 + h], ge)
                pts.append((pf + pb).astype(BF16))
                cf.append(coef_f)
                cb.append(coef_b)
            x_f = jnp.where(sub_lo, cf[0], cf[1])
            x_b = jnp.where(sub_lo, cb[0], cb[1])
            old = [r.c_scr[p - 4].astype(BF16), r.cmb_ref[blk, p - 4]]
        lhs.append(jnp.concatenate(vbd[p] + old, axis=1))
        rhs.append(jnp.concatenate(pts + [(qt[p] * x_f).astype(BF16), (qt[p] * x_b).astype(BF16)], axis=0))
    yield

    ht = [_dot(lhs[p], rhs[p]) for p in range(8)]
    yield

    for p in range(8):
        sl = slice((p % 4) * LANES, (p % 4 + 1) * LANES)
        gate_ref = r.rg_ref if p < 4 else r.mo_ref
        y = _heads_out(ht[p], r.nw_ref[p])
        new = (y * gate_ref[rows, sl].astype(F32)).astype(BF16)
        r.mix_scr[rows, p * LANES:(p + 1) * LANES] = jnp.where(r.live, new, r.mix_scr[rows, p * LANES:(p + 1) * LANES])
    yield

    for p in range(4):
        cd = jnp.exp(r.lgk_ref[p:p + 1, :] * float(CHUNK))
        _ret_state_update(r.s_scr, p, kb[p], vts[p], r.dec_scr[2, p], cd, bd_ret)
    _mlstm_state_update(r.c_scr, r.n_scr, r.m_state, kf32[4:], vts[4:], cf_all, bal_f[CHUNK - 1:CHUNK, :],
                        0, lo, bd_m)
    yield


def _attn_inproj_stages(x, ab_ref, w_ref, nw_ref, cos, sin, y_ref):
    hb = _norm_mod(x, ab_ref)
    r2 = lax.broadcasted_iota(jnp.int32, (2 * LANES, 2 * LANES), 0)
    c2 = lax.broadcasted_iota(jnp.int32, (2 * LANES, 2 * LANES), 1)
    same_head = (((r2 ^ c2) & (LANES | 32)) == 0).astype(BF16)
    acc_q = _dot(hb, w_ref[:, 0:8 * LANES])
    acc_k = _dot(hb, w_ref[:, 8 * LANES:10 * LANES])
    y_ref[:, 10 * LANES:12 * LANES] = _dot(hb, w_ref[:, 10 * LANES:12 * LANES]).astype(BF16)
    yield
    for j in range(5):
        acc = acc_q[:, j * 2 * LANES:(j + 1) * 2 * LANES] if j < 4 else acc_k
        sq = acc * acc
        hi = sq.astype(BF16)
        lo = (sq - hi.astype(F32)).astype(BF16)
        ms = (_dot(hi, same_head) + _dot(lo, same_head)) * (1.0 / HEAD_DIM)
        nrm = acc * lax.rsqrt(ms + EPS)
        nw = nw_ref[0:1, :] if j < 4 else nw_ref[1:2, :]
        for v in range(2):
            ls = slice(v * LANES, (v + 1) * LANES)
            y_ref[:, (2 * j + v) * LANES:(2 * j + v + 1) * LANES] = _rope(nrm[:, ls] * nw, cos, sin).astype(BF16)
        if j in (1, 4):
            yield


def _ffn_splits(d_ff, pieces):
    blocks = d_ff // (2 * LANES)
    assert blocks * 2 * LANES == d_ff and blocks >= pieces
    cuts = [((i * blocks) // pieces) * 2 * LANES for i in range(pieces)]
    return cuts + [d_ff]


def _mixer_ffn_kernel(lg_ref, y_ref, qk_ref,
                      g_ref, gt_ref, sretb_ref, cmb_ref, nmb_ref, mmb_ref,
                      lgk_ref, lgkt_ref, nw_ref,
                      ctx_ref, x_ref, p_ref, wo_ref, wi_ref, w2_ref,
                      ab2_ref, wa_ref, nwa_ref, cos_ref, sin_ref,
                      o_ref, y2_ref,
                      s_scr, c_scr, n_scr, m_state, dm_scr, dec_scr, mix_scr, act_scr, x2_scr,
                      *, tiles, tpb, d_ff):
    s = pl.program_id(0)
    nb = tiles // tpb
    jt = _div_small(jnp.minimum(s, tiles - 1), tpb, nb)[1]
    gw = 4 * LANES
    rq_ref, rk_ref, rv_ref, rg_ref, mv_ref, mo_ref = [y_ref.at[:, j * gw:(j + 1) * gw] for j in range(6)]
    lane = _lane()
    sub = lax.broadcasted_iota(jnp.int32, (CHUNK, LANES), 0)

    @pl.when(s == 0)
    def _():
        mix_scr[...] = jnp.zeros_like(mix_scr)
        x2_scr[...] = jnp.zeros_like(x2_scr)
        le = sub <= lane
        ge = sub >= lane
        spos = sub.astype(F32)
        tpos = lane.astype(F32)
        diff = (lane - sub).astype(F32)
        for h in range(8):
            dm_scr[h] = (jnp.where(le, jnp.exp(lg_ref[h] * diff), 0.0)
                         + jnp.where(ge, jnp.exp(lg_ref[8 + h] * (-diff)), 0.0))
        for p in range(4):
            dec_scr[0, p] = jnp.exp(lgkt_ref[:, p:p + 1] * (tpos + 1.0))
            dec_scr[1, p] = jnp.exp(lgkt_ref[:, 4 + p:5 + p] * (float(CHUNK) - tpos))
            dec_scr[2, p] = jnp.exp(lgk_ref[p:p + 1, :] * (float(CHUNK) - 1.0 - spos))

    @pl.when(jt == 0)
    def _():
        s_scr[...] = jnp.zeros_like(s_scr)
        c_scr[...] = jnp.zeros_like(c_scr)
        n_scr[...] = jnp.zeros_like(n_scr)
        m_state[...] = jnp.zeros_like(m_state)

    r = _Bag(rq_ref=rq_ref, rk_ref=rk_ref, rv_ref=rv_ref, rg_ref=rg_ref, qk_ref=qk_ref, mv_ref=mv_ref, mo_ref=mo_ref,
             g_ref=g_ref, gt_ref=gt_ref, sretb_ref=sretb_ref, cmb_ref=cmb_ref, nmb_ref=nmb_ref, mmb_ref=mmb_ref,
             lgk_ref=lgk_ref, nw_ref=nw_ref, s_scr=s_scr, c_scr=c_scr, n_scr=n_scr, m_state=m_state,
             dm_scr=dm_scr, dec_scr=dec_scr, mix_scr=mix_scr, live=s < tiles)
    cuts = _ffn_splits(d_ff, 4)
    ffn_piece = lambda i: _ffn_cols(h, wi_ref, act_scr, cuts[i], cuts[i + 1], d_ff)

    x = _ctx_or_latent_rows(ctx_ref, x_ref, jnp.clip(s - 1, 0, tiles - 1), nb, tpb)
    x1, h = _ffn_in(x, mix_scr[...], p_ref, wo_ref)

    nxt = _attn_inproj_stages(x2_scr[...], ab2_ref, wa_ref, nwa_ref, cos_ref[...], sin_ref[...], y2_ref)
    chunk_a, chunk_b = _mixer_chunk_stages(0, r), _mixer_chunk_stages(1, r)
    next(nxt)
    next(chunk_a), next(chunk_b)
    next(chunk_a), next(chunk_b)
    ffn_piece(0)
    next(chunk_a), next(chunk_a)
    ffn_piece(1)
    next(chunk_a), next(chunk_a)
    next(nxt)
    ffn_piece(2)
    ffn_piece(3)
    next(chunk_b), next(chunk_b)
    next(nxt)
    x2 = x1 + p_ref[3:4, :] * _dot(act_scr[...], w2_ref[...])
    o_ref[...] = x2
    next(chunk_b), next(chunk_b)
    x2_scr[...] = x2


def _mixer_ffn(y, g, gt, states, lg_smem, lgk, lgkt, nw, xs, prm, wo, wi_all, w2_all, layer,
               ab_next, w_next, nw_next, cos, sin, nb, tpb):
    gw = 4 * LANES
    tm = ROW_TILE
    cpt = tm // CHUNK
    sretb, cmb, nmb, mmb, qk_act = states
    tiles = nb * tpb
    d = xs[-1].shape[1]
    d_ff = w2_all.shape[1]
    n_next = w_next.shape[1]
    mix = lambda s: jnp.minimum(s, tiles - 1)
    ffn = lambda s: jnp.clip(s - 1, 0, tiles - 1)
    nxt = lambda s: jnp.maximum(s - 2, 0)
    state_spec = lambda a: pl.BlockSpec((cpt,) + a.shape[1:], lambda s: (mix(s),) + (0,) * (a.ndim - 1))
    def sel(t):
        b, j = _div_small(t, tpb, nb)
        return b * 2 + jnp.minimum(j, 1), 0, 0

    layer_spec = lambda a: pl.BlockSpec((None,) + a.shape[1:], lambda s: (layer, 0, 0),
                                        pipeline_mode=pl.Buffered(1))
    return pl.pallas_call(
        functools.partial(_mixer_ffn_kernel, tiles=tiles, tpb=tpb, d_ff=d_ff),
        grid=(tiles + 2,),
        in_specs=[_smem_spec(),
                  pl.BlockSpec((tm, 6 * gw), lambda s: (mix(s), 0)),
                  pl.BlockSpec((tm, 2 * gw), lambda s: (mix(s), 0)),
                  pl.BlockSpec((tm, LANES), lambda s: (mix(s), 0)),
                  pl.BlockSpec((32, tm), lambda s: (0, mix(s))),
                  state_spec(sretb), state_spec(cmb), state_spec(nmb), state_spec(mmb),
                  _const_spec(lgk.shape), _const_spec(lgkt.shape), _const_spec(nw.shape)]
                 + _split_row_specs(tm, d, nb, tpb, ffn)
                 + [pl.BlockSpec((None, 8, d), lambda s: sel(ffn(s))),
                    _const_spec(wo.shape), layer_spec(wi_all), layer_spec(w2_all),
                    pl.BlockSpec((None, 2, d), lambda s: sel(nxt(s))),
                    _const_spec(w_next.shape), _const_spec(nw_next.shape),
                    pl.BlockSpec((tm, LANES), lambda s: (_div_small(nxt(s), tpb, nb)[1], 0)),
                    pl.BlockSpec((tm, LANES), lambda s: (_div_small(nxt(s), tpb, nb)[1], 0))],
        out_specs=[pl.BlockSpec((tm, d), lambda s: (ffn(s), 0)),
                   pl.BlockSpec((tm, n_next), lambda s: (nxt(s), 0))],
        out_shape=[jax.ShapeDtypeStruct((tiles * tm, d), F32),
                   jax.ShapeDtypeStruct((tiles * tm, n_next), BF16)],
        scratch_shapes=[pltpu.VMEM((4, CHUNK, LANES), F32), pltpu.VMEM((4, CHUNK, LANES), F32),
                        pltpu.VMEM((4, HALO, LANES), F32), pltpu.VMEM((8, LANES), F32),
                        pltpu.VMEM((8, CHUNK, LANES), F32), pltpu.VMEM((3, 4, CHUNK, LANES), F32),
                        pltpu.VMEM((tm, 2 * gw), BF16), pltpu.VMEM((tm, d_ff), BF16),
                        pltpu.VMEM((tm, d), F32)],
        compiler_params=_cparams("arbitrary"),
        name="ret_mlstm_mixer_ffn",
    )(lg_smem, y, qk_act, g, gt, sretb, cmb, nmb, mmb, lgk, lgkt, nw,
      *xs, prm, wo, wi_all, w2_all, ab_next, w_next, nw_next, cos, sin)


def _ffn_in(x, m, p_ref, wo_ref):
    x1 = x + p_ref[0:1, :] * _dot(m, wo_ref[...])
    ms = jnp.mean(x1 * x1, axis=-1, keepdims=True)
    return x1, ((x1 * lax.rsqrt(ms + EPS)) * p_ref[1:2, :] + p_ref[2:3, :]).astype(BF16)


def _ffn_cols(h, wi_ref, act_scr, lo, hi, d_ff):
    gate = _dot(h, wi_ref[:, lo:hi])
    up = _dot(h, wi_ref[:, d_ff + lo:d_ff + hi])
    act_scr[:, lo:hi] = (gate * jax.nn.sigmoid(gate) * up).astype(BF16)


def _window_bias():
    kk = np.arange(CHUNK)[:, None]
    t = np.arange(CHUNK)[None, :]
    tabs = []
    for has_prev, has_next in ((False, True), (True, True), (True, False)):
        prev_ok = (kk >= t) & has_prev
        next_ok = (kk <= t) & has_next
        tabs.append(np.where(np.concatenate([prev_ok, next_ok], axis=0), 0.0, NEG))
    return jnp.asarray(np.stack(tabs), F32)


def _attn_ffn_kernel(sink_ref, qkv_ref, kvp_ref, kvn_ref, kvx_ref, bias_ref,
                     x_ref, p_ref, wo_ref, wi_ref, w2_ref, o_ref, m_scr, act_scr, *, tiles, lat_tiles, d_ff):
    s = pl.program_id(0)
    q_ref = qkv_ref
    kw = 2 * LANES
    kc_ref, vc_ref = qkv_ref.at[:, 8 * LANES:8 * LANES + kw], qkv_ref.at[:, 8 * LANES + kw:8 * LANES + 2 * kw]
    kp_ref, vp_ref = kvp_ref.at[:, 0:kw], kvp_ref.at[:, kw:2 * kw]
    kn_ref, vn_ref = kvn_ref.at[:, 0:kw], kvn_ref.at[:, kw:2 * kw]
    kx_ref, vx_ref = kvx_ref.at[:, 0:kw], kvx_ref.at[:, kw:2 * kw]

    @pl.when(s == 0)
    def _():
        m_scr[...] = jnp.zeros_like(m_scr)

    j = _div_small(jnp.minimum(s, tiles - 1), lat_tiles, tiles // lat_tiles)[1]
    grp = lax.broadcasted_iota(jnp.int32, (1, 4 * CHUNK), 1) // CHUNK
    lane = _lane()
    mask_q = [(lane & 32) == 0, (lane & 32) != 0]
    tile4 = lambda b: jnp.concatenate([b] * 4, axis=1)
    biases = [tile4(jnp.where(j == 0, bias_ref[0], bias_ref[1])),
              tile4(jnp.where(j == lat_tiles - 1, bias_ref[2], bias_ref[1]))]
    cuts = _ffn_splits(d_ff, 3)

    def ffn_up(i):
        _ffn_cols(h, wi_ref, act_scr, cuts[i], cuts[i + 1], d_ff)

    def scores(u):
        blk, kvp, a = u // 4, (u % 4) // 2, u % 2
        rows = slice(blk * CHUNK, (blk + 1) * CHUNK)
        qs = jnp.concatenate(
            [jnp.where(mask_q[a], q_ref[rows, (kvp * 4 + g) * LANES:(kvp * 4 + g + 1) * LANES],
                       jnp.zeros((CHUNK, LANES), BF16)) for g in range(4)], axis=0)
        return _dot_nt(kcats[kvp][blk], qs)

    def softmax_pv(u, st):
        blk, kv = u // 4, u % 4
        bias = biases[blk]
        st = jnp.concatenate([st[0:CHUNK] + bias[0:CHUNK], st[CHUNK:2 * CHUNK],
                              st[2 * CHUNK:3 * CHUNK] + bias[CHUNK:2 * CHUNK], st[3 * CHUNK:]], axis=0)
        snk = jnp.where(grp == 0, sink_ref[kv * 4],
                        jnp.where(grp == 1, sink_ref[kv * 4 + 1],
                                  jnp.where(grp == 2, sink_ref[kv * 4 + 2], sink_ref[kv * 4 + 3])))
        m = jnp.maximum(jnp.max(st, axis=0, keepdims=True), snk)
        e = jnp.exp2(st - m)
        denom = jnp.exp2(snk - m) + jnp.sum(e, axis=0, keepdims=True)
        a = kv % 2
        return _dot(vts[kv // 2][blk][a * HEAD_DIM:(a + 1) * HEAD_DIM, :], e.astype(BF16)) * (1.0 / denom)

    def hand_over(blk, outs):
        for kvp in range(2):
            full = jnp.concatenate(outs[2 * kvp:2 * kvp + 2], axis=0)
            for g in range(4):
                m_scr[blk * CHUNK:(blk + 1) * CHUNK, (kvp * 4 + g) * LANES:(kvp * 4 + g + 1) * LANES] = (
                    full[:, g * CHUNK:(g + 1) * CHUNK].T.astype(BF16))

    x1, h = _ffn_in(x_ref[...], m_scr[...], p_ref, wo_ref)

    kcats, vts = [], []
    for kvp in range(2):
        sl = slice(kvp * LANES, (kvp + 1) * LANES)
        k_chunks = [kp_ref[:, sl], kc_ref[0:CHUNK, sl], kc_ref[CHUNK:2 * CHUNK, sl], kn_ref[:, sl]]
        v_chunks = [vp_ref[:, sl], vc_ref[0:CHUNK, sl], vc_ref[CHUNK:2 * CHUNK, sl], vn_ref[:, sl],
                    vx_ref[0:CHUNK, sl], vx_ref[CHUNK:2 * CHUNK, sl]]
        v_t = [v.astype(F32).T.astype(BF16) for v in v_chunks]
        kcats.append([jnp.concatenate(k_chunks[b:b + 3] + [kx_ref[:, sl]], axis=0) for b in range(2)])
        vts.append([jnp.concatenate(v_t[b:b + 3] + v_t[4:], axis=1) for b in range(2)])

    for blk in range(2):
        sts = [scores(4 * blk + kv) for kv in range(4)]
        ffn_up(blk)
        hand_over(blk, [softmax_pv(4 * blk + kv, sts[kv]) for kv in range(4)])
    ffn_up(2)
    o_ref[...] = x1 + p_ref[3:4, :] * _dot(act_scr[...], w2_ref[...])


def _attn_ffn(y, sink, xc, prm, wo, wi_all, w2_all, layer, nb, nc, ctx_chunks):
    tm = ROW_TILE
    cpt = tm // CHUNK
    tpb = nc // cpt
    lat_tiles = (nc - ctx_chunks) // cpt
    tiles = nb * lat_tiles
    ctx_tiles = ctx_chunks // cpt
    assert ctx_tiles == 1 and lat_tiles >= 2
    d = xc.shape[1]
    d_ff = w2_all.shape[1]
    bias = _window_bias()
    n_qkv = y.shape[1]
    kv_w = n_qkv - d
    assert d % kv_w == 0
    kv_col = d // kv_w
    att = lambda s: jnp.minimum(s, tiles - 1)
    ffn = lambda s: jnp.maximum(s - 1, 0)
    split = lambda t: _div_small(t, lat_tiles, nb)
    row_tile = lambda t: split(t)[0] * tpb + ctx_tiles + split(t)[1]
    chunk0 = lambda t: split(t)[0] * nc + ctx_chunks
    prev_c = lambda s: chunk0(att(s)) + jnp.maximum(split(att(s))[1] * cpt - 1, 0)
    next_c = lambda s: chunk0(att(s)) + jnp.minimum(split(att(s))[1] * cpt + cpt, lat_tiles * cpt - 1)
    edge_spec = lambda f: pl.BlockSpec((CHUNK, kv_w), lambda s: (f(s), kv_col))
    layer_spec = lambda a: pl.BlockSpec((None,) + a.shape[1:], lambda s: (layer, 0, 0),
                                        pipeline_mode=pl.Buffered(1))
    return pl.pallas_call(
        functools.partial(_attn_ffn_kernel, tiles=tiles, lat_tiles=lat_tiles, d_ff=d_ff),
        grid=(tiles + 1,),
        in_specs=[_smem_spec(),
                  pl.BlockSpec((tm, n_qkv), lambda s: (row_tile(att(s)), 0)),
                  edge_spec(prev_c), edge_spec(next_c),
                  pl.BlockSpec((tm, kv_w), lambda s: (split(att(s))[0] * tpb, kv_col)),
                  _const_spec(bias.shape),
                  pl.BlockSpec((tm, d), lambda s: (row_tile(ffn(s)), 0)),
                  pl.BlockSpec((None, 8, d), lambda s: (split(ffn(s))[0] * 2 + 1, 0, 0)),
                  _const_spec(wo.shape), layer_spec(wi_all), layer_spec(w2_all)],
        out_specs=pl.BlockSpec((tm, d), lambda s: (ffn(s), 0)),
        out_shape=jax.ShapeDtypeStruct((tiles * tm, d), F32),
        scratch_shapes=[pltpu.VMEM((tm, d), BF16), pltpu.VMEM((tm, d_ff), BF16)],
        compiler_params=_cparams("arbitrary"),
        name="window_gqa_ffn",
    )(sink, y, y, y, y, bias, xc, prm, wo, wi_all, w2_all)


def _pair_cols(w):
    rows, cols = w.shape
    return w.reshape(rows, cols // LANES, 2, 2, 32).transpose(0, 1, 3, 2, 4).reshape(rows, cols)


def _attn_q_cols(w):
    rows = w.shape[0]
    g_per = w.shape[1] // (H_KV * HEAD_DIM)
    return (w.reshape(rows, H_KV // 2, 2, g_per, 2, 32).transpose(0, 1, 3, 4, 2, 5)
            .reshape(rows, w.shape[1]))


def _attn_o_rows(w):
    cols = w.shape[1]
    g_per = w.shape[0] // (H_KV * HEAD_DIM)
    return (w.reshape(H_KV // 2, 2, g_per, HEAD_DIM, cols).transpose(0, 2, 1, 3, 4)
            .reshape(w.shape[0], cols))


def _rope_tables(seq, ctx_len):
    rows = seq // GRID_W
    row = np.repeat(np.arange(rows, dtype=np.float32), GRID_W)
    col = np.tile(np.arange(GRID_W, dtype=np.float32), rows)
    n = HEAD_DIM // 4
    inv = (np.float32(ROPE_BASE) ** (-np.arange(n, dtype=np.float32) / np.float32(n))).astype(np.float32)
    ang = np.concatenate([row[:, None] * inv, col[:, None] * inv], axis=-1).astype(np.float32)
    cos, sin = np.cos(ang), np.sin(ang)
    cos_t = np.concatenate([np.ones((ctx_len, LANES), np.float32), np.tile(cos, (1, 4))], axis=0)
    sin_t = np.concatenate([np.zeros((ctx_len, LANES), np.float32),
                            np.concatenate([-sin, -sin, sin, sin], axis=-1)], axis=0)
    return jnp.asarray(cos_t, F32), jnp.asarray(sin_t, F32)


def _mod_tables(mod, nb, norm_w):
    d = norm_w.shape[-1]
    lat = mod[:nb].reshape(nb, 6, d)
    ctx = jnp.broadcast_to(mod[nb].reshape(1, 6, d), (nb, 6, d))
    both = jnp.stack([ctx, lat], axis=1).reshape(nb * 2, 6, d)
    sh1, sc1, g1, sh2, sc2, g2 = [both[:, k] for k in range(6)]
    ab1 = jnp.stack([norm_w[0] * (1.0 + sc1), sh1], axis=1)
    zeros = jnp.zeros_like(g1)
    prm = jnp.stack([g1, norm_w[1] * (1.0 + sc2), sh2, g2, zeros, zeros, zeros, zeros], axis=1)
    return ab1, prm


def kernel(x, c, ctx, c_ctx, ada_w, ada_b, norm_w, ffn_w_in, ffn_w_out, ab_w_in, ab_w_out,
           ret_log_gamma, ret_norm_w, mlstm_conv_w, mlstm_conv_b, mlstm_gate_b, mlstm_norm_w,
           attn_w_in, attn_w_out, attn_q_norm_w, attn_k_norm_w, attn_sink):
    nb, seq, d = x.shape
    ctx_len = ctx.shape[1]
    depth = ada_w.shape[0]
    assert ctx_len == ROW_TILE and seq % ROW_TILE == 0 and d == 8 * LANES and nb < 8
    t_all = ctx_len + seq
    nc = t_all // CHUNK
    ctx_chunks = ctx_len // CHUNK
    tpb = t_all // ROW_TILE
    dr = d // 2

    rows = jnp.zeros((8, d), F32).at[:nb].set(c).at[nb].set(c_ctx)
    mod_all = _modulation(rows, ada_w, ada_b)
    cos_t, sin_t = _rope_tables(seq, ctx_len)
    wi_all = ffn_w_in.astype(BF16)
    w2_all = ffn_w_out.astype(BF16)
    xs = (ctx.reshape(nb * ctx_len, d), x.reshape(nb * seq, d))

    assert depth == 2 and ab_w_in.shape[0] == 1 and attn_w_in.shape[0] == 1
    ab_0, prm_0 = _mod_tables(mod_all[0], nb, norm_w[0])
    ab_1, prm_1 = _mod_tables(mod_all[1], nb, norm_w[1])

    w = ab_w_in[0]
    w_main = jnp.concatenate([_pair_cols(w[:, :dr]), _pair_cols(w[:, dr:2 * dr]), w[:, 2 * dr:8 * dr]],
                             axis=1).astype(BF16)
    wg = jnp.zeros((d, LANES), F32).at[:, :32].set(w[:, 8 * dr:]).astype(BF16)
    wgt = w[:, 8 * dr:].T.astype(BF16)
    gb = jnp.zeros((1, LANES), F32).at[0, :32].set(mlstm_gate_b[0].reshape(-1))
    gbt = mlstm_gate_b[0].reshape(32, 1)
    lg = ret_log_gamma[0].astype(F32)
    lgk = jnp.tile(jnp.repeat(lg.reshape(2, 4, 2), 32, axis=-1), (1, 1, 2)).reshape(8, LANES)
    cw = jnp.concatenate([mlstm_conv_w[0], mlstm_conv_b[0][None], jnp.zeros((4, 2 * dr), F32)], axis=0)
    nw = jnp.broadcast_to(jnp.concatenate([ret_norm_w[0], mlstm_norm_w[0]]).reshape(8, LANES, 1),
                          (8, LANES, LANES))
    wo_0 = ab_w_out[0].astype(BF16)

    w = attn_w_in[0]
    w_attn = jnp.concatenate([_attn_q_cols(w[:, :d]), _pair_cols(w[:, d:d + 2 * LANES]), w[:, d + 2 * LANES:]],
                             axis=1).astype(BF16)
    lane_w = lambda v: jnp.concatenate([v[:32], v[:32], v[32:], v[32:]])
    nwq = jnp.stack([lane_w(attn_q_norm_w[0]) * (HEAD_DIM ** -0.5 * LOG2E), lane_w(attn_k_norm_w[0])]
                    + [jnp.zeros((LANES,), F32)] * 6)
    wo_1 = _attn_o_rows(attn_w_out[0]).astype(BF16)

    y, g, gt, *states = _inproj_sweep(*xs, ab_0, w_main, wg, wgt, gb, gbt, cos_t, sin_t, lgk, cw, nb, tpb)
    x_mid, y_attn = _mixer_ffn(y, g, gt, states, lg.reshape(-1), lgk, lgk.T, nw, xs, prm_0, wo_0, wi_all, w2_all, 0,
                               ab_1, w_attn, nwq, cos_t, sin_t, nb, tpb)
    out = _attn_ffn(y_attn, attn_sink[0].astype(F32) * LOG2E, x_mid, prm_1, wo_1, wi_all, w2_all, 1, nb, nc, ctx_chunks)
    return out.reshape(nb, seq, d)
```

```python
import functools

import numpy as np
import jax
import jax.numpy as jnp
from jax import lax
from jax.experimental import pallas as pl
from jax.experimental.pallas import tpu as pltpu

F32 = jnp.float32
BF16 = jnp.bfloat16

HEAD_DIM = 64
CHUNK = 128
GRID_W = 64
ROPE_BASE = 10000.0
EPS = 1e-6
H_KV = 4
LANES = 128
ROW_TILE = 256
HALO = 16
LOG2E = 1.4426950408889634
NEG = -1e30
VMEM_LIMIT = 56 * 1024 * 1024


def _cparams(*sem):
    return pltpu.CompilerParams(dimension_semantics=sem, vmem_limit_bytes=VMEM_LIMIT)


def _const_spec(shape):
    nd = len(shape)
    return pl.BlockSpec(shape, lambda *_: (0,) * nd, pipeline_mode=pl.Buffered(1))


def _smem_spec():
    return pl.BlockSpec(memory_space=pltpu.SMEM)


def _lane(shape=(CHUNK, LANES)):
    return lax.broadcasted_iota(jnp.int32, shape, len(shape) - 1)


def _dot(a, b):
    return jnp.dot(a, b, preferred_element_type=F32)


def _dot_nt(a, b):
    return lax.dot_general(a, b, (((1,), (1,)), ((), ())), preferred_element_type=F32)


def _div_small(t, m, n):
    q = 0
    for b in range(1, n):
        q = q + jnp.where(t >= b * m, 1, 0)
    return q, t - q * m


def _split3(x):
    hi = x.astype(BF16)
    r = x - hi.astype(F32)
    mid = r.astype(BF16)
    lo = (r - mid.astype(F32)).astype(BF16)
    return hi, mid, lo


def _log_sigmoid(x):
    return jnp.minimum(x, 0.0) - jnp.log1p(jnp.exp(-jnp.abs(x)))


def _rope(x, cos, sin_signed):
    return x * cos + pltpu.roll(x, LANES // 2, 1) * sin_signed


def _mod_kernel(rows_ref, w_ref, b_ref, o_ref):
    a = rows_ref[...]
    a = a * jax.nn.sigmoid(a)
    a_hi = a.astype(BF16)
    a_lo = (a - a_hi.astype(F32)).astype(BF16)
    w = w_ref[...]
    w_hi = w.astype(BF16)
    w_lo = (w - w_hi.astype(F32)).astype(BF16)
    o_ref[...] = _dot(a_hi, w_hi) + _dot(a_hi, w_lo) + _dot(a_lo, w_hi) + b_ref[...]


def _modulation(rows, ada_w, ada_b):
    depth, d, n = ada_w.shape
    tn = n // 4
    return pl.pallas_call(
        _mod_kernel,
        grid=(depth, n // tn),
        in_specs=[pl.BlockSpec((8, d), lambda l, j: (0, 0)),
                  pl.BlockSpec((None, d, tn), lambda l, j: (l, 0, j)),
                  pl.BlockSpec((None, 1, tn), lambda l, j: (l, 0, j))],
        out_specs=pl.BlockSpec((None, 8, tn), lambda l, j: (l, 0, j)),
        out_shape=jax.ShapeDtypeStruct((depth, 8, n), F32),
        compiler_params=_cparams("arbitrary", "arbitrary"),
        name="adaln_modulation",
    )(rows, ada_w, ada_b.reshape(depth, 1, n))


def _norm_mod(x, ab_ref):
    ms = jnp.mean(x * x, axis=-1, keepdims=True)
    h = (x * lax.rsqrt(ms + EPS)) * ab_ref[0:1, :] + ab_ref[1:2, :]
    return h.astype(BF16)


def _ctx_or_latent_rows(ctx_ref, x_ref, tile, nb, tiles_per_batch):
    return jnp.where(_div_small(tile, tiles_per_batch, nb)[1] == 0, ctx_ref[...], x_ref[...])


def _split_row_specs(tm, d, nb, tpb, tile_of):
    lat = tpb - 1

    def latent_row(i):
        b, j = _div_small(tile_of(i), tpb, nb)
        return b * lat + jnp.maximum(j - 1, 0), 0

    return [pl.BlockSpec((tm, d), lambda i: (_div_small(tile_of(i), tpb, nb)[0], 0)),
            pl.BlockSpec((tm, d), latent_row)]


def _cumsum_cols(tri_bf, lf):
    hi, mid, lo = _split3(lf)
    return _dot(tri_bf, hi) + _dot(tri_bf, mid) + _dot(tri_bf, lo)


def _cumsum_rows(lf, tri_bf):
    hi, mid, lo = _split3(lf)
    return _dot(hi, tri_bf) + _dot(mid, tri_bf) + _dot(lo, tri_bf)


def _ret_state_update(s_ref, p, k2, vt, kdec, cd_lanes, bd):
    kf = (k2.astype(F32) * kdec).astype(BF16)
    s_ref[p] = s_ref[p] * cd_lanes + jnp.where(bd, _dot(vt, kf), 0.0)


def _mlstm_state_update(c_ref, n_ref, m_ref, k_pairs, vt_pairs, c_all, bend, col0, lo, bd):
    cmax = jnp.max(c_all, axis=0, keepdims=True)
    w_all = jnp.exp(c_all - cmax)
    m_old = m_ref[0:1, :]
    mrel = jnp.maximum(m_old, cmax)
    a_row = jnp.exp(m_old - mrel)
    bb_row = jnp.exp(cmax - mrel)
    m_ref[0:1, :] = bend + mrel
    lo_row = lo[0:1, :]
    for p in range(4):
        h0 = col0 + 2 * p
        kw = k_pairs[p] * jnp.where(lo, w_all[:, h0:h0 + 1], w_all[:, h0 + 1:h0 + 2])
        kvt = _dot(vt_pairs[p], kw.astype(BF16))
        nloc = jnp.sum(kw, axis=0, keepdims=True)
        a_l = jnp.where(lo_row, a_row[:, h0:h0 + 1], a_row[:, h0 + 1:h0 + 2])
        bb_l = jnp.where(lo_row, bb_row[:, h0:h0 + 1], bb_row[:, h0 + 1:h0 + 2])
        c_ref[p] = c_ref[p] * a_l + jnp.where(bd, kvt, 0.0) * bb_l
        n_new = (n_ref[p, 0:1, :] + n_ref[p, 1:2, :]) * a_l + nloc * bb_l
        n_ref[p, 0:1, :] = jnp.where(lo_row, n_new, 0.0)
        n_ref[p, 1:2, :] = jnp.where(lo_row, 0.0, n_new)


def _mlstm_dir_weights(st, qn_row, c_col, bt_row, m_prev, tri):
    dl = jnp.where(tri, c_col + bt_row, NEG)
    mx = jnp.max(dl, axis=0, keepdims=True)
    al = bt_row + m_prev
    m_t = jnp.maximum(al, mx)
    w = jnp.exp2(dl - m_t)
    a_t = jnp.exp2(al - m_t)
    sw = st * w
    den = jnp.sum(sw, axis=0, keepdims=True) + a_t * qn_row
    r = 1.0 / jnp.maximum(jnp.abs(den), jnp.exp2(-m_t))
    return sw * r, a_t * r


def _heads_out(ht, nw_tab):
    rows = []
    for a in range(2):
        ha = ht[a * HEAD_DIM:(a + 1) * HEAD_DIM, :]
        ms = jnp.mean(ha * ha, axis=0, keepdims=True)
        rows.append(ha * lax.rsqrt(ms + EPS))
    return (jnp.concatenate(rows, axis=0) * nw_tab).T


def _inproj_sweep_kernel(ctx_ref, x_ref, ab_ref, w_ref, wg_ref, wgt_ref, gb_ref, gbt_ref, cos_ref, sin_ref,
                         lgk_ref, cw_ref,
                         y_ref, g_ref, gt_ref, sret_ref, cm_ref, nm_ref, mm_ref, qk_ref,
                         s_scr, c_scr, n_scr, m_scr, kdec_scr,
                         p_rk, p_rv, p_mv, p_mqk, p_g, next_row_scr, *, tiles, tpb):
    i = pl.program_id(0)

    def order(t):
        v = _div_small(t, tpb, tiles // tpb)[1]
        return jnp.where(v == 0, 0, tpb - v)

    jt_a = order(jnp.maximum(i - 1, 0))
    lane = _lane()
    sub = lax.broadcasted_iota(jnp.int32, (CHUNK, LANES), 0)
    lo = lane < HEAD_DIM
    lo_row = lo[0:1, :]
    bd_ret = (sub >= HEAD_DIM) == ((lane & 32) != 0)
    bd_m = (sub >= HEAD_DIM) == (lane >= HEAD_DIM)
    gw = 4 * LANES

    @pl.when(i == 0)
    def _():
        pos = sub.astype(F32)
        for p in range(4):
            kdec_scr[p] = jnp.exp(lgk_ref[4 + p:5 + p, :] * pos)
        for ref in (p_rk, p_rv, p_mv, p_mqk, p_g, next_row_scr):
            ref[...] = jnp.zeros_like(ref)

    @pl.when(jt_a == 0)
    def _():
        s_scr[...] = jnp.zeros_like(s_scr)
        c_scr[...] = jnp.zeros_like(c_scr)
        n_scr[...] = jnp.zeros_like(n_scr)
        m_scr[...] = jnp.zeros_like(m_scr)

    tile_i = jnp.minimum(i, tiles - 1)
    jt_i = order(tile_i)
    hb = _norm_mod(jnp.where(jt_i == 0, ctx_ref[...], x_ref[...]), ab_ref)
    cos, sin = cos_ref[...], sin_ref[...]

    def project(j):
        acc = _dot(hb, w_ref[:, j * gw:(j + 1) * gw])
        if j in (0, 1):
            if j == 0:
                acc = acc * (HEAD_DIM ** -0.5)
            acc = jnp.concatenate([_rope(acc[:, p * LANES:(p + 1) * LANES], cos, sin) for p in range(4)], axis=1)
        elif j == 3:
            acc = acc * jax.nn.sigmoid(acc)
        elif j == 7:
            acc = jax.nn.sigmoid(acc)
        return acc.astype(BF16)

    raw_q, raw_k = project(4), project(5)

    prev_on = jnp.where(jt_a <= 1, 0.0, 1.0).astype(F32)
    next_on = jnp.where((jt_a == 0) | (jt_a == tpb - 1), 0.0, 1.0).astype(F32)
    cur = p_mqk[...].astype(F32)
    n = cur.shape[0]
    row = lax.broadcasted_iota(jnp.int32, cur.shape, 0)
    prev_row = jnp.concatenate([raw_q[n - HALO:, :], raw_k[n - HALO:, :]], axis=1)[HALO - 1:HALO, :].astype(F32)
    xm = jnp.where(row == 0, prev_row * prev_on, pltpu.roll(cur, 1, 0))
    xp = jnp.where(row == n - 1, next_row_scr[0:1, :] * next_on, pltpu.roll(cur, n - 1, 0))
    conv = cw_ref[3:4, :] + cw_ref[0:1, :] * xm + cw_ref[1:2, :] * cur + cw_ref[2:3, :] * xp
    qk = conv * jax.nn.sigmoid(conv)
    qk_ref[:, 0:gw] = (qk[:, 0:gw] * (HEAD_DIM ** -0.5)).astype(BF16)
    qk_ref[:, gw:] = qk[:, gw:].astype(BF16)
    le_bf = (sub <= lane).astype(BF16)
    chunks = (1, 0)
    pre = {}
    for blk in chunks:
        rows = slice(blk * CHUNK, (blk + 1) * CHUNK)
        g = p_g[rows, :]
        bal = pltpu.roll(_cumsum_cols(le_bf, _log_sigmoid(g)), LANES - 8, 1)
        c_all = g - bal
        cmax = jnp.max(c_all, axis=0, keepdims=True)
        w_all = jnp.exp(c_all - cmax)
        vts, ks, nlocs = [], [], []
        for p in range(8):
            sl = slice((p % 4) * LANES, (p % 4 + 1) * LANES)
            if p < 4:
                v2 = p_rv[rows, sl]
                ks.append((p_rk[rows, sl].astype(F32) * kdec_scr[p]).astype(BF16))
            else:
                v2 = p_mv[rows, sl]
                h0 = 16 + 2 * (p - 4)
                kw = qk[rows, gw + (p - 4) * LANES:gw + (p - 3) * LANES] * jnp.where(
                    lo, w_all[:, h0:h0 + 1], w_all[:, h0 + 1:h0 + 2])
                ks.append(kw.astype(BF16))
                nlocs.append(jnp.sum(kw, axis=0, keepdims=True))
            vts.append(v2.astype(F32).T.astype(BF16))
        pre[blk] = (vts, ks, nlocs, cmax, bal[0:1, :])

    cur_rk = project(1)
    y_ref[:, 1 * gw:2 * gw] = cur_rk
    y_ref[:, 0:gw] = project(0)
    kvs = {blk: [_dot(pre[blk][0][p], pre[blk][1][p]) for p in range(8)] for blk in chunks}
    cur_rv = project(2)
    y_ref[:, 2 * gw:3 * gw] = cur_rv
    y_ref[:, 3 * gw:4 * gw] = project(3)

    for blk in chunks:
        _, _, nlocs, cmax, bend = pre[blk]
        sret_ref[blk] = s_scr[...].astype(BF16)
        cm_ref[blk] = c_scr[...].astype(BF16)
        nm_ref[blk] = n_scr[...].astype(BF16)
        mm_ref[blk] = m_scr[...]
        m_old = m_scr[0:1, :]
        mrel = jnp.maximum(m_old, cmax)
        a_row = jnp.exp(m_old - mrel)
        bb_row = jnp.exp(cmax - mrel)
        m_scr[0:1, :] = bend + mrel
        for p in range(4):
            cd = jnp.exp(lgk_ref[4 + p:5 + p, :] * float(CHUNK))
            s_scr[p] = s_scr[p] * cd + jnp.where(bd_ret, kvs[blk][p], 0.0)
            h0 = 16 + 2 * p
            a_l = jnp.where(lo_row, a_row[:, h0:h0 + 1], a_row[:, h0 + 1:h0 + 2])
            bb_l = jnp.where(lo_row, bb_row[:, h0:h0 + 1], bb_row[:, h0 + 1:h0 + 2])
            c_scr[p] = c_scr[p] * a_l + jnp.where(bd_m, kvs[blk][4 + p], 0.0) * bb_l
            n_new = (n_scr[p, 0:1, :] + n_scr[p, 1:2, :]) * a_l + nlocs[p] * bb_l
            n_scr[p, 0:1, :] = jnp.where(lo_row, n_new, 0.0)
            n_scr[p, 1:2, :] = jnp.where(lo_row, 0.0, n_new)

    cur_mv = project(6)
    y_ref[:, 4 * gw:5 * gw] = cur_mv
    y_ref[:, 5 * gw:6 * gw] = project(7)
    gates = _dot(hb, wg_ref[...]) + gb_ref[...]
    g_ref[...] = gates
    gt_ref[...] = _dot_nt(wgt_ref[...], hb) + gbt_ref[...]
    next_row_scr[...] = p_mqk[0:HALO, :].astype(F32)
    p_mqk[:, 0:gw] = raw_q
    p_mqk[:, gw:] = raw_k
    p_rk[...] = cur_rk
    p_rv[...] = cur_rv
    p_mv[...] = cur_mv
    p_g[...] = gates


def _inproj_sweep(ctx2, x2, ab, w, wg, wgt, gb, gbt, cos, sin, lgk, cw, nb, tpb):
    d = x2.shape[1]
    tm = ROW_TILE
    cpt = tm // CHUNK
    gw = 4 * LANES
    tiles = nb * tpb
    r = tiles * tm
    lat = tpb - 1
    def visit(t):
        b, v = _div_small(t, tpb, nb)
        return b, jnp.where(v == 0, 0, tpb - v)

    cur = lambda i: visit(jnp.minimum(i, tiles - 1))
    flat = lambda bj: bj[0] * tpb + bj[1]
    tile_i = lambda i: flat(cur(i))
    tile_a = lambda i: flat(visit(jnp.maximum(i - 1, 0)))
    in_batch = lambda i: cur(i)[1]
    sel = lambda i: (cur(i)[0] * 2 + jnp.minimum(cur(i)[1], 1), 0, 0)
    state = lambda *dims: pl.BlockSpec((cpt,) + dims, lambda i: (tile_a(i),) + (0,) * len(dims))
    nchunks = tiles * cpt
    return pl.pallas_call(
        functools.partial(_inproj_sweep_kernel, tiles=tiles, tpb=tpb),
        grid=(tiles + 1,),
        in_specs=[pl.BlockSpec((tm, d), lambda i: (cur(i)[0], 0)),
                  pl.BlockSpec((tm, d), lambda i: (cur(i)[0] * lat + jnp.maximum(cur(i)[1] - 1, 0), 0)),
                  pl.BlockSpec((None, 2, d), sel),
                  _const_spec(w.shape), _const_spec(wg.shape), _const_spec(wgt.shape),
                  _const_spec(gb.shape), _const_spec(gbt.shape),
                  pl.BlockSpec((tm, LANES), lambda i: (in_batch(i), 0)),
                  pl.BlockSpec((tm, LANES), lambda i: (in_batch(i), 0)),
                  _const_spec(lgk.shape), _const_spec(cw.shape)],
        out_specs=[pl.BlockSpec((tm, 6 * gw), lambda i: (tile_i(i), 0)),
                   pl.BlockSpec((tm, LANES), lambda i: (tile_i(i), 0)),
                   pl.BlockSpec((32, tm), lambda i: (0, tile_i(i))),
                   state(4, CHUNK, LANES), state(4, CHUNK, LANES), state(4, HALO, LANES), state(8, LANES),
                   pl.BlockSpec((tm, 2 * gw), lambda i: (tile_a(i), 0))],
        out_shape=[jax.ShapeDtypeStruct((r, 6 * gw), BF16),
                   jax.ShapeDtypeStruct((r, LANES), F32),
                   jax.ShapeDtypeStruct((32, r), F32),
                   jax.ShapeDtypeStruct((nchunks, 4, CHUNK, LANES), BF16),
                   jax.ShapeDtypeStruct((nchunks, 4, CHUNK, LANES), BF16),
                   jax.ShapeDtypeStruct((nchunks, 4, HALO, LANES), BF16),
                   jax.ShapeDtypeStruct((nchunks, 8, LANES), F32),
                   jax.ShapeDtypeStruct((r, 2 * gw), BF16)],
        scratch_shapes=[pltpu.VMEM((4, CHUNK, LANES), F32), pltpu.VMEM((4, CHUNK, LANES), F32),
                        pltpu.VMEM((4, HALO, LANES), F32), pltpu.VMEM((8, LANES), F32),
                        pltpu.VMEM((4, CHUNK, LANES), F32),
                        pltpu.VMEM((tm, gw), BF16), pltpu.VMEM((tm, gw), BF16), pltpu.VMEM((tm, gw), BF16),
                        pltpu.VMEM((tm, 2 * gw), BF16), pltpu.VMEM((tm, LANES), F32),
                        pltpu.VMEM((HALO, 2 * gw), F32)],
        compiler_params=_cparams("arbitrary"),
        name="inproj_bwd_sweep",
    )(ctx2, x2, ab, w, wg, wgt, gb, gbt, cos, sin, lgk, cw)


class _Bag:
    def __init__(self, **kw):
        self.__dict__.update(kw)


def _mixer_chunk_stages(blk, r):
    rows = slice(blk * CHUNK, (blk + 1) * CHUNK)
    lane = _lane()
    sub = lax.broadcasted_iota(jnp.int32, (CHUNK, LANES), 0)
    lo = lane < HEAD_DIM
    sub_lo = sub < HEAD_DIM
    mask_ret = [(lane & 32) == 0, (lane & 32) != 0]
    mask_nat = [lo, lane >= HEAD_DIM]
    bd_ret = (sub >= HEAD_DIM) == ((lane & 32) != 0)
    bd_m = (sub >= HEAD_DIM) == (lane >= HEAD_DIM)
    le = sub <= lane
    ge = sub >= lane

    g = r.g_ref[rows, :]
    gt = r.gt_ref[:, rows]
    lf_col = _log_sigmoid(g)
    lf_row = _log_sigmoid(gt)
    le_bf = le.astype(BF16)
    ge_bf = ge.astype(BF16)
    pre_col = _cumsum_cols(ge_bf, lf_col)
    bal_f = pltpu.roll(pre_col, LANES - 8, 1)
    bal_b = pltpu.roll(pre_col[CHUNK - 1:CHUNK, :] - pre_col + lf_col, LANES - 8, 1)
    cf_all = g - bal_f
    cb_all = g - bal_b
    bf_row = _cumsum_rows(lf_row, le_bf)
    bb_row = bf_row[:, CHUNK - 1:CHUNK] - bf_row + lf_row
    qb, kb, vts, kf32, qt, vbd = [], [], [], [], [], []
    for p in range(8):
        sl = slice((p % 4) * LANES, (p % 4 + 1) * LANES)
        if p < 4:
            q2, k2, v2 = r.rq_ref[rows, sl], r.rk_ref[rows, sl], r.rv_ref[rows, sl]
            qf = q2.astype(F32)
            kf = None
        else:
            q2 = r.qk_ref[rows, sl]
            k2 = r.qk_ref[rows, 4 * LANES + (p - 4) * LANES:4 * LANES + (p - 3) * LANES]
            v2 = r.mv_ref[rows, sl]
            qf, kf = q2.astype(F32), k2.astype(F32)
        vt = v2.astype(F32).T.astype(BF16)
        qb.append(q2)
        kb.append(k2)
        vts.append(vt)
        kf32.append(kf)
        qt.append(qf.T)
        vbd.append([jnp.where(sub_lo, vt, jnp.zeros_like(vt)), jnp.where(sub_lo, jnp.zeros_like(vt), vt)])
    yield

    st2 = []
    for p in range(8):
        masks = mask_ret if p < 4 else mask_nat
        zero = jnp.zeros_like(qb[p])
        qstack = jnp.concatenate([jnp.where(masks[0], qb[p], zero), jnp.where(masks[1], qb[p], zero)], axis=0)
        st2.append(_dot_nt(kb[p], qstack))
    yield

    qn = [_dot_nt(jnp.concatenate([r.n_scr[p].astype(BF16), r.nmb_ref[blk, p]], axis=0), qb[4 + p])
          for p in range(4)]
    qn_f = [x[0:HALO] for x in qn]
    qn_b = [x[HALO:2 * HALO] for x in qn]
    cf2, cb2, bf2, bb2 = cf_all * LOG2E, cb_all * LOG2E, bf_row * LOG2E, bb_row * LOG2E
    m_f2, m_b2 = r.m_state[0:1, :] * LOG2E, r.mmb_ref[blk, 0:1, :] * LOG2E
    lhs, rhs = [], []
    for p in range(8):
        if p < 4:
            pts = [(st2[p][:, a * LANES:(a + 1) * LANES] * r.dm_scr[2 * p + a]).astype(BF16) for a in range(2)]
            x_f, x_b = r.dec_scr[0, p], r.dec_scr[1, p]
            old = [r.s_scr[p].astype(BF16), r.sretb_ref[blk, p]]
        else:
            pts, cf, cb = [], [], []
            for a in range(2):
                h = 2 * (p - 4) + a
                st = st2[p][:, a * LANES:(a + 1) * LANES]
                pf, coef_f = _mlstm_dir_weights(st, qn_f[p - 4][a:a + 1, :], cf2[:, h:h + 1],
                                                bf2[8 + h:9 + h, :], m_f2[0:1, h:h + 1], le)
                pb, coef_b = _mlstm_dir_weights(st, qn_b[p - 4][a:a + 1, :], cb2[:, 16 + h:17 + h],
                                                bb2[24 + h:25 + h, :], m_b2[0:1, 16 + h:17 + h], ge)
                pts.append((pf + pb).astype(BF16))
                cf.append(coef_f)
                cb.append(coef_b)
            x_f = jnp.where(sub_lo, cf[0], cf[1])
            x_b = jnp.where(sub_lo, cb[0], cb[1])
            old = [r.c_scr[p - 4].astype(BF16), r.cmb_ref[blk, p - 4]]
        lhs.append(jnp.concatenate(vbd[p] + old, axis=1))
        rhs.append(jnp.concatenate(pts + [(qt[p] * x_f).astype(BF16), (qt[p] * x_b).astype(BF16)], axis=0))
    yield

    ht = [_dot(lhs[p], rhs[p]) for p in range(8)]
    yield

    for p in range(8):
        sl = slice((p % 4) * LANES, (p % 4 + 1) * LANES)
        gate_ref = r.rg_ref if p < 4 else r.mo_ref
        y = _heads_out(ht[p], r.nw_ref[p])
        new = (y * gate_ref[rows, sl].astype(F32)).astype(BF16)
        r.mix_scr[rows, p * LANES:(p + 1) * LANES] = jnp.where(r.live, new, r.mix_scr[rows, p * LANES:(p + 1) * LANES])
    yield

    for p in range(4):
        cd = jnp.exp(r.lgk_ref[p:p + 1, :] * float(CHUNK))
        _ret_state_update(r.s_scr, p, kb[p], vts[p], r.dec_scr[2, p], cd, bd_ret)
    _mlstm_state_update(r.c_scr, r.n_scr, r.m_state, kf32[4:], vts[4:], cf_all, bal_f[CHUNK - 1:CHUNK, :],
                        0, lo, bd_m)
    yield


def _attn_inproj_stages(x, ab_ref, w_ref, nw_ref, cos, sin, y_ref):
    hb = _norm_mod(x, ab_ref)
    r2 = lax.broadcasted_iota(jnp.int32, (2 * LANES, 2 * LANES), 0)
    c2 = lax.broadcasted_iota(jnp.int32, (2 * LANES, 2 * LANES), 1)
    same_head = (((r2 ^ c2) & (LANES | 32)) == 0).astype(BF16)
    acc_q = _dot(hb, w_ref[:, 0:8 * LANES])
    acc_k = _dot(hb, w_ref[:, 8 * LANES:10 * LANES])
    y_ref[:, 10 * LANES:12 * LANES] = _dot(hb, w_ref[:, 10 * LANES:12 * LANES]).astype(BF16)
    yield
    for j in range(5):
        acc = acc_q[:, j * 2 * LANES:(j + 1) * 2 * LANES] if j < 4 else acc_k
        sq = acc * acc
        ms = _dot(sq.astype(BF16), same_head) * (1.0 / HEAD_DIM)
        nrm = acc * lax.rsqrt(ms + EPS)
        nw = nw_ref[0:1, :] if j < 4 else nw_ref[1:2, :]
        for v in range(2):
            ls = slice(v * LANES, (v + 1) * LANES)
            y_ref[:, (2 * j + v) * LANES:(2 * j + v + 1) * LANES] = _rope(nrm[:, ls] * nw, cos, sin).astype(BF16)
        if j in (1, 4):
            yield


def _ffn_splits(d_ff, pieces):
    blocks = d_ff // (2 * LANES)
    assert blocks * 2 * LANES == d_ff and blocks >= pieces
    cuts = [((i * blocks) // pieces) * 2 * LANES for i in range(pieces)]
    return cuts + [d_ff]


def _mixer_ffn_kernel(lg_ref, y_ref, qk_ref,
                      g_ref, gt_ref, sretb_ref, cmb_ref, nmb_ref, mmb_ref,
                      lgk_ref, lgkt_ref, nw_ref,
                      ctx_ref, x_ref, p_ref, wo_ref, wi_ref, w2_ref,
                      ab2_ref, wa_ref, nwa_ref, cos_ref, sin_ref,
                      o_ref, y2_ref,
                      s_scr, c_scr, n_scr, m_state, dm_scr, dec_scr, mix_scr, act_scr, x2_scr,
                      *, tiles, tpb, d_ff):
    s = pl.program_id(0)
    nb = tiles // tpb
    jt = _div_small(jnp.minimum(s, tiles - 1), tpb, nb)[1]
    gw = 4 * LANES
    rq_ref, rk_ref, rv_ref, rg_ref, mv_ref, mo_ref = [y_ref.at[:, j * gw:(j + 1) * gw] for j in range(6)]
    lane = _lane()
    sub = lax.broadcasted_iota(jnp.int32, (CHUNK, LANES), 0)

    @pl.when(s == 0)
    def _():
        mix_scr[...] = jnp.zeros_like(mix_scr)
        x2_scr[...] = jnp.zeros_like(x2_scr)
        le = sub <= lane
        ge = sub >= lane
        spos = sub.astype(F32)
        tpos = lane.astype(F32)
        diff = (lane - sub).astype(F32)
        for h in range(8):
            dm_scr[h] = (jnp.where(le, jnp.exp(lg_ref[h] * diff), 0.0)
                         + jnp.where(ge, jnp.exp(lg_ref[8 + h] * (-diff)), 0.0))
        for p in range(4):
            dec_scr[0, p] = jnp.exp(lgkt_ref[:, p:p + 1] * (tpos + 1.0))
            dec_scr[1, p] = jnp.exp(lgkt_ref[:, 4 + p:5 + p] * (float(CHUNK) - tpos))
            dec_scr[2, p] = jnp.exp(lgk_ref[p:p + 1, :] * (float(CHUNK) - 1.0 - spos))

    @pl.when(jt == 0)
    def _():
        s_scr[...] = jnp.zeros_like(s_scr)
        c_scr[...] = jnp.zeros_like(c_scr)
        n_scr[...] = jnp.zeros_like(n_scr)
        m_state[...] = jnp.zeros_like(m_state)

    r = _Bag(rq_ref=rq_ref, rk_ref=rk_ref, rv_ref=rv_ref, rg_ref=rg_ref, qk_ref=qk_ref, mv_ref=mv_ref, mo_ref=mo_ref,
             g_ref=g_ref, gt_ref=gt_ref, sretb_ref=sretb_ref, cmb_ref=cmb_ref, nmb_ref=nmb_ref, mmb_ref=mmb_ref,
             lgk_ref=lgk_ref, nw_ref=nw_ref, s_scr=s_scr, c_scr=c_scr, n_scr=n_scr, m_state=m_state,
             dm_scr=dm_scr, dec_scr=dec_scr, mix_scr=mix_scr, live=s < tiles)
    cuts = _ffn_splits(d_ff, 4)
    ffn_piece = lambda i: _ffn_cols(h, wi_ref, act_scr, cuts[i], cuts[i + 1], d_ff)

    x = _ctx_or_latent_rows(ctx_ref, x_ref, jnp.clip(s - 1, 0, tiles - 1), nb, tpb)
    x1, h = _ffn_in(x, mix_scr[...], p_ref, wo_ref)

    nxt = _attn_inproj_stages(x2_scr[...], ab2_ref, wa_ref, nwa_ref, cos_ref[...], sin_ref[...], y2_ref)
    chunk_a, chunk_b = _mixer_chunk_stages(0, r), _mixer_chunk_stages(1, r)
    next(nxt)
    next(chunk_a), next(chunk_b)
    next(chunk_a), next(chunk_b)
    ffn_piece(0)
    next(chunk_a), next(chunk_a)
    ffn_piece(1)
    next(chunk_a), next(chunk_a)
    next(nxt)
    ffn_piece(2)
    ffn_piece(3)
    next(chunk_b), next(chunk_b)
    next(nxt)
    x2 = x1 + p_ref[3:4, :] * _dot(act_scr[...], w2_ref[...])
    o_ref[...] = x2
    next(chunk_b), next(chunk_b)
    x2_scr[...] = x2


def _mixer_ffn(y, g, gt, states, lg_smem, lgk, lgkt, nw, xs, prm, wo, wi_all, w2_all, layer,
               ab_next, w_next, nw_next, cos, sin, nb, tpb):
    gw = 4 * LANES
    tm = ROW_TILE
    cpt = tm // CHUNK
    sretb, cmb, nmb, mmb, qk_act = states
    tiles = nb * tpb
    d = xs[-1].shape[1]
    d_ff = w2_all.shape[1]
    n_next = w_next.shape[1]
    mix = lambda s: jnp.minimum(s, tiles - 1)
    ffn = lambda s: jnp.clip(s - 1, 0, tiles - 1)
    nxt = lambda s: jnp.maximum(s - 2, 0)
    state_spec = lambda a: pl.BlockSpec((cpt,) + a.shape[1:], lambda s: (mix(s),) + (0,) * (a.ndim - 1))
    def sel(t):
        b, j = _div_small(t, tpb, nb)
        return b * 2 + jnp.minimum(j, 1), 0, 0

    layer_spec = lambda a: pl.BlockSpec((None,) + a.shape[1:], lambda s: (layer, 0, 0),
                                        pipeline_mode=pl.Buffered(1))
    return pl.pallas_call(
        functools.partial(_mixer_ffn_kernel, tiles=tiles, tpb=tpb, d_ff=d_ff),
        grid=(tiles + 2,),
        in_specs=[_smem_spec(),
                  pl.BlockSpec((tm, 6 * gw), lambda s: (mix(s), 0)),
                  pl.BlockSpec((tm, 2 * gw), lambda s: (mix(s), 0)),
                  pl.BlockSpec((tm, LANES), lambda s: (mix(s), 0)),
                  pl.BlockSpec((32, tm), lambda s: (0, mix(s))),
                  state_spec(sretb), state_spec(cmb), state_spec(nmb), state_spec(mmb),
                  _const_spec(lgk.shape), _const_spec(lgkt.shape), _const_spec(nw.shape)]
                 + _split_row_specs(tm, d, nb, tpb, ffn)
                 + [pl.BlockSpec((None, 8, d), lambda s: sel(ffn(s))),
                    _const_spec(wo.shape), layer_spec(wi_all), layer_spec(w2_all),
                    pl.BlockSpec((None, 2, d), lambda s: sel(nxt(s))),
                    _const_spec(w_next.shape), _const_spec(nw_next.shape),
                    pl.BlockSpec((tm, LANES), lambda s: (_div_small(nxt(s), tpb, nb)[1], 0)),
                    pl.BlockSpec((tm, LANES), lambda s: (_div_small(nxt(s), tpb, nb)[1], 0))],
        out_specs=[pl.BlockSpec((tm, d), lambda s: (ffn(s), 0)),
                   pl.BlockSpec((tm, n_next), lambda s: (nxt(s), 0))],
        out_shape=[jax.ShapeDtypeStruct((tiles * tm, d), F32),
                   jax.ShapeDtypeStruct((tiles * tm, n_next), BF16)],
        scratch_shapes=[pltpu.VMEM((4, CHUNK, LANES), F32), pltpu.VMEM((4, CHUNK, LANES), F32),
                        pltpu.VMEM((4, HALO, LANES), F32), pltpu.VMEM((8, LANES), F32),
                        pltpu.VMEM((8, CHUNK, LANES), F32), pltpu.VMEM((3, 4, CHUNK, LANES), F32),
                        pltpu.VMEM((tm, 2 * gw), BF16), pltpu.VMEM((tm, d_ff), BF16),
                        pltpu.VMEM((tm, d), F32)],
        compiler_params=_cparams("arbitrary"),
        name="ret_mlstm_mixer_ffn",
    )(lg_smem, y, qk_act, g, gt, sretb, cmb, nmb, mmb, lgk, lgkt, nw,
      *xs, prm, wo, wi_all, w2_all, ab_next, w_next, nw_next, cos, sin)


def _ffn_in(x, m, p_ref, wo_ref):
    x1 = x + p_ref[0:1, :] * _dot(m, wo_ref[...])
    ms = jnp.mean(x1 * x1, axis=-1, keepdims=True)
    return x1, ((x1 * lax.rsqrt(ms + EPS)) * p_ref[1:2, :] + p_ref[2:3, :]).astype(BF16)


def _ffn_cols(h, wi_ref, act_scr, lo, hi, d_ff):
    gate = _dot(h, wi_ref[:, lo:hi])
    up = _dot(h, wi_ref[:, d_ff + lo:d_ff + hi])
    act_scr[:, lo:hi] = (gate * jax.nn.sigmoid(gate) * up).astype(BF16)


def _window_bias():
    kk = np.arange(CHUNK)[:, None]
    t = np.arange(CHUNK)[None, :]
    tabs = []
    for has_prev, has_next in ((False, True), (True, True), (True, False)):
        prev_ok = (kk >= t) & has_prev
        next_ok = (kk <= t) & has_next
        tabs.append(np.where(np.concatenate([prev_ok, next_ok], axis=0), 0.0, NEG))
    return jnp.asarray(np.stack(tabs), F32)


def _attn_ffn_kernel(sink_ref, qkv_ref, kvp_ref, kvn_ref, kvx_ref, bias_ref,
                     x_ref, p_ref, wo_ref, wi_ref, w2_ref, o_ref, m_scr, act_scr, *, tiles, lat_tiles, d_ff):
    s = pl.program_id(0)
    q_ref = qkv_ref
    kw = 2 * LANES
    kc_ref, vc_ref = qkv_ref.at[:, 8 * LANES:8 * LANES + kw], qkv_ref.at[:, 8 * LANES + kw:8 * LANES + 2 * kw]
    kp_ref, vp_ref = kvp_ref.at[:, 0:kw], kvp_ref.at[:, kw:2 * kw]
    kn_ref, vn_ref = kvn_ref.at[:, 0:kw], kvn_ref.at[:, kw:2 * kw]
    kx_ref, vx_ref = kvx_ref.at[:, 0:kw], kvx_ref.at[:, kw:2 * kw]

    @pl.when(s == 0)
    def _():
        m_scr[...] = jnp.zeros_like(m_scr)

    j = _div_small(jnp.minimum(s, tiles - 1), lat_tiles, tiles // lat_tiles)[1]
    grp = lax.broadcasted_iota(jnp.int32, (1, 4 * CHUNK), 1) // CHUNK
    lane = _lane()
    mask_q = [(lane & 32) == 0, (lane & 32) != 0]
    tile4 = lambda b: jnp.concatenate([b] * 4, axis=1)
    biases = [tile4(jnp.where(j == 0, bias_ref[0], bias_ref[1])),
              tile4(jnp.where(j == lat_tiles - 1, bias_ref[2], bias_ref[1]))]
    cuts = _ffn_splits(d_ff, 3)

    def ffn_up(i):
        _ffn_cols(h, wi_ref, act_scr, cuts[i], cuts[i + 1], d_ff)

    def scores(u):
        blk, kvp, a = u // 4, (u % 4) // 2, u % 2
        rows = slice(blk * CHUNK, (blk + 1) * CHUNK)
        qs = jnp.concatenate(
            [jnp.where(mask_q[a], q_ref[rows, (kvp * 4 + g) * LANES:(kvp * 4 + g + 1) * LANES],
                       jnp.zeros((CHUNK, LANES), BF16)) for g in range(4)], axis=0)
        return _dot_nt(kcats[kvp][blk], qs)

    def softmax_pv(u, st):
        blk, kv = u // 4, u % 4
        bias = biases[blk]
        st = jnp.concatenate([st[0:CHUNK] + bias[0:CHUNK], st[CHUNK:2 * CHUNK],
                              st[2 * CHUNK:3 * CHUNK] + bias[CHUNK:2 * CHUNK], st[3 * CHUNK:]], axis=0)
        snk = jnp.where(grp == 0, sink_ref[kv * 4],
                        jnp.where(grp == 1, sink_ref[kv * 4 + 1],
                                  jnp.where(grp == 2, sink_ref[kv * 4 + 2], sink_ref[kv * 4 + 3])))
        m = jnp.maximum(jnp.max(st, axis=0, keepdims=True), snk)
        e = jnp.exp2(st - m)
        denom = jnp.exp2(snk - m) + jnp.sum(e, axis=0, keepdims=True)
        a = kv % 2
        return _dot(vts[kv // 2][blk][a * HEAD_DIM:(a + 1) * HEAD_DIM, :], e.astype(BF16)) * (1.0 / denom)

    def hand_over(blk, outs):
        for kvp in range(2):
            full = jnp.concatenate(outs[2 * kvp:2 * kvp + 2], axis=0)
            for g in range(4):
                m_scr[blk * CHUNK:(blk + 1) * CHUNK, (kvp * 4 + g) * LANES:(kvp * 4 + g + 1) * LANES] = (
                    full[:, g * CHUNK:(g + 1) * CHUNK].T.astype(BF16))

    x1, h = _ffn_in(x_ref[...], m_scr[...], p_ref, wo_ref)

    kcats, vts = [], []
    for kvp in range(2):
        sl = slice(kvp * LANES, (kvp + 1) * LANES)
        k_chunks = [kp_ref[:, sl], kc_ref[0:CHUNK, sl], kc_ref[CHUNK:2 * CHUNK, sl], kn_ref[:, sl]]
        v_chunks = [vp_ref[:, sl], vc_ref[0:CHUNK, sl], vc_ref[CHUNK:2 * CHUNK, sl], vn_ref[:, sl],
                    vx_ref[0:CHUNK, sl], vx_ref[CHUNK:2 * CHUNK, sl]]
        v_t = [v.astype(F32).T.astype(BF16) for v in v_chunks]
        kcats.append([jnp.concatenate(k_chunks[b:b + 3] + [kx_ref[:, sl]], axis=0) for b in range(2)])
        vts.append([jnp.concatenate(v_t[b:b + 3] + v_t[4:], axis=1) for b in range(2)])

    for blk in range(2):
        sts = [scores(4 * blk + kv) for kv in range(4)]
        ffn_up(blk)
        hand_over(blk, [softmax_pv(4 * blk + kv, sts[kv]) for kv in range(4)])
    ffn_up(2)
    o_ref[...] = x1 + p_ref[3:4, :] * _dot(act_scr[...], w2_ref[...])


def _attn_ffn(y, sink, xc, prm, wo, wi_all, w2_all, layer, nb, nc, ctx_chunks):
    tm = ROW_TILE
    cpt = tm // CHUNK
    tpb = nc // cpt
    lat_tiles = (nc - ctx_chunks) // cpt
    tiles = nb * lat_tiles
    ctx_tiles = ctx_chunks // cpt
    assert ctx_tiles == 1 and lat_tiles >= 2
    d = xc.shape[1]
    d_ff = w2_all.shape[1]
    bias = _window_bias()
    n_qkv = y.shape[1]
    kv_w = n_qkv - d
    assert d % kv_w == 0
    kv_col = d // kv_w
    att = lambda s: jnp.minimum(s, tiles - 1)
    ffn = lambda s: jnp.maximum(s - 1, 0)
    split = lambda t: _div_small(t, lat_tiles, nb)
    row_tile = lambda t: split(t)[0] * tpb + ctx_tiles + split(t)[1]
    chunk0 = lambda t: split(t)[0] * nc + ctx_chunks
    prev_c = lambda s: chunk0(att(s)) + jnp.maximum(split(att(s))[1] * cpt - 1, 0)
    next_c = lambda s: chunk0(att(s)) + jnp.minimum(split(att(s))[1] * cpt + cpt, lat_tiles * cpt - 1)
    edge_spec = lambda f: pl.BlockSpec((CHUNK, kv_w), lambda s: (f(s), kv_col))
    layer_spec = lambda a: pl.BlockSpec((None,) + a.shape[1:], lambda s: (layer, 0, 0),
                                        pipeline_mode=pl.Buffered(1))
    return pl.pallas_call(
        functools.partial(_attn_ffn_kernel, tiles=tiles, lat_tiles=lat_tiles, d_ff=d_ff),
        grid=(tiles + 1,),
        in_specs=[_smem_spec(),
                  pl.BlockSpec((tm, n_qkv), lambda s: (row_tile(att(s)), 0)),
                  edge_spec(prev_c), edge_spec(next_c),
                  pl.BlockSpec((tm, kv_w), lambda s: (split(att(s))[0] * tpb, kv_col)),
                  _const_spec(bias.shape),
                  pl.BlockSpec((tm, d), lambda s: (row_tile(ffn(s)), 0)),
                  pl.BlockSpec((None, 8, d), lambda s: (split(ffn(s))[0] * 2 + 1, 0, 0)),
                  _const_spec(wo.shape), layer_spec(wi_all), layer_spec(w2_all)],
        out_specs=pl.BlockSpec((tm, d), lambda s: (ffn(s), 0)),
        out_shape=jax.ShapeDtypeStruct((tiles * tm, d), F32),
        scratch_shapes=[pltpu.VMEM((tm, d), BF16), pltpu.VMEM((tm, d_ff), BF16)],
        compiler_params=_cparams("arbitrary"),
        name="window_gqa_ffn",
    )(sink, y, y, y, y, bias, xc, prm, wo, wi_all, w2_all)


def _pair_cols(w):
    rows, cols = w.shape
    return w.reshape(rows, cols // LANES, 2, 2, 32).transpose(0, 1, 3, 2, 4).reshape(rows, cols)


def _attn_q_cols(w):
    rows = w.shape[0]
    g_per = w.shape[1] // (H_KV * HEAD_DIM)
    return (w.reshape(rows, H_KV // 2, 2, g_per, 2, 32).transpose(0, 1, 3, 4, 2, 5)
            .reshape(rows, w.shape[1]))


def _attn_o_rows(w):
    cols = w.shape[1]
    g_per = w.shape[0] // (H_KV * HEAD_DIM)
    return (w.reshape(H_KV // 2, 2, g_per, HEAD_DIM, cols).transpose(0, 2, 1, 3, 4)
            .reshape(w.shape[0], cols))


def _rope_tables(seq, ctx_len):
    rows = seq // GRID_W
    row = np.repeat(np.arange(rows, dtype=np.float32), GRID_W)
    col = np.tile(np.arange(GRID_W, dtype=np.float32), rows)
    n = HEAD_DIM // 4
    inv = (np.float32(ROPE_BASE) ** (-np.arange(n, dtype=np.float32) / np.float32(n))).astype(np.float32)
    ang = np.concatenate([row[:, None] * inv, col[:, None] * inv], axis=-1).astype(np.float32)
    cos, sin = np.cos(ang), np.sin(ang)
    cos_t = np.concatenate([np.ones((ctx_len, LANES), np.float32), np.tile(cos, (1, 4))], axis=0)
    sin_t = np.concatenate([np.zeros((ctx_len, LANES), np.float32),
                            np.concatenate([-sin, -sin, sin, sin], axis=-1)], axis=0)
    return jnp.asarray(cos_t, F32), jnp.asarray(sin_t, F32)


def _mod_tables(mod, nb, norm_w):
    d = norm_w.shape[-1]
    lat = mod[:nb].reshape(nb, 6, d)
    ctx = jnp.broadcast_to(mod[nb].reshape(1, 6, d), (nb, 6, d))
    both = jnp.stack([ctx, lat], axis=1).reshape(nb * 2, 6, d)
    sh1, sc1, g1, sh2, sc2, g2 = [both[:, k] for k in range(6)]
    ab1 = jnp.stack([norm_w[0] * (1.0 + sc1), sh1], axis=1)
    zeros = jnp.zeros_like(g1)
    prm = jnp.stack([g1, norm_w[1] * (1.0 + sc2), sh2, g2, zeros, zeros, zeros, zeros], axis=1)
    return ab1, prm


def kernel(x, c, ctx, c_ctx, ada_w, ada_b, norm_w, ffn_w_in, ffn_w_out, ab_w_in, ab_w_out,
           ret_log_gamma, ret_norm_w, mlstm_conv_w, mlstm_conv_b, mlstm_gate_b, mlstm_norm_w,
           attn_w_in, attn_w_out, attn_q_norm_w, attn_k_norm_w, attn_sink):
    nb, seq, d = x.shape
    ctx_len = ctx.shape[1]
    depth = ada_w.shape[0]
    assert ctx_len == ROW_TILE and seq % ROW_TILE == 0 and d == 8 * LANES and nb < 8
    t_all = ctx_len + seq
    nc = t_all // CHUNK
    ctx_chunks = ctx_len // CHUNK
    tpb = t_all // ROW_TILE
    dr = d // 2

    rows = jnp.zeros((8, d), F32).at[:nb].set(c).at[nb].set(c_ctx)
    mod_all = _modulation(rows, ada_w, ada_b)
    cos_t, sin_t = _rope_tables(seq, ctx_len)
    wi_all = ffn_w_in.astype(BF16)
    w2_all = ffn_w_out.astype(BF16)
    xs = (ctx.reshape(nb * ctx_len, d), x.reshape(nb * seq, d))

    assert depth == 2 and ab_w_in.shape[0] == 1 and attn_w_in.shape[0] == 1
    ab_0, prm_0 = _mod_tables(mod_all[0], nb, norm_w[0])
    ab_1, prm_1 = _mod_tables(mod_all[1], nb, norm_w[1])

    w = ab_w_in[0]
    w_main = jnp.concatenate([_pair_cols(w[:, :dr]), _pair_cols(w[:, dr:2 * dr]), w[:, 2 * dr:8 * dr]],
                             axis=1).astype(BF16)
    wg = jnp.zeros((d, LANES), F32).at[:, :32].set(w[:, 8 * dr:]).astype(BF16)
    wgt = w[:, 8 * dr:].T.astype(BF16)
    gb = jnp.zeros((1, LANES), F32).at[0, :32].set(mlstm_gate_b[0].reshape(-1))
    gbt = mlstm_gate_b[0].reshape(32, 1)
    lg = ret_log_gamma[0].astype(F32)
    lgk = jnp.tile(jnp.repeat(lg.reshape(2, 4, 2), 32, axis=-1), (1, 1, 2)).reshape(8, LANES)
    cw = jnp.concatenate([mlstm_conv_w[0], mlstm_conv_b[0][None], jnp.zeros((4, 2 * dr), F32)], axis=0)
    nw = jnp.broadcast_to(jnp.concatenate([ret_norm_w[0], mlstm_norm_w[0]]).reshape(8, LANES, 1),
                          (8, LANES, LANES))
    wo_0 = ab_w_out[0].astype(BF16)

    w = attn_w_in[0]
    w_attn = jnp.concatenate([_attn_q_cols(w[:, :d]), _pair_cols(w[:, d:d + 2 * LANES]), w[:, d + 2 * LANES:]],
                             axis=1).astype(BF16)
    lane_w = lambda v: jnp.concatenate([v[:32], v[:32], v[32:], v[32:]])
    nwq = jnp.stack([lane_w(attn_q_norm_w[0]) * (HEAD_DIM ** -0.5 * LOG2E), lane_w(attn_k_norm_w[0])]
                    + [jnp.zeros((LANES,), F32)] * 6)
    wo_1 = _attn_o_rows(attn_w_out[0]).astype(BF16)

    y, g, gt, *states = _inproj_sweep(*xs, ab_0, w_main, wg, wgt, gb, gbt, cos_t, sin_t, lgk, cw, nb, tpb)
    x_mid, y_attn = _mixer_ffn(y, g, gt, states, lg.reshape(-1), lgk, lgk.T, nw, xs, prm_0, wo_0, wi_all, w2_all, 0,
                               ab_1, w_attn, nwq, cos_t, sin_t, nb, tpb)
    out = _attn_ffn(y_attn, attn_sink[0].astype(F32) * LOG2E, x_mid, prm_1, wo_1, wi_all, w2_all, 1, nb, nc, ctx_chunks)
    return out.reshape(nb, seq, d)
```

```python
import functools

import numpy as np
import jax
import jax.numpy as jnp
from jax import lax
from jax.experimental import pallas as pl
from jax.experimental.pallas import tpu as pltpu

F32 = jnp.float32
BF16 = jnp.bfloat16

HEAD_DIM = 64
CHUNK = 128
GRID_W = 64
ROPE_BASE = 10000.0
EPS = 1e-6
H_KV = 4
LANES = 128
ROW_TILE = 256
HALO = 16
LOG2E = 1.4426950408889634
NEG = -1e30
VMEM_LIMIT = 56 * 1024 * 1024


def _cparams(*sem):
    return pltpu.CompilerParams(dimension_semantics=sem, vmem_limit_bytes=VMEM_LIMIT)


def _const_spec(shape):
    nd = len(shape)
    return pl.BlockSpec(shape, lambda *_: (0,) * nd, pipeline_mode=pl.Buffered(1))


def _smem_spec():
    return pl.BlockSpec(memory_space=pltpu.SMEM)


def _lane(shape=(CHUNK, LANES)):
    return lax.broadcasted_iota(jnp.int32, shape, len(shape) - 1)


def _dot(a, b):
    return jnp.dot(a, b, preferred_element_type=F32)


def _dot_nt(a, b):
    return lax.dot_general(a, b, (((1,), (1,)), ((), ())), preferred_element_type=F32)


def _div_small(t, m, n):
    q = 0
    for b in range(1, n):
        q = q + jnp.where(t >= b * m, 1, 0)
    return q, t - q * m


def _split3(x):
    hi = x.astype(BF16)
    r = x - hi.astype(F32)
    mid = r.astype(BF16)
    lo = (r - mid.astype(F32)).astype(BF16)
    return hi, mid, lo


def _log_sigmoid(x):
    return jnp.minimum(x, 0.0) - jnp.log1p(jnp.exp(-jnp.abs(x)))


def _rope(x, cos, sin_signed):
    return x * cos + pltpu.roll(x, LANES // 2, 1) * sin_signed


def _mod_kernel(rows_ref, w_ref, b_ref, o_ref):
    a = rows_ref[...]
    a = a * jax.nn.sigmoid(a)
    a_hi = a.astype(BF16)
    a_lo = (a - a_hi.astype(F32)).astype(BF16)
    w = w_ref[...]
    w_hi = w.astype(BF16)
    w_lo = (w - w_hi.astype(F32)).astype(BF16)
    o_ref[...] = _dot(a_hi, w_hi) + _dot(a_hi, w_lo) + _dot(a_lo, w_hi) + b_ref[...]


def _modulation(rows, ada_w, ada_b):
    depth, d, n = ada_w.shape
    tn = n // 4
    return pl.pallas_call(
        _mod_kernel,
        grid=(depth, n // tn),
        in_specs=[pl.BlockSpec((8, d), lambda l, j: (0, 0)),
                  pl.BlockSpec((None, d, tn), lambda l, j: (l, 0, j)),
                  pl.BlockSpec((None, 1, tn), lambda l, j: (l, 0, j))],
        out_specs=pl.BlockSpec((None, 8, tn), lambda l, j: (l, 0, j)),
        out_shape=jax.ShapeDtypeStruct((depth, 8, n), F32),
        compiler_params=_cparams("arbitrary", "arbitrary"),
        name="adaln_modulation",
    )(rows, ada_w, ada_b.reshape(depth, 1, n))


def _norm_mod(x, ab_ref):
    ms = jnp.mean(x * x, axis=-1, keepdims=True)
    h = (x * lax.rsqrt(ms + EPS)) * ab_ref[0:1, :] + ab_ref[1:2, :]
    return h.astype(BF16)


def _ctx_or_latent_rows(ctx_ref, x_ref, tile, nb, tiles_per_batch):
    return jnp.where(_div_small(tile, tiles_per_batch, nb)[1] == 0, ctx_ref[...], x_ref[...])


def _split_row_specs(tm, d, nb, tpb, tile_of):
    lat = tpb - 1

    def latent_row(i):
        b, j = _div_small(tile_of(i), tpb, nb)
        return b * lat + jnp.maximum(j - 1, 0), 0

    return [pl.BlockSpec((tm, d), lambda i: (_div_small(tile_of(i), tpb, nb)[0], 0)),
            pl.BlockSpec((tm, d), latent_row)]


def _cumsum_cols(tri_bf, lf):
    hi, mid, lo = _split3(lf)
    return _dot(tri_bf, hi) + _dot(tri_bf, mid) + _dot(tri_bf, lo)


def _cumsum_rows(lf, tri_bf):
    hi, mid, lo = _split3(lf)
    return _dot(hi, tri_bf) + _dot(mid, tri_bf) + _dot(lo, tri_bf)


def _ret_state_update(s_ref, p, k2, vt, kdec, cd_lanes, bd):
    kf = (k2.astype(F32) * kdec).astype(BF16)
    s_ref[p] = s_ref[p] * cd_lanes + jnp.where(bd, _dot(vt, kf), 0.0)


def _mlstm_state_update(c_ref, n_ref, m_ref, k_pairs, vt_pairs, c_all, bend, col0, lo, bd):
    cmax = jnp.max(c_all, axis=0, keepdims=True)
    w_all = jnp.exp(c_all - cmax)
    m_old = m_ref[0:1, :]
    mrel = jnp.maximum(m_old, cmax)
    a_row = jnp.exp(m_old - mrel)
    bb_row = jnp.exp(cmax - mrel)
    m_ref[0:1, :] = bend + mrel
    lo_row = lo[0:1, :]
    for p in range(4):
        h0 = col0 + 2 * p
        kw = k_pairs[p] * jnp.where(lo, w_all[:, h0:h0 + 1], w_all[:, h0 + 1:h0 + 2])
        kvt = _dot(vt_pairs[p], kw.astype(BF16))
        nloc = jnp.sum(kw, axis=0, keepdims=True)
        a_l = jnp.where(lo_row, a_row[:, h0:h0 + 1], a_row[:, h0 + 1:h0 + 2])
        bb_l = jnp.where(lo_row, bb_row[:, h0:h0 + 1], bb_row[:, h0 + 1:h0 + 2])
        c_ref[p] = c_ref[p] * a_l + jnp.where(bd, kvt, 0.0) * bb_l
        n_new = (n_ref[p, 0:1, :] + n_ref[p, 1:2, :]) * a_l + nloc * bb_l
        n_ref[p, 0:1, :] = jnp.where(lo_row, n_new, 0.0)
        n_ref[p, 1:2, :] = jnp.where(lo_row, 0.0, n_new)


def _mlstm_dir_weights(st, qn_row, c_col, bt_row, m_prev, tri):
    dl = jnp.where(tri, c_col + bt_row, NEG)
    mx = jnp.max(dl, axis=0, keepdims=True)
    al = bt_row + m_prev
    m_t = jnp.maximum(al, mx)
    w = jnp.exp2(dl - m_t)
    a_t = jnp.exp2(al - m_t)
    sw = st * w
    den = jnp.sum(sw, axis=0, keepdims=True) + a_t * qn_row
    r = 1.0 / jnp.maximum(jnp.abs(den), jnp.exp2(-m_t))
    return sw * r, a_t * r


def _heads_out(ht, nw_tab):
    rows = []
    for a in range(2):
        ha = ht[a * HEAD_DIM:(a + 1) * HEAD_DIM, :]
        ms = jnp.mean(ha * ha, axis=0, keepdims=True)
        rows.append(ha * lax.rsqrt(ms + EPS))
    return (jnp.concatenate(rows, axis=0) * nw_tab).T


def _inproj_sweep_kernel(ctx_ref, x_ref, ab_ref, w_ref, wg_ref, wgt_ref, gb_ref, gbt_ref, cos_ref, sin_ref,
                         lgk_ref, cw_ref,
                         y_ref, g_ref, gt_ref, sret_ref, cm_ref, nm_ref, mm_ref, qk_ref,
                         s_scr, c_scr, n_scr, m_scr, kdec_scr,
                         p_rk, p_rv, p_mv, p_mqk, p_g, next_row_scr, *, tiles, tpb):
    i = pl.program_id(0)

    def order(t):
        v = _div_small(t, tpb, tiles // tpb)[1]
        return jnp.where(v == 0, 0, tpb - v)

    jt_a = order(jnp.maximum(i - 1, 0))
    lane = _lane()
    sub = lax.broadcasted_iota(jnp.int32, (CHUNK, LANES), 0)
    lo = lane < HEAD_DIM
    lo_row = lo[0:1, :]
    bd_ret = (sub >= HEAD_DIM) == ((lane & 32) != 0)
    bd_m = (sub >= HEAD_DIM) == (lane >= HEAD_DIM)
    gw = 4 * LANES

    @pl.when(i == 0)
    def _():
        pos = sub.astype(F32)
        for p in range(4):
            kdec_scr[p] = jnp.exp(lgk_ref[4 + p:5 + p, :] * pos)
        for ref in (p_rk, p_rv, p_mv, p_mqk, p_g, next_row_scr):
            ref[...] = jnp.zeros_like(ref)

    @pl.when(jt_a == 0)
    def _():
        s_scr[...] = jnp.zeros_like(s_scr)
        c_scr[...] = jnp.zeros_like(c_scr)
        n_scr[...] = jnp.zeros_like(n_scr)
        m_scr[...] = jnp.zeros_like(m_scr)

    tile_i = jnp.minimum(i, tiles - 1)
    jt_i = order(tile_i)
    hb = _norm_mod(jnp.where(jt_i == 0, ctx_ref[...], x_ref[...]), ab_ref)
    cos, sin = cos_ref[...], sin_ref[...]

    def project(j):
        acc = _dot(hb, w_ref[:, j * gw:(j + 1) * gw])
        if j in (0, 1):
            if j == 0:
                acc = acc * (HEAD_DIM ** -0.5)
            acc = jnp.concatenate([_rope(acc[:, p * LANES:(p + 1) * LANES], cos, sin) for p in range(4)], axis=1)
        elif j == 3:
            acc = acc * jax.nn.sigmoid(acc)
        elif j == 7:
            acc = jax.nn.sigmoid(acc)
        return acc.astype(BF16)

    raw_q, raw_k = project(4), project(5)

    prev_on = jnp.where(jt_a <= 1, 0.0, 1.0).astype(F32)
    next_on = jnp.where((jt_a == 0) | (jt_a == tpb - 1), 0.0, 1.0).astype(F32)
    cur = p_mqk[...].astype(F32)
    n = cur.shape[0]
    row = lax.broadcasted_iota(jnp.int32, cur.shape, 0)
    prev_row = jnp.concatenate([raw_q[n - HALO:, :], raw_k[n - HALO:, :]], axis=1)[HALO - 1:HALO, :].astype(F32)
    xm = jnp.where(row == 0, prev_row * prev_on, pltpu.roll(cur, 1, 0))
    xp = jnp.where(row == n - 1, next_row_scr[0:1, :] * next_on, pltpu.roll(cur, n - 1, 0))
    conv = cw_ref[3:4, :] + cw_ref[0:1, :] * xm + cw_ref[1:2, :] * cur + cw_ref[2:3, :] * xp
    qk = conv * jax.nn.sigmoid(conv)
    qk_ref[:, 0:gw] = (qk[:, 0:gw] * (HEAD_DIM ** -0.5)).astype(BF16)
    qk_ref[:, gw:] = qk[:, gw:].astype(BF16)
    le_bf = (sub <= lane).astype(BF16)
    chunks = (1, 0)
    pre = {}
    for blk in chunks:
        rows = slice(blk * CHUNK, (blk + 1) * CHUNK)
        g = p_g[rows, :]
        bal = pltpu.roll(_cumsum_cols(le_bf, _log_sigmoid(g)), LANES - 8, 1)
        c_all = g - bal
        cmax = jnp.max(c_all, axis=0, keepdims=True)
        w_all = jnp.exp(c_all - cmax)
        vts, ks, nlocs = [], [], []
        for p in range(8):
            sl = slice((p % 4) * LANES, (p % 4 + 1) * LANES)
            if p < 4:
                v2 = p_rv[rows, sl]
                ks.append((p_rk[rows, sl].astype(F32) * kdec_scr[p]).astype(BF16))
            else:
                v2 = p_mv[rows, sl]
                h0 = 16 + 2 * (p - 4)
                kw = qk[rows, gw + (p - 4) * LANES:gw + (p - 3) * LANES] * jnp.where(
                    lo, w_all[:, h0:h0 + 1], w_all[:, h0 + 1:h0 + 2])
                ks.append(kw.astype(BF16))
                nlocs.append(jnp.sum(kw, axis=0, keepdims=True))
            vts.append(v2.astype(F32).T.astype(BF16))
        pre[blk] = (vts, ks, nlocs, cmax, bal[0:1, :])

    cur_rk = project(1)
    y_ref[:, 1 * gw:2 * gw] = cur_rk
    y_ref[:, 0:gw] = project(0)
    kvs = {blk: [_dot(pre[blk][0][p], pre[blk][1][p]) for p in range(8)] for blk in chunks}
    cur_rv = project(2)
    y_ref[:, 2 * gw:3 * gw] = cur_rv
    y_ref[:, 3 * gw:4 * gw] = project(3)

    for blk in chunks:
        _, _, nlocs, cmax, bend = pre[blk]
        sret_ref[blk] = s_scr[...].astype(BF16)
        cm_ref[blk] = c_scr[...].astype(BF16)
        nm_ref[blk] = n_scr[...].astype(BF16)
        mm_ref[blk] = m_scr[...]
        m_old = m_scr[0:1, :]
        mrel = jnp.maximum(m_old, cmax)
        a_row = jnp.exp(m_old - mrel)
        bb_row = jnp.exp(cmax - mrel)
        m_scr[0:1, :] = bend + mrel
        for p in range(4):
            cd = jnp.exp(lgk_ref[4 + p:5 + p, :] * float(CHUNK))
            s_scr[p] = s_scr[p] * cd + jnp.where(bd_ret, kvs[blk][p], 0.0)
            h0 = 16 + 2 * p
            a_l = jnp.where(lo_row, a_row[:, h0:h0 + 1], a_row[:, h0 + 1:h0 + 2])
            bb_l = jnp.where(lo_row, bb_row[:, h0:h0 + 1], bb_row[:, h0 + 1:h0 + 2])
            c_scr[p] = c_scr[p] * a_l + jnp.where(bd_m, kvs[blk][4 + p], 0.0) * bb_l
            n_new = (n_scr[p, 0:1, :] + n_scr[p, 1:2, :]) * a_l + nlocs[p] * bb_l
            n_scr[p, 0:1, :] = jnp.where(lo_row, n_new, 0.0)
            n_scr[p, 1:2, :] = jnp.where(lo_row, 0.0, n_new)

    cur_mv = project(6)
    y_ref[:, 4 * gw:5 * gw] = cur_mv
    y_ref[:, 5 * gw:6 * gw] = project(7)
    gates = _dot(hb, wg_ref[...]) + gb_ref[...]
    g_ref[...] = gates
    gt_ref[...] = _dot_nt(wgt_ref[...], hb) + gbt_ref[...]
    next_row_scr[...] = p_mqk[0:HALO, :].astype(F32)
    p_mqk[:, 0:gw] = raw_q
    p_mqk[:, gw:] = raw_k
    p_rk[...] = cur_rk
    p_rv[...] = cur_rv
    p_mv[...] = cur_mv
    p_g[...] = gates


def _inproj_sweep(ctx2, x2, ab, w, wg, wgt, gb, gbt, cos, sin, lgk, cw, nb, tpb):
    d = x2.shape[1]
    tm = ROW_TILE
    cpt = tm // CHUNK
    gw = 4 * LANES
    tiles = nb * tpb
    r = tiles * tm
    lat = tpb - 1
    def visit(t):
        b, v = _div_small(t, tpb, nb)
        return b, jnp.where(v == 0, 0, tpb - v)

    cur = lambda i: visit(jnp.minimum(i, tiles - 1))
    flat = lambda bj: bj[0] * tpb + bj[1]
    tile_i = lambda i: flat(cur(i))
    tile_a = lambda i: flat(visit(jnp.maximum(i - 1, 0)))
    in_batch = lambda i: cur(i)[1]
    sel = lambda i: (cur(i)[0] * 2 + jnp.minimum(cur(i)[1], 1), 0, 0)
    state = lambda *dims: pl.BlockSpec((cpt,) + dims, lambda i: (tile_a(i),) + (0,) * len(dims))
    nchunks = tiles * cpt
    return pl.pallas_call(
        functools.partial(_inproj_sweep_kernel, tiles=tiles, tpb=tpb),
        grid=(tiles + 1,),
        in_specs=[pl.BlockSpec((tm, d), lambda i: (cur(i)[0], 0)),
                  pl.BlockSpec((tm, d), lambda i: (cur(i)[0] * lat + jnp.maximum(cur(i)[1] - 1, 0), 0)),
                  pl.BlockSpec((None, 2, d), sel),
                  _const_spec(w.shape), _const_spec(wg.shape), _const_spec(wgt.shape),
                  _const_spec(gb.shape), _const_spec(gbt.shape),
                  pl.BlockSpec((tm, LANES), lambda i: (in_batch(i), 0)),
                  pl.BlockSpec((tm, LANES), lambda i: (in_batch(i), 0)),
                  _const_spec(lgk.shape), _const_spec(cw.shape)],
        out_specs=[pl.BlockSpec((tm, 6 * gw), lambda i: (tile_i(i), 0)),
                   pl.BlockSpec((tm, LANES), lambda i: (tile_i(i), 0)),
                   pl.BlockSpec((32, tm), lambda i: (0, tile_i(i))),
                   state(4, CHUNK, LANES), state(4, CHUNK, LANES), state(4, HALO, LANES), state(8, LANES),
                   pl.BlockSpec((tm, 2 * gw), lambda i: (tile_a(i), 0))],
        out_shape=[jax.ShapeDtypeStruct((r, 6 * gw), BF16),
                   jax.ShapeDtypeStruct((r, LANES), F32),
                   jax.ShapeDtypeStruct((32, r), F32),
                   jax.ShapeDtypeStruct((nchunks, 4, CHUNK, LANES), BF16),
                   jax.ShapeDtypeStruct((nchunks, 4, CHUNK, LANES), BF16),
                   jax.ShapeDtypeStruct((nchunks, 4, HALO, LANES), BF16),
                   jax.ShapeDtypeStruct((nchunks, 8, LANES), F32),
                   jax.ShapeDtypeStruct((r, 2 * gw), BF16)],
        scratch_shapes=[pltpu.VMEM((4, CHUNK, LANES), F32), pltpu.VMEM((4, CHUNK, LANES), F32),
                        pltpu.VMEM((4, HALO, LANES), F32), pltpu.VMEM((8, LANES), F32),
                        pltpu.VMEM((4, CHUNK, LANES), F32),
                        pltpu.VMEM((tm, gw), BF16), pltpu.VMEM((tm, gw), BF16), pltpu.VMEM((tm, gw), BF16),
                        pltpu.VMEM((tm, 2 * gw), BF16), pltpu.VMEM((tm, LANES), F32),
                        pltpu.VMEM((HALO, 2 * gw), F32)],
        compiler_params=_cparams("arbitrary"),
        name="inproj_bwd_sweep",
    )(ctx2, x2, ab, w, wg, wgt, gb, gbt, cos, sin, lgk, cw)


class _Bag:
    def __init__(self, **kw):
        self.__dict__.update(kw)


def _mixer_chunk_stages(blk, r):
    rows = slice(blk * CHUNK, (blk + 1) * CHUNK)
    lane = _lane()
    sub = lax.broadcasted_iota(jnp.int32, (CHUNK, LANES), 0)
    lo = lane < HEAD_DIM
    sub_lo = sub < HEAD_DIM
    mask_ret = [(lane & 32) == 0, (lane & 32) != 0]
    mask_nat = [lo, lane >= HEAD_DIM]
    bd_ret = (sub >= HEAD_DIM) == ((lane & 32) != 0)
    bd_m = (sub >= HEAD_DIM) == (lane >= HEAD_DIM)
    le = sub <= lane
    ge = sub >= lane

    g = r.g_ref[rows, :]
    gt = r.gt_ref[:, rows]
    lf_col = _log_sigmoid(g)
    lf_row = _log_sigmoid(gt)
    le_bf = le.astype(BF16)
    ge_bf = ge.astype(BF16)
    pre_col = _cumsum_cols(ge_bf, lf_col)
    bal_f = pltpu.roll(pre_col, LANES - 8, 1)
    bal_b = pltpu.roll(pre_col[CHUNK - 1:CHUNK, :] - pre_col + lf_col, LANES - 8, 1)
    cf_all = g - bal_f
    cb_all = g - bal_b
    bf_row = _cumsum_rows(lf_row, le_bf)
    bb_row = bf_row[:, CHUNK - 1:CHUNK] - bf_row + lf_row
    qb, kb, vts, kf32, qt, vbd = [], [], [], [], [], []
    for p in range(8):
        sl = slice((p % 4) * LANES, (p % 4 + 1) * LANES)
        if p < 4:
            q2, k2, v2 = r.rq_ref[rows, sl], r.rk_ref[rows, sl], r.rv_ref[rows, sl]
            qf = q2.astype(F32)
            kf = None
        else:
            q2 = r.qk_ref[rows, sl]
            k2 = r.qk_ref[rows, 4 * LANES + (p - 4) * LANES:4 * LANES + (p - 3) * LANES]
            v2 = r.mv_ref[rows, sl]
            qf, kf = q2.astype(F32), k2.astype(F32)
        vt = v2.astype(F32).T.astype(BF16)
        qb.append(q2)
        kb.append(k2)
        vts.append(vt)
        kf32.append(kf)
        qt.append(qf.T)
        vbd.append([jnp.where(sub_lo, vt, jnp.zeros_like(vt)), jnp.where(sub_lo, jnp.zeros_like(vt), vt)])
    yield

    st2 = []
    for p in range(8):
        masks = mask_ret if p < 4 else mask_nat
        zero = jnp.zeros_like(qb[p])
        qstack = jnp.concatenate([jnp.where(masks[0], qb[p], zero), jnp.where(masks[1], qb[p], zero)], axis=0)
        st2.append(_dot_nt(kb[p], qstack))
    yield

    qn = [_dot_nt(jnp.concatenate([r.n_scr[p].astype(BF16), r.nmb_ref[blk, p]], axis=0), qb[4 + p])
          for p in range(4)]
    qn_f = [x[0:HALO] for x in qn]
    qn_b = [x[HALO:2 * HALO] for x in qn]
    cf2, cb2, bf2, bb2 = cf_all * LOG2E, cb_all * LOG2E, bf_row * LOG2E, bb_row * LOG2E
    m_f2, m_b2 = r.m_state[0:1, :] * LOG2E, r.mmb_ref[blk, 0:1, :] * LOG2E
    lhs, rhs = [], []
    for p in range(8):
        if p < 4:
            pts = [(st2[p][:, a * LANES:(a + 1) * LANES] * r.dm_scr[2 * p + a]).astype(BF16) for a in range(2)]
            x_f, x_b = r.dec_scr[0, p], r.dec_scr[1, p]
            old = [r.s_scr[p].astype(BF16), r.sretb_ref[blk, p]]
        else:
            pts, cf, cb = [], [], []
            for a in range(2):
                h = 2 * (p - 4) + a
                st = st2[p][:, a * LANES:(a + 1) * LANES]
                pf, coef_f = _mlstm_dir_weights(st, qn_f[p - 4][a:a + 1, :], cf2[:, h:h + 1],
                                                bf2[8 + h:9 + h, :], m_f2[0:1, h:h + 1], le)
                pb, coef_b = _mlstm_dir_weights(st, qn_b[p - 4][a:a + 1, :], cb2[:, 16 + h:17 + h],
                                                bb2[24 + h:25 + h, :], m_b2[0:1, 16 + h:17 + h], ge)
                pts.append((pf + pb).astype(BF16))
                cf.append(coef_f)
                cb.append(coef_b)
            x_f = jnp.where(sub_lo, cf[0], cf[1])
            x_b = jnp.where(sub_lo, cb[0], cb[1])
            old = [r.c_scr[p - 4].astype(BF16), r.cmb_ref[blk, p - 4]]
        lhs.append(jnp.concatenate(vbd[p] + old, axis=1))
        rhs.append(jnp.concatenate(pts + [(qt[p] * x_f).astype(BF16), (qt[p] * x_b).astype(BF16)], axis=0))
    yield

    ht = [_dot(lhs[p], rhs[p]) for p in range(8)]
    yield

    for p in range(8):
        sl = slice((p % 4) * LANES, (p % 4 + 1) * LANES)
        gate_ref = r.rg_ref if p < 4 else r.mo_ref
        y = _heads_out(ht[p], r.nw_ref[p])
        new = (y * gate_ref[rows, sl].astype(F32)).astype(BF16)
        r.mix_scr[rows, p * LANES:(p + 1) * LANES] = jnp.where(r.live, new, r.mix_scr[rows, p * LANES:(p + 1) * LANES])
    yield

    for p in range(4):
        cd = jnp.exp(r.lgk_ref[p:p + 1, :] * float(CHUNK))
        _ret_state_update(r.s_scr, p, kb[p], vts[p], r.dec_scr[2, p], cd, bd_ret)
    _mlstm_state_update(r.c_scr, r.n_scr, r.m_state, kf32[4:], vts[4:], cf_all, bal_f[CHUNK - 1:CHUNK, :],
                        0, lo, bd_m)
    yield


def _attn_inproj_stages(x, ab_ref, w_ref, nw_ref, cos, sin, y_ref):
    hb = _norm_mod(x, ab_ref)
    r2 = lax.broadcasted_iota(jnp.int32, (2 * LANES, 2 * LANES), 0)
    c2 = lax.broadcasted_iota(jnp.int32, (2 * LANES, 2 * LANES), 1)
    same_head = (((r2 ^ c2) & (LANES | 32)) == 0).astype(BF16)
    acc_q = _dot(hb, w_ref[:, 0:8 * LANES])
    acc_k = _dot(hb, w_ref[:, 8 * LANES:10 * LANES])
    y_ref[:, 10 * LANES:12 * LANES] = _dot(hb, w_ref[:, 10 * LANES:12 * LANES]).astype(BF16)
    yield
    for j in range(5):
        acc = acc_q[:, j * 2 * LANES:(j + 1) * 2 * LANES] if j < 4 else acc_k
        sq = acc * acc
        ms = _dot(sq.astype(BF16), same_head) * (1.0 / HEAD_DIM)
        nrm = acc * lax.rsqrt(ms + EPS)
        nw = nw_ref[0:1, :] if j < 4 else nw_ref[1:2, :]
        for v in range(2):
            ls = slice(v * LANES, (v + 1) * LANES)
            y_ref[:, (2 * j + v) * LANES:(2 * j + v + 1) * LANES] = _rope(nrm[:, ls] * nw, cos, sin).astype(BF16)
        if j in (1, 4):
            yield


def _ffn_splits(d_ff, pieces):
    blocks = d_ff // (2 * LANES)
    assert blocks * 2 * LANES == d_ff and blocks >= pieces
    cuts = [((i * blocks) // pieces) * 2 * LANES for i in range(pieces)]
    return cuts + [d_ff]


def _mixer_ffn_kernel(lg_ref, y_ref, qk_ref,
                      g_ref, gt_ref, sretb_ref, cmb_ref, nmb_ref, mmb_ref,
                      lgk_ref, lgkt_ref, nw_ref,
                      ctx_ref, x_ref, p_ref, wo_ref, wi_ref, w2_ref,
                      ab2_ref, wa_ref, nwa_ref, cos_ref, sin_ref,
                      o_ref, y2_ref,
                      s_scr, c_scr, n_scr, m_state, dm_scr, dec_scr, mix_scr, act_scr, x2_scr,
                      *, tiles, tpb, d_ff):
    s = pl.program_id(0)
    nb = tiles // tpb
    jt = _div_small(jnp.minimum(s, tiles - 1), tpb, nb)[1]
    gw = 4 * LANES
    rq_ref, rk_ref, rv_ref, rg_ref, mv_ref, mo_ref = [y_ref.at[:, j * gw:(j + 1) * gw] for j in range(6)]
    lane = _lane()
    sub = lax.broadcasted_iota(jnp.int32, (CHUNK, LANES), 0)

    @pl.when(s == 0)
    def _():
        mix_scr[...] = jnp.zeros_like(mix_scr)
        x2_scr[...] = jnp.zeros_like(x2_scr)
        le = sub <= lane
        ge = sub >= lane
        spos = sub.astype(F32)
        tpos = lane.astype(F32)
        diff = (lane - sub).astype(F32)
        for h in range(8):
            dm_scr[h] = (jnp.where(le, jnp.exp(lg_ref[h] * diff), 0.0)
                         + jnp.where(ge, jnp.exp(lg_ref[8 + h] * (-diff)), 0.0))
        for p in range(4):
            dec_scr[0, p] = jnp.exp(lgkt_ref[:, p:p + 1] * (tpos + 1.0))
            dec_scr[1, p] = jnp.exp(lgkt_ref[:, 4 + p:5 + p] * (float(CHUNK) - tpos))
            dec_scr[2, p] = jnp.exp(lgk_ref[p:p + 1, :] * (float(CHUNK) - 1.0 - spos))

    @pl.when(jt == 0)
    def _():
        s_scr[...] = jnp.zeros_like(s_scr)
        c_scr[...] = jnp.zeros_like(c_scr)
        n_scr[...] = jnp.zeros_like(n_scr)
        m_state[...] = jnp.zeros_like(m_state)

    r = _Bag(rq_ref=rq_ref, rk_ref=rk_ref, rv_ref=rv_ref, rg_ref=rg_ref, qk_ref=qk_ref, mv_ref=mv_ref, mo_ref=mo_ref,
             g_ref=g_ref, gt_ref=gt_ref, sretb_ref=sretb_ref, cmb_ref=cmb_ref, nmb_ref=nmb_ref, mmb_ref=mmb_ref,
             lgk_ref=lgk_ref, nw_ref=nw_ref, s_scr=s_scr, c_scr=c_scr, n_scr=n_scr, m_state=m_state,
             dm_scr=dm_scr, dec_scr=dec_scr, mix_scr=mix_scr, live=s < tiles)
    cuts = _ffn_splits(d_ff, 4)
    ffn_piece = lambda i: _ffn_cols(h, wi_ref, act_scr, cuts[i], cuts[i + 1], d_ff)

    x = _ctx_or_latent_rows(ctx_ref, x_ref, jnp.clip(s - 1, 0, tiles - 1), nb, tpb)
    x1, h = _ffn_in(x, mix_scr[...], p_ref, wo_ref)

    nxt = _attn_inproj_stages(x2_scr[...], ab2_ref, wa_ref, nwa_ref, cos_ref[...], sin_ref[...], y2_ref)
    chunk_a, chunk_b = _mixer_chunk_stages(0, r), _mixer_chunk_stages(1, r)
    next(nxt)
    next(chunk_a), next(chunk_b)
    next(chunk_a), next(chunk_b)
    ffn_piece(0)
    next(chunk_a), next(chunk_a)
    ffn_piece(1)
    next(chunk_a), next(chunk_a)
    next(nxt)
    ffn_piece(2)
    ffn_piece(3)
    next(chunk_b), next(chunk_b)
    next(nxt)
    x2 = x1 + p_ref[3:4, :] * _dot(act_scr[...], w2_ref[...])
    o_ref[...] = x2
    next(chunk_b), next(chunk_b)
    x2_scr[...] = x2


def _mixer_ffn(y, g, gt, states, lg_smem, lgk, lgkt, nw, xs, prm, wo, wi_all, w2_all, layer,
               ab_next, w_next, nw_next, cos, sin, nb, tpb):
    gw = 4 * LANES
    tm = ROW_TILE
    cpt = tm // CHUNK
    sretb, cmb, nmb, mmb, qk_act = states
    tiles = nb * tpb
    d = xs[-1].shape[1]
    d_ff = w2_all.shape[1]
    n_next = w_next.shape[1]
    mix = lambda s: jnp.minimum(s, tiles - 1)
    ffn = lambda s: jnp.clip(s - 1, 0, tiles - 1)
    nxt = lambda s: jnp.maximum(s - 2, 0)
    state_spec = lambda a: pl.BlockSpec((cpt,) + a.shape[1:], lambda s: (mix(s),) + (0,) * (a.ndim - 1))
    def sel(t):
        b, j = _div_small(t, tpb, nb)
        return b * 2 + jnp.minimum(j, 1), 0, 0

    layer_spec = lambda a: pl.BlockSpec((None,) + a.shape[1:], lambda s: (layer, 0, 0),
                                        pipeline_mode=pl.Buffered(1))
    return pl.pallas_call(
        functools.partial(_mixer_ffn_kernel, tiles=tiles, tpb=tpb, d_ff=d_ff),
        grid=(tiles + 2,),
        in_specs=[_smem_spec(),
                  pl.BlockSpec((tm, 6 * gw), lambda s: (mix(s), 0)),
                  pl.BlockSpec((tm, 2 * gw), lambda s: (mix(s), 0)),
                  pl.BlockSpec((tm, LANES), lambda s: (mix(s), 0)),
                  pl.BlockSpec((32, tm), lambda s: (0, mix(s))),
                  state_spec(sretb), state_spec(cmb), state_spec(nmb), state_spec(mmb),
                  _const_spec(lgk.shape), _const_spec(lgkt.shape), _const_spec(nw.shape)]
                 + _split_row_specs(tm, d, nb, tpb, ffn)
                 + [pl.BlockSpec((None, 8, d), lambda s: sel(ffn(s))),
                    _const_spec(wo.shape), layer_spec(wi_all), layer_spec(w2_all),
                    pl.BlockSpec((None, 2, d), lambda s: sel(nxt(s))),
                    _const_spec(w_next.shape), _const_spec(nw_next.shape),
                    pl.BlockSpec((tm, LANES), lambda s: (_div_small(nxt(s), tpb, nb)[1], 0)),
                    pl.BlockSpec((tm, LANES), lambda s: (_div_small(nxt(s), tpb, nb)[1], 0))],
        out_specs=[pl.BlockSpec((tm, d), lambda s: (ffn(s), 0)),
                   pl.BlockSpec((tm, n_next), lambda s: (nxt(s), 0))],
        out_shape=[jax.ShapeDtypeStruct((tiles * tm, d), F32),
                   jax.ShapeDtypeStruct((tiles * tm, n_next), BF16)],
        scratch_shapes=[pltpu.VMEM((4, CHUNK, LANES), F32), pltpu.VMEM((4, CHUNK, LANES), F32),
                        pltpu.VMEM((4, HALO, LANES), F32), pltpu.VMEM((8, LANES), F32),
                        pltpu.VMEM((8, CHUNK, LANES), F32), pltpu.VMEM((3, 4, CHUNK, LANES), F32),
                        pltpu.VMEM((tm, 2 * gw), BF16), pltpu.VMEM((tm, d_ff), BF16),
                        pltpu.VMEM((tm, d), F32)],
        compiler_params=_cparams("arbitrary"),
        name="ret_mlstm_mixer_ffn",
    )(lg_smem, y, qk_act, g, gt, sretb, cmb, nmb, mmb, lgk, lgkt, nw,
      *xs, prm, wo, wi_all, w2_all, ab_next, w_next, nw_next, cos, sin)


def _ffn_in(x, m, p_ref, wo_ref):
    x1 = x + p_ref[0:1, :] * _dot(m, wo_ref[...])
    ms = jnp.mean(x1 * x1, axis=-1, keepdims=True)
    return x1, ((x1 * lax.rsqrt(ms + EPS)) * p_ref[1:2, :] + p_ref[2:3, :]).astype(BF16)


def _ffn_cols(h, wi_ref, act_scr, lo, hi, d_ff):
    gate = _dot(h, wi_ref[:, lo:hi])
    up = _dot(h, wi_ref[:, d_ff + lo:d_ff + hi])
    act_scr[:, lo:hi] = (gate * jax.nn.sigmoid(gate) * up).astype(BF16)


def _window_bias():
    kk = np.arange(CHUNK)[:, None]
    t = np.arange(CHUNK)[None, :]
    tabs = []
    for has_prev, has_next in ((False, True), (True, True), (True, False)):
        prev_ok = (kk >= t) & has_prev
        next_ok = (kk <= t) & has_next
        tabs.append(np.where(np.concatenate([prev_ok, next_ok], axis=0), 0.0, NEG))
    return jnp.asarray(np.stack(tabs), F32)


def _attn_ffn_kernel(sink_ref, qkv_a_ref, qkv_b_ref, kvp_ref, kvn_ref, kvx_ref, bias_ref,
                     x_a_ref, x_b_ref, p_ref, wo_ref, wi_ref, w2_ref, o_ref, m_scr, act_scr,
                     *, steps, per_batch, d_ff):
    s = pl.program_id(0)
    kw = 2 * LANES
    k0 = 8 * LANES
    tm = qkv_a_ref.shape[0]
    nblk = 2 * tm // CHUNK

    @pl.when(s == 0)
    def _():
        m_scr[...] = jnp.zeros_like(m_scr)

    j = _div_small(jnp.minimum(s, steps - 1), per_batch, steps // per_batch)[1]
    grp = lax.broadcasted_iota(jnp.int32, (1, 4 * CHUNK), 1) // CHUNK
    lane = _lane()
    mask_q = [(lane & 32) == 0, (lane & 32) != 0]
    tile4 = lambda b: jnp.concatenate([b] * 4, axis=1)
    inner = tile4(bias_ref[1])
    biases = ([tile4(jnp.where(j == 0, bias_ref[0], bias_ref[1]))] + [inner] * (nblk - 2)
              + [tile4(jnp.where(j == per_batch - 1, bias_ref[2], bias_ref[1]))])
    cuts = _ffn_splits(d_ff, nblk - 1)

    def ffn_up(i):
        _ffn_cols(h, wi_ref, act_scr, cuts[i], cuts[i + 1], d_ff)

    def q_rows(blk):
        ref = qkv_a_ref if blk * CHUNK < tm else qkv_b_ref
        r0 = (blk * CHUNK) % tm
        return ref, slice(r0, r0 + CHUNK)

    def scores(blk, kv):
        kvp, a = kv // 2, kv % 2
        ref, rows = q_rows(blk)
        qs = jnp.concatenate(
            [jnp.where(mask_q[a], ref[rows, (kvp * 4 + g) * LANES:(kvp * 4 + g + 1) * LANES],
                       jnp.zeros((CHUNK, LANES), BF16)) for g in range(4)], axis=0)
        return _dot_nt(kcats[kvp][blk], qs)

    def softmax_pv(blk, kv, st):
        bias = biases[blk]
        st = jnp.concatenate([st[0:CHUNK] + bias[0:CHUNK], st[CHUNK:2 * CHUNK],
                              st[2 * CHUNK:3 * CHUNK] + bias[CHUNK:2 * CHUNK], st[3 * CHUNK:]], axis=0)
        snk = jnp.where(grp == 0, sink_ref[kv * 4],
                        jnp.where(grp == 1, sink_ref[kv * 4 + 1],
                                  jnp.where(grp == 2, sink_ref[kv * 4 + 2], sink_ref[kv * 4 + 3])))
        m = jnp.maximum(jnp.max(st, axis=0, keepdims=True), snk)
        e = jnp.exp2(st - m)
        denom = jnp.exp2(snk - m) + jnp.sum(e, axis=0, keepdims=True)
        a = kv % 2
        return _dot(vts[kv // 2][blk][a * HEAD_DIM:(a + 1) * HEAD_DIM, :], e.astype(BF16)) * (1.0 / denom)

    def hand_over(blk, outs):
        for kvp in range(2):
            full = jnp.concatenate(outs[2 * kvp:2 * kvp + 2], axis=0)
            for g in range(4):
                m_scr[blk * CHUNK:(blk + 1) * CHUNK, (kvp * 4 + g) * LANES:(kvp * 4 + g + 1) * LANES] = (
                    full[:, g * CHUNK:(g + 1) * CHUNK].T.astype(BF16))

    x1, h = _ffn_in(jnp.concatenate([x_a_ref[...], x_b_ref[...]], axis=0), m_scr[...], p_ref, wo_ref)

    kcats, vts = [], []
    for kvp in range(2):
        ks = slice(k0 + kvp * LANES, k0 + (kvp + 1) * LANES)
        vs = slice(k0 + kw + kvp * LANES, k0 + kw + (kvp + 1) * LANES)
        es, ev = slice(kvp * LANES, (kvp + 1) * LANES), slice(kw + kvp * LANES, kw + (kvp + 1) * LANES)
        tile_rows = [(ref, slice(c * CHUNK, (c + 1) * CHUNK)) for ref in (qkv_a_ref, qkv_b_ref)
                     for c in range(tm // CHUNK)]
        k_chunks = [kvp_ref[:, es]] + [ref[rows, ks] for ref, rows in tile_rows] + [kvn_ref[:, es]]
        v_chunks = ([kvp_ref[:, ev]] + [ref[rows, vs] for ref, rows in tile_rows] + [kvn_ref[:, ev]]
                    + [kvx_ref[c * CHUNK:(c + 1) * CHUNK, ev] for c in range(kvx_ref.shape[0] // CHUNK)])
        v_t = [v.astype(F32).T.astype(BF16) for v in v_chunks]
        kcats.append([jnp.concatenate(k_chunks[b:b + 3] + [kvx_ref[:, es]], axis=0) for b in range(nblk)])
        vts.append([jnp.concatenate(v_t[b:b + 3] + v_t[nblk + 2:], axis=1) for b in range(nblk)])

    for blk in range(nblk):
        sts = [scores(blk, kv) for kv in range(4)]
        if blk < nblk - 1:
            ffn_up(blk)
        else:
            o_ref[...] = x1 + p_ref[3:4, :] * _dot(act_scr[...], w2_ref[...])
        hand_over(blk, [softmax_pv(blk, kv, sts[kv]) for kv in range(4)])


def _attn_ffn(y, sink, xc, prm, wo, wi_all, w2_all, layer, nb, nc, ctx_chunks):
    tm = ROW_TILE
    cpt = tm // CHUNK
    tpb = nc // cpt
    lat_tiles = (nc - ctx_chunks) // cpt
    ctx_tiles = ctx_chunks // cpt
    assert ctx_tiles == 1 and lat_tiles % 2 == 0 and lat_tiles >= 4
    per_batch = lat_tiles // 2
    steps = nb * per_batch
    d = xc.shape[1]
    d_ff = w2_all.shape[1]
    bias = _window_bias()
    n_qkv = y.shape[1]
    kv_w = n_qkv - d
    assert d % kv_w == 0
    kv_col = d // kv_w
    att = lambda s: jnp.minimum(s, steps - 1)
    ffn = lambda s: jnp.maximum(s - 1, 0)
    split = lambda t: _div_small(t, per_batch, nb)
    row_tile = lambda t, u: split(t)[0] * tpb + ctx_tiles + 2 * split(t)[1] + u
    chunk0 = lambda t: split(t)[0] * nc + ctx_chunks
    prev_c = lambda s: chunk0(att(s)) + jnp.maximum(split(att(s))[1] * 2 * cpt - 1, 0)
    next_c = lambda s: chunk0(att(s)) + jnp.minimum(split(att(s))[1] * 2 * cpt + 2 * cpt, lat_tiles * cpt - 1)
    edge_spec = lambda f: pl.BlockSpec((CHUNK, kv_w), lambda s: (f(s), kv_col))
    layer_spec = lambda a: pl.BlockSpec((None,) + a.shape[1:], lambda s: (layer, 0, 0),
                                        pipeline_mode=pl.Buffered(1))
    return pl.pallas_call(
        functools.partial(_attn_ffn_kernel, steps=steps, per_batch=per_batch, d_ff=d_ff),
        grid=(steps + 1,),
        in_specs=[_smem_spec(),
                  pl.BlockSpec((tm, n_qkv), lambda s: (row_tile(att(s), 0), 0)),
                  pl.BlockSpec((tm, n_qkv), lambda s: (row_tile(att(s), 1), 0)),
                  edge_spec(prev_c), edge_spec(next_c),
                  pl.BlockSpec((tm, kv_w), lambda s: (split(att(s))[0] * tpb, kv_col)),
                  _const_spec(bias.shape),
                  pl.BlockSpec((tm, d), lambda s: (row_tile(ffn(s), 0), 0)),
                  pl.BlockSpec((tm, d), lambda s: (row_tile(ffn(s), 1), 0)),
                  pl.BlockSpec((None, 8, d), lambda s: (split(ffn(s))[0] * 2 + 1, 0, 0)),
                  _const_spec(wo.shape), layer_spec(wi_all), layer_spec(w2_all)],
        out_specs=pl.BlockSpec((2 * tm, d), lambda s: (ffn(s), 0)),
        out_shape=jax.ShapeDtypeStruct((steps * 2 * tm, d), F32),
        scratch_shapes=[pltpu.VMEM((2 * tm, d), BF16), pltpu.VMEM((2 * tm, d_ff), BF16)],
        compiler_params=_cparams("arbitrary"),
        name="window_gqa_ffn",
    )(sink, y, y, y, y, y, bias, xc, xc, prm, wo, wi_all, w2_all)


def _pair_cols(w):
    rows, cols = w.shape
    return w.reshape(rows, cols // LANES, 2, 2, 32).transpose(0, 1, 3, 2, 4).reshape(rows, cols)


def _attn_q_cols(w):
    rows = w.shape[0]
    g_per = w.shape[1] // (H_KV * HEAD_DIM)
    return (w.reshape(rows, H_KV // 2, 2, g_per, 2, 32).transpose(0, 1, 3, 4, 2, 5)
            .reshape(rows, w.shape[1]))


def _attn_o_rows(w):
    cols = w.shape[1]
    g_per = w.shape[0] // (H_KV * HEAD_DIM)
    return (w.reshape(H_KV // 2, 2, g_per, HEAD_DIM, cols).transpose(0, 2, 1, 3, 4)
            .reshape(w.shape[0], cols))


def _rope_tables(seq, ctx_len):
    rows = seq // GRID_W
    row = np.repeat(np.arange(rows, dtype=np.float32), GRID_W)
    col = np.tile(np.arange(GRID_W, dtype=np.float32), rows)
    n = HEAD_DIM // 4
    inv = (np.float32(ROPE_BASE) ** (-np.arange(n, dtype=np.float32) / np.float32(n))).astype(np.float32)
    ang = np.concatenate([row[:, None] * inv, col[:, None] * inv], axis=-1).astype(np.float32)
    cos, sin = np.cos(ang), np.sin(ang)
    cos_t = np.concatenate([np.ones((ctx_len, LANES), np.float32), np.tile(cos, (1, 4))], axis=0)
    sin_t = np.concatenate([np.zeros((ctx_len, LANES), np.float32),
                            np.concatenate([-sin, -sin, sin, sin], axis=-1)], axis=0)
    return jnp.asarray(cos_t, F32), jnp.asarray(sin_t, F32)


def _mod_tables(mod, nb, norm_w):
    d = norm_w.shape[-1]
    lat = mod[:nb].reshape(nb, 6, d)
    ctx = jnp.broadcast_to(mod[nb].reshape(1, 6, d), (nb, 6, d))
    both = jnp.stack([ctx, lat], axis=1).reshape(nb * 2, 6, d)
    sh1, sc1, g1, sh2, sc2, g2 = [both[:, k] for k in range(6)]
    ab1 = jnp.stack([norm_w[0] * (1.0 + sc1), sh1], axis=1)
    zeros = jnp.zeros_like(g1)
    prm = jnp.stack([g1, norm_w[1] * (1.0 + sc2), sh2, g2, zeros, zeros, zeros, zeros], axis=1)
    return ab1, prm


def kernel(x, c, ctx, c_ctx, ada_w, ada_b, norm_w, ffn_w_in, ffn_w_out, ab_w_in, ab_w_out,
           ret_log_gamma, ret_norm_w, mlstm_conv_w, mlstm_conv_b, mlstm_gate_b, mlstm_norm_w,
           attn_w_in, attn_w_out, attn_q_norm_w, attn_k_norm_w, attn_sink):
    nb, seq, d = x.shape
    ctx_len = ctx.shape[1]
    depth = ada_w.shape[0]
    assert ctx_len == ROW_TILE and seq % ROW_TILE == 0 and d == 8 * LANES and nb < 8
    t_all = ctx_len + seq
    nc = t_all // CHUNK
    ctx_chunks = ctx_len // CHUNK
    tpb = t_all // ROW_TILE
    dr = d // 2

    rows = jnp.zeros((8, d), F32).at[:nb].set(c).at[nb].set(c_ctx)
    mod_all = _modulation(rows, ada_w, ada_b)
    cos_t, sin_t = _rope_tables(seq, ctx_len)
    wi_all = ffn_w_in.astype(BF16)
    w2_all = ffn_w_out.astype(BF16)
    xs = (ctx.reshape(nb * ctx_len, d), x.reshape(nb * seq, d))

    assert depth == 2 and ab_w_in.shape[0] == 1 and attn_w_in.shape[0] == 1
    ab_0, prm_0 = _mod_tables(mod_all[0], nb, norm_w[0])
    ab_1, prm_1 = _mod_tables(mod_all[1], nb, norm_w[1])

    w = ab_w_in[0]
    w_main = jnp.concatenate([_pair_cols(w[:, :dr]), _pair_cols(w[:, dr:2 * dr]), w[:, 2 * dr:8 * dr]],
                             axis=1).astype(BF16)
    wg = jnp.zeros((d, LANES), F32).at[:, :32].set(w[:, 8 * dr:]).astype(BF16)
    wgt = w[:, 8 * dr:].T.astype(BF16)
    gb = jnp.zeros((1, LANES), F32).at[0, :32].set(mlstm_gate_b[0].reshape(-1))
    gbt = mlstm_gate_b[0].reshape(32, 1)
    lg = ret_log_gamma[0].astype(F32)
    lgk = jnp.tile(jnp.repeat(lg.reshape(2, 4, 2), 32, axis=-1), (1, 1, 2)).reshape(8, LANES)
    cw = jnp.concatenate([mlstm_conv_w[0], mlstm_conv_b[0][None], jnp.zeros((4, 2 * dr), F32)], axis=0)
    nw = jnp.broadcast_to(jnp.concatenate([ret_norm_w[0], mlstm_norm_w[0]]).reshape(8, LANES, 1),
                          (8, LANES, LANES))
    wo_0 = ab_w_out[0].astype(BF16)

    w = attn_w_in[0]
    w_attn = jnp.concatenate([_attn_q_cols(w[:, :d]), _pair_cols(w[:, d:d + 2 * LANES]), w[:, d + 2 * LANES:]],
                             axis=1).astype(BF16)
    lane_w = lambda v: jnp.concatenate([v[:32], v[:32], v[32:], v[32:]])
    nwq = jnp.stack([lane_w(attn_q_norm_w[0]) * (HEAD_DIM ** -0.5 * LOG2E), lane_w(attn_k_norm_w[0])]
                    + [jnp.zeros((LANES,), F32)] * 6)
    wo_1 = _attn_o_rows(attn_w_out[0]).astype(BF16)

    y, g, gt, *states = _inproj_sweep(*xs, ab_0, w_main, wg, wgt, gb, gbt, cos_t, sin_t, lgk, cw, nb, tpb)
    x_mid, y_attn = _mixer_ffn(y, g, gt, states, lg.reshape(-1), lgk, lgk.T, nw, xs, prm_0, wo_0, wi_all, w2_all, 0,
                               ab_1, w_attn, nwq, cos_t, sin_t, nb, tpb)
    out = _attn_ffn(y_attn, attn_sink[0].astype(F32) * LOG2E, x_mid, prm_1, wo_1, wi_all, w2_all, 1, nb, nc, ctx_chunks)
    return out.reshape(nb, seq, d)
```

```python
import functools

import numpy as np
import jax
import jax.numpy as jnp
from jax import lax
from jax.experimental import pallas as pl
from jax.experimental.pallas import tpu as pltpu

F32 = jnp.float32
BF16 = jnp.bfloat16

HEAD_DIM = 64
CHUNK = 128
GRID_W = 64
ROPE_BASE = 10000.0
EPS = 1e-6
H_KV = 4
LANES = 128
ROW_TILE = 256
HALO = 16
LOG2E = 1.4426950408889634
NEG = -1e30
VMEM_LIMIT = 56 * 1024 * 1024


def _cparams(*sem):
    return pltpu.CompilerParams(dimension_semantics=sem, vmem_limit_bytes=VMEM_LIMIT)


def _const_spec(shape):
    nd = len(shape)
    return pl.BlockSpec(shape, lambda *_: (0,) * nd, pipeline_mode=pl.Buffered(1))


def _smem_spec():
    return pl.BlockSpec(memory_space=pltpu.SMEM)


def _lane(shape=(CHUNK, LANES)):
    return lax.broadcasted_iota(jnp.int32, shape, len(shape) - 1)


def _dot(a, b):
    return jnp.dot(a, b, preferred_element_type=F32)


def _dot_nt(a, b):
    return lax.dot_general(a, b, (((1,), (1,)), ((), ())), preferred_element_type=F32)


def _div_small(t, m, n):
    q = 0
    for b in range(1, n):
        q = q + jnp.where(t >= b * m, 1, 0)
    return q, t - q * m


def _split3(x):
    hi = x.astype(BF16)
    r = x - hi.astype(F32)
    mid = r.astype(BF16)
    lo = (r - mid.astype(F32)).astype(BF16)
    return hi, mid, lo


def _log_sigmoid(x):
    return jnp.minimum(x, 0.0) - jnp.log1p(jnp.exp(-jnp.abs(x)))


def _rope(x, cos, sin_signed):
    return x * cos + pltpu.roll(x, LANES // 2, 1) * sin_signed


def _mod_kernel(rows_ref, w_ref, b_ref, o_ref):
    a = rows_ref[...]
    a = a * jax.nn.sigmoid(a)
    a_hi = a.astype(BF16)
    a_lo = (a - a_hi.astype(F32)).astype(BF16)
    w = w_ref[...]
    w_hi = w.astype(BF16)
    w_lo = (w - w_hi.astype(F32)).astype(BF16)
    o_ref[...] = _dot(a_hi, w_hi) + _dot(a_hi, w_lo) + _dot(a_lo, w_hi) + b_ref[...]


def _modulation(rows, ada_w, ada_b):
    depth, d, n = ada_w.shape
    tn = n // 4
    return pl.pallas_call(
        _mod_kernel,
        grid=(depth, n // tn),
        in_specs=[pl.BlockSpec((8, d), lambda l, j: (0, 0)),
                  pl.BlockSpec((None, d, tn), lambda l, j: (l, 0, j)),
                  pl.BlockSpec((None, 1, tn), lambda l, j: (l, 0, j))],
        out_specs=pl.BlockSpec((None, 8, tn), lambda l, j: (l, 0, j)),
        out_shape=jax.ShapeDtypeStruct((depth, 8, n), F32),
        compiler_params=_cparams("arbitrary", "arbitrary"),
        name="adaln_modulation",
    )(rows, ada_w, ada_b.reshape(depth, 1, n))


def _norm_mod(x, ab_ref):
    ms = jnp.mean(x * x, axis=-1, keepdims=True)
    h = (x * lax.rsqrt(ms + EPS)) * ab_ref[0:1, :] + ab_ref[1:2, :]
    return h.astype(BF16)


def _ctx_or_latent_rows(ctx_ref, x_ref, tile, nb, tiles_per_batch):
    return jnp.where(_div_small(tile, tiles_per_batch, nb)[1] == 0, ctx_ref[...], x_ref[...])


def _split_row_specs(tm, d, nb, tpb, tile_of):
    lat = tpb - 1

    def latent_row(i):
        b, j = _div_small(tile_of(i), tpb, nb)
        return b * lat + jnp.maximum(j - 1, 0), 0

    return [pl.BlockSpec((tm, d), lambda i: (_div_small(tile_of(i), tpb, nb)[0], 0)),
            pl.BlockSpec((tm, d), latent_row)]


def _cumsum_cols(tri_bf, lf):
    hi, mid, lo = _split3(lf)
    return _dot(tri_bf, hi) + _dot(tri_bf, mid) + _dot(tri_bf, lo)


def _cumsum_rows(lf, tri_bf):
    hi, mid, lo = _split3(lf)
    return _dot(hi, tri_bf) + _dot(mid, tri_bf) + _dot(lo, tri_bf)


def _ret_state_update(s_ref, p, k2, vt, kdec, cd_lanes, bd):
    kf = (k2.astype(F32) * kdec).astype(BF16)
    s_ref[p] = s_ref[p] * cd_lanes + jnp.where(bd, _dot(vt, kf), 0.0)


def _mlstm_state_update(c_ref, n_ref, m_ref, k_pairs, vt_pairs, c_all, bend, col0, lo, bd):
    cmax = jnp.max(c_all, axis=0, keepdims=True)
    w_all = jnp.exp(c_all - cmax)
    m_old = m_ref[0:1, :]
    mrel = jnp.maximum(m_old, cmax)
    a_row = jnp.exp(m_old - mrel)
    bb_row = jnp.exp(cmax - mrel)
    m_ref[0:1, :] = bend + mrel
    lo_row = lo[0:1, :]
    for p in range(4):
        h0 = col0 + 2 * p
        kw = k_pairs[p] * jnp.where(lo, w_all[:, h0:h0 + 1], w_all[:, h0 + 1:h0 + 2])
        kvt = _dot(vt_pairs[p], kw.astype(BF16))
        nloc = jnp.sum(kw, axis=0, keepdims=True)
        a_l = jnp.where(lo_row, a_row[:, h0:h0 + 1], a_row[:, h0 + 1:h0 + 2])
        bb_l = jnp.where(lo_row, bb_row[:, h0:h0 + 1], bb_row[:, h0 + 1:h0 + 2])
        c_ref[p] = c_ref[p] * a_l + jnp.where(bd, kvt, 0.0) * bb_l
        n_new = (n_ref[p, 0:1, :] + n_ref[p, 1:2, :]) * a_l + nloc * bb_l
        n_ref[p, 0:1, :] = jnp.where(lo_row, n_new, 0.0)
        n_ref[p, 1:2, :] = jnp.where(lo_row, 0.0, n_new)


def _mlstm_dir_weights(st, qn_row, c_col, bt_row, m_prev, tri):
    dl = jnp.where(tri, c_col + bt_row, NEG)
    mx = jnp.max(dl, axis=0, keepdims=True)
    al = bt_row + m_prev
    m_t = jnp.maximum(al, mx)
    w = jnp.exp2(dl - m_t)
    a_t = jnp.exp2(al - m_t)
    sw = st * w
    den = jnp.sum(sw, axis=0, keepdims=True) + a_t * qn_row
    r = 1.0 / jnp.maximum(jnp.abs(den), jnp.exp2(-m_t))
    return sw * r, a_t * r


def _heads_out(ht, nw_tab):
    rows = []
    for a in range(2):
        ha = ht[a * HEAD_DIM:(a + 1) * HEAD_DIM, :]
        ms = jnp.mean(ha * ha, axis=0, keepdims=True)
        rows.append(ha * lax.rsqrt(ms + EPS))
    return (jnp.concatenate(rows, axis=0) * nw_tab).T


def _inproj_sweep_kernel(ctx_ref, x_ref, ab_ref, w_ref, wg_ref, wgt_ref, gb_ref, gbt_ref, cos_ref, sin_ref,
                         lgk_ref, cw_ref,
                         y_ref, g_ref, gt_ref, sret_ref, cm_ref, nm_ref, mm_ref, qk_ref,
                         s_scr, c_scr, n_scr, m_scr, kdec_scr,
                         p_rk, p_rv, p_mv, p_mqk, p_g, next_row_scr, *, tiles, tpb):
    i = pl.program_id(0)

    def order(t):
        v = _div_small(t, tpb, tiles // tpb)[1]
        return jnp.where(v == 0, 0, tpb - v)

    jt_a = order(jnp.maximum(i - 1, 0))
    lane = _lane()
    sub = lax.broadcasted_iota(jnp.int32, (CHUNK, LANES), 0)
    lo = lane < HEAD_DIM
    lo_row = lo[0:1, :]
    bd_ret = (sub >= HEAD_DIM) == ((lane & 32) != 0)
    bd_m = (sub >= HEAD_DIM) == (lane >= HEAD_DIM)
    gw = 4 * LANES

    @pl.when(i == 0)
    def _():
        pos = sub.astype(F32)
        for p in range(4):
            kdec_scr[p] = jnp.exp(lgk_ref[4 + p:5 + p, :] * pos)
        for ref in (p_rk, p_rv, p_mv, p_mqk, p_g, next_row_scr):
            ref[...] = jnp.zeros_like(ref)

    @pl.when(jt_a == 0)
    def _():
        s_scr[...] = jnp.zeros_like(s_scr)
        c_scr[...] = jnp.zeros_like(c_scr)
        n_scr[...] = jnp.zeros_like(n_scr)
        m_scr[...] = jnp.zeros_like(m_scr)

    tile_i = jnp.minimum(i, tiles - 1)
    jt_i = order(tile_i)
    hb = _norm_mod(jnp.where(jt_i == 0, ctx_ref[...], x_ref[...]), ab_ref)
    cos, sin = cos_ref[...], sin_ref[...]

    def project(j):
        acc = _dot(hb, w_ref[:, j * gw:(j + 1) * gw])
        if j in (0, 1):
            if j == 0:
                acc = acc * (HEAD_DIM ** -0.5)
            acc = jnp.concatenate([_rope(acc[:, p * LANES:(p + 1) * LANES], cos, sin) for p in range(4)], axis=1)
        elif j == 3:
            acc = acc * jax.nn.sigmoid(acc)
        elif j == 7:
            acc = jax.nn.sigmoid(acc)
        return acc.astype(BF16)

    raw_q, raw_k = project(4), project(5)

    prev_on = jnp.where(jt_a <= 1, 0.0, 1.0).astype(F32)
    next_on = jnp.where((jt_a == 0) | (jt_a == tpb - 1), 0.0, 1.0).astype(F32)
    cur = p_mqk[...].astype(F32)
    n = cur.shape[0]
    row = lax.broadcasted_iota(jnp.int32, cur.shape, 0)
    prev_row = jnp.concatenate([raw_q[n - HALO:, :], raw_k[n - HALO:, :]], axis=1)[HALO - 1:HALO, :].astype(F32)
    xm = jnp.where(row == 0, prev_row * prev_on, pltpu.roll(cur, 1, 0))
    xp = jnp.where(row == n - 1, next_row_scr[0:1, :] * next_on, pltpu.roll(cur, n - 1, 0))
    conv = cw_ref[3:4, :] + cw_ref[0:1, :] * xm + cw_ref[1:2, :] * cur + cw_ref[2:3, :] * xp
    qk = conv * jax.nn.sigmoid(conv)
    qk_ref[:, 0:gw] = (qk[:, 0:gw] * (HEAD_DIM ** -0.5)).astype(BF16)
    qk_ref[:, gw:] = qk[:, gw:].astype(BF16)
    le_bf = (sub <= lane).astype(BF16)
    chunks = (1, 0)
    pre = {}
    for blk in chunks:
        rows = slice(blk * CHUNK, (blk + 1) * CHUNK)
        g = p_g[rows, :]
        bal = pltpu.roll(_cumsum_cols(le_bf, _log_sigmoid(g)), LANES - 8, 1)
        c_all = g - bal
        cmax = jnp.max(c_all, axis=0, keepdims=True)
        w_all = jnp.exp(c_all - cmax)
        vts, ks, nlocs = [], [], []
        for p in range(8):
            sl = slice((p % 4) * LANES, (p % 4 + 1) * LANES)
            if p < 4:
                v2 = p_rv[rows, sl]
                ks.append((p_rk[rows, sl].astype(F32) * kdec_scr[p]).astype(BF16))
            else:
                v2 = p_mv[rows, sl]
                h0 = 16 + 2 * (p - 4)
                kw = qk[rows, gw + (p - 4) * LANES:gw + (p - 3) * LANES] * jnp.where(
                    lo, w_all[:, h0:h0 + 1], w_all[:, h0 + 1:h0 + 2])
                ks.append(kw.astype(BF16))
                nlocs.append(jnp.sum(kw, axis=0, keepdims=True))
            vts.append(v2.astype(F32).T.astype(BF16))
        pre[blk] = (vts, ks, nlocs, cmax, bal[0:1, :])

    cur_rk = project(1)
    y_ref[:, 1 * gw:2 * gw] = cur_rk
    y_ref[:, 0:gw] = project(0)
    kvs = {blk: [_dot(pre[blk][0][p], pre[blk][1][p]) for p in range(8)] for blk in chunks}
    cur_rv = project(2)
    y_ref[:, 2 * gw:3 * gw] = cur_rv
    y_ref[:, 3 * gw:4 * gw] = project(3)

    for blk in chunks:
        _, _, nlocs, cmax, bend = pre[blk]
        sret_ref[blk] = s_scr[...].astype(BF16)
        cm_ref[blk] = c_scr[...].astype(BF16)
        nm_ref[blk] = n_scr[...].astype(BF16)
        mm_ref[blk] = m_scr[...]
        m_old = m_scr[0:1, :]
        mrel = jnp.maximum(m_old, cmax)
        a_row = jnp.exp(m_old - mrel)
        bb_row = jnp.exp(cmax - mrel)
        m_scr[0:1, :] = bend + mrel
        for p in range(4):
            cd = jnp.exp(lgk_ref[4 + p:5 + p, :] * float(CHUNK))
            s_scr[p] = s_scr[p] * cd + jnp.where(bd_ret, kvs[blk][p], 0.0)
            h0 = 16 + 2 * p
            a_l = jnp.where(lo_row, a_row[:, h0:h0 + 1], a_row[:, h0 + 1:h0 + 2])
            bb_l = jnp.where(lo_row, bb_row[:, h0:h0 + 1], bb_row[:, h0 + 1:h0 + 2])
            c_scr[p] = c_scr[p] * a_l + jnp.where(bd_m, kvs[blk][4 + p], 0.0) * bb_l
            n_new = (n_scr[p, 0:1, :] + n_scr[p, 1:2, :]) * a_l + nlocs[p] * bb_l
            n_scr[p, 0:1, :] = jnp.where(lo_row, n_new, 0.0)
            n_scr[p, 1:2, :] = jnp.where(lo_row, 0.0, n_new)

    cur_mv = project(6)
    y_ref[:, 4 * gw:5 * gw] = cur_mv
    y_ref[:, 5 * gw:6 * gw] = project(7)
    gates = _dot(hb, wg_ref[...]) + gb_ref[...]
    g_ref[...] = gates
    gt_ref[...] = _dot_nt(wgt_ref[...], hb) + gbt_ref[...]
    next_row_scr[...] = p_mqk[0:HALO, :].astype(F32)
    p_mqk[:, 0:gw] = raw_q
    p_mqk[:, gw:] = raw_k
    p_rk[...] = cur_rk
    p_rv[...] = cur_rv
    p_mv[...] = cur_mv
    p_g[...] = gates


def _inproj_sweep(ctx2, x2, ab, w, wg, wgt, gb, gbt, cos, sin, lgk, cw, nb, tpb):
    d = x2.shape[1]
    tm = ROW_TILE
    cpt = tm // CHUNK
    gw = 4 * LANES
    tiles = nb * tpb
    r = tiles * tm
    lat = tpb - 1
    def visit(t):
        b, v = _div_small(t, tpb, nb)
        return b, jnp.where(v == 0, 0, tpb - v)

    cur = lambda i: visit(jnp.minimum(i, tiles - 1))
    flat = lambda bj: bj[0] * tpb + bj[1]
    tile_i = lambda i: flat(cur(i))
    tile_a = lambda i: flat(visit(jnp.maximum(i - 1, 0)))
    in_batch = lambda i: cur(i)[1]
    sel = lambda i: (cur(i)[0] * 2 + jnp.minimum(cur(i)[1], 1), 0, 0)
    state = lambda *dims: pl.BlockSpec((cpt,) + dims, lambda i: (tile_a(i),) + (0,) * len(dims))
    nchunks = tiles * cpt
    return pl.pallas_call(
        functools.partial(_inproj_sweep_kernel, tiles=tiles, tpb=tpb),
        grid=(tiles + 1,),
        in_specs=[pl.BlockSpec((tm, d), lambda i: (cur(i)[0], 0)),
                  pl.BlockSpec((tm, d), lambda i: (cur(i)[0] * lat + jnp.maximum(cur(i)[1] - 1, 0), 0)),
                  pl.BlockSpec((None, 2, d), sel),
                  _const_spec(w.shape), _const_spec(wg.shape), _const_spec(wgt.shape),
                  _const_spec(gb.shape), _const_spec(gbt.shape),
                  pl.BlockSpec((tm, LANES), lambda i: (in_batch(i), 0)),
                  pl.BlockSpec((tm, LANES), lambda i: (in_batch(i), 0)),
                  _const_spec(lgk.shape), _const_spec(cw.shape)],
        out_specs=[pl.BlockSpec((tm, 6 * gw), lambda i: (tile_i(i), 0)),
                   pl.BlockSpec((tm, LANES), lambda i: (tile_i(i), 0)),
                   pl.BlockSpec((32, tm), lambda i: (0, tile_i(i))),
                   state(4, CHUNK, LANES), state(4, CHUNK, LANES), state(4, HALO, LANES), state(8, LANES),
                   pl.BlockSpec((tm, 2 * gw), lambda i: (tile_a(i), 0))],
        out_shape=[jax.ShapeDtypeStruct((r, 6 * gw), BF16),
                   jax.ShapeDtypeStruct((r, LANES), F32),
                   jax.ShapeDtypeStruct((32, r), F32),
                   jax.ShapeDtypeStruct((nchunks, 4, CHUNK, LANES), BF16),
                   jax.ShapeDtypeStruct((nchunks, 4, CHUNK, LANES), BF16),
                   jax.ShapeDtypeStruct((nchunks, 4, HALO, LANES), BF16),
                   jax.ShapeDtypeStruct((nchunks, 8, LANES), F32),
                   jax.ShapeDtypeStruct((r, 2 * gw), BF16)],
        scratch_shapes=[pltpu.VMEM((4, CHUNK, LANES), F32), pltpu.VMEM((4, CHUNK, LANES), F32),
                        pltpu.VMEM((4, HALO, LANES), F32), pltpu.VMEM((8, LANES), F32),
                        pltpu.VMEM((4, CHUNK, LANES), F32),
                        pltpu.VMEM((tm, gw), BF16), pltpu.VMEM((tm, gw), BF16), pltpu.VMEM((tm, gw), BF16),
                        pltpu.VMEM((tm, 2 * gw), BF16), pltpu.VMEM((tm, LANES), F32),
                        pltpu.VMEM((HALO, 2 * gw), F32)],
        compiler_params=_cparams("arbitrary"),
        name="inproj_bwd_sweep",
    )(ctx2, x2, ab, w, wg, wgt, gb, gbt, cos, sin, lgk, cw)


class _Bag:
    def __init__(self, **kw):
        self.__dict__.update(kw)


def _mixer_chunk_stages(blk, r):
    rows = slice(blk * CHUNK, (blk + 1) * CHUNK)
    lane = _lane()
    sub = lax.broadcasted_iota(jnp.int32, (CHUNK, LANES), 0)
    lo = lane < HEAD_DIM
    sub_lo = sub < HEAD_DIM
    mask_ret = [(lane & 32) == 0, (lane & 32) != 0]
    mask_nat = [lo, lane >= HEAD_DIM]
    bd_ret = (sub >= HEAD_DIM) == ((lane & 32) != 0)
    bd_m = (sub >= HEAD_DIM) == (lane >= HEAD_DIM)
    le = sub <= lane
    ge = sub >= lane

    g = r.g_ref[rows, :]
    gt = r.gt_ref[:, rows]
    lf_col = _log_sigmoid(g)
    lf_row = _log_sigmoid(gt)
    le_bf = le.astype(BF16)
    ge_bf = ge.astype(BF16)
    pre_col = _cumsum_cols(ge_bf, lf_col)
    bal_f = pltpu.roll(pre_col, LANES - 8, 1)
    bal_b = pltpu.roll(pre_col[CHUNK - 1:CHUNK, :] - pre_col + lf_col, LANES - 8, 1)
    cf_all = g - bal_f
    cb_all = g - bal_b
    bf_row = _cumsum_rows(lf_row, le_bf)
    bb_row = bf_row[:, CHUNK - 1:CHUNK] - bf_row + lf_row
    qb, kb, vts, kf32, qt, vbd = [], [], [], [], [], []
    for p in range(8):
        sl = slice((p % 4) * LANES, (p % 4 + 1) * LANES)
        if p < 4:
            q2, k2, v2 = r.rq_ref[rows, sl], r.rk_ref[rows, sl], r.rv_ref[rows, sl]
            qf = q2.astype(F32)
            kf = None
        else:
            q2 = r.qk_ref[rows, sl]
            k2 = r.qk_ref[rows, 4 * LANES + (p - 4) * LANES:4 * LANES + (p - 3) * LANES]
            v2 = r.mv_ref[rows, sl]
            qf, kf = q2.astype(F32), k2.astype(F32)
        vt = v2.astype(F32).T.astype(BF16)
        qb.append(q2)
        kb.append(k2)
        vts.append(vt)
        kf32.append(kf)
        qt.append(qf.T)
        vbd.append([jnp.where(sub_lo, vt, jnp.zeros_like(vt)), jnp.where(sub_lo, jnp.zeros_like(vt), vt)])
    yield

    st2 = []
    for p in range(8):
        masks = mask_ret if p < 4 else mask_nat
        zero = jnp.zeros_like(qb[p])
        qstack = jnp.concatenate([jnp.where(masks[0], qb[p], zero), jnp.where(masks[1], qb[p], zero)], axis=0)
        st2.append(_dot_nt(kb[p], qstack))
    yield

    qn = [_dot_nt(jnp.concatenate([r.n_scr[p].astype(BF16), r.nmb_ref[blk, p]], axis=0), qb[4 + p])
          for p in range(4)]
    qn_f = [x[0:HALO] for x in qn]
    qn_b = [x[HALO:2 * HALO] for x in qn]
    cf2, cb2, bf2, bb2 = cf_all * LOG2E, cb_all * LOG2E, bf_row * LOG2E, bb_row * LOG2E
    m_f2, m_b2 = r.m_state[0:1, :] * LOG2E, r.mmb_ref[blk, 0:1, :] * LOG2E
    lhs, rhs = [], []
    for p in range(8):
        if p < 4:
            pts = [(st2[p][:, a * LANES:(a + 1) * LANES] * r.dm_scr[2 * p + a]).astype(BF16) for a in range(2)]
            x_f, x_b = r.dec_scr[0, p], r.dec_scr[1, p]
            old = [r.s_scr[p].astype(BF16), r.sretb_ref[blk, p]]
        else:
            pts, cf, cb = [], [], []
            for a in range(2):
                h = 2 * (p - 4) + a
                st = st2[p][:, a * LANES:(a + 1) * LANES]
                pf, coef_f = _mlstm_dir_weights(st, qn_f[p - 4][a:a + 1, :], cf2[:, h:h + 1],
                                                bf2[8 + h:9 + h, :], m_f2[0:1, h:h + 1], le)
                pb, coef_b = _mlstm_dir_weights(st, qn_b[p - 4][a:a + 1, :], cb2[:, 16 + h:17 + h],
                                                bb2[24 + h:25 + h, :], m_b2[0:1, 16 + h:17 + h], ge)
                pts.append((pf + pb).astype(BF16))
                cf.append(coef_f)
                cb.append(coef_b)
            x_f = jnp.where(sub_lo, cf[0], cf[1])
            x_b = jnp.where(sub_lo, cb[0], cb[1])
            old = [r.c_scr[p - 4].astype(BF16), r.cmb_ref[blk, p - 4]]
        lhs.append(jnp.concatenate(vbd[p] + old, axis=1))
        rhs.append(jnp.concatenate(pts + [(qt[p] * x_f).astype(BF16), (qt[p] * x_b).astype(BF16)], axis=0))
    yield

    ht = [_dot(lhs[p], rhs[p]) for p in range(8)]
    yield

    for p in range(8):
        sl = slice((p % 4) * LANES, (p % 4 + 1) * LANES)
        gate_ref = r.rg_ref if p < 4 else r.mo_ref
        y = _heads_out(ht[p], r.nw_ref[p])
        new = (y * gate_ref[rows, sl].astype(F32)).astype(BF16)
        r.mix_scr[rows, p * LANES:(p + 1) * LANES] = jnp.where(r.live, new, r.mix_scr[rows, p * LANES:(p + 1) * LANES])
    yield

    for p in range(4):
        cd = jnp.exp(r.lgk_ref[p:p + 1, :] * float(CHUNK))
        _ret_state_update(r.s_scr, p, kb[p], vts[p], r.dec_scr[2, p], cd, bd_ret)
    _mlstm_state_update(r.c_scr, r.n_scr, r.m_state, kf32[4:], vts[4:], cf_all, bal_f[CHUNK - 1:CHUNK, :],
                        0, lo, bd_m)
    yield


def _attn_inproj_stages(x, ab_ref, w_ref, nw_ref, cos, sin, y_ref):
    hb = _norm_mod(x, ab_ref)
    r2 = lax.broadcasted_iota(jnp.int32, (2 * LANES, 2 * LANES), 0)
    c2 = lax.broadcasted_iota(jnp.int32, (2 * LANES, 2 * LANES), 1)
    same_head = (((r2 ^ c2) & (LANES | 32)) == 0).astype(BF16)
    acc_q = _dot(hb, w_ref[:, 0:8 * LANES])
    acc_k = _dot(hb, w_ref[:, 8 * LANES:10 * LANES])
    y_ref[:, 10 * LANES:12 * LANES] = _dot(hb, w_ref[:, 10 * LANES:12 * LANES]).astype(BF16)
    yield
    for j in range(5):
        acc = acc_q[:, j * 2 * LANES:(j + 1) * 2 * LANES] if j < 4 else acc_k
        sq = acc * acc
        ms = _dot(sq.astype(BF16), same_head) * (1.0 / HEAD_DIM)
        nrm = acc * lax.rsqrt(ms + EPS)
        nw = nw_ref[0:1, :] if j < 4 else nw_ref[1:2, :]
        for v in range(2):
            ls = slice(v * LANES, (v + 1) * LANES)
            y_ref[:, (2 * j + v) * LANES:(2 * j + v + 1) * LANES] = _rope(nrm[:, ls] * nw, cos, sin).astype(BF16)
        if j in (1, 4):
            yield


def _ffn_splits(d_ff, pieces):
    blocks = d_ff // (2 * LANES)
    assert blocks * 2 * LANES == d_ff and blocks >= pieces
    cuts = [((i * blocks) // pieces) * 2 * LANES for i in range(pieces)]
    return cuts + [d_ff]


def _mixer_ffn_kernel(lg_ref, y_ref, qk_ref,
                      g_ref, gt_ref, sretb_ref, cmb_ref, nmb_ref, mmb_ref,
                      lgk_ref, lgkt_ref, nw_ref,
                      ctx_ref, x_ref, p_ref, wo_ref, wi_ref, w2_ref,
                      ab2_ref, wa_ref, nwa_ref, cos_ref, sin_ref,
                      o_ref, y2_ref,
                      s_scr, c_scr, n_scr, m_state, dm_scr, dec_scr, mix_scr, act_scr, x2_scr,
                      *, tiles, tpb, d_ff):
    s = pl.program_id(0)
    nb = tiles // tpb
    jt = _div_small(jnp.minimum(s, tiles - 1), tpb, nb)[1]
    gw = 4 * LANES
    rq_ref, rk_ref, rv_ref, rg_ref, mv_ref, mo_ref = [y_ref.at[:, j * gw:(j + 1) * gw] for j in range(6)]
    lane = _lane()
    sub = lax.broadcasted_iota(jnp.int32, (CHUNK, LANES), 0)

    @pl.when(s == 0)
    def _():
        mix_scr[...] = jnp.zeros_like(mix_scr)
        x2_scr[...] = jnp.zeros_like(x2_scr)
        le = sub <= lane
        ge = sub >= lane
        spos = sub.astype(F32)
        tpos = lane.astype(F32)
        diff = (lane - sub).astype(F32)
        for h in range(8):
            dm_scr[h] = (jnp.where(le, jnp.exp(lg_ref[h] * diff), 0.0)
                         + jnp.where(ge, jnp.exp(lg_ref[8 + h] * (-diff)), 0.0))
        for p in range(4):
            dec_scr[0, p] = jnp.exp(lgkt_ref[:, p:p + 1] * (tpos + 1.0))
            dec_scr[1, p] = jnp.exp(lgkt_ref[:, 4 + p:5 + p] * (float(CHUNK) - tpos))
            dec_scr[2, p] = jnp.exp(lgk_ref[p:p + 1, :] * (float(CHUNK) - 1.0 - spos))

    @pl.when(jt == 0)
    def _():
        s_scr[...] = jnp.zeros_like(s_scr)
        c_scr[...] = jnp.zeros_like(c_scr)
        n_scr[...] = jnp.zeros_like(n_scr)
        m_state[...] = jnp.zeros_like(m_state)

    r = _Bag(rq_ref=rq_ref, rk_ref=rk_ref, rv_ref=rv_ref, rg_ref=rg_ref, qk_ref=qk_ref, mv_ref=mv_ref, mo_ref=mo_ref,
             g_ref=g_ref, gt_ref=gt_ref, sretb_ref=sretb_ref, cmb_ref=cmb_ref, nmb_ref=nmb_ref, mmb_ref=mmb_ref,
             lgk_ref=lgk_ref, nw_ref=nw_ref, s_scr=s_scr, c_scr=c_scr, n_scr=n_scr, m_state=m_state,
             dm_scr=dm_scr, dec_scr=dec_scr, mix_scr=mix_scr, live=s < tiles)
    cuts = _ffn_splits(d_ff, 3)
    ffn_piece = lambda i: _ffn_cols(h, wi_ref, act_scr, cuts[i], cuts[i + 1], d_ff)

    x = _ctx_or_latent_rows(ctx_ref, x_ref, jnp.clip(s - 1, 0, tiles - 1), nb, tpb)
    x1, h = _ffn_in(x, mix_scr[...], p_ref, wo_ref)

    nxt = _attn_inproj_stages(x2_scr[...], ab2_ref, wa_ref, nwa_ref, cos_ref[...], sin_ref[...], y2_ref)
    chunk_a, chunk_b = _mixer_chunk_stages(0, r), _mixer_chunk_stages(1, r)
    next(nxt)
    next(chunk_a), next(chunk_b)
    next(chunk_a), next(chunk_b)
    ffn_piece(0)
    next(chunk_a), next(chunk_a)
    ffn_piece(1)
    next(chunk_a), next(chunk_a)
    next(nxt)
    ffn_piece(2)
    next(chunk_b), next(chunk_b)
    next(nxt)
    x2 = x1 + p_ref[3:4, :] * _dot(act_scr[...], w2_ref[...])
    o_ref[...] = x2
    next(chunk_b), next(chunk_b)
    x2_scr[...] = x2


def _mixer_ffn(y, g, gt, states, lg_smem, lgk, lgkt, nw, xs, prm, wo, wi_all, w2_all, layer,
               ab_next, w_next, nw_next, cos, sin, nb, tpb):
    gw = 4 * LANES
    tm = ROW_TILE
    cpt = tm // CHUNK
    sretb, cmb, nmb, mmb, qk_act = states
    tiles = nb * tpb
    d = xs[-1].shape[1]
    d_ff = w2_all.shape[1]
    n_next = w_next.shape[1]
    mix = lambda s: jnp.minimum(s, tiles - 1)
    ffn = lambda s: jnp.clip(s - 1, 0, tiles - 1)
    nxt = lambda s: jnp.maximum(s - 2, 0)
    state_spec = lambda a: pl.BlockSpec((cpt,) + a.shape[1:], lambda s: (mix(s),) + (0,) * (a.ndim - 1))
    def sel(t):
        b, j = _div_small(t, tpb, nb)
        return b * 2 + jnp.minimum(j, 1), 0, 0

    layer_spec = lambda a: pl.BlockSpec((None,) + a.shape[1:], lambda s: (layer, 0, 0),
                                        pipeline_mode=pl.Buffered(1))
    return pl.pallas_call(
        functools.partial(_mixer_ffn_kernel, tiles=tiles, tpb=tpb, d_ff=d_ff),
        grid=(tiles + 2,),
        in_specs=[_smem_spec(),
                  pl.BlockSpec((tm, 6 * gw), lambda s: (mix(s), 0)),
                  pl.BlockSpec((tm, 2 * gw), lambda s: (mix(s), 0)),
                  pl.BlockSpec((tm, LANES), lambda s: (mix(s), 0)),
                  pl.BlockSpec((32, tm), lambda s: (0, mix(s))),
                  state_spec(sretb), state_spec(cmb), state_spec(nmb), state_spec(mmb),
                  _const_spec(lgk.shape), _const_spec(lgkt.shape), _const_spec(nw.shape)]
                 + _split_row_specs(tm, d, nb, tpb, ffn)
                 + [pl.BlockSpec((None, 8, d), lambda s: sel(ffn(s))),
                    _const_spec(wo.shape), layer_spec(wi_all), layer_spec(w2_all),
                    pl.BlockSpec((None, 2, d), lambda s: sel(nxt(s))),
                    _const_spec(w_next.shape), _const_spec(nw_next.shape),
                    pl.BlockSpec((tm, LANES), lambda s: (_div_small(nxt(s), tpb, nb)[1], 0)),
                    pl.BlockSpec((tm, LANES), lambda s: (_div_small(nxt(s), tpb, nb)[1], 0))],
        out_specs=[pl.BlockSpec((tm, d), lambda s: (ffn(s), 0)),
                   pl.BlockSpec((tm, n_next), lambda s: (nxt(s), 0))],
        out_shape=[jax.ShapeDtypeStruct((tiles * tm, d), F32),
                   jax.ShapeDtypeStruct((tiles * tm, n_next), BF16)],
        scratch_shapes=[pltpu.VMEM((4, CHUNK, LANES), F32), pltpu.VMEM((4, CHUNK, LANES), F32),
                        pltpu.VMEM((4, HALO, LANES), F32), pltpu.VMEM((8, LANES), F32),
                        pltpu.VMEM((8, CHUNK, LANES), F32), pltpu.VMEM((3, 4, CHUNK, LANES), F32),
                        pltpu.VMEM((tm, 2 * gw), BF16), pltpu.VMEM((tm, d_ff), BF16),
                        pltpu.VMEM((tm, d), F32)],
        compiler_params=_cparams("arbitrary"),
        name="ret_mlstm_mixer_ffn",
    )(lg_smem, y, qk_act, g, gt, sretb, cmb, nmb, mmb, lgk, lgkt, nw,
      *xs, prm, wo, wi_all, w2_all, ab_next, w_next, nw_next, cos, sin)


def _ffn_in(x, m, p_ref, wo_ref):
    x1 = x + p_ref[0:1, :] * _dot(m, wo_ref[...])
    ms = jnp.mean(x1 * x1, axis=-1, keepdims=True)
    return x1, ((x1 * lax.rsqrt(ms + EPS)) * p_ref[1:2, :] + p_ref[2:3, :]).astype(BF16)


def _ffn_cols(h, wi_ref, act_scr, lo, hi, d_ff):
    gate = _dot(h, wi_ref[:, lo:hi])
    up = _dot(h, wi_ref[:, d_ff + lo:d_ff + hi])
    act_scr[:, lo:hi] = (gate * jax.nn.sigmoid(gate) * up).astype(BF16)


def _window_bias():
    kk = np.arange(CHUNK)[:, None]
    t = np.arange(CHUNK)[None, :]
    tabs = []
    for has_prev, has_next in ((False, True), (True, True), (True, False)):
        prev_ok = (kk >= t) & has_prev
        next_ok = (kk <= t) & has_next
        tabs.append(np.where(np.concatenate([prev_ok, next_ok], axis=0), 0.0, NEG))
    return jnp.asarray(np.stack(tabs), F32)


def _attn_ffn_kernel(sink_ref, qkv_a_ref, qkv_b_ref, kvp_ref, kvn_ref, kvx_ref, bias_ref,
                     x_a_ref, x_b_ref, p_ref, wo_ref, wi_ref, w2_ref, o_ref, m_scr, act_scr,
                     *, steps, per_batch, d_ff):
    s = pl.program_id(0)
    kw = 2 * LANES
    k0 = 8 * LANES
    tm = qkv_a_ref.shape[0]
    nblk = 2 * tm // CHUNK

    @pl.when(s == 0)
    def _():
        m_scr[...] = jnp.zeros_like(m_scr)

    j = _div_small(jnp.minimum(s, steps - 1), per_batch, steps // per_batch)[1]
    grp = lax.broadcasted_iota(jnp.int32, (1, 4 * CHUNK), 1) // CHUNK
    lane = _lane()
    mask_q = [(lane & 32) == 0, (lane & 32) != 0]
    tile4 = lambda b: jnp.concatenate([b] * 4, axis=1)
    inner = tile4(bias_ref[1])
    biases = ([tile4(jnp.where(j == 0, bias_ref[0], bias_ref[1]))] + [inner] * (nblk - 2)
              + [tile4(jnp.where(j == per_batch - 1, bias_ref[2], bias_ref[1]))])
    cuts = _ffn_splits(d_ff, nblk - 1)

    def ffn_up(i):
        _ffn_cols(h, wi_ref, act_scr, cuts[i], cuts[i + 1], d_ff)

    def q_rows(blk):
        ref = qkv_a_ref if blk * CHUNK < tm else qkv_b_ref
        r0 = (blk * CHUNK) % tm
        return ref, slice(r0, r0 + CHUNK)

    def scores(blk, kv):
        kvp, a = kv // 2, kv % 2
        ref, rows = q_rows(blk)
        qs = jnp.concatenate(
            [jnp.where(mask_q[a], ref[rows, (kvp * 4 + g) * LANES:(kvp * 4 + g + 1) * LANES],
                       jnp.zeros((CHUNK, LANES), BF16)) for g in range(4)], axis=0)
        return _dot_nt(kcats[kvp][blk], qs)

    def softmax_pv(blk, kv, st):
        bias = biases[blk]
        st = jnp.concatenate([st[0:CHUNK] + bias[0:CHUNK], st[CHUNK:2 * CHUNK],
                              st[2 * CHUNK:3 * CHUNK] + bias[CHUNK:2 * CHUNK], st[3 * CHUNK:]], axis=0)
        snk = jnp.where(grp == 0, sink_ref[kv * 4],
                        jnp.where(grp == 1, sink_ref[kv * 4 + 1],
                                  jnp.where(grp == 2, sink_ref[kv * 4 + 2], sink_ref[kv * 4 + 3])))
        m = jnp.maximum(jnp.max(st, axis=0, keepdims=True), snk)
        e = jnp.exp2(st - m)
        denom = jnp.exp2(snk - m) + jnp.sum(e, axis=0, keepdims=True)
        a = kv % 2
        return _dot(vts[kv // 2][blk][a * HEAD_DIM:(a + 1) * HEAD_DIM, :], e.astype(BF16)) * (1.0 / denom)

    def hand_over(blk, outs):
        for kvp in range(2):
            full = jnp.concatenate(outs[2 * kvp:2 * kvp + 2], axis=0)
            for g in range(4):
                m_scr[blk * CHUNK:(blk + 1) * CHUNK, (kvp * 4 + g) * LANES:(kvp * 4 + g + 1) * LANES] = (
                    full[:, g * CHUNK:(g + 1) * CHUNK].T.astype(BF16))

    x1, h = _ffn_in(jnp.concatenate([x_a_ref[...], x_b_ref[...]], axis=0), m_scr[...], p_ref, wo_ref)

    kcats, vts = [], []
    for kvp in range(2):
        ks = slice(k0 + kvp * LANES, k0 + (kvp + 1) * LANES)
        vs = slice(k0 + kw + kvp * LANES, k0 + kw + (kvp + 1) * LANES)
        es, ev = slice(kvp * LANES, (kvp + 1) * LANES), slice(kw + kvp * LANES, kw + (kvp + 1) * LANES)
        tile_rows = [(ref, slice(c * CHUNK, (c + 1) * CHUNK)) for ref in (qkv_a_ref, qkv_b_ref)
                     for c in range(tm // CHUNK)]
        k_chunks = [kvp_ref[:, es]] + [ref[rows, ks] for ref, rows in tile_rows] + [kvn_ref[:, es]]
        v_chunks = ([kvp_ref[:, ev]] + [ref[rows, vs] for ref, rows in tile_rows] + [kvn_ref[:, ev]]
                    + [kvx_ref[c * CHUNK:(c + 1) * CHUNK, ev] for c in range(kvx_ref.shape[0] // CHUNK)])
        v_t = [v.astype(F32).T.astype(BF16) for v in v_chunks]
        kcats.append([jnp.concatenate(k_chunks[b:b + 3] + [kvx_ref[:, es]], axis=0) for b in range(nblk)])
        vts.append([jnp.concatenate(v_t[b:b + 3] + v_t[nblk + 2:], axis=1) for b in range(nblk)])

    for blk in range(nblk):
        sts = [scores(blk, kv) for kv in range(4)]
        if blk < nblk - 1:
            ffn_up(blk)
        else:
            o_ref[...] = x1 + p_ref[3:4, :] * _dot(act_scr[...], w2_ref[...])
        hand_over(blk, [softmax_pv(blk, kv, sts[kv]) for kv in range(4)])


def _attn_ffn(y, sink, xc, prm, wo, wi_all, w2_all, layer, nb, nc, ctx_chunks):
    tm = ROW_TILE
    cpt = tm // CHUNK
    tpb = nc // cpt
    lat_tiles = (nc - ctx_chunks) // cpt
    ctx_tiles = ctx_chunks // cpt
    assert ctx_tiles == 1 and lat_tiles % 2 == 0 and lat_tiles >= 4
    per_batch = lat_tiles // 2
    steps = nb * per_batch
    d = xc.shape[1]
    d_ff = w2_all.shape[1]
    bias = _window_bias()
    n_qkv = y.shape[1]
    kv_w = n_qkv - d
    assert d % kv_w == 0
    kv_col = d // kv_w
    att = lambda s: jnp.minimum(s, steps - 1)
    ffn = lambda s: jnp.maximum(s - 1, 0)
    split = lambda t: _div_small(t, per_batch, nb)
    row_tile = lambda t, u: split(t)[0] * tpb + ctx_tiles + 2 * split(t)[1] + u
    chunk0 = lambda t: split(t)[0] * nc + ctx_chunks
    prev_c = lambda s: chunk0(att(s)) + jnp.maximum(split(att(s))[1] * 2 * cpt - 1, 0)
    next_c = lambda s: chunk0(att(s)) + jnp.minimum(split(att(s))[1] * 2 * cpt + 2 * cpt, lat_tiles * cpt - 1)
    edge_spec = lambda f: pl.BlockSpec((CHUNK, kv_w), lambda s: (f(s), kv_col))
    layer_spec = lambda a: pl.BlockSpec((None,) + a.shape[1:], lambda s: (layer, 0, 0),
                                        pipeline_mode=pl.Buffered(1))
    return pl.pallas_call(
        functools.partial(_attn_ffn_kernel, steps=steps, per_batch=per_batch, d_ff=d_ff),
        grid=(steps + 1,),
        in_specs=[_smem_spec(),
                  pl.BlockSpec((tm, n_qkv), lambda s: (row_tile(att(s), 0), 0)),
                  pl.BlockSpec((tm, n_qkv), lambda s: (row_tile(att(s), 1), 0)),
                  edge_spec(prev_c), edge_spec(next_c),
                  pl.BlockSpec((tm, kv_w), lambda s: (split(att(s))[0] * tpb, kv_col)),
                  _const_spec(bias.shape),
                  pl.BlockSpec((tm, d), lambda s: (row_tile(ffn(s), 0), 0)),
                  pl.BlockSpec((tm, d), lambda s: (row_tile(ffn(s), 1), 0)),
                  pl.BlockSpec((None, 8, d), lambda s: (split(ffn(s))[0] * 2 + 1, 0, 0)),
                  _const_spec(wo.shape), layer_spec(wi_all), layer_spec(w2_all)],
        out_specs=pl.BlockSpec((2 * tm, d), lambda s: (ffn(s), 0)),
        out_shape=jax.ShapeDtypeStruct((steps * 2 * tm, d), F32),
        scratch_shapes=[pltpu.VMEM((2 * tm, d), BF16), pltpu.VMEM((2 * tm, d_ff), BF16)],
        compiler_params=_cparams("arbitrary"),
        name="window_gqa_ffn",
    )(sink, y, y, y, y, y, bias, xc, xc, prm, wo, wi_all, w2_all)


def _pair_cols(w):
    rows, cols = w.shape
    return w.reshape(rows, cols // LANES, 2, 2, 32).transpose(0, 1, 3, 2, 4).reshape(rows, cols)


def _attn_q_cols(w):
    rows = w.shape[0]
    g_per = w.shape[1] // (H_KV * HEAD_DIM)
    return (w.reshape(rows, H_KV // 2, 2, g_per, 2, 32).transpose(0, 1, 3, 4, 2, 5)
            .reshape(rows, w.shape[1]))


def _attn_o_rows(w):
    cols = w.shape[1]
    g_per = w.shape[0] // (H_KV * HEAD_DIM)
    return (w.reshape(H_KV // 2, 2, g_per, HEAD_DIM, cols).transpose(0, 2, 1, 3, 4)
            .reshape(w.shape[0], cols))


def _rope_tables(seq, ctx_len):
    rows = seq // GRID_W
    row = np.repeat(np.arange(rows, dtype=np.float32), GRID_W)
    col = np.tile(np.arange(GRID_W, dtype=np.float32), rows)
    n = HEAD_DIM // 4
    inv = (np.float32(ROPE_BASE) ** (-np.arange(n, dtype=np.float32) / np.float32(n))).astype(np.float32)
    ang = np.concatenate([row[:, None] * inv, col[:, None] * inv], axis=-1).astype(np.float32)
    cos, sin = np.cos(ang), np.sin(ang)
    cos_t = np.concatenate([np.ones((ctx_len, LANES), np.float32), np.tile(cos, (1, 4))], axis=0)
    sin_t = np.concatenate([np.zeros((ctx_len, LANES), np.float32),
                            np.concatenate([-sin, -sin, sin, sin], axis=-1)], axis=0)
    return jnp.asarray(cos_t, F32), jnp.asarray(sin_t, F32)


def _mod_tables(mod, nb, norm_w):
    d = norm_w.shape[-1]
    lat = mod[:nb].reshape(nb, 6, d)
    ctx = jnp.broadcast_to(mod[nb].reshape(1, 6, d), (nb, 6, d))
    both = jnp.stack([ctx, lat], axis=1).reshape(nb * 2, 6, d)
    sh1, sc1, g1, sh2, sc2, g2 = [both[:, k] for k in range(6)]
    ab1 = jnp.stack([norm_w[0] * (1.0 + sc1), sh1], axis=1)
    zeros = jnp.zeros_like(g1)
    prm = jnp.stack([g1, norm_w[1] * (1.0 + sc2), sh2, g2, zeros, zeros, zeros, zeros], axis=1)
    return ab1, prm


def kernel(x, c, ctx, c_ctx, ada_w, ada_b, norm_w, ffn_w_in, ffn_w_out, ab_w_in, ab_w_out,
           ret_log_gamma, ret_norm_w, mlstm_conv_w, mlstm_conv_b, mlstm_gate_b, mlstm_norm_w,
           attn_w_in, attn_w_out, attn_q_norm_w, attn_k_norm_w, attn_sink):
    nb, seq, d = x.shape
    ctx_len = ctx.shape[1]
    depth = ada_w.shape[0]
    assert ctx_len == ROW_TILE and seq % ROW_TILE == 0 and d == 8 * LANES and nb < 8
    t_all = ctx_len + seq
    nc = t_all // CHUNK
    ctx_chunks = ctx_len // CHUNK
    tpb = t_all // ROW_TILE
    dr = d // 2

    rows = jnp.zeros((8, d), F32).at[:nb].set(c).at[nb].set(c_ctx)
    mod_all = _modulation(rows, ada_w, ada_b)
    cos_t, sin_t = _rope_tables(seq, ctx_len)
    wi_all = ffn_w_in.astype(BF16)
    w2_all = ffn_w_out.astype(BF16)
    xs = (ctx.reshape(nb * ctx_len, d), x.reshape(nb * seq, d))

    assert depth == 2 and ab_w_in.shape[0] == 1 and attn_w_in.shape[0] == 1
    ab_0, prm_0 = _mod_tables(mod_all[0], nb, norm_w[0])
    ab_1, prm_1 = _mod_tables(mod_all[1], nb, norm_w[1])

    w = ab_w_in[0]
    w_main = jnp.concatenate([_pair_cols(w[:, :dr]), _pair_cols(w[:, dr:2 * dr]), w[:, 2 * dr:8 * dr]],
                             axis=1).astype(BF16)
    wg = jnp.zeros((d, LANES), F32).at[:, :32].set(w[:, 8 * dr:]).astype(BF16)
    wgt = w[:, 8 * dr:].T.astype(BF16)
    gb = jnp.zeros((1, LANES), F32).at[0, :32].set(mlstm_gate_b[0].reshape(-1))
    gbt = mlstm_gate_b[0].reshape(32, 1)
    lg = ret_log_gamma[0].astype(F32)
    lgk = jnp.tile(jnp.repeat(lg.reshape(2, 4, 2), 32, axis=-1), (1, 1, 2)).reshape(8, LANES)
    cw = jnp.concatenate([mlstm_conv_w[0], mlstm_conv_b[0][None], jnp.zeros((4, 2 * dr), F32)], axis=0)
    nw = jnp.broadcast_to(jnp.concatenate([ret_norm_w[0], mlstm_norm_w[0]]).reshape(8, LANES, 1),
                          (8, LANES, LANES))
    wo_0 = ab_w_out[0].astype(BF16)

    w = attn_w_in[0]
    w_attn = jnp.concatenate([_attn_q_cols(w[:, :d]), _pair_cols(w[:, d:d + 2 * LANES]), w[:, d + 2 * LANES:]],
                             axis=1).astype(BF16)
    lane_w = lambda v: jnp.concatenate([v[:32], v[:32], v[32:], v[32:]])
    nwq = jnp.stack([lane_w(attn_q_norm_w[0]) * (HEAD_DIM ** -0.5 * LOG2E), lane_w(attn_k_norm_w[0])]
                    + [jnp.zeros((LANES,), F32)] * 6)
    wo_1 = _attn_o_rows(attn_w_out[0]).astype(BF16)

    y, g, gt, *states = _inproj_sweep(*xs, ab_0, w_main, wg, wgt, gb, gbt, cos_t, sin_t, lgk, cw, nb, tpb)
    x_mid, y_attn = _mixer_ffn(y, g, gt, states, lg.reshape(-1), lgk, lgk.T, nw, xs, prm_0, wo_0, wi_all, w2_all, 0,
                               ab_1, w_attn, nwq, cos_t, sin_t, nb, tpb)
    out = _attn_ffn(y_attn, attn_sink[0].astype(F32) * LOG2E, x_mid, prm_1, wo_1, wi_all, w2_all, 1, nb, nc, ctx_chunks)
    return out.reshape(nb, seq, d)
```

```python
import functools

import numpy as np
import jax
import jax.numpy as jnp
from jax import lax
from jax.experimental import pallas as pl
from jax.experimental.pallas import tpu as pltpu

F32 = jnp.float32
BF16 = jnp.bfloat16

HEAD_DIM = 64
CHUNK = 128
GRID_W = 64
ROPE_BASE = 10000.0
EPS = 1e-6
H_KV = 4
LANES = 128
ROW_TILE = 256
HALO = 16
LOG2E = 1.4426950408889634
NEG = -1e30
VMEM_LIMIT = 56 * 1024 * 1024


def _cparams(*sem):
    return pltpu.CompilerParams(dimension_semantics=sem, vmem_limit_bytes=VMEM_LIMIT)


def _const_spec(shape):
    nd = len(shape)
    return pl.BlockSpec(shape, lambda *_: (0,) * nd, pipeline_mode=pl.Buffered(1))


def _smem_spec():
    return pl.BlockSpec(memory_space=pltpu.SMEM)


def _wcast_specs(w_in_all, w_out_all, layer, steps):
    specs_in, specs_out, shapes = [], [], []
    for w in (w_in_all, w_out_all):
        total, cols = w.shape[1:]
        rows = next(r for r in range(HALO, total + 1, HALO) if total % r == 0 and total // r <= steps)
        n_blocks = total // rows
        specs_in.append(pl.BlockSpec((None, rows, cols),
                                     lambda i, n=n_blocks: (layer, jnp.minimum(i, n - 1), 0)))
        specs_out.append(pl.BlockSpec((rows, cols), lambda i, n=n_blocks: (jnp.minimum(i, n - 1), 0)))
        shapes.append(jax.ShapeDtypeStruct(w.shape[1:], BF16))
    return specs_in, specs_out, shapes


def _wcast(src_refs, dst_refs):
    for src, dst in zip(src_refs, dst_refs):
        dst[...] = src[...].astype(BF16)


def _lane(shape=(CHUNK, LANES)):
    return lax.broadcasted_iota(jnp.int32, shape, len(shape) - 1)


def _dot(a, b):
    return jnp.dot(a, b, preferred_element_type=F32)


def _dot_nt(a, b):
    return lax.dot_general(a, b, (((1,), (1,)), ((), ())), preferred_element_type=F32)


def _div_small(t, m, n):
    q = 0
    for b in range(1, n):
        q = q + jnp.where(t >= b * m, 1, 0)
    return q, t - q * m


def _split3(x):
    hi = x.astype(BF16)
    r = x - hi.astype(F32)
    mid = r.astype(BF16)
    lo = (r - mid.astype(F32)).astype(BF16)
    return hi, mid, lo


def _log_sigmoid(x):
    return jnp.minimum(x, 0.0) - jnp.log1p(jnp.exp(-jnp.abs(x)))


def _rope(x, cos, sin_signed):
    return x * cos + pltpu.roll(x, LANES // 2, 1) * sin_signed


def _mod_kernel(rows_ref, w_ref, b_ref, o_ref):
    a = rows_ref[...]
    a = a * jax.nn.sigmoid(a)
    a_hi = a.astype(BF16)
    a_lo = (a - a_hi.astype(F32)).astype(BF16)
    w = w_ref[...]
    w_hi = w.astype(BF16)
    w_lo = (w - w_hi.astype(F32)).astype(BF16)
    o_ref[...] = _dot(a_hi, w_hi) + _dot(a_hi, w_lo) + _dot(a_lo, w_hi) + b_ref[...]


def _modulation(rows, ada_w, ada_b):
    depth, d, n = ada_w.shape
    tn = n // 4
    return pl.pallas_call(
        _mod_kernel,
        grid=(depth, n // tn),
        in_specs=[pl.BlockSpec((8, d), lambda l, j: (0, 0)),
                  pl.BlockSpec((None, d, tn), lambda l, j: (l, 0, j)),
                  pl.BlockSpec((None, 1, tn), lambda l, j: (l, 0, j))],
        out_specs=pl.BlockSpec((None, 8, tn), lambda l, j: (l, 0, j)),
        out_shape=jax.ShapeDtypeStruct((depth, 8, n), F32),
        compiler_params=_cparams("arbitrary", "arbitrary"),
        name="adaln_modulation",
    )(rows, ada_w, ada_b.reshape(depth, 1, n))


def _norm_mod(x, ab_ref):
    ms = jnp.mean(x * x, axis=-1, keepdims=True)
    h = (x * lax.rsqrt(ms + EPS)) * ab_ref[0:1, :] + ab_ref[1:2, :]
    return h.astype(BF16)


def _ctx_or_latent_rows(ctx_ref, x_ref, tile, nb, tiles_per_batch):
    return jnp.where(_div_small(tile, tiles_per_batch, nb)[1] == 0, ctx_ref[...], x_ref[...])


def _split_row_specs(tm, d, nb, tpb, tile_of):
    lat = tpb - 1

    def latent_row(i):
        b, j = _div_small(tile_of(i), tpb, nb)
        return b * lat + jnp.maximum(j - 1, 0), 0

    return [pl.BlockSpec((tm, d), lambda i: (_div_small(tile_of(i), tpb, nb)[0], 0)),
            pl.BlockSpec((tm, d), latent_row)]


def _cumsum_cols(tri_bf, lf):
    hi, mid, lo = _split3(lf)
    return _dot(tri_bf, hi) + _dot(tri_bf, mid) + _dot(tri_bf, lo)


def _cumsum_rows(lf, tri_bf):
    hi, mid, lo = _split3(lf)
    return _dot(hi, tri_bf) + _dot(mid, tri_bf) + _dot(lo, tri_bf)


def _ret_state_update(s_ref, p, k2, vt, kdec, cd_lanes, bd):
    kf = (k2.astype(F32) * kdec).astype(BF16)
    s_ref[p] = s_ref[p] * cd_lanes + jnp.where(bd, _dot(vt, kf), 0.0)


def _mlstm_state_update(c_ref, n_ref, m_ref, k_pairs, vt_pairs, c_all, bend, col0, lo, bd):
    cmax = jnp.max(c_all, axis=0, keepdims=True)
    w_all = jnp.exp(c_all - cmax)
    m_old = m_ref[0:1, :]
    mrel = jnp.maximum(m_old, cmax)
    a_row = jnp.exp(m_old - mrel)
    bb_row = jnp.exp(cmax - mrel)
    m_ref[0:1, :] = bend + mrel
    lo_row = lo[0:1, :]
    for p in range(4):
        h0 = col0 + 2 * p
        kw = k_pairs[p] * jnp.where(lo, w_all[:, h0:h0 + 1], w_all[:, h0 + 1:h0 + 2])
        kvt = _dot(vt_pairs[p], kw.astype(BF16))
        nloc = jnp.sum(kw, axis=0, keepdims=True)
        a_l = jnp.where(lo_row, a_row[:, h0:h0 + 1], a_row[:, h0 + 1:h0 + 2])
        bb_l = jnp.where(lo_row, bb_row[:, h0:h0 + 1], bb_row[:, h0 + 1:h0 + 2])
        c_ref[p] = c_ref[p] * a_l + jnp.where(bd, kvt, 0.0) * bb_l
        n_new = (n_ref[p, 0:1, :] + n_ref[p, 1:2, :]) * a_l + nloc * bb_l
        n_ref[p, 0:1, :] = jnp.where(lo_row, n_new, 0.0)
        n_ref[p, 1:2, :] = jnp.where(lo_row, 0.0, n_new)


def _mlstm_dir_weights(st, qn_row, c_col, bt_row, m_prev, tri):
    dl = jnp.where(tri, c_col + bt_row, NEG)
    mx = jnp.max(dl, axis=0, keepdims=True)
    al = bt_row + m_prev
    m_t = jnp.maximum(al, mx)
    w = jnp.exp2(dl - m_t)
    a_t = jnp.exp2(al - m_t)
    sw = st * w
    den = jnp.sum(sw, axis=0, keepdims=True) + a_t * qn_row
    r = 1.0 / jnp.maximum(jnp.abs(den), jnp.exp2(-m_t))
    return sw * r, a_t * r


def _heads_out(ht, nw_tab):
    rows = []
    for a in range(2):
        ha = ht[a * HEAD_DIM:(a + 1) * HEAD_DIM, :]
        ms = jnp.mean(ha * ha, axis=0, keepdims=True)
        rows.append(ha * lax.rsqrt(ms + EPS))
    return (jnp.concatenate(rows, axis=0) * nw_tab).T


def _inproj_sweep_kernel(ctx_ref, x_ref, ab_ref, w_ref, wg_ref, wgt_ref, gb_ref, gbt_ref, cos_ref, sin_ref,
                         lgk_ref, cw_ref, wi_src, w2_src,
                         y_ref, g_ref, gt_ref, sret_ref, cm_ref, nm_ref, mm_ref, qk_ref, wi_dst, w2_dst,
                         s_scr, c_scr, n_scr, m_scr, kdec_scr,
                         p_rk, p_rv, p_mv, p_mqk, p_g, next_row_scr, *, tiles, tpb):
    i = pl.program_id(0)
    _wcast((wi_src, w2_src), (wi_dst, w2_dst))

    def order(t):
        v = _div_small(t, tpb, tiles // tpb)[1]
        return jnp.where(v == 0, 0, tpb - v)

    jt_a = order(jnp.maximum(i - 1, 0))
    lane = _lane()
    sub = lax.broadcasted_iota(jnp.int32, (CHUNK, LANES), 0)
    lo = lane < HEAD_DIM
    lo_row = lo[0:1, :]
    bd_ret = (sub >= HEAD_DIM) == ((lane & 32) != 0)
    bd_m = (sub >= HEAD_DIM) == (lane >= HEAD_DIM)
    gw = 4 * LANES

    @pl.when(i == 0)
    def _():
        pos = sub.astype(F32)
        for p in range(4):
            kdec_scr[p] = jnp.exp(lgk_ref[4 + p:5 + p, :] * pos)
        for ref in (p_rk, p_rv, p_mv, p_mqk, p_g, next_row_scr):
            ref[...] = jnp.zeros_like(ref)

    @pl.when(jt_a == 0)
    def _():
        s_scr[...] = jnp.zeros_like(s_scr)
        c_scr[...] = jnp.zeros_like(c_scr)
        n_scr[...] = jnp.zeros_like(n_scr)
        m_scr[...] = jnp.zeros_like(m_scr)

    tile_i = jnp.minimum(i, tiles - 1)
    jt_i = order(tile_i)
    hb = _norm_mod(jnp.where(jt_i == 0, ctx_ref[...], x_ref[...]), ab_ref)
    cos, sin = cos_ref[...], sin_ref[...]

    def project(j):
        acc = _dot(hb, w_ref[:, j * gw:(j + 1) * gw])
        if j in (0, 1):
            if j == 0:
                acc = acc * (HEAD_DIM ** -0.5)
            acc = jnp.concatenate([_rope(acc[:, p * LANES:(p + 1) * LANES], cos, sin) for p in range(4)], axis=1)
        elif j == 3:
            acc = acc * jax.nn.sigmoid(acc)
        elif j == 7:
            acc = jax.nn.sigmoid(acc)
        return acc.astype(BF16)

    raw_q, raw_k = project(4), project(5)

    prev_on = jnp.where(jt_a <= 1, 0.0, 1.0).astype(F32)
    next_on = jnp.where((jt_a == 0) | (jt_a == tpb - 1), 0.0, 1.0).astype(F32)
    cur = p_mqk[...].astype(F32)
    n = cur.shape[0]
    row = lax.broadcasted_iota(jnp.int32, cur.shape, 0)
    prev_row = jnp.concatenate([raw_q[n - HALO:, :], raw_k[n - HALO:, :]], axis=1)[HALO - 1:HALO, :].astype(F32)
    xm = jnp.where(row == 0, prev_row * prev_on, pltpu.roll(cur, 1, 0))
    xp = jnp.where(row == n - 1, next_row_scr[0:1, :] * next_on, pltpu.roll(cur, n - 1, 0))
    conv = cw_ref[3:4, :] + cw_ref[0:1, :] * xm + cw_ref[1:2, :] * cur + cw_ref[2:3, :] * xp
    qk = conv * jax.nn.sigmoid(conv)
    qk_ref[:, 0:gw] = (qk[:, 0:gw] * (HEAD_DIM ** -0.5)).astype(BF16)
    qk_ref[:, gw:] = qk[:, gw:].astype(BF16)
    le_bf = (sub <= lane).astype(BF16)
    chunks = (1, 0)
    pre = {}
    for blk in chunks:
        rows = slice(blk * CHUNK, (blk + 1) * CHUNK)
        g = p_g[rows, :]
        bal = pltpu.roll(_cumsum_cols(le_bf, _log_sigmoid(g)), LANES - 8, 1)
        c_all = g - bal
        cmax = jnp.max(c_all, axis=0, keepdims=True)
        w_all = jnp.exp(c_all - cmax)
        vts, ks, nlocs = [], [], []
        for p in range(8):
            sl = slice((p % 4) * LANES, (p % 4 + 1) * LANES)
            if p < 4:
                v2 = p_rv[rows, sl]
                ks.append((p_rk[rows, sl].astype(F32) * kdec_scr[p]).astype(BF16))
            else:
                v2 = p_mv[rows, sl]
                h0 = 16 + 2 * (p - 4)
                kw = qk[rows, gw + (p - 4) * LANES:gw + (p - 3) * LANES] * jnp.where(
                    lo, w_all[:, h0:h0 + 1], w_all[:, h0 + 1:h0 + 2])
                ks.append(kw.astype(BF16))
                nlocs.append(jnp.sum(kw, axis=0, keepdims=True))
            vts.append(v2.astype(F32).T.astype(BF16))
        pre[blk] = (vts, ks, nlocs, cmax, bal[0:1, :])

    cur_rk = project(1)
    y_ref[:, 1 * gw:2 * gw] = cur_rk
    y_ref[:, 0:gw] = project(0)
    kvs = {blk: [_dot(pre[blk][0][p], pre[blk][1][p]) for p in range(8)] for blk in chunks}
    cur_rv = project(2)
    y_ref[:, 2 * gw:3 * gw] = cur_rv
    y_ref[:, 3 * gw:4 * gw] = project(3)

    for blk in chunks:
        _, _, nlocs, cmax, bend = pre[blk]
        sret_ref[blk] = s_scr[...].astype(BF16)
        cm_ref[blk] = c_scr[...].astype(BF16)
        nm_ref[blk] = n_scr[...].astype(BF16)
        mm_ref[blk] = m_scr[...]
        m_old = m_scr[0:1, :]
        mrel = jnp.maximum(m_old, cmax)
        a_row = jnp.exp(m_old - mrel)
        bb_row = jnp.exp(cmax - mrel)
        m_scr[0:1, :] = bend + mrel
        for p in range(4):
            cd = jnp.exp(lgk_ref[4 + p:5 + p, :] * float(CHUNK))
            s_scr[p] = s_scr[p] * cd + jnp.where(bd_ret, kvs[blk][p], 0.0)
            h0 = 16 + 2 * p
            a_l = jnp.where(lo_row, a_row[:, h0:h0 + 1], a_row[:, h0 + 1:h0 + 2])
            bb_l = jnp.where(lo_row, bb_row[:, h0:h0 + 1], bb_row[:, h0 + 1:h0 + 2])
            c_scr[p] = c_scr[p] * a_l + jnp.where(bd_m, kvs[blk][4 + p], 0.0) * bb_l
            n_new = (n_scr[p, 0:1, :] + n_scr[p, 1:2, :]) * a_l + nlocs[p] * bb_l
            n_scr[p, 0:1, :] = jnp.where(lo_row, n_new, 0.0)
            n_scr[p, 1:2, :] = jnp.where(lo_row, 0.0, n_new)

    cur_mv = project(6)
    y_ref[:, 4 * gw:5 * gw] = cur_mv
    y_ref[:, 5 * gw:6 * gw] = project(7)
    gates = _dot(hb, wg_ref[...]) + gb_ref[...]
    g_ref[...] = gates
    gt_ref[...] = _dot_nt(wgt_ref[...], hb) + gbt_ref[...]
    next_row_scr[...] = p_mqk[0:HALO, :].astype(F32)
    p_mqk[:, 0:gw] = raw_q
    p_mqk[:, gw:] = raw_k
    p_rk[...] = cur_rk
    p_rv[...] = cur_rv
    p_mv[...] = cur_mv
    p_g[...] = gates


def _inproj_sweep(ctx2, x2, ab, w, wg, wgt, gb, gbt, cos, sin, lgk, cw, ffn_w, nb, tpb):
    d = x2.shape[1]
    tm = ROW_TILE
    cpt = tm // CHUNK
    gw = 4 * LANES
    tiles = nb * tpb
    r = tiles * tm
    lat = tpb - 1
    def visit(t):
        b, v = _div_small(t, tpb, nb)
        return b, jnp.where(v == 0, 0, tpb - v)

    cur = lambda i: visit(jnp.minimum(i, tiles - 1))
    flat = lambda bj: bj[0] * tpb + bj[1]
    tile_i = lambda i: flat(cur(i))
    tile_a = lambda i: flat(visit(jnp.maximum(i - 1, 0)))
    in_batch = lambda i: cur(i)[1]
    sel = lambda i: (cur(i)[0] * 2 + jnp.minimum(cur(i)[1], 1), 0, 0)
    state = lambda *dims: pl.BlockSpec((cpt,) + dims, lambda i: (tile_a(i),) + (0,) * len(dims))
    nchunks = tiles * cpt
    wc_in, wc_out, wc_shapes = _wcast_specs(*ffn_w, 0, tiles + 1)
    return pl.pallas_call(
        functools.partial(_inproj_sweep_kernel, tiles=tiles, tpb=tpb),
        grid=(tiles + 1,),
        in_specs=[pl.BlockSpec((tm, d), lambda i: (cur(i)[0], 0)),
                  pl.BlockSpec((tm, d), lambda i: (cur(i)[0] * lat + jnp.maximum(cur(i)[1] - 1, 0), 0)),
                  pl.BlockSpec((None, 2, d), sel),
                  _const_spec(w.shape), _const_spec(wg.shape), _const_spec(wgt.shape),
                  _const_spec(gb.shape), _const_spec(gbt.shape),
                  pl.BlockSpec((tm, LANES), lambda i: (in_batch(i), 0)),
                  pl.BlockSpec((tm, LANES), lambda i: (in_batch(i), 0)),
                  _const_spec(lgk.shape), _const_spec(cw.shape)] + wc_in,
        out_specs=[pl.BlockSpec((tm, 6 * gw), lambda i: (tile_i(i), 0)),
                   pl.BlockSpec((tm, LANES), lambda i: (tile_i(i), 0)),
                   pl.BlockSpec((32, tm), lambda i: (0, tile_i(i))),
                   state(4, CHUNK, LANES), state(4, CHUNK, LANES), state(4, HALO, LANES), state(8, LANES),
                   pl.BlockSpec((tm, 2 * gw), lambda i: (tile_a(i), 0))] + wc_out,
        out_shape=[jax.ShapeDtypeStruct((r, 6 * gw), BF16),
                   jax.ShapeDtypeStruct((r, LANES), F32),
                   jax.ShapeDtypeStruct((32, r), F32),
                   jax.ShapeDtypeStruct((nchunks, 4, CHUNK, LANES), BF16),
                   jax.ShapeDtypeStruct((nchunks, 4, CHUNK, LANES), BF16),
                   jax.ShapeDtypeStruct((nchunks, 4, HALO, LANES), BF16),
                   jax.ShapeDtypeStruct((nchunks, 8, LANES), F32),
                   jax.ShapeDtypeStruct((r, 2 * gw), BF16)] + wc_shapes,
        scratch_shapes=[pltpu.VMEM((4, CHUNK, LANES), F32), pltpu.VMEM((4, CHUNK, LANES), F32),
                        pltpu.VMEM((4, HALO, LANES), F32), pltpu.VMEM((8, LANES), F32),
                        pltpu.VMEM((4, CHUNK, LANES), F32),
                        pltpu.VMEM((tm, gw), BF16), pltpu.VMEM((tm, gw), BF16), pltpu.VMEM((tm, gw), BF16),
                        pltpu.VMEM((tm, 2 * gw), BF16), pltpu.VMEM((tm, LANES), F32),
                        pltpu.VMEM((HALO, 2 * gw), F32)],
        compiler_params=_cparams("arbitrary"),
        name="inproj_bwd_sweep",
    )(ctx2, x2, ab, w, wg, wgt, gb, gbt, cos, sin, lgk, cw, *ffn_w)


class _Bag:
    def __init__(self, **kw):
        self.__dict__.update(kw)


def _mixer_chunk_stages(blk, r):
    rows = slice(blk * CHUNK, (blk + 1) * CHUNK)
    lane = _lane()
    sub = lax.broadcasted_iota(jnp.int32, (CHUNK, LANES), 0)
    lo = lane < HEAD_DIM
    sub_lo = sub < HEAD_DIM
    mask_ret = [(lane & 32) == 0, (lane & 32) != 0]
    mask_nat = [lo, lane >= HEAD_DIM]
    bd_ret = (sub >= HEAD_DIM) == ((lane & 32) != 0)
    bd_m = (sub >= HEAD_DIM) == (lane >= HEAD_DIM)
    le = sub <= lane
    ge = sub >= lane

    g = r.g_ref[rows, :]
    gt = r.gt_ref[:, rows]
    lf_col = _log_sigmoid(g)
    lf_row = _log_sigmoid(gt)
    le_bf = le.astype(BF16)
    ge_bf = ge.astype(BF16)
    pre_col = _cumsum_cols(ge_bf, lf_col)
    bal_f = pltpu.roll(pre_col, LANES - 8, 1)
    bal_b = pltpu.roll(pre_col[CHUNK - 1:CHUNK, :] - pre_col + lf_col, LANES - 8, 1)
    cf_all = g - bal_f
    cb_all = g - bal_b
    bf_row = _cumsum_rows(lf_row, le_bf)
    bb_row = bf_row[:, CHUNK - 1:CHUNK] - bf_row + lf_row
    qb, kb, vts, kf32, qt, vbd = [], [], [], [], [], []
    for p in range(8):
        sl = slice((p % 4) * LANES, (p % 4 + 1) * LANES)
        if p < 4:
            q2, k2, v2 = r.rq_ref[rows, sl], r.rk_ref[rows, sl], r.rv_ref[rows, sl]
            qf = q2.astype(F32)
            kf = None
        else:
            q2 = r.qk_ref[rows, sl]
            k2 = r.qk_ref[rows, 4 * LANES + (p - 4) * LANES:4 * LANES + (p - 3) * LANES]
            v2 = r.mv_ref[rows, sl]
            qf, kf = q2.astype(F32), k2.astype(F32)
        vt = v2.astype(F32).T.astype(BF16)
        qb.append(q2)
        kb.append(k2)
        vts.append(vt)
        kf32.append(kf)
        qt.append(qf.T)
        vbd.append([jnp.where(sub_lo, vt, jnp.zeros_like(vt)), jnp.where(sub_lo, jnp.zeros_like(vt), vt)])
    yield

    st2 = []
    for p in range(8):
        masks = mask_ret if p < 4 else mask_nat
        zero = jnp.zeros_like(qb[p])
        qstack = jnp.concatenate([jnp.where(masks[0], qb[p], zero), jnp.where(masks[1], qb[p], zero)], axis=0)
        st2.append(_dot_nt(kb[p], qstack))
    yield

    qn = [_dot_nt(jnp.concatenate([r.n_scr[p].astype(BF16), r.nmb_ref[blk, p]], axis=0), qb[4 + p])
          for p in range(4)]
    qn_f = [x[0:HALO] for x in qn]
    qn_b = [x[HALO:2 * HALO] for x in qn]
    cf2, cb2, bf2, bb2 = cf_all * LOG2E, cb_all * LOG2E, bf_row * LOG2E, bb_row * LOG2E
    m_f2, m_b2 = r.m_state[0:1, :] * LOG2E, r.mmb_ref[blk, 0:1, :] * LOG2E
    lhs, rhs = [], []
    for p in range(8):
        if p < 4:
            pts = [(st2[p][:, a * LANES:(a + 1) * LANES] * r.dm_scr[2 * p + a]).astype(BF16) for a in range(2)]
            x_f, x_b = r.dec_scr[0, p], r.dec_scr[1, p]
            old = [r.s_scr[p].astype(BF16), r.sretb_ref[blk, p]]
        else:
            pts, cf, cb = [], [], []
            for a in range(2):
                h = 2 * (p - 4) + a
                st = st2[p][:, a * LANES:(a + 1) * LANES]
                pf, coef_f = _mlstm_dir_weights(st, qn_f[p - 4][a:a + 1, :], cf2[:, h:h + 1],
                                                bf2[8 + h:9 + h, :], m_f2[0:1, h:h + 1], le)
                pb, coef_b = _mlstm_dir_weights(st, qn_b[p - 4][a:a + 1, :], cb2[:, 16 + h:17 + h],
                                                bb2[24 + h:25 + h, :], m_b2[0:1, 16 + h:17 + h], ge)
                pts.append((pf + pb).astype(BF16))
                cf.append(coef_f)
                cb.append(coef_b)
            x_f = jnp.where(sub_lo, cf[0], cf[1])
            x_b = jnp.where(sub_lo, cb[0], cb[1])
            old = [r.c_scr[p - 4].astype(BF16), r.cmb_ref[blk, p - 4]]
        lhs.append(jnp.concatenate(vbd[p] + old, axis=1))
        rhs.append(jnp.concatenate(pts + [(qt[p] * x_f).astype(BF16), (qt[p] * x_b).astype(BF16)], axis=0))
    yield

    ht = [_dot(lhs[p], rhs[p]) for p in range(8)]
    yield

    for p in range(8):
        sl = slice((p % 4) * LANES, (p % 4 + 1) * LANES)
        gate_ref = r.rg_ref if p < 4 else r.mo_ref
        y = _heads_out(ht[p], r.nw_ref[p])
        new = (y * gate_ref[rows, sl].astype(F32)).astype(BF16)
        r.mix_scr[rows, p * LANES:(p + 1) * LANES] = jnp.where(r.live, new, r.mix_scr[rows, p * LANES:(p + 1) * LANES])
    yield

    for p in range(4):
        cd = jnp.exp(r.lgk_ref[p:p + 1, :] * float(CHUNK))
        _ret_state_update(r.s_scr, p, kb[p], vts[p], r.dec_scr[2, p], cd, bd_ret)
    _mlstm_state_update(r.c_scr, r.n_scr, r.m_state, kf32[4:], vts[4:], cf_all, bal_f[CHUNK - 1:CHUNK, :],
                        0, lo, bd_m)
    yield


def _attn_inproj_stages(x, ab_ref, w_ref, nw_ref, cos, sin, y_ref):
    hb = _norm_mod(x, ab_ref)
    r2 = lax.broadcasted_iota(jnp.int32, (2 * LANES, 2 * LANES), 0)
    c2 = lax.broadcasted_iota(jnp.int32, (2 * LANES, 2 * LANES), 1)
    same_head = (((r2 ^ c2) & (LANES | 32)) == 0).astype(BF16)
    acc_q = _dot(hb, w_ref[:, 0:8 * LANES])
    acc_k = _dot(hb, w_ref[:, 8 * LANES:10 * LANES])
    y_ref[:, 10 * LANES:12 * LANES] = _dot(hb, w_ref[:, 10 * LANES:12 * LANES]).astype(BF16)
    yield
    for j in range(5):
        acc = acc_q[:, j * 2 * LANES:(j + 1) * 2 * LANES] if j < 4 else acc_k
        sq = acc * acc
        ms = _dot(sq.astype(BF16), same_head) * (1.0 / HEAD_DIM)
        nrm = acc * lax.rsqrt(ms + EPS)
        nw = nw_ref[0:1, :] if j < 4 else nw_ref[1:2, :]
        for v in range(2):
            ls = slice(v * LANES, (v + 1) * LANES)
            y_ref[:, (2 * j + v) * LANES:(2 * j + v + 1) * LANES] = _rope(nrm[:, ls] * nw, cos, sin).astype(BF16)
        if j in (1, 4):
            yield


def _ffn_splits(d_ff, pieces):
    blocks = d_ff // (2 * LANES)
    assert blocks * 2 * LANES == d_ff and blocks >= pieces
    cuts = [((i * blocks) // pieces) * 2 * LANES for i in range(pieces)]
    return cuts + [d_ff]


def _mixer_ffn_kernel(lg_ref, y_ref, qk_ref,
                      g_ref, gt_ref, sretb_ref, cmb_ref, nmb_ref, mmb_ref,
                      lgk_ref, lgkt_ref, nw_ref,
                      ctx_ref, x_ref, p_ref, wo_ref, wi_ref, w2_ref,
                      ab2_ref, wa_ref, nwa_ref, cos_ref, sin_ref, wi_src, w2_src,
                      o_ref, y2_ref, wi_dst, w2_dst,
                      s_scr, c_scr, n_scr, m_state, dm_scr, dec_scr, mix_scr, act_scr, x2_scr,
                      *, tiles, tpb, d_ff):
    s = pl.program_id(0)
    _wcast((wi_src, w2_src), (wi_dst, w2_dst))
    nb = tiles // tpb
    jt = _div_small(jnp.minimum(s, tiles - 1), tpb, nb)[1]
    gw = 4 * LANES
    rq_ref, rk_ref, rv_ref, rg_ref, mv_ref, mo_ref = [y_ref.at[:, j * gw:(j + 1) * gw] for j in range(6)]
    lane = _lane()
    sub = lax.broadcasted_iota(jnp.int32, (CHUNK, LANES), 0)

    @pl.when(s == 0)
    def _():
        mix_scr[...] = jnp.zeros_like(mix_scr)
        x2_scr[...] = jnp.zeros_like(x2_scr)
        le = sub <= lane
        ge = sub >= lane
        spos = sub.astype(F32)
        tpos = lane.astype(F32)
        diff = (lane - sub).astype(F32)
        for h in range(8):
            dm_scr[h] = (jnp.where(le, jnp.exp(lg_ref[h] * diff), 0.0)
                         + jnp.where(ge, jnp.exp(lg_ref[8 + h] * (-diff)), 0.0))
        for p in range(4):
            dec_scr[0, p] = jnp.exp(lgkt_ref[:, p:p + 1] * (tpos + 1.0))
            dec_scr[1, p] = jnp.exp(lgkt_ref[:, 4 + p:5 + p] * (float(CHUNK) - tpos))
            dec_scr[2, p] = jnp.exp(lgk_ref[p:p + 1, :] * (float(CHUNK) - 1.0 - spos))

    @pl.when(jt == 0)
    def _():
        s_scr[...] = jnp.zeros_like(s_scr)
        c_scr[...] = jnp.zeros_like(c_scr)
        n_scr[...] = jnp.zeros_like(n_scr)
        m_state[...] = jnp.zeros_like(m_state)

    r = _Bag(rq_ref=rq_ref, rk_ref=rk_ref, rv_ref=rv_ref, rg_ref=rg_ref, qk_ref=qk_ref, mv_ref=mv_ref, mo_ref=mo_ref,
             g_ref=g_ref, gt_ref=gt_ref, sretb_ref=sretb_ref, cmb_ref=cmb_ref, nmb_ref=nmb_ref, mmb_ref=mmb_ref,
             lgk_ref=lgk_ref, nw_ref=nw_ref, s_scr=s_scr, c_scr=c_scr, n_scr=n_scr, m_state=m_state,
             dm_scr=dm_scr, dec_scr=dec_scr, mix_scr=mix_scr, live=s < tiles)
    cuts = _ffn_splits(d_ff, 3)
    ffn_piece = lambda i: _ffn_cols(h, wi_ref, act_scr, cuts[i], cuts[i + 1], d_ff)

    x = _ctx_or_latent_rows(ctx_ref, x_ref, jnp.clip(s - 1, 0, tiles - 1), nb, tpb)
    x1, h = _ffn_in(x, mix_scr[...], p_ref, wo_ref)

    nxt = _attn_inproj_stages(x2_scr[...], ab2_ref, wa_ref, nwa_ref, cos_ref[...], sin_ref[...], y2_ref)
    chunk_a, chunk_b = _mixer_chunk_stages(0, r), _mixer_chunk_stages(1, r)
    next(nxt)
    next(chunk_a), next(chunk_b)
    next(chunk_a), next(chunk_b)
    ffn_piece(0)
    next(chunk_a), next(chunk_a)
    ffn_piece(1)
    next(chunk_a), next(chunk_a)
    next(nxt)
    ffn_piece(2)
    next(chunk_b), next(chunk_b)
    next(nxt)
    x2 = x1 + p_ref[3:4, :] * _dot(act_scr[...], w2_ref[...])
    o_ref[...] = x2
    next(chunk_b), next(chunk_b)
    x2_scr[...] = x2


def _mixer_ffn(y, g, gt, states, lg_smem, lgk, lgkt, nw, xs, prm, wo, wi, w2,
               ab_next, w_next, nw_next, cos, sin, ffn_w, nb, tpb):
    gw = 4 * LANES
    tm = ROW_TILE
    cpt = tm // CHUNK
    sretb, cmb, nmb, mmb, qk_act = states
    tiles = nb * tpb
    d = xs[-1].shape[1]
    d_ff = w2.shape[0]
    n_next = w_next.shape[1]
    mix = lambda s: jnp.minimum(s, tiles - 1)
    ffn = lambda s: jnp.clip(s - 1, 0, tiles - 1)
    nxt = lambda s: jnp.maximum(s - 2, 0)
    state_spec = lambda a: pl.BlockSpec((cpt,) + a.shape[1:], lambda s: (mix(s),) + (0,) * (a.ndim - 1))
    def sel(t):
        b, j = _div_small(t, tpb, nb)
        return b * 2 + jnp.minimum(j, 1), 0, 0

    wc_in, wc_out, wc_shapes = _wcast_specs(*ffn_w, 1, tiles + 2)
    return pl.pallas_call(
        functools.partial(_mixer_ffn_kernel, tiles=tiles, tpb=tpb, d_ff=d_ff),
        grid=(tiles + 2,),
        in_specs=[_smem_spec(),
                  pl.BlockSpec((tm, 6 * gw), lambda s: (mix(s), 0)),
                  pl.BlockSpec((tm, 2 * gw), lambda s: (mix(s), 0)),
                  pl.BlockSpec((tm, LANES), lambda s: (mix(s), 0)),
                  pl.BlockSpec((32, tm), lambda s: (0, mix(s))),
                  state_spec(sretb), state_spec(cmb), state_spec(nmb), state_spec(mmb),
                  _const_spec(lgk.shape), _const_spec(lgkt.shape), _const_spec(nw.shape)]
                 + _split_row_specs(tm, d, nb, tpb, ffn)
                 + [pl.BlockSpec((None, 8, d), lambda s: sel(ffn(s))),
                    _const_spec(wo.shape), _const_spec(wi.shape), _const_spec(w2.shape),
                    pl.BlockSpec((None, 2, d), lambda s: sel(nxt(s))),
                    _const_spec(w_next.shape), _const_spec(nw_next.shape),
                    pl.BlockSpec((tm, LANES), lambda s: (_div_small(nxt(s), tpb, nb)[1], 0)),
                    pl.BlockSpec((tm, LANES), lambda s: (_div_small(nxt(s), tpb, nb)[1], 0))] + wc_in,
        out_specs=[pl.BlockSpec((tm, d), lambda s: (ffn(s), 0)),
                   pl.BlockSpec((tm, n_next), lambda s: (nxt(s), 0))] + wc_out,
        out_shape=[jax.ShapeDtypeStruct((tiles * tm, d), F32),
                   jax.ShapeDtypeStruct((tiles * tm, n_next), BF16)] + wc_shapes,
        scratch_shapes=[pltpu.VMEM((4, CHUNK, LANES), F32), pltpu.VMEM((4, CHUNK, LANES), F32),
                        pltpu.VMEM((4, HALO, LANES), F32), pltpu.VMEM((8, LANES), F32),
                        pltpu.VMEM((8, CHUNK, LANES), F32), pltpu.VMEM((3, 4, CHUNK, LANES), F32),
                        pltpu.VMEM((tm, 2 * gw), BF16), pltpu.VMEM((tm, d_ff), BF16),
                        pltpu.VMEM((tm, d), F32)],
        compiler_params=_cparams("arbitrary"),
        name="ret_mlstm_mixer_ffn",
    )(lg_smem, y, qk_act, g, gt, sretb, cmb, nmb, mmb, lgk, lgkt, nw,
      *xs, prm, wo, wi, w2, ab_next, w_next, nw_next, cos, sin, *ffn_w)


def _ffn_in(x, m, p_ref, wo_ref):
    x1 = x + p_ref[0:1, :] * _dot(m, wo_ref[...])
    ms = jnp.mean(x1 * x1, axis=-1, keepdims=True)
    return x1, ((x1 * lax.rsqrt(ms + EPS)) * p_ref[1:2, :] + p_ref[2:3, :]).astype(BF16)


def _ffn_cols(h, wi_ref, act_scr, lo, hi, d_ff):
    gate = _dot(h, wi_ref[:, lo:hi])
    up = _dot(h, wi_ref[:, d_ff + lo:d_ff + hi])
    act_scr[:, lo:hi] = (gate * jax.nn.sigmoid(gate) * up).astype(BF16)


def _window_bias():
    kk = np.arange(CHUNK)[:, None]
    t = np.arange(CHUNK)[None, :]
    tabs = []
    for has_prev, has_next in ((False, True), (True, True), (True, False)):
        prev_ok = (kk >= t) & has_prev
        next_ok = (kk <= t) & has_next
        tabs.append(np.where(np.concatenate([prev_ok, next_ok], axis=0), 0.0, NEG))
    return jnp.asarray(np.stack(tabs), F32)


def _attn_ffn_kernel(sink_ref, qkv_a_ref, qkv_b_ref, kvp_ref, kvn_ref, kvx_ref, bias_ref,
                     x_a_ref, x_b_ref, p_ref, wo_ref, wi_ref, w2_ref, o_ref, m_scr, act_scr,
                     *, steps, per_batch, d_ff):
    s = pl.program_id(0)
    kw = 2 * LANES
    k0 = 8 * LANES
    tm = qkv_a_ref.shape[0]
    nblk = 2 * tm // CHUNK

    @pl.when(s == 0)
    def _():
        m_scr[...] = jnp.zeros_like(m_scr)

    j = _div_small(jnp.minimum(s, steps - 1), per_batch, steps // per_batch)[1]
    grp = lax.broadcasted_iota(jnp.int32, (1, 4 * CHUNK), 1) // CHUNK
    lane = _lane()
    mask_q = [(lane & 32) == 0, (lane & 32) != 0]
    tile4 = lambda b: jnp.concatenate([b] * 4, axis=1)
    inner = tile4(bias_ref[1])
    biases = ([tile4(jnp.where(j == 0, bias_ref[0], bias_ref[1]))] + [inner] * (nblk - 2)
              + [tile4(jnp.where(j == per_batch - 1, bias_ref[2], bias_ref[1]))])
    cuts = _ffn_splits(d_ff, nblk - 1)

    def ffn_up(i):
        _ffn_cols(h, wi_ref, act_scr, cuts[i], cuts[i + 1], d_ff)

    def q_rows(blk):
        ref = qkv_a_ref if blk * CHUNK < tm else qkv_b_ref
        r0 = (blk * CHUNK) % tm
        return ref, slice(r0, r0 + CHUNK)

    def scores(blk, kv):
        kvp, a = kv // 2, kv % 2
        ref, rows = q_rows(blk)
        qs = jnp.concatenate(
            [jnp.where(mask_q[a], ref[rows, (kvp * 4 + g) * LANES:(kvp * 4 + g + 1) * LANES],
                       jnp.zeros((CHUNK, LANES), BF16)) for g in range(4)], axis=0)
        return _dot_nt(kcats[kvp][blk], qs)

    def softmax_pv(blk, kv, st):
        bias = biases[blk]
        st = jnp.concatenate([st[0:CHUNK] + bias[0:CHUNK], st[CHUNK:2 * CHUNK],
                              st[2 * CHUNK:3 * CHUNK] + bias[CHUNK:2 * CHUNK], st[3 * CHUNK:]], axis=0)
        snk = jnp.where(grp == 0, sink_ref[kv * 4],
                        jnp.where(grp == 1, sink_ref[kv * 4 + 1],
                                  jnp.where(grp == 2, sink_ref[kv * 4 + 2], sink_ref[kv * 4 + 3])))
        m = jnp.maximum(jnp.max(st, axis=0, keepdims=True), snk)
        e = jnp.exp2(st - m)
        denom = jnp.exp2(snk - m) + jnp.sum(e, axis=0, keepdims=True)
        a = kv % 2
        return _dot(vts[kv // 2][blk][a * HEAD_DIM:(a + 1) * HEAD_DIM, :], e.astype(BF16)) * (1.0 / denom)

    def hand_over(blk, outs):
        for kvp in range(2):
            full = jnp.concatenate(outs[2 * kvp:2 * kvp + 2], axis=0)
            for g in range(4):
                m_scr[blk * CHUNK:(blk + 1) * CHUNK, (kvp * 4 + g) * LANES:(kvp * 4 + g + 1) * LANES] = (
                    full[:, g * CHUNK:(g + 1) * CHUNK].T.astype(BF16))

    x1, h = _ffn_in(jnp.concatenate([x_a_ref[...], x_b_ref[...]], axis=0), m_scr[...], p_ref, wo_ref)

    kcats, vts = [], []
    for kvp in range(2):
        ks = slice(k0 + kvp * LANES, k0 + (kvp + 1) * LANES)
        vs = slice(k0 + kw + kvp * LANES, k0 + kw + (kvp + 1) * LANES)
        es, ev = slice(kvp * LANES, (kvp + 1) * LANES), slice(kw + kvp * LANES, kw + (kvp + 1) * LANES)
        tile_rows = [(ref, slice(c * CHUNK, (c + 1) * CHUNK)) for ref in (qkv_a_ref, qkv_b_ref)
                     for c in range(tm // CHUNK)]
        k_chunks = [kvp_ref[:, es]] + [ref[rows, ks] for ref, rows in tile_rows] + [kvn_ref[:, es]]
        v_chunks = ([kvp_ref[:, ev]] + [ref[rows, vs] for ref, rows in tile_rows] + [kvn_ref[:, ev]]
                    + [kvx_ref[c * CHUNK:(c + 1) * CHUNK, ev] for c in range(kvx_ref.shape[0] // CHUNK)])
        v_t = [v.astype(F32).T.astype(BF16) for v in v_chunks]
        kcats.append([jnp.concatenate(k_chunks[b:b + 3] + [kvx_ref[:, es]], axis=0) for b in range(nblk)])
        vts.append([jnp.concatenate(v_t[b:b + 3] + v_t[nblk + 2:], axis=1) for b in range(nblk)])

    for blk in range(nblk):
        sts = [scores(blk, kv) for kv in range(4)]
        if blk < nblk - 1:
            ffn_up(blk)
        else:
            o_ref[...] = x1 + p_ref[3:4, :] * _dot(act_scr[...], w2_ref[...])
        hand_over(blk, [softmax_pv(blk, kv, sts[kv]) for kv in range(4)])


def _attn_ffn(y, sink, xc, prm, wo, wi, w2, nb, nc, ctx_chunks):
    tm = ROW_TILE
    cpt = tm // CHUNK
    tpb = nc // cpt
    lat_tiles = (nc - ctx_chunks) // cpt
    ctx_tiles = ctx_chunks // cpt
    assert ctx_tiles == 1 and lat_tiles % 2 == 0 and lat_tiles >= 4
    per_batch = lat_tiles // 2
    steps = nb * per_batch
    d = xc.shape[1]
    d_ff = w2.shape[0]
    bias = _window_bias()
    n_qkv = y.shape[1]
    kv_w = n_qkv - d
    assert d % kv_w == 0
    kv_col = d // kv_w
    att = lambda s: jnp.minimum(s, steps - 1)
    ffn = lambda s: jnp.maximum(s - 1, 0)
    split = lambda t: _div_small(t, per_batch, nb)
    row_tile = lambda t, u: split(t)[0] * tpb + ctx_tiles + 2 * split(t)[1] + u
    chunk0 = lambda t: split(t)[0] * nc + ctx_chunks
    prev_c = lambda s: chunk0(att(s)) + jnp.maximum(split(att(s))[1] * 2 * cpt - 1, 0)
    next_c = lambda s: chunk0(att(s)) + jnp.minimum(split(att(s))[1] * 2 * cpt + 2 * cpt, lat_tiles * cpt - 1)
    edge_spec = lambda f: pl.BlockSpec((CHUNK, kv_w), lambda s: (f(s), kv_col))
    return pl.pallas_call(
        functools.partial(_attn_ffn_kernel, steps=steps, per_batch=per_batch, d_ff=d_ff),
        grid=(steps + 1,),
        in_specs=[_smem_spec(),
                  pl.BlockSpec((tm, n_qkv), lambda s: (row_tile(att(s), 0), 0)),
                  pl.BlockSpec((tm, n_qkv), lambda s: (row_tile(att(s), 1), 0)),
                  edge_spec(prev_c), edge_spec(next_c),
                  pl.BlockSpec((tm, kv_w), lambda s: (split(att(s))[0] * tpb, kv_col)),
                  _const_spec(bias.shape),
                  pl.BlockSpec((tm, d), lambda s: (row_tile(ffn(s), 0), 0)),
                  pl.BlockSpec((tm, d), lambda s: (row_tile(ffn(s), 1), 0)),
                  pl.BlockSpec((None, 8, d), lambda s: (split(ffn(s))[0] * 2 + 1, 0, 0)),
                  _const_spec(wo.shape), _const_spec(wi.shape), _const_spec(w2.shape)],
        out_specs=pl.BlockSpec((2 * tm, d), lambda s: (ffn(s), 0)),
        out_shape=jax.ShapeDtypeStruct((steps * 2 * tm, d), F32),
        scratch_shapes=[pltpu.VMEM((2 * tm, d), BF16), pltpu.VMEM((2 * tm, d_ff), BF16)],
        compiler_params=_cparams("arbitrary"),
        name="window_gqa_ffn",
    )(sink, y, y, y, y, y, bias, xc, xc, prm, wo, wi, w2)


def _pair_cols(w):
    rows, cols = w.shape
    return w.reshape(rows, cols // LANES, 2, 2, 32).transpose(0, 1, 3, 2, 4).reshape(rows, cols)


def _attn_q_cols(w):
    rows = w.shape[0]
    g_per = w.shape[1] // (H_KV * HEAD_DIM)
    return (w.reshape(rows, H_KV // 2, 2, g_per, 2, 32).transpose(0, 1, 3, 4, 2, 5)
            .reshape(rows, w.shape[1]))


def _attn_o_rows(w):
    cols = w.shape[1]
    g_per = w.shape[0] // (H_KV * HEAD_DIM)
    return (w.reshape(H_KV // 2, 2, g_per, HEAD_DIM, cols).transpose(0, 2, 1, 3, 4)
            .reshape(w.shape[0], cols))


def _rope_tables(seq, ctx_len):
    rows = seq // GRID_W
    row = np.repeat(np.arange(rows, dtype=np.float32), GRID_W)
    col = np.tile(np.arange(GRID_W, dtype=np.float32), rows)
    n = HEAD_DIM // 4
    inv = (np.float32(ROPE_BASE) ** (-np.arange(n, dtype=np.float32) / np.float32(n))).astype(np.float32)
    ang = np.concatenate([row[:, None] * inv, col[:, None] * inv], axis=-1).astype(np.float32)
    cos, sin = np.cos(ang), np.sin(ang)
    cos_t = np.concatenate([np.ones((ctx_len, LANES), np.float32), np.tile(cos, (1, 4))], axis=0)
    sin_t = np.concatenate([np.zeros((ctx_len, LANES), np.float32),
                            np.concatenate([-sin, -sin, sin, sin], axis=-1)], axis=0)
    return jnp.asarray(cos_t, F32), jnp.asarray(sin_t, F32)


def _mod_tables(mod, nb, norm_w):
    d = norm_w.shape[-1]
    lat = mod[:nb].reshape(nb, 6, d)
    ctx = jnp.broadcast_to(mod[nb].reshape(1, 6, d), (nb, 6, d))
    both = jnp.stack([ctx, lat], axis=1).reshape(nb * 2, 6, d)
    sh1, sc1, g1, sh2, sc2, g2 = [both[:, k] for k in range(6)]
    ab1 = jnp.stack([norm_w[0] * (1.0 + sc1), sh1], axis=1)
    zeros = jnp.zeros_like(g1)
    prm = jnp.stack([g1, norm_w[1] * (1.0 + sc2), sh2, g2, zeros, zeros, zeros, zeros], axis=1)
    return ab1, prm


def kernel(x, c, ctx, c_ctx, ada_w, ada_b, norm_w, ffn_w_in, ffn_w_out, ab_w_in, ab_w_out,
           ret_log_gamma, ret_norm_w, mlstm_conv_w, mlstm_conv_b, mlstm_gate_b, mlstm_norm_w,
           attn_w_in, attn_w_out, attn_q_norm_w, attn_k_norm_w, attn_sink):
    nb, seq, d = x.shape
    ctx_len = ctx.shape[1]
    depth = ada_w.shape[0]
    assert ctx_len == ROW_TILE and seq % ROW_TILE == 0 and d == 8 * LANES and nb < 8
    t_all = ctx_len + seq
    nc = t_all // CHUNK
    ctx_chunks = ctx_len // CHUNK
    tpb = t_all // ROW_TILE
    dr = d // 2

    rows = jnp.zeros((8, d), F32).at[:nb].set(c).at[nb].set(c_ctx)
    mod_all = _modulation(rows, ada_w, ada_b)
    cos_t, sin_t = _rope_tables(seq, ctx_len)
    ffn_w = (ffn_w_in, ffn_w_out)
    xs = (ctx.reshape(nb * ctx_len, d), x.reshape(nb * seq, d))

    assert depth == 2 and ab_w_in.shape[0] == 1 and attn_w_in.shape[0] == 1
    ab_0, prm_0 = _mod_tables(mod_all[0], nb, norm_w[0])
    ab_1, prm_1 = _mod_tables(mod_all[1], nb, norm_w[1])

    w = ab_w_in[0]
    w_main = jnp.concatenate([_pair_cols(w[:, :dr]), _pair_cols(w[:, dr:2 * dr]), w[:, 2 * dr:8 * dr]],
                             axis=1).astype(BF16)
    wg = jnp.zeros((d, LANES), F32).at[:, :32].set(w[:, 8 * dr:]).astype(BF16)
    wgt = w[:, 8 * dr:].T.astype(BF16)
    gb = jnp.zeros((1, LANES), F32).at[0, :32].set(mlstm_gate_b[0].reshape(-1))
    gbt = mlstm_gate_b[0].reshape(32, 1)
    lg = ret_log_gamma[0].astype(F32)
    lgk = jnp.tile(jnp.repeat(lg.reshape(2, 4, 2), 32, axis=-1), (1, 1, 2)).reshape(8, LANES)
    cw = jnp.concatenate([mlstm_conv_w[0], mlstm_conv_b[0][None], jnp.zeros((4, 2 * dr), F32)], axis=0)
    nw = jnp.broadcast_to(jnp.concatenate([ret_norm_w[0], mlstm_norm_w[0]]).reshape(8, LANES, 1),
                          (8, LANES, LANES))
    wo_0 = ab_w_out[0].astype(BF16)

    w = attn_w_in[0]
    w_attn = jnp.concatenate([_attn_q_cols(w[:, :d]), _pair_cols(w[:, d:d + 2 * LANES]), w[:, d + 2 * LANES:]],
                             axis=1).astype(BF16)
    lane_w = lambda v: jnp.concatenate([v[:32], v[:32], v[32:], v[32:]])
    nwq = jnp.stack([lane_w(attn_q_norm_w[0]) * (HEAD_DIM ** -0.5 * LOG2E), lane_w(attn_k_norm_w[0])]
                    + [jnp.zeros((LANES,), F32)] * 6)
    wo_1 = _attn_o_rows(attn_w_out[0]).astype(BF16)

    y, g, gt, *states, wi_0, w2_0 = _inproj_sweep(*xs, ab_0, w_main, wg, wgt, gb, gbt, cos_t, sin_t, lgk, cw,
                                                  ffn_w, nb, tpb)
    x_mid, y_attn, wi_1, w2_1 = _mixer_ffn(y, g, gt, states, lg.reshape(-1), lgk, lgk.T, nw, xs, prm_0, wo_0,
                                           wi_0, w2_0, ab_1, w_attn, nwq, cos_t, sin_t, ffn_w, nb, tpb)
    out = _attn_ffn(y_attn, attn_sink[0].astype(F32) * LOG2E, x_mid, prm_1, wo_1, wi_1, w2_1, nb, nc, ctx_chunks)
    return out.reshape(nb, seq, d)
```

```python
import functools

import numpy as np
import jax
import jax.numpy as jnp
from jax import lax
from jax.experimental import pallas as pl
from jax.experimental.pallas import tpu as pltpu

F32 = jnp.float32
BF16 = jnp.bfloat16

HEAD_DIM = 64
CHUNK = 128
GRID_W = 64
ROPE_BASE = 10000.0
EPS = 1e-6
H_KV = 4
LANES = 128
ROW_TILE = 256
HALO = 16
LOG2E = 1.4426950408889634
NEG = -1e30
VMEM_LIMIT = 56 * 1024 * 1024


def _cparams(*sem):
    return pltpu.CompilerParams(dimension_semantics=sem, vmem_limit_bytes=VMEM_LIMIT)


def _const_spec(shape):
    nd = len(shape)
    return pl.BlockSpec(shape, lambda *_: (0,) * nd, pipeline_mode=pl.Buffered(1))


def _smem_spec():
    return pl.BlockSpec(memory_space=pltpu.SMEM)


def _wcast_specs(w_in_all, w_out_all, layer, steps):
    specs_in, specs_out, shapes = [], [], []
    for w in (w_in_all, w_out_all):
        total, cols = w.shape[1:]
        rows = next(r for r in range(HALO, total + 1, HALO) if total % r == 0 and total // r <= steps)
        n_blocks = total // rows
        specs_in.append(pl.BlockSpec((None, rows, cols),
                                     lambda i, n=n_blocks: (layer, jnp.minimum(i, n - 1), 0)))
        specs_out.append(pl.BlockSpec((rows, cols), lambda i, n=n_blocks: (jnp.minimum(i, n - 1), 0)))
        shapes.append(jax.ShapeDtypeStruct(w.shape[1:], BF16))
    return specs_in, specs_out, shapes


def _wcast(src_refs, dst_refs):
    for src, dst in zip(src_refs, dst_refs):
        dst[...] = src[...].astype(BF16)


def _lane(shape=(CHUNK, LANES)):
    return lax.broadcasted_iota(jnp.int32, shape, len(shape) - 1)


def _dot(a, b):
    return jnp.dot(a, b, preferred_element_type=F32)


def _dot_nt(a, b):
    return lax.dot_general(a, b, (((1,), (1,)), ((), ())), preferred_element_type=F32)


def _div_small(t, m, n):
    q = 0
    for b in range(1, n):
        q = q + jnp.where(t >= b * m, 1, 0)
    return q, t - q * m


def _split3(x):
    hi = x.astype(BF16)
    r = x - hi.astype(F32)
    mid = r.astype(BF16)
    lo = (r - mid.astype(F32)).astype(BF16)
    return hi, mid, lo


def _log_sigmoid(x):
    return jnp.minimum(x, 0.0) - jnp.log1p(jnp.exp(-jnp.abs(x)))


def _rope(x, cos, sin_signed):
    return x * cos + pltpu.roll(x, LANES // 2, 1) * sin_signed


def _mod_kernel(rows_ref, w_ref, b_ref, o_ref):
    a = rows_ref[...]
    a = a * jax.nn.sigmoid(a)
    a_hi = a.astype(BF16)
    a_lo = (a - a_hi.astype(F32)).astype(BF16)
    w = w_ref[...]
    w_hi = w.astype(BF16)
    w_lo = (w - w_hi.astype(F32)).astype(BF16)
    o_ref[...] = _dot(a_hi, w_hi) + _dot(a_hi, w_lo) + _dot(a_lo, w_hi) + b_ref[...]


def _modulation(rows, ada_w, ada_b):
    depth, d, n = ada_w.shape
    tn = n // 4
    return pl.pallas_call(
        _mod_kernel,
        grid=(depth, n // tn),
        in_specs=[pl.BlockSpec((8, d), lambda l, j: (0, 0)),
                  pl.BlockSpec((None, d, tn), lambda l, j: (l, 0, j)),
                  pl.BlockSpec((None, 1, tn), lambda l, j: (l, 0, j))],
        out_specs=pl.BlockSpec((None, 8, tn), lambda l, j: (l, 0, j)),
        out_shape=jax.ShapeDtypeStruct((depth, 8, n), F32),
        compiler_params=_cparams("arbitrary", "arbitrary"),
        name="adaln_modulation",
    )(rows, ada_w, ada_b.reshape(depth, 1, n))


def _norm_mod(x, ab_ref):
    ms = jnp.mean(x * x, axis=-1, keepdims=True)
    h = (x * lax.rsqrt(ms + EPS)) * ab_ref[0:1, :] + ab_ref[1:2, :]
    return h.astype(BF16)


def _ctx_or_latent_rows(ctx_ref, x_ref, tile, nb, tiles_per_batch):
    return jnp.where(_div_small(tile, tiles_per_batch, nb)[1] == 0, ctx_ref[...], x_ref[...])


def _split_row_specs(tm, d, nb, tpb, tile_of):
    lat = tpb - 1

    def latent_row(i):
        b, j = _div_small(tile_of(i), tpb, nb)
        return b * lat + jnp.maximum(j - 1, 0), 0

    return [pl.BlockSpec((tm, d), lambda i: (_div_small(tile_of(i), tpb, nb)[0], 0)),
            pl.BlockSpec((tm, d), latent_row)]


def _cumsum_cols(tri_bf, lf):
    hi, mid, lo = _split3(lf)
    return _dot(tri_bf, hi) + _dot(tri_bf, mid) + _dot(tri_bf, lo)


def _cumsum_rows(lf, tri_bf):
    hi, mid, lo = _split3(lf)
    return _dot(hi, tri_bf) + _dot(mid, tri_bf) + _dot(lo, tri_bf)


def _ret_state_update(s_ref, p, k2, vt, kdec, cd_lanes, bd):
    kf = (k2.astype(F32) * kdec).astype(BF16)
    s_ref[p] = s_ref[p] * cd_lanes + jnp.where(bd, _dot(vt, kf), 0.0)


def _mlstm_state_update(c_ref, n_ref, m_ref, k_pairs, vt_pairs, c_all, bend, col0, lo, bd):
    cmax = jnp.max(c_all, axis=0, keepdims=True)
    w_all = jnp.exp(c_all - cmax)
    m_old = m_ref[0:1, :]
    mrel = jnp.maximum(m_old, cmax)
    a_row = jnp.exp(m_old - mrel)
    bb_row = jnp.exp(cmax - mrel)
    m_ref[0:1, :] = bend + mrel
    lo_row = lo[0:1, :]
    for p in range(4):
        h0 = col0 + 2 * p
        kw = k_pairs[p] * jnp.where(lo, w_all[:, h0:h0 + 1], w_all[:, h0 + 1:h0 + 2])
        kvt = _dot(vt_pairs[p], kw.astype(BF16))
        nloc = jnp.sum(kw, axis=0, keepdims=True)
        a_l = jnp.where(lo_row, a_row[:, h0:h0 + 1], a_row[:, h0 + 1:h0 + 2])
        bb_l = jnp.where(lo_row, bb_row[:, h0:h0 + 1], bb_row[:, h0 + 1:h0 + 2])
        c_ref[p] = c_ref[p] * a_l + jnp.where(bd, kvt, 0.0) * bb_l
        n_new = (n_ref[p, 0:1, :] + n_ref[p, 1:2, :]) * a_l + nloc * bb_l
        n_ref[p, 0:1, :] = jnp.where(lo_row, n_new, 0.0)
        n_ref[p, 1:2, :] = jnp.where(lo_row, 0.0, n_new)


def _mlstm_dir_weights(st, qn_row, c_col, bt_row, m_prev, tri):
    dl = jnp.where(tri, c_col + bt_row, NEG)
    mx = jnp.max(dl, axis=0, keepdims=True)
    al = bt_row + m_prev
    m_t = jnp.maximum(al, mx)
    w = jnp.exp2(dl - m_t)
    a_t = jnp.exp2(al - m_t)
    sw = st * w
    den = jnp.sum(sw, axis=0, keepdims=True) + a_t * qn_row
    r = 1.0 / jnp.maximum(jnp.abs(den), jnp.exp2(-m_t))
    return sw * r, a_t * r


def _heads_out(ht, nw_tab):
    rows = []
    for a in range(2):
        ha = ht[a * HEAD_DIM:(a + 1) * HEAD_DIM, :]
        ms = jnp.mean(ha * ha, axis=0, keepdims=True)
        rows.append(ha * lax.rsqrt(ms + EPS))
    return (jnp.concatenate(rows, axis=0) * nw_tab).T


def _inproj_sweep_kernel(ctx_ref, x_ref, ab_ref, wqk_ref, wrest_ref, wg_ref, wgt_ref, gb_ref, gbt_ref, cos_ref, sin_ref,
                         lgk_ref, cw_ref, wi_src, w2_src,
                         y_ref, g_ref, gt_ref, sret_ref, cm_ref, nm_ref, mm_ref, qk_ref, wi_dst, w2_dst,
                         s_scr, c_scr, n_scr, m_scr, kdec_scr,
                         p_rk, p_rv, p_mv, p_mqk, p_g, next_row_scr, *, tiles, tpb):
    i = pl.program_id(0)
    _wcast((wi_src, w2_src), (wi_dst, w2_dst))

    def order(t):
        v = _div_small(t, tpb, tiles // tpb)[1]
        return jnp.where(v == 0, 0, tpb - v)

    jt_a = order(jnp.maximum(i - 1, 0))
    lane = _lane()
    sub = lax.broadcasted_iota(jnp.int32, (CHUNK, LANES), 0)
    lo = lane < HEAD_DIM
    lo_row = lo[0:1, :]
    bd_ret = (sub >= HEAD_DIM) == ((lane & 32) != 0)
    bd_m = (sub >= HEAD_DIM) == (lane >= HEAD_DIM)
    gw = 4 * LANES

    @pl.when(i == 0)
    def _():
        pos = sub.astype(F32)
        for p in range(4):
            kdec_scr[p] = jnp.exp(lgk_ref[4 + p:5 + p, :] * pos)
        for ref in (p_rk, p_rv, p_mv, p_mqk, p_g, next_row_scr):
            ref[...] = jnp.zeros_like(ref)

    @pl.when(jt_a == 0)
    def _():
        s_scr[...] = jnp.zeros_like(s_scr)
        c_scr[...] = jnp.zeros_like(c_scr)
        n_scr[...] = jnp.zeros_like(n_scr)
        m_scr[...] = jnp.zeros_like(m_scr)

    tile_i = jnp.minimum(i, tiles - 1)
    jt_i = order(tile_i)
    hb = _norm_mod(jnp.where(jt_i == 0, ctx_ref[...], x_ref[...]), ab_ref)
    cos, sin = cos_ref[...], sin_ref[...]

    def project(j):
        w_cols = wqk_ref[:, j * gw:(j + 1) * gw] if j < 2 else wrest_ref[:, (j - 2) * gw:(j - 1) * gw]
        acc = _dot(hb, w_cols)
        if j in (0, 1):
            if j == 0:
                acc = acc * (HEAD_DIM ** -0.5)
            acc = jnp.concatenate([_rope(acc[:, p * LANES:(p + 1) * LANES], cos, sin) for p in range(4)], axis=1)
        elif j == 3:
            acc = acc * jax.nn.sigmoid(acc)
        elif j == 7:
            acc = jax.nn.sigmoid(acc)
        return acc.astype(BF16)

    raw_q, raw_k = project(4), project(5)

    prev_on = jnp.where(jt_a <= 1, 0.0, 1.0).astype(F32)
    next_on = jnp.where((jt_a == 0) | (jt_a == tpb - 1), 0.0, 1.0).astype(F32)
    cur = p_mqk[...].astype(F32)
    n = cur.shape[0]
    row = lax.broadcasted_iota(jnp.int32, cur.shape, 0)
    prev_row = jnp.concatenate([raw_q[n - HALO:, :], raw_k[n - HALO:, :]], axis=1)[HALO - 1:HALO, :].astype(F32)
    xm = jnp.where(row == 0, prev_row * prev_on, pltpu.roll(cur, 1, 0))
    xp = jnp.where(row == n - 1, next_row_scr[0:1, :] * next_on, pltpu.roll(cur, n - 1, 0))
    conv = cw_ref[3:4, :] + cw_ref[0:1, :] * xm + cw_ref[1:2, :] * cur + cw_ref[2:3, :] * xp
    qk = conv * jax.nn.sigmoid(conv)
    qk_ref[:, 0:gw] = (qk[:, 0:gw] * (HEAD_DIM ** -0.5)).astype(BF16)
    qk_ref[:, gw:] = qk[:, gw:].astype(BF16)
    le_bf = (sub <= lane).astype(BF16)
    chunks = (1, 0)
    pre = {}
    for blk in chunks:
        rows = slice(blk * CHUNK, (blk + 1) * CHUNK)
        g = p_g[rows, :]
        bal = pltpu.roll(_cumsum_cols(le_bf, _log_sigmoid(g)), LANES - 8, 1)
        c_all = g - bal
        cmax = jnp.max(c_all, axis=0, keepdims=True)
        w_all = jnp.exp(c_all - cmax)
        vts, ks, nlocs = [], [], []
        for p in range(8):
            sl = slice((p % 4) * LANES, (p % 4 + 1) * LANES)
            if p < 4:
                v2 = p_rv[rows, sl]
                ks.append((p_rk[rows, sl].astype(F32) * kdec_scr[p]).astype(BF16))
            else:
                v2 = p_mv[rows, sl]
                h0 = 16 + 2 * (p - 4)
                kw = qk[rows, gw + (p - 4) * LANES:gw + (p - 3) * LANES] * jnp.where(
                    lo, w_all[:, h0:h0 + 1], w_all[:, h0 + 1:h0 + 2])
                ks.append(kw.astype(BF16))
                nlocs.append(jnp.sum(kw, axis=0, keepdims=True))
            vts.append(v2.astype(F32).T.astype(BF16))
        pre[blk] = (vts, ks, nlocs, cmax, bal[0:1, :])

    cur_rk = project(1)
    y_ref[:, 1 * gw:2 * gw] = cur_rk
    y_ref[:, 0:gw] = project(0)
    kvs = {blk: [_dot(pre[blk][0][p], pre[blk][1][p]) for p in range(8)] for blk in chunks}
    cur_rv = project(2)
    y_ref[:, 2 * gw:3 * gw] = cur_rv
    y_ref[:, 3 * gw:4 * gw] = project(3)

    for blk in chunks:
        _, _, nlocs, cmax, bend = pre[blk]
        sret_ref[blk] = s_scr[...].astype(BF16)
        cm_ref[blk] = c_scr[...].astype(BF16)
        nm_ref[blk] = n_scr[...].astype(BF16)
        mm_ref[blk] = m_scr[...]
        m_old = m_scr[0:1, :]
        mrel = jnp.maximum(m_old, cmax)
        a_row = jnp.exp(m_old - mrel)
        bb_row = jnp.exp(cmax - mrel)
        m_scr[0:1, :] = bend + mrel
        for p in range(4):
            cd = jnp.exp(lgk_ref[4 + p:5 + p, :] * float(CHUNK))
            s_scr[p] = s_scr[p] * cd + jnp.where(bd_ret, kvs[blk][p], 0.0)
            h0 = 16 + 2 * p
            a_l = jnp.where(lo_row, a_row[:, h0:h0 + 1], a_row[:, h0 + 1:h0 + 2])
            bb_l = jnp.where(lo_row, bb_row[:, h0:h0 + 1], bb_row[:, h0 + 1:h0 + 2])
            c_scr[p] = c_scr[p] * a_l + jnp.where(bd_m, kvs[blk][4 + p], 0.0) * bb_l
            n_new = (n_scr[p, 0:1, :] + n_scr[p, 1:2, :]) * a_l + nlocs[p] * bb_l
            n_scr[p, 0:1, :] = jnp.where(lo_row, n_new, 0.0)
            n_scr[p, 1:2, :] = jnp.where(lo_row, 0.0, n_new)

    cur_mv = project(6)
    y_ref[:, 4 * gw:5 * gw] = cur_mv
    y_ref[:, 5 * gw:6 * gw] = project(7)
    gates = _dot(hb, wg_ref[...]) + gb_ref[...]
    g_ref[...] = gates
    gt_ref[...] = _dot_nt(wgt_ref[...], hb) + gbt_ref[...]
    next_row_scr[...] = p_mqk[0:HALO, :].astype(F32)
    p_mqk[:, 0:gw] = raw_q
    p_mqk[:, gw:] = raw_k
    p_rk[...] = cur_rk
    p_rv[...] = cur_rv
    p_mv[...] = cur_mv
    p_g[...] = gates


def _inproj_sweep(ctx2, x2, ab, w_qk, w_rest, wg, wgt, gb, gbt, cos, sin, lgk, cw, ffn_w, nb, tpb):
    d = x2.shape[1]
    tm = ROW_TILE
    cpt = tm // CHUNK
    gw = 4 * LANES
    tiles = nb * tpb
    r = tiles * tm
    lat = tpb - 1
    def visit(t):
        b, v = _div_small(t, tpb, nb)
        return b, jnp.where(v == 0, 0, tpb - v)

    cur = lambda i: visit(jnp.minimum(i, tiles - 1))
    flat = lambda bj: bj[0] * tpb + bj[1]
    tile_i = lambda i: flat(cur(i))
    tile_a = lambda i: flat(visit(jnp.maximum(i - 1, 0)))
    in_batch = lambda i: cur(i)[1]
    sel = lambda i: (cur(i)[0] * 2 + jnp.minimum(cur(i)[1], 1), 0, 0)
    state = lambda *dims: pl.BlockSpec((cpt,) + dims, lambda i: (tile_a(i),) + (0,) * len(dims))
    nchunks = tiles * cpt
    wc_in, wc_out, wc_shapes = _wcast_specs(*ffn_w, 0, tiles + 1)
    return pl.pallas_call(
        functools.partial(_inproj_sweep_kernel, tiles=tiles, tpb=tpb),
        grid=(tiles + 1,),
        in_specs=[pl.BlockSpec((tm, d), lambda i: (cur(i)[0], 0)),
                  pl.BlockSpec((tm, d), lambda i: (cur(i)[0] * lat + jnp.maximum(cur(i)[1] - 1, 0), 0)),
                  pl.BlockSpec((None, 2, d), sel),
                  _const_spec(w_qk.shape), _const_spec(w_rest.shape), _const_spec(wg.shape), _const_spec(wgt.shape),
                  _const_spec(gb.shape), _const_spec(gbt.shape),
                  pl.BlockSpec((tm, LANES), lambda i: (in_batch(i), 0)),
                  pl.BlockSpec((tm, LANES), lambda i: (in_batch(i), 0)),
                  _const_spec(lgk.shape), _const_spec(cw.shape)] + wc_in,
        out_specs=[pl.BlockSpec((tm, 6 * gw), lambda i: (tile_i(i), 0)),
                   pl.BlockSpec((tm, LANES), lambda i: (tile_i(i), 0)),
                   pl.BlockSpec((32, tm), lambda i: (0, tile_i(i))),
                   state(4, CHUNK, LANES), state(4, CHUNK, LANES), state(4, HALO, LANES), state(8, LANES),
                   pl.BlockSpec((tm, 2 * gw), lambda i: (tile_a(i), 0))] + wc_out,
        out_shape=[jax.ShapeDtypeStruct((r, 6 * gw), BF16),
                   jax.ShapeDtypeStruct((r, LANES), F32),
                   jax.ShapeDtypeStruct((32, r), F32),
                   jax.ShapeDtypeStruct((nchunks, 4, CHUNK, LANES), BF16),
                   jax.ShapeDtypeStruct((nchunks, 4, CHUNK, LANES), BF16),
                   jax.ShapeDtypeStruct((nchunks, 4, HALO, LANES), BF16),
                   jax.ShapeDtypeStruct((nchunks, 8, LANES), F32),
                   jax.ShapeDtypeStruct((r, 2 * gw), BF16)] + wc_shapes,
        scratch_shapes=[pltpu.VMEM((4, CHUNK, LANES), F32), pltpu.VMEM((4, CHUNK, LANES), F32),
                        pltpu.VMEM((4, HALO, LANES), F32), pltpu.VMEM((8, LANES), F32),
                        pltpu.VMEM((4, CHUNK, LANES), F32),
                        pltpu.VMEM((tm, gw), BF16), pltpu.VMEM((tm, gw), BF16), pltpu.VMEM((tm, gw), BF16),
                        pltpu.VMEM((tm, 2 * gw), BF16), pltpu.VMEM((tm, LANES), F32),
                        pltpu.VMEM((HALO, 2 * gw), F32)],
        compiler_params=_cparams("arbitrary"),
        name="inproj_bwd_sweep",
    )(ctx2, x2, ab, w_qk, w_rest, wg, wgt, gb, gbt, cos, sin, lgk, cw, *ffn_w)


class _Bag:
    def __init__(self, **kw):
        self.__dict__.update(kw)


def _mixer_chunk_stages(blk, r):
    rows = slice(blk * CHUNK, (blk + 1) * CHUNK)
    lane = _lane()
    sub = lax.broadcasted_iota(jnp.int32, (CHUNK, LANES), 0)
    lo = lane < HEAD_DIM
    sub_lo = sub < HEAD_DIM
    mask_ret = [(lane & 32) == 0, (lane & 32) != 0]
    mask_nat = [lo, lane >= HEAD_DIM]
    bd_ret = (sub >= HEAD_DIM) == ((lane & 32) != 0)
    bd_m = (sub >= HEAD_DIM) == (lane >= HEAD_DIM)
    le = sub <= lane
    ge = sub >= lane

    g = r.g_ref[rows, :]
    gt = r.gt_ref[:, rows]
    lf_col = _log_sigmoid(g)
    lf_row = _log_sigmoid(gt)
    le_bf = le.astype(BF16)
    ge_bf = ge.astype(BF16)
    pre_col = _cumsum_cols(ge_bf, lf_col)
    bal_f = pltpu.roll(pre_col, LANES - 8, 1)
    bal_b = pltpu.roll(pre_col[CHUNK - 1:CHUNK, :] - pre_col + lf_col, LANES - 8, 1)
    cf_all = g - bal_f
    cb_all = g - bal_b
    bf_row = _cumsum_rows(lf_row, le_bf)
    bb_row = bf_row[:, CHUNK - 1:CHUNK] - bf_row + lf_row
    qb, kb, vts, kf32, qt, vbd = [], [], [], [], [], []
    for p in range(8):
        sl = slice((p % 4) * LANES, (p % 4 + 1) * LANES)
        if p < 4:
            q2, k2, v2 = r.rq_ref[rows, sl], r.rk_ref[rows, sl], r.rv_ref[rows, sl]
            qf = q2.astype(F32)
            kf = None
        else:
            q2 = r.qk_ref[rows, sl]
            k2 = r.qk_ref[rows, 4 * LANES + (p - 4) * LANES:4 * LANES + (p - 3) * LANES]
            v2 = r.mv_ref[rows, sl]
            qf, kf = q2.astype(F32), k2.astype(F32)
        vt = v2.astype(F32).T.astype(BF16)
        qb.append(q2)
        kb.append(k2)
        vts.append(vt)
        kf32.append(kf)
        qt.append(qf.T)
        vbd.append([jnp.where(sub_lo, vt, jnp.zeros_like(vt)), jnp.where(sub_lo, jnp.zeros_like(vt), vt)])
    yield

    st2 = []
    for p in range(8):
        masks = mask_ret if p < 4 else mask_nat
        zero = jnp.zeros_like(qb[p])
        qstack = jnp.concatenate([jnp.where(masks[0], qb[p], zero), jnp.where(masks[1], qb[p], zero)], axis=0)
        st2.append(_dot_nt(kb[p], qstack))
    yield

    qn = [_dot_nt(jnp.concatenate([r.n_scr[p].astype(BF16), r.nmb_ref[blk, p]], axis=0), qb[4 + p])
          for p in range(4)]
    qn_f = [x[0:HALO] for x in qn]
    qn_b = [x[HALO:2 * HALO] for x in qn]
    cf2, cb2, bf2, bb2 = cf_all * LOG2E, cb_all * LOG2E, bf_row * LOG2E, bb_row * LOG2E
    m_f2, m_b2 = r.m_state[0:1, :] * LOG2E, r.mmb_ref[blk, 0:1, :] * LOG2E
    lhs, rhs = [], []
    for p in range(8):
        if p < 4:
            pts = [(st2[p][:, a * LANES:(a + 1) * LANES] * r.dm_scr[2 * p + a]).astype(BF16) for a in range(2)]
            x_f, x_b = r.dec_scr[0, p], r.dec_scr[1, p]
            old = [r.s_scr[p].astype(BF16), r.sretb_ref[blk, p]]
        else:
            pts, cf, cb = [], [], []
            for a in range(2):
                h = 2 * (p - 4) + a
                st = st2[p][:, a * LANES:(a + 1) * LANES]
                pf, coef_f = _mlstm_dir_weights(st, qn_f[p - 4][a:a + 1, :], cf2[:, h:h + 1],
                                                bf2[8 + h:9 + h, :], m_f2[0:1, h:h + 1], le)
                pb, coef_b = _mlstm_dir_weights(st, qn_b[p - 4][a:a + 1, :], cb2[:, 16 + h:17 + h],
                                                bb2[24 + h:25 + h, :], m_b2[0:1, 16 + h:17 + h], ge)
                pts.append((pf + pb).astype(BF16))
                cf.append(coef_f)
                cb.append(coef_b)
            x_f = jnp.where(sub_lo, cf[0], cf[1])
            x_b = jnp.where(sub_lo, cb[0], cb[1])
            old = [r.c_scr[p - 4].astype(BF16), r.cmb_ref[blk, p - 4]]
        lhs.append(jnp.concatenate(vbd[p] + old, axis=1))
        rhs.append(jnp.concatenate(pts + [(qt[p] * x_f).astype(BF16), (qt[p] * x_b).astype(BF16)], axis=0))
    yield

    ht = [_dot(lhs[p], rhs[p]) for p in range(8)]
    yield

    for p in range(8):
        sl = slice((p % 4) * LANES, (p % 4 + 1) * LANES)
        gate_ref = r.rg_ref if p < 4 else r.mo_ref
        y = _heads_out(ht[p], r.nw_ref[p])
        new = (y * gate_ref[rows, sl].astype(F32)).astype(BF16)
        r.mix_scr[rows, p * LANES:(p + 1) * LANES] = jnp.where(r.live, new, r.mix_scr[rows, p * LANES:(p + 1) * LANES])
    yield

    for p in range(4):
        cd = jnp.exp(r.lgk_ref[p:p + 1, :] * float(CHUNK))
        _ret_state_update(r.s_scr, p, kb[p], vts[p], r.dec_scr[2, p], cd, bd_ret)
    _mlstm_state_update(r.c_scr, r.n_scr, r.m_state, kf32[4:], vts[4:], cf_all, bal_f[CHUNK - 1:CHUNK, :],
                        0, lo, bd_m)
    yield


def _attn_inproj_stages(x, ab_ref, wq_ref, wk_ref, wv_ref, nw_ref, cos, sin, y_ref):
    hb = _norm_mod(x, ab_ref)
    r2 = lax.broadcasted_iota(jnp.int32, (2 * LANES, 2 * LANES), 0)
    c2 = lax.broadcasted_iota(jnp.int32, (2 * LANES, 2 * LANES), 1)
    same_head = (((r2 ^ c2) & (LANES | 32)) == 0).astype(BF16)
    acc_q = _dot(hb, wq_ref[...])
    acc_k = _dot(hb, wk_ref[...])
    y_ref[:, 10 * LANES:12 * LANES] = _dot(hb, wv_ref[...]).astype(BF16)
    yield
    for j in range(5):
        acc = acc_q[:, j * 2 * LANES:(j + 1) * 2 * LANES] if j < 4 else acc_k
        sq = acc * acc
        ms = _dot(sq.astype(BF16), same_head) * (1.0 / HEAD_DIM)
        nrm = acc * lax.rsqrt(ms + EPS)
        nw = nw_ref[0:1, :] if j < 4 else nw_ref[1:2, :]
        for v in range(2):
            ls = slice(v * LANES, (v + 1) * LANES)
            y_ref[:, (2 * j + v) * LANES:(2 * j + v + 1) * LANES] = _rope(nrm[:, ls] * nw, cos, sin).astype(BF16)
        if j in (1, 4):
            yield


def _ffn_splits(d_ff, pieces):
    blocks = d_ff // (2 * LANES)
    assert blocks * 2 * LANES == d_ff and blocks >= pieces
    cuts = [((i * blocks) // pieces) * 2 * LANES for i in range(pieces)]
    return cuts + [d_ff]


def _mixer_ffn_kernel(lg_ref, y_ref, qk_ref,
                      g_ref, gt_ref, sretb_ref, cmb_ref, nmb_ref, mmb_ref,
                      lgk_ref, lgkt_ref, nw_ref,
                      ctx_ref, x_ref, p_ref, wo_ref, wi_ref, w2_ref,
                      ab2_ref, waq_ref, wak_ref, wav_ref, nwa_ref, cos_ref, sin_ref, wi_src, w2_src,
                      o_ref, y2_ref, wi_dst, w2_dst,
                      s_scr, c_scr, n_scr, m_state, dm_scr, dec_scr, mix_scr, act_scr, x2_scr,
                      *, tiles, tpb, d_ff):
    s = pl.program_id(0)
    _wcast((wi_src, w2_src), (wi_dst, w2_dst))
    nb = tiles // tpb
    jt = _div_small(jnp.minimum(s, tiles - 1), tpb, nb)[1]
    gw = 4 * LANES
    rq_ref, rk_ref, rv_ref, rg_ref, mv_ref, mo_ref = [y_ref.at[:, j * gw:(j + 1) * gw] for j in range(6)]
    lane = _lane()
    sub = lax.broadcasted_iota(jnp.int32, (CHUNK, LANES), 0)

    @pl.when(s == 0)
    def _():
        mix_scr[...] = jnp.zeros_like(mix_scr)
        x2_scr[...] = jnp.zeros_like(x2_scr)
        le = sub <= lane
        ge = sub >= lane
        spos = sub.astype(F32)
        tpos = lane.astype(F32)
        diff = (lane - sub).astype(F32)
        for h in range(8):
            dm_scr[h] = (jnp.where(le, jnp.exp(lg_ref[h] * diff), 0.0)
                         + jnp.where(ge, jnp.exp(lg_ref[8 + h] * (-diff)), 0.0))
        for p in range(4):
            dec_scr[0, p] = jnp.exp(lgkt_ref[:, p:p + 1] * (tpos + 1.0))
            dec_scr[1, p] = jnp.exp(lgkt_ref[:, 4 + p:5 + p] * (float(CHUNK) - tpos))
            dec_scr[2, p] = jnp.exp(lgk_ref[p:p + 1, :] * (float(CHUNK) - 1.0 - spos))

    @pl.when(jt == 0)
    def _():
        s_scr[...] = jnp.zeros_like(s_scr)
        c_scr[...] = jnp.zeros_like(c_scr)
        n_scr[...] = jnp.zeros_like(n_scr)
        m_state[...] = jnp.zeros_like(m_state)

    r = _Bag(rq_ref=rq_ref, rk_ref=rk_ref, rv_ref=rv_ref, rg_ref=rg_ref, qk_ref=qk_ref, mv_ref=mv_ref, mo_ref=mo_ref,
             g_ref=g_ref, gt_ref=gt_ref, sretb_ref=sretb_ref, cmb_ref=cmb_ref, nmb_ref=nmb_ref, mmb_ref=mmb_ref,
             lgk_ref=lgk_ref, nw_ref=nw_ref, s_scr=s_scr, c_scr=c_scr, n_scr=n_scr, m_state=m_state,
             dm_scr=dm_scr, dec_scr=dec_scr, mix_scr=mix_scr, live=s < tiles)
    cuts = _ffn_splits(d_ff, 3)
    ffn_piece = lambda i: _ffn_cols(h, wi_ref, act_scr, cuts[i], cuts[i + 1], d_ff)

    x = _ctx_or_latent_rows(ctx_ref, x_ref, jnp.clip(s - 1, 0, tiles - 1), nb, tpb)
    x1, h = _ffn_in(x, mix_scr[...], p_ref, wo_ref)

    nxt = _attn_inproj_stages(x2_scr[...], ab2_ref, waq_ref, wak_ref, wav_ref, nwa_ref, cos_ref[...], sin_ref[...],
                              y2_ref)
    chunk_a, chunk_b = _mixer_chunk_stages(0, r), _mixer_chunk_stages(1, r)
    next(nxt)
    next(chunk_a), next(chunk_b)
    next(chunk_a), next(chunk_b)
    ffn_piece(0)
    next(chunk_a), next(chunk_a)
    ffn_piece(1)
    next(chunk_a), next(chunk_a)
    next(nxt)
    ffn_piece(2)
    next(chunk_b), next(chunk_b)
    next(nxt)
    x2 = x1 + p_ref[3:4, :] * _dot(act_scr[...], w2_ref[...])
    o_ref[...] = x2
    next(chunk_b), next(chunk_b)
    x2_scr[...] = x2


def _mixer_ffn(y, g, gt, states, lg_smem, lgk, lgkt, nw, xs, prm, wo, wi, w2,
               ab_next, w_next, nw_next, cos, sin, ffn_w, nb, tpb):
    gw = 4 * LANES
    tm = ROW_TILE
    cpt = tm // CHUNK
    sretb, cmb, nmb, mmb, qk_act = states
    tiles = nb * tpb
    d = xs[-1].shape[1]
    d_ff = w2.shape[0]
    n_next = sum(a.shape[1] for a in w_next)
    mix = lambda s: jnp.minimum(s, tiles - 1)
    ffn = lambda s: jnp.clip(s - 1, 0, tiles - 1)
    nxt = lambda s: jnp.maximum(s - 2, 0)
    state_spec = lambda a: pl.BlockSpec((cpt,) + a.shape[1:], lambda s: (mix(s),) + (0,) * (a.ndim - 1))
    def sel(t):
        b, j = _div_small(t, tpb, nb)
        return b * 2 + jnp.minimum(j, 1), 0, 0

    wc_in, wc_out, wc_shapes = _wcast_specs(*ffn_w, 1, tiles + 2)
    return pl.pallas_call(
        functools.partial(_mixer_ffn_kernel, tiles=tiles, tpb=tpb, d_ff=d_ff),
        grid=(tiles + 2,),
        in_specs=[_smem_spec(),
                  pl.BlockSpec((tm, 6 * gw), lambda s: (mix(s), 0)),
                  pl.BlockSpec((tm, 2 * gw), lambda s: (mix(s), 0)),
                  pl.BlockSpec((tm, LANES), lambda s: (mix(s), 0)),
                  pl.BlockSpec((32, tm), lambda s: (0, mix(s))),
                  state_spec(sretb), state_spec(cmb), state_spec(nmb), state_spec(mmb),
                  _const_spec(lgk.shape), _const_spec(lgkt.shape), _const_spec(nw.shape)]
                 + _split_row_specs(tm, d, nb, tpb, ffn)
                 + [pl.BlockSpec((None, 8, d), lambda s: sel(ffn(s))),
                    _const_spec(wo.shape), _const_spec(wi.shape), _const_spec(w2.shape),
                    pl.BlockSpec((None, 2, d), lambda s: sel(nxt(s))),
                    *[_const_spec(a.shape) for a in w_next], _const_spec(nw_next.shape),
                    pl.BlockSpec((tm, LANES), lambda s: (_div_small(nxt(s), tpb, nb)[1], 0)),
                    pl.BlockSpec((tm, LANES), lambda s: (_div_small(nxt(s), tpb, nb)[1], 0))] + wc_in,
        out_specs=[pl.BlockSpec((tm, d), lambda s: (ffn(s), 0)),
                   pl.BlockSpec((tm, n_next), lambda s: (nxt(s), 0))] + wc_out,
        out_shape=[jax.ShapeDtypeStruct((tiles * tm, d), F32),
                   jax.ShapeDtypeStruct((tiles * tm, n_next), BF16)] + wc_shapes,
        scratch_shapes=[pltpu.VMEM((4, CHUNK, LANES), F32), pltpu.VMEM((4, CHUNK, LANES), F32),
                        pltpu.VMEM((4, HALO, LANES), F32), pltpu.VMEM((8, LANES), F32),
                        pltpu.VMEM((8, CHUNK, LANES), F32), pltpu.VMEM((3, 4, CHUNK, LANES), F32),
                        pltpu.VMEM((tm, 2 * gw), BF16), pltpu.VMEM((tm, d_ff), BF16),
                        pltpu.VMEM((tm, d), F32)],
        compiler_params=_cparams("arbitrary"),
        name="ret_mlstm_mixer_ffn",
    )(lg_smem, y, qk_act, g, gt, sretb, cmb, nmb, mmb, lgk, lgkt, nw,
      *xs, prm, wo, wi, w2, ab_next, *w_next, nw_next, cos, sin, *ffn_w)


def _ffn_in(x, m, p_ref, wo_ref):
    x1 = x + p_ref[0:1, :] * _dot(m, wo_ref[...])
    ms = jnp.mean(x1 * x1, axis=-1, keepdims=True)
    return x1, ((x1 * lax.rsqrt(ms + EPS)) * p_ref[1:2, :] + p_ref[2:3, :]).astype(BF16)


def _ffn_cols(h, wi_ref, act_scr, lo, hi, d_ff):
    gate = _dot(h, wi_ref[:, lo:hi])
    up = _dot(h, wi_ref[:, d_ff + lo:d_ff + hi])
    act_scr[:, lo:hi] = (gate * jax.nn.sigmoid(gate) * up).astype(BF16)


def _window_bias():
    kk = np.arange(CHUNK)[:, None]
    t = np.arange(CHUNK)[None, :]
    tabs = []
    for has_prev, has_next in ((False, True), (True, True), (True, False)):
        prev_ok = (kk >= t) & has_prev
        next_ok = (kk <= t) & has_next
        tabs.append(np.where(np.concatenate([prev_ok, next_ok], axis=0), 0.0, NEG))
    return jnp.asarray(np.stack(tabs), F32)


def _attn_ffn_kernel(sink_ref, qkv_a_ref, qkv_b_ref, kvp_ref, kvn_ref, kvx_ref, bias_ref,
                     x_a_ref, x_b_ref, p_ref, wo_ref, wi_ref, w2_ref, o_ref, m_scr, act_scr,
                     *, steps, per_batch, d_ff):
    s = pl.program_id(0)
    kw = 2 * LANES
    k0 = 8 * LANES
    tm = qkv_a_ref.shape[0]
    nblk = 2 * tm // CHUNK

    @pl.when(s == 0)
    def _():
        m_scr[...] = jnp.zeros_like(m_scr)

    j = _div_small(jnp.minimum(s, steps - 1), per_batch, steps // per_batch)[1]
    grp = lax.broadcasted_iota(jnp.int32, (1, 4 * CHUNK), 1) // CHUNK
    lane = _lane()
    mask_q = [(lane & 32) == 0, (lane & 32) != 0]
    tile4 = lambda b: jnp.concatenate([b] * 4, axis=1)
    inner = tile4(bias_ref[1])
    biases = ([tile4(jnp.where(j == 0, bias_ref[0], bias_ref[1]))] + [inner] * (nblk - 2)
              + [tile4(jnp.where(j == per_batch - 1, bias_ref[2], bias_ref[1]))])
    cuts = _ffn_splits(d_ff, nblk - 1)

    def ffn_up(i):
        _ffn_cols(h, wi_ref, act_scr, cuts[i], cuts[i + 1], d_ff)

    def q_rows(blk):
        ref = qkv_a_ref if blk * CHUNK < tm else qkv_b_ref
        r0 = (blk * CHUNK) % tm
        return ref, slice(r0, r0 + CHUNK)

    def scores(blk, kv):
        kvp, a = kv // 2, kv % 2
        ref, rows = q_rows(blk)
        qs = jnp.concatenate(
            [jnp.where(mask_q[a], ref[rows, (kvp * 4 + g) * LANES:(kvp * 4 + g + 1) * LANES],
                       jnp.zeros((CHUNK, LANES), BF16)) for g in range(4)], axis=0)
        return _dot_nt(kcats[kvp][blk], qs)

    def softmax_pv(blk, kv, st):
        bias = biases[blk]
        st = jnp.concatenate([st[0:CHUNK] + bias[0:CHUNK], st[CHUNK:2 * CHUNK],
                              st[2 * CHUNK:3 * CHUNK] + bias[CHUNK:2 * CHUNK], st[3 * CHUNK:]], axis=0)
        snk = jnp.where(grp == 0, sink_ref[kv * 4],
                        jnp.where(grp == 1, sink_ref[kv * 4 + 1],
                                  jnp.where(grp == 2, sink_ref[kv * 4 + 2], sink_ref[kv * 4 + 3])))
        m = jnp.maximum(jnp.max(st, axis=0, keepdims=True), snk)
        e = jnp.exp2(st - m)
        denom = jnp.exp2(snk - m) + jnp.sum(e, axis=0, keepdims=True)
        a = kv % 2
        return _dot(vts[kv // 2][blk][a * HEAD_DIM:(a + 1) * HEAD_DIM, :], e.astype(BF16)) * (1.0 / denom)

    def hand_over(blk, outs):
        for kvp in range(2):
            full = jnp.concatenate(outs[2 * kvp:2 * kvp + 2], axis=0)
            for g in range(4):
                m_scr[blk * CHUNK:(blk + 1) * CHUNK, (kvp * 4 + g) * LANES:(kvp * 4 + g + 1) * LANES] = (
                    full[:, g * CHUNK:(g + 1) * CHUNK].T.astype(BF16))

    x1, h = _ffn_in(jnp.concatenate([x_a_ref[...], x_b_ref[...]], axis=0), m_scr[...], p_ref, wo_ref)

    kcats, vts = [], []
    for kvp in range(2):
        ks = slice(k0 + kvp * LANES, k0 + (kvp + 1) * LANES)
        vs = slice(k0 + kw + kvp * LANES, k0 + kw + (kvp + 1) * LANES)
        es, ev = slice(kvp * LANES, (kvp + 1) * LANES), slice(kw + kvp * LANES, kw + (kvp + 1) * LANES)
        tile_rows = [(ref, slice(c * CHUNK, (c + 1) * CHUNK)) for ref in (qkv_a_ref, qkv_b_ref)
                     for c in range(tm // CHUNK)]
        k_chunks = [kvp_ref[:, es]] + [ref[rows, ks] for ref, rows in tile_rows] + [kvn_ref[:, es]]
        v_chunks = ([kvp_ref[:, ev]] + [ref[rows, vs] for ref, rows in tile_rows] + [kvn_ref[:, ev]]
                    + [kvx_ref[c * CHUNK:(c + 1) * CHUNK, ev] for c in range(kvx_ref.shape[0] // CHUNK)])
        v_t = [v.astype(F32).T.astype(BF16) for v in v_chunks]
        kcats.append([jnp.concatenate(k_chunks[b:b + 3] + [kvx_ref[:, es]], axis=0) for b in range(nblk)])
        vts.append([jnp.concatenate(v_t[b:b + 3] + v_t[nblk + 2:], axis=1) for b in range(nblk)])

    for blk in range(nblk):
        sts = [scores(blk, kv) for kv in range(4)]
        if blk < nblk - 1:
            ffn_up(blk)
        else:
            o_ref[...] = x1 + p_ref[3:4, :] * _dot(act_scr[...], w2_ref[...])
        hand_over(blk, [softmax_pv(blk, kv, sts[kv]) for kv in range(4)])


def _attn_ffn(y, sink, xc, prm, wo, wi, w2, nb, nc, ctx_chunks):
    tm = ROW_TILE
    cpt = tm // CHUNK
    tpb = nc // cpt
    lat_tiles = (nc - ctx_chunks) // cpt
    ctx_tiles = ctx_chunks // cpt
    assert ctx_tiles == 1 and lat_tiles % 2 == 0 and lat_tiles >= 4
    per_batch = lat_tiles // 2
    steps = nb * per_batch
    d = xc.shape[1]
    d_ff = w2.shape[0]
    bias = _window_bias()
    n_qkv = y.shape[1]
    kv_w = n_qkv - d
    assert d % kv_w == 0
    kv_col = d // kv_w
    att = lambda s: jnp.minimum(s, steps - 1)
    ffn = lambda s: jnp.maximum(s - 1, 0)
    split = lambda t: _div_small(t, per_batch, nb)
    row_tile = lambda t, u: split(t)[0] * tpb + ctx_tiles + 2 * split(t)[1] + u
    chunk0 = lambda t: split(t)[0] * nc + ctx_chunks
    prev_c = lambda s: chunk0(att(s)) + jnp.maximum(split(att(s))[1] * 2 * cpt - 1, 0)
    next_c = lambda s: chunk0(att(s)) + jnp.minimum(split(att(s))[1] * 2 * cpt + 2 * cpt, lat_tiles * cpt - 1)
    edge_spec = lambda f: pl.BlockSpec((CHUNK, kv_w), lambda s: (f(s), kv_col))
    return pl.pallas_call(
        functools.partial(_attn_ffn_kernel, steps=steps, per_batch=per_batch, d_ff=d_ff),
        grid=(steps + 1,),
        in_specs=[_smem_spec(),
                  pl.BlockSpec((tm, n_qkv), lambda s: (row_tile(att(s), 0), 0)),
                  pl.BlockSpec((tm, n_qkv), lambda s: (row_tile(att(s), 1), 0)),
                  edge_spec(prev_c), edge_spec(next_c),
                  pl.BlockSpec((tm, kv_w), lambda s: (split(att(s))[0] * tpb, kv_col)),
                  _const_spec(bias.shape),
                  pl.BlockSpec((tm, d), lambda s: (row_tile(ffn(s), 0), 0)),
                  pl.BlockSpec((tm, d), lambda s: (row_tile(ffn(s), 1), 0)),
                  pl.BlockSpec((None, 8, d), lambda s: (split(ffn(s))[0] * 2 + 1, 0, 0)),
                  _const_spec(wo.shape), _const_spec(wi.shape), _const_spec(w2.shape)],
        out_specs=pl.BlockSpec((2 * tm, d), lambda s: (ffn(s), 0)),
        out_shape=jax.ShapeDtypeStruct((steps * 2 * tm, d), F32),
        scratch_shapes=[pltpu.VMEM((2 * tm, d), BF16), pltpu.VMEM((2 * tm, d_ff), BF16)],
        compiler_params=_cparams("arbitrary"),
        name="window_gqa_ffn",
    )(sink, y, y, y, y, y, bias, xc, xc, prm, wo, wi, w2)


def _pair_cols(w):
    rows, cols = w.shape
    return w.reshape(rows, cols // LANES, 2, 2, 32).transpose(0, 1, 3, 2, 4).reshape(rows, cols)


def _attn_q_cols(w):
    rows = w.shape[0]
    g_per = w.shape[1] // (H_KV * HEAD_DIM)
    return (w.reshape(rows, H_KV // 2, 2, g_per, 2, 32).transpose(0, 1, 3, 4, 2, 5)
            .reshape(rows, w.shape[1]))


def _attn_o_rows(w):
    cols = w.shape[1]
    g_per = w.shape[0] // (H_KV * HEAD_DIM)
    return (w.reshape(H_KV // 2, 2, g_per, HEAD_DIM, cols).transpose(0, 2, 1, 3, 4)
            .reshape(w.shape[0], cols))


def _rope_tables(seq, ctx_len):
    rows = seq // GRID_W
    row = np.repeat(np.arange(rows, dtype=np.float32), GRID_W)
    col = np.tile(np.arange(GRID_W, dtype=np.float32), rows)
    n = HEAD_DIM // 4
    inv = (np.float32(ROPE_BASE) ** (-np.arange(n, dtype=np.float32) / np.float32(n))).astype(np.float32)
    ang = np.concatenate([row[:, None] * inv, col[:, None] * inv], axis=-1).astype(np.float32)
    cos, sin = np.cos(ang), np.sin(ang)
    cos_t = np.concatenate([np.ones((ctx_len, LANES), np.float32), np.tile(cos, (1, 4))], axis=0)
    sin_t = np.concatenate([np.zeros((ctx_len, LANES), np.float32),
                            np.concatenate([-sin, -sin, sin, sin], axis=-1)], axis=0)
    return jnp.asarray(cos_t, F32), jnp.asarray(sin_t, F32)


def _mod_tables(mod, nb, norm_w):
    d = norm_w.shape[-1]
    lat = mod[:nb].reshape(nb, 6, d)
    ctx = jnp.broadcast_to(mod[nb].reshape(1, 6, d), (nb, 6, d))
    both = jnp.stack([ctx, lat], axis=1).reshape(nb * 2, 6, d)
    sh1, sc1, g1, sh2, sc2, g2 = [both[:, k] for k in range(6)]
    ab1 = jnp.stack([norm_w[0] * (1.0 + sc1), sh1], axis=1)
    zeros = jnp.zeros_like(g1)
    prm = jnp.stack([g1, norm_w[1] * (1.0 + sc2), sh2, g2, zeros, zeros, zeros, zeros], axis=1)
    return ab1, prm


def kernel(x, c, ctx, c_ctx, ada_w, ada_b, norm_w, ffn_w_in, ffn_w_out, ab_w_in, ab_w_out,
           ret_log_gamma, ret_norm_w, mlstm_conv_w, mlstm_conv_b, mlstm_gate_b, mlstm_norm_w,
           attn_w_in, attn_w_out, attn_q_norm_w, attn_k_norm_w, attn_sink):
    nb, seq, d = x.shape
    ctx_len = ctx.shape[1]
    depth = ada_w.shape[0]
    assert ctx_len == ROW_TILE and seq % ROW_TILE == 0 and d == 8 * LANES and nb < 8
    t_all = ctx_len + seq
    nc = t_all // CHUNK
    ctx_chunks = ctx_len // CHUNK
    tpb = t_all // ROW_TILE
    dr = d // 2

    rows = jnp.zeros((8, d), F32).at[:nb].set(c).at[nb].set(c_ctx)
    mod_all = _modulation(rows, ada_w, ada_b)
    cos_t, sin_t = _rope_tables(seq, ctx_len)
    ffn_w = (ffn_w_in, ffn_w_out)
    xs = (ctx.reshape(nb * ctx_len, d), x.reshape(nb * seq, d))

    assert depth == 2 and ab_w_in.shape[0] == 1 and attn_w_in.shape[0] == 1
    ab_0, prm_0 = _mod_tables(mod_all[0], nb, norm_w[0])
    ab_1, prm_1 = _mod_tables(mod_all[1], nb, norm_w[1])

    w = ab_w_in[0]
    w_qk = _pair_cols(w[:, :2 * dr]).astype(BF16)
    w_rest = w[:, 2 * dr:8 * dr].astype(BF16)
    wg = jnp.zeros((d, LANES), F32).at[:, :32].set(w[:, 8 * dr:]).astype(BF16)
    wgt = w[:, 8 * dr:].T.astype(BF16)
    gb = jnp.zeros((1, LANES), F32).at[0, :32].set(mlstm_gate_b[0].reshape(-1))
    gbt = mlstm_gate_b[0].reshape(32, 1)
    lg = ret_log_gamma[0].astype(F32)
    lgk = jnp.tile(jnp.repeat(lg.reshape(2, 4, 2), 32, axis=-1), (1, 1, 2)).reshape(8, LANES)
    cw = jnp.concatenate([mlstm_conv_w[0], mlstm_conv_b[0][None], jnp.zeros((4, 2 * dr), F32)], axis=0)
    nw = jnp.broadcast_to(jnp.concatenate([ret_norm_w[0], mlstm_norm_w[0]]).reshape(8, LANES, 1),
                          (8, LANES, LANES))
    wo_0 = ab_w_out[0].astype(BF16)

    w = attn_w_in[0]
    w_attn = (_attn_q_cols(w[:, :d]).astype(BF16), _pair_cols(w[:, d:d + 2 * LANES]).astype(BF16),
              w[:, d + 2 * LANES:].astype(BF16))
    lane_w = lambda v: jnp.concatenate([v[:32], v[:32], v[32:], v[32:]])
    nwq = jnp.stack([lane_w(attn_q_norm_w[0]) * (HEAD_DIM ** -0.5 * LOG2E), lane_w(attn_k_norm_w[0])]
                    + [jnp.zeros((LANES,), F32)] * 6)
    wo_1 = _attn_o_rows(attn_w_out[0]).astype(BF16)

    y, g, gt, *states, wi_0, w2_0 = _inproj_sweep(*xs, ab_0, w_qk, w_rest, wg, wgt, gb, gbt, cos_t, sin_t, lgk, cw,
                                                  ffn_w, nb, tpb)
    x_mid, y_attn, wi_1, w2_1 = _mixer_ffn(y, g, gt, states, lg.reshape(-1), lgk, lgk.T, nw, xs, prm_0, wo_0,
                                           wi_0, w2_0, ab_1, w_attn, nwq, cos_t, sin_t, ffn_w, nb, tpb)
    out = _attn_ffn(y_attn, attn_sink[0].astype(F32) * LOG2E, x_mid, prm_1, wo_1, wi_1, w2_1, nb, nc, ctx_chunks)
    return out.reshape(nb, seq, d)
```

```python
import functools

import numpy as np
import jax
import jax.numpy as jnp
from jax import lax
from jax.experimental import pallas as pl
from jax.experimental.pallas import tpu as pltpu

F32 = jnp.float32
BF16 = jnp.bfloat16

HEAD_DIM = 64
CHUNK = 128
GRID_W = 64
ROPE_BASE = 10000.0
EPS = 1e-6
H_KV = 4
LANES = 128
ROW_TILE = 256
HALO = 16
LOG2E = 1.4426950408889634
NEG = -1e30
VMEM_LIMIT = 56 * 1024 * 1024


def _cparams(*sem):
    return pltpu.CompilerParams(dimension_semantics=sem, vmem_limit_bytes=VMEM_LIMIT)


def _const_spec(shape):
    nd = len(shape)
    return pl.BlockSpec(shape, lambda *_: (0,) * nd, pipeline_mode=pl.Buffered(1))


def _smem_spec():
    return pl.BlockSpec(memory_space=pltpu.SMEM)


def _wcast_specs(w_in_all, w_out_all, layer, steps):
    specs_in, specs_out, shapes = [], [], []
    for w in (w_in_all, w_out_all):
        total, cols = w.shape[1:]
        rows = next(r for r in range(HALO, total + 1, HALO) if total % r == 0 and total // r <= steps)
        n_blocks = total // rows
        specs_in.append(pl.BlockSpec((None, rows, cols),
                                     lambda i, n=n_blocks: (layer, jnp.minimum(i, n - 1), 0)))
        specs_out.append(pl.BlockSpec((rows, cols), lambda i, n=n_blocks: (jnp.minimum(i, n - 1), 0)))
        shapes.append(jax.ShapeDtypeStruct(w.shape[1:], BF16))
    return specs_in, specs_out, shapes


def _wcast(src_refs, dst_refs):
    for src, dst in zip(src_refs, dst_refs):
        dst[...] = src[...].astype(BF16)


def _lane(shape=(CHUNK, LANES)):
    return lax.broadcasted_iota(jnp.int32, shape, len(shape) - 1)


def _dot(a, b):
    return jnp.dot(a, b, preferred_element_type=F32)


def _dot_nt(a, b):
    return lax.dot_general(a, b, (((1,), (1,)), ((), ())), preferred_element_type=F32)


def _div_small(t, m, n):
    q = 0
    for b in range(1, n):
        q = q + jnp.where(t >= b * m, 1, 0)
    return q, t - q * m


def _split3(x):
    hi = x.astype(BF16)
    r = x - hi.astype(F32)
    mid = r.astype(BF16)
    lo = (r - mid.astype(F32)).astype(BF16)
    return hi, mid, lo


def _log_sigmoid(x):
    return jnp.minimum(x, 0.0) - jnp.log1p(jnp.exp(-jnp.abs(x)))


def _rope(x, cos, sin_signed):
    return x * cos + pltpu.roll(x, LANES // 2, 1) * sin_signed


def _mod_kernel(rows_ref, w_ref, b_ref, o_ref):
    a = rows_ref[...]
    a = a * jax.nn.sigmoid(a)
    a_hi = a.astype(BF16)
    a_lo = (a - a_hi.astype(F32)).astype(BF16)
    w = w_ref[...]
    w_hi = w.astype(BF16)
    w_lo = (w - w_hi.astype(F32)).astype(BF16)
    o_ref[...] = _dot(a_hi, w_hi) + _dot(a_hi, w_lo) + _dot(a_lo, w_hi) + b_ref[...]


def _modulation(rows, ada_w, ada_b):
    depth, d, n = ada_w.shape
    tn = n // 4
    return pl.pallas_call(
        _mod_kernel,
        grid=(depth, n // tn),
        in_specs=[pl.BlockSpec((8, d), lambda l, j: (0, 0)),
                  pl.BlockSpec((None, d, tn), lambda l, j: (l, 0, j)),
                  pl.BlockSpec((None, 1, tn), lambda l, j: (l, 0, j))],
        out_specs=pl.BlockSpec((None, 8, tn), lambda l, j: (l, 0, j)),
        out_shape=jax.ShapeDtypeStruct((depth, 8, n), F32),
        compiler_params=_cparams("arbitrary", "arbitrary"),
        name="adaln_modulation",
    )(rows, ada_w, ada_b.reshape(depth, 1, n))


def _norm_mod(x, ab_ref):
    ms = jnp.mean(x * x, axis=-1, keepdims=True)
    h = (x * lax.rsqrt(ms + EPS)) * ab_ref[0:1, :] + ab_ref[1:2, :]
    return h.astype(BF16)


def _ctx_or_latent_rows(ctx_ref, x_ref, tile, nb, tiles_per_batch):
    return jnp.where(_div_small(tile, tiles_per_batch, nb)[1] == 0, ctx_ref[...], x_ref[...])


def _split_row_specs(tm, d, nb, tpb, tile_of):
    lat = tpb - 1

    def latent_row(i):
        b, j = _div_small(tile_of(i), tpb, nb)
        return b * lat + jnp.maximum(j - 1, 0), 0

    return [pl.BlockSpec((tm, d), lambda i: (_div_small(tile_of(i), tpb, nb)[0], 0)),
            pl.BlockSpec((tm, d), latent_row)]


def _cumsum_cols(tri_bf, lf):
    hi, mid, lo = _split3(lf)
    return _dot(tri_bf, hi) + _dot(tri_bf, mid) + _dot(tri_bf, lo)


def _cumsum_rows(lf, tri_bf):
    hi, mid, lo = _split3(lf)
    return _dot(hi, tri_bf) + _dot(mid, tri_bf) + _dot(lo, tri_bf)


def _ret_state_update(s_ref, p, k2, vt, kdec, cd_lanes, bd):
    kf = (k2.astype(F32) * kdec).astype(BF16)
    s_ref[p] = s_ref[p] * cd_lanes + jnp.where(bd, _dot(vt, kf), 0.0)


def _mlstm_state_update(c_ref, n_ref, m_ref, k_pairs, vt_pairs, c_all, bend, col0, lo, bd):
    cmax = jnp.max(c_all, axis=0, keepdims=True)
    w_all = jnp.exp(c_all - cmax)
    m_old = m_ref[0:1, :]
    mrel = jnp.maximum(m_old, cmax)
    a_row = jnp.exp(m_old - mrel)
    bb_row = jnp.exp(cmax - mrel)
    m_ref[0:1, :] = bend + mrel
    lo_row = lo[0:1, :]
    for p in range(4):
        h0 = col0 + 2 * p
        kw = k_pairs[p] * jnp.where(lo, w_all[:, h0:h0 + 1], w_all[:, h0 + 1:h0 + 2])
        kvt = _dot(vt_pairs[p], kw.astype(BF16))
        nloc = jnp.sum(kw, axis=0, keepdims=True)
        a_l = jnp.where(lo_row, a_row[:, h0:h0 + 1], a_row[:, h0 + 1:h0 + 2])
        bb_l = jnp.where(lo_row, bb_row[:, h0:h0 + 1], bb_row[:, h0 + 1:h0 + 2])
        c_ref[p] = c_ref[p] * a_l + jnp.where(bd, kvt, 0.0) * bb_l
        n_new = (n_ref[p, 0:1, :] + n_ref[p, 1:2, :]) * a_l + nloc * bb_l
        n_ref[p, 0:1, :] = jnp.where(lo_row, n_new, 0.0)
        n_ref[p, 1:2, :] = jnp.where(lo_row, 0.0, n_new)


def _mlstm_dir_weights(st, qn_row, c_col, bt_row, m_prev, tri):
    dl = jnp.where(tri, c_col + bt_row, NEG)
    mx = jnp.max(dl, axis=0, keepdims=True)
    al = bt_row + m_prev
    m_t = jnp.maximum(al, mx)
    w = jnp.exp2(dl - m_t)
    a_t = jnp.exp2(al - m_t)
    sw = st * w
    den = jnp.sum(sw, axis=0, keepdims=True) + a_t * qn_row
    r = 1.0 / jnp.maximum(jnp.abs(den), jnp.exp2(-m_t))
    return sw * r, a_t * r


def _heads_out(ht, nw_tab):
    rows = []
    for a in range(2):
        ha = ht[a * HEAD_DIM:(a + 1) * HEAD_DIM, :]
        ms = jnp.mean(ha * ha, axis=0, keepdims=True)
        rows.append(ha * lax.rsqrt(ms + EPS))
    return (jnp.concatenate(rows, axis=0) * nw_tab).T


def _inproj_sweep_kernel(ctx_ref, x_ref, ab_ref, w_ref, wgate_ref, gb_ref, gbt_ref, cos_ref, sin_ref,
                         lgk_ref, cw_ref, wi_src, w2_src,
                         y_ref, g_ref, gt_ref, sret_ref, cm_ref, nm_ref, mm_ref, qk_ref, wi_dst, w2_dst,
                         s_scr, c_scr, n_scr, m_scr, kdec_scr,
                         p_rk, p_rv, p_mv, p_mqk, p_g, next_row_scr, wqk_scr, wg_scr, wgt_scr, *, tiles, tpb):
    i = pl.program_id(0)
    _wcast((wi_src, w2_src), (wi_dst, w2_dst))

    def order(t):
        v = _div_small(t, tpb, tiles // tpb)[1]
        return jnp.where(v == 0, 0, tpb - v)

    jt_a = order(jnp.maximum(i - 1, 0))
    lane = _lane()
    sub = lax.broadcasted_iota(jnp.int32, (CHUNK, LANES), 0)
    lo = lane < HEAD_DIM
    lo_row = lo[0:1, :]
    bd_ret = (sub >= HEAD_DIM) == ((lane & 32) != 0)
    bd_m = (sub >= HEAD_DIM) == (lane >= HEAD_DIM)
    gw = 4 * LANES

    @pl.when(i == 0)
    def _():
        pos = sub.astype(F32)
        for p in range(4):
            kdec_scr[p] = jnp.exp(lgk_ref[4 + p:5 + p, :] * pos)
        for ref in (p_rk, p_rv, p_mv, p_mqk, p_g, next_row_scr):
            ref[...] = jnp.zeros_like(ref)
        for grp in range(2 * gw // LANES):
            cols = slice(grp * LANES, (grp + 1) * LANES)
            wg_f = w_ref[:, cols].astype(F32)
            quarter = _lane((wg_f.shape[0], LANES)) // 32
            wg_f = jnp.where(quarter == 1, pltpu.roll(wg_f, LANES - 32, 1),
                             jnp.where(quarter == 2, pltpu.roll(wg_f, 32, 1), wg_f))
            wqk_scr[:, cols] = wg_f.astype(BF16)
        gate_w = jnp.where(_lane((wgate_ref.shape[0], LANES)) < 32, wgate_ref[...].astype(F32), 0.0)
        wg_scr[...] = gate_w.astype(BF16)
        wgt_scr[...] = gate_w.T[0:32, :].astype(BF16)

    @pl.when(jt_a == 0)
    def _():
        s_scr[...] = jnp.zeros_like(s_scr)
        c_scr[...] = jnp.zeros_like(c_scr)
        n_scr[...] = jnp.zeros_like(n_scr)
        m_scr[...] = jnp.zeros_like(m_scr)

    tile_i = jnp.minimum(i, tiles - 1)
    jt_i = order(tile_i)
    hb = _norm_mod(jnp.where(jt_i == 0, ctx_ref[...], x_ref[...]), ab_ref)
    cos, sin = cos_ref[...], sin_ref[...]

    def project(j):
        w_cols = (wqk_scr if j < 2 else w_ref)[:, j * gw:(j + 1) * gw]
        acc = _dot(hb, w_cols)
        if j in (0, 1):
            if j == 0:
                acc = acc * (HEAD_DIM ** -0.5)
            acc = jnp.concatenate([_rope(acc[:, p * LANES:(p + 1) * LANES], cos, sin) for p in range(4)], axis=1)
        elif j == 3:
            acc = acc * jax.nn.sigmoid(acc)
        elif j == 7:
            acc = jax.nn.sigmoid(acc)
        return acc.astype(BF16)

    raw_q, raw_k = project(4), project(5)

    prev_on = jnp.where(jt_a <= 1, 0.0, 1.0).astype(F32)
    next_on = jnp.where((jt_a == 0) | (jt_a == tpb - 1), 0.0, 1.0).astype(F32)
    cur = p_mqk[...].astype(F32)
    n = cur.shape[0]
    row = lax.broadcasted_iota(jnp.int32, cur.shape, 0)
    prev_row = jnp.concatenate([raw_q[n - HALO:, :], raw_k[n - HALO:, :]], axis=1)[HALO - 1:HALO, :].astype(F32)
    xm = jnp.where(row == 0, prev_row * prev_on, pltpu.roll(cur, 1, 0))
    xp = jnp.where(row == n - 1, next_row_scr[0:1, :] * next_on, pltpu.roll(cur, n - 1, 0))
    conv = cw_ref[3:4, :] + cw_ref[0:1, :] * xm + cw_ref[1:2, :] * cur + cw_ref[2:3, :] * xp
    qk = conv * jax.nn.sigmoid(conv)
    qk_ref[:, 0:gw] = (qk[:, 0:gw] * (HEAD_DIM ** -0.5)).astype(BF16)
    qk_ref[:, gw:] = qk[:, gw:].astype(BF16)
    le_bf = (sub <= lane).astype(BF16)
    chunks = (1, 0)
    pre = {}
    for blk in chunks:
        rows = slice(blk * CHUNK, (blk + 1) * CHUNK)
        g = p_g[rows, :]
        bal = pltpu.roll(_cumsum_cols(le_bf, _log_sigmoid(g)), LANES - 8, 1)
        c_all = g - bal
        cmax = jnp.max(c_all, axis=0, keepdims=True)
        w_all = jnp.exp(c_all - cmax)
        vts, ks, nlocs = [], [], []
        for p in range(8):
            sl = slice((p % 4) * LANES, (p % 4 + 1) * LANES)
            if p < 4:
                v2 = p_rv[rows, sl]
                ks.append((p_rk[rows, sl].astype(F32) * kdec_scr[p]).astype(BF16))
            else:
                v2 = p_mv[rows, sl]
                h0 = 16 + 2 * (p - 4)
                kw = qk[rows, gw + (p - 4) * LANES:gw + (p - 3) * LANES] * jnp.where(
                    lo, w_all[:, h0:h0 + 1], w_all[:, h0 + 1:h0 + 2])
                ks.append(kw.astype(BF16))
                nlocs.append(jnp.sum(kw, axis=0, keepdims=True))
            vts.append(v2.astype(F32).T.astype(BF16))
        pre[blk] = (vts, ks, nlocs, cmax, bal[0:1, :])

    cur_rk = project(1)
    y_ref[:, 1 * gw:2 * gw] = cur_rk
    y_ref[:, 0:gw] = project(0)
    kvs = {blk: [_dot(pre[blk][0][p], pre[blk][1][p]) for p in range(8)] for blk in chunks}
    cur_rv = project(2)
    y_ref[:, 2 * gw:3 * gw] = cur_rv
    y_ref[:, 3 * gw:4 * gw] = project(3)

    for blk in chunks:
        _, _, nlocs, cmax, bend = pre[blk]
        sret_ref[blk] = s_scr[...].astype(BF16)
        cm_ref[blk] = c_scr[...].astype(BF16)
        nm_ref[blk] = n_scr[...].astype(BF16)
        mm_ref[blk] = m_scr[...]
        m_old = m_scr[0:1, :]
        mrel = jnp.maximum(m_old, cmax)
        a_row = jnp.exp(m_old - mrel)
        bb_row = jnp.exp(cmax - mrel)
        m_scr[0:1, :] = bend + mrel
        for p in range(4):
            cd = jnp.exp(lgk_ref[4 + p:5 + p, :] * float(CHUNK))
            s_scr[p] = s_scr[p] * cd + jnp.where(bd_ret, kvs[blk][p], 0.0)
            h0 = 16 + 2 * p
            a_l = jnp.where(lo_row, a_row[:, h0:h0 + 1], a_row[:, h0 + 1:h0 + 2])
            bb_l = jnp.where(lo_row, bb_row[:, h0:h0 + 1], bb_row[:, h0 + 1:h0 + 2])
            c_scr[p] = c_scr[p] * a_l + jnp.where(bd_m, kvs[blk][4 + p], 0.0) * bb_l
            n_new = (n_scr[p, 0:1, :] + n_scr[p, 1:2, :]) * a_l + nlocs[p] * bb_l
            n_scr[p, 0:1, :] = jnp.where(lo_row, n_new, 0.0)
            n_scr[p, 1:2, :] = jnp.where(lo_row, 0.0, n_new)

    cur_mv = project(6)
    y_ref[:, 4 * gw:5 * gw] = cur_mv
    y_ref[:, 5 * gw:6 * gw] = project(7)
    gates = _dot(hb, wg_scr[...]) + gb_ref[...]
    g_ref[...] = gates
    gt_ref[...] = _dot_nt(wgt_scr[...], hb) + gbt_ref[...]
    next_row_scr[...] = p_mqk[0:HALO, :].astype(F32)
    p_mqk[:, 0:gw] = raw_q
    p_mqk[:, gw:] = raw_k
    p_rk[...] = cur_rk
    p_rv[...] = cur_rv
    p_mv[...] = cur_mv
    p_g[...] = gates


def _inproj_sweep(ctx2, x2, ab, w_in, gb, gbt, cos, sin, lgk, cw, ffn_w, nb, tpb):
    d = x2.shape[1]
    tm = ROW_TILE
    cpt = tm // CHUNK
    gw = 4 * LANES
    tiles = nb * tpb
    r = tiles * tm
    lat = tpb - 1
    def visit(t):
        b, v = _div_small(t, tpb, nb)
        return b, jnp.where(v == 0, 0, tpb - v)

    cur = lambda i: visit(jnp.minimum(i, tiles - 1))
    flat = lambda bj: bj[0] * tpb + bj[1]
    tile_i = lambda i: flat(cur(i))
    tile_a = lambda i: flat(visit(jnp.maximum(i - 1, 0)))
    in_batch = lambda i: cur(i)[1]
    sel = lambda i: (cur(i)[0] * 2 + jnp.minimum(cur(i)[1], 1), 0, 0)
    state = lambda *dims: pl.BlockSpec((cpt,) + dims, lambda i: (tile_a(i),) + (0,) * len(dims))
    nchunks = tiles * cpt
    wc_in, wc_out, wc_shapes = _wcast_specs(*ffn_w, 0, tiles + 1)
    return pl.pallas_call(
        functools.partial(_inproj_sweep_kernel, tiles=tiles, tpb=tpb),
        grid=(tiles + 1,),
        in_specs=[pl.BlockSpec((tm, d), lambda i: (cur(i)[0], 0)),
                  pl.BlockSpec((tm, d), lambda i: (cur(i)[0] * lat + jnp.maximum(cur(i)[1] - 1, 0), 0)),
                  pl.BlockSpec((None, 2, d), sel),
                  pl.BlockSpec((d, 8 * gw), lambda i: (0, 0), pipeline_mode=pl.Buffered(1)),
                  pl.BlockSpec((d, LANES), lambda i: (0, 8 * gw // LANES), pipeline_mode=pl.Buffered(1)),
                  _const_spec(gb.shape), _const_spec(gbt.shape),
                  pl.BlockSpec((tm, LANES), lambda i: (in_batch(i), 0)),
                  pl.BlockSpec((tm, LANES), lambda i: (in_batch(i), 0)),
                  _const_spec(lgk.shape), _const_spec(cw.shape)] + wc_in,
        out_specs=[pl.BlockSpec((tm, 6 * gw), lambda i: (tile_i(i), 0)),
                   pl.BlockSpec((tm, LANES), lambda i: (tile_i(i), 0)),
                   pl.BlockSpec((32, tm), lambda i: (0, tile_i(i))),
                   state(4, CHUNK, LANES), state(4, CHUNK, LANES), state(4, HALO, LANES), state(8, LANES),
                   pl.BlockSpec((tm, 2 * gw), lambda i: (tile_a(i), 0))] + wc_out,
        out_shape=[jax.ShapeDtypeStruct((r, 6 * gw), BF16),
                   jax.ShapeDtypeStruct((r, LANES), F32),
                   jax.ShapeDtypeStruct((32, r), F32),
                   jax.ShapeDtypeStruct((nchunks, 4, CHUNK, LANES), BF16),
                   jax.ShapeDtypeStruct((nchunks, 4, CHUNK, LANES), BF16),
                   jax.ShapeDtypeStruct((nchunks, 4, HALO, LANES), BF16),
                   jax.ShapeDtypeStruct((nchunks, 8, LANES), F32),
                   jax.ShapeDtypeStruct((r, 2 * gw), BF16)] + wc_shapes,
        scratch_shapes=[pltpu.VMEM((4, CHUNK, LANES), F32), pltpu.VMEM((4, CHUNK, LANES), F32),
                        pltpu.VMEM((4, HALO, LANES), F32), pltpu.VMEM((8, LANES), F32),
                        pltpu.VMEM((4, CHUNK, LANES), F32),
                        pltpu.VMEM((tm, gw), BF16), pltpu.VMEM((tm, gw), BF16), pltpu.VMEM((tm, gw), BF16),
                        pltpu.VMEM((tm, 2 * gw), BF16), pltpu.VMEM((tm, LANES), F32),
                        pltpu.VMEM((HALO, 2 * gw), F32),
                        pltpu.VMEM((d, 2 * gw), BF16), pltpu.VMEM((d, LANES), BF16), pltpu.VMEM((32, d), BF16)],
        compiler_params=_cparams("arbitrary"),
        name="inproj_bwd_sweep",
    )(ctx2, x2, ab, w_in, w_in, gb, gbt, cos, sin, lgk, cw, *ffn_w)


class _Bag:
    def __init__(self, **kw):
        self.__dict__.update(kw)


def _mixer_chunk_stages(blk, r):
    rows = slice(blk * CHUNK, (blk + 1) * CHUNK)
    lane = _lane()
    sub = lax.broadcasted_iota(jnp.int32, (CHUNK, LANES), 0)
    lo = lane < HEAD_DIM
    sub_lo = sub < HEAD_DIM
    mask_ret = [(lane & 32) == 0, (lane & 32) != 0]
    mask_nat = [lo, lane >= HEAD_DIM]
    bd_ret = (sub >= HEAD_DIM) == ((lane & 32) != 0)
    bd_m = (sub >= HEAD_DIM) == (lane >= HEAD_DIM)
    le = sub <= lane
    ge = sub >= lane

    g = r.g_ref[rows, :]
    gt = r.gt_ref[:, rows]
    lf_col = _log_sigmoid(g)
    lf_row = _log_sigmoid(gt)
    le_bf = le.astype(BF16)
    ge_bf = ge.astype(BF16)
    pre_col = _cumsum_cols(ge_bf, lf_col)
    bal_f = pltpu.roll(pre_col, LANES - 8, 1)
    bal_b = pltpu.roll(pre_col[CHUNK - 1:CHUNK, :] - pre_col + lf_col, LANES - 8, 1)
    cf_all = g - bal_f
    cb_all = g - bal_b
    bf_row = _cumsum_rows(lf_row, le_bf)
    bb_row = bf_row[:, CHUNK - 1:CHUNK] - bf_row + lf_row
    qb, kb, vts, kf32, qt, vbd = [], [], [], [], [], []
    for p in range(8):
        sl = slice((p % 4) * LANES, (p % 4 + 1) * LANES)
        if p < 4:
            q2, k2, v2 = r.rq_ref[rows, sl], r.rk_ref[rows, sl], r.rv_ref[rows, sl]
            qf = q2.astype(F32)
            kf = None
        else:
            q2 = r.qk_ref[rows, sl]
            k2 = r.qk_ref[rows, 4 * LANES + (p - 4) * LANES:4 * LANES + (p - 3) * LANES]
            v2 = r.mv_ref[rows, sl]
            qf, kf = q2.astype(F32), k2.astype(F32)
        vt = v2.astype(F32).T.astype(BF16)
        qb.append(q2)
        kb.append(k2)
        vts.append(vt)
        kf32.append(kf)
        qt.append(qf.T)
        vbd.append([jnp.where(sub_lo, vt, jnp.zeros_like(vt)), jnp.where(sub_lo, jnp.zeros_like(vt), vt)])
    yield

    st2 = []
    for p in range(8):
        masks = mask_ret if p < 4 else mask_nat
        zero = jnp.zeros_like(qb[p])
        qstack = jnp.concatenate([jnp.where(masks[0], qb[p], zero), jnp.where(masks[1], qb[p], zero)], axis=0)
        st2.append(_dot_nt(kb[p], qstack))
    yield

    qn = [_dot_nt(jnp.concatenate([r.n_scr[p].astype(BF16), r.nmb_ref[blk, p]], axis=0), qb[4 + p])
          for p in range(4)]
    qn_f = [x[0:HALO] for x in qn]
    qn_b = [x[HALO:2 * HALO] for x in qn]
    cf2, cb2, bf2, bb2 = cf_all * LOG2E, cb_all * LOG2E, bf_row * LOG2E, bb_row * LOG2E
    m_f2, m_b2 = r.m_state[0:1, :] * LOG2E, r.mmb_ref[blk, 0:1, :] * LOG2E
    lhs, rhs = [], []
    for p in range(8):
        if p < 4:
            pts = [(st2[p][:, a * LANES:(a + 1) * LANES] * r.dm_scr[2 * p + a]).astype(BF16) for a in range(2)]
            x_f, x_b = r.dec_scr[0, p], r.dec_scr[1, p]
            old = [r.s_scr[p].astype(BF16), r.sretb_ref[blk, p]]
        else:
            pts, cf, cb = [], [], []
            for a in range(2):
                h = 2 * (p - 4) + a
                st = st2[p][:, a * LANES:(a + 1) * LANES]
                pf, coef_f = _mlstm_dir_weights(st, qn_f[p - 4][a:a + 1, :], cf2[:, h:h + 1],
                                                bf2[8 + h:9 + h, :], m_f2[0:1, h:h + 1], le)
                pb, coef_b = _mlstm_dir_weights(st, qn_b[p - 4][a:a + 1, :], cb2[:, 16 + h:17 + h],
                                                bb2[24 + h:25 + h, :], m_b2[0:1, 16 + h:17 + h], ge)
                pts.append((pf + pb).astype(BF16))
                cf.append(coef_f)
                cb.append(coef_b)
            x_f = jnp.where(sub_lo, cf[0], cf[1])
            x_b = jnp.where(sub_lo, cb[0], cb[1])
            old = [r.c_scr[p - 4].astype(BF16), r.cmb_ref[blk, p - 4]]
        lhs.append(jnp.concatenate(vbd[p] + old, axis=1))
        rhs.append(jnp.concatenate(pts + [(qt[p] * x_f).astype(BF16), (qt[p] * x_b).astype(BF16)], axis=0))
    yield

    ht = [_dot(lhs[p], rhs[p]) for p in range(8)]
    yield

    for p in range(8):
        sl = slice((p % 4) * LANES, (p % 4 + 1) * LANES)
        gate_ref = r.rg_ref if p < 4 else r.mo_ref
        y = _heads_out(ht[p], r.nw_ref[p])
        new = (y * gate_ref[rows, sl].astype(F32)).astype(BF16)
        r.mix_scr[rows, p * LANES:(p + 1) * LANES] = jnp.where(r.live, new, r.mix_scr[rows, p * LANES:(p + 1) * LANES])
    yield

    for p in range(4):
        cd = jnp.exp(r.lgk_ref[p:p + 1, :] * float(CHUNK))
        _ret_state_update(r.s_scr, p, kb[p], vts[p], r.dec_scr[2, p], cd, bd_ret)
    _mlstm_state_update(r.c_scr, r.n_scr, r.m_state, kf32[4:], vts[4:], cf_all, bal_f[CHUNK - 1:CHUNK, :],
                        0, lo, bd_m)
    yield


def _attn_inproj_stages(x, ab_ref, wq_ref, wk_ref, wv_ref, nw_ref, cos, sin, y_ref):
    hb = _norm_mod(x, ab_ref)
    r2 = lax.broadcasted_iota(jnp.int32, (2 * LANES, 2 * LANES), 0)
    c2 = lax.broadcasted_iota(jnp.int32, (2 * LANES, 2 * LANES), 1)
    same_head = (((r2 ^ c2) & (LANES | 32)) == 0).astype(BF16)
    acc_q = _dot(hb, wq_ref[...])
    acc_k = _dot(hb, wk_ref[...])
    y_ref[:, 10 * LANES:12 * LANES] = _dot(hb, wv_ref[...]).astype(BF16)
    yield
    for j in range(5):
        acc = acc_q[:, j * 2 * LANES:(j + 1) * 2 * LANES] if j < 4 else acc_k
        sq = acc * acc
        ms = _dot(sq.astype(BF16), same_head) * (1.0 / HEAD_DIM)
        nrm = acc * lax.rsqrt(ms + EPS)
        nw = nw_ref[0:1, :] if j < 4 else nw_ref[1:2, :]
        for v in range(2):
            ls = slice(v * LANES, (v + 1) * LANES)
            y_ref[:, (2 * j + v) * LANES:(2 * j + v + 1) * LANES] = _rope(nrm[:, ls] * nw, cos, sin).astype(BF16)
        if j in (1, 4):
            yield


def _ffn_splits(d_ff, pieces):
    blocks = d_ff // (2 * LANES)
    assert blocks * 2 * LANES == d_ff and blocks >= pieces
    cuts = [((i * blocks) // pieces) * 2 * LANES for i in range(pieces)]
    return cuts + [d_ff]


def _mixer_ffn_kernel(lg_ref, y_ref, qk_ref,
                      g_ref, gt_ref, sretb_ref, cmb_ref, nmb_ref, mmb_ref,
                      lgk_ref, lgkt_ref, nw_ref,
                      ctx_ref, x_ref, p_ref, wo_ref, wi_ref, w2_ref,
                      ab2_ref, waq_ref, wak_ref, wav_ref, nwa_ref, cos_ref, sin_ref, wi_src, w2_src,
                      o_ref, y2_ref, wi_dst, w2_dst,
                      s_scr, c_scr, n_scr, m_state, dm_scr, dec_scr, mix_scr, act_scr, x2_scr,
                      *, tiles, tpb, d_ff):
    s = pl.program_id(0)
    _wcast((wi_src, w2_src), (wi_dst, w2_dst))
    nb = tiles // tpb
    jt = _div_small(jnp.minimum(s, tiles - 1), tpb, nb)[1]
    gw = 4 * LANES
    rq_ref, rk_ref, rv_ref, rg_ref, mv_ref, mo_ref = [y_ref.at[:, j * gw:(j + 1) * gw] for j in range(6)]
    lane = _lane()
    sub = lax.broadcasted_iota(jnp.int32, (CHUNK, LANES), 0)

    @pl.when(s == 0)
    def _():
        mix_scr[...] = jnp.zeros_like(mix_scr)
        x2_scr[...] = jnp.zeros_like(x2_scr)
        le = sub <= lane
        ge = sub >= lane
        spos = sub.astype(F32)
        tpos = lane.astype(F32)
        diff = (lane - sub).astype(F32)
        for h in range(8):
            dm_scr[h] = (jnp.where(le, jnp.exp(lg_ref[h] * diff), 0.0)
                         + jnp.where(ge, jnp.exp(lg_ref[8 + h] * (-diff)), 0.0))
        for p in range(4):
            dec_scr[0, p] = jnp.exp(lgkt_ref[:, p:p + 1] * (tpos + 1.0))
            dec_scr[1, p] = jnp.exp(lgkt_ref[:, 4 + p:5 + p] * (float(CHUNK) - tpos))
            dec_scr[2, p] = jnp.exp(lgk_ref[p:p + 1, :] * (float(CHUNK) - 1.0 - spos))

    @pl.when(jt == 0)
    def _():
        s_scr[...] = jnp.zeros_like(s_scr)
        c_scr[...] = jnp.zeros_like(c_scr)
        n_scr[...] = jnp.zeros_like(n_scr)
        m_state[...] = jnp.zeros_like(m_state)

    r = _Bag(rq_ref=rq_ref, rk_ref=rk_ref, rv_ref=rv_ref, rg_ref=rg_ref, qk_ref=qk_ref, mv_ref=mv_ref, mo_ref=mo_ref,
             g_ref=g_ref, gt_ref=gt_ref, sretb_ref=sretb_ref, cmb_ref=cmb_ref, nmb_ref=nmb_ref, mmb_ref=mmb_ref,
             lgk_ref=lgk_ref, nw_ref=nw_ref, s_scr=s_scr, c_scr=c_scr, n_scr=n_scr, m_state=m_state,
             dm_scr=dm_scr, dec_scr=dec_scr, mix_scr=mix_scr, live=s < tiles)
    cuts = _ffn_splits(d_ff, 3)
    ffn_piece = lambda i: _ffn_cols(h, wi_ref, act_scr, cuts[i], cuts[i + 1], d_ff)

    x = _ctx_or_latent_rows(ctx_ref, x_ref, jnp.clip(s - 1, 0, tiles - 1), nb, tpb)
    x1, h = _ffn_in(x, mix_scr[...], p_ref, wo_ref)

    nxt = _attn_inproj_stages(x2_scr[...], ab2_ref, waq_ref, wak_ref, wav_ref, nwa_ref, cos_ref[...], sin_ref[...],
                              y2_ref)
    chunk_a, chunk_b = _mixer_chunk_stages(0, r), _mixer_chunk_stages(1, r)
    next(nxt)
    next(chunk_a), next(chunk_b)
    next(chunk_a), next(chunk_b)
    ffn_piece(0)
    next(chunk_a), next(chunk_a)
    ffn_piece(1)
    next(chunk_a), next(chunk_a)
    next(nxt)
    ffn_piece(2)
    next(chunk_b), next(chunk_b)
    next(nxt)
    x2 = x1 + p_ref[3:4, :] * _dot(act_scr[...], w2_ref[...])
    o_ref[...] = x2
    next(chunk_b), next(chunk_b)
    x2_scr[...] = x2


def _mixer_ffn(y, g, gt, states, lg_smem, lgk, lgkt, nw, xs, prm, wo, wi, w2,
               ab_next, w_next, nw_next, cos, sin, ffn_w, nb, tpb):
    gw = 4 * LANES
    tm = ROW_TILE
    cpt = tm // CHUNK
    sretb, cmb, nmb, mmb, qk_act = states
    tiles = nb * tpb
    d = xs[-1].shape[1]
    d_ff = w2.shape[0]
    n_next = sum(a.shape[1] for a in w_next)
    mix = lambda s: jnp.minimum(s, tiles - 1)
    ffn = lambda s: jnp.clip(s - 1, 0, tiles - 1)
    nxt = lambda s: jnp.maximum(s - 2, 0)
    state_spec = lambda a: pl.BlockSpec((cpt,) + a.shape[1:], lambda s: (mix(s),) + (0,) * (a.ndim - 1))
    def sel(t):
        b, j = _div_small(t, tpb, nb)
        return b * 2 + jnp.minimum(j, 1), 0, 0

    wc_in, wc_out, wc_shapes = _wcast_specs(*ffn_w, 1, tiles + 2)
    return pl.pallas_call(
        functools.partial(_mixer_ffn_kernel, tiles=tiles, tpb=tpb, d_ff=d_ff),
        grid=(tiles + 2,),
        in_specs=[_smem_spec(),
                  pl.BlockSpec((tm, 6 * gw), lambda s: (mix(s), 0)),
                  pl.BlockSpec((tm, 2 * gw), lambda s: (mix(s), 0)),
                  pl.BlockSpec((tm, LANES), lambda s: (mix(s), 0)),
                  pl.BlockSpec((32, tm), lambda s: (0, mix(s))),
                  state_spec(sretb), state_spec(cmb), state_spec(nmb), state_spec(mmb),
                  _const_spec(lgk.shape), _const_spec(lgkt.shape), _const_spec(nw.shape)]
                 + _split_row_specs(tm, d, nb, tpb, ffn)
                 + [pl.BlockSpec((None, 8, d), lambda s: sel(ffn(s))),
                    _const_spec(wo.shape), _const_spec(wi.shape), _const_spec(w2.shape),
                    pl.BlockSpec((None, 2, d), lambda s: sel(nxt(s))),
                    *[_const_spec(a.shape) for a in w_next], _const_spec(nw_next.shape),
                    pl.BlockSpec((tm, LANES), lambda s: (_div_small(nxt(s), tpb, nb)[1], 0)),
                    pl.BlockSpec((tm, LANES), lambda s: (_div_small(nxt(s), tpb, nb)[1], 0))] + wc_in,
        out_specs=[pl.BlockSpec((tm, d), lambda s: (ffn(s), 0)),
                   pl.BlockSpec((tm, n_next), lambda s: (nxt(s), 0))] + wc_out,
        out_shape=[jax.ShapeDtypeStruct((tiles * tm, d), F32),
                   jax.ShapeDtypeStruct((tiles * tm, n_next), BF16)] + wc_shapes,
        scratch_shapes=[pltpu.VMEM((4, CHUNK, LANES), F32), pltpu.VMEM((4, CHUNK, LANES), F32),
                        pltpu.VMEM((4, HALO, LANES), F32), pltpu.VMEM((8, LANES), F32),
                        pltpu.VMEM((8, CHUNK, LANES), F32), pltpu.VMEM((3, 4, CHUNK, LANES), F32),
                        pltpu.VMEM((tm, 2 * gw), BF16), pltpu.VMEM((tm, d_ff), BF16),
                        pltpu.VMEM((tm, d), F32)],
        compiler_params=_cparams("arbitrary"),
        name="ret_mlstm_mixer_ffn",
    )(lg_smem, y, qk_act, g, gt, sretb, cmb, nmb, mmb, lgk, lgkt, nw,
      *xs, prm, wo, wi, w2, ab_next, *w_next, nw_next, cos, sin, *ffn_w)


def _ffn_in(x, m, p_ref, wo_ref):
    x1 = x + p_ref[0:1, :] * _dot(m, wo_ref[...])
    ms = jnp.mean(x1 * x1, axis=-1, keepdims=True)
    return x1, ((x1 * lax.rsqrt(ms + EPS)) * p_ref[1:2, :] + p_ref[2:3, :]).astype(BF16)


def _ffn_cols(h, wi_ref, act_scr, lo, hi, d_ff):
    gate = _dot(h, wi_ref[:, lo:hi])
    up = _dot(h, wi_ref[:, d_ff + lo:d_ff + hi])
    act_scr[:, lo:hi] = (gate * jax.nn.sigmoid(gate) * up).astype(BF16)


def _window_bias():
    kk = np.arange(CHUNK)[:, None]
    t = np.arange(CHUNK)[None, :]
    tabs = []
    for has_prev, has_next in ((False, True), (True, True), (True, False)):
        prev_ok = (kk >= t) & has_prev
        next_ok = (kk <= t) & has_next
        tabs.append(np.where(np.concatenate([prev_ok, next_ok], axis=0), 0.0, NEG))
    return jnp.asarray(np.stack(tabs), F32)


def _attn_ffn_kernel(sink_ref, qkv_a_ref, qkv_b_ref, kvp_ref, kvn_ref, kvx_ref, bias_ref,
                     x_a_ref, x_b_ref, p_ref, wo_ref, wi_ref, w2_ref, o_ref, m_scr, act_scr,
                     *, steps, per_batch, d_ff):
    s = pl.program_id(0)
    kw = 2 * LANES
    k0 = 8 * LANES
    tm = qkv_a_ref.shape[0]
    nblk = 2 * tm // CHUNK

    @pl.when(s == 0)
    def _():
        m_scr[...] = jnp.zeros_like(m_scr)

    j = _div_small(jnp.minimum(s, steps - 1), per_batch, steps // per_batch)[1]
    grp = lax.broadcasted_iota(jnp.int32, (1, 4 * CHUNK), 1) // CHUNK
    lane = _lane()
    mask_q = [(lane & 32) == 0, (lane & 32) != 0]
    tile4 = lambda b: jnp.concatenate([b] * 4, axis=1)
    inner = tile4(bias_ref[1])
    biases = ([tile4(jnp.where(j == 0, bias_ref[0], bias_ref[1]))] + [inner] * (nblk - 2)
              + [tile4(jnp.where(j == per_batch - 1, bias_ref[2], bias_ref[1]))])
    cuts = _ffn_splits(d_ff, nblk - 1)

    def ffn_up(i):
        _ffn_cols(h, wi_ref, act_scr, cuts[i], cuts[i + 1], d_ff)

    def q_rows(blk):
        ref = qkv_a_ref if blk * CHUNK < tm else qkv_b_ref
        r0 = (blk * CHUNK) % tm
        return ref, slice(r0, r0 + CHUNK)

    def scores(blk, kv):
        kvp, a = kv // 2, kv % 2
        ref, rows = q_rows(blk)
        qs = jnp.concatenate(
            [jnp.where(mask_q[a], ref[rows, (kvp * 4 + g) * LANES:(kvp * 4 + g + 1) * LANES],
                       jnp.zeros((CHUNK, LANES), BF16)) for g in range(4)], axis=0)
        return _dot_nt(kcats[kvp][blk], qs)

    def softmax_pv(blk, kv, st):
        bias = biases[blk]
        st = jnp.concatenate([st[0:CHUNK] + bias[0:CHUNK], st[CHUNK:2 * CHUNK],
                              st[2 * CHUNK:3 * CHUNK] + bias[CHUNK:2 * CHUNK], st[3 * CHUNK:]], axis=0)
        snk = jnp.where(grp == 0, sink_ref[kv * 4],
                        jnp.where(grp == 1, sink_ref[kv * 4 + 1],
                                  jnp.where(grp == 2, sink_ref[kv * 4 + 2], sink_ref[kv * 4 + 3])))
        m = jnp.maximum(jnp.max(st, axis=0, keepdims=True), snk)
        e = jnp.exp2(st - m)
        denom = jnp.exp2(snk - m) + jnp.sum(e, axis=0, keepdims=True)
        a = kv % 2
        return _dot(vts[kv // 2][blk][a * HEAD_DIM:(a + 1) * HEAD_DIM, :], e.astype(BF16)) * (1.0 / denom)

    def hand_over(blk, outs):
        for kvp in range(2):
            full = jnp.concatenate(outs[2 * kvp:2 * kvp + 2], axis=0)
            for g in range(4):
                m_scr[blk * CHUNK:(blk + 1) * CHUNK, (kvp * 4 + g) * LANES:(kvp * 4 + g + 1) * LANES] = (
                    full[:, g * CHUNK:(g + 1) * CHUNK].T.astype(BF16))

    x1, h = _ffn_in(jnp.concatenate([x_a_ref[...], x_b_ref[...]], axis=0), m_scr[...], p_ref, wo_ref)

    kcats, vts = [], []
    for kvp in range(2):
        ks = slice(k0 + kvp * LANES, k0 + (kvp + 1) * LANES)
        vs = slice(k0 + kw + kvp * LANES, k0 + kw + (kvp + 1) * LANES)
        es, ev = slice(kvp * LANES, (kvp + 1) * LANES), slice(kw + kvp * LANES, kw + (kvp + 1) * LANES)
        tile_rows = [(ref, slice(c * CHUNK, (c + 1) * CHUNK)) for ref in (qkv_a_ref, qkv_b_ref)
                     for c in range(tm // CHUNK)]
        k_chunks = [kvp_ref[:, es]] + [ref[rows, ks] for ref, rows in tile_rows] + [kvn_ref[:, es]]
        v_chunks = ([kvp_ref[:, ev]] + [ref[rows, vs] for ref, rows in tile_rows] + [kvn_ref[:, ev]]
                    + [kvx_ref[c * CHUNK:(c + 1) * CHUNK, ev] for c in range(kvx_ref.shape[0] // CHUNK)])
        v_t = [v.astype(F32).T.astype(BF16) for v in v_chunks]
        kcats.append([jnp.concatenate(k_chunks[b:b + 3] + [kvx_ref[:, es]], axis=0) for b in range(nblk)])
        vts.append([jnp.concatenate(v_t[b:b + 3] + v_t[nblk + 2:], axis=1) for b in range(nblk)])

    for blk in range(nblk):
        sts = [scores(blk, kv) for kv in range(4)]
        if blk < nblk - 1:
            ffn_up(blk)
        else:
            o_ref[...] = x1 + p_ref[3:4, :] * _dot(act_scr[...], w2_ref[...])
        hand_over(blk, [softmax_pv(blk, kv, sts[kv]) for kv in range(4)])


def _attn_ffn(y, sink, xc, prm, wo, wi, w2, nb, nc, ctx_chunks):
    tm = ROW_TILE
    cpt = tm // CHUNK
    tpb = nc // cpt
    lat_tiles = (nc - ctx_chunks) // cpt
    ctx_tiles = ctx_chunks // cpt
    assert ctx_tiles == 1 and lat_tiles % 2 == 0 and lat_tiles >= 4
    per_batch = lat_tiles // 2
    steps = nb * per_batch
    d = xc.shape[1]
    d_ff = w2.shape[0]
    bias = _window_bias()
    n_qkv = y.shape[1]
    kv_w = n_qkv - d
    assert d % kv_w == 0
    kv_col = d // kv_w
    att = lambda s: jnp.minimum(s, steps - 1)
    ffn = lambda s: jnp.maximum(s - 1, 0)
    split = lambda t: _div_small(t, per_batch, nb)
    row_tile = lambda t, u: split(t)[0] * tpb + ctx_tiles + 2 * split(t)[1] + u
    chunk0 = lambda t: split(t)[0] * nc + ctx_chunks
    prev_c = lambda s: chunk0(att(s)) + jnp.maximum(split(att(s))[1] * 2 * cpt - 1, 0)
    next_c = lambda s: chunk0(att(s)) + jnp.minimum(split(att(s))[1] * 2 * cpt + 2 * cpt, lat_tiles * cpt - 1)
    edge_spec = lambda f: pl.BlockSpec((CHUNK, kv_w), lambda s: (f(s), kv_col))
    return pl.pallas_call(
        functools.partial(_attn_ffn_kernel, steps=steps, per_batch=per_batch, d_ff=d_ff),
        grid=(steps + 1,),
        in_specs=[_smem_spec(),
                  pl.BlockSpec((tm, n_qkv), lambda s: (row_tile(att(s), 0), 0)),
                  pl.BlockSpec((tm, n_qkv), lambda s: (row_tile(att(s), 1), 0)),
                  edge_spec(prev_c), edge_spec(next_c),
                  pl.BlockSpec((tm, kv_w), lambda s: (split(att(s))[0] * tpb, kv_col)),
                  _const_spec(bias.shape),
                  pl.BlockSpec((tm, d), lambda s: (row_tile(ffn(s), 0), 0)),
                  pl.BlockSpec((tm, d), lambda s: (row_tile(ffn(s), 1), 0)),
                  pl.BlockSpec((None, 8, d), lambda s: (split(ffn(s))[0] * 2 + 1, 0, 0)),
                  _const_spec(wo.shape), _const_spec(wi.shape), _const_spec(w2.shape)],
        out_specs=pl.BlockSpec((2 * tm, d), lambda s: (ffn(s), 0)),
        out_shape=jax.ShapeDtypeStruct((steps * 2 * tm, d), F32),
        scratch_shapes=[pltpu.VMEM((2 * tm, d), BF16), pltpu.VMEM((2 * tm, d_ff), BF16)],
        compiler_params=_cparams("arbitrary"),
        name="window_gqa_ffn",
    )(sink, y, y, y, y, y, bias, xc, xc, prm, wo, wi, w2)


def _pair_cols(w):
    rows, cols = w.shape
    return w.reshape(rows, cols // LANES, 2, 2, 32).transpose(0, 1, 3, 2, 4).reshape(rows, cols)


def _attn_q_cols(w):
    rows = w.shape[0]
    g_per = w.shape[1] // (H_KV * HEAD_DIM)
    return (w.reshape(rows, H_KV // 2, 2, g_per, 2, 32).transpose(0, 1, 3, 4, 2, 5)
            .reshape(rows, w.shape[1]))


def _attn_o_rows(w):
    cols = w.shape[1]
    g_per = w.shape[0] // (H_KV * HEAD_DIM)
    return (w.reshape(H_KV // 2, 2, g_per, HEAD_DIM, cols).transpose(0, 2, 1, 3, 4)
            .reshape(w.shape[0], cols))


def _rope_tables(seq, ctx_len):
    rows = seq // GRID_W
    row = np.repeat(np.arange(rows, dtype=np.float32), GRID_W)
    col = np.tile(np.arange(GRID_W, dtype=np.float32), rows)
    n = HEAD_DIM // 4
    inv = (np.float32(ROPE_BASE) ** (-np.arange(n, dtype=np.float32) / np.float32(n))).astype(np.float32)
    ang = np.concatenate([row[:, None] * inv, col[:, None] * inv], axis=-1).astype(np.float32)
    cos, sin = np.cos(ang), np.sin(ang)
    cos_t = np.concatenate([np.ones((ctx_len, LANES), np.float32), np.tile(cos, (1, 4))], axis=0)
    sin_t = np.concatenate([np.zeros((ctx_len, LANES), np.float32),
                            np.concatenate([-sin, -sin, sin, sin], axis=-1)], axis=0)
    return jnp.asarray(cos_t, F32), jnp.asarray(sin_t, F32)


def _mod_tables(mod, nb, norm_w):
    d = norm_w.shape[-1]
    lat = mod[:nb].reshape(nb, 6, d)
    ctx = jnp.broadcast_to(mod[nb].reshape(1, 6, d), (nb, 6, d))
    both = jnp.stack([ctx, lat], axis=1).reshape(nb * 2, 6, d)
    sh1, sc1, g1, sh2, sc2, g2 = [both[:, k] for k in range(6)]
    ab1 = jnp.stack([norm_w[0] * (1.0 + sc1), sh1], axis=1)
    zeros = jnp.zeros_like(g1)
    prm = jnp.stack([g1, norm_w[1] * (1.0 + sc2), sh2, g2, zeros, zeros, zeros, zeros], axis=1)
    return ab1, prm


def kernel(x, c, ctx, c_ctx, ada_w, ada_b, norm_w, ffn_w_in, ffn_w_out, ab_w_in, ab_w_out,
           ret_log_gamma, ret_norm_w, mlstm_conv_w, mlstm_conv_b, mlstm_gate_b, mlstm_norm_w,
           attn_w_in, attn_w_out, attn_q_norm_w, attn_k_norm_w, attn_sink):
    nb, seq, d = x.shape
    ctx_len = ctx.shape[1]
    depth = ada_w.shape[0]
    assert ctx_len == ROW_TILE and seq % ROW_TILE == 0 and d == 8 * LANES and nb < 8
    t_all = ctx_len + seq
    nc = t_all // CHUNK
    ctx_chunks = ctx_len // CHUNK
    tpb = t_all // ROW_TILE
    dr = d // 2

    rows = jnp.zeros((8, d), F32).at[:nb].set(c).at[nb].set(c_ctx)
    mod_all = _modulation(rows, ada_w, ada_b)
    cos_t, sin_t = _rope_tables(seq, ctx_len)
    ffn_w = (ffn_w_in, ffn_w_out)
    xs = (ctx.reshape(nb * ctx_len, d), x.reshape(nb * seq, d))

    assert depth == 2 and ab_w_in.shape[0] == 1 and attn_w_in.shape[0] == 1
    ab_0, prm_0 = _mod_tables(mod_all[0], nb, norm_w[0])
    ab_1, prm_1 = _mod_tables(mod_all[1], nb, norm_w[1])

    w_in_0 = ab_w_in[0].astype(BF16)
    assert w_in_0.shape[1] == 8 * dr + 32
    gb = jnp.zeros((1, LANES), F32).at[0, :32].set(mlstm_gate_b[0].reshape(-1))
    gbt = mlstm_gate_b[0].reshape(32, 1)
    lg = ret_log_gamma[0].astype(F32)
    lgk = jnp.tile(jnp.repeat(lg.reshape(2, 4, 2), 32, axis=-1), (1, 1, 2)).reshape(8, LANES)
    cw = jnp.concatenate([mlstm_conv_w[0], mlstm_conv_b[0][None], jnp.zeros((4, 2 * dr), F32)], axis=0)
    nw = jnp.broadcast_to(jnp.concatenate([ret_norm_w[0], mlstm_norm_w[0]]).reshape(8, LANES, 1),
                          (8, LANES, LANES))
    wo_0 = ab_w_out[0].astype(BF16)

    w = attn_w_in[0]
    w_attn = (_attn_q_cols(w[:, :d]).astype(BF16), _pair_cols(w[:, d:d + 2 * LANES]).astype(BF16),
              w[:, d + 2 * LANES:].astype(BF16))
    lane_w = lambda v: jnp.concatenate([v[:32], v[:32], v[32:], v[32:]])
    nwq = jnp.stack([lane_w(attn_q_norm_w[0]) * (HEAD_DIM ** -0.5 * LOG2E), lane_w(attn_k_norm_w[0])]
                    + [jnp.zeros((LANES,), F32)] * 6)
    wo_1 = _attn_o_rows(attn_w_out[0]).astype(BF16)

    y, g, gt, *states, wi_0, w2_0 = _inproj_sweep(*xs, ab_0, w_in_0, gb, gbt, cos_t, sin_t, lgk, cw,
                                                  ffn_w, nb, tpb)
    x_mid, y_attn, wi_1, w2_1 = _mixer_ffn(y, g, gt, states, lg.reshape(-1), lgk, lgk.T, nw, xs, prm_0, wo_0,
                                           wi_0, w2_0, ab_1, w_attn, nwq, cos_t, sin_t, ffn_w, nb, tpb)
    out = _attn_ffn(y_attn, attn_sink[0].astype(F32) * LOG2E, x_mid, prm_1, wo_1, wi_1, w2_1, nb, nc, ctx_chunks)
    return out.reshape(nb, seq, d)
```

```python
import functools

import numpy as np
import jax
import jax.numpy as jnp
from jax import lax
from jax.experimental import pallas as pl
from jax.experimental.pallas import tpu as pltpu

F32 = jnp.float32
BF16 = jnp.bfloat16

HEAD_DIM = 64
CHUNK = 128
GRID_W = 64
ROPE_BASE = 10000.0
EPS = 1e-6
H_KV = 4
LANES = 128
ROW_TILE = 256
HALO = 16
LOG2E = 1.4426950408889634
NEG = -1e30
VMEM_LIMIT = 56 * 1024 * 1024


def _cparams(*sem):
    return pltpu.CompilerParams(dimension_semantics=sem, vmem_limit_bytes=VMEM_LIMIT)


def _const_spec(shape):
    nd = len(shape)
    return pl.BlockSpec(shape, lambda *_: (0,) * nd, pipeline_mode=pl.Buffered(1))


def _smem_spec():
    return pl.BlockSpec(memory_space=pltpu.SMEM)


def _wcast_specs(w_in_all, w_out_all, layer, steps):
    specs_in, specs_out, shapes = [], [], []
    for w in (w_in_all, w_out_all):
        total, cols = w.shape[1:]
        rows = next(r for r in range(HALO, total + 1, HALO) if total % r == 0 and total // r <= steps)
        n_blocks = total // rows
        specs_in.append(pl.BlockSpec((None, rows, cols),
                                     lambda i, n=n_blocks: (layer, jnp.minimum(i, n - 1), 0)))
        specs_out.append(pl.BlockSpec((rows, cols), lambda i, n=n_blocks: (jnp.minimum(i, n - 1), 0)))
        shapes.append(jax.ShapeDtypeStruct(w.shape[1:], BF16))
    return specs_in, specs_out, shapes


def _wcast(src_refs, dst_refs):
    for src, dst in zip(src_refs, dst_refs):
        dst[...] = src[...].astype(BF16)


def _lane(shape=(CHUNK, LANES)):
    return lax.broadcasted_iota(jnp.int32, shape, len(shape) - 1)


def _dot(a, b):
    return jnp.dot(a, b, preferred_element_type=F32)


def _dot_nt(a, b):
    return lax.dot_general(a, b, (((1,), (1,)), ((), ())), preferred_element_type=F32)


def _div_small(t, m, n):
    q = 0
    for b in range(1, n):
        q = q + jnp.where(t >= b * m, 1, 0)
    return q, t - q * m


def _split3(x):
    hi = x.astype(BF16)
    r = x - hi.astype(F32)
    mid = r.astype(BF16)
    lo = (r - mid.astype(F32)).astype(BF16)
    return hi, mid, lo


def _log_sigmoid(x):
    return jnp.minimum(x, 0.0) - jnp.log1p(jnp.exp(-jnp.abs(x)))


def _rope(x, cos, sin_signed):
    return x * cos + pltpu.roll(x, LANES // 2, 1) * sin_signed


def _mod_kernel(rows_ref, w_ref, b_ref, o_ref):
    a = rows_ref[...]
    a = a * jax.nn.sigmoid(a)
    a_hi = a.astype(BF16)
    a_lo = (a - a_hi.astype(F32)).astype(BF16)
    w = w_ref[...]
    w_hi = w.astype(BF16)
    w_lo = (w - w_hi.astype(F32)).astype(BF16)
    o_ref[...] = _dot(a_hi, w_hi) + _dot(a_hi, w_lo) + _dot(a_lo, w_hi) + b_ref[...]


def _modulation(rows, ada_w, ada_b):
    depth, d, n = ada_w.shape
    tn = n // 4
    return pl.pallas_call(
        _mod_kernel,
        grid=(depth, n // tn),
        in_specs=[pl.BlockSpec((8, d), lambda l, j: (0, 0)),
                  pl.BlockSpec((None, d, tn), lambda l, j: (l, 0, j)),
                  pl.BlockSpec((None, 1, tn), lambda l, j: (l, 0, j))],
        out_specs=pl.BlockSpec((None, 8, tn), lambda l, j: (l, 0, j)),
        out_shape=jax.ShapeDtypeStruct((depth, 8, n), F32),
        compiler_params=_cparams("arbitrary", "arbitrary"),
        name="adaln_modulation",
    )(rows, ada_w, ada_b.reshape(depth, 1, n))


def _norm_mod(x, ab_ref):
    ms = jnp.mean(x * x, axis=-1, keepdims=True)
    h = (x * lax.rsqrt(ms + EPS)) * ab_ref[0:1, :] + ab_ref[1:2, :]
    return h.astype(BF16)


def _ctx_or_latent_rows(ctx_ref, x_ref, tile, nb, tiles_per_batch):
    return jnp.where(_div_small(tile, tiles_per_batch, nb)[1] == 0, ctx_ref[...], x_ref[...])


def _split_row_specs(tm, d, nb, tpb, tile_of):
    lat = tpb - 1

    def latent_row(i):
        b, j = _div_small(tile_of(i), tpb, nb)
        return b * lat + jnp.maximum(j - 1, 0), 0

    return [pl.BlockSpec((tm, d), lambda i: (_div_small(tile_of(i), tpb, nb)[0], 0)),
            pl.BlockSpec((tm, d), latent_row)]


def _cumsum_cols(tri_bf, lf):
    hi, mid, lo = _split3(lf)
    return _dot(tri_bf, hi) + _dot(tri_bf, mid) + _dot(tri_bf, lo)


def _cumsum_rows(lf, tri_bf):
    hi, mid, lo = _split3(lf)
    return _dot(hi, tri_bf) + _dot(mid, tri_bf) + _dot(lo, tri_bf)


def _ret_state_update(s_ref, p, k2, vt, kdec, cd_lanes, bd):
    kf = (k2.astype(F32) * kdec).astype(BF16)
    s_ref[p] = s_ref[p] * cd_lanes + jnp.where(bd, _dot(vt, kf), 0.0)


def _mlstm_state_update(c_ref, n_ref, m_ref, k_pairs, vt_pairs, c_all, bend, col0, lo, bd):
    cmax = jnp.max(c_all, axis=0, keepdims=True)
    w_all = jnp.exp(c_all - cmax)
    m_old = m_ref[0:1, :]
    mrel = jnp.maximum(m_old, cmax)
    a_row = jnp.exp(m_old - mrel)
    bb_row = jnp.exp(cmax - mrel)
    m_ref[0:1, :] = bend + mrel
    lo_row = lo[0:1, :]
    for p in range(4):
        h0 = col0 + 2 * p
        kw = k_pairs[p] * jnp.where(lo, w_all[:, h0:h0 + 1], w_all[:, h0 + 1:h0 + 2])
        kvt = _dot(vt_pairs[p], kw.astype(BF16))
        nloc = jnp.sum(kw, axis=0, keepdims=True)
        a_l = jnp.where(lo_row, a_row[:, h0:h0 + 1], a_row[:, h0 + 1:h0 + 2])
        bb_l = jnp.where(lo_row, bb_row[:, h0:h0 + 1], bb_row[:, h0 + 1:h0 + 2])
        c_ref[p] = c_ref[p] * a_l + jnp.where(bd, kvt, 0.0) * bb_l
        n_new = (n_ref[p, 0:1, :] + n_ref[p, 1:2, :]) * a_l + nloc * bb_l
        n_ref[p, 0:1, :] = jnp.where(lo_row, n_new, 0.0)
        n_ref[p, 1:2, :] = jnp.where(lo_row, 0.0, n_new)


def _mlstm_dir_weights(st, qn_row, c_col, bt_row, m_prev, tri):
    dl = jnp.where(tri, c_col + bt_row, NEG)
    mx = jnp.max(dl, axis=0, keepdims=True)
    al = bt_row + m_prev
    m_t = jnp.maximum(al, mx)
    w = jnp.exp2(dl - m_t)
    a_t = jnp.exp2(al - m_t)
    sw = st * w
    den = jnp.sum(sw, axis=0, keepdims=True) + a_t * qn_row
    r = 1.0 / jnp.maximum(jnp.abs(den), jnp.exp2(-m_t))
    return sw * r, a_t * r


def _heads_out(ht, nw_tab):
    rows = []
    for a in range(2):
        ha = ht[a * HEAD_DIM:(a + 1) * HEAD_DIM, :]
        ms = jnp.mean(ha * ha, axis=0, keepdims=True)
        rows.append(ha * lax.rsqrt(ms + EPS))
    return (jnp.concatenate(rows, axis=0) * nw_tab).T


def _inproj_sweep_kernel(ctx_ref, x_ref, ab_ref, w_ref, wgate_ref, gb_ref, gbt_ref, cos_ref, sin_ref,
                         lgk_ref, cw_ref, wi_src, w2_src,
                         y_ref, g_ref, gt_ref, sret_ref, cm_ref, nm_ref, mm_ref, qk_ref, wi_dst, w2_dst,
                         s_scr, c_scr, n_scr, m_scr, kdec_scr,
                         p_rk, p_rv, p_mv, p_mqk, p_g, next_row_scr, w_scr, wg_scr, wgt_scr, *, tiles, tpb):
    i = pl.program_id(0)
    _wcast((wi_src, w2_src), (wi_dst, w2_dst))

    def order(t):
        v = _div_small(t, tpb, tiles // tpb)[1]
        return jnp.where(v == 0, 0, tpb - v)

    jt_a = order(jnp.maximum(i - 1, 0))
    lane = _lane()
    sub = lax.broadcasted_iota(jnp.int32, (CHUNK, LANES), 0)
    lo = lane < HEAD_DIM
    lo_row = lo[0:1, :]
    bd_ret = (sub >= HEAD_DIM) == ((lane & 32) != 0)
    bd_m = (sub >= HEAD_DIM) == (lane >= HEAD_DIM)
    gw = 4 * LANES

    @pl.when(i == 0)
    def _():
        pos = sub.astype(F32)
        for p in range(4):
            kdec_scr[p] = jnp.exp(lgk_ref[4 + p:5 + p, :] * pos)
        for ref in (p_rk, p_rv, p_mv, p_mqk, p_g, next_row_scr):
            ref[...] = jnp.zeros_like(ref)
        for grp in range(2 * gw // LANES):
            cols = slice(grp * LANES, (grp + 1) * LANES)
            wg_f = w_ref[:, cols]
            quarter = _lane((wg_f.shape[0], LANES)) // 32
            wg_f = jnp.where(quarter == 1, pltpu.roll(wg_f, LANES - 32, 1),
                             jnp.where(quarter == 2, pltpu.roll(wg_f, 32, 1), wg_f))
            w_scr[:, cols] = wg_f.astype(BF16)
        for j in range(2, 8):
            w_scr[:, j * gw:(j + 1) * gw] = w_ref[:, j * gw:(j + 1) * gw].astype(BF16)
        gate_w = jnp.where(_lane((wgate_ref.shape[0], LANES)) < 32, wgate_ref[...], 0.0)
        wg_scr[...] = gate_w.astype(BF16)
        wgt_scr[...] = gate_w.T[0:32, :].astype(BF16)

    @pl.when(jt_a == 0)
    def _():
        s_scr[...] = jnp.zeros_like(s_scr)
        c_scr[...] = jnp.zeros_like(c_scr)
        n_scr[...] = jnp.zeros_like(n_scr)
        m_scr[...] = jnp.zeros_like(m_scr)

    tile_i = jnp.minimum(i, tiles - 1)
    jt_i = order(tile_i)
    hb = _norm_mod(jnp.where(jt_i == 0, ctx_ref[...], x_ref[...]), ab_ref)
    cos, sin = cos_ref[...], sin_ref[...]

    def project(j):
        w_cols = w_scr[:, j * gw:(j + 1) * gw]
        acc = _dot(hb, w_cols)
        if j in (0, 1):
            if j == 0:
                acc = acc * (HEAD_DIM ** -0.5)
            acc = jnp.concatenate([_rope(acc[:, p * LANES:(p + 1) * LANES], cos, sin) for p in range(4)], axis=1)
        elif j == 3:
            acc = acc * jax.nn.sigmoid(acc)
        elif j == 7:
            acc = jax.nn.sigmoid(acc)
        return acc.astype(BF16)

    raw_q, raw_k = project(4), project(5)

    prev_on = jnp.where(jt_a <= 1, 0.0, 1.0).astype(F32)
    next_on = jnp.where((jt_a == 0) | (jt_a == tpb - 1), 0.0, 1.0).astype(F32)
    cur = p_mqk[...].astype(F32)
    n = cur.shape[0]
    row = lax.broadcasted_iota(jnp.int32, cur.shape, 0)
    prev_row = jnp.concatenate([raw_q[n - HALO:, :], raw_k[n - HALO:, :]], axis=1)[HALO - 1:HALO, :].astype(F32)
    xm = jnp.where(row == 0, prev_row * prev_on, pltpu.roll(cur, 1, 0))
    xp = jnp.where(row == n - 1, next_row_scr[0:1, :] * next_on, pltpu.roll(cur, n - 1, 0))
    conv = cw_ref[3:4, :] + cw_ref[0:1, :] * xm + cw_ref[1:2, :] * cur + cw_ref[2:3, :] * xp
    qk = conv * jax.nn.sigmoid(conv)
    qk_ref[:, 0:gw] = (qk[:, 0:gw] * (HEAD_DIM ** -0.5)).astype(BF16)
    qk_ref[:, gw:] = qk[:, gw:].astype(BF16)
    le_bf = (sub <= lane).astype(BF16)
    chunks = (1, 0)
    pre = {}
    for blk in chunks:
        rows = slice(blk * CHUNK, (blk + 1) * CHUNK)
        g = p_g[rows, :]
        bal = pltpu.roll(_cumsum_cols(le_bf, _log_sigmoid(g)), LANES - 8, 1)
        c_all = g - bal
        cmax = jnp.max(c_all, axis=0, keepdims=True)
        w_all = jnp.exp(c_all - cmax)
        vts, ks, nlocs = [], [], []
        for p in range(8):
            sl = slice((p % 4) * LANES, (p % 4 + 1) * LANES)
            if p < 4:
                v2 = p_rv[rows, sl]
                ks.append((p_rk[rows, sl].astype(F32) * kdec_scr[p]).astype(BF16))
            else:
                v2 = p_mv[rows, sl]
                h0 = 16 + 2 * (p - 4)
                kw = qk[rows, gw + (p - 4) * LANES:gw + (p - 3) * LANES] * jnp.where(
                    lo, w_all[:, h0:h0 + 1], w_all[:, h0 + 1:h0 + 2])
                ks.append(kw.astype(BF16))
                nlocs.append(jnp.sum(kw, axis=0, keepdims=True))
            vts.append(v2.astype(F32).T.astype(BF16))
        pre[blk] = (vts, ks, nlocs, cmax, bal[0:1, :])

    cur_rk = project(1)
    y_ref[:, 1 * gw:2 * gw] = cur_rk
    y_ref[:, 0:gw] = project(0)
    kvs = {blk: [_dot(pre[blk][0][p], pre[blk][1][p]) for p in range(8)] for blk in chunks}
    cur_rv = project(2)
    y_ref[:, 2 * gw:3 * gw] = cur_rv
    y_ref[:, 3 * gw:4 * gw] = project(3)

    for blk in chunks:
        _, _, nlocs, cmax, bend = pre[blk]
        sret_ref[blk] = s_scr[...].astype(BF16)
        cm_ref[blk] = c_scr[...].astype(BF16)
        nm_ref[blk] = n_scr[...].astype(BF16)
        mm_ref[blk] = m_scr[...]
        m_old = m_scr[0:1, :]
        mrel = jnp.maximum(m_old, cmax)
        a_row = jnp.exp(m_old - mrel)
        bb_row = jnp.exp(cmax - mrel)
        m_scr[0:1, :] = bend + mrel
        for p in range(4):
            cd = jnp.exp(lgk_ref[4 + p:5 + p, :] * float(CHUNK))
            s_scr[p] = s_scr[p] * cd + jnp.where(bd_ret, kvs[blk][p], 0.0)
            h0 = 16 + 2 * p
            a_l = jnp.where(lo_row, a_row[:, h0:h0 + 1], a_row[:, h0 + 1:h0 + 2])
            bb_l = jnp.where(lo_row, bb_row[:, h0:h0 + 1], bb_row[:, h0 + 1:h0 + 2])
            c_scr[p] = c_scr[p] * a_l + jnp.where(bd_m, kvs[blk][4 + p], 0.0) * bb_l
            n_new = (n_scr[p, 0:1, :] + n_scr[p, 1:2, :]) * a_l + nlocs[p] * bb_l
            n_scr[p, 0:1, :] = jnp.where(lo_row, n_new, 0.0)
            n_scr[p, 1:2, :] = jnp.where(lo_row, 0.0, n_new)

    cur_mv = project(6)
    y_ref[:, 4 * gw:5 * gw] = cur_mv
    y_ref[:, 5 * gw:6 * gw] = project(7)
    gates = _dot(hb, wg_scr[...]) + gb_ref[...]
    g_ref[...] = gates
    gt_ref[...] = _dot_nt(wgt_scr[...], hb) + gbt_ref[...]
    next_row_scr[...] = p_mqk[0:HALO, :].astype(F32)
    p_mqk[:, 0:gw] = raw_q
    p_mqk[:, gw:] = raw_k
    p_rk[...] = cur_rk
    p_rv[...] = cur_rv
    p_mv[...] = cur_mv
    p_g[...] = gates


def _inproj_sweep(ctx2, x2, ab, w_in, gb, gbt, cos, sin, lgk, cw, ffn_w, nb, tpb):
    d = x2.shape[1]
    tm = ROW_TILE
    cpt = tm // CHUNK
    gw = 4 * LANES
    tiles = nb * tpb
    r = tiles * tm
    lat = tpb - 1
    def visit(t):
        b, v = _div_small(t, tpb, nb)
        return b, jnp.where(v == 0, 0, tpb - v)

    cur = lambda i: visit(jnp.minimum(i, tiles - 1))
    flat = lambda bj: bj[0] * tpb + bj[1]
    tile_i = lambda i: flat(cur(i))
    tile_a = lambda i: flat(visit(jnp.maximum(i - 1, 0)))
    in_batch = lambda i: cur(i)[1]
    sel = lambda i: (cur(i)[0] * 2 + jnp.minimum(cur(i)[1], 1), 0, 0)
    state = lambda *dims: pl.BlockSpec((cpt,) + dims, lambda i: (tile_a(i),) + (0,) * len(dims))
    nchunks = tiles * cpt
    wc_in, wc_out, wc_shapes = _wcast_specs(*ffn_w, 0, tiles + 1)
    return pl.pallas_call(
        functools.partial(_inproj_sweep_kernel, tiles=tiles, tpb=tpb),
        grid=(tiles + 1,),
        in_specs=[pl.BlockSpec((tm, d), lambda i: (cur(i)[0], 0)),
                  pl.BlockSpec((tm, d), lambda i: (cur(i)[0] * lat + jnp.maximum(cur(i)[1] - 1, 0), 0)),
                  pl.BlockSpec((None, 2, d), sel),
                  pl.BlockSpec((None, d, 8 * gw), lambda i: (0, 0, 0), pipeline_mode=pl.Buffered(1)),
                  pl.BlockSpec((None, d, LANES), lambda i: (0, 0, 8 * gw // LANES), pipeline_mode=pl.Buffered(1)),
                  _const_spec(gb.shape), _const_spec(gbt.shape),
                  pl.BlockSpec((tm, LANES), lambda i: (in_batch(i), 0)),
                  pl.BlockSpec((tm, LANES), lambda i: (in_batch(i), 0)),
                  _const_spec(lgk.shape), _const_spec(cw.shape)] + wc_in,
        out_specs=[pl.BlockSpec((tm, 6 * gw), lambda i: (tile_i(i), 0)),
                   pl.BlockSpec((tm, LANES), lambda i: (tile_i(i), 0)),
                   pl.BlockSpec((32, tm), lambda i: (0, tile_i(i))),
                   state(4, CHUNK, LANES), state(4, CHUNK, LANES), state(4, HALO, LANES), state(8, LANES),
                   pl.BlockSpec((tm, 2 * gw), lambda i: (tile_a(i), 0))] + wc_out,
        out_shape=[jax.ShapeDtypeStruct((r, 6 * gw), BF16),
                   jax.ShapeDtypeStruct((r, LANES), F32),
                   jax.ShapeDtypeStruct((32, r), F32),
                   jax.ShapeDtypeStruct((nchunks, 4, CHUNK, LANES), BF16),
                   jax.ShapeDtypeStruct((nchunks, 4, CHUNK, LANES), BF16),
                   jax.ShapeDtypeStruct((nchunks, 4, HALO, LANES), BF16),
                   jax.ShapeDtypeStruct((nchunks, 8, LANES), F32),
                   jax.ShapeDtypeStruct((r, 2 * gw), BF16)] + wc_shapes,
        scratch_shapes=[pltpu.VMEM((4, CHUNK, LANES), F32), pltpu.VMEM((4, CHUNK, LANES), F32),
                        pltpu.VMEM((4, HALO, LANES), F32), pltpu.VMEM((8, LANES), F32),
                        pltpu.VMEM((4, CHUNK, LANES), F32),
                        pltpu.VMEM((tm, gw), BF16), pltpu.VMEM((tm, gw), BF16), pltpu.VMEM((tm, gw), BF16),
                        pltpu.VMEM((tm, 2 * gw), BF16), pltpu.VMEM((tm, LANES), F32),
                        pltpu.VMEM((HALO, 2 * gw), F32),
                        pltpu.VMEM((d, 8 * gw), BF16), pltpu.VMEM((d, LANES), BF16), pltpu.VMEM((32, d), BF16)],
        compiler_params=_cparams("arbitrary"),
        name="inproj_bwd_sweep",
    )(ctx2, x2, ab, w_in, w_in, gb, gbt, cos, sin, lgk, cw, *ffn_w)


class _Bag:
    def __init__(self, **kw):
        self.__dict__.update(kw)


def _mixer_chunk_stages(blk, r):
    rows = slice(blk * CHUNK, (blk + 1) * CHUNK)
    lane = _lane()
    sub = lax.broadcasted_iota(jnp.int32, (CHUNK, LANES), 0)
    lo = lane < HEAD_DIM
    sub_lo = sub < HEAD_DIM
    mask_ret = [(lane & 32) == 0, (lane & 32) != 0]
    mask_nat = [lo, lane >= HEAD_DIM]
    bd_ret = (sub >= HEAD_DIM) == ((lane & 32) != 0)
    bd_m = (sub >= HEAD_DIM) == (lane >= HEAD_DIM)
    le = sub <= lane
    ge = sub >= lane

    g = r.g_ref[rows, :]
    gt = r.gt_ref[:, rows]
    lf_col = _log_sigmoid(g)
    lf_row = _log_sigmoid(gt)
    le_bf = le.astype(BF16)
    ge_bf = ge.astype(BF16)
    pre_col = _cumsum_cols(ge_bf, lf_col)
    bal_f = pltpu.roll(pre_col, LANES - 8, 1)
    bal_b = pltpu.roll(pre_col[CHUNK - 1:CHUNK, :] - pre_col + lf_col, LANES - 8, 1)
    cf_all = g - bal_f
    cb_all = g - bal_b
    bf_row = _cumsum_rows(lf_row, le_bf)
    bb_row = bf_row[:, CHUNK - 1:CHUNK] - bf_row + lf_row
    qb, kb, vts, kf32, qt, vbd = [], [], [], [], [], []
    for p in range(8):
        sl = slice((p % 4) * LANES, (p % 4 + 1) * LANES)
        if p < 4:
            q2, k2, v2 = r.rq_ref[rows, sl], r.rk_ref[rows, sl], r.rv_ref[rows, sl]
            qf = q2.astype(F32)
            kf = None
        else:
            q2 = r.qk_ref[rows, sl]
            k2 = r.qk_ref[rows, 4 * LANES + (p - 4) * LANES:4 * LANES + (p - 3) * LANES]
            v2 = r.mv_ref[rows, sl]
            qf, kf = q2.astype(F32), k2.astype(F32)
        vt = v2.astype(F32).T.astype(BF16)
        qb.append(q2)
        kb.append(k2)
        vts.append(vt)
        kf32.append(kf)
        qt.append(qf.T)
        vbd.append([jnp.where(sub_lo, vt, jnp.zeros_like(vt)), jnp.where(sub_lo, jnp.zeros_like(vt), vt)])
    yield

    st2 = []
    for p in range(8):
        masks = mask_ret if p < 4 else mask_nat
        zero = jnp.zeros_like(qb[p])
        qstack = jnp.concatenate([jnp.where(masks[0], qb[p], zero), jnp.where(masks[1], qb[p], zero)], axis=0)
        st2.append(_dot_nt(kb[p], qstack))
    yield

    qn = [_dot_nt(jnp.concatenate([r.n_scr[p].astype(BF16), r.nmb_ref[blk, p]], axis=0), qb[4 + p])
          for p in range(4)]
    qn_f = [x[0:HALO] for x in qn]
    qn_b = [x[HALO:2 * HALO] for x in qn]
    cf2, cb2, bf2, bb2 = cf_all * LOG2E, cb_all * LOG2E, bf_row * LOG2E, bb_row * LOG2E
    m_f2, m_b2 = r.m_state[0:1, :] * LOG2E, r.mmb_ref[blk, 0:1, :] * LOG2E
    lhs, rhs = [], []
    for p in range(8):
        if p < 4:
            pts = [(st2[p][:, a * LANES:(a + 1) * LANES] * r.dm_scr[2 * p + a]).astype(BF16) for a in range(2)]
            x_f, x_b = r.dec_scr[0, p], r.dec_scr[1, p]
            old = [r.s_scr[p].astype(BF16), r.sretb_ref[blk, p]]
        else:
            pts, cf, cb = [], [], []
            for a in range(2):
                h = 2 * (p - 4) + a
                st = st2[p][:, a * LANES:(a + 1) * LANES]
                pf, coef_f = _mlstm_dir_weights(st, qn_f[p - 4][a:a + 1, :], cf2[:, h:h + 1],
                                                bf2[8 + h:9 + h, :], m_f2[0:1, h:h + 1], le)
                pb, coef_b = _mlstm_dir_weights(st, qn_b[p - 4][a:a + 1, :], cb2[:, 16 + h:17 + h],
                                                bb2[24 + h:25 + h, :], m_b2[0:1, 16 + h:17 + h], ge)
                pts.append((pf + pb).astype(BF16))
                cf.append(coef_f)
                cb.append(coef_b)
            x_f = jnp.where(sub_lo, cf[0], cf[1])
            x_b = jnp.where(sub_lo, cb[0], cb[1])
            old = [r.c_scr[p - 4].astype(BF16), r.cmb_ref[blk, p - 4]]
        lhs.append(jnp.concatenate(vbd[p] + old, axis=1))
        rhs.append(jnp.concatenate(pts + [(qt[p] * x_f).astype(BF16), (qt[p] * x_b).astype(BF16)], axis=0))
    yield

    ht = [_dot(lhs[p], rhs[p]) for p in range(8)]
    yield

    for p in range(8):
        sl = slice((p % 4) * LANES, (p % 4 + 1) * LANES)
        gate_ref = r.rg_ref if p < 4 else r.mo_ref
        y = _heads_out(ht[p], r.nw_ref[p])
        new = (y * gate_ref[rows, sl].astype(F32)).astype(BF16)
        r.mix_scr[rows, p * LANES:(p + 1) * LANES] = jnp.where(r.live, new, r.mix_scr[rows, p * LANES:(p + 1) * LANES])
    yield

    for p in range(4):
        cd = jnp.exp(r.lgk_ref[p:p + 1, :] * float(CHUNK))
        _ret_state_update(r.s_scr, p, kb[p], vts[p], r.dec_scr[2, p], cd, bd_ret)
    _mlstm_state_update(r.c_scr, r.n_scr, r.m_state, kf32[4:], vts[4:], cf_all, bal_f[CHUNK - 1:CHUNK, :],
                        0, lo, bd_m)
    yield


def _attn_inproj_stages(x, ab_ref, wq_ref, wk_ref, wv_ref, nw_ref, cos, sin, y_ref):
    hb = _norm_mod(x, ab_ref)
    r2 = lax.broadcasted_iota(jnp.int32, (2 * LANES, 2 * LANES), 0)
    c2 = lax.broadcasted_iota(jnp.int32, (2 * LANES, 2 * LANES), 1)
    same_head = (((r2 ^ c2) & (LANES | 32)) == 0).astype(BF16)
    acc_q = _dot(hb, wq_ref[...])
    acc_k = _dot(hb, wk_ref[...])
    y_ref[:, 10 * LANES:12 * LANES] = _dot(hb, wv_ref[...]).astype(BF16)
    yield
    for j in range(5):
        acc = acc_q[:, j * 2 * LANES:(j + 1) * 2 * LANES] if j < 4 else acc_k
        sq = acc * acc
        ms = _dot(sq.astype(BF16), same_head) * (1.0 / HEAD_DIM)
        nrm = acc * lax.rsqrt(ms + EPS)
        nw = nw_ref[0:1, :] if j < 4 else nw_ref[1:2, :]
        for v in range(2):
            ls = slice(v * LANES, (v + 1) * LANES)
            y_ref[:, (2 * j + v) * LANES:(2 * j + v + 1) * LANES] = _rope(nrm[:, ls] * nw, cos, sin).astype(BF16)
        if j in (1, 4):
            yield


def _ffn_splits(d_ff, pieces):
    blocks = d_ff // (2 * LANES)
    assert blocks * 2 * LANES == d_ff and blocks >= pieces
    cuts = [((i * blocks) // pieces) * 2 * LANES for i in range(pieces)]
    return cuts + [d_ff]


def _mixer_ffn_kernel(lg_ref, y_ref, qk_ref,
                      g_ref, gt_ref, sretb_ref, cmb_ref, nmb_ref, mmb_ref,
                      lgk_ref, lgkt_ref, nw_ref,
                      ctx_ref, x_ref, p_ref, wo_ref, wi_ref, w2_ref,
                      ab2_ref, waq_ref, wak_ref, wav_ref, nwa_ref, cos_ref, sin_ref, wi_src, w2_src,
                      o_ref, y2_ref, wi_dst, w2_dst,
                      s_scr, c_scr, n_scr, m_state, dm_scr, dec_scr, mix_scr, act_scr, x2_scr,
                      *, tiles, tpb, d_ff):
    s = pl.program_id(0)
    _wcast((wi_src, w2_src), (wi_dst, w2_dst))
    nb = tiles // tpb
    jt = _div_small(jnp.minimum(s, tiles - 1), tpb, nb)[1]
    gw = 4 * LANES
    rq_ref, rk_ref, rv_ref, rg_ref, mv_ref, mo_ref = [y_ref.at[:, j * gw:(j + 1) * gw] for j in range(6)]
    lane = _lane()
    sub = lax.broadcasted_iota(jnp.int32, (CHUNK, LANES), 0)

    @pl.when(s == 0)
    def _():
        mix_scr[...] = jnp.zeros_like(mix_scr)
        x2_scr[...] = jnp.zeros_like(x2_scr)
        le = sub <= lane
        ge = sub >= lane
        spos = sub.astype(F32)
        tpos = lane.astype(F32)
        diff = (lane - sub).astype(F32)
        for h in range(8):
            dm_scr[h] = (jnp.where(le, jnp.exp(lg_ref[h] * diff), 0.0)
                         + jnp.where(ge, jnp.exp(lg_ref[8 + h] * (-diff)), 0.0))
        for p in range(4):
            dec_scr[0, p] = jnp.exp(lgkt_ref[:, p:p + 1] * (tpos + 1.0))
            dec_scr[1, p] = jnp.exp(lgkt_ref[:, 4 + p:5 + p] * (float(CHUNK) - tpos))
            dec_scr[2, p] = jnp.exp(lgk_ref[p:p + 1, :] * (float(CHUNK) - 1.0 - spos))

    @pl.when(jt == 0)
    def _():
        s_scr[...] = jnp.zeros_like(s_scr)
        c_scr[...] = jnp.zeros_like(c_scr)
        n_scr[...] = jnp.zeros_like(n_scr)
        m_state[...] = jnp.zeros_like(m_state)

    r = _Bag(rq_ref=rq_ref, rk_ref=rk_ref, rv_ref=rv_ref, rg_ref=rg_ref, qk_ref=qk_ref, mv_ref=mv_ref, mo_ref=mo_ref,
             g_ref=g_ref, gt_ref=gt_ref, sretb_ref=sretb_ref, cmb_ref=cmb_ref, nmb_ref=nmb_ref, mmb_ref=mmb_ref,
             lgk_ref=lgk_ref, nw_ref=nw_ref, s_scr=s_scr, c_scr=c_scr, n_scr=n_scr, m_state=m_state,
             dm_scr=dm_scr, dec_scr=dec_scr, mix_scr=mix_scr, live=s < tiles)
    cuts = _ffn_splits(d_ff, 3)
    ffn_piece = lambda i: _ffn_cols(h, wi_ref, act_scr, cuts[i], cuts[i + 1], d_ff)

    x = _ctx_or_latent_rows(ctx_ref, x_ref, jnp.clip(s - 1, 0, tiles - 1), nb, tpb)
    x1, h = _ffn_in(x, mix_scr[...], p_ref, wo_ref)

    nxt = _attn_inproj_stages(x2_scr[...], ab2_ref, waq_ref, wak_ref, wav_ref, nwa_ref, cos_ref[...], sin_ref[...],
                              y2_ref)
    chunk_a, chunk_b = _mixer_chunk_stages(0, r), _mixer_chunk_stages(1, r)
    next(nxt)
    next(chunk_a), next(chunk_b)
    next(chunk_a), next(chunk_b)
    ffn_piece(0)
    next(chunk_a), next(chunk_a)
    ffn_piece(1)
    next(chunk_a), next(chunk_a)
    next(nxt)
    ffn_piece(2)
    next(chunk_b), next(chunk_b)
    next(nxt)
    x2 = x1 + p_ref[3:4, :] * _dot(act_scr[...], w2_ref[...])
    o_ref[...] = x2
    next(chunk_b), next(chunk_b)
    x2_scr[...] = x2


def _mixer_ffn(y, g, gt, states, lg_smem, lgk, lgkt, nw, xs, prm, wo, wi, w2,
               ab_next, w_next, nw_next, cos, sin, ffn_w, nb, tpb):
    gw = 4 * LANES
    tm = ROW_TILE
    cpt = tm // CHUNK
    sretb, cmb, nmb, mmb, qk_act = states
    tiles = nb * tpb
    d = xs[-1].shape[1]
    d_ff = w2.shape[0]
    n_next = sum(a.shape[1] for a in w_next)
    mix = lambda s: jnp.minimum(s, tiles - 1)
    ffn = lambda s: jnp.clip(s - 1, 0, tiles - 1)
    nxt = lambda s: jnp.maximum(s - 2, 0)
    state_spec = lambda a: pl.BlockSpec((cpt,) + a.shape[1:], lambda s: (mix(s),) + (0,) * (a.ndim - 1))
    def sel(t):
        b, j = _div_small(t, tpb, nb)
        return b * 2 + jnp.minimum(j, 1), 0, 0

    wc_in, wc_out, wc_shapes = _wcast_specs(*ffn_w, 1, tiles + 2)
    return pl.pallas_call(
        functools.partial(_mixer_ffn_kernel, tiles=tiles, tpb=tpb, d_ff=d_ff),
        grid=(tiles + 2,),
        in_specs=[_smem_spec(),
                  pl.BlockSpec((tm, 6 * gw), lambda s: (mix(s), 0)),
                  pl.BlockSpec((tm, 2 * gw), lambda s: (mix(s), 0)),
                  pl.BlockSpec((tm, LANES), lambda s: (mix(s), 0)),
                  pl.BlockSpec((32, tm), lambda s: (0, mix(s))),
                  state_spec(sretb), state_spec(cmb), state_spec(nmb), state_spec(mmb),
                  _const_spec(lgk.shape), _const_spec(lgkt.shape), _const_spec(nw.shape)]
                 + _split_row_specs(tm, d, nb, tpb, ffn)
                 + [pl.BlockSpec((None, 8, d), lambda s: sel(ffn(s))),
                    _const_spec(wo.shape), _const_spec(wi.shape), _const_spec(w2.shape),
                    pl.BlockSpec((None, 2, d), lambda s: sel(nxt(s))),
                    *[_const_spec(a.shape) for a in w_next], _const_spec(nw_next.shape),
                    pl.BlockSpec((tm, LANES), lambda s: (_div_small(nxt(s), tpb, nb)[1], 0)),
                    pl.BlockSpec((tm, LANES), lambda s: (_div_small(nxt(s), tpb, nb)[1], 0))] + wc_in,
        out_specs=[pl.BlockSpec((tm, d), lambda s: (ffn(s), 0)),
                   pl.BlockSpec((tm, n_next), lambda s: (nxt(s), 0))] + wc_out,
        out_shape=[jax.ShapeDtypeStruct((tiles * tm, d), F32),
                   jax.ShapeDtypeStruct((tiles * tm, n_next), BF16)] + wc_shapes,
        scratch_shapes=[pltpu.VMEM((4, CHUNK, LANES), F32), pltpu.VMEM((4, CHUNK, LANES), F32),
                        pltpu.VMEM((4, HALO, LANES), F32), pltpu.VMEM((8, LANES), F32),
                        pltpu.VMEM((8, CHUNK, LANES), F32), pltpu.VMEM((3, 4, CHUNK, LANES), F32),
                        pltpu.VMEM((tm, 2 * gw), BF16), pltpu.VMEM((tm, d_ff), BF16),
                        pltpu.VMEM((tm, d), F32)],
        compiler_params=_cparams("arbitrary"),
        name="ret_mlstm_mixer_ffn",
    )(lg_smem, y, qk_act, g, gt, sretb, cmb, nmb, mmb, lgk, lgkt, nw,
      *xs, prm, wo, wi, w2, ab_next, *w_next, nw_next, cos, sin, *ffn_w)


def _ffn_in(x, m, p_ref, wo_ref):
    x1 = x + p_ref[0:1, :] * _dot(m, wo_ref[...])
    ms = jnp.mean(x1 * x1, axis=-1, keepdims=True)
    return x1, ((x1 * lax.rsqrt(ms + EPS)) * p_ref[1:2, :] + p_ref[2:3, :]).astype(BF16)


def _ffn_cols(h, wi_ref, act_scr, lo, hi, d_ff):
    gate = _dot(h, wi_ref[:, lo:hi])
    up = _dot(h, wi_ref[:, d_ff + lo:d_ff + hi])
    act_scr[:, lo:hi] = (gate * jax.nn.sigmoid(gate) * up).astype(BF16)


def _window_bias():
    kk = np.arange(CHUNK)[:, None]
    t = np.arange(CHUNK)[None, :]
    tabs = []
    for has_prev, has_next in ((False, True), (True, True), (True, False)):
        prev_ok = (kk >= t) & has_prev
        next_ok = (kk <= t) & has_next
        tabs.append(np.where(np.concatenate([prev_ok, next_ok], axis=0), 0.0, NEG))
    return jnp.asarray(np.stack(tabs), F32)


def _attn_ffn_kernel(sink_ref, qkv_a_ref, qkv_b_ref, kvp_ref, kvn_ref, kvx_ref, bias_ref,
                     x_a_ref, x_b_ref, p_ref, wo_ref, wi_ref, w2_ref, o_ref, m_scr, act_scr,
                     *, steps, per_batch, d_ff):
    s = pl.program_id(0)
    kw = 2 * LANES
    k0 = 8 * LANES
    tm = qkv_a_ref.shape[0]
    nblk = 2 * tm // CHUNK

    @pl.when(s == 0)
    def _():
        m_scr[...] = jnp.zeros_like(m_scr)

    j = _div_small(jnp.minimum(s, steps - 1), per_batch, steps // per_batch)[1]
    grp = lax.broadcasted_iota(jnp.int32, (1, 4 * CHUNK), 1) // CHUNK
    lane = _lane()
    mask_q = [(lane & 32) == 0, (lane & 32) != 0]
    tile4 = lambda b: jnp.concatenate([b] * 4, axis=1)
    inner = tile4(bias_ref[1])
    biases = ([tile4(jnp.where(j == 0, bias_ref[0], bias_ref[1]))] + [inner] * (nblk - 2)
              + [tile4(jnp.where(j == per_batch - 1, bias_ref[2], bias_ref[1]))])
    cuts = _ffn_splits(d_ff, nblk - 1)

    def ffn_up(i):
        _ffn_cols(h, wi_ref, act_scr, cuts[i], cuts[i + 1], d_ff)

    def q_rows(blk):
        ref = qkv_a_ref if blk * CHUNK < tm else qkv_b_ref
        r0 = (blk * CHUNK) % tm
        return ref, slice(r0, r0 + CHUNK)

    def scores(blk, kv):
        kvp, a = kv // 2, kv % 2
        ref, rows = q_rows(blk)
        qs = jnp.concatenate(
            [jnp.where(mask_q[a], ref[rows, (kvp * 4 + g) * LANES:(kvp * 4 + g + 1) * LANES],
                       jnp.zeros((CHUNK, LANES), BF16)) for g in range(4)], axis=0)
        return _dot_nt(kcats[kvp][blk], qs)

    def softmax_pv(blk, kv, st):
        bias = biases[blk]
        st = jnp.concatenate([st[0:CHUNK] + bias[0:CHUNK], st[CHUNK:2 * CHUNK],
                              st[2 * CHUNK:3 * CHUNK] + bias[CHUNK:2 * CHUNK], st[3 * CHUNK:]], axis=0)
        snk = jnp.where(grp == 0, sink_ref[kv * 4],
                        jnp.where(grp == 1, sink_ref[kv * 4 + 1],
                                  jnp.where(grp == 2, sink_ref[kv * 4 + 2], sink_ref[kv * 4 + 3])))
        m = jnp.maximum(jnp.max(st, axis=0, keepdims=True), snk)
        e = jnp.exp2(st - m)
        denom = jnp.exp2(snk - m) + jnp.sum(e, axis=0, keepdims=True)
        a = kv % 2
        return _dot(vts[kv // 2][blk][a * HEAD_DIM:(a + 1) * HEAD_DIM, :], e.astype(BF16)) * (1.0 / denom)

    def hand_over(blk, outs):
        for kvp in range(2):
            full = jnp.concatenate(outs[2 * kvp:2 * kvp + 2], axis=0)
            for g in range(4):
                m_scr[blk * CHUNK:(blk + 1) * CHUNK, (kvp * 4 + g) * LANES:(kvp * 4 + g + 1) * LANES] = (
                    full[:, g * CHUNK:(g + 1) * CHUNK].T.astype(BF16))

    x1, h = _ffn_in(jnp.concatenate([x_a_ref[...], x_b_ref[...]], axis=0), m_scr[...], p_ref, wo_ref)

    kcats, vts = [], []
    for kvp in range(2):
        ks = slice(k0 + kvp * LANES, k0 + (kvp + 1) * LANES)
        vs = slice(k0 + kw + kvp * LANES, k0 + kw + (kvp + 1) * LANES)
        es, ev = slice(kvp * LANES, (kvp + 1) * LANES), slice(kw + kvp * LANES, kw + (kvp + 1) * LANES)
        tile_rows = [(ref, slice(c * CHUNK, (c + 1) * CHUNK)) for ref in (qkv_a_ref, qkv_b_ref)
                     for c in range(tm // CHUNK)]
        k_chunks = [kvp_ref[:, es]] + [ref[rows, ks] for ref, rows in tile_rows] + [kvn_ref[:, es]]
        v_chunks = ([kvp_ref[:, ev]] + [ref[rows, vs] for ref, rows in tile_rows] + [kvn_ref[:, ev]]
                    + [kvx_ref[c * CHUNK:(c + 1) * CHUNK, ev] for c in range(kvx_ref.shape[0] // CHUNK)])
        v_t = [v.astype(F32).T.astype(BF16) for v in v_chunks]
        kcats.append([jnp.concatenate(k_chunks[b:b + 3] + [kvx_ref[:, es]], axis=0) for b in range(nblk)])
        vts.append([jnp.concatenate(v_t[b:b + 3] + v_t[nblk + 2:], axis=1) for b in range(nblk)])

    for blk in range(nblk):
        sts = [scores(blk, kv) for kv in range(4)]
        if blk < nblk - 1:
            ffn_up(blk)
        else:
            o_ref[...] = x1 + p_ref[3:4, :] * _dot(act_scr[...], w2_ref[...])
        hand_over(blk, [softmax_pv(blk, kv, sts[kv]) for kv in range(4)])


def _attn_ffn(y, sink, xc, prm, wo, wi, w2, nb, nc, ctx_chunks):
    tm = ROW_TILE
    cpt = tm // CHUNK
    tpb = nc // cpt
    lat_tiles = (nc - ctx_chunks) // cpt
    ctx_tiles = ctx_chunks // cpt
    assert ctx_tiles == 1 and lat_tiles % 2 == 0 and lat_tiles >= 4
    per_batch = lat_tiles // 2
    steps = nb * per_batch
    d = xc.shape[1]
    d_ff = w2.shape[0]
    bias = _window_bias()
    n_qkv = y.shape[1]
    kv_w = n_qkv - d
    assert d % kv_w == 0
    kv_col = d // kv_w
    att = lambda s: jnp.minimum(s, steps - 1)
    ffn = lambda s: jnp.maximum(s - 1, 0)
    split = lambda t: _div_small(t, per_batch, nb)
    row_tile = lambda t, u: split(t)[0] * tpb + ctx_tiles + 2 * split(t)[1] + u
    chunk0 = lambda t: split(t)[0] * nc + ctx_chunks
    prev_c = lambda s: chunk0(att(s)) + jnp.maximum(split(att(s))[1] * 2 * cpt - 1, 0)
    next_c = lambda s: chunk0(att(s)) + jnp.minimum(split(att(s))[1] * 2 * cpt + 2 * cpt, lat_tiles * cpt - 1)
    edge_spec = lambda f: pl.BlockSpec((CHUNK, kv_w), lambda s: (f(s), kv_col))
    return pl.pallas_call(
        functools.partial(_attn_ffn_kernel, steps=steps, per_batch=per_batch, d_ff=d_ff),
        grid=(steps + 1,),
        in_specs=[_smem_spec(),
                  pl.BlockSpec((tm, n_qkv), lambda s: (row_tile(att(s), 0), 0)),
                  pl.BlockSpec((tm, n_qkv), lambda s: (row_tile(att(s), 1), 0)),
                  edge_spec(prev_c), edge_spec(next_c),
                  pl.BlockSpec((tm, kv_w), lambda s: (split(att(s))[0] * tpb, kv_col)),
                  _const_spec(bias.shape),
                  pl.BlockSpec((tm, d), lambda s: (row_tile(ffn(s), 0), 0)),
                  pl.BlockSpec((tm, d), lambda s: (row_tile(ffn(s), 1), 0)),
                  pl.BlockSpec((None, 8, d), lambda s: (split(ffn(s))[0] * 2 + 1, 0, 0)),
                  _const_spec(wo.shape), _const_spec(wi.shape), _const_spec(w2.shape)],
        out_specs=pl.BlockSpec((2 * tm, d), lambda s: (ffn(s), 0)),
        out_shape=jax.ShapeDtypeStruct((steps * 2 * tm, d), F32),
        scratch_shapes=[pltpu.VMEM((2 * tm, d), BF16), pltpu.VMEM((2 * tm, d_ff), BF16)],
        compiler_params=_cparams("arbitrary"),
        name="window_gqa_ffn",
    )(sink, y, y, y, y, y, bias, xc, xc, prm, wo, wi, w2)


def _pair_cols(w):
    rows, cols = w.shape
    return w.reshape(rows, cols // LANES, 2, 2, 32).transpose(0, 1, 3, 2, 4).reshape(rows, cols)


def _attn_q_cols(w):
    rows = w.shape[0]
    g_per = w.shape[1] // (H_KV * HEAD_DIM)
    return (w.reshape(rows, H_KV // 2, 2, g_per, 2, 32).transpose(0, 1, 3, 4, 2, 5)
            .reshape(rows, w.shape[1]))


def _attn_o_rows(w):
    cols = w.shape[1]
    g_per = w.shape[0] // (H_KV * HEAD_DIM)
    return (w.reshape(H_KV // 2, 2, g_per, HEAD_DIM, cols).transpose(0, 2, 1, 3, 4)
            .reshape(w.shape[0], cols))


def _rope_tables(seq, ctx_len):
    rows = seq // GRID_W
    row = np.repeat(np.arange(rows, dtype=np.float32), GRID_W)
    col = np.tile(np.arange(GRID_W, dtype=np.float32), rows)
    n = HEAD_DIM // 4
    inv = (np.float32(ROPE_BASE) ** (-np.arange(n, dtype=np.float32) / np.float32(n))).astype(np.float32)
    ang = np.concatenate([row[:, None] * inv, col[:, None] * inv], axis=-1).astype(np.float32)
    cos, sin = np.cos(ang), np.sin(ang)
    cos_t = np.concatenate([np.ones((ctx_len, LANES), np.float32), np.tile(cos, (1, 4))], axis=0)
    sin_t = np.concatenate([np.zeros((ctx_len, LANES), np.float32),
                            np.concatenate([-sin, -sin, sin, sin], axis=-1)], axis=0)
    return jnp.asarray(cos_t, F32), jnp.asarray(sin_t, F32)


def _mod_tables(mod, nb, norm_w):
    d = norm_w.shape[-1]
    lat = mod[:nb].reshape(nb, 6, d)
    ctx = jnp.broadcast_to(mod[nb].reshape(1, 6, d), (nb, 6, d))
    both = jnp.stack([ctx, lat], axis=1).reshape(nb * 2, 6, d)
    sh1, sc1, g1, sh2, sc2, g2 = [both[:, k] for k in range(6)]
    ab1 = jnp.stack([norm_w[0] * (1.0 + sc1), sh1], axis=1)
    zeros = jnp.zeros_like(g1)
    prm = jnp.stack([g1, norm_w[1] * (1.0 + sc2), sh2, g2, zeros, zeros, zeros, zeros], axis=1)
    return ab1, prm


def kernel(x, c, ctx, c_ctx, ada_w, ada_b, norm_w, ffn_w_in, ffn_w_out, ab_w_in, ab_w_out,
           ret_log_gamma, ret_norm_w, mlstm_conv_w, mlstm_conv_b, mlstm_gate_b, mlstm_norm_w,
           attn_w_in, attn_w_out, attn_q_norm_w, attn_k_norm_w, attn_sink):
    nb, seq, d = x.shape
    ctx_len = ctx.shape[1]
    depth = ada_w.shape[0]
    assert ctx_len == ROW_TILE and seq % ROW_TILE == 0 and d == 8 * LANES and nb < 8
    t_all = ctx_len + seq
    nc = t_all // CHUNK
    ctx_chunks = ctx_len // CHUNK
    tpb = t_all // ROW_TILE
    dr = d // 2

    rows = jnp.zeros((8, d), F32).at[:nb].set(c).at[nb].set(c_ctx)
    mod_all = _modulation(rows, ada_w, ada_b)
    cos_t, sin_t = _rope_tables(seq, ctx_len)
    ffn_w = (ffn_w_in, ffn_w_out)
    xs = (ctx.reshape(nb * ctx_len, d), x.reshape(nb * seq, d))

    assert depth == 2 and ab_w_in.shape[0] == 1 and attn_w_in.shape[0] == 1
    ab_0, prm_0 = _mod_tables(mod_all[0], nb, norm_w[0])
    ab_1, prm_1 = _mod_tables(mod_all[1], nb, norm_w[1])

    assert ab_w_in.shape[2] == 8 * dr + 32
    gb = jnp.zeros((1, LANES), F32).at[0, :32].set(mlstm_gate_b[0].reshape(-1))
    gbt = mlstm_gate_b[0].reshape(32, 1)
    lg = ret_log_gamma[0].astype(F32)
    lgk = jnp.tile(jnp.repeat(lg.reshape(2, 4, 2), 32, axis=-1), (1, 1, 2)).reshape(8, LANES)
    cw = jnp.concatenate([mlstm_conv_w[0], mlstm_conv_b[0][None], jnp.zeros((4, 2 * dr), F32)], axis=0)
    nw = jnp.broadcast_to(jnp.concatenate([ret_norm_w[0], mlstm_norm_w[0]]).reshape(8, LANES, 1),
                          (8, LANES, LANES))
    wo_0 = ab_w_out[0].astype(BF16)

    w = attn_w_in[0]
    w_attn = (_attn_q_cols(w[:, :d]).astype(BF16), _pair_cols(w[:, d:d + 2 * LANES]).astype(BF16),
              w[:, d + 2 * LANES:].astype(BF16))
    lane_w = lambda v: jnp.concatenate([v[:32], v[:32], v[32:], v[32:]])
    nwq = jnp.stack([lane_w(attn_q_norm_w[0]) * (HEAD_DIM ** -0.5 * LOG2E), lane_w(attn_k_norm_w[0])]
                    + [jnp.zeros((LANES,), F32)] * 6)
    wo_1 = _attn_o_rows(attn_w_out[0]).astype(BF16)

    y, g, gt, *states, wi_0, w2_0 = _inproj_sweep(*xs, ab_0, ab_w_in, gb, gbt, cos_t, sin_t, lgk, cw,
                                                  ffn_w, nb, tpb)
    x_mid, y_attn, wi_1, w2_1 = _mixer_ffn(y, g, gt, states, lg.reshape(-1), lgk, lgk.T, nw, xs, prm_0, wo_0,
                                           wi_0, w2_0, ab_1, w_attn, nwq, cos_t, sin_t, ffn_w, nb, tpb)
    out = _attn_ffn(y_attn, attn_sink[0].astype(F32) * LOG2E, x_mid, prm_1, wo_1, wi_1, w2_1, nb, nc, ctx_chunks)
    return out.reshape(nb, seq, d)
```

```python
import functools

import numpy as np
import jax
import jax.numpy as jnp
from jax import lax
from jax.experimental import pallas as pl
from jax.experimental.pallas import tpu as pltpu

F32 = jnp.float32
BF16 = jnp.bfloat16

HEAD_DIM = 64
CHUNK = 128
GRID_W = 64
ROPE_BASE = 10000.0
EPS = 1e-6
H_KV = 4
LANES = 128
ROW_TILE = 256
HALO = 16
LOG2E = 1.4426950408889634
NEG = -1e30
VMEM_LIMIT = 56 * 1024 * 1024


def _cparams(*sem):
    return pltpu.CompilerParams(dimension_semantics=sem, vmem_limit_bytes=VMEM_LIMIT)


def _const_spec(shape):
    nd = len(shape)
    return pl.BlockSpec(shape, lambda *_: (0,) * nd, pipeline_mode=pl.Buffered(1))


def _smem_spec():
    return pl.BlockSpec(memory_space=pltpu.SMEM)


def _wcast_specs(w_in_all, w_out_all, layer, steps):
    specs_in, specs_out, shapes = [], [], []
    for w in (w_in_all, w_out_all):
        total, cols = w.shape[1:]
        rows = next(r for r in range(HALO, total + 1, HALO) if total % r == 0 and total // r <= steps)
        n_blocks = total // rows
        specs_in.append(pl.BlockSpec((None, rows, cols),
                                     lambda i, n=n_blocks: (layer, jnp.minimum(i, n - 1), 0)))
        specs_out.append(pl.BlockSpec((rows, cols), lambda i, n=n_blocks: (jnp.minimum(i, n - 1), 0)))
        shapes.append(jax.ShapeDtypeStruct(w.shape[1:], BF16))
    return specs_in, specs_out, shapes


def _wcast(src_refs, dst_refs):
    for src, dst in zip(src_refs, dst_refs):
        dst[...] = src[...].astype(BF16)


def _lane(shape=(CHUNK, LANES)):
    return lax.broadcasted_iota(jnp.int32, shape, len(shape) - 1)


def _dot(a, b):
    return jnp.dot(a, b, preferred_element_type=F32)


def _dot_nt(a, b):
    return lax.dot_general(a, b, (((1,), (1,)), ((), ())), preferred_element_type=F32)


def _div_small(t, m, n):
    q = 0
    for b in range(1, n):
        q = q + jnp.where(t >= b * m, 1, 0)
    return q, t - q * m


def _split3(x):
    hi = x.astype(BF16)
    r = x - hi.astype(F32)
    mid = r.astype(BF16)
    lo = (r - mid.astype(F32)).astype(BF16)
    return hi, mid, lo


def _log_sigmoid(x):
    return jnp.minimum(x, 0.0) - jnp.log1p(jnp.exp(-jnp.abs(x)))


def _rope(x, cos, sin_signed):
    return x * cos + pltpu.roll(x, LANES // 2, 1) * sin_signed


def _mod_kernel(rows_ref, w_ref, b_ref, o_ref):
    a = rows_ref[...]
    a = a * jax.nn.sigmoid(a)
    a_hi = a.astype(BF16)
    a_lo = (a - a_hi.astype(F32)).astype(BF16)
    w = w_ref[...]
    w_hi = w.astype(BF16)
    w_lo = (w - w_hi.astype(F32)).astype(BF16)
    o_ref[...] = _dot(a_hi, w_hi) + _dot(a_hi, w_lo) + _dot(a_lo, w_hi) + b_ref[...]


def _modulation(rows, ada_w, ada_b):
    depth, d, n = ada_w.shape
    tn = n // 4
    return pl.pallas_call(
        _mod_kernel,
        grid=(depth, n // tn),
        in_specs=[pl.BlockSpec((8, d), lambda l, j: (0, 0)),
                  pl.BlockSpec((None, d, tn), lambda l, j: (l, 0, j)),
                  pl.BlockSpec((None, 1, tn), lambda l, j: (l, 0, j))],
        out_specs=pl.BlockSpec((None, 8, tn), lambda l, j: (l, 0, j)),
        out_shape=jax.ShapeDtypeStruct((depth, 8, n), F32),
        compiler_params=_cparams("arbitrary", "arbitrary"),
        name="adaln_modulation",
    )(rows, ada_w, ada_b.reshape(depth, 1, n))


def _norm_mod(x, ab_ref):
    ms = jnp.mean(x * x, axis=-1, keepdims=True)
    h = (x * lax.rsqrt(ms + EPS)) * ab_ref[0:1, :] + ab_ref[1:2, :]
    return h.astype(BF16)


def _ctx_or_latent_rows(ctx_ref, x_ref, tile, nb, tiles_per_batch):
    return jnp.where(_div_small(tile, tiles_per_batch, nb)[1] == 0, ctx_ref[...], x_ref[...])


def _split_row_specs(tm, d, nb, tpb, tile_of):
    lat = tpb - 1

    def latent_row(i):
        b, j = _div_small(tile_of(i), tpb, nb)
        return b * lat + jnp.maximum(j - 1, 0), 0

    return [pl.BlockSpec((tm, d), lambda i: (_div_small(tile_of(i), tpb, nb)[0], 0)),
            pl.BlockSpec((tm, d), latent_row)]


def _cumsum_cols(tri_bf, lf):
    hi, mid, lo = _split3(lf)
    return _dot(tri_bf, hi) + _dot(tri_bf, mid) + _dot(tri_bf, lo)


def _cumsum_rows(lf, tri_bf):
    hi, mid, lo = _split3(lf)
    return _dot(hi, tri_bf) + _dot(mid, tri_bf) + _dot(lo, tri_bf)


def _ret_state_update(s_ref, p, k2, vt, kdec, cd_lanes, bd):
    kf = (k2.astype(F32) * kdec).astype(BF16)
    s_ref[p] = s_ref[p] * cd_lanes + jnp.where(bd, _dot(vt, kf), 0.0)


def _mlstm_state_update(c_ref, n_ref, m_ref, k_pairs, vt_pairs, c_all, bend, col0, lo, bd):
    cmax = jnp.max(c_all, axis=0, keepdims=True)
    w_all = jnp.exp(c_all - cmax)
    m_old = m_ref[0:1, :]
    mrel = jnp.maximum(m_old, cmax)
    a_row = jnp.exp(m_old - mrel)
    bb_row = jnp.exp(cmax - mrel)
    m_ref[0:1, :] = bend + mrel
    lo_row = lo[0:1, :]
    for p in range(4):
        h0 = col0 + 2 * p
        kw = k_pairs[p] * jnp.where(lo, w_all[:, h0:h0 + 1], w_all[:, h0 + 1:h0 + 2])
        kvt = _dot(vt_pairs[p], kw.astype(BF16))
        nloc = jnp.sum(kw, axis=0, keepdims=True)
        a_l = jnp.where(lo_row, a_row[:, h0:h0 + 1], a_row[:, h0 + 1:h0 + 2])
        bb_l = jnp.where(lo_row, bb_row[:, h0:h0 + 1], bb_row[:, h0 + 1:h0 + 2])
        c_ref[p] = c_ref[p] * a_l + jnp.where(bd, kvt, 0.0) * bb_l
        n_new = (n_ref[p, 0:1, :] + n_ref[p, 1:2, :]) * a_l + nloc * bb_l
        n_ref[p, 0:1, :] = jnp.where(lo_row, n_new, 0.0)
        n_ref[p, 1:2, :] = jnp.where(lo_row, 0.0, n_new)


def _mlstm_dir_weights(st, qn_row, c_col, bt_row, m_prev, tri):
    dl = jnp.where(tri, c_col + bt_row, NEG)
    mx = jnp.max(dl, axis=0, keepdims=True)
    al = bt_row + m_prev
    m_t = jnp.maximum(al, mx)
    w = jnp.exp2(dl - m_t)
    a_t = jnp.exp2(al - m_t)
    sw = st * w
    den = jnp.sum(sw, axis=0, keepdims=True) + a_t * qn_row
    r = 1.0 / jnp.maximum(jnp.abs(den), jnp.exp2(-m_t))
    return sw * r, a_t * r


def _heads_out(ht, nw_tab):
    rows = []
    for a in range(2):
        ha = ht[a * HEAD_DIM:(a + 1) * HEAD_DIM, :]
        ms = jnp.mean(ha * ha, axis=0, keepdims=True)
        rows.append(ha * lax.rsqrt(ms + EPS))
    return (jnp.concatenate(rows, axis=0) * nw_tab).T


def _inproj_sweep_kernel(ctx_ref, x_ref, ab_ref, w_ref, wgate_ref, gb_ref, gbt_ref, cos_ref, sin_ref,
                         lgk_ref, cw_ref, wi_src, w2_src,
                         y_ref, g_ref, gt_ref, sret_ref, cm_ref, nm_ref, mm_ref, qk_ref, wi_dst, w2_dst,
                         s_scr, c_scr, n_scr, m_scr, kdec_scr,
                         p_rk, p_rv, p_mv, p_mqk, p_g, next_row_scr, w_scr, wg_scr, wgt_scr, *, tiles, tpb):
    i = pl.program_id(0)
    _wcast((wi_src, w2_src), (wi_dst, w2_dst))

    def order(t):
        v = _div_small(t, tpb, tiles // tpb)[1]
        return jnp.where(v == 0, 0, tpb - v)

    jt_a = order(jnp.maximum(i - 1, 0))
    lane = _lane()
    sub = lax.broadcasted_iota(jnp.int32, (CHUNK, LANES), 0)
    lo = lane < HEAD_DIM
    lo_row = lo[0:1, :]
    bd_ret = (sub >= HEAD_DIM) == ((lane & 32) != 0)
    bd_m = (sub >= HEAD_DIM) == (lane >= HEAD_DIM)
    gw = 4 * LANES

    @pl.when(i == 0)
    def _():
        pos = sub.astype(F32)
        for p in range(4):
            kdec_scr[p] = jnp.exp(lgk_ref[4 + p:5 + p, :] * pos)
        for ref in (p_rk, p_rv, p_mv, p_mqk, p_g, next_row_scr):
            ref[...] = jnp.zeros_like(ref)
        for grp in range(8 * gw // LANES):
            quarters = (0, 64, 32, 96) if grp < 2 * gw // LANES else (0, 32, 64, 96)
            w_t = jnp.concatenate([w_ref[grp * LANES + a:grp * LANES + a + 32, :] for a in quarters], axis=0)
            w_scr[:, grp * LANES:(grp + 1) * LANES] = w_t.T.astype(BF16)
        gate_row = lax.broadcasted_iota(jnp.int32, wgate_ref.shape, 0)
        gate_t = jnp.where(gate_row < 32, wgate_ref[...], 0.0)
        wg_scr[...] = gate_t.T.astype(BF16)
        wgt_scr[...] = gate_t[0:32, :].astype(BF16)

    @pl.when(jt_a == 0)
    def _():
        s_scr[...] = jnp.zeros_like(s_scr)
        c_scr[...] = jnp.zeros_like(c_scr)
        n_scr[...] = jnp.zeros_like(n_scr)
        m_scr[...] = jnp.zeros_like(m_scr)

    tile_i = jnp.minimum(i, tiles - 1)
    jt_i = order(tile_i)
    hb = _norm_mod(jnp.where(jt_i == 0, ctx_ref[...], x_ref[...]), ab_ref)
    cos, sin = cos_ref[...], sin_ref[...]

    def project(j):
        w_cols = w_scr[:, j * gw:(j + 1) * gw]
        acc = _dot(hb, w_cols)
        if j in (0, 1):
            if j == 0:
                acc = acc * (HEAD_DIM ** -0.5)
            acc = jnp.concatenate([_rope(acc[:, p * LANES:(p + 1) * LANES], cos, sin) for p in range(4)], axis=1)
        elif j == 3:
            acc = acc * jax.nn.sigmoid(acc)
        elif j == 7:
            acc = jax.nn.sigmoid(acc)
        return acc.astype(BF16)

    raw_q, raw_k = project(4), project(5)

    prev_on = jnp.where(jt_a <= 1, 0.0, 1.0).astype(F32)
    next_on = jnp.where((jt_a == 0) | (jt_a == tpb - 1), 0.0, 1.0).astype(F32)
    cur = p_mqk[...].astype(F32)
    n = cur.shape[0]
    row = lax.broadcasted_iota(jnp.int32, cur.shape, 0)
    prev_row = jnp.concatenate([raw_q[n - HALO:, :], raw_k[n - HALO:, :]], axis=1)[HALO - 1:HALO, :].astype(F32)
    xm = jnp.where(row == 0, prev_row * prev_on, pltpu.roll(cur, 1, 0))
    xp = jnp.where(row == n - 1, next_row_scr[0:1, :] * next_on, pltpu.roll(cur, n - 1, 0))
    conv = cw_ref[3:4, :] + cw_ref[0:1, :] * xm + cw_ref[1:2, :] * cur + cw_ref[2:3, :] * xp
    qk = conv * jax.nn.sigmoid(conv)
    qk_ref[:, 0:gw] = (qk[:, 0:gw] * (HEAD_DIM ** -0.5)).astype(BF16)
    qk_ref[:, gw:] = qk[:, gw:].astype(BF16)
    le_bf = (sub <= lane).astype(BF16)
    chunks = (1, 0)
    pre = {}
    for blk in chunks:
        rows = slice(blk * CHUNK, (blk + 1) * CHUNK)
        g = p_g[rows, :]
        bal = pltpu.roll(_cumsum_cols(le_bf, _log_sigmoid(g)), LANES - 8, 1)
        c_all = g - bal
        cmax = jnp.max(c_all, axis=0, keepdims=True)
        w_all = jnp.exp(c_all - cmax)
        vts, ks, nlocs = [], [], []
        for p in range(8):
            sl = slice((p % 4) * LANES, (p % 4 + 1) * LANES)
            if p < 4:
                v2 = p_rv[rows, sl]
                ks.append((p_rk[rows, sl].astype(F32) * kdec_scr[p]).astype(BF16))
            else:
                v2 = p_mv[rows, sl]
                h0 = 16 + 2 * (p - 4)
                kw = qk[rows, gw + (p - 4) * LANES:gw + (p - 3) * LANES] * jnp.where(
                    lo, w_all[:, h0:h0 + 1], w_all[:, h0 + 1:h0 + 2])
                ks.append(kw.astype(BF16))
                nlocs.append(jnp.sum(kw, axis=0, keepdims=True))
            vts.append(v2.astype(F32).T.astype(BF16))
        pre[blk] = (vts, ks, nlocs, cmax, bal[0:1, :])

    cur_rk = project(1)
    y_ref[:, 1 * gw:2 * gw] = cur_rk
    y_ref[:, 0:gw] = project(0)
    kvs = {blk: [_dot(pre[blk][0][p], pre[blk][1][p]) for p in range(8)] for blk in chunks}
    cur_rv = project(2)
    y_ref[:, 2 * gw:3 * gw] = cur_rv
    y_ref[:, 3 * gw:4 * gw] = project(3)

    for blk in chunks:
        _, _, nlocs, cmax, bend = pre[blk]
        sret_ref[blk] = s_scr[...].astype(BF16)
        cm_ref[blk] = c_scr[...].astype(BF16)
        nm_ref[blk] = n_scr[...].astype(BF16)
        mm_ref[blk] = m_scr[...]
        m_old = m_scr[0:1, :]
        mrel = jnp.maximum(m_old, cmax)
        a_row = jnp.exp(m_old - mrel)
        bb_row = jnp.exp(cmax - mrel)
        m_scr[0:1, :] = bend + mrel
        for p in range(4):
            cd = jnp.exp(lgk_ref[4 + p:5 + p, :] * float(CHUNK))
            s_scr[p] = s_scr[p] * cd + jnp.where(bd_ret, kvs[blk][p], 0.0)
            h0 = 16 + 2 * p
            a_l = jnp.where(lo_row, a_row[:, h0:h0 + 1], a_row[:, h0 + 1:h0 + 2])
            bb_l = jnp.where(lo_row, bb_row[:, h0:h0 + 1], bb_row[:, h0 + 1:h0 + 2])
            c_scr[p] = c_scr[p] * a_l + jnp.where(bd_m, kvs[blk][4 + p], 0.0) * bb_l
            n_new = (n_scr[p, 0:1, :] + n_scr[p, 1:2, :]) * a_l + nlocs[p] * bb_l
            n_scr[p, 0:1, :] = jnp.where(lo_row, n_new, 0.0)
            n_scr[p, 1:2, :] = jnp.where(lo_row, 0.0, n_new)

    cur_mv = project(6)
    y_ref[:, 4 * gw:5 * gw] = cur_mv
    y_ref[:, 5 * gw:6 * gw] = project(7)
    gates = _dot(hb, wg_scr[...]) + gb_ref[...]
    g_ref[...] = gates
    gt_ref[...] = _dot_nt(wgt_scr[...], hb) + gbt_ref[...]
    next_row_scr[...] = p_mqk[0:HALO, :].astype(F32)
    p_mqk[:, 0:gw] = raw_q
    p_mqk[:, gw:] = raw_k
    p_rk[...] = cur_rk
    p_rv[...] = cur_rv
    p_mv[...] = cur_mv
    p_g[...] = gates


def _inproj_sweep(ctx2, x2, ab, w_in, gb, gbt, cos, sin, lgk, cw, ffn_w, nb, tpb):
    d = x2.shape[1]
    tm = ROW_TILE
    cpt = tm // CHUNK
    gw = 4 * LANES
    tiles = nb * tpb
    r = tiles * tm
    lat = tpb - 1
    def visit(t):
        b, v = _div_small(t, tpb, nb)
        return b, jnp.where(v == 0, 0, tpb - v)

    cur = lambda i: visit(jnp.minimum(i, tiles - 1))
    flat = lambda bj: bj[0] * tpb + bj[1]
    tile_i = lambda i: flat(cur(i))
    tile_a = lambda i: flat(visit(jnp.maximum(i - 1, 0)))
    in_batch = lambda i: cur(i)[1]
    sel = lambda i: (cur(i)[0] * 2 + jnp.minimum(cur(i)[1], 1), 0, 0)
    state = lambda *dims: pl.BlockSpec((cpt,) + dims, lambda i: (tile_a(i),) + (0,) * len(dims))
    nchunks = tiles * cpt
    wc_in, wc_out, wc_shapes = _wcast_specs(*ffn_w, 0, tiles + 1)
    return pl.pallas_call(
        functools.partial(_inproj_sweep_kernel, tiles=tiles, tpb=tpb),
        grid=(tiles + 1,),
        in_specs=[pl.BlockSpec((tm, d), lambda i: (cur(i)[0], 0)),
                  pl.BlockSpec((tm, d), lambda i: (cur(i)[0] * lat + jnp.maximum(cur(i)[1] - 1, 0), 0)),
                  pl.BlockSpec((None, 2, d), sel),
                  pl.BlockSpec((None, 8 * gw, d), lambda i: (0, 0, 0), pipeline_mode=pl.Buffered(1)),
                  pl.BlockSpec((None, LANES, d), lambda i: (0, 8 * gw // LANES, 0), pipeline_mode=pl.Buffered(1)),
                  _const_spec(gb.shape), _const_spec(gbt.shape),
                  pl.BlockSpec((tm, LANES), lambda i: (in_batch(i), 0)),
                  pl.BlockSpec((tm, LANES), lambda i: (in_batch(i), 0)),
                  _const_spec(lgk.shape), _const_spec(cw.shape)] + wc_in,
        out_specs=[pl.BlockSpec((tm, 6 * gw), lambda i: (tile_i(i), 0)),
                   pl.BlockSpec((tm, LANES), lambda i: (tile_i(i), 0)),
                   pl.BlockSpec((32, tm), lambda i: (0, tile_i(i))),
                   state(4, CHUNK, LANES), state(4, CHUNK, LANES), state(4, HALO, LANES), state(8, LANES),
                   pl.BlockSpec((tm, 2 * gw), lambda i: (tile_a(i), 0))] + wc_out,
        out_shape=[jax.ShapeDtypeStruct((r, 6 * gw), BF16),
                   jax.ShapeDtypeStruct((r, LANES), F32),
                   jax.ShapeDtypeStruct((32, r), F32),
                   jax.ShapeDtypeStruct((nchunks, 4, CHUNK, LANES), BF16),
                   jax.ShapeDtypeStruct((nchunks, 4, CHUNK, LANES), BF16),
                   jax.ShapeDtypeStruct((nchunks, 4, HALO, LANES), BF16),
                   jax.ShapeDtypeStruct((nchunks, 8, LANES), F32),
                   jax.ShapeDtypeStruct((r, 2 * gw), BF16)] + wc_shapes,
        scratch_shapes=[pltpu.VMEM((4, CHUNK, LANES), F32), pltpu.VMEM((4, CHUNK, LANES), F32),
                        pltpu.VMEM((4, HALO, LANES), F32), pltpu.VMEM((8, LANES), F32),
                        pltpu.VMEM((4, CHUNK, LANES), F32),
                        pltpu.VMEM((tm, gw), BF16), pltpu.VMEM((tm, gw), BF16), pltpu.VMEM((tm, gw), BF16),
                        pltpu.VMEM((tm, 2 * gw), BF16), pltpu.VMEM((tm, LANES), F32),
                        pltpu.VMEM((HALO, 2 * gw), F32),
                        pltpu.VMEM((d, 8 * gw), BF16), pltpu.VMEM((d, LANES), BF16), pltpu.VMEM((32, d), BF16)],
        compiler_params=_cparams("arbitrary"),
        name="inproj_bwd_sweep",
    )(ctx2, x2, ab, w_in, w_in, gb, gbt, cos, sin, lgk, cw, *ffn_w)


class _Bag:
    def __init__(self, **kw):
        self.__dict__.update(kw)


def _mixer_chunk_stages(blk, r):
    rows = slice(blk * CHUNK, (blk + 1) * CHUNK)
    lane = _lane()
    sub = lax.broadcasted_iota(jnp.int32, (CHUNK, LANES), 0)
    lo = lane < HEAD_DIM
    sub_lo = sub < HEAD_DIM
    mask_ret = [(lane & 32) == 0, (lane & 32) != 0]
    mask_nat = [lo, lane >= HEAD_DIM]
    bd_ret = (sub >= HEAD_DIM) == ((lane & 32) != 0)
    bd_m = (sub >= HEAD_DIM) == (lane >= HEAD_DIM)
    le = sub <= lane
    ge = sub >= lane

    g = r.g_ref[rows, :]
    gt = r.gt_ref[:, rows]
    lf_col = _log_sigmoid(g)
    lf_row = _log_sigmoid(gt)
    le_bf = le.astype(BF16)
    ge_bf = ge.astype(BF16)
    pre_col = _cumsum_cols(ge_bf, lf_col)
    bal_f = pltpu.roll(pre_col, LANES - 8, 1)
    bal_b = pltpu.roll(pre_col[CHUNK - 1:CHUNK, :] - pre_col + lf_col, LANES - 8, 1)
    cf_all = g - bal_f
    cb_all = g - bal_b
    bf_row = _cumsum_rows(lf_row, le_bf)
    bb_row = bf_row[:, CHUNK - 1:CHUNK] - bf_row + lf_row
    qb, kb, vts, kf32, qt, vbd = [], [], [], [], [], []
    for p in range(8):
        sl = slice((p % 4) * LANES, (p % 4 + 1) * LANES)
        if p < 4:
            q2, k2, v2 = r.rq_ref[rows, sl], r.rk_ref[rows, sl], r.rv_ref[rows, sl]
            qf = q2.astype(F32)
            kf = None
        else:
            q2 = r.qk_ref[rows, sl]
            k2 = r.qk_ref[rows, 4 * LANES + (p - 4) * LANES:4 * LANES + (p - 3) * LANES]
            v2 = r.mv_ref[rows, sl]
            qf, kf = q2.astype(F32), k2.astype(F32)
        vt = v2.astype(F32).T.astype(BF16)
        qb.append(q2)
        kb.append(k2)
        vts.append(vt)
        kf32.append(kf)
        qt.append(qf.T)
        vbd.append([jnp.where(sub_lo, vt, jnp.zeros_like(vt)), jnp.where(sub_lo, jnp.zeros_like(vt), vt)])
    yield

    st2 = []
    for p in range(8):
        masks = mask_ret if p < 4 else mask_nat
        zero = jnp.zeros_like(qb[p])
        qstack = jnp.concatenate([jnp.where(masks[0], qb[p], zero), jnp.where(masks[1], qb[p], zero)], axis=0)
        st2.append(_dot_nt(kb[p], qstack))
    yield

    qn = [_dot_nt(jnp.concatenate([r.n_scr[p].astype(BF16), r.nmb_ref[blk, p]], axis=0), qb[4 + p])
          for p in range(4)]
    qn_f = [x[0:HALO] for x in qn]
    qn_b = [x[HALO:2 * HALO] for x in qn]
    cf2, cb2, bf2, bb2 = cf_all * LOG2E, cb_all * LOG2E, bf_row * LOG2E, bb_row * LOG2E
    m_f2, m_b2 = r.m_state[0:1, :] * LOG2E, r.mmb_ref[blk, 0:1, :] * LOG2E
    lhs, rhs = [], []
    for p in range(8):
        if p < 4:
            pts = [(st2[p][:, a * LANES:(a + 1) * LANES] * r.dm_scr[2 * p + a]).astype(BF16) for a in range(2)]
            x_f, x_b = r.dec_scr[0, p], r.dec_scr[1, p]
            old = [r.s_scr[p].astype(BF16), r.sretb_ref[blk, p]]
        else:
            pts, cf, cb = [], [], []
            for a in range(2):
                h = 2 * (p - 4) + a
                st = st2[p][:, a * LANES:(a + 1) * LANES]
                pf, coef_f = _mlstm_dir_weights(st, qn_f[p - 4][a:a + 1, :], cf2[:, h:h + 1],
                                                bf2[8 + h:9 + h, :], m_f2[0:1, h:h + 1], le)
                pb, coef_b = _mlstm_dir_weights(st, qn_b[p - 4][a:a + 1, :], cb2[:, 16 + h:17 + h],
                                                bb2[24 + h:25 + h, :], m_b2[0:1, 16 + h:17 + h], ge)
                pts.append((pf + pb).astype(BF16))
                cf.append(coef_f)
                cb.append(coef_b)
            x_f = jnp.where(sub_lo, cf[0], cf[1])
            x_b = jnp.where(sub_lo, cb[0], cb[1])
            old = [r.c_scr[p - 4].astype(BF16), r.cmb_ref[blk, p - 4]]
        lhs.append(jnp.concatenate(vbd[p] + old, axis=1))
        rhs.append(jnp.concatenate(pts + [(qt[p] * x_f).astype(BF16), (qt[p] * x_b).astype(BF16)], axis=0))
    yield

    ht = [_dot(lhs[p], rhs[p]) for p in range(8)]
    yield

    for p in range(8):
        sl = slice((p % 4) * LANES, (p % 4 + 1) * LANES)
        gate_ref = r.rg_ref if p < 4 else r.mo_ref
        y = _heads_out(ht[p], r.nw_ref[p])
        new = (y * gate_ref[rows, sl].astype(F32)).astype(BF16)
        r.mix_scr[rows, p * LANES:(p + 1) * LANES] = jnp.where(r.live, new, r.mix_scr[rows, p * LANES:(p + 1) * LANES])
    yield

    for p in range(4):
        cd = jnp.exp(r.lgk_ref[p:p + 1, :] * float(CHUNK))
        _ret_state_update(r.s_scr, p, kb[p], vts[p], r.dec_scr[2, p], cd, bd_ret)
    _mlstm_state_update(r.c_scr, r.n_scr, r.m_state, kf32[4:], vts[4:], cf_all, bal_f[CHUNK - 1:CHUNK, :],
                        0, lo, bd_m)
    yield


def _attn_inproj_stages(x, ab_ref, wq_ref, wk_ref, wv_ref, nw_ref, cos, sin, y_ref):
    hb = _norm_mod(x, ab_ref)
    r2 = lax.broadcasted_iota(jnp.int32, (2 * LANES, 2 * LANES), 0)
    c2 = lax.broadcasted_iota(jnp.int32, (2 * LANES, 2 * LANES), 1)
    same_head = (((r2 ^ c2) & (LANES | 32)) == 0).astype(BF16)
    acc_q = _dot(hb, wq_ref[...])
    acc_k = _dot(hb, wk_ref[...])
    y_ref[:, 10 * LANES:12 * LANES] = _dot(hb, wv_ref[...]).astype(BF16)
    yield
    for j in range(5):
        acc = acc_q[:, j * 2 * LANES:(j + 1) * 2 * LANES] if j < 4 else acc_k
        sq = acc * acc
        ms = _dot(sq.astype(BF16), same_head) * (1.0 / HEAD_DIM)
        nrm = acc * lax.rsqrt(ms + EPS)
        nw = nw_ref[0:1, :] if j < 4 else nw_ref[1:2, :]
        for v in range(2):
            ls = slice(v * LANES, (v + 1) * LANES)
            y_ref[:, (2 * j + v) * LANES:(2 * j + v + 1) * LANES] = _rope(nrm[:, ls] * nw, cos, sin).astype(BF16)
        if j in (1, 4):
            yield


def _ffn_splits(d_ff, pieces):
    blocks = d_ff // (2 * LANES)
    assert blocks * 2 * LANES == d_ff and blocks >= pieces
    cuts = [((i * blocks) // pieces) * 2 * LANES for i in range(pieces)]
    return cuts + [d_ff]


def _mixer_ffn_kernel(lg_ref, y_ref, qk_ref,
                      g_ref, gt_ref, sretb_ref, cmb_ref, nmb_ref, mmb_ref,
                      lgk_ref, lgkt_ref, nw_ref,
                      ctx_ref, x_ref, p_ref, wo_ref, wi_ref, w2_ref,
                      ab2_ref, waq_ref, wak_ref, wav_ref, nwa_ref, cos_ref, sin_ref, wi_src, w2_src,
                      o_ref, y2_ref, wi_dst, w2_dst,
                      s_scr, c_scr, n_scr, m_state, dm_scr, dec_scr, mix_scr, act_scr, x2_scr,
                      *, tiles, tpb, d_ff):
    s = pl.program_id(0)
    _wcast((wi_src, w2_src), (wi_dst, w2_dst))
    nb = tiles // tpb
    jt = _div_small(jnp.minimum(s, tiles - 1), tpb, nb)[1]
    gw = 4 * LANES
    rq_ref, rk_ref, rv_ref, rg_ref, mv_ref, mo_ref = [y_ref.at[:, j * gw:(j + 1) * gw] for j in range(6)]
    lane = _lane()
    sub = lax.broadcasted_iota(jnp.int32, (CHUNK, LANES), 0)

    @pl.when(s == 0)
    def _():
        mix_scr[...] = jnp.zeros_like(mix_scr)
        x2_scr[...] = jnp.zeros_like(x2_scr)
        le = sub <= lane
        ge = sub >= lane
        spos = sub.astype(F32)
        tpos = lane.astype(F32)
        diff = (lane - sub).astype(F32)
        for h in range(8):
            dm_scr[h] = (jnp.where(le, jnp.exp(lg_ref[h] * diff), 0.0)
                         + jnp.where(ge, jnp.exp(lg_ref[8 + h] * (-diff)), 0.0))
        for p in range(4):
            dec_scr[0, p] = jnp.exp(lgkt_ref[:, p:p + 1] * (tpos + 1.0))
            dec_scr[1, p] = jnp.exp(lgkt_ref[:, 4 + p:5 + p] * (float(CHUNK) - tpos))
            dec_scr[2, p] = jnp.exp(lgk_ref[p:p + 1, :] * (float(CHUNK) - 1.0 - spos))

    @pl.when(jt == 0)
    def _():
        s_scr[...] = jnp.zeros_like(s_scr)
        c_scr[...] = jnp.zeros_like(c_scr)
        n_scr[...] = jnp.zeros_like(n_scr)
        m_state[...] = jnp.zeros_like(m_state)

    r = _Bag(rq_ref=rq_ref, rk_ref=rk_ref, rv_ref=rv_ref, rg_ref=rg_ref, qk_ref=qk_ref, mv_ref=mv_ref, mo_ref=mo_ref,
             g_ref=g_ref, gt_ref=gt_ref, sretb_ref=sretb_ref, cmb_ref=cmb_ref, nmb_ref=nmb_ref, mmb_ref=mmb_ref,
             lgk_ref=lgk_ref, nw_ref=nw_ref, s_scr=s_scr, c_scr=c_scr, n_scr=n_scr, m_state=m_state,
             dm_scr=dm_scr, dec_scr=dec_scr, mix_scr=mix_scr, live=s < tiles)
    cuts = _ffn_splits(d_ff, 3)
    ffn_piece = lambda i: _ffn_cols(h, wi_ref, act_scr, cuts[i], cuts[i + 1], d_ff)

    x = _ctx_or_latent_rows(ctx_ref, x_ref, jnp.clip(s - 1, 0, tiles - 1), nb, tpb)
    x1, h = _ffn_in(x, mix_scr[...], p_ref, wo_ref)

    nxt = _attn_inproj_stages(x2_scr[...], ab2_ref, waq_ref, wak_ref, wav_ref, nwa_ref, cos_ref[...], sin_ref[...],
                              y2_ref)
    chunk_a, chunk_b = _mixer_chunk_stages(0, r), _mixer_chunk_stages(1, r)
    next(nxt)
    next(chunk_a), next(chunk_b)
    next(chunk_a), next(chunk_b)
    ffn_piece(0)
    next(chunk_a), next(chunk_a)
    ffn_piece(1)
    next(chunk_a), next(chunk_a)
    next(nxt)
    ffn_piece(2)
    next(chunk_b), next(chunk_b)
    next(nxt)
    x2 = x1 + p_ref[3:4, :] * _dot(act_scr[...], w2_ref[...])
    o_ref[...] = x2
    next(chunk_b), next(chunk_b)
    x2_scr[...] = x2


def _mixer_ffn(y, g, gt, states, lg_smem, lgk, lgkt, nw, xs, prm, wo, wi, w2,
               ab_next, w_next, nw_next, cos, sin, ffn_w, nb, tpb):
    gw = 4 * LANES
    tm = ROW_TILE
    cpt = tm // CHUNK
    sretb, cmb, nmb, mmb, qk_act = states
    tiles = nb * tpb
    d = xs[-1].shape[1]
    d_ff = w2.shape[0]
    n_next = sum(a.shape[1] for a in w_next)
    mix = lambda s: jnp.minimum(s, tiles - 1)
    ffn = lambda s: jnp.clip(s - 1, 0, tiles - 1)
    nxt = lambda s: jnp.maximum(s - 2, 0)
    state_spec = lambda a: pl.BlockSpec((cpt,) + a.shape[1:], lambda s: (mix(s),) + (0,) * (a.ndim - 1))
    def sel(t):
        b, j = _div_small(t, tpb, nb)
        return b * 2 + jnp.minimum(j, 1), 0, 0

    wc_in, wc_out, wc_shapes = _wcast_specs(*ffn_w, 1, tiles + 2)
    return pl.pallas_call(
        functools.partial(_mixer_ffn_kernel, tiles=tiles, tpb=tpb, d_ff=d_ff),
        grid=(tiles + 2,),
        in_specs=[_smem_spec(),
                  pl.BlockSpec((tm, 6 * gw), lambda s: (mix(s), 0)),
                  pl.BlockSpec((tm, 2 * gw), lambda s: (mix(s), 0)),
                  pl.BlockSpec((tm, LANES), lambda s: (mix(s), 0)),
                  pl.BlockSpec((32, tm), lambda s: (0, mix(s))),
                  state_spec(sretb), state_spec(cmb), state_spec(nmb), state_spec(mmb),
                  _const_spec(lgk.shape), _const_spec(lgkt.shape), _const_spec(nw.shape)]
                 + _split_row_specs(tm, d, nb, tpb, ffn)
                 + [pl.BlockSpec((None, 8, d), lambda s: sel(ffn(s))),
                    _const_spec(wo.shape), _const_spec(wi.shape), _const_spec(w2.shape),
                    pl.BlockSpec((None, 2, d), lambda s: sel(nxt(s))),
                    *[_const_spec(a.shape) for a in w_next], _const_spec(nw_next.shape),
                    pl.BlockSpec((tm, LANES), lambda s: (_div_small(nxt(s), tpb, nb)[1], 0)),
                    pl.BlockSpec((tm, LANES), lambda s: (_div_small(nxt(s), tpb, nb)[1], 0))] + wc_in,
        out_specs=[pl.BlockSpec((tm, d), lambda s: (ffn(s), 0)),
                   pl.BlockSpec((tm, n_next), lambda s: (nxt(s), 0))] + wc_out,
        out_shape=[jax.ShapeDtypeStruct((tiles * tm, d), F32),
                   jax.ShapeDtypeStruct((tiles * tm, n_next), BF16)] + wc_shapes,
        scratch_shapes=[pltpu.VMEM((4, CHUNK, LANES), F32), pltpu.VMEM((4, CHUNK, LANES), F32),
                        pltpu.VMEM((4, HALO, LANES), F32), pltpu.VMEM((8, LANES), F32),
                        pltpu.VMEM((8, CHUNK, LANES), F32), pltpu.VMEM((3, 4, CHUNK, LANES), F32),
                        pltpu.VMEM((tm, 2 * gw), BF16), pltpu.VMEM((tm, d_ff), BF16),
                        pltpu.VMEM((tm, d), F32)],
        compiler_params=_cparams("arbitrary"),
        name="ret_mlstm_mixer_ffn",
    )(lg_smem, y, qk_act, g, gt, sretb, cmb, nmb, mmb, lgk, lgkt, nw,
      *xs, prm, wo, wi, w2, ab_next, *w_next, nw_next, cos, sin, *ffn_w)


def _ffn_in(x, m, p_ref, wo_ref):
    x1 = x + p_ref[0:1, :] * _dot(m, wo_ref[...])
    ms = jnp.mean(x1 * x1, axis=-1, keepdims=True)
    return x1, ((x1 * lax.rsqrt(ms + EPS)) * p_ref[1:2, :] + p_ref[2:3, :]).astype(BF16)


def _ffn_cols(h, wi_ref, act_scr, lo, hi, d_ff):
    gate = _dot(h, wi_ref[:, lo:hi])
    up = _dot(h, wi_ref[:, d_ff + lo:d_ff + hi])
    act_scr[:, lo:hi] = (gate * jax.nn.sigmoid(gate) * up).astype(BF16)


def _window_bias():
    kk = np.arange(CHUNK)[:, None]
    t = np.arange(CHUNK)[None, :]
    tabs = []
    for has_prev, has_next in ((False, True), (True, True), (True, False)):
        prev_ok = (kk >= t) & has_prev
        next_ok = (kk <= t) & has_next
        tabs.append(np.where(np.concatenate([prev_ok, next_ok], axis=0), 0.0, NEG))
    return jnp.asarray(np.stack(tabs), F32)


def _attn_ffn_kernel(sink_ref, qkv_a_ref, qkv_b_ref, kvp_ref, kvn_ref, kvx_ref, bias_ref,
                     x_a_ref, x_b_ref, p_ref, wo_ref, wi_ref, w2_ref, o_ref, m_scr, act_scr,
                     *, steps, per_batch, d_ff):
    s = pl.program_id(0)
    kw = 2 * LANES
    k0 = 8 * LANES
    tm = qkv_a_ref.shape[0]
    nblk = 2 * tm // CHUNK

    @pl.when(s == 0)
    def _():
        m_scr[...] = jnp.zeros_like(m_scr)

    j = _div_small(jnp.minimum(s, steps - 1), per_batch, steps // per_batch)[1]
    grp = lax.broadcasted_iota(jnp.int32, (1, 4 * CHUNK), 1) // CHUNK
    lane = _lane()
    mask_q = [(lane & 32) == 0, (lane & 32) != 0]
    tile4 = lambda b: jnp.concatenate([b] * 4, axis=1)
    inner = tile4(bias_ref[1])
    biases = ([tile4(jnp.where(j == 0, bias_ref[0], bias_ref[1]))] + [inner] * (nblk - 2)
              + [tile4(jnp.where(j == per_batch - 1, bias_ref[2], bias_ref[1]))])
    cuts = _ffn_splits(d_ff, nblk - 1)

    def ffn_up(i):
        _ffn_cols(h, wi_ref, act_scr, cuts[i], cuts[i + 1], d_ff)

    def q_rows(blk):
        ref = qkv_a_ref if blk * CHUNK < tm else qkv_b_ref
        r0 = (blk * CHUNK) % tm
        return ref, slice(r0, r0 + CHUNK)

    def scores(blk, kv):
        kvp, a = kv // 2, kv % 2
        ref, rows = q_rows(blk)
        qs = jnp.concatenate(
            [jnp.where(mask_q[a], ref[rows, (kvp * 4 + g) * LANES:(kvp * 4 + g + 1) * LANES],
                       jnp.zeros((CHUNK, LANES), BF16)) for g in range(4)], axis=0)
        return _dot_nt(kcats[kvp][blk], qs)

    def softmax_pv(blk, kv, st):
        bias = biases[blk]
        st = jnp.concatenate([st[0:CHUNK] + bias[0:CHUNK], st[CHUNK:2 * CHUNK],
                              st[2 * CHUNK:3 * CHUNK] + bias[CHUNK:2 * CHUNK], st[3 * CHUNK:]], axis=0)
        snk = jnp.where(grp == 0, sink_ref[kv * 4],
                        jnp.where(grp == 1, sink_ref[kv * 4 + 1],
                                  jnp.where(grp == 2, sink_ref[kv * 4 + 2], sink_ref[kv * 4 + 3])))
        m = jnp.maximum(jnp.max(st, axis=0, keepdims=True), snk)
        e = jnp.exp2(st - m)
        denom = jnp.exp2(snk - m) + jnp.sum(e, axis=0, keepdims=True)
        a = kv % 2
        return _dot(vts[kv // 2][blk][a * HEAD_DIM:(a + 1) * HEAD_DIM, :], e.astype(BF16)) * (1.0 / denom)

    def hand_over(blk, outs):
        for kvp in range(2):
            full = jnp.concatenate(outs[2 * kvp:2 * kvp + 2], axis=0)
            for g in range(4):
                m_scr[blk * CHUNK:(blk + 1) * CHUNK, (kvp * 4 + g) * LANES:(kvp * 4 + g + 1) * LANES] = (
                    full[:, g * CHUNK:(g + 1) * CHUNK].T.astype(BF16))

    x1, h = _ffn_in(jnp.concatenate([x_a_ref[...], x_b_ref[...]], axis=0), m_scr[...], p_ref, wo_ref)

    kcats, vts = [], []
    for kvp in range(2):
        ks = slice(k0 + kvp * LANES, k0 + (kvp + 1) * LANES)
        vs = slice(k0 + kw + kvp * LANES, k0 + kw + (kvp + 1) * LANES)
        es, ev = slice(kvp * LANES, (kvp + 1) * LANES), slice(kw + kvp * LANES, kw + (kvp + 1) * LANES)
        tile_rows = [(ref, slice(c * CHUNK, (c + 1) * CHUNK)) for ref in (qkv_a_ref, qkv_b_ref)
                     for c in range(tm // CHUNK)]
        k_chunks = [kvp_ref[:, es]] + [ref[rows, ks] for ref, rows in tile_rows] + [kvn_ref[:, es]]
        v_chunks = ([kvp_ref[:, ev]] + [ref[rows, vs] for ref, rows in tile_rows] + [kvn_ref[:, ev]]
                    + [kvx_ref[c * CHUNK:(c + 1) * CHUNK, ev] for c in range(kvx_ref.shape[0] // CHUNK)])
        v_t = [v.astype(F32).T.astype(BF16) for v in v_chunks]
        kcats.append([jnp.concatenate(k_chunks[b:b + 3] + [kvx_ref[:, es]], axis=0) for b in range(nblk)])
        vts.append([jnp.concatenate(v_t[b:b + 3] + v_t[nblk + 2:], axis=1) for b in range(nblk)])

    for blk in range(nblk):
        sts = [scores(blk, kv) for kv in range(4)]
        if blk < nblk - 1:
            ffn_up(blk)
        else:
            o_ref[...] = x1 + p_ref[3:4, :] * _dot(act_scr[...], w2_ref[...])
        hand_over(blk, [softmax_pv(blk, kv, sts[kv]) for kv in range(4)])


def _attn_ffn(y, sink, xc, prm, wo, wi, w2, nb, nc, ctx_chunks):
    tm = ROW_TILE
    cpt = tm // CHUNK
    tpb = nc // cpt
    lat_tiles = (nc - ctx_chunks) // cpt
    ctx_tiles = ctx_chunks // cpt
    assert ctx_tiles == 1 and lat_tiles % 2 == 0 and lat_tiles >= 4
    per_batch = lat_tiles // 2
    steps = nb * per_batch
    d = xc.shape[1]
    d_ff = w2.shape[0]
    bias = _window_bias()
    n_qkv = y.shape[1]
    kv_w = n_qkv - d
    assert d % kv_w == 0
    kv_col = d // kv_w
    att = lambda s: jnp.minimum(s, steps - 1)
    ffn = lambda s: jnp.maximum(s - 1, 0)
    split = lambda t: _div_small(t, per_batch, nb)
    row_tile = lambda t, u: split(t)[0] * tpb + ctx_tiles + 2 * split(t)[1] + u
    chunk0 = lambda t: split(t)[0] * nc + ctx_chunks
    prev_c = lambda s: chunk0(att(s)) + jnp.maximum(split(att(s))[1] * 2 * cpt - 1, 0)
    next_c = lambda s: chunk0(att(s)) + jnp.minimum(split(att(s))[1] * 2 * cpt + 2 * cpt, lat_tiles * cpt - 1)
    edge_spec = lambda f: pl.BlockSpec((CHUNK, kv_w), lambda s: (f(s), kv_col))
    return pl.pallas_call(
        functools.partial(_attn_ffn_kernel, steps=steps, per_batch=per_batch, d_ff=d_ff),
        grid=(steps + 1,),
        in_specs=[_smem_spec(),
                  pl.BlockSpec((tm, n_qkv), lambda s: (row_tile(att(s), 0), 0)),
                  pl.BlockSpec((tm, n_qkv), lambda s: (row_tile(att(s), 1), 0)),
                  edge_spec(prev_c), edge_spec(next_c),
                  pl.BlockSpec((tm, kv_w), lambda s: (split(att(s))[0] * tpb, kv_col)),
                  _const_spec(bias.shape),
                  pl.BlockSpec((tm, d), lambda s: (row_tile(ffn(s), 0), 0)),
                  pl.BlockSpec((tm, d), lambda s: (row_tile(ffn(s), 1), 0)),
                  pl.BlockSpec((None, 8, d), lambda s: (split(ffn(s))[0] * 2 + 1, 0, 0)),
                  _const_spec(wo.shape), _const_spec(wi.shape), _const_spec(w2.shape)],
        out_specs=pl.BlockSpec((2 * tm, d), lambda s: (ffn(s), 0)),
        out_shape=jax.ShapeDtypeStruct((steps * 2 * tm, d), F32),
        scratch_shapes=[pltpu.VMEM((2 * tm, d), BF16), pltpu.VMEM((2 * tm, d_ff), BF16)],
        compiler_params=_cparams("arbitrary"),
        name="window_gqa_ffn",
    )(sink, y, y, y, y, y, bias, xc, xc, prm, wo, wi, w2)


def _pair_cols(w):
    rows, cols = w.shape
    return w.reshape(rows, cols // LANES, 2, 2, 32).transpose(0, 1, 3, 2, 4).reshape(rows, cols)


def _attn_q_cols(w):
    rows = w.shape[0]
    g_per = w.shape[1] // (H_KV * HEAD_DIM)
    return (w.reshape(rows, H_KV // 2, 2, g_per, 2, 32).transpose(0, 1, 3, 4, 2, 5)
            .reshape(rows, w.shape[1]))


def _attn_o_rows(w):
    cols = w.shape[1]
    g_per = w.shape[0] // (H_KV * HEAD_DIM)
    return (w.reshape(H_KV // 2, 2, g_per, HEAD_DIM, cols).transpose(0, 2, 1, 3, 4)
            .reshape(w.shape[0], cols))


def _rope_tables(seq, ctx_len):
    rows = seq // GRID_W
    row = np.repeat(np.arange(rows, dtype=np.float32), GRID_W)
    col = np.tile(np.arange(GRID_W, dtype=np.float32), rows)
    n = HEAD_DIM // 4
    inv = (np.float32(ROPE_BASE) ** (-np.arange(n, dtype=np.float32) / np.float32(n))).astype(np.float32)
    ang = np.concatenate([row[:, None] * inv, col[:, None] * inv], axis=-1).astype(np.float32)
    cos, sin = np.cos(ang), np.sin(ang)
    cos_t = np.concatenate([np.ones((ctx_len, LANES), np.float32), np.tile(cos, (1, 4))], axis=0)
    sin_t = np.concatenate([np.zeros((ctx_len, LANES), np.float32),
                            np.concatenate([-sin, -sin, sin, sin], axis=-1)], axis=0)
    return jnp.asarray(cos_t, F32), jnp.asarray(sin_t, F32)


def _mod_tables(mod, nb, norm_w):
    d = norm_w.shape[-1]
    lat = mod[:nb].reshape(nb, 6, d)
    ctx = jnp.broadcast_to(mod[nb].reshape(1, 6, d), (nb, 6, d))
    both = jnp.stack([ctx, lat], axis=1).reshape(nb * 2, 6, d)
    sh1, sc1, g1, sh2, sc2, g2 = [both[:, k] for k in range(6)]
    ab1 = jnp.stack([norm_w[0] * (1.0 + sc1), sh1], axis=1)
    zeros = jnp.zeros_like(g1)
    prm = jnp.stack([g1, norm_w[1] * (1.0 + sc2), sh2, g2, zeros, zeros, zeros, zeros], axis=1)
    return ab1, prm


def kernel(x, c, ctx, c_ctx, ada_w, ada_b, norm_w, ffn_w_in, ffn_w_out, ab_w_in, ab_w_out,
           ret_log_gamma, ret_norm_w, mlstm_conv_w, mlstm_conv_b, mlstm_gate_b, mlstm_norm_w,
           attn_w_in, attn_w_out, attn_q_norm_w, attn_k_norm_w, attn_sink):
    nb, seq, d = x.shape
    ctx_len = ctx.shape[1]
    depth = ada_w.shape[0]
    assert ctx_len == ROW_TILE and seq % ROW_TILE == 0 and d == 8 * LANES and nb < 8
    t_all = ctx_len + seq
    nc = t_all // CHUNK
    ctx_chunks = ctx_len // CHUNK
    tpb = t_all // ROW_TILE
    dr = d // 2

    rows = jnp.zeros((8, d), F32).at[:nb].set(c).at[nb].set(c_ctx)
    mod_all = _modulation(rows, ada_w, ada_b)
    cos_t, sin_t = _rope_tables(seq, ctx_len)
    ffn_w = (ffn_w_in, ffn_w_out)
    xs = (ctx.reshape(nb * ctx_len, d), x.reshape(nb * seq, d))

    assert depth == 2 and ab_w_in.shape[0] == 1 and attn_w_in.shape[0] == 1
    ab_0, prm_0 = _mod_tables(mod_all[0], nb, norm_w[0])
    ab_1, prm_1 = _mod_tables(mod_all[1], nb, norm_w[1])

    assert ab_w_in.shape[2] == 8 * dr + 32
    w_in_t = jnp.swapaxes(ab_w_in, 1, 2)
    gb = jnp.zeros((1, LANES), F32).at[0, :32].set(mlstm_gate_b[0].reshape(-1))
    gbt = mlstm_gate_b[0].reshape(32, 1)
    lg = ret_log_gamma[0].astype(F32)
    lgk = jnp.tile(jnp.repeat(lg.reshape(2, 4, 2), 32, axis=-1), (1, 1, 2)).reshape(8, LANES)
    cw = jnp.concatenate([mlstm_conv_w[0], mlstm_conv_b[0][None], jnp.zeros((4, 2 * dr), F32)], axis=0)
    nw = jnp.broadcast_to(jnp.concatenate([ret_norm_w[0], mlstm_norm_w[0]]).reshape(8, LANES, 1),
                          (8, LANES, LANES))
    wo_0 = ab_w_out[0].astype(BF16)

    w = attn_w_in[0]
    w_attn = (_attn_q_cols(w[:, :d]).astype(BF16), _pair_cols(w[:, d:d + 2 * LANES]).astype(BF16),
              w[:, d + 2 * LANES:].astype(BF16))
    lane_w = lambda v: jnp.concatenate([v[:32], v[:32], v[32:], v[32:]])
    nwq = jnp.stack([lane_w(attn_q_norm_w[0]) * (HEAD_DIM ** -0.5 * LOG2E), lane_w(attn_k_norm_w[0])]
                    + [jnp.zeros((LANES,), F32)] * 6)
    wo_1 = _attn_o_rows(attn_w_out[0]).astype(BF16)

    y, g, gt, *states, wi_0, w2_0 = _inproj_sweep(*xs, ab_0, w_in_t, gb, gbt, cos_t, sin_t, lgk, cw,
                                                  ffn_w, nb, tpb)
    x_mid, y_attn, wi_1, w2_1 = _mixer_ffn(y, g, gt, states, lg.reshape(-1), lgk, lgk.T, nw, xs, prm_0, wo_0,
                                           wi_0, w2_0, ab_1, w_attn, nwq, cos_t, sin_t, ffn_w, nb, tpb)
    out = _attn_ffn(y_attn, attn_sink[0].astype(F32) * LOG2E, x_mid, prm_1, wo_1, wi_1, w2_1, nb, nc, ctx_chunks)
    return out.reshape(nb, seq, d)
```

```python
import functools

import numpy as np
import jax
import jax.numpy as jnp
from jax import lax
from jax.experimental import pallas as pl
from jax.experimental.pallas import tpu as pltpu

F32 = jnp.float32
BF16 = jnp.bfloat16

HEAD_DIM = 64
CHUNK = 128
GRID_W = 64
ROPE_BASE = 10000.0
EPS = 1e-6
H_KV = 4
LANES = 128
ROW_TILE = 256
HALO = 16
LOG2E = 1.4426950408889634
NEG = -1e30
VMEM_LIMIT = 58 * 1024 * 1024


def _cparams(*sem):
    return pltpu.CompilerParams(dimension_semantics=sem, vmem_limit_bytes=VMEM_LIMIT)


def _const_spec(shape):
    nd = len(shape)
    return pl.BlockSpec(shape, lambda *_: (0,) * nd, pipeline_mode=pl.Buffered(1))


def _smem_spec():
    return pl.BlockSpec(memory_space=pltpu.SMEM)


def _wcast_specs(w_in_all, w_out_all, layer, steps):
    specs_in, specs_out, shapes = [], [], []
    for w in (w_in_all, w_out_all):
        total, cols = w.shape[1:]
        rows = next(r for r in range(HALO, total + 1, HALO) if total % r == 0 and total // r <= steps)
        n_blocks = total // rows
        specs_in.append(pl.BlockSpec((None, rows, cols),
                                     lambda i, n=n_blocks: (layer, jnp.minimum(i, n - 1), 0)))
        specs_out.append(pl.BlockSpec((rows, cols), lambda i, n=n_blocks: (jnp.minimum(i, n - 1), 0)))
        shapes.append(jax.ShapeDtypeStruct(w.shape[1:], BF16))
    return specs_in, specs_out, shapes


def _wcast(src_refs, dst_refs):
    for src, dst in zip(src_refs, dst_refs):
        dst[...] = src[...].astype(BF16)


def _lane(shape=(CHUNK, LANES)):
    return lax.broadcasted_iota(jnp.int32, shape, len(shape) - 1)


def _dot(a, b):
    return jnp.dot(a, b, preferred_element_type=F32)


def _dot_nt(a, b):
    return lax.dot_general(a, b, (((1,), (1,)), ((), ())), preferred_element_type=F32)


def _div_small(t, m, n):
    q = 0
    for b in range(1, n):
        q = q + jnp.where(t >= b * m, 1, 0)
    return q, t - q * m


def _split3(x):
    hi = x.astype(BF16)
    r = x - hi.astype(F32)
    mid = r.astype(BF16)
    lo = (r - mid.astype(F32)).astype(BF16)
    return hi, mid, lo


def _log_sigmoid(x):
    return jnp.minimum(x, 0.0) - jnp.log1p(jnp.exp(-jnp.abs(x)))


def _rope(x, cos, sin_signed):
    return x * cos + pltpu.roll(x, LANES // 2, 1) * sin_signed


def _mod_kernel(rows_ref, w_ref, b_ref, o_ref):
    a = rows_ref[...]
    a = a * jax.nn.sigmoid(a)
    a_hi = a.astype(BF16)
    a_lo = (a - a_hi.astype(F32)).astype(BF16)
    w = w_ref[...]
    w_hi = w.astype(BF16)
    w_lo = (w - w_hi.astype(F32)).astype(BF16)
    o_ref[...] = _dot(a_hi, w_hi) + _dot(a_hi, w_lo) + _dot(a_lo, w_hi) + b_ref[...]


def _modulation(rows, ada_w, ada_b):
    depth, d, n = ada_w.shape
    tn = n // 4
    return pl.pallas_call(
        _mod_kernel,
        grid=(depth, n // tn),
        in_specs=[pl.BlockSpec((8, d), lambda l, j: (0, 0)),
                  pl.BlockSpec((None, d, tn), lambda l, j: (l, 0, j)),
                  pl.BlockSpec((None, 1, tn), lambda l, j: (l, 0, j))],
        out_specs=pl.BlockSpec((None, 8, tn), lambda l, j: (l, 0, j)),
        out_shape=jax.ShapeDtypeStruct((depth, 8, n), F32),
        compiler_params=_cparams("arbitrary", "arbitrary"),
        name="adaln_modulation",
    )(rows, ada_w, ada_b.reshape(depth, 1, n))


def _norm_mod(x, ab_ref):
    ms = jnp.mean(x * x, axis=-1, keepdims=True)
    h = (x * lax.rsqrt(ms + EPS)) * ab_ref[0:1, :] + ab_ref[1:2, :]
    return h.astype(BF16)


def _ctx_or_latent_rows(ctx_ref, x_ref, tile, nb, tiles_per_batch):
    return jnp.where(_div_small(tile, tiles_per_batch, nb)[1] == 0, ctx_ref[...], x_ref[...])


def _split_row_specs(tm, d, nb, tpb, tile_of):
    lat = tpb - 1

    def latent_row(i):
        b, j = _div_small(tile_of(i), tpb, nb)
        return b * lat + jnp.maximum(j - 1, 0), 0

    return [pl.BlockSpec((tm, d), lambda i: (_div_small(tile_of(i), tpb, nb)[0], 0)),
            pl.BlockSpec((tm, d), latent_row)]


def _cumsum_cols(tri_bf, lf):
    hi, mid, lo = _split3(lf)
    return _dot(tri_bf, hi) + _dot(tri_bf, mid) + _dot(tri_bf, lo)


def _cumsum_rows(lf, tri_bf):
    hi, mid, lo = _split3(lf)
    return _dot(hi, tri_bf) + _dot(mid, tri_bf) + _dot(lo, tri_bf)


def _ret_state_update(s_ref, p, k2, vt, kdec, cd_lanes, bd):
    kf = (k2.astype(F32) * kdec).astype(BF16)
    s_ref[p] = s_ref[p] * cd_lanes + jnp.where(bd, _dot(vt, kf), 0.0)


def _mlstm_state_update(c_ref, n_ref, m_ref, k_pairs, vt_pairs, c_all, bend, col0, lo, bd):
    cmax = jnp.max(c_all, axis=0, keepdims=True)
    w_all = jnp.exp(c_all - cmax)
    m_old = m_ref[0:1, :]
    mrel = jnp.maximum(m_old, cmax)
    a_row = jnp.exp(m_old - mrel)
    bb_row = jnp.exp(cmax - mrel)
    m_ref[0:1, :] = bend + mrel
    lo_row = lo[0:1, :]
    for p in range(4):
        h0 = col0 + 2 * p
        kw = k_pairs[p] * jnp.where(lo, w_all[:, h0:h0 + 1], w_all[:, h0 + 1:h0 + 2])
        kvt = _dot(vt_pairs[p], kw.astype(BF16))
        nloc = jnp.sum(kw, axis=0, keepdims=True)
        a_l = jnp.where(lo_row, a_row[:, h0:h0 + 1], a_row[:, h0 + 1:h0 + 2])
        bb_l = jnp.where(lo_row, bb_row[:, h0:h0 + 1], bb_row[:, h0 + 1:h0 + 2])
        c_ref[p] = c_ref[p] * a_l + jnp.where(bd, kvt, 0.0) * bb_l
        n_new = (n_ref[p, 0:1, :] + n_ref[p, 1:2, :]) * a_l + nloc * bb_l
        n_ref[p, 0:1, :] = jnp.where(lo_row, n_new, 0.0)
        n_ref[p, 1:2, :] = jnp.where(lo_row, 0.0, n_new)


def _mlstm_dir_weights(st, qn_row, c_col, bt_row, m_prev, tri):
    dl = jnp.where(tri, c_col + bt_row, NEG)
    mx = jnp.max(dl, axis=0, keepdims=True)
    al = bt_row + m_prev
    m_t = jnp.maximum(al, mx)
    w = jnp.exp2(dl - m_t)
    a_t = jnp.exp2(al - m_t)
    sw = st * w
    den = jnp.sum(sw, axis=0, keepdims=True) + a_t * qn_row
    r = 1.0 / jnp.maximum(jnp.abs(den), jnp.exp2(-m_t))
    return sw * r, a_t * r


def _heads_out(ht, nw_tab):
    rows = []
    for a in range(2):
        ha = ht[a * HEAD_DIM:(a + 1) * HEAD_DIM, :]
        ms = jnp.mean(ha * ha, axis=0, keepdims=True)
        rows.append(ha * lax.rsqrt(ms + EPS))
    return (jnp.concatenate(rows, axis=0) * nw_tab).T


def _inproj_sweep_kernel(ctx_ref, x_ref, ab_ref, w_ref, wgate_ref, gb_ref, gbt_ref, cos_ref, sin_ref,
                         lgk_ref, cw_ref, wi_src, w2_src,
                         y_ref, g_ref, gt_ref, sret_ref, cm_ref, nm_ref, mm_ref, qk_ref, wi_dst, w2_dst,
                         s_scr, c_scr, n_scr, m_scr, kdec_scr,
                         p_rk, p_rv, p_mv, p_mqk, p_g, next_row_scr, w_scr, wg_scr, wgt_scr, *, tiles, tpb):
    i = pl.program_id(0)
    _wcast((wi_src, w2_src), (wi_dst, w2_dst))

    def order(t):
        v = _div_small(t, tpb, tiles // tpb)[1]
        return jnp.where(v == 0, 0, tpb - v)

    jt_a = order(jnp.maximum(i - 1, 0))
    lane = _lane()
    sub = lax.broadcasted_iota(jnp.int32, (CHUNK, LANES), 0)
    lo = lane < HEAD_DIM
    lo_row = lo[0:1, :]
    bd_ret = (sub >= HEAD_DIM) == ((lane & 32) != 0)
    bd_m = (sub >= HEAD_DIM) == (lane >= HEAD_DIM)
    gw = 4 * LANES

    @pl.when(i == 0)
    def _():
        pos = sub.astype(F32)
        for p in range(4):
            kdec_scr[p] = jnp.exp(lgk_ref[4 + p:5 + p, :] * pos)
        for ref in (p_rk, p_rv, p_mv, p_mqk, p_g, next_row_scr):
            ref[...] = jnp.zeros_like(ref)
        for grp in range(8 * gw // LANES):
            quarters = (0, 64, 32, 96) if grp < 2 * gw // LANES else (0, 32, 64, 96)
            w_t = jnp.concatenate([w_ref[grp * LANES + a:grp * LANES + a + 32, :] for a in quarters], axis=0)
            w_scr[:, grp * LANES:(grp + 1) * LANES] = w_t.T.astype(BF16)
        gate_row = lax.broadcasted_iota(jnp.int32, wgate_ref.shape, 0)
        gate_t = jnp.where(gate_row < 32, wgate_ref[...], 0.0)
        wg_scr[...] = gate_t.T.astype(BF16)
        wgt_scr[...] = gate_t[0:32, :].astype(BF16)

    @pl.when(jt_a == 0)
    def _():
        s_scr[...] = jnp.zeros_like(s_scr)
        c_scr[...] = jnp.zeros_like(c_scr)
        n_scr[...] = jnp.zeros_like(n_scr)
        m_scr[...] = jnp.zeros_like(m_scr)

    tile_i = jnp.minimum(i, tiles - 1)
    jt_i = order(tile_i)
    hb = _norm_mod(jnp.where(jt_i == 0, ctx_ref[...], x_ref[...]), ab_ref)
    cos, sin = cos_ref[...], sin_ref[...]

    def project(j):
        w_cols = w_scr[:, j * gw:(j + 1) * gw]
        acc = _dot(hb, w_cols)
        if j in (0, 1):
            if j == 0:
                acc = acc * (HEAD_DIM ** -0.5)
            acc = jnp.concatenate([_rope(acc[:, p * LANES:(p + 1) * LANES], cos, sin) for p in range(4)], axis=1)
        elif j == 3:
            acc = acc * jax.nn.sigmoid(acc)
        elif j == 7:
            acc = jax.nn.sigmoid(acc)
        return acc.astype(BF16)

    raw_q, raw_k = project(4), project(5)

    prev_on = jnp.where(jt_a <= 1, 0.0, 1.0).astype(F32)
    next_on = jnp.where((jt_a == 0) | (jt_a == tpb - 1), 0.0, 1.0).astype(F32)
    cur = p_mqk[...].astype(F32)
    n = cur.shape[0]
    row = lax.broadcasted_iota(jnp.int32, cur.shape, 0)
    prev_row = jnp.concatenate([raw_q[n - HALO:, :], raw_k[n - HALO:, :]], axis=1)[HALO - 1:HALO, :].astype(F32)
    xm = jnp.where(row == 0, prev_row * prev_on, pltpu.roll(cur, 1, 0))
    xp = jnp.where(row == n - 1, next_row_scr[0:1, :] * next_on, pltpu.roll(cur, n - 1, 0))
    conv = cw_ref[3:4, :] + cw_ref[0:1, :] * xm + cw_ref[1:2, :] * cur + cw_ref[2:3, :] * xp
    qk = conv * jax.nn.sigmoid(conv)
    qk_ref[:, 0:gw] = (qk[:, 0:gw] * (HEAD_DIM ** -0.5)).astype(BF16)
    qk_ref[:, gw:] = qk[:, gw:].astype(BF16)
    le_bf = (sub <= lane).astype(BF16)
    chunks = (1, 0)
    pre = {}
    for blk in chunks:
        rows = slice(blk * CHUNK, (blk + 1) * CHUNK)
        g = p_g[rows, :]
        bal = pltpu.roll(_cumsum_cols(le_bf, _log_sigmoid(g)), LANES - 8, 1)
        c_all = g - bal
        cmax = jnp.max(c_all, axis=0, keepdims=True)
        w_all = jnp.exp(c_all - cmax)
        vts, ks, nlocs = [], [], []
        for p in range(8):
            sl = slice((p % 4) * LANES, (p % 4 + 1) * LANES)
            if p < 4:
                v2 = p_rv[rows, sl]
                ks.append((p_rk[rows, sl].astype(F32) * kdec_scr[p]).astype(BF16))
            else:
                v2 = p_mv[rows, sl]
                h0 = 16 + 2 * (p - 4)
                kw = qk[rows, gw + (p - 4) * LANES:gw + (p - 3) * LANES] * jnp.where(
                    lo, w_all[:, h0:h0 + 1], w_all[:, h0 + 1:h0 + 2])
                ks.append(kw.astype(BF16))
                nlocs.append(jnp.sum(kw, axis=0, keepdims=True))
            vts.append(v2.astype(F32).T.astype(BF16))
        pre[blk] = (vts, ks, nlocs, cmax, bal[0:1, :])

    cur_rk = project(1)
    y_ref[:, 1 * gw:2 * gw] = cur_rk
    y_ref[:, 0:gw] = project(0)
    kvs = {blk: [_dot(pre[blk][0][p], pre[blk][1][p]) for p in range(8)] for blk in chunks}
    cur_rv = project(2)
    y_ref[:, 2 * gw:3 * gw] = cur_rv
    y_ref[:, 3 * gw:4 * gw] = project(3)

    for blk in chunks:
        _, _, nlocs, cmax, bend = pre[blk]
        sret_ref[blk] = s_scr[...].astype(BF16)
        cm_ref[blk] = c_scr[...].astype(BF16)
        nm_ref[blk] = n_scr[...].astype(BF16)
        mm_ref[blk] = m_scr[...]
        m_old = m_scr[0:1, :]
        mrel = jnp.maximum(m_old, cmax)
        a_row = jnp.exp(m_old - mrel)
        bb_row = jnp.exp(cmax - mrel)
        m_scr[0:1, :] = bend + mrel
        for p in range(4):
            cd = jnp.exp(lgk_ref[4 + p:5 + p, :] * float(CHUNK))
            s_scr[p] = s_scr[p] * cd + jnp.where(bd_ret, kvs[blk][p], 0.0)
            h0 = 16 + 2 * p
            a_l = jnp.where(lo_row, a_row[:, h0:h0 + 1], a_row[:, h0 + 1:h0 + 2])
            bb_l = jnp.where(lo_row, bb_row[:, h0:h0 + 1], bb_row[:, h0 + 1:h0 + 2])
            c_scr[p] = c_scr[p] * a_l + jnp.where(bd_m, kvs[blk][4 + p], 0.0) * bb_l
            n_new = (n_scr[p, 0:1, :] + n_scr[p, 1:2, :]) * a_l + nlocs[p] * bb_l
            n_scr[p, 0:1, :] = jnp.where(lo_row, n_new, 0.0)
            n_scr[p, 1:2, :] = jnp.where(lo_row, 0.0, n_new)

    cur_mv = project(6)
    y_ref[:, 4 * gw:5 * gw] = cur_mv
    y_ref[:, 5 * gw:6 * gw] = project(7)
    gates = _dot(hb, wg_scr[...]) + gb_ref[...]
    g_ref[...] = gates
    gt_ref[...] = _dot_nt(wgt_scr[...], hb) + gbt_ref[...]
    next_row_scr[...] = p_mqk[0:HALO, :].astype(F32)
    p_mqk[:, 0:gw] = raw_q
    p_mqk[:, gw:] = raw_k
    p_rk[...] = cur_rk
    p_rv[...] = cur_rv
    p_mv[...] = cur_mv
    p_g[...] = gates


def _inproj_sweep(ctx2, x2, ab, w_in, gb, gbt, cos, sin, lgk, cw, ffn_w, nb, tpb):
    d = x2.shape[1]
    tm = ROW_TILE
    cpt = tm // CHUNK
    gw = 4 * LANES
    tiles = nb * tpb
    r = tiles * tm
    lat = tpb - 1
    def visit(t):
        b, v = _div_small(t, tpb, nb)
        return b, jnp.where(v == 0, 0, tpb - v)

    cur = lambda i: visit(jnp.minimum(i, tiles - 1))
    flat = lambda bj: bj[0] * tpb + bj[1]
    tile_i = lambda i: flat(cur(i))
    tile_a = lambda i: flat(visit(jnp.maximum(i - 1, 0)))
    in_batch = lambda i: cur(i)[1]
    sel = lambda i: (cur(i)[0] * 2 + jnp.minimum(cur(i)[1], 1), 0, 0)
    state = lambda *dims: pl.BlockSpec((cpt,) + dims, lambda i: (tile_a(i),) + (0,) * len(dims))
    nchunks = tiles * cpt
    wc_in, wc_out, wc_shapes = _wcast_specs(*ffn_w, 0, tiles + 1)
    return pl.pallas_call(
        functools.partial(_inproj_sweep_kernel, tiles=tiles, tpb=tpb),
        grid=(tiles + 1,),
        in_specs=[pl.BlockSpec((tm, d), lambda i: (cur(i)[0], 0)),
                  pl.BlockSpec((tm, d), lambda i: (cur(i)[0] * lat + jnp.maximum(cur(i)[1] - 1, 0), 0)),
                  pl.BlockSpec((None, 2, d), sel),
                  pl.BlockSpec((None, 8 * gw, d), lambda i: (0, 0, 0), pipeline_mode=pl.Buffered(1)),
                  pl.BlockSpec((None, LANES, d), lambda i: (0, 8 * gw // LANES, 0), pipeline_mode=pl.Buffered(1)),
                  _const_spec(gb.shape), _const_spec(gbt.shape),
                  pl.BlockSpec((tm, LANES), lambda i: (in_batch(i), 0)),
                  pl.BlockSpec((tm, LANES), lambda i: (in_batch(i), 0)),
                  _const_spec(lgk.shape), _const_spec(cw.shape)] + wc_in,
        out_specs=[pl.BlockSpec((tm, 6 * gw), lambda i: (tile_i(i), 0)),
                   pl.BlockSpec((tm, LANES), lambda i: (tile_i(i), 0)),
                   pl.BlockSpec((32, tm), lambda i: (0, tile_i(i))),
                   state(4, CHUNK, LANES), state(4, CHUNK, LANES), state(4, HALO, LANES), state(8, LANES),
                   pl.BlockSpec((tm, 2 * gw), lambda i: (tile_a(i), 0))] + wc_out,
        out_shape=[jax.ShapeDtypeStruct((r, 6 * gw), BF16),
                   jax.ShapeDtypeStruct((r, LANES), F32),
                   jax.ShapeDtypeStruct((32, r), F32),
                   jax.ShapeDtypeStruct((nchunks, 4, CHUNK, LANES), BF16),
                   jax.ShapeDtypeStruct((nchunks, 4, CHUNK, LANES), BF16),
                   jax.ShapeDtypeStruct((nchunks, 4, HALO, LANES), BF16),
                   jax.ShapeDtypeStruct((nchunks, 8, LANES), F32),
                   jax.ShapeDtypeStruct((r, 2 * gw), BF16)] + wc_shapes,
        scratch_shapes=[pltpu.VMEM((4, CHUNK, LANES), F32), pltpu.VMEM((4, CHUNK, LANES), F32),
                        pltpu.VMEM((4, HALO, LANES), F32), pltpu.VMEM((8, LANES), F32),
                        pltpu.VMEM((4, CHUNK, LANES), F32),
                        pltpu.VMEM((tm, gw), BF16), pltpu.VMEM((tm, gw), BF16), pltpu.VMEM((tm, gw), BF16),
                        pltpu.VMEM((tm, 2 * gw), BF16), pltpu.VMEM((tm, LANES), F32),
                        pltpu.VMEM((HALO, 2 * gw), F32),
                        pltpu.VMEM((d, 8 * gw), BF16), pltpu.VMEM((d, LANES), BF16), pltpu.VMEM((32, d), BF16)],
        compiler_params=_cparams("arbitrary"),
        name="inproj_bwd_sweep",
    )(ctx2, x2, ab, w_in, w_in, gb, gbt, cos, sin, lgk, cw, *ffn_w)


class _Bag:
    def __init__(self, **kw):
        self.__dict__.update(kw)


def _mixer_chunk_stages(blk, r):
    rows = slice(blk * CHUNK, (blk + 1) * CHUNK)
    lane = _lane()
    sub = lax.broadcasted_iota(jnp.int32, (CHUNK, LANES), 0)
    lo = lane < HEAD_DIM
    sub_lo = sub < HEAD_DIM
    mask_ret = [(lane & 32) == 0, (lane & 32) != 0]
    mask_nat = [lo, lane >= HEAD_DIM]
    bd_ret = (sub >= HEAD_DIM) == ((lane & 32) != 0)
    bd_m = (sub >= HEAD_DIM) == (lane >= HEAD_DIM)
    le = sub <= lane
    ge = sub >= lane

    g = r.g_ref[rows, :]
    gt = r.gt_ref[:, rows]
    lf_col = _log_sigmoid(g)
    lf_row = _log_sigmoid(gt)
    le_bf = le.astype(BF16)
    ge_bf = ge.astype(BF16)
    pre_col = _cumsum_cols(ge_bf, lf_col)
    bal_f = pltpu.roll(pre_col, LANES - 8, 1)
    bal_b = pltpu.roll(pre_col[CHUNK - 1:CHUNK, :] - pre_col + lf_col, LANES - 8, 1)
    cf_all = g - bal_f
    cb_all = g - bal_b
    bf_row = _cumsum_rows(lf_row, le_bf)
    bb_row = bf_row[:, CHUNK - 1:CHUNK] - bf_row + lf_row
    qb, kb, vts, kf32, qt, vbd = [], [], [], [], [], []
    for p in range(8):
        sl = slice((p % 4) * LANES, (p % 4 + 1) * LANES)
        if p < 4:
            q2, k2, v2 = r.rq_ref[rows, sl], r.rk_ref[rows, sl], r.rv_ref[rows, sl]
            qf = q2.astype(F32)
            kf = None
        else:
            q2 = r.qk_ref[rows, sl]
            k2 = r.qk_ref[rows, 4 * LANES + (p - 4) * LANES:4 * LANES + (p - 3) * LANES]
            v2 = r.mv_ref[rows, sl]
            qf, kf = q2.astype(F32), k2.astype(F32)
        vt = v2.astype(F32).T.astype(BF16)
        qb.append(q2)
        kb.append(k2)
        vts.append(vt)
        kf32.append(kf)
        qt.append(qf.T)
        vbd.append([jnp.where(sub_lo, vt, jnp.zeros_like(vt)), jnp.where(sub_lo, jnp.zeros_like(vt), vt)])
    yield

    st2 = []
    for p in range(8):
        masks = mask_ret if p < 4 else mask_nat
        zero = jnp.zeros_like(qb[p])
        qstack = jnp.concatenate([jnp.where(masks[0], qb[p], zero), jnp.where(masks[1], qb[p], zero)], axis=0)
        st2.append(_dot_nt(kb[p], qstack))
    yield

    qn = [_dot_nt(jnp.concatenate([r.n_scr[p].astype(BF16), r.nmb_ref[blk, p]], axis=0), qb[4 + p])
          for p in range(4)]
    qn_f = [x[0:HALO] for x in qn]
    qn_b = [x[HALO:2 * HALO] for x in qn]
    cf2, cb2, bf2, bb2 = cf_all * LOG2E, cb_all * LOG2E, bf_row * LOG2E, bb_row * LOG2E
    m_f2, m_b2 = r.m_state[0:1, :] * LOG2E, r.mmb_ref[blk, 0:1, :] * LOG2E
    lhs, rhs = [], []
    for p in range(8):
        if p < 4:
            pts = [(st2[p][:, a * LANES:(a + 1) * LANES] * r.dm_scr[2 * p + a]).astype(BF16) for a in range(2)]
            x_f, x_b = r.dec_scr[0, p], r.dec_scr[1, p]
            old = [r.s_scr[p].astype(BF16), r.sretb_ref[blk, p]]
        else:
            pts, cf, cb = [], [], []
            for a in range(2):
                h = 2 * (p - 4) + a
                st = st2[p][:, a * LANES:(a + 1) * LANES]
                pf, coef_f = _mlstm_dir_weights(st, qn_f[p - 4][a:a + 1, :], cf2[:, h:h + 1],
                                                bf2[8 + h:9 + h, :], m_f2[0:1, h:h + 1], le)
                pb, coef_b = _mlstm_dir_weights(st, qn_b[p - 4][a:a + 1, :], cb2[:, 16 + h:17 + h],
                                                bb2[24 + h:25 + h, :], m_b2[0:1, 16 + h:17 + h], ge)
                pts.append((pf + pb).astype(BF16))
                cf.append(coef_f)
                cb.append(coef_b)
            x_f = jnp.where(sub_lo, cf[0], cf[1])
            x_b = jnp.where(sub_lo, cb[0], cb[1])
            old = [r.c_scr[p - 4].astype(BF16), r.cmb_ref[blk, p - 4]]
        lhs.append(jnp.concatenate(vbd[p] + old, axis=1))
        rhs.append(jnp.concatenate(pts + [(qt[p] * x_f).astype(BF16), (qt[p] * x_b).astype(BF16)], axis=0))
    yield

    ht = [_dot(lhs[p], rhs[p]) for p in range(8)]
    yield

    for p in range(8):
        sl = slice((p % 4) * LANES, (p % 4 + 1) * LANES)
        gate_ref = r.rg_ref if p < 4 else r.mo_ref
        y = _heads_out(ht[p], r.nw_ref[p])
        new = (y * gate_ref[rows, sl].astype(F32)).astype(BF16)
        r.mix_scr[rows, p * LANES:(p + 1) * LANES] = jnp.where(r.live, new, r.mix_scr[rows, p * LANES:(p + 1) * LANES])
    yield

    for p in range(4):
        cd = jnp.exp(r.lgk_ref[p:p + 1, :] * float(CHUNK))
        _ret_state_update(r.s_scr, p, kb[p], vts[p], r.dec_scr[2, p], cd, bd_ret)
    _mlstm_state_update(r.c_scr, r.n_scr, r.m_state, kf32[4:], vts[4:], cf_all, bal_f[CHUNK - 1:CHUNK, :],
                        0, lo, bd_m)
    yield


def _attn_inproj_stages(x, ab_ref, wq_ref, wk_ref, wv_ref, nw_ref, cos, sin, y_ref):
    hb = _norm_mod(x, ab_ref)
    r2 = lax.broadcasted_iota(jnp.int32, (2 * LANES, 2 * LANES), 0)
    c2 = lax.broadcasted_iota(jnp.int32, (2 * LANES, 2 * LANES), 1)
    same_head = (((r2 ^ c2) & (LANES | 32)) == 0).astype(BF16)
    acc_q = _dot(hb, wq_ref[...])
    acc_k = _dot(hb, wk_ref[...])
    y_ref[:, 10 * LANES:12 * LANES] = _dot(hb, wv_ref[...]).astype(BF16)
    yield
    for j in range(5):
        acc = acc_q[:, j * 2 * LANES:(j + 1) * 2 * LANES] if j < 4 else acc_k
        sq = acc * acc
        ms = _dot(sq.astype(BF16), same_head) * (1.0 / HEAD_DIM)
        nrm = acc * lax.rsqrt(ms + EPS)
        nw = nw_ref[0:1, :] if j < 4 else nw_ref[1:2, :]
        for v in range(2):
            ls = slice(v * LANES, (v + 1) * LANES)
            y_ref[:, (2 * j + v) * LANES:(2 * j + v + 1) * LANES] = _rope(nrm[:, ls] * nw, cos, sin).astype(BF16)
        if j in (1, 4):
            yield


def _ffn_splits(d_ff, pieces):
    blocks = d_ff // (2 * LANES)
    assert blocks * 2 * LANES == d_ff and blocks >= pieces
    cuts = [((i * blocks) // pieces) * 2 * LANES for i in range(pieces)]
    return cuts + [d_ff]


def _prepare_attn_weights(wa_src, wa_scr):
    rows = wa_src.shape[0]
    d = wa_src.shape[1] - 4 * LANES
    quarter = _lane((rows, LANES)) // 32
    group = lambda ref, grp: ref[:, grp * LANES:(grp + 1) * LANES]
    roll = lambda v, shift: pltpu.roll(v, shift % LANES, 1) if shift % LANES else v
    g_per = d // (H_KV * HEAD_DIM)
    for pair in range(H_KV // 2):
        for g in range(g_per):
            h0 = (g * HEAD_DIM) % LANES
            src = lambda kv: group(wa_src, (kv * g_per * HEAD_DIM + g * HEAD_DIM) // LANES)
            head_a, head_b = src(2 * pair), src(2 * pair + 1)
            out = jnp.where(quarter == 0, roll(head_a, -h0),
                            jnp.where(quarter == 1, roll(head_b, 32 - h0),
                                      jnp.where(quarter == 2, roll(head_a, 32 - h0), roll(head_b, 64 - h0))))
            wa_scr[:, (pair * g_per + g) * LANES:(pair * g_per + g + 1) * LANES] = out.astype(BF16)
    for grp in range(d // LANES, d // LANES + H_KV // 2):
        w = group(wa_src, grp)
        out = jnp.where(quarter == 1, roll(w, -32), jnp.where(quarter == 2, roll(w, 32), w))
        wa_scr[:, grp * LANES:(grp + 1) * LANES] = out.astype(BF16)
    wa_scr[:, d + 2 * LANES:] = wa_src[:, d + 2 * LANES:].astype(BF16)


def _mixer_ffn_kernel(lg_ref, y_ref, qk_ref,
                      g_ref, gt_ref, sretb_ref, cmb_ref, nmb_ref, mmb_ref,
                      lgk_ref, lgkt_ref, nw_ref,
                      ctx_ref, x_ref, p_ref, wo_ref, wi_ref, w2_ref,
                      ab2_ref, wa_src, nwa_ref, cos_ref, sin_ref, wi_src, w2_src,
                      o_ref, y2_ref, wi_dst, w2_dst,
                      s_scr, c_scr, n_scr, m_state, dm_scr, dec_scr, mix_scr, act_scr, x2_scr, wa_scr,
                      *, tiles, tpb, d_ff):
    s = pl.program_id(0)
    _wcast((wi_src, w2_src), (wi_dst, w2_dst))
    nb = tiles // tpb
    jt = _div_small(jnp.minimum(s, tiles - 1), tpb, nb)[1]
    gw = 4 * LANES
    rq_ref, rk_ref, rv_ref, rg_ref, mv_ref, mo_ref = [y_ref.at[:, j * gw:(j + 1) * gw] for j in range(6)]
    lane = _lane()
    sub = lax.broadcasted_iota(jnp.int32, (CHUNK, LANES), 0)

    @pl.when(s == 0)
    def _():
        mix_scr[...] = jnp.zeros_like(mix_scr)
        x2_scr[...] = jnp.zeros_like(x2_scr)
        le = sub <= lane
        ge = sub >= lane
        spos = sub.astype(F32)
        tpos = lane.astype(F32)
        diff = (lane - sub).astype(F32)
        for h in range(8):
            dm_scr[h] = (jnp.where(le, jnp.exp(lg_ref[h] * diff), 0.0)
                         + jnp.where(ge, jnp.exp(lg_ref[8 + h] * (-diff)), 0.0))
        for p in range(4):
            dec_scr[0, p] = jnp.exp(lgkt_ref[:, p:p + 1] * (tpos + 1.0))
            dec_scr[1, p] = jnp.exp(lgkt_ref[:, 4 + p:5 + p] * (float(CHUNK) - tpos))
            dec_scr[2, p] = jnp.exp(lgk_ref[p:p + 1, :] * (float(CHUNK) - 1.0 - spos))
        _prepare_attn_weights(wa_src, wa_scr)

    @pl.when(jt == 0)
    def _():
        s_scr[...] = jnp.zeros_like(s_scr)
        c_scr[...] = jnp.zeros_like(c_scr)
        n_scr[...] = jnp.zeros_like(n_scr)
        m_state[...] = jnp.zeros_like(m_state)

    r = _Bag(rq_ref=rq_ref, rk_ref=rk_ref, rv_ref=rv_ref, rg_ref=rg_ref, qk_ref=qk_ref, mv_ref=mv_ref, mo_ref=mo_ref,
             g_ref=g_ref, gt_ref=gt_ref, sretb_ref=sretb_ref, cmb_ref=cmb_ref, nmb_ref=nmb_ref, mmb_ref=mmb_ref,
             lgk_ref=lgk_ref, nw_ref=nw_ref, s_scr=s_scr, c_scr=c_scr, n_scr=n_scr, m_state=m_state,
             dm_scr=dm_scr, dec_scr=dec_scr, mix_scr=mix_scr, live=s < tiles)
    cuts = _ffn_splits(d_ff, 3)
    ffn_piece = lambda i: _ffn_cols(h, wi_ref, act_scr, cuts[i], cuts[i + 1], d_ff)

    x = _ctx_or_latent_rows(ctx_ref, x_ref, jnp.clip(s - 1, 0, tiles - 1), nb, tpb)
    x1, h = _ffn_in(x, mix_scr[...], p_ref, wo_ref)

    d = x2_scr.shape[1]
    nxt = _attn_inproj_stages(x2_scr[...], ab2_ref, wa_scr.at[:, 0:d], wa_scr.at[:, d:d + 2 * LANES],
                              wa_scr.at[:, d + 2 * LANES:], nwa_ref, cos_ref[...], sin_ref[...], y2_ref)
    chunk_a, chunk_b = _mixer_chunk_stages(0, r), _mixer_chunk_stages(1, r)
    next(nxt)
    next(chunk_a), next(chunk_b)
    next(chunk_a), next(chunk_b)
    ffn_piece(0)
    next(chunk_a), next(chunk_a)
    ffn_piece(1)
    next(chunk_a), next(chunk_a)
    next(nxt)
    ffn_piece(2)
    next(chunk_b), next(chunk_b)
    next(nxt)
    x2 = x1 + p_ref[3:4, :] * _dot(act_scr[...], w2_ref[...])
    o_ref[...] = x2
    next(chunk_b), next(chunk_b)
    x2_scr[...] = x2


def _mixer_ffn(y, g, gt, states, lg_smem, lgk, lgkt, nw, xs, prm, wo, wi, w2,
               ab_next, w_next, nw_next, cos, sin, ffn_w, nb, tpb):
    gw = 4 * LANES
    tm = ROW_TILE
    cpt = tm // CHUNK
    sretb, cmb, nmb, mmb, qk_act = states
    tiles = nb * tpb
    d = xs[-1].shape[1]
    d_ff = w2.shape[0]
    n_next = w_next.shape[-1]
    mix = lambda s: jnp.minimum(s, tiles - 1)
    ffn = lambda s: jnp.clip(s - 1, 0, tiles - 1)
    nxt = lambda s: jnp.maximum(s - 2, 0)
    state_spec = lambda a: pl.BlockSpec((cpt,) + a.shape[1:], lambda s: (mix(s),) + (0,) * (a.ndim - 1))
    def sel(t):
        b, j = _div_small(t, tpb, nb)
        return b * 2 + jnp.minimum(j, 1), 0, 0

    wc_in, wc_out, wc_shapes = _wcast_specs(*ffn_w, 1, tiles + 2)
    return pl.pallas_call(
        functools.partial(_mixer_ffn_kernel, tiles=tiles, tpb=tpb, d_ff=d_ff),
        grid=(tiles + 2,),
        in_specs=[_smem_spec(),
                  pl.BlockSpec((tm, 6 * gw), lambda s: (mix(s), 0)),
                  pl.BlockSpec((tm, 2 * gw), lambda s: (mix(s), 0)),
                  pl.BlockSpec((tm, LANES), lambda s: (mix(s), 0)),
                  pl.BlockSpec((32, tm), lambda s: (0, mix(s))),
                  state_spec(sretb), state_spec(cmb), state_spec(nmb), state_spec(mmb),
                  _const_spec(lgk.shape), _const_spec(lgkt.shape), _const_spec(nw.shape)]
                 + _split_row_specs(tm, d, nb, tpb, ffn)
                 + [pl.BlockSpec((None, 8, d), lambda s: sel(ffn(s))),
                    _const_spec(wo.shape), _const_spec(wi.shape), _const_spec(w2.shape),
                    pl.BlockSpec((None, 2, d), lambda s: sel(nxt(s))),
                    pl.BlockSpec((None,) + w_next.shape[1:], lambda s: (0, 0, 0), pipeline_mode=pl.Buffered(1)),
                    _const_spec(nw_next.shape),
                    pl.BlockSpec((tm, LANES), lambda s: (_div_small(nxt(s), tpb, nb)[1], 0)),
                    pl.BlockSpec((tm, LANES), lambda s: (_div_small(nxt(s), tpb, nb)[1], 0))] + wc_in,
        out_specs=[pl.BlockSpec((tm, d), lambda s: (ffn(s), 0)),
                   pl.BlockSpec((tm, n_next), lambda s: (nxt(s), 0))] + wc_out,
        out_shape=[jax.ShapeDtypeStruct((tiles * tm, d), F32),
                   jax.ShapeDtypeStruct((tiles * tm, n_next), BF16)] + wc_shapes,
        scratch_shapes=[pltpu.VMEM((4, CHUNK, LANES), F32), pltpu.VMEM((4, CHUNK, LANES), F32),
                        pltpu.VMEM((4, HALO, LANES), F32), pltpu.VMEM((8, LANES), F32),
                        pltpu.VMEM((8, CHUNK, LANES), F32), pltpu.VMEM((3, 4, CHUNK, LANES), F32),
                        pltpu.VMEM((tm, 2 * gw), BF16), pltpu.VMEM((tm, d_ff), BF16),
                        pltpu.VMEM((tm, d), F32),
                        pltpu.VMEM(w_next.shape[1:], BF16)],
        compiler_params=_cparams("arbitrary"),
        name="ret_mlstm_mixer_ffn",
    )(lg_smem, y, qk_act, g, gt, sretb, cmb, nmb, mmb, lgk, lgkt, nw,
      *xs, prm, wo, wi, w2, ab_next, w_next, nw_next, cos, sin, *ffn_w)


def _ffn_in(x, m, p_ref, wo_ref):
    x1 = x + p_ref[0:1, :] * _dot(m, wo_ref[...])
    ms = jnp.mean(x1 * x1, axis=-1, keepdims=True)
    return x1, ((x1 * lax.rsqrt(ms + EPS)) * p_ref[1:2, :] + p_ref[2:3, :]).astype(BF16)


def _ffn_cols(h, wi_ref, act_scr, lo, hi, d_ff):
    gate = _dot(h, wi_ref[:, lo:hi])
    up = _dot(h, wi_ref[:, d_ff + lo:d_ff + hi])
    act_scr[:, lo:hi] = (gate * jax.nn.sigmoid(gate) * up).astype(BF16)


def _window_bias():
    kk = np.arange(CHUNK)[:, None]
    t = np.arange(CHUNK)[None, :]
    tabs = []
    for has_prev, has_next in ((False, True), (True, True), (True, False)):
        prev_ok = (kk >= t) & has_prev
        next_ok = (kk <= t) & has_next
        tabs.append(np.where(np.concatenate([prev_ok, next_ok], axis=0), 0.0, NEG))
    return jnp.asarray(np.stack(tabs), F32)


def _attn_ffn_kernel(sink_ref, qkv_a_ref, qkv_b_ref, kvp_ref, kvn_ref, kvx_ref, bias_ref,
                     x_a_ref, x_b_ref, p_ref, wo_ref, wi_ref, w2_ref, o_ref, m_scr, act_scr,
                     *, steps, per_batch, d_ff):
    s = pl.program_id(0)
    kw = 2 * LANES
    k0 = 8 * LANES
    tm = qkv_a_ref.shape[0]
    nblk = 2 * tm // CHUNK

    @pl.when(s == 0)
    def _():
        m_scr[...] = jnp.zeros_like(m_scr)

    j = _div_small(jnp.minimum(s, steps - 1), per_batch, steps // per_batch)[1]
    grp = lax.broadcasted_iota(jnp.int32, (1, 4 * CHUNK), 1) // CHUNK
    lane = _lane()
    mask_q = [(lane & 32) == 0, (lane & 32) != 0]
    tile4 = lambda b: jnp.concatenate([b] * 4, axis=1)
    inner = tile4(bias_ref[1])
    biases = ([tile4(jnp.where(j == 0, bias_ref[0], bias_ref[1]))] + [inner] * (nblk - 2)
              + [tile4(jnp.where(j == per_batch - 1, bias_ref[2], bias_ref[1]))])
    cuts = _ffn_splits(d_ff, nblk - 1)

    def ffn_up(i):
        _ffn_cols(h, wi_ref, act_scr, cuts[i], cuts[i + 1], d_ff)

    def q_rows(blk):
        ref = qkv_a_ref if blk * CHUNK < tm else qkv_b_ref
        r0 = (blk * CHUNK) % tm
        return ref, slice(r0, r0 + CHUNK)

    def scores(blk, kv):
        kvp, a = kv // 2, kv % 2
        ref, rows = q_rows(blk)
        qs = jnp.concatenate(
            [jnp.where(mask_q[a], ref[rows, (kvp * 4 + g) * LANES:(kvp * 4 + g + 1) * LANES],
                       jnp.zeros((CHUNK, LANES), BF16)) for g in range(4)], axis=0)
        return _dot_nt(kcats[kvp][blk], qs)

    def softmax_pv(blk, kv, st):
        bias = biases[blk]
        st = jnp.concatenate([st[0:CHUNK] + bias[0:CHUNK], st[CHUNK:2 * CHUNK],
                              st[2 * CHUNK:3 * CHUNK] + bias[CHUNK:2 * CHUNK], st[3 * CHUNK:]], axis=0)
        snk = jnp.where(grp == 0, sink_ref[kv * 4],
                        jnp.where(grp == 1, sink_ref[kv * 4 + 1],
                                  jnp.where(grp == 2, sink_ref[kv * 4 + 2], sink_ref[kv * 4 + 3])))
        m = jnp.maximum(jnp.max(st, axis=0, keepdims=True), snk)
        e = jnp.exp2(st - m)
        denom = jnp.exp2(snk - m) + jnp.sum(e, axis=0, keepdims=True)
        a = kv % 2
        return _dot(vts[kv // 2][blk][a * HEAD_DIM:(a + 1) * HEAD_DIM, :], e.astype(BF16)) * (1.0 / denom)

    def hand_over(blk, outs):
        for kvp in range(2):
            full = jnp.concatenate(outs[2 * kvp:2 * kvp + 2], axis=0)
            for g in range(4):
                m_scr[blk * CHUNK:(blk + 1) * CHUNK, (kvp * 4 + g) * LANES:(kvp * 4 + g + 1) * LANES] = (
                    full[:, g * CHUNK:(g + 1) * CHUNK].T.astype(BF16))

    x1, h = _ffn_in(jnp.concatenate([x_a_ref[...], x_b_ref[...]], axis=0), m_scr[...], p_ref, wo_ref)

    kcats, vts = [], []
    for kvp in range(2):
        ks = slice(k0 + kvp * LANES, k0 + (kvp + 1) * LANES)
        vs = slice(k0 + kw + kvp * LANES, k0 + kw + (kvp + 1) * LANES)
        es, ev = slice(kvp * LANES, (kvp + 1) * LANES), slice(kw + kvp * LANES, kw + (kvp + 1) * LANES)
        tile_rows = [(ref, slice(c * CHUNK, (c + 1) * CHUNK)) for ref in (qkv_a_ref, qkv_b_ref)
                     for c in range(tm // CHUNK)]
        k_chunks = [kvp_ref[:, es]] + [ref[rows, ks] for ref, rows in tile_rows] + [kvn_ref[:, es]]
        v_chunks = ([kvp_ref[:, ev]] + [ref[rows, vs] for ref, rows in tile_rows] + [kvn_ref[:, ev]]
                    + [kvx_ref[c * CHUNK:(c + 1) * CHUNK, ev] for c in range(kvx_ref.shape[0] // CHUNK)])
        v_t = [v.astype(F32).T.astype(BF16) for v in v_chunks]
        kcats.append([jnp.concatenate(k_chunks[b:b + 3] + [kvx_ref[:, es]], axis=0) for b in range(nblk)])
        vts.append([jnp.concatenate(v_t[b:b + 3] + v_t[nblk + 2:], axis=1) for b in range(nblk)])

    for blk in range(nblk):
        sts = [scores(blk, kv) for kv in range(4)]
        if blk < nblk - 1:
            ffn_up(blk)
        else:
            o_ref[...] = x1 + p_ref[3:4, :] * _dot(act_scr[...], w2_ref[...])
        hand_over(blk, [softmax_pv(blk, kv, sts[kv]) for kv in range(4)])


def _attn_ffn(y, sink, xc, prm, wo, wi, w2, nb, nc, ctx_chunks):
    tm = ROW_TILE
    cpt = tm // CHUNK
    tpb = nc // cpt
    lat_tiles = (nc - ctx_chunks) // cpt
    ctx_tiles = ctx_chunks // cpt
    assert ctx_tiles == 1 and lat_tiles % 2 == 0 and lat_tiles >= 4
    per_batch = lat_tiles // 2
    steps = nb * per_batch
    d = xc.shape[1]
    d_ff = w2.shape[0]
    bias = _window_bias()
    n_qkv = y.shape[1]
    kv_w = n_qkv - d
    assert d % kv_w == 0
    kv_col = d // kv_w
    att = lambda s: jnp.minimum(s, steps - 1)
    ffn = lambda s: jnp.maximum(s - 1, 0)
    split = lambda t: _div_small(t, per_batch, nb)
    row_tile = lambda t, u: split(t)[0] * tpb + ctx_tiles + 2 * split(t)[1] + u
    chunk0 = lambda t: split(t)[0] * nc + ctx_chunks
    prev_c = lambda s: chunk0(att(s)) + jnp.maximum(split(att(s))[1] * 2 * cpt - 1, 0)
    next_c = lambda s: chunk0(att(s)) + jnp.minimum(split(att(s))[1] * 2 * cpt + 2 * cpt, lat_tiles * cpt - 1)
    edge_spec = lambda f: pl.BlockSpec((CHUNK, kv_w), lambda s: (f(s), kv_col))
    return pl.pallas_call(
        functools.partial(_attn_ffn_kernel, steps=steps, per_batch=per_batch, d_ff=d_ff),
        grid=(steps + 1,),
        in_specs=[_smem_spec(),
                  pl.BlockSpec((tm, n_qkv), lambda s: (row_tile(att(s), 0), 0)),
                  pl.BlockSpec((tm, n_qkv), lambda s: (row_tile(att(s), 1), 0)),
                  edge_spec(prev_c), edge_spec(next_c),
                  pl.BlockSpec((tm, kv_w), lambda s: (split(att(s))[0] * tpb, kv_col)),
                  _const_spec(bias.shape),
                  pl.BlockSpec((tm, d), lambda s: (row_tile(ffn(s), 0), 0)),
                  pl.BlockSpec((tm, d), lambda s: (row_tile(ffn(s), 1), 0)),
                  pl.BlockSpec((None, 8, d), lambda s: (split(ffn(s))[0] * 2 + 1, 0, 0)),
                  _const_spec(wo.shape), _const_spec(wi.shape), _const_spec(w2.shape)],
        out_specs=pl.BlockSpec((2 * tm, d), lambda s: (ffn(s), 0)),
        out_shape=jax.ShapeDtypeStruct((steps * 2 * tm, d), F32),
        scratch_shapes=[pltpu.VMEM((2 * tm, d), BF16), pltpu.VMEM((2 * tm, d_ff), BF16)],
        compiler_params=_cparams("arbitrary"),
        name="window_gqa_ffn",
    )(sink, y, y, y, y, y, bias, xc, xc, prm, wo, wi, w2)


def _attn_o_rows(w):
    cols = w.shape[1]
    g_per = w.shape[0] // (H_KV * HEAD_DIM)
    return (w.reshape(H_KV // 2, 2, g_per, HEAD_DIM, cols).transpose(0, 2, 1, 3, 4)
            .reshape(w.shape[0], cols))


def _rope_tables(seq, ctx_len):
    rows = seq // GRID_W
    row = np.repeat(np.arange(rows, dtype=np.float32), GRID_W)
    col = np.tile(np.arange(GRID_W, dtype=np.float32), rows)
    n = HEAD_DIM // 4
    inv = (np.float32(ROPE_BASE) ** (-np.arange(n, dtype=np.float32) / np.float32(n))).astype(np.float32)
    ang = np.concatenate([row[:, None] * inv, col[:, None] * inv], axis=-1).astype(np.float32)
    cos, sin = np.cos(ang), np.sin(ang)
    cos_t = np.concatenate([np.ones((ctx_len, LANES), np.float32), np.tile(cos, (1, 4))], axis=0)
    sin_t = np.concatenate([np.zeros((ctx_len, LANES), np.float32),
                            np.concatenate([-sin, -sin, sin, sin], axis=-1)], axis=0)
    return jnp.asarray(cos_t, F32), jnp.asarray(sin_t, F32)


def _mod_tables(mod, nb, norm_w):
    d = norm_w.shape[-1]
    lat = mod[:nb].reshape(nb, 6, d)
    ctx = jnp.broadcast_to(mod[nb].reshape(1, 6, d), (nb, 6, d))
    both = jnp.stack([ctx, lat], axis=1).reshape(nb * 2, 6, d)
    sh1, sc1, g1, sh2, sc2, g2 = [both[:, k] for k in range(6)]
    ab1 = jnp.stack([norm_w[0] * (1.0 + sc1), sh1], axis=1)
    zeros = jnp.zeros_like(g1)
    prm = jnp.stack([g1, norm_w[1] * (1.0 + sc2), sh2, g2, zeros, zeros, zeros, zeros], axis=1)
    return ab1, prm


def kernel(x, c, ctx, c_ctx, ada_w, ada_b, norm_w, ffn_w_in, ffn_w_out, ab_w_in, ab_w_out,
           ret_log_gamma, ret_norm_w, mlstm_conv_w, mlstm_conv_b, mlstm_gate_b, mlstm_norm_w,
           attn_w_in, attn_w_out, attn_q_norm_w, attn_k_norm_w, attn_sink):
    nb, seq, d = x.shape
    ctx_len = ctx.shape[1]
    depth = ada_w.shape[0]
    assert ctx_len == ROW_TILE and seq % ROW_TILE == 0 and d == 8 * LANES and nb < 8
    t_all = ctx_len + seq
    nc = t_all // CHUNK
    ctx_chunks = ctx_len // CHUNK
    tpb = t_all // ROW_TILE
    dr = d // 2

    rows = jnp.zeros((8, d), F32).at[:nb].set(c).at[nb].set(c_ctx)
    mod_all = _modulation(rows, ada_w, ada_b)
    cos_t, sin_t = _rope_tables(seq, ctx_len)
    ffn_w = (ffn_w_in, ffn_w_out)
    xs = (ctx.reshape(nb * ctx_len, d), x.reshape(nb * seq, d))

    assert depth == 2 and ab_w_in.shape[0] == 1 and attn_w_in.shape[0] == 1
    ab_0, prm_0 = _mod_tables(mod_all[0], nb, norm_w[0])
    ab_1, prm_1 = _mod_tables(mod_all[1], nb, norm_w[1])

    assert ab_w_in.shape[2] == 8 * dr + 32
    w_in_t = jnp.swapaxes(ab_w_in, 1, 2)
    gb = jnp.zeros((1, LANES), F32).at[0, :32].set(mlstm_gate_b[0].reshape(-1))
    gbt = mlstm_gate_b[0].reshape(32, 1)
    lg = ret_log_gamma[0].astype(F32)
    lgk = jnp.tile(jnp.repeat(lg.reshape(2, 4, 2), 32, axis=-1), (1, 1, 2)).reshape(8, LANES)
    cw = jnp.concatenate([mlstm_conv_w[0], mlstm_conv_b[0][None], jnp.zeros((4, 2 * dr), F32)], axis=0)
    nw = jnp.broadcast_to(jnp.concatenate([ret_norm_w[0], mlstm_norm_w[0]]).reshape(8, LANES, 1),
                          (8, LANES, LANES))
    wo_0 = ab_w_out[0].astype(BF16)

    lane_w = lambda v: jnp.concatenate([v[:32], v[:32], v[32:], v[32:]])
    nwq = jnp.stack([lane_w(attn_q_norm_w[0]) * (HEAD_DIM ** -0.5 * LOG2E), lane_w(attn_k_norm_w[0])]
                    + [jnp.zeros((LANES,), F32)] * 6)
    wo_1 = _attn_o_rows(attn_w_out[0]).astype(BF16)

    y, g, gt, *states, wi_0, w2_0 = _inproj_sweep(*xs, ab_0, w_in_t, gb, gbt, cos_t, sin_t, lgk, cw,
                                                  ffn_w, nb, tpb)
    x_mid, y_attn, wi_1, w2_1 = _mixer_ffn(y, g, gt, states, lg.reshape(-1), lgk, lgk.T, nw, xs, prm_0, wo_0,
                                           wi_0, w2_0, ab_1, attn_w_in, nwq, cos_t, sin_t, ffn_w, nb, tpb)
    out = _attn_ffn(y_attn, attn_sink[0].astype(F32) * LOG2E, x_mid, prm_1, wo_1, wi_1, w2_1, nb, nc, ctx_chunks)
    return out.reshape(nb, seq, d)
```

```python
import functools

import numpy as np
import jax
import jax.numpy as jnp
from jax import lax
from jax.experimental import pallas as pl
from jax.experimental.pallas import tpu as pltpu

F32 = jnp.float32
BF16 = jnp.bfloat16

HEAD_DIM = 64
CHUNK = 128
GRID_W = 64
ROPE_BASE = 10000.0
EPS = 1e-6
H_KV = 4
LANES = 128
ROW_TILE = 256
HALO = 16
LOG2E = 1.4426950408889634
NEG = -1e30
VMEM_LIMIT = 58 * 1024 * 1024


def _cparams(*sem):
    return pltpu.CompilerParams(dimension_semantics=sem, vmem_limit_bytes=VMEM_LIMIT)


def _const_spec(shape):
    nd = len(shape)
    return pl.BlockSpec(shape, lambda *_: (0,) * nd, pipeline_mode=pl.Buffered(1))


def _smem_spec():
    return pl.BlockSpec(memory_space=pltpu.SMEM)


def _wcast_specs(w_in_all, w_out_all, layer, steps):
    specs_in, specs_out, shapes = [], [], []
    for w in (w_in_all, w_out_all):
        total, cols = w.shape[1:]
        rows = next(r for r in range(HALO, total + 1, HALO) if total % r == 0 and total // r <= steps)
        n_blocks = total // rows
        specs_in.append(pl.BlockSpec((None, rows, cols),
                                     lambda i, n=n_blocks: (layer, jnp.minimum(i, n - 1), 0)))
        specs_out.append(pl.BlockSpec((rows, cols), lambda i, n=n_blocks: (jnp.minimum(i, n - 1), 0)))
        shapes.append(jax.ShapeDtypeStruct(w.shape[1:], BF16))
    return specs_in, specs_out, shapes


def _wcast(src_refs, dst_refs):
    for src, dst in zip(src_refs, dst_refs):
        dst[...] = src[...].astype(BF16)


def _lane(shape=(CHUNK, LANES)):
    return lax.broadcasted_iota(jnp.int32, shape, len(shape) - 1)


def _dot(a, b):
    return jnp.dot(a, b, preferred_element_type=F32)


def _dot_nt(a, b):
    return lax.dot_general(a, b, (((1,), (1,)), ((), ())), preferred_element_type=F32)


def _div_small(t, m, n):
    q = 0
    for b in range(1, n):
        q = q + jnp.where(t >= b * m, 1, 0)
    return q, t - q * m


def _split3(x):
    hi = x.astype(BF16)
    r = x - hi.astype(F32)
    mid = r.astype(BF16)
    lo = (r - mid.astype(F32)).astype(BF16)
    return hi, mid, lo


def _log_sigmoid(x):
    return jnp.minimum(x, 0.0) - jnp.log1p(jnp.exp(-jnp.abs(x)))


def _rope(x, cos, sin_signed):
    return x * cos + pltpu.roll(x, LANES // 2, 1) * sin_signed


def _mod_kernel(rows_ref, w_ref, b_ref, o_ref):
    a = rows_ref[...]
    a = a * jax.nn.sigmoid(a)
    a_hi = a.astype(BF16)
    a_lo = (a - a_hi.astype(F32)).astype(BF16)
    w = w_ref[...]
    w_hi = w.astype(BF16)
    w_lo = (w - w_hi.astype(F32)).astype(BF16)
    o_ref[...] = _dot(a_hi, w_hi) + _dot(a_hi, w_lo) + _dot(a_lo, w_hi) + b_ref[...]


def _modulation(rows, ada_w, ada_b):
    depth, d, n = ada_w.shape
    tn = n // 4
    return pl.pallas_call(
        _mod_kernel,
        grid=(depth, n // tn),
        in_specs=[pl.BlockSpec((8, d), lambda l, j: (0, 0)),
                  pl.BlockSpec((None, d, tn), lambda l, j: (l, 0, j)),
                  pl.BlockSpec((None, 1, tn), lambda l, j: (l, 0, j))],
        out_specs=pl.BlockSpec((None, 8, tn), lambda l, j: (l, 0, j)),
        out_shape=jax.ShapeDtypeStruct((depth, 8, n), F32),
        compiler_params=_cparams("arbitrary", "arbitrary"),
        name="adaln_modulation",
    )(rows, ada_w, ada_b.reshape(depth, 1, n))


def _norm_mod(x, ab_ref):
    ms = jnp.mean(x * x, axis=-1, keepdims=True)
    h = (x * lax.rsqrt(ms + EPS)) * ab_ref[0:1, :] + ab_ref[1:2, :]
    return h.astype(BF16)


def _ctx_or_latent_rows(ctx_ref, x_ref, tile, nb, tiles_per_batch):
    return jnp.where(_div_small(tile, tiles_per_batch, nb)[1] == 0, ctx_ref[...], x_ref[...])


def _split_row_specs(tm, d, nb, tpb, tile_of):
    lat = tpb - 1

    def latent_row(i):
        b, j = _div_small(tile_of(i), tpb, nb)
        return b * lat + jnp.maximum(j - 1, 0), 0

    return [pl.BlockSpec((tm, d), lambda i: (_div_small(tile_of(i), tpb, nb)[0], 0)),
            pl.BlockSpec((tm, d), latent_row)]


def _cumsum_cols(tri_bf, lf):
    hi, mid, lo = _split3(lf)
    return _dot(tri_bf, hi) + _dot(tri_bf, mid) + _dot(tri_bf, lo)


def _cumsum_rows(lf, tri_bf):
    hi, mid, lo = _split3(lf)
    return _dot(hi, tri_bf) + _dot(mid, tri_bf) + _dot(lo, tri_bf)


def _ret_state_update(s_ref, p, k2, vt, kdec, cd_lanes, bd):
    kf = (k2.astype(F32) * kdec).astype(BF16)
    s_ref[p] = s_ref[p] * cd_lanes + jnp.where(bd, _dot(vt, kf), 0.0)


def _mlstm_state_update(c_ref, n_ref, m_ref, k_pairs, vt_pairs, c_all, bend, col0, lo, bd):
    cmax = jnp.max(c_all, axis=0, keepdims=True)
    w_all = jnp.exp(c_all - cmax)
    m_old = m_ref[0:1, :]
    mrel = jnp.maximum(m_old, cmax)
    a_row = jnp.exp(m_old - mrel)
    bb_row = jnp.exp(cmax - mrel)
    m_ref[0:1, :] = bend + mrel
    lo_row = lo[0:1, :]
    for p in range(4):
        h0 = col0 + 2 * p
        kw = k_pairs[p] * jnp.where(lo, w_all[:, h0:h0 + 1], w_all[:, h0 + 1:h0 + 2])
        kvt = _dot(vt_pairs[p], kw.astype(BF16))
        nloc = jnp.sum(kw, axis=0, keepdims=True)
        a_l = jnp.where(lo_row, a_row[:, h0:h0 + 1], a_row[:, h0 + 1:h0 + 2])
        bb_l = jnp.where(lo_row, bb_row[:, h0:h0 + 1], bb_row[:, h0 + 1:h0 + 2])
        c_ref[p] = c_ref[p] * a_l + jnp.where(bd, kvt, 0.0) * bb_l
        n_new = (n_ref[p, 0:1, :] + n_ref[p, 1:2, :]) * a_l + nloc * bb_l
        n_ref[p, 0:1, :] = jnp.where(lo_row, n_new, 0.0)
        n_ref[p, 1:2, :] = jnp.where(lo_row, 0.0, n_new)


def _mlstm_dir_weights(st, qn_row, c_col, bt_row, m_prev, tri):
    dl = jnp.where(tri, c_col + bt_row, NEG)
    mx = jnp.max(dl, axis=0, keepdims=True)
    al = bt_row + m_prev
    m_t = jnp.maximum(al, mx)
    w = jnp.exp2(dl - m_t)
    a_t = jnp.exp2(al - m_t)
    sw = st * w
    den = jnp.sum(sw, axis=0, keepdims=True) + a_t * qn_row
    r = 1.0 / jnp.maximum(jnp.abs(den), jnp.exp2(-m_t))
    return sw * r, a_t * r


def _heads_out(ht, nw_tab):
    rows = []
    for a in range(2):
        ha = ht[a * HEAD_DIM:(a + 1) * HEAD_DIM, :]
        ms = jnp.mean(ha * ha, axis=0, keepdims=True)
        rows.append(ha * lax.rsqrt(ms + EPS))
    return (jnp.concatenate(rows, axis=0) * nw_tab).T


def _inproj_sweep_kernel(ctx_ref, x_ref, ab_ref, w_hbm, wgate_ref, gb_ref, gbt_ref, cos_ref, sin_ref,
                         lgk_ref, cw_ref, wi_src, w2_src,
                         y_ref, g_ref, gt_ref, sret_ref, cm_ref, nm_ref, mm_ref, qk_ref, wi_dst, w2_dst,
                         s_scr, c_scr, n_scr, m_scr, kdec_scr,
                         p_rk, p_rv, p_mv, p_mqk, p_g, next_row_scr, w_scr, wg_scr, wgt_scr, w_ref, w_sem, *, tiles, tpb):
    i = pl.program_id(0)
    _wcast((wi_src, w2_src), (wi_dst, w2_dst))

    def order(t):
        v = _div_small(t, tpb, tiles // tpb)[1]
        return jnp.where(v == 0, 0, tpb - v)

    jt_a = order(jnp.maximum(i - 1, 0))
    lane = _lane()
    sub = lax.broadcasted_iota(jnp.int32, (CHUNK, LANES), 0)
    lo = lane < HEAD_DIM
    lo_row = lo[0:1, :]
    bd_ret = (sub >= HEAD_DIM) == ((lane & 32) != 0)
    bd_m = (sub >= HEAD_DIM) == (lane >= HEAD_DIM)
    gw = 4 * LANES

    @pl.when(i == 0)
    def _():
        w_copies = [pltpu.make_async_copy(w_hbm.at[j * gw:(j + 1) * gw, :], w_ref.at[j * gw:(j + 1) * gw, :],
                                          w_sem.at[j]) for j in range(8)]
        for cp in w_copies:
            cp.start()
        pos = sub.astype(F32)
        for p in range(4):
            kdec_scr[p] = jnp.exp(lgk_ref[4 + p:5 + p, :] * pos)
        for ref in (p_rk, p_rv, p_mv, p_mqk, p_g, next_row_scr):
            ref[...] = jnp.zeros_like(ref)
        for grp in range(8 * gw // LANES):
            if grp % (gw // LANES) == 0:
                w_copies[grp // (gw // LANES)].wait()
            quarters = (0, 64, 32, 96) if grp < 2 * gw // LANES else (0, 32, 64, 96)
            w_t = jnp.concatenate([w_ref[grp * LANES + a:grp * LANES + a + 32, :] for a in quarters], axis=0)
            w_scr[:, grp * LANES:(grp + 1) * LANES] = w_t.T.astype(BF16)
        gate_row = lax.broadcasted_iota(jnp.int32, wgate_ref.shape, 0)
        gate_t = jnp.where(gate_row < 32, wgate_ref[...], 0.0)
        wg_scr[...] = gate_t.T.astype(BF16)
        wgt_scr[...] = gate_t[0:32, :].astype(BF16)

    @pl.when(jt_a == 0)
    def _():
        s_scr[...] = jnp.zeros_like(s_scr)
        c_scr[...] = jnp.zeros_like(c_scr)
        n_scr[...] = jnp.zeros_like(n_scr)
        m_scr[...] = jnp.zeros_like(m_scr)

    tile_i = jnp.minimum(i, tiles - 1)
    jt_i = order(tile_i)
    hb = _norm_mod(jnp.where(jt_i == 0, ctx_ref[...], x_ref[...]), ab_ref)
    cos, sin = cos_ref[...], sin_ref[...]

    def project(j):
        w_cols = w_scr[:, j * gw:(j + 1) * gw]
        acc = _dot(hb, w_cols)
        if j in (0, 1):
            if j == 0:
                acc = acc * (HEAD_DIM ** -0.5)
            acc = jnp.concatenate([_rope(acc[:, p * LANES:(p + 1) * LANES], cos, sin) for p in range(4)], axis=1)
        elif j == 3:
            acc = acc * jax.nn.sigmoid(acc)
        elif j == 7:
            acc = jax.nn.sigmoid(acc)
        return acc.astype(BF16)

    raw_q, raw_k = project(4), project(5)

    prev_on = jnp.where(jt_a <= 1, 0.0, 1.0).astype(F32)
    next_on = jnp.where((jt_a == 0) | (jt_a == tpb - 1), 0.0, 1.0).astype(F32)
    cur = p_mqk[...].astype(F32)
    n = cur.shape[0]
    row = lax.broadcasted_iota(jnp.int32, cur.shape, 0)
    prev_row = jnp.concatenate([raw_q[n - HALO:, :], raw_k[n - HALO:, :]], axis=1)[HALO - 1:HALO, :].astype(F32)
    xm = jnp.where(row == 0, prev_row * prev_on, pltpu.roll(cur, 1, 0))
    xp = jnp.where(row == n - 1, next_row_scr[0:1, :] * next_on, pltpu.roll(cur, n - 1, 0))
    conv = cw_ref[3:4, :] + cw_ref[0:1, :] * xm + cw_ref[1:2, :] * cur + cw_ref[2:3, :] * xp
    qk = conv * jax.nn.sigmoid(conv)
    qk_ref[:, 0:gw] = (qk[:, 0:gw] * (HEAD_DIM ** -0.5)).astype(BF16)
    qk_ref[:, gw:] = qk[:, gw:].astype(BF16)
    le_bf = (sub <= lane).astype(BF16)
    chunks = (1, 0)
    pre = {}
    for blk in chunks:
        rows = slice(blk * CHUNK, (blk + 1) * CHUNK)
        g = p_g[rows, :]
        bal = pltpu.roll(_cumsum_cols(le_bf, _log_sigmoid(g)), LANES - 8, 1)
        c_all = g - bal
        cmax = jnp.max(c_all, axis=0, keepdims=True)
        w_all = jnp.exp(c_all - cmax)
        vts, ks, nlocs = [], [], []
        for p in range(8):
            sl = slice((p % 4) * LANES, (p % 4 + 1) * LANES)
            if p < 4:
                v2 = p_rv[rows, sl]
                ks.append((p_rk[rows, sl].astype(F32) * kdec_scr[p]).astype(BF16))
            else:
                v2 = p_mv[rows, sl]
                h0 = 16 + 2 * (p - 4)
                kw = qk[rows, gw + (p - 4) * LANES:gw + (p - 3) * LANES] * jnp.where(
                    lo, w_all[:, h0:h0 + 1], w_all[:, h0 + 1:h0 + 2])
                ks.append(kw.astype(BF16))
                nlocs.append(jnp.sum(kw, axis=0, keepdims=True))
            vts.append(v2.astype(F32).T.astype(BF16))
        pre[blk] = (vts, ks, nlocs, cmax, bal[0:1, :])

    cur_rk = project(1)
    y_ref[:, 1 * gw:2 * gw] = cur_rk
    y_ref[:, 0:gw] = project(0)
    kvs = {blk: [_dot(pre[blk][0][p], pre[blk][1][p]) for p in range(8)] for blk in chunks}
    cur_rv = project(2)
    y_ref[:, 2 * gw:3 * gw] = cur_rv
    y_ref[:, 3 * gw:4 * gw] = project(3)

    for blk in chunks:
        _, _, nlocs, cmax, bend = pre[blk]
        sret_ref[blk] = s_scr[...].astype(BF16)
        cm_ref[blk] = c_scr[...].astype(BF16)
        nm_ref[blk] = n_scr[...].astype(BF16)
        mm_ref[blk] = m_scr[...]
        m_old = m_scr[0:1, :]
        mrel = jnp.maximum(m_old, cmax)
        a_row = jnp.exp(m_old - mrel)
        bb_row = jnp.exp(cmax - mrel)
        m_scr[0:1, :] = bend + mrel
        for p in range(4):
            cd = jnp.exp(lgk_ref[4 + p:5 + p, :] * float(CHUNK))
            s_scr[p] = s_scr[p] * cd + jnp.where(bd_ret, kvs[blk][p], 0.0)
            h0 = 16 + 2 * p
            a_l = jnp.where(lo_row, a_row[:, h0:h0 + 1], a_row[:, h0 + 1:h0 + 2])
            bb_l = jnp.where(lo_row, bb_row[:, h0:h0 + 1], bb_row[:, h0 + 1:h0 + 2])
            c_scr[p] = c_scr[p] * a_l + jnp.where(bd_m, kvs[blk][4 + p], 0.0) * bb_l
            n_new = (n_scr[p, 0:1, :] + n_scr[p, 1:2, :]) * a_l + nlocs[p] * bb_l
            n_scr[p, 0:1, :] = jnp.where(lo_row, n_new, 0.0)
            n_scr[p, 1:2, :] = jnp.where(lo_row, 0.0, n_new)

    cur_mv = project(6)
    y_ref[:, 4 * gw:5 * gw] = cur_mv
    y_ref[:, 5 * gw:6 * gw] = project(7)
    gates = _dot(hb, wg_scr[...]) + gb_ref[...]
    g_ref[...] = gates
    gt_ref[...] = _dot_nt(wgt_scr[...], hb) + gbt_ref[...]
    next_row_scr[...] = p_mqk[0:HALO, :].astype(F32)
    p_mqk[:, 0:gw] = raw_q
    p_mqk[:, gw:] = raw_k
    p_rk[...] = cur_rk
    p_rv[...] = cur_rv
    p_mv[...] = cur_mv
    p_g[...] = gates


def _inproj_sweep(ctx2, x2, ab, w_in, gb, gbt, cos, sin, lgk, cw, ffn_w, nb, tpb):
    d = x2.shape[1]
    tm = ROW_TILE
    cpt = tm // CHUNK
    gw = 4 * LANES
    tiles = nb * tpb
    r = tiles * tm
    lat = tpb - 1
    def visit(t):
        b, v = _div_small(t, tpb, nb)
        return b, jnp.where(v == 0, 0, tpb - v)

    cur = lambda i: visit(jnp.minimum(i, tiles - 1))
    flat = lambda bj: bj[0] * tpb + bj[1]
    tile_i = lambda i: flat(cur(i))
    tile_a = lambda i: flat(visit(jnp.maximum(i - 1, 0)))
    in_batch = lambda i: cur(i)[1]
    sel = lambda i: (cur(i)[0] * 2 + jnp.minimum(cur(i)[1], 1), 0, 0)
    state = lambda *dims: pl.BlockSpec((cpt,) + dims, lambda i: (tile_a(i),) + (0,) * len(dims))
    nchunks = tiles * cpt
    wc_in, wc_out, wc_shapes = _wcast_specs(*ffn_w, 0, tiles + 1)
    return pl.pallas_call(
        functools.partial(_inproj_sweep_kernel, tiles=tiles, tpb=tpb),
        grid=(tiles + 1,),
        in_specs=[pl.BlockSpec((tm, d), lambda i: (cur(i)[0], 0)),
                  pl.BlockSpec((tm, d), lambda i: (cur(i)[0] * lat + jnp.maximum(cur(i)[1] - 1, 0), 0)),
                  pl.BlockSpec((None, 2, d), sel),
                  pl.BlockSpec(memory_space=pl.ANY),
                  pl.BlockSpec((LANES, d), lambda i: (8 * gw // LANES, 0), pipeline_mode=pl.Buffered(1)),
                  _const_spec(gb.shape), _const_spec(gbt.shape),
                  pl.BlockSpec((tm, LANES), lambda i: (in_batch(i), 0)),
                  pl.BlockSpec((tm, LANES), lambda i: (in_batch(i), 0)),
                  _const_spec(lgk.shape), _const_spec(cw.shape)] + wc_in,
        out_specs=[pl.BlockSpec((tm, 6 * gw), lambda i: (tile_i(i), 0)),
                   pl.BlockSpec((tm, LANES), lambda i: (tile_i(i), 0)),
                   pl.BlockSpec((32, tm), lambda i: (0, tile_i(i))),
                   state(4, CHUNK, LANES), state(4, CHUNK, LANES), state(4, HALO, LANES), state(8, LANES),
                   pl.BlockSpec((tm, 2 * gw), lambda i: (tile_a(i), 0))] + wc_out,
        out_shape=[jax.ShapeDtypeStruct((r, 6 * gw), BF16),
                   jax.ShapeDtypeStruct((r, LANES), F32),
                   jax.ShapeDtypeStruct((32, r), F32),
                   jax.ShapeDtypeStruct((nchunks, 4, CHUNK, LANES), BF16),
                   jax.ShapeDtypeStruct((nchunks, 4, CHUNK, LANES), BF16),
                   jax.ShapeDtypeStruct((nchunks, 4, HALO, LANES), BF16),
                   jax.ShapeDtypeStruct((nchunks, 8, LANES), F32),
                   jax.ShapeDtypeStruct((r, 2 * gw), BF16)] + wc_shapes,
        scratch_shapes=[pltpu.VMEM((4, CHUNK, LANES), F32), pltpu.VMEM((4, CHUNK, LANES), F32),
                        pltpu.VMEM((4, HALO, LANES), F32), pltpu.VMEM((8, LANES), F32),
                        pltpu.VMEM((4, CHUNK, LANES), F32),
                        pltpu.VMEM((tm, gw), BF16), pltpu.VMEM((tm, gw), BF16), pltpu.VMEM((tm, gw), BF16),
                        pltpu.VMEM((tm, 2 * gw), BF16), pltpu.VMEM((tm, LANES), F32),
                        pltpu.VMEM((HALO, 2 * gw), F32),
                        pltpu.VMEM((d, 8 * gw), BF16), pltpu.VMEM((d, LANES), BF16), pltpu.VMEM((32, d), BF16),
                        pltpu.VMEM((8 * gw, d), F32), pltpu.SemaphoreType.DMA((8,))],
        compiler_params=_cparams("arbitrary"),
        name="inproj_bwd_sweep",
    )(ctx2, x2, ab, w_in, w_in, gb, gbt, cos, sin, lgk, cw, *ffn_w)


class _Bag:
    def __init__(self, **kw):
        self.__dict__.update(kw)


def _mixer_chunk_stages(blk, r):
    rows = slice(blk * CHUNK, (blk + 1) * CHUNK)
    lane = _lane()
    sub = lax.broadcasted_iota(jnp.int32, (CHUNK, LANES), 0)
    lo = lane < HEAD_DIM
    sub_lo = sub < HEAD_DIM
    mask_ret = [(lane & 32) == 0, (lane & 32) != 0]
    mask_nat = [lo, lane >= HEAD_DIM]
    bd_ret = (sub >= HEAD_DIM) == ((lane & 32) != 0)
    bd_m = (sub >= HEAD_DIM) == (lane >= HEAD_DIM)
    le = sub <= lane
    ge = sub >= lane

    g = r.g_ref[rows, :]
    gt = r.gt_ref[:, rows]
    lf_col = _log_sigmoid(g)
    lf_row = _log_sigmoid(gt)
    le_bf = le.astype(BF16)
    ge_bf = ge.astype(BF16)
    pre_col = _cumsum_cols(ge_bf, lf_col)
    bal_f = pltpu.roll(pre_col, LANES - 8, 1)
    bal_b = pltpu.roll(pre_col[CHUNK - 1:CHUNK, :] - pre_col + lf_col, LANES - 8, 1)
    cf_all = g - bal_f
    cb_all = g - bal_b
    bf_row = _cumsum_rows(lf_row, le_bf)
    bb_row = bf_row[:, CHUNK - 1:CHUNK] - bf_row + lf_row
    qb, kb, vts, kf32, qt, vbd = [], [], [], [], [], []
    for p in range(8):
        sl = slice((p % 4) * LANES, (p % 4 + 1) * LANES)
        if p < 4:
            q2, k2, v2 = r.rq_ref[rows, sl], r.rk_ref[rows, sl], r.rv_ref[rows, sl]
            qf = q2.astype(F32)
            kf = None
        else:
            q2 = r.qk_ref[rows, sl]
            k2 = r.qk_ref[rows, 4 * LANES + (p - 4) * LANES:4 * LANES + (p - 3) * LANES]
            v2 = r.mv_ref[rows, sl]
            qf, kf = q2.astype(F32), k2.astype(F32)
        vt = v2.astype(F32).T.astype(BF16)
        qb.append(q2)
        kb.append(k2)
        vts.append(vt)
        kf32.append(kf)
        qt.append(qf.T)
        vbd.append([jnp.where(sub_lo, vt, jnp.zeros_like(vt)), jnp.where(sub_lo, jnp.zeros_like(vt), vt)])
    yield

    st2 = []
    for p in range(8):
        masks = mask_ret if p < 4 else mask_nat
        zero = jnp.zeros_like(qb[p])
        qstack = jnp.concatenate([jnp.where(masks[0], qb[p], zero), jnp.where(masks[1], qb[p], zero)], axis=0)
        st2.append(_dot_nt(kb[p], qstack))
    yield

    qn = [_dot_nt(jnp.concatenate([r.n_scr[p].astype(BF16), r.nmb_ref[blk, p]], axis=0), qb[4 + p])
          for p in range(4)]
    qn_f = [x[0:HALO] for x in qn]
    qn_b = [x[HALO:2 * HALO] for x in qn]
    cf2, cb2, bf2, bb2 = cf_all * LOG2E, cb_all * LOG2E, bf_row * LOG2E, bb_row * LOG2E
    m_f2, m_b2 = r.m_state[0:1, :] * LOG2E, r.mmb_ref[blk, 0:1, :] * LOG2E
    lhs, rhs = [], []
    for p in range(8):
        if p < 4:
            pts = [(st2[p][:, a * LANES:(a + 1) * LANES] * r.dm_scr[2 * p + a]).astype(BF16) for a in range(2)]
            x_f, x_b = r.dec_scr[0, p], r.dec_scr[1, p]
            old = [r.s_scr[p].astype(BF16), r.sretb_ref[blk, p]]
        else:
            pts, cf, cb = [], [], []
            for a in range(2):
                h = 2 * (p - 4) + a
                st = st2[p][:, a * LANES:(a + 1) * LANES]
                pf, coef_f = _mlstm_dir_weights(st, qn_f[p - 4][a:a + 1, :], cf2[:, h:h + 1],
                                                bf2[8 + h:9 + h, :], m_f2[0:1, h:h + 1], le)
                pb, coef_b = _mlstm_dir_weights(st, qn_b[p - 4][a:a + 1, :], cb2[:, 16 + h:17 + h],
                                                bb2[24 + h:25 + h, :], m_b2[0:1, 16 + h:17 + h], ge)
                pts.append((pf + pb).astype(BF16))
                cf.append(coef_f)
                cb.append(coef_b)
            x_f = jnp.where(sub_lo, cf[0], cf[1])
            x_b = jnp.where(sub_lo, cb[0], cb[1])
            old = [r.c_scr[p - 4].astype(BF16), r.cmb_ref[blk, p - 4]]
        lhs.append(jnp.concatenate(vbd[p] + old, axis=1))
        rhs.append(jnp.concatenate(pts + [(qt[p] * x_f).astype(BF16), (qt[p] * x_b).astype(BF16)], axis=0))
    yield

    ht = [_dot(lhs[p], rhs[p]) for p in range(8)]
    yield

    for p in range(8):
        sl = slice((p % 4) * LANES, (p % 4 + 1) * LANES)
        gate_ref = r.rg_ref if p < 4 else r.mo_ref
        y = _heads_out(ht[p], r.nw_ref[p])
        new = (y * gate_ref[rows, sl].astype(F32)).astype(BF16)
        r.mix_scr[rows, p * LANES:(p + 1) * LANES] = jnp.where(r.live, new, r.mix_scr[rows, p * LANES:(p + 1) * LANES])
    yield

    for p in range(4):
        cd = jnp.exp(r.lgk_ref[p:p + 1, :] * float(CHUNK))
        _ret_state_update(r.s_scr, p, kb[p], vts[p], r.dec_scr[2, p], cd, bd_ret)
    _mlstm_state_update(r.c_scr, r.n_scr, r.m_state, kf32[4:], vts[4:], cf_all, bal_f[CHUNK - 1:CHUNK, :],
                        0, lo, bd_m)
    yield


def _attn_inproj_stages(x, ab_ref, wq_ref, wk_ref, wv_ref, nw_ref, cos, sin, y_ref):
    hb = _norm_mod(x, ab_ref)
    r2 = lax.broadcasted_iota(jnp.int32, (2 * LANES, 2 * LANES), 0)
    c2 = lax.broadcasted_iota(jnp.int32, (2 * LANES, 2 * LANES), 1)
    same_head = (((r2 ^ c2) & (LANES | 32)) == 0).astype(BF16)
    acc_q = _dot(hb, wq_ref[...])
    acc_k = _dot(hb, wk_ref[...])
    y_ref[:, 10 * LANES:12 * LANES] = _dot(hb, wv_ref[...]).astype(BF16)
    yield
    for j in range(5):
        acc = acc_q[:, j * 2 * LANES:(j + 1) * 2 * LANES] if j < 4 else acc_k
        sq = acc * acc
        ms = _dot(sq.astype(BF16), same_head) * (1.0 / HEAD_DIM)
        nrm = acc * lax.rsqrt(ms + EPS)
        nw = nw_ref[0:1, :] if j < 4 else nw_ref[1:2, :]
        for v in range(2):
            ls = slice(v * LANES, (v + 1) * LANES)
            y_ref[:, (2 * j + v) * LANES:(2 * j + v + 1) * LANES] = _rope(nrm[:, ls] * nw, cos, sin).astype(BF16)
        if j in (1, 4):
            yield


def _ffn_splits(d_ff, pieces):
    blocks = d_ff // (2 * LANES)
    assert blocks * 2 * LANES == d_ff and blocks >= pieces
    cuts = [((i * blocks) // pieces) * 2 * LANES for i in range(pieces)]
    return cuts + [d_ff]


def _prepare_attn_weights(wa_src, wa_scr):
    rows = wa_src.shape[0]
    d = wa_src.shape[1] - 4 * LANES
    quarter = _lane((rows, LANES)) // 32
    group = lambda ref, grp: ref[:, grp * LANES:(grp + 1) * LANES]
    roll = lambda v, shift: pltpu.roll(v, shift % LANES, 1) if shift % LANES else v
    g_per = d // (H_KV * HEAD_DIM)
    for pair in range(H_KV // 2):
        for g in range(g_per):
            h0 = (g * HEAD_DIM) % LANES
            src = lambda kv: group(wa_src, (kv * g_per * HEAD_DIM + g * HEAD_DIM) // LANES)
            head_a, head_b = src(2 * pair), src(2 * pair + 1)
            out = jnp.where(quarter == 0, roll(head_a, -h0),
                            jnp.where(quarter == 1, roll(head_b, 32 - h0),
                                      jnp.where(quarter == 2, roll(head_a, 32 - h0), roll(head_b, 64 - h0))))
            wa_scr[:, (pair * g_per + g) * LANES:(pair * g_per + g + 1) * LANES] = out.astype(BF16)
    for grp in range(d // LANES, d // LANES + H_KV // 2):
        w = group(wa_src, grp)
        out = jnp.where(quarter == 1, roll(w, -32), jnp.where(quarter == 2, roll(w, 32), w))
        wa_scr[:, grp * LANES:(grp + 1) * LANES] = out.astype(BF16)
    wa_scr[:, d + 2 * LANES:] = wa_src[:, d + 2 * LANES:].astype(BF16)


def _mixer_ffn_kernel(lg_ref, y_ref, qk_ref,
                      g_ref, gt_ref, sretb_ref, cmb_ref, nmb_ref, mmb_ref,
                      lgk_ref, lgkt_ref, nw_ref,
                      ctx_ref, x_ref, p_ref, wo_ref, wi_ref, w2_ref,
                      ab2_ref, wa_src, nwa_ref, cos_ref, sin_ref, wi_src, w2_src,
                      o_ref, y2_ref, wi_dst, w2_dst,
                      s_scr, c_scr, n_scr, m_state, dm_scr, dec_scr, mix_scr, act_scr, x2_scr, wa_scr,
                      *, tiles, tpb, d_ff):
    s = pl.program_id(0)
    _wcast((wi_src, w2_src), (wi_dst, w2_dst))
    nb = tiles // tpb
    jt = _div_small(jnp.minimum(s, tiles - 1), tpb, nb)[1]
    gw = 4 * LANES
    rq_ref, rk_ref, rv_ref, rg_ref, mv_ref, mo_ref = [y_ref.at[:, j * gw:(j + 1) * gw] for j in range(6)]
    lane = _lane()
    sub = lax.broadcasted_iota(jnp.int32, (CHUNK, LANES), 0)

    @pl.when(s == 0)
    def _():
        mix_scr[...] = jnp.zeros_like(mix_scr)
        x2_scr[...] = jnp.zeros_like(x2_scr)
        le = sub <= lane
        ge = sub >= lane
        spos = sub.astype(F32)
        tpos = lane.astype(F32)
        diff = (lane - sub).astype(F32)
        for h in range(8):
            dm_scr[h] = (jnp.where(le, jnp.exp(lg_ref[h] * diff), 0.0)
                         + jnp.where(ge, jnp.exp(lg_ref[8 + h] * (-diff)), 0.0))
        for p in range(4):
            dec_scr[0, p] = jnp.exp(lgkt_ref[:, p:p + 1] * (tpos + 1.0))
            dec_scr[1, p] = jnp.exp(lgkt_ref[:, 4 + p:5 + p] * (float(CHUNK) - tpos))
            dec_scr[2, p] = jnp.exp(lgk_ref[p:p + 1, :] * (float(CHUNK) - 1.0 - spos))
        _prepare_attn_weights(wa_src, wa_scr)

    @pl.when(jt == 0)
    def _():
        s_scr[...] = jnp.zeros_like(s_scr)
        c_scr[...] = jnp.zeros_like(c_scr)
        n_scr[...] = jnp.zeros_like(n_scr)
        m_state[...] = jnp.zeros_like(m_state)

    r = _Bag(rq_ref=rq_ref, rk_ref=rk_ref, rv_ref=rv_ref, rg_ref=rg_ref, qk_ref=qk_ref, mv_ref=mv_ref, mo_ref=mo_ref,
             g_ref=g_ref, gt_ref=gt_ref, sretb_ref=sretb_ref, cmb_ref=cmb_ref, nmb_ref=nmb_ref, mmb_ref=mmb_ref,
             lgk_ref=lgk_ref, nw_ref=nw_ref, s_scr=s_scr, c_scr=c_scr, n_scr=n_scr, m_state=m_state,
             dm_scr=dm_scr, dec_scr=dec_scr, mix_scr=mix_scr, live=s < tiles)
    cuts = _ffn_splits(d_ff, 3)
    ffn_piece = lambda i: _ffn_cols(h, wi_ref, act_scr, cuts[i], cuts[i + 1], d_ff)

    x = _ctx_or_latent_rows(ctx_ref, x_ref, jnp.clip(s - 1, 0, tiles - 1), nb, tpb)
    x1, h = _ffn_in(x, mix_scr[...], p_ref, wo_ref)

    d = x2_scr.shape[1]
    nxt = _attn_inproj_stages(x2_scr[...], ab2_ref, wa_scr.at[:, 0:d], wa_scr.at[:, d:d + 2 * LANES],
                              wa_scr.at[:, d + 2 * LANES:], nwa_ref, cos_ref[...], sin_ref[...], y2_ref)
    chunk_a, chunk_b = _mixer_chunk_stages(0, r), _mixer_chunk_stages(1, r)
    next(nxt)
    next(chunk_a), next(chunk_b)
    next(chunk_a), next(chunk_b)
    ffn_piece(0)
    next(chunk_a), next(chunk_a)
    ffn_piece(1)
    next(chunk_a), next(chunk_a)
    next(nxt)
    ffn_piece(2)
    next(chunk_b), next(chunk_b)
    next(nxt)
    x2 = x1 + p_ref[3:4, :] * _dot(act_scr[...], w2_ref[...])
    o_ref[...] = x2
    next(chunk_b), next(chunk_b)
    x2_scr[...] = x2


def _mixer_ffn(y, g, gt, states, lg_smem, lgk, lgkt, nw, xs, prm, wo, wi, w2,
               ab_next, w_next, nw_next, cos, sin, ffn_w, nb, tpb):
    gw = 4 * LANES
    tm = ROW_TILE
    cpt = tm // CHUNK
    sretb, cmb, nmb, mmb, qk_act = states
    tiles = nb * tpb
    d = xs[-1].shape[1]
    d_ff = w2.shape[0]
    n_next = w_next.shape[-1]
    mix = lambda s: jnp.minimum(s, tiles - 1)
    ffn = lambda s: jnp.clip(s - 1, 0, tiles - 1)
    nxt = lambda s: jnp.maximum(s - 2, 0)
    state_spec = lambda a: pl.BlockSpec((cpt,) + a.shape[1:], lambda s: (mix(s),) + (0,) * (a.ndim - 1))
    def sel(t):
        b, j = _div_small(t, tpb, nb)
        return b * 2 + jnp.minimum(j, 1), 0, 0

    wc_in, wc_out, wc_shapes = _wcast_specs(*ffn_w, 1, tiles + 2)
    return pl.pallas_call(
        functools.partial(_mixer_ffn_kernel, tiles=tiles, tpb=tpb, d_ff=d_ff),
        grid=(tiles + 2,),
        in_specs=[_smem_spec(),
                  pl.BlockSpec((tm, 6 * gw), lambda s: (mix(s), 0)),
                  pl.BlockSpec((tm, 2 * gw), lambda s: (mix(s), 0)),
                  pl.BlockSpec((tm, LANES), lambda s: (mix(s), 0)),
                  pl.BlockSpec((32, tm), lambda s: (0, mix(s))),
                  state_spec(sretb), state_spec(cmb), state_spec(nmb), state_spec(mmb),
                  _const_spec(lgk.shape), _const_spec(lgkt.shape), _const_spec(nw.shape)]
                 + _split_row_specs(tm, d, nb, tpb, ffn)
                 + [pl.BlockSpec((None, 8, d), lambda s: sel(ffn(s))),
                    _const_spec(wo.shape), _const_spec(wi.shape), _const_spec(w2.shape),
                    pl.BlockSpec((None, 2, d), lambda s: sel(nxt(s))),
                    pl.BlockSpec((None,) + w_next.shape[1:], lambda s: (0, 0, 0), pipeline_mode=pl.Buffered(1)),
                    _const_spec(nw_next.shape),
                    pl.BlockSpec((tm, LANES), lambda s: (_div_small(nxt(s), tpb, nb)[1], 0)),
                    pl.BlockSpec((tm, LANES), lambda s: (_div_small(nxt(s), tpb, nb)[1], 0))] + wc_in,
        out_specs=[pl.BlockSpec((tm, d), lambda s: (ffn(s), 0)),
                   pl.BlockSpec((tm, n_next), lambda s: (nxt(s), 0))] + wc_out,
        out_shape=[jax.ShapeDtypeStruct((tiles * tm, d), F32),
                   jax.ShapeDtypeStruct((tiles * tm, n_next), BF16)] + wc_shapes,
        scratch_shapes=[pltpu.VMEM((4, CHUNK, LANES), F32), pltpu.VMEM((4, CHUNK, LANES), F32),
                        pltpu.VMEM((4, HALO, LANES), F32), pltpu.VMEM((8, LANES), F32),
                        pltpu.VMEM((8, CHUNK, LANES), F32), pltpu.VMEM((3, 4, CHUNK, LANES), F32),
                        pltpu.VMEM((tm, 2 * gw), BF16), pltpu.VMEM((tm, d_ff), BF16),
                        pltpu.VMEM((tm, d), F32),
                        pltpu.VMEM(w_next.shape[1:], BF16)],
        compiler_params=_cparams("arbitrary"),
        name="ret_mlstm_mixer_ffn",
    )(lg_smem, y, qk_act, g, gt, sretb, cmb, nmb, mmb, lgk, lgkt, nw,
      *xs, prm, wo, wi, w2, ab_next, w_next, nw_next, cos, sin, *ffn_w)


def _ffn_in(x, m, p_ref, wo_ref):
    x1 = x + p_ref[0:1, :] * _dot(m, wo_ref[...])
    ms = jnp.mean(x1 * x1, axis=-1, keepdims=True)
    return x1, ((x1 * lax.rsqrt(ms + EPS)) * p_ref[1:2, :] + p_ref[2:3, :]).astype(BF16)


def _ffn_cols(h, wi_ref, act_scr, lo, hi, d_ff):
    gate = _dot(h, wi_ref[:, lo:hi])
    up = _dot(h, wi_ref[:, d_ff + lo:d_ff + hi])
    act_scr[:, lo:hi] = (gate * jax.nn.sigmoid(gate) * up).astype(BF16)


def _window_bias():
    kk = np.arange(CHUNK)[:, None]
    t = np.arange(CHUNK)[None, :]
    tabs = []
    for has_prev, has_next in ((False, True), (True, True), (True, False)):
        prev_ok = (kk >= t) & has_prev
        next_ok = (kk <= t) & has_next
        tabs.append(np.where(np.concatenate([prev_ok, next_ok], axis=0), 0.0, NEG))
    return jnp.asarray(np.stack(tabs), F32)


def _attn_ffn_kernel(sink_ref, qkv_a_ref, qkv_b_ref, kvp_ref, kvn_ref, kvx_ref, bias_ref,
                     x_a_ref, x_b_ref, p_ref, wo_ref, wi_ref, w2_ref, o_ref, m_scr, act_scr,
                     *, steps, per_batch, d_ff):
    s = pl.program_id(0)
    kw = 2 * LANES
    k0 = 8 * LANES
    tm = qkv_a_ref.shape[0]
    nblk = 2 * tm // CHUNK

    @pl.when(s == 0)
    def _():
        m_scr[...] = jnp.zeros_like(m_scr)

    j = _div_small(jnp.minimum(s, steps - 1), per_batch, steps // per_batch)[1]
    grp = lax.broadcasted_iota(jnp.int32, (1, 4 * CHUNK), 1) // CHUNK
    lane = _lane()
    mask_q = [(lane & 32) == 0, (lane & 32) != 0]
    tile4 = lambda b: jnp.concatenate([b] * 4, axis=1)
    inner = tile4(bias_ref[1])
    biases = ([tile4(jnp.where(j == 0, bias_ref[0], bias_ref[1]))] + [inner] * (nblk - 2)
              + [tile4(jnp.where(j == per_batch - 1, bias_ref[2], bias_ref[1]))])
    cuts = _ffn_splits(d_ff, nblk - 1)

    def ffn_up(i):
        _ffn_cols(h, wi_ref, act_scr, cuts[i], cuts[i + 1], d_ff)

    def q_rows(blk):
        ref = qkv_a_ref if blk * CHUNK < tm else qkv_b_ref
        r0 = (blk * CHUNK) % tm
        return ref, slice(r0, r0 + CHUNK)

    def scores(blk, kv):
        kvp, a = kv // 2, kv % 2
        ref, rows = q_rows(blk)
        qs = jnp.concatenate(
            [jnp.where(mask_q[a], ref[rows, (kvp * 4 + g) * LANES:(kvp * 4 + g + 1) * LANES],
                       jnp.zeros((CHUNK, LANES), BF16)) for g in range(4)], axis=0)
        return _dot_nt(kcats[kvp][blk], qs)

    def softmax_pv(blk, kv, st):
        bias = biases[blk]
        st = jnp.concatenate([st[0:CHUNK] + bias[0:CHUNK], st[CHUNK:2 * CHUNK],
                              st[2 * CHUNK:3 * CHUNK] + bias[CHUNK:2 * CHUNK], st[3 * CHUNK:]], axis=0)
        snk = jnp.where(grp == 0, sink_ref[kv * 4],
                        jnp.where(grp == 1, sink_ref[kv * 4 + 1],
                                  jnp.where(grp == 2, sink_ref[kv * 4 + 2], sink_ref[kv * 4 + 3])))
        m = jnp.maximum(jnp.max(st, axis=0, keepdims=True), snk)
        e = jnp.exp2(st - m)
        denom = jnp.exp2(snk - m) + jnp.sum(e, axis=0, keepdims=True)
        a = kv % 2
        return _dot(vts[kv // 2][blk][a * HEAD_DIM:(a + 1) * HEAD_DIM, :], e.astype(BF16)) * (1.0 / denom)

    def hand_over(blk, outs):
        for kvp in range(2):
            full = jnp.concatenate(outs[2 * kvp:2 * kvp + 2], axis=0)
            for g in range(4):
                m_scr[blk * CHUNK:(blk + 1) * CHUNK, (kvp * 4 + g) * LANES:(kvp * 4 + g + 1) * LANES] = (
                    full[:, g * CHUNK:(g + 1) * CHUNK].T.astype(BF16))

    x1, h = _ffn_in(jnp.concatenate([x_a_ref[...], x_b_ref[...]], axis=0), m_scr[...], p_ref, wo_ref)

    kcats, vts = [], []
    for kvp in range(2):
        ks = slice(k0 + kvp * LANES, k0 + (kvp + 1) * LANES)
        vs = slice(k0 + kw + kvp * LANES, k0 + kw + (kvp + 1) * LANES)
        es, ev = slice(kvp * LANES, (kvp + 1) * LANES), slice(kw + kvp * LANES, kw + (kvp + 1) * LANES)
        tile_rows = [(ref, slice(c * CHUNK, (c + 1) * CHUNK)) for ref in (qkv_a_ref, qkv_b_ref)
                     for c in range(tm // CHUNK)]
        k_chunks = [kvp_ref[:, es]] + [ref[rows, ks] for ref, rows in tile_rows] + [kvn_ref[:, es]]
        v_chunks = ([kvp_ref[:, ev]] + [ref[rows, vs] for ref, rows in tile_rows] + [kvn_ref[:, ev]]
                    + [kvx_ref[c * CHUNK:(c + 1) * CHUNK, ev] for c in range(kvx_ref.shape[0] // CHUNK)])
        v_t = [v.astype(F32).T.astype(BF16) for v in v_chunks]
        kcats.append([jnp.concatenate(k_chunks[b:b + 3] + [kvx_ref[:, es]], axis=0) for b in range(nblk)])
        vts.append([jnp.concatenate(v_t[b:b + 3] + v_t[nblk + 2:], axis=1) for b in range(nblk)])

    for blk in range(nblk):
        sts = [scores(blk, kv) for kv in range(4)]
        if blk < nblk - 1:
            ffn_up(blk)
        else:
            o_ref[...] = x1 + p_ref[3:4, :] * _dot(act_scr[...], w2_ref[...])
        hand_over(blk, [softmax_pv(blk, kv, sts[kv]) for kv in range(4)])


def _attn_ffn(y, sink, xc, prm, wo, wi, w2, nb, nc, ctx_chunks):
    tm = ROW_TILE
    cpt = tm // CHUNK
    tpb = nc // cpt
    lat_tiles = (nc - ctx_chunks) // cpt
    ctx_tiles = ctx_chunks // cpt
    assert ctx_tiles == 1 and lat_tiles % 2 == 0 and lat_tiles >= 4
    per_batch = lat_tiles // 2
    steps = nb * per_batch
    d = xc.shape[1]
    d_ff = w2.shape[0]
    bias = _window_bias()
    n_qkv = y.shape[1]
    kv_w = n_qkv - d
    assert d % kv_w == 0
    kv_col = d // kv_w
    att = lambda s: jnp.minimum(s, steps - 1)
    ffn = lambda s: jnp.maximum(s - 1, 0)
    split = lambda t: _div_small(t, per_batch, nb)
    row_tile = lambda t, u: split(t)[0] * tpb + ctx_tiles + 2 * split(t)[1] + u
    chunk0 = lambda t: split(t)[0] * nc + ctx_chunks
    prev_c = lambda s: chunk0(att(s)) + jnp.maximum(split(att(s))[1] * 2 * cpt - 1, 0)
    next_c = lambda s: chunk0(att(s)) + jnp.minimum(split(att(s))[1] * 2 * cpt + 2 * cpt, lat_tiles * cpt - 1)
    edge_spec = lambda f: pl.BlockSpec((CHUNK, kv_w), lambda s: (f(s), kv_col))
    return pl.pallas_call(
        functools.partial(_attn_ffn_kernel, steps=steps, per_batch=per_batch, d_ff=d_ff),
        grid=(steps + 1,),
        in_specs=[_smem_spec(),
                  pl.BlockSpec((tm, n_qkv), lambda s: (row_tile(att(s), 0), 0)),
                  pl.BlockSpec((tm, n_qkv), lambda s: (row_tile(att(s), 1), 0)),
                  edge_spec(prev_c), edge_spec(next_c),
                  pl.BlockSpec((tm, kv_w), lambda s: (split(att(s))[0] * tpb, kv_col)),
                  _const_spec(bias.shape),
                  pl.BlockSpec((tm, d), lambda s: (row_tile(ffn(s), 0), 0)),
                  pl.BlockSpec((tm, d), lambda s: (row_tile(ffn(s), 1), 0)),
                  pl.BlockSpec((None, 8, d), lambda s: (split(ffn(s))[0] * 2 + 1, 0, 0)),
                  _const_spec(wo.shape), _const_spec(wi.shape), _const_spec(w2.shape)],
        out_specs=pl.BlockSpec((2 * tm, d), lambda s: (ffn(s), 0)),
        out_shape=jax.ShapeDtypeStruct((steps * 2 * tm, d), F32),
        scratch_shapes=[pltpu.VMEM((2 * tm, d), BF16), pltpu.VMEM((2 * tm, d_ff), BF16)],
        compiler_params=_cparams("arbitrary"),
        name="window_gqa_ffn",
    )(sink, y, y, y, y, y, bias, xc, xc, prm, wo, wi, w2)


def _attn_o_rows(w):
    cols = w.shape[1]
    g_per = w.shape[0] // (H_KV * HEAD_DIM)
    return (w.reshape(H_KV // 2, 2, g_per, HEAD_DIM, cols).transpose(0, 2, 1, 3, 4)
            .reshape(w.shape[0], cols))


def _rope_tables(seq, ctx_len):
    rows = seq // GRID_W
    row = np.repeat(np.arange(rows, dtype=np.float32), GRID_W)
    col = np.tile(np.arange(GRID_W, dtype=np.float32), rows)
    n = HEAD_DIM // 4
    inv = (np.float32(ROPE_BASE) ** (-np.arange(n, dtype=np.float32) / np.float32(n))).astype(np.float32)
    ang = np.concatenate([row[:, None] * inv, col[:, None] * inv], axis=-1).astype(np.float32)
    cos, sin = np.cos(ang), np.sin(ang)
    cos_t = np.concatenate([np.ones((ctx_len, LANES), np.float32), np.tile(cos, (1, 4))], axis=0)
    sin_t = np.concatenate([np.zeros((ctx_len, LANES), np.float32),
                            np.concatenate([-sin, -sin, sin, sin], axis=-1)], axis=0)
    return jnp.asarray(cos_t, F32), jnp.asarray(sin_t, F32)


def _mod_tables(mod, nb, norm_w):
    d = norm_w.shape[-1]
    lat = mod[:nb].reshape(nb, 6, d)
    ctx = jnp.broadcast_to(mod[nb].reshape(1, 6, d), (nb, 6, d))
    both = jnp.stack([ctx, lat], axis=1).reshape(nb * 2, 6, d)
    sh1, sc1, g1, sh2, sc2, g2 = [both[:, k] for k in range(6)]
    ab1 = jnp.stack([norm_w[0] * (1.0 + sc1), sh1], axis=1)
    zeros = jnp.zeros_like(g1)
    prm = jnp.stack([g1, norm_w[1] * (1.0 + sc2), sh2, g2, zeros, zeros, zeros, zeros], axis=1)
    return ab1, prm


def kernel(x, c, ctx, c_ctx, ada_w, ada_b, norm_w, ffn_w_in, ffn_w_out, ab_w_in, ab_w_out,
           ret_log_gamma, ret_norm_w, mlstm_conv_w, mlstm_conv_b, mlstm_gate_b, mlstm_norm_w,
           attn_w_in, attn_w_out, attn_q_norm_w, attn_k_norm_w, attn_sink):
    nb, seq, d = x.shape
    ctx_len = ctx.shape[1]
    depth = ada_w.shape[0]
    assert ctx_len == ROW_TILE and seq % ROW_TILE == 0 and d == 8 * LANES and nb < 8
    t_all = ctx_len + seq
    nc = t_all // CHUNK
    ctx_chunks = ctx_len // CHUNK
    tpb = t_all // ROW_TILE
    dr = d // 2

    rows = jnp.zeros((8, d), F32).at[:nb].set(c).at[nb].set(c_ctx)
    mod_all = _modulation(rows, ada_w, ada_b)
    cos_t, sin_t = _rope_tables(seq, ctx_len)
    ffn_w = (ffn_w_in, ffn_w_out)
    xs = (ctx.reshape(nb * ctx_len, d), x.reshape(nb * seq, d))

    assert depth == 2 and ab_w_in.shape[0] == 1 and attn_w_in.shape[0] == 1
    ab_0, prm_0 = _mod_tables(mod_all[0], nb, norm_w[0])
    ab_1, prm_1 = _mod_tables(mod_all[1], nb, norm_w[1])

    assert ab_w_in.shape[2] == 8 * dr + 32
    w_in_t = jnp.swapaxes(ab_w_in, 1, 2)[0]
    gb = jnp.zeros((1, LANES), F32).at[0, :32].set(mlstm_gate_b[0].reshape(-1))
    gbt = mlstm_gate_b[0].reshape(32, 1)
    lg = ret_log_gamma[0].astype(F32)
    lgk = jnp.tile(jnp.repeat(lg.reshape(2, 4, 2), 32, axis=-1), (1, 1, 2)).reshape(8, LANES)
    cw = jnp.concatenate([mlstm_conv_w[0], mlstm_conv_b[0][None], jnp.zeros((4, 2 * dr), F32)], axis=0)
    nw = jnp.broadcast_to(jnp.concatenate([ret_norm_w[0], mlstm_norm_w[0]]).reshape(8, LANES, 1),
                          (8, LANES, LANES))
    wo_0 = ab_w_out[0].astype(BF16)

    lane_w = lambda v: jnp.concatenate([v[:32], v[:32], v[32:], v[32:]])
    nwq = jnp.stack([lane_w(attn_q_norm_w[0]) * (HEAD_DIM ** -0.5 * LOG2E), lane_w(attn_k_norm_w[0])]
                    + [jnp.zeros((LANES,), F32)] * 6)
    wo_1 = _attn_o_rows(attn_w_out[0]).astype(BF16)

    y, g, gt, *states, wi_0, w2_0 = _inproj_sweep(*xs, ab_0, w_in_t, gb, gbt, cos_t, sin_t, lgk, cw,
                                                  ffn_w, nb, tpb)
    x_mid, y_attn, wi_1, w2_1 = _mixer_ffn(y, g, gt, states, lg.reshape(-1), lgk, lgk.T, nw, xs, prm_0, wo_0,
                                           wi_0, w2_0, ab_1, attn_w_in, nwq, cos_t, sin_t, ffn_w, nb, tpb)
    out = _attn_ffn(y_attn, attn_sink[0].astype(F32) * LOG2E, x_mid, prm_1, wo_1, wi_1, w2_1, nb, nc, ctx_chunks)
    return out.reshape(nb, seq, d)
```

```python
import functools

import numpy as np
import jax
import jax.numpy as jnp
from jax import lax
from jax.experimental import pallas as pl
from jax.experimental.pallas import tpu as pltpu

F32 = jnp.float32
BF16 = jnp.bfloat16

HEAD_DIM = 64
CHUNK = 128
GRID_W = 64
ROPE_BASE = 10000.0
EPS = 1e-6
H_KV = 4
LANES = 128
ROW_TILE = 256
HALO = 16
LOG2E = 1.4426950408889634
NEG = -1e30
VMEM_LIMIT = 58 * 1024 * 1024


def _cparams(*sem):
    return pltpu.CompilerParams(dimension_semantics=sem, vmem_limit_bytes=VMEM_LIMIT)


def _const_spec(shape):
    nd = len(shape)
    return pl.BlockSpec(shape, lambda *_: (0,) * nd, pipeline_mode=pl.Buffered(1))


def _smem_spec():
    return pl.BlockSpec(memory_space=pltpu.SMEM)


def _wcast_specs(w_in_all, w_out_all, layer, steps):
    specs_in, specs_out, shapes = [], [], []
    for w in (w_in_all, w_out_all):
        total, cols = w.shape[1:]
        rows = next(r for r in range(HALO, total + 1, HALO) if total % r == 0 and total // r <= steps)
        n_blocks = total // rows
        specs_in.append(pl.BlockSpec((None, rows, cols),
                                     lambda i, n=n_blocks: (layer, jnp.minimum(i, n - 1), 0)))
        specs_out.append(pl.BlockSpec((rows, cols), lambda i, n=n_blocks: (jnp.minimum(i, n - 1), 0)))
        shapes.append(jax.ShapeDtypeStruct(w.shape[1:], BF16))
    return specs_in, specs_out, shapes


def _wcast(src_refs, dst_refs):
    for src, dst in zip(src_refs, dst_refs):
        dst[...] = src[...].astype(BF16)


def _lane(shape=(CHUNK, LANES)):
    return lax.broadcasted_iota(jnp.int32, shape, len(shape) - 1)


def _dot(a, b):
    return jnp.dot(a, b, preferred_element_type=F32)


def _dot_nt(a, b):
    return lax.dot_general(a, b, (((1,), (1,)), ((), ())), preferred_element_type=F32)


def _div_small(t, m, n):
    q = 0
    for b in range(1, n):
        q = q + jnp.where(t >= b * m, 1, 0)
    return q, t - q * m


def _split3(x):
    hi = x.astype(BF16)
    r = x - hi.astype(F32)
    mid = r.astype(BF16)
    lo = (r - mid.astype(F32)).astype(BF16)
    return hi, mid, lo


def _log_sigmoid(x):
    return jnp.minimum(x, 0.0) - jnp.log1p(jnp.exp(-jnp.abs(x)))


def _rope(x, cos, sin_signed):
    return x * cos + pltpu.roll(x, LANES // 2, 1) * sin_signed


def _mod_kernel(rows_ref, w_ref, b_ref, o_ref):
    a = rows_ref[...]
    a = a * jax.nn.sigmoid(a)
    a_hi = a.astype(BF16)
    a_lo = (a - a_hi.astype(F32)).astype(BF16)
    w = w_ref[...]
    w_hi = w.astype(BF16)
    w_lo = (w - w_hi.astype(F32)).astype(BF16)
    o_ref[...] = _dot(a_hi, w_hi) + _dot(a_hi, w_lo) + _dot(a_lo, w_hi) + b_ref[...]


def _modulation(rows, ada_w, ada_b):
    depth, d, n = ada_w.shape
    tn = n // 4
    return pl.pallas_call(
        _mod_kernel,
        grid=(depth, n // tn),
        in_specs=[pl.BlockSpec((8, d), lambda l, j: (0, 0)),
                  pl.BlockSpec((None, d, tn), lambda l, j: (l, 0, j)),
                  pl.BlockSpec((None, 1, tn), lambda l, j: (l, 0, j))],
        out_specs=pl.BlockSpec((None, 8, tn), lambda l, j: (l, 0, j)),
        out_shape=jax.ShapeDtypeStruct((depth, 8, n), F32),
        compiler_params=_cparams("arbitrary", "arbitrary"),
        name="adaln_modulation",
    )(rows, ada_w, ada_b.reshape(depth, 1, n))


def _norm_mod(x, ab_ref):
    ms = jnp.mean(x * x, axis=-1, keepdims=True)
    h = (x * lax.rsqrt(ms + EPS)) * ab_ref[0:1, :] + ab_ref[1:2, :]
    return h.astype(BF16)


def _ctx_or_latent_rows(ctx_ref, x_ref, tile, nb, tiles_per_batch):
    return jnp.where(_div_small(tile, tiles_per_batch, nb)[1] == 0, ctx_ref[...], x_ref[...])


def _split_row_specs(tm, d, nb, tpb, tile_of):
    lat = tpb - 1

    def latent_row(i):
        b, j = _div_small(tile_of(i), tpb, nb)
        return b * lat + jnp.maximum(j - 1, 0), 0

    return [pl.BlockSpec((tm, d), lambda i: (_div_small(tile_of(i), tpb, nb)[0], 0)),
            pl.BlockSpec((tm, d), latent_row)]


def _cumsum_cols(tri_bf, lf):
    hi, mid, lo = _split3(lf)
    return _dot(tri_bf, hi) + _dot(tri_bf, mid) + _dot(tri_bf, lo)


def _cumsum_rows(lf, tri_bf):
    hi, mid, lo = _split3(lf)
    return _dot(hi, tri_bf) + _dot(mid, tri_bf) + _dot(lo, tri_bf)


def _ret_state_update(s_ref, p, k2, vt, kdec, cd_lanes, bd):
    kf = (k2.astype(F32) * kdec).astype(BF16)
    s_ref[p] = s_ref[p] * cd_lanes + jnp.where(bd, _dot(vt, kf), 0.0)


def _mlstm_state_update(c_ref, n_ref, m_ref, k_pairs, vt_pairs, c_all, bend, col0, lo, bd):
    cmax = jnp.max(c_all, axis=0, keepdims=True)
    w_all = jnp.exp(c_all - cmax)
    m_old = m_ref[0:1, :]
    mrel = jnp.maximum(m_old, cmax)
    a_row = jnp.exp(m_old - mrel)
    bb_row = jnp.exp(cmax - mrel)
    m_ref[0:1, :] = bend + mrel
    lo_row = lo[0:1, :]
    for p in range(4):
        h0 = col0 + 2 * p
        kw = k_pairs[p] * jnp.where(lo, w_all[:, h0:h0 + 1], w_all[:, h0 + 1:h0 + 2])
        kvt = _dot(vt_pairs[p], kw.astype(BF16))
        nloc = jnp.sum(kw, axis=0, keepdims=True)
        a_l = jnp.where(lo_row, a_row[:, h0:h0 + 1], a_row[:, h0 + 1:h0 + 2])
        bb_l = jnp.where(lo_row, bb_row[:, h0:h0 + 1], bb_row[:, h0 + 1:h0 + 2])
        c_ref[p] = c_ref[p] * a_l + jnp.where(bd, kvt, 0.0) * bb_l
        n_new = (n_ref[p, 0:1, :] + n_ref[p, 1:2, :]) * a_l + nloc * bb_l
        n_ref[p, 0:1, :] = jnp.where(lo_row, n_new, 0.0)
        n_ref[p, 1:2, :] = jnp.where(lo_row, 0.0, n_new)


def _mlstm_dir_weights(st, qn_row, c_col, bt_row, m_prev, tri):
    dl = jnp.where(tri, c_col + bt_row, NEG)
    mx = jnp.max(dl, axis=0, keepdims=True)
    al = bt_row + m_prev
    m_t = jnp.maximum(al, mx)
    w = jnp.exp2(dl - m_t)
    a_t = jnp.exp2(al - m_t)
    sw = st * w
    den = jnp.sum(sw, axis=0, keepdims=True) + a_t * qn_row
    r = 1.0 / jnp.maximum(jnp.abs(den), jnp.exp2(-m_t))
    return sw * r, a_t * r


def _heads_out(ht, nw_tab):
    rows = []
    for a in range(2):
        ha = ht[a * HEAD_DIM:(a + 1) * HEAD_DIM, :]
        ms = jnp.mean(ha * ha, axis=0, keepdims=True)
        rows.append(ha * lax.rsqrt(ms + EPS))
    return (jnp.concatenate(rows, axis=0) * nw_tab).T


def _inproj_sweep_kernel(ctx_ref, x_ref, ab_ref, w_hbm, wgate_ref, gb_ref, gbt_ref, cos_ref, sin_ref,
                         lgk_ref, cw_ref, wi_src, w2_src,
                         y_ref, g_ref, gt_ref, sret_ref, cm_ref, nm_ref, mm_ref, qk_ref, wi_dst, w2_dst,
                         s_scr, c_scr, n_scr, m_scr, kdec_scr,
                         p_rk, p_rv, p_mv, p_mqk, p_g, next_row_scr, w_scr, wg_scr, wgt_scr, w_ref, w_sem, *, tiles, tpb):
    i = pl.program_id(0)
    _wcast((wi_src, w2_src), (wi_dst, w2_dst))

    def order(t):
        v = _div_small(t, tpb, tiles // tpb)[1]
        return jnp.where(v == 0, 0, tpb - v)

    jt_a = order(jnp.maximum(i - 1, 0))
    lane = _lane()
    sub = lax.broadcasted_iota(jnp.int32, (CHUNK, LANES), 0)
    lo = lane < HEAD_DIM
    lo_row = lo[0:1, :]
    bd_ret = (sub >= HEAD_DIM) == ((lane & 32) != 0)
    bd_m = (sub >= HEAD_DIM) == (lane >= HEAD_DIM)
    gw = 4 * LANES

    @pl.when(i == 0)
    def _():
        w_copies = [pltpu.make_async_copy(w_hbm.at[j * gw:(j + 1) * gw, :], w_ref.at[j * gw:(j + 1) * gw, :],
                                          w_sem.at[j]) for j in range(8)]
        for cp in w_copies:
            cp.start()
        pos = sub.astype(F32)
        for p in range(4):
            kdec_scr[p] = jnp.exp(lgk_ref[4 + p:5 + p, :] * pos)
        for ref in (p_rk, p_rv, p_mv, p_mqk, p_g, next_row_scr):
            ref[...] = jnp.zeros_like(ref)
        for grp in range(8 * gw // LANES):
            if grp % (gw // LANES) == 0:
                w_copies[grp // (gw // LANES)].wait()
            quarters = (0, 64, 32, 96) if grp < 2 * gw // LANES else (0, 32, 64, 96)
            w_t = jnp.concatenate([w_ref[grp * LANES + a:grp * LANES + a + 32, :] for a in quarters], axis=0)
            w_scr[:, grp * LANES:(grp + 1) * LANES] = w_t.T.astype(BF16)
        gate_row = lax.broadcasted_iota(jnp.int32, wgate_ref.shape, 0)
        gate_t = jnp.where(gate_row < 32, wgate_ref[...], 0.0)
        wg_scr[...] = gate_t.T.astype(BF16)
        wgt_scr[...] = gate_t[0:32, :].astype(BF16)

    @pl.when(jt_a == 0)
    def _():
        s_scr[...] = jnp.zeros_like(s_scr)
        c_scr[...] = jnp.zeros_like(c_scr)
        n_scr[...] = jnp.zeros_like(n_scr)
        m_scr[...] = jnp.zeros_like(m_scr)

    tile_i = jnp.minimum(i, tiles - 1)
    jt_i = order(tile_i)
    hb = _norm_mod(jnp.where(jt_i == 0, ctx_ref[...], x_ref[...]), ab_ref)
    cos, sin = cos_ref[...], sin_ref[...]

    def project(j):
        w_cols = w_scr[:, j * gw:(j + 1) * gw]
        acc = _dot(hb, w_cols)
        if j in (0, 1):
            if j == 0:
                acc = acc * (HEAD_DIM ** -0.5)
            acc = jnp.concatenate([_rope(acc[:, p * LANES:(p + 1) * LANES], cos, sin) for p in range(4)], axis=1)
        elif j == 3:
            acc = acc * jax.nn.sigmoid(acc)
        elif j == 7:
            acc = jax.nn.sigmoid(acc)
        return acc.astype(BF16)

    raw_q, raw_k = project(4), project(5)

    prev_on = jnp.where(jt_a <= 1, 0.0, 1.0).astype(F32)
    next_on = jnp.where((jt_a == 0) | (jt_a == tpb - 1), 0.0, 1.0).astype(F32)
    cur = p_mqk[...].astype(F32)
    n = cur.shape[0]
    row = lax.broadcasted_iota(jnp.int32, cur.shape, 0)
    prev_row = jnp.concatenate([raw_q[n - HALO:, :], raw_k[n - HALO:, :]], axis=1)[HALO - 1:HALO, :].astype(F32)
    xm = jnp.where(row == 0, prev_row * prev_on, pltpu.roll(cur, 1, 0))
    xp = jnp.where(row == n - 1, next_row_scr[0:1, :] * next_on, pltpu.roll(cur, n - 1, 0))
    conv = cw_ref[3:4, :] + cw_ref[0:1, :] * xm + cw_ref[1:2, :] * cur + cw_ref[2:3, :] * xp
    qk = conv * jax.nn.sigmoid(conv)
    qk_ref[:, 0:gw] = (qk[:, 0:gw] * (HEAD_DIM ** -0.5)).astype(BF16)
    qk_ref[:, gw:] = qk[:, gw:].astype(BF16)
    le_bf = (sub <= lane).astype(BF16)
    chunks = (1, 0)
    pre = {}
    for blk in chunks:
        rows = slice(blk * CHUNK, (blk + 1) * CHUNK)
        g = p_g[rows, :]
        bal = pltpu.roll(_cumsum_cols(le_bf, _log_sigmoid(g)), LANES - 8, 1)
        c_all = g - bal
        cmax = jnp.max(c_all, axis=0, keepdims=True)
        w_all = jnp.exp(c_all - cmax)
        vts, ks, nlocs = [], [], []
        for p in range(8):
            sl = slice((p % 4) * LANES, (p % 4 + 1) * LANES)
            if p < 4:
                v2 = p_rv[rows, sl]
                ks.append((p_rk[rows, sl].astype(F32) * kdec_scr[p]).astype(BF16))
            else:
                v2 = p_mv[rows, sl]
                h0 = 16 + 2 * (p - 4)
                kw = qk[rows, gw + (p - 4) * LANES:gw + (p - 3) * LANES] * jnp.where(
                    lo, w_all[:, h0:h0 + 1], w_all[:, h0 + 1:h0 + 2])
                ks.append(kw.astype(BF16))
                nlocs.append(jnp.sum(kw, axis=0, keepdims=True))
            vts.append(v2.astype(F32).T.astype(BF16))
        pre[blk] = (vts, ks, nlocs, cmax, bal[0:1, :])

    cur_rk = project(1)
    y_ref[:, 1 * gw:2 * gw] = cur_rk
    y_ref[:, 0:gw] = project(0)
    kvs = {blk: [_dot(pre[blk][0][p], pre[blk][1][p]) for p in range(8)] for blk in chunks}
    cur_rv = project(2)
    y_ref[:, 2 * gw:3 * gw] = cur_rv
    y_ref[:, 3 * gw:4 * gw] = project(3)

    for blk in chunks:
        _, _, nlocs, cmax, bend = pre[blk]
        sret_ref[blk] = s_scr[...].astype(BF16)
        cm_ref[blk] = c_scr[...].astype(BF16)
        nm_ref[blk] = n_scr[...].astype(BF16)
        mm_ref[blk] = m_scr[...]
        m_old = m_scr[0:1, :]
        mrel = jnp.maximum(m_old, cmax)
        a_row = jnp.exp(m_old - mrel)
        bb_row = jnp.exp(cmax - mrel)
        m_scr[0:1, :] = bend + mrel
        for p in range(4):
            cd = jnp.exp(lgk_ref[4 + p:5 + p, :] * float(CHUNK))
            s_scr[p] = s_scr[p] * cd + jnp.where(bd_ret, kvs[blk][p], 0.0)
            h0 = 16 + 2 * p
            a_l = jnp.where(lo_row, a_row[:, h0:h0 + 1], a_row[:, h0 + 1:h0 + 2])
            bb_l = jnp.where(lo_row, bb_row[:, h0:h0 + 1], bb_row[:, h0 + 1:h0 + 2])
            c_scr[p] = c_scr[p] * a_l + jnp.where(bd_m, kvs[blk][4 + p], 0.0) * bb_l
            n_new = (n_scr[p, 0:1, :] + n_scr[p, 1:2, :]) * a_l + nlocs[p] * bb_l
            n_scr[p, 0:1, :] = jnp.where(lo_row, n_new, 0.0)
            n_scr[p, 1:2, :] = jnp.where(lo_row, 0.0, n_new)

    cur_mv = project(6)
    y_ref[:, 4 * gw:5 * gw] = cur_mv
    y_ref[:, 5 * gw:6 * gw] = project(7)
    gates = _dot(hb, wg_scr[...]) + gb_ref[...]
    g_ref[...] = gates
    gt_ref[...] = _dot_nt(wgt_scr[...], hb) + gbt_ref[...]
    next_row_scr[...] = p_mqk[0:HALO, :].astype(F32)
    p_mqk[:, 0:gw] = raw_q
    p_mqk[:, gw:] = raw_k
    p_rk[...] = cur_rk
    p_rv[...] = cur_rv
    p_mv[...] = cur_mv
    p_g[...] = gates


def _inproj_sweep(ctx2, x2, ab, w_in, gb, gbt, cos, sin, lgk, cw, ffn_w, nb, tpb):
    d = x2.shape[1]
    tm = ROW_TILE
    cpt = tm // CHUNK
    gw = 4 * LANES
    tiles = nb * tpb
    r = tiles * tm
    lat = tpb - 1
    def visit(t):
        b, v = _div_small(t, tpb, nb)
        return b, jnp.where(v == 0, 0, tpb - v)

    cur = lambda i: visit(jnp.minimum(i, tiles - 1))
    flat = lambda bj: bj[0] * tpb + bj[1]
    tile_i = lambda i: flat(cur(i))
    tile_a = lambda i: flat(visit(jnp.maximum(i - 1, 0)))
    in_batch = lambda i: cur(i)[1]
    sel = lambda i: (cur(i)[0] * 2 + jnp.minimum(cur(i)[1], 1), 0, 0)
    state = lambda *dims: pl.BlockSpec((cpt,) + dims, lambda i: (tile_a(i),) + (0,) * len(dims))
    nchunks = tiles * cpt
    wc_in, wc_out, wc_shapes = _wcast_specs(*ffn_w, 0, tiles + 1)
    return pl.pallas_call(
        functools.partial(_inproj_sweep_kernel, tiles=tiles, tpb=tpb),
        grid=(tiles + 1,),
        in_specs=[pl.BlockSpec((tm, d), lambda i: (cur(i)[0], 0)),
                  pl.BlockSpec((tm, d), lambda i: (cur(i)[0] * lat + jnp.maximum(cur(i)[1] - 1, 0), 0)),
                  pl.BlockSpec((None, 2, d), sel),
                  pl.BlockSpec(memory_space=pl.ANY),
                  pl.BlockSpec((LANES, d), lambda i: (8 * gw // LANES, 0), pipeline_mode=pl.Buffered(1)),
                  _const_spec(gb.shape), _const_spec(gbt.shape),
                  pl.BlockSpec((tm, LANES), lambda i: (in_batch(i), 0)),
                  pl.BlockSpec((tm, LANES), lambda i: (in_batch(i), 0)),
                  _const_spec(lgk.shape), _const_spec(cw.shape)] + wc_in,
        out_specs=[pl.BlockSpec((tm, 6 * gw), lambda i: (tile_i(i), 0)),
                   pl.BlockSpec((tm, LANES), lambda i: (tile_i(i), 0)),
                   pl.BlockSpec((32, tm), lambda i: (0, tile_i(i))),
                   state(4, CHUNK, LANES), state(4, CHUNK, LANES), state(4, HALO, LANES), state(8, LANES),
                   pl.BlockSpec((tm, 2 * gw), lambda i: (tile_a(i), 0))] + wc_out,
        out_shape=[jax.ShapeDtypeStruct((r, 6 * gw), BF16),
                   jax.ShapeDtypeStruct((r, LANES), F32),
                   jax.ShapeDtypeStruct((32, r), F32),
                   jax.ShapeDtypeStruct((nchunks, 4, CHUNK, LANES), BF16),
                   jax.ShapeDtypeStruct((nchunks, 4, CHUNK, LANES), BF16),
                   jax.ShapeDtypeStruct((nchunks, 4, HALO, LANES), BF16),
                   jax.ShapeDtypeStruct((nchunks, 8, LANES), F32),
                   jax.ShapeDtypeStruct((r, 2 * gw), BF16)] + wc_shapes,
        scratch_shapes=[pltpu.VMEM((4, CHUNK, LANES), F32), pltpu.VMEM((4, CHUNK, LANES), F32),
                        pltpu.VMEM((4, HALO, LANES), F32), pltpu.VMEM((8, LANES), F32),
                        pltpu.VMEM((4, CHUNK, LANES), F32),
                        pltpu.VMEM((tm, gw), BF16), pltpu.VMEM((tm, gw), BF16), pltpu.VMEM((tm, gw), BF16),
                        pltpu.VMEM((tm, 2 * gw), BF16), pltpu.VMEM((tm, LANES), F32),
                        pltpu.VMEM((HALO, 2 * gw), F32),
                        pltpu.VMEM((d, 8 * gw), BF16), pltpu.VMEM((d, LANES), BF16), pltpu.VMEM((32, d), BF16),
                        pltpu.VMEM((8 * gw, d), F32), pltpu.SemaphoreType.DMA((8,))],
        compiler_params=_cparams("arbitrary"),
        name="inproj_bwd_sweep",
    )(ctx2, x2, ab, w_in, w_in, gb, gbt, cos, sin, lgk, cw, *ffn_w)


class _Bag:
    def __init__(self, **kw):
        self.__dict__.update(kw)


def _mixer_chunk_stages(blk, r):
    rows = slice(blk * CHUNK, (blk + 1) * CHUNK)
    lane = _lane()
    sub = lax.broadcasted_iota(jnp.int32, (CHUNK, LANES), 0)
    lo = lane < HEAD_DIM
    sub_lo = sub < HEAD_DIM
    mask_ret = [(lane & 32) == 0, (lane & 32) != 0]
    mask_nat = [lo, lane >= HEAD_DIM]
    bd_ret = (sub >= HEAD_DIM) == ((lane & 32) != 0)
    bd_m = (sub >= HEAD_DIM) == (lane >= HEAD_DIM)
    le = sub <= lane
    ge = sub >= lane

    g = r.g_ref[rows, :]
    gt = r.gt_ref[:, rows]
    lf_col = _log_sigmoid(g)
    lf_row = _log_sigmoid(gt)
    le_bf = le.astype(BF16)
    ge_bf = ge.astype(BF16)
    pre_col = _cumsum_cols(ge_bf, lf_col)
    bal_f = pltpu.roll(pre_col, LANES - 8, 1)
    bal_b = pltpu.roll(pre_col[CHUNK - 1:CHUNK, :] - pre_col + lf_col, LANES - 8, 1)
    cf_all = g - bal_f
    cb_all = g - bal_b
    bf_row = _cumsum_rows(lf_row, le_bf)
    bb_row = bf_row[:, CHUNK - 1:CHUNK] - bf_row + lf_row
    qb, kb, vts, kf32, qt, vbd = [], [], [], [], [], []
    for p in range(8):
        sl = slice((p % 4) * LANES, (p % 4 + 1) * LANES)
        if p < 4:
            q2, k2, v2 = r.rq_ref[rows, sl], r.rk_ref[rows, sl], r.rv_ref[rows, sl]
            qf = q2.astype(F32)
            kf = None
        else:
            q2 = r.qk_ref[rows, sl]
            k2 = r.qk_ref[rows, 4 * LANES + (p - 4) * LANES:4 * LANES + (p - 3) * LANES]
            v2 = r.mv_ref[rows, sl]
            qf, kf = q2.astype(F32), k2.astype(F32)
        vt = v2.astype(F32).T.astype(BF16)
        qb.append(q2)
        kb.append(k2)
        vts.append(vt)
        kf32.append(kf)
        qt.append(qf.T)
        vbd.append([jnp.where(sub_lo, vt, jnp.zeros_like(vt)), jnp.where(sub_lo, jnp.zeros_like(vt), vt)])
    yield

    st2 = []
    for p in range(8):
        masks = mask_ret if p < 4 else mask_nat
        zero = jnp.zeros_like(qb[p])
        qstack = jnp.concatenate([jnp.where(masks[0], qb[p], zero), jnp.where(masks[1], qb[p], zero)], axis=0)
        st2.append(_dot_nt(kb[p], qstack))
    yield

    qn = [_dot_nt(jnp.concatenate([r.n_scr[p].astype(BF16), r.nmb_ref[blk, p]], axis=0), qb[4 + p])
          for p in range(4)]
    qn_f = [x[0:HALO] for x in qn]
    qn_b = [x[HALO:2 * HALO] for x in qn]
    cf2, cb2, bf2, bb2 = cf_all * LOG2E, cb_all * LOG2E, bf_row * LOG2E, bb_row * LOG2E
    m_f2, m_b2 = r.m_state[0:1, :] * LOG2E, r.mmb_ref[blk, 0:1, :] * LOG2E
    lhs, rhs = [], []
    for p in range(8):
        if p < 4:
            pts = [(st2[p][:, a * LANES:(a + 1) * LANES] * r.dm_scr[2 * p + a]).astype(BF16) for a in range(2)]
            x_f, x_b = r.dec_scr[0, p], r.dec_scr[1, p]
            old = [r.s_scr[p].astype(BF16), r.sretb_ref[blk, p]]
        else:
            pts, cf, cb = [], [], []
            for a in range(2):
                h = 2 * (p - 4) + a
                st = st2[p][:, a * LANES:(a + 1) * LANES]
                pf, coef_f = _mlstm_dir_weights(st, qn_f[p - 4][a:a + 1, :], cf2[:, h:h + 1],
                                                bf2[8 + h:9 + h, :], m_f2[0:1, h:h + 1], le)
                pb, coef_b = _mlstm_dir_weights(st, qn_b[p - 4][a:a + 1, :], cb2[:, 16 + h:17 + h],
                                                bb2[24 + h:25 + h, :], m_b2[0:1, 16 + h:17 + h], ge)
                pts.append((pf + pb).astype(BF16))
                cf.append(coef_f)
                cb.append(coef_b)
            x_f = jnp.where(sub_lo, cf[0], cf[1])
            x_b = jnp.where(sub_lo, cb[0], cb[1])
            old = [r.c_scr[p - 4].astype(BF16), r.cmb_ref[blk, p - 4]]
        lhs.append(jnp.concatenate(vbd[p] + old, axis=1))
        rhs.append(jnp.concatenate(pts + [(qt[p] * x_f).astype(BF16), (qt[p] * x_b).astype(BF16)], axis=0))
    yield

    ht = [_dot(lhs[p], rhs[p]) for p in range(8)]
    yield

    for p in range(8):
        sl = slice((p % 4) * LANES, (p % 4 + 1) * LANES)
        gate_ref = r.rg_ref if p < 4 else r.mo_ref
        y = _heads_out(ht[p], r.nw_ref[p])
        new = (y * gate_ref[rows, sl].astype(F32)).astype(BF16)
        r.mix_scr[rows, p * LANES:(p + 1) * LANES] = jnp.where(r.live, new, r.mix_scr[rows, p * LANES:(p + 1) * LANES])
    yield

    for p in range(4):
        cd = jnp.exp(r.lgk_ref[p:p + 1, :] * float(CHUNK))
        _ret_state_update(r.s_scr, p, kb[p], vts[p], r.dec_scr[2, p], cd, bd_ret)
    _mlstm_state_update(r.c_scr, r.n_scr, r.m_state, kf32[4:], vts[4:], cf_all, bal_f[CHUNK - 1:CHUNK, :],
                        0, lo, bd_m)
    yield


def _attn_inproj_stages(x, ab_ref, wq_ref, wk_ref, wv_ref, nw_ref, cos, sin, y_ref):
    hb = _norm_mod(x, ab_ref)
    r2 = lax.broadcasted_iota(jnp.int32, (2 * LANES, 2 * LANES), 0)
    c2 = lax.broadcasted_iota(jnp.int32, (2 * LANES, 2 * LANES), 1)
    same_head = (((r2 ^ c2) & (LANES | 32)) == 0).astype(BF16)
    acc_q = _dot(hb, wq_ref[...])
    acc_k = _dot(hb, wk_ref[...])
    y_ref[:, 10 * LANES:12 * LANES] = _dot(hb, wv_ref[...]).astype(BF16)
    yield
    for j in range(5):
        acc = acc_q[:, j * 2 * LANES:(j + 1) * 2 * LANES] if j < 4 else acc_k
        sq = acc * acc
        ms = _dot(sq.astype(BF16), same_head) * (1.0 / HEAD_DIM)
        nrm = acc * lax.rsqrt(ms + EPS)
        nw = nw_ref[0:1, :] if j < 4 else nw_ref[1:2, :]
        for v in range(2):
            ls = slice(v * LANES, (v + 1) * LANES)
            y_ref[:, (2 * j + v) * LANES:(2 * j + v + 1) * LANES] = _rope(nrm[:, ls] * nw, cos, sin).astype(BF16)
        if j in (1, 4):
            yield


def _ffn_splits(d_ff, pieces):
    blocks = d_ff // (2 * LANES)
    assert blocks * 2 * LANES == d_ff and blocks >= pieces
    cuts = [((i * blocks) // pieces) * 2 * LANES for i in range(pieces)]
    return cuts + [d_ff]


def _attn_weight_copies(wa_hbm, wa_src, sem):
    width = 4 * LANES
    return [pltpu.make_async_copy(wa_hbm.at[:, c * width:(c + 1) * width], wa_src.at[:, c * width:(c + 1) * width],
                                  sem.at[c]) for c in range(wa_src.shape[1] // width)]


def _prepare_attn_weights(copies, wa_src, wa_scr):
    rows = wa_src.shape[0]
    d = wa_src.shape[1] - 4 * LANES
    quarter = _lane((rows, LANES)) // 32
    group = lambda ref, grp: ref[:, grp * LANES:(grp + 1) * LANES]
    roll = lambda v, shift: pltpu.roll(v, shift % LANES, 1) if shift % LANES else v
    g_per = d // (H_KV * HEAD_DIM)
    assert len(copies) == H_KV // 2 + 1 and 2 * g_per * HEAD_DIM == 4 * LANES
    for pair in range(H_KV // 2):
        copies[pair].wait()
        for g in range(g_per):
            h0 = (g * HEAD_DIM) % LANES
            src = lambda kv: group(wa_src, (kv * g_per * HEAD_DIM + g * HEAD_DIM) // LANES)
            head_a, head_b = src(2 * pair), src(2 * pair + 1)
            out = jnp.where(quarter == 0, roll(head_a, -h0),
                            jnp.where(quarter == 1, roll(head_b, 32 - h0),
                                      jnp.where(quarter == 2, roll(head_a, 32 - h0), roll(head_b, 64 - h0))))
            wa_scr[:, (pair * g_per + g) * LANES:(pair * g_per + g + 1) * LANES] = out.astype(BF16)
    copies[-1].wait()
    for grp in range(d // LANES, d // LANES + H_KV // 2):
        w = group(wa_src, grp)
        out = jnp.where(quarter == 1, roll(w, -32), jnp.where(quarter == 2, roll(w, 32), w))
        wa_scr[:, grp * LANES:(grp + 1) * LANES] = out.astype(BF16)
    wa_scr[:, d + 2 * LANES:] = wa_src[:, d + 2 * LANES:].astype(BF16)


def _mixer_ffn_kernel(lg_ref, y_ref, qk_ref,
                      g_ref, gt_ref, sretb_ref, cmb_ref, nmb_ref, mmb_ref,
                      lgk_ref, lgkt_ref, nw_ref,
                      ctx_ref, x_ref, p_ref, wo_ref, wi_ref, w2_ref,
                      ab2_ref, wa_hbm, nwa_ref, cos_ref, sin_ref, wi_src, w2_src,
                      o_ref, y2_ref, wi_dst, w2_dst,
                      s_scr, c_scr, n_scr, m_state, dm_scr, dec_scr, mix_scr, act_scr, x2_scr, wa_scr, wa_src, wa_sem,
                      *, tiles, tpb, d_ff):
    s = pl.program_id(0)
    _wcast((wi_src, w2_src), (wi_dst, w2_dst))
    nb = tiles // tpb
    jt = _div_small(jnp.minimum(s, tiles - 1), tpb, nb)[1]
    gw = 4 * LANES
    rq_ref, rk_ref, rv_ref, rg_ref, mv_ref, mo_ref = [y_ref.at[:, j * gw:(j + 1) * gw] for j in range(6)]
    lane = _lane()
    sub = lax.broadcasted_iota(jnp.int32, (CHUNK, LANES), 0)

    @pl.when(s == 0)
    def _():
        wa_copies = _attn_weight_copies(wa_hbm, wa_src, wa_sem)
        for cp in wa_copies:
            cp.start()
        mix_scr[...] = jnp.zeros_like(mix_scr)
        x2_scr[...] = jnp.zeros_like(x2_scr)
        le = sub <= lane
        ge = sub >= lane
        spos = sub.astype(F32)
        tpos = lane.astype(F32)
        diff = (lane - sub).astype(F32)
        for h in range(8):
            dm_scr[h] = (jnp.where(le, jnp.exp(lg_ref[h] * diff), 0.0)
                         + jnp.where(ge, jnp.exp(lg_ref[8 + h] * (-diff)), 0.0))
        for p in range(4):
            dec_scr[0, p] = jnp.exp(lgkt_ref[:, p:p + 1] * (tpos + 1.0))
            dec_scr[1, p] = jnp.exp(lgkt_ref[:, 4 + p:5 + p] * (float(CHUNK) - tpos))
            dec_scr[2, p] = jnp.exp(lgk_ref[p:p + 1, :] * (float(CHUNK) - 1.0 - spos))
        _prepare_attn_weights(wa_copies, wa_src, wa_scr)

    @pl.when(jt == 0)
    def _():
        s_scr[...] = jnp.zeros_like(s_scr)
        c_scr[...] = jnp.zeros_like(c_scr)
        n_scr[...] = jnp.zeros_like(n_scr)
        m_state[...] = jnp.zeros_like(m_state)

    r = _Bag(rq_ref=rq_ref, rk_ref=rk_ref, rv_ref=rv_ref, rg_ref=rg_ref, qk_ref=qk_ref, mv_ref=mv_ref, mo_ref=mo_ref,
             g_ref=g_ref, gt_ref=gt_ref, sretb_ref=sretb_ref, cmb_ref=cmb_ref, nmb_ref=nmb_ref, mmb_ref=mmb_ref,
             lgk_ref=lgk_ref, nw_ref=nw_ref, s_scr=s_scr, c_scr=c_scr, n_scr=n_scr, m_state=m_state,
             dm_scr=dm_scr, dec_scr=dec_scr, mix_scr=mix_scr, live=s < tiles)
    cuts = _ffn_splits(d_ff, 3)
    ffn_piece = lambda i: _ffn_cols(h, wi_ref, act_scr, cuts[i], cuts[i + 1], d_ff)

    x = _ctx_or_latent_rows(ctx_ref, x_ref, jnp.clip(s - 1, 0, tiles - 1), nb, tpb)
    x1, h = _ffn_in(x, mix_scr[...], p_ref, wo_ref)

    d = x2_scr.shape[1]
    nxt = _attn_inproj_stages(x2_scr[...], ab2_ref, wa_scr.at[:, 0:d], wa_scr.at[:, d:d + 2 * LANES],
                              wa_scr.at[:, d + 2 * LANES:], nwa_ref, cos_ref[...], sin_ref[...], y2_ref)
    chunk_a, chunk_b = _mixer_chunk_stages(0, r), _mixer_chunk_stages(1, r)
    next(nxt)
    next(chunk_a), next(chunk_b)
    next(chunk_a), next(chunk_b)
    ffn_piece(0)
    next(chunk_a), next(chunk_a)
    ffn_piece(1)
    next(chunk_a), next(chunk_a)
    next(nxt)
    ffn_piece(2)
    next(chunk_b), next(chunk_b)
    next(nxt)
    x2 = x1 + p_ref[3:4, :] * _dot(act_scr[...], w2_ref[...])
    o_ref[...] = x2
    next(chunk_b), next(chunk_b)
    x2_scr[...] = x2


def _mixer_ffn(y, g, gt, states, lg_smem, lgk, lgkt, nw, xs, prm, wo, wi, w2,
               ab_next, w_next, nw_next, cos, sin, ffn_w, nb, tpb):
    gw = 4 * LANES
    tm = ROW_TILE
    cpt = tm // CHUNK
    sretb, cmb, nmb, mmb, qk_act = states
    tiles = nb * tpb
    d = xs[-1].shape[1]
    d_ff = w2.shape[0]
    w_next = w_next[0]
    n_next = w_next.shape[-1]
    mix = lambda s: jnp.minimum(s, tiles - 1)
    ffn = lambda s: jnp.clip(s - 1, 0, tiles - 1)
    nxt = lambda s: jnp.maximum(s - 2, 0)
    state_spec = lambda a: pl.BlockSpec((cpt,) + a.shape[1:], lambda s: (mix(s),) + (0,) * (a.ndim - 1))
    def sel(t):
        b, j = _div_small(t, tpb, nb)
        return b * 2 + jnp.minimum(j, 1), 0, 0

    wc_in, wc_out, wc_shapes = _wcast_specs(*ffn_w, 1, tiles + 2)
    return pl.pallas_call(
        functools.partial(_mixer_ffn_kernel, tiles=tiles, tpb=tpb, d_ff=d_ff),
        grid=(tiles + 2,),
        in_specs=[_smem_spec(),
                  pl.BlockSpec((tm, 6 * gw), lambda s: (mix(s), 0)),
                  pl.BlockSpec((tm, 2 * gw), lambda s: (mix(s), 0)),
                  pl.BlockSpec((tm, LANES), lambda s: (mix(s), 0)),
                  pl.BlockSpec((32, tm), lambda s: (0, mix(s))),
                  state_spec(sretb), state_spec(cmb), state_spec(nmb), state_spec(mmb),
                  _const_spec(lgk.shape), _const_spec(lgkt.shape), _const_spec(nw.shape)]
                 + _split_row_specs(tm, d, nb, tpb, ffn)
                 + [pl.BlockSpec((None, 8, d), lambda s: sel(ffn(s))),
                    _const_spec(wo.shape), _const_spec(wi.shape), _const_spec(w2.shape),
                    pl.BlockSpec((None, 2, d), lambda s: sel(nxt(s))),
                    pl.BlockSpec(memory_space=pl.ANY),
                    _const_spec(nw_next.shape),
                    pl.BlockSpec((tm, LANES), lambda s: (_div_small(nxt(s), tpb, nb)[1], 0)),
                    pl.BlockSpec((tm, LANES), lambda s: (_div_small(nxt(s), tpb, nb)[1], 0))] + wc_in,
        out_specs=[pl.BlockSpec((tm, d), lambda s: (ffn(s), 0)),
                   pl.BlockSpec((tm, n_next), lambda s: (nxt(s), 0))] + wc_out,
        out_shape=[jax.ShapeDtypeStruct((tiles * tm, d), F32),
                   jax.ShapeDtypeStruct((tiles * tm, n_next), BF16)] + wc_shapes,
        scratch_shapes=[pltpu.VMEM((4, CHUNK, LANES), F32), pltpu.VMEM((4, CHUNK, LANES), F32),
                        pltpu.VMEM((4, HALO, LANES), F32), pltpu.VMEM((8, LANES), F32),
                        pltpu.VMEM((8, CHUNK, LANES), F32), pltpu.VMEM((3, 4, CHUNK, LANES), F32),
                        pltpu.VMEM((tm, 2 * gw), BF16), pltpu.VMEM((tm, d_ff), BF16),
                        pltpu.VMEM((tm, d), F32),
                        pltpu.VMEM(w_next.shape, BF16), pltpu.VMEM(w_next.shape, F32),
                        pltpu.SemaphoreType.DMA((w_next.shape[1] // (4 * LANES),))],
        compiler_params=_cparams("arbitrary"),
        name="ret_mlstm_mixer_ffn",
    )(lg_smem, y, qk_act, g, gt, sretb, cmb, nmb, mmb, lgk, lgkt, nw,
      *xs, prm, wo, wi, w2, ab_next, w_next, nw_next, cos, sin, *ffn_w)


def _ffn_in(x, m, p_ref, wo_ref):
    x1 = x + p_ref[0:1, :] * _dot(m, wo_ref[...])
    ms = jnp.mean(x1 * x1, axis=-1, keepdims=True)
    return x1, ((x1 * lax.rsqrt(ms + EPS)) * p_ref[1:2, :] + p_ref[2:3, :]).astype(BF16)


def _ffn_cols(h, wi_ref, act_scr, lo, hi, d_ff):
    gate = _dot(h, wi_ref[:, lo:hi])
    up = _dot(h, wi_ref[:, d_ff + lo:d_ff + hi])
    act_scr[:, lo:hi] = (gate * jax.nn.sigmoid(gate) * up).astype(BF16)


def _window_bias():
    kk = np.arange(CHUNK)[:, None]
    t = np.arange(CHUNK)[None, :]
    tabs = []
    for has_prev, has_next in ((False, True), (True, True), (True, False)):
        prev_ok = (kk >= t) & has_prev
        next_ok = (kk <= t) & has_next
        tabs.append(np.where(np.concatenate([prev_ok, next_ok], axis=0), 0.0, NEG))
    return jnp.asarray(np.stack(tabs), F32)


def _attn_ffn_kernel(sink_ref, qkv_a_ref, qkv_b_ref, kvp_ref, kvn_ref, kvx_ref, bias_ref,
                     x_a_ref, x_b_ref, p_ref, wo_ref, wi_ref, w2_ref, o_ref, m_scr, act_scr,
                     *, steps, per_batch, d_ff):
    s = pl.program_id(0)
    kw = 2 * LANES
    k0 = 8 * LANES
    tm = qkv_a_ref.shape[0]
    nblk = 2 * tm // CHUNK

    @pl.when(s == 0)
    def _():
        m_scr[...] = jnp.zeros_like(m_scr)

    j = _div_small(jnp.minimum(s, steps - 1), per_batch, steps // per_batch)[1]
    grp = lax.broadcasted_iota(jnp.int32, (1, 4 * CHUNK), 1) // CHUNK
    lane = _lane()
    mask_q = [(lane & 32) == 0, (lane & 32) != 0]
    tile4 = lambda b: jnp.concatenate([b] * 4, axis=1)
    inner = tile4(bias_ref[1])
    biases = ([tile4(jnp.where(j == 0, bias_ref[0], bias_ref[1]))] + [inner] * (nblk - 2)
              + [tile4(jnp.where(j == per_batch - 1, bias_ref[2], bias_ref[1]))])
    cuts = _ffn_splits(d_ff, nblk - 1)

    def ffn_up(i):
        _ffn_cols(h, wi_ref, act_scr, cuts[i], cuts[i + 1], d_ff)

    def q_rows(blk):
        ref = qkv_a_ref if blk * CHUNK < tm else qkv_b_ref
        r0 = (blk * CHUNK) % tm
        return ref, slice(r0, r0 + CHUNK)

    def scores(blk, kv):
        kvp, a = kv // 2, kv % 2
        ref, rows = q_rows(blk)
        qs = jnp.concatenate(
            [jnp.where(mask_q[a], ref[rows, (kvp * 4 + g) * LANES:(kvp * 4 + g + 1) * LANES],
                       jnp.zeros((CHUNK, LANES), BF16)) for g in range(4)], axis=0)
        return _dot_nt(kcats[kvp][blk], qs)

    def softmax_pv(blk, kv, st):
        bias = biases[blk]
        st = jnp.concatenate([st[0:CHUNK] + bias[0:CHUNK], st[CHUNK:2 * CHUNK],
                              st[2 * CHUNK:3 * CHUNK] + bias[CHUNK:2 * CHUNK], st[3 * CHUNK:]], axis=0)
        snk = jnp.where(grp == 0, sink_ref[kv * 4],
                        jnp.where(grp == 1, sink_ref[kv * 4 + 1],
                                  jnp.where(grp == 2, sink_ref[kv * 4 + 2], sink_ref[kv * 4 + 3])))
        m = jnp.maximum(jnp.max(st, axis=0, keepdims=True), snk)
        e = jnp.exp2(st - m)
        denom = jnp.exp2(snk - m) + jnp.sum(e, axis=0, keepdims=True)
        a = kv % 2
        return _dot(vts[kv // 2][blk][a * HEAD_DIM:(a + 1) * HEAD_DIM, :], e.astype(BF16)) * (1.0 / denom)

    def hand_over(blk, outs):
        for kvp in range(2):
            full = jnp.concatenate(outs[2 * kvp:2 * kvp + 2], axis=0)
            for g in range(4):
                m_scr[blk * CHUNK:(blk + 1) * CHUNK, (kvp * 4 + g) * LANES:(kvp * 4 + g + 1) * LANES] = (
                    full[:, g * CHUNK:(g + 1) * CHUNK].T.astype(BF16))

    x1, h = _ffn_in(jnp.concatenate([x_a_ref[...], x_b_ref[...]], axis=0), m_scr[...], p_ref, wo_ref)

    kcats, vts = [], []
    for kvp in range(2):
        ks = slice(k0 + kvp * LANES, k0 + (kvp + 1) * LANES)
        vs = slice(k0 + kw + kvp * LANES, k0 + kw + (kvp + 1) * LANES)
        es, ev = slice(kvp * LANES, (kvp + 1) * LANES), slice(kw + kvp * LANES, kw + (kvp + 1) * LANES)
        tile_rows = [(ref, slice(c * CHUNK, (c + 1) * CHUNK)) for ref in (qkv_a_ref, qkv_b_ref)
                     for c in range(tm // CHUNK)]
        k_chunks = [kvp_ref[:, es]] + [ref[rows, ks] for ref, rows in tile_rows] + [kvn_ref[:, es]]
        v_chunks = ([kvp_ref[:, ev]] + [ref[rows, vs] for ref, rows in tile_rows] + [kvn_ref[:, ev]]
                    + [kvx_ref[c * CHUNK:(c + 1) * CHUNK, ev] for c in range(kvx_ref.shape[0] // CHUNK)])
        v_t = [v.astype(F32).T.astype(BF16) for v in v_chunks]
        kcats.append([jnp.concatenate(k_chunks[b:b + 3] + [kvx_ref[:, es]], axis=0) for b in range(nblk)])
        vts.append([jnp.concatenate(v_t[b:b + 3] + v_t[nblk + 2:], axis=1) for b in range(nblk)])

    for blk in range(nblk):
        sts = [scores(blk, kv) for kv in range(4)]
        if blk < nblk - 1:
            ffn_up(blk)
        else:
            o_ref[...] = x1 + p_ref[3:4, :] * _dot(act_scr[...], w2_ref[...])
        hand_over(blk, [softmax_pv(blk, kv, sts[kv]) for kv in range(4)])


def _attn_ffn(y, sink, xc, prm, wo, wi, w2, nb, nc, ctx_chunks):
    tm = ROW_TILE
    cpt = tm // CHUNK
    tpb = nc // cpt
    lat_tiles = (nc - ctx_chunks) // cpt
    ctx_tiles = ctx_chunks // cpt
    assert ctx_tiles == 1 and lat_tiles % 2 == 0 and lat_tiles >= 4
    per_batch = lat_tiles // 2
    steps = nb * per_batch
    d = xc.shape[1]
    d_ff = w2.shape[0]
    bias = _window_bias()
    n_qkv = y.shape[1]
    kv_w = n_qkv - d
    assert d % kv_w == 0
    kv_col = d // kv_w
    att = lambda s: jnp.minimum(s, steps - 1)
    ffn = lambda s: jnp.maximum(s - 1, 0)
    split = lambda t: _div_small(t, per_batch, nb)
    row_tile = lambda t, u: split(t)[0] * tpb + ctx_tiles + 2 * split(t)[1] + u
    chunk0 = lambda t: split(t)[0] * nc + ctx_chunks
    prev_c = lambda s: chunk0(att(s)) + jnp.maximum(split(att(s))[1] * 2 * cpt - 1, 0)
    next_c = lambda s: chunk0(att(s)) + jnp.minimum(split(att(s))[1] * 2 * cpt + 2 * cpt, lat_tiles * cpt - 1)
    edge_spec = lambda f: pl.BlockSpec((CHUNK, kv_w), lambda s: (f(s), kv_col))
    return pl.pallas_call(
        functools.partial(_attn_ffn_kernel, steps=steps, per_batch=per_batch, d_ff=d_ff),
        grid=(steps + 1,),
        in_specs=[_smem_spec(),
                  pl.BlockSpec((tm, n_qkv), lambda s: (row_tile(att(s), 0), 0)),
                  pl.BlockSpec((tm, n_qkv), lambda s: (row_tile(att(s), 1), 0)),
                  edge_spec(prev_c), edge_spec(next_c),
                  pl.BlockSpec((tm, kv_w), lambda s: (split(att(s))[0] * tpb, kv_col)),
                  _const_spec(bias.shape),
                  pl.BlockSpec((tm, d), lambda s: (row_tile(ffn(s), 0), 0)),
                  pl.BlockSpec((tm, d), lambda s: (row_tile(ffn(s), 1), 0)),
                  pl.BlockSpec((None, 8, d), lambda s: (split(ffn(s))[0] * 2 + 1, 0, 0)),
                  _const_spec(wo.shape), _const_spec(wi.shape), _const_spec(w2.shape)],
        out_specs=pl.BlockSpec((2 * tm, d), lambda s: (ffn(s), 0)),
        out_shape=jax.ShapeDtypeStruct((steps * 2 * tm, d), F32),
        scratch_shapes=[pltpu.VMEM((2 * tm, d), BF16), pltpu.VMEM((2 * tm, d_ff), BF16)],
        compiler_params=_cparams("arbitrary"),
        name="window_gqa_ffn",
    )(sink, y, y, y, y, y, bias, xc, xc, prm, wo, wi, w2)


def _attn_o_rows(w):
    cols = w.shape[1]
    g_per = w.shape[0] // (H_KV * HEAD_DIM)
    return (w.reshape(H_KV // 2, 2, g_per, HEAD_DIM, cols).transpose(0, 2, 1, 3, 4)
            .reshape(w.shape[0], cols))


def _rope_tables(seq, ctx_len):
    rows = seq // GRID_W
    row = np.repeat(np.arange(rows, dtype=np.float32), GRID_W)
    col = np.tile(np.arange(GRID_W, dtype=np.float32), rows)
    n = HEAD_DIM // 4
    inv = (np.float32(ROPE_BASE) ** (-np.arange(n, dtype=np.float32) / np.float32(n))).astype(np.float32)
    ang = np.concatenate([row[:, None] * inv, col[:, None] * inv], axis=-1).astype(np.float32)
    cos, sin = np.cos(ang), np.sin(ang)
    cos_t = np.concatenate([np.ones((ctx_len, LANES), np.float32), np.tile(cos, (1, 4))], axis=0)
    sin_t = np.concatenate([np.zeros((ctx_len, LANES), np.float32),
                            np.concatenate([-sin, -sin, sin, sin], axis=-1)], axis=0)
    return jnp.asarray(cos_t, F32), jnp.asarray(sin_t, F32)


def _mod_tables(mod, nb, norm_w):
    d = norm_w.shape[-1]
    lat = mod[:nb].reshape(nb, 6, d)
    ctx = jnp.broadcast_to(mod[nb].reshape(1, 6, d), (nb, 6, d))
    both = jnp.stack([ctx, lat], axis=1).reshape(nb * 2, 6, d)
    sh1, sc1, g1, sh2, sc2, g2 = [both[:, k] for k in range(6)]
    ab1 = jnp.stack([norm_w[0] * (1.0 + sc1), sh1], axis=1)
    zeros = jnp.zeros_like(g1)
    prm = jnp.stack([g1, norm_w[1] * (1.0 + sc2), sh2, g2, zeros, zeros, zeros, zeros], axis=1)
    return ab1, prm


def kernel(x, c, ctx, c_ctx, ada_w, ada_b, norm_w, ffn_w_in, ffn_w_out, ab_w_in, ab_w_out,
           ret_log_gamma, ret_norm_w, mlstm_conv_w, mlstm_conv_b, mlstm_gate_b, mlstm_norm_w,
           attn_w_in, attn_w_out, attn_q_norm_w, attn_k_norm_w, attn_sink):
    nb, seq, d = x.shape
    ctx_len = ctx.shape[1]
    depth = ada_w.shape[0]
    assert ctx_len == ROW_TILE and seq % ROW_TILE == 0 and d == 8 * LANES and nb < 8
    t_all = ctx_len + seq
    nc = t_all // CHUNK
    ctx_chunks = ctx_len // CHUNK
    tpb = t_all // ROW_TILE
    dr = d // 2

    rows = jnp.zeros((8, d), F32).at[:nb].set(c).at[nb].set(c_ctx)
    mod_all = _modulation(rows, ada_w, ada_b)
    cos_t, sin_t = _rope_tables(seq, ctx_len)
    ffn_w = (ffn_w_in, ffn_w_out)
    xs = (ctx.reshape(nb * ctx_len, d), x.reshape(nb * seq, d))

    assert depth == 2 and ab_w_in.shape[0] == 1 and attn_w_in.shape[0] == 1
    ab_0, prm_0 = _mod_tables(mod_all[0], nb, norm_w[0])
    ab_1, prm_1 = _mod_tables(mod_all[1], nb, norm_w[1])

    assert ab_w_in.shape[2] == 8 * dr + 32
    w_in_t = jnp.swapaxes(ab_w_in, 1, 2)[0]
    gb = jnp.zeros((1, LANES), F32).at[0, :32].set(mlstm_gate_b[0].reshape(-1))
    gbt = mlstm_gate_b[0].reshape(32, 1)
    lg = ret_log_gamma[0].astype(F32)
    lgk = jnp.tile(jnp.repeat(lg.reshape(2, 4, 2), 32, axis=-1), (1, 1, 2)).reshape(8, LANES)
    cw = jnp.concatenate([mlstm_conv_w[0], mlstm_conv_b[0][None], jnp.zeros((4, 2 * dr), F32)], axis=0)
    nw = jnp.broadcast_to(jnp.concatenate([ret_norm_w[0], mlstm_norm_w[0]]).reshape(8, LANES, 1),
                          (8, LANES, LANES))
    wo_0 = ab_w_out[0].astype(BF16)

    lane_w = lambda v: jnp.concatenate([v[:32], v[:32], v[32:], v[32:]])
    nwq = jnp.stack([lane_w(attn_q_norm_w[0]) * (HEAD_DIM ** -0.5 * LOG2E), lane_w(attn_k_norm_w[0])]
                    + [jnp.zeros((LANES,), F32)] * 6)
    wo_1 = _attn_o_rows(attn_w_out[0]).astype(BF16)

    y, g, gt, *states, wi_0, w2_0 = _inproj_sweep(*xs, ab_0, w_in_t, gb, gbt, cos_t, sin_t, lgk, cw,
                                                  ffn_w, nb, tpb)
    x_mid, y_attn, wi_1, w2_1 = _mixer_ffn(y, g, gt, states, lg.reshape(-1), lgk, lgk.T, nw, xs, prm_0, wo_0,
                                           wi_0, w2_0, ab_1, attn_w_in, nwq, cos_t, sin_t, ffn_w, nb, tpb)
    out = _attn_ffn(y_attn, attn_sink[0].astype(F32) * LOG2E, x_mid, prm_1, wo_1, wi_1, w2_1, nb, nc, ctx_chunks)
    return out.reshape(nb, seq, d)
```
